```python
import math
import jax
import jax.numpy as jnp
from jax import lax
import numpy as np

D_MODEL = 1024
BATCH = 8
SEQ = 2048
DEPTH = 2

N_MIXERS = 2
N_DIFF_LAYERS = (DEPTH + N_MIXERS - 1) // N_MIXERS
N_RET_LAYERS = DEPTH // N_MIXERS

DIFF_HEADS = 8
DIFF_HEAD_DIM = D_MODEL // DIFF_HEADS // 2
DIFF_V_DIM = 2 * DIFF_HEAD_DIM
Q_BLOCK = 128

NUM_BUCKETS = 32
MAX_EXACT = NUM_BUCKETS // 2
MAX_DISTANCE = 128

RET_HEADS = 4
RET_QK_DIM = D_MODEL
RET_HEAD_DK = RET_QK_DIM // RET_HEADS
RET_V_DIM = 2 * D_MODEL
RET_HEAD_DV = RET_V_DIM // RET_HEADS
RET_CHUNK = 128
RET_IN_DIM = 2 * RET_QK_DIM + 2 * RET_V_DIM

N_GROUPS = 4
EXPERTS_PER_GROUP = 8
N_EXPERTS = N_GROUPS * EXPERTS_PER_GROUP
TOP_K_INNER = 2
EXPERT_FF = D_MODEL // 2
ROW_BLOCK = 128

RMS_EPS = 1e-6
SUBLN_EPS = 1e-5

kernel_name = "hybrid_diffattn_retnet_hmoe"


def rms_norm(x, g, eps=RMS_EPS):
    xf = x.astype(jnp.float32)
    y = xf * lax.rsqrt(jnp.mean(xf * xf, axis=-1, keepdims=True) + eps)
    return (y * g.astype(jnp.float32)).astype(x.dtype)


def t5_bucket(rel):
    n = jnp.maximum(rel, 0)
    is_small = n < MAX_EXACT
    nf = jnp.maximum(n, MAX_EXACT).astype(jnp.float32)
    large = MAX_EXACT + (jnp.log(nf / MAX_EXACT) / math.log(MAX_DISTANCE / MAX_EXACT)
                         * (NUM_BUCKETS - MAX_EXACT)).astype(jnp.int32)
    large = jnp.minimum(large, NUM_BUCKETS - 1)
    return jnp.where(is_small, n, large)


def diff_lambda_init(layer_idx):
    return 0.8 - 0.6 * math.exp(-0.3 * layer_idx)


def differential_attention(h, w_in, lam_qk, subln_g, w_out, rel_table, lambda_init):
    B, S, D = h.shape
    proj = h @ w_in
    q, k, v = jnp.split(proj, [D, 2 * D], axis=-1)
    q = q.reshape(B, S, DIFF_HEADS, 2, DIFF_HEAD_DIM).transpose(3, 0, 2, 1, 4)
    k = k.reshape(B, S, DIFF_HEADS, 2, DIFF_HEAD_DIM).transpose(3, 0, 2, 1, 4)
    v = v.reshape(B, S, DIFF_HEADS, DIFF_V_DIM).transpose(0, 2, 1, 3)
    lq = lam_qk.astype(jnp.float32)
    lam = (jnp.exp(jnp.sum(lq[0] * lq[1])) - jnp.exp(jnp.sum(lq[2] * lq[3]))
           + lambda_init)
    scale = DIFF_HEAD_DIM ** -0.5
    nb = S // Q_BLOCK
    qb = q.reshape(2, B, DIFF_HEADS, nb, Q_BLOCK, DIFF_HEAD_DIM).transpose(3, 0, 1, 2, 4, 5)
    kpos = jnp.arange(S)

    def block(args):
        qblk, bi = args
        qpos = bi * Q_BLOCK + jnp.arange(Q_BLOCK)
        rel = qpos[:, None] - kpos[None, :]
        bias = rel_table[t5_bucket(rel)].transpose(2, 0, 1).astype(jnp.float32)
        s = jnp.einsum('ibhqd,ibhkd->ibhqk', qblk, k).astype(jnp.float32) * scale + bias
        s = jnp.where(rel >= 0, s, -jnp.inf)
        p = jax.nn.softmax(s, axis=-1)
        a = p[0] - lam * p[1]
        return jnp.einsum('bhqk,bhkd->bhqd', a.astype(v.dtype), v)

    o = lax.map(block, (qb, jnp.arange(nb)))
    o = o.transpose(1, 0, 3, 2, 4).reshape(B, S, DIFF_HEADS, DIFF_V_DIM)
    o = rms_norm(o, subln_g, SUBLN_EPS) * (1.0 - lambda_init)
    return o.reshape(B, S, D) @ w_out


def rotate_every_two(x):
    x1 = x[..., ::2]
    x2 = x[..., 1::2]
    return jnp.stack((-x2, x1), axis=-1).reshape(x.shape)


def retention_chunkwise(q, k, v):
    B, H, S, DK = q.shape
    DV = v.shape[-1]
    C = RET_CHUNK
    N = S // C
    log_gamma = jnp.log(1.0 - 2.0 ** (-5.0 - jnp.arange(H, dtype=jnp.float32)))
    pos = jnp.arange(C, dtype=jnp.float32)
    dist = pos[:, None] - pos[None, :]
    inner_decay = jnp.where(dist[None] >= 0,
                            jnp.exp(jnp.maximum(dist, 0.0)[None] * log_gamma[:, None, None]),
                            0.0)
    q_decay = jnp.exp((pos + 1.0)[None] * log_gamma[:, None])
    k_decay = jnp.exp((C - 1.0 - pos)[None] * log_gamma[:, None])
    chunk_decay = jnp.exp(C * log_gamma)
    qc = q.reshape(B, H, N, C, DK)
    kc = k.reshape(B, H, N, C, DK)
    vc = v.reshape(B, H, N, C, DV)
    scores = jnp.einsum('bhncd,bhnmd->bhncm', qc, kc) * inner_decay[None, :, None]
    inner = jnp.einsum('bhncm,bhnmv->bhncv', scores, vc)

    def step(state, xs):
        qn, kn, vn = xs
        cross = jnp.einsum('bhcd,bhdv->bhcv', qn * q_decay[None, :, :, None], state)
        state = (state * chunk_decay[None, :, None, None]
                 + jnp.einsum('bhcd,bhcv->bhdv', kn * k_decay[None, :, :, None], vn))
        return state, cross

    xs = (qc.transpose(2, 0, 1, 3, 4), kc.transpose(2, 0, 1, 3, 4), vc.transpose(2, 0, 1, 3, 4))
    _, cross = lax.scan(step, jnp.zeros((B, H, DK, DV), jnp.float32), xs)
    cross = cross.transpose(1, 2, 0, 3, 4)
    return (inner + cross).reshape(B, H, S, DV)


def retention(h, w_in, w_out):
    B, S, D = h.shape
    proj = h @ w_in
    q, k, v, g = jnp.split(proj, [RET_QK_DIM, 2 * RET_QK_DIM, 2 * RET_QK_DIM + RET_V_DIM], axis=-1)
    q = q.reshape(B, S, RET_HEADS, RET_HEAD_DK).transpose(0, 2, 1, 3).astype(jnp.float32)
    k = k.reshape(B, S, RET_HEADS, RET_HEAD_DK).transpose(0, 2, 1, 3).astype(jnp.float32)
    k = k * (RET_HEAD_DK ** -0.5)
    v = v.reshape(B, S, RET_HEADS, RET_HEAD_DV).transpose(0, 2, 1, 3).astype(jnp.float32)
    angle = 1.0 / (10000.0 ** jnp.linspace(0.0, 1.0, RET_HEAD_DK // 2, dtype=jnp.float32))
    angle = jnp.repeat(angle, 2)
    phase = jnp.arange(S, dtype=jnp.float32)[:, None] * angle[None, :]
    sin, cos = jnp.sin(phase), jnp.cos(phase)
    q = q * cos + rotate_every_two(q) * sin
    k = k * cos + rotate_every_two(k) * sin
    o = retention_chunkwise(q, k, v)
    o = o * lax.rsqrt(jnp.mean(o * o, axis=-1, keepdims=True) + RMS_EPS)
    o = o.transpose(0, 2, 1, 3).reshape(B, S, RET_V_DIM).astype(h.dtype)
    o = jax.nn.silu(g) * o
    return o @ w_out


def hierarchical_moe(h, w_group, b_group, w_expert, b_expert, w_gate, w_up, w_down):
    B, S, D = h.shape
    T = B * S
    ht = h.reshape(T, D)
    tok = jnp.arange(T)
    g_probs = jax.nn.softmax((ht @ w_group + b_group).astype(jnp.float32), axis=-1)
    g_idx = jnp.argmax(g_probs, axis=-1)
    g_p = jnp.max(g_probs, axis=-1)
    e_logits = (ht @ w_expert + b_expert).astype(jnp.float32).reshape(T, N_GROUPS, EXPERTS_PER_GROUP)
    e_sel = e_logits[tok, g_idx]
    top_v, top_i = lax.top_k(e_sel, TOP_K_INNER)
    top_w = jax.nn.softmax(top_v, axis=-1) * g_p[:, None]
    expert_id = (g_idx[:, None] * EXPERTS_PER_GROUP + top_i).reshape(-1)
    weight = top_w.reshape(-1)
    token = jnp.repeat(tok, TOP_K_INNER)
    order = jnp.argsort(expert_id)
    s_e, s_tok, s_w = expert_id[order], token[order], weight[order]
    counts = jnp.zeros((N_EXPERTS,), jnp.int32).at[expert_id].add(1)
    padded = ((counts + ROW_BLOCK - 1) // ROW_BLOCK) * ROW_BLOCK
    pad_end = jnp.cumsum(padded)
    pad_start = pad_end - padded
    start = jnp.cumsum(counts) - counts
    n_assign = T * TOP_K_INNER
    dest = pad_start[s_e] + (jnp.arange(n_assign) - start[s_e])
    n_blocks = (n_assign + ROW_BLOCK - 1) // ROW_BLOCK + N_EXPERTS
    n_rows = n_blocks * ROW_BLOCK
    row_tok = jnp.zeros((n_rows,), jnp.int32).at[dest].set(s_tok.astype(jnp.int32))
    row_w = jnp.zeros((n_rows,), jnp.float32).at[dest].set(s_w)
    block_e = jnp.clip(jnp.searchsorted(pad_end, jnp.arange(n_blocks) * ROW_BLOCK, side='right'),
                       0, N_EXPERTS - 1)
    xr = ht[row_tok].reshape(n_blocks, ROW_BLOCK, D)

    def expert_block(args):
        xb, e = args
        return (jax.nn.silu(xb @ w_gate[e]) * (xb @ w_up[e])) @ w_down[e]

    yr = lax.map(expert_block, (xr, block_e)).reshape(n_rows, D)
    y = jnp.zeros((T, D), h.dtype).at[row_tok].add(yr * row_w[:, None].astype(h.dtype))
    return y.reshape(B, S, D)


def setup_inputs(seed: int = 0) -> dict:
    key = jax.random.key(seed)
    ks = jax.random.split(key, 20)

    def nrm(k, shape, scale):
        return jax.random.normal(k, shape, jnp.float32) * scale

    return {
        "x": nrm(ks[0], (BATCH, SEQ, D_MODEL), 1.0),
        "rel_bias_table": nrm(ks[1], (NUM_BUCKETS, DIFF_HEADS), 0.2),
        "norm_mix": 1.0 + nrm(ks[2], (DEPTH, D_MODEL), 0.01),
        "norm_ffn": 1.0 + nrm(ks[3], (DEPTH, D_MODEL), 0.01),
        "norm_final": 1.0 + nrm(ks[4], (D_MODEL,), 0.01),
        "diff_w_in": nrm(ks[5], (N_DIFF_LAYERS, D_MODEL, 3 * D_MODEL), D_MODEL ** -0.5),
        "diff_lambda_qk": nrm(ks[6], (N_DIFF_LAYERS, 4, DIFF_HEAD_DIM), 0.1),
        "diff_subln": 1.0 + nrm(ks[7], (N_DIFF_LAYERS, DIFF_V_DIM), 0.01),
        "diff_w_out": nrm(ks[8], (N_DIFF_LAYERS, D_MODEL, D_MODEL), D_MODEL ** -0.5),
        "ret_w_in": nrm(ks[9], (N_RET_LAYERS, D_MODEL, RET_IN_DIM), D_MODEL ** -0.5),
        "ret_w_out": nrm(ks[10], (N_RET_LAYERS, RET_V_DIM, D_MODEL), RET_V_DIM ** -0.5),
        "moe_w_group": nrm(ks[11], (DEPTH, D_MODEL, N_GROUPS), D_MODEL ** -0.5),
        "moe_b_group": nrm(ks[12], (DEPTH, N_GROUPS), 0.01),
        "moe_w_expert": nrm(ks[13], (DEPTH, D_MODEL, N_EXPERTS), D_MODEL ** -0.5),
        "moe_b_expert": nrm(ks[14], (DEPTH, N_EXPERTS), 0.01),
        "moe_w_gate": nrm(ks[15], (DEPTH, N_EXPERTS, D_MODEL, EXPERT_FF), D_MODEL ** -0.5),
        "moe_w_up": nrm(ks[16], (DEPTH, N_EXPERTS, D_MODEL, EXPERT_FF), D_MODEL ** -0.5),
        "moe_w_down": nrm(ks[17], (DEPTH, N_EXPERTS, EXPERT_FF, D_MODEL), EXPERT_FF ** -0.5),
    }


def reference(x, rel_bias_table, norm_mix, norm_ffn, norm_final, diff_w_in, diff_lambda_qk,
              diff_subln, diff_w_out, ret_w_in, ret_w_out, moe_w_group, moe_b_group,
              moe_w_expert, moe_b_expert, moe_w_gate, moe_w_up, moe_w_down):
    for i in range(DEPTH):
        hn = rms_norm(x, norm_mix[i])
        j = i // N_MIXERS
        if i % N_MIXERS == 0:
            mix = differential_attention(hn, diff_w_in[j], diff_lambda_qk[j], diff_subln[j],
                                         diff_w_out[j], rel_bias_table, diff_lambda_init(i))
        else:
            mix = retention(hn, ret_w_in[j], ret_w_out[j])
        x = x + mix
        x = x + hierarchical_moe(rms_norm(x, norm_ffn[i]), moe_w_group[i], moe_b_group[i],
                                 moe_w_expert[i], moe_b_expert[i], moe_w_gate[i],
                                 moe_w_up[i], moe_w_down[i])
    return rms_norm(x, norm_final)
```

```python
import functools
import math

import jax
import jax.numpy as jnp
from jax import lax
from jax.experimental import pallas as pl
from jax.experimental.pallas import tpu as pltpu

F32 = jnp.float32
BF16 = jnp.bfloat16

N_MIXERS = 2
NUM_BUCKETS = 32
MAX_DISTANCE = 128
RET_HEADS = 4
TOP_K_INNER = 2
RMS_EPS = 1e-6
SUBLN_EPS = 1e-5

LANES = 128
VMEM_LIMIT = 48 * 1024 * 1024

TM = 512
TN = 1024
TQ = 256
RET_C = 256
ROW_BLOCK = 256
TC = 256


def _cparams(sem):
    return pltpu.CompilerParams(dimension_semantics=sem, vmem_limit_bytes=VMEM_LIMIT)


def _norm_matmul_kernel(x_ref, g_ref, w_ref, o_ref, xn_ref):
    @pl.when(pl.program_id(1) == 0)
    def _():
        x = x_ref[...]
        y = x * lax.rsqrt(jnp.mean(x * x, axis=-1, keepdims=True) + RMS_EPS) * g_ref[...]
        xn_ref[...] = y.astype(BF16)

    o_ref[...] = jnp.dot(xn_ref[...], w_ref[...], preferred_element_type=F32).astype(o_ref.dtype)


def _norm_matmul_rope_kernel(x_ref, g_ref, w_ref, cos_ref, sa_ref, sb_ref, o_ref, xn_ref, *, n_rope):
    j = pl.program_id(1)

    @pl.when(j == 0)
    def _():
        x = x_ref[...]
        y = x * lax.rsqrt(jnp.mean(x * x, axis=-1, keepdims=True) + RMS_EPS) * g_ref[...]
        xn_ref[...] = y.astype(BF16)

    acc = jnp.dot(xn_ref[...], w_ref[...], preferred_element_type=F32)
    tn = acc.shape[1]

    @pl.when(j < n_rope)
    def _():
        reps = tn // cos_ref.shape[1]
        c = jnp.tile(cos_ref[...], (1, reps))
        sa = jnp.tile(sa_ref[...], (1, reps))
        sb = jnp.tile(sb_ref[...], (1, reps))
        nxt = pltpu.roll(acc, tn - 1, axis=1)
        prv = pltpu.roll(acc, 1, axis=1)
        o_ref[...] = (acc * c + nxt * sa + prv * sb).astype(o_ref.dtype)

    @pl.when(j >= n_rope)
    def _():
        o_ref[...] = acc.astype(o_ref.dtype)


def _norm_matmul(x, g, w, rope=None, n_rope=0, seq=None):
    t, d = x.shape
    n = w.shape[1]
    grid = (t // TM, n // TN)
    in_specs = [
        pl.BlockSpec((TM, d), lambda i, j: (i, 0)),
        pl.BlockSpec((1, d), lambda i, j: (0, 0)),
        pl.BlockSpec((d, TN), lambda i, j: (0, j)),
    ]
    args = [x, g.reshape(1, d), w]
    if rope is None:
        body = _norm_matmul_kernel
    else:
        nb = seq // TM
        dk = rope[0].shape[1]
        in_specs += [pl.BlockSpec((TM, dk), lambda i, j: (i % nb, 0))] * 3
        args += list(rope)
        body = functools.partial(_norm_matmul_rope_kernel, n_rope=n_rope)
    return pl.pallas_call(
        body,
        grid=grid,
        in_specs=in_specs,
        out_specs=pl.BlockSpec((TM, TN), lambda i, j: (i, j)),
        out_shape=jax.ShapeDtypeStruct((t, n), BF16),
        scratch_shapes=[pltpu.VMEM((TM, d), BF16)],
        compiler_params=_cparams(("parallel", "arbitrary")),
    )(*args)


def _diff_attn_kernel(far_ref, q_ref, k_ref, v_ref, bias_ref, lam_ref, g_ref, o_ref,
                      m_ref, l_ref, acc_ref, *, lambda_init):
    h = pl.program_id(1)
    qi = pl.program_id(2)
    tq, dv = q_ref.shape
    dh = dv // 2

    q = q_ref[...]
    lane = lax.broadcasted_iota(jnp.int32, q.shape, 1)
    zero = jnp.zeros_like(q)
    q_maps = (jnp.where(lane < dh, q, zero), jnp.where(lane >= dh, q, zero))

    m_ref[...] = jnp.full(m_ref.shape, -jnp.inf, F32)
    l_ref[...] = jnp.zeros(l_ref.shape, F32)
    acc_ref[...] = jnp.zeros(acc_ref.shape, F32)

    def step(j, bias):
        rows = pl.ds(pl.multiple_of(j * tq, tq), tq)
        k = k_ref[rows, :]
        v = v_ref[rows, :]
        for i in range(2):
            s = lax.dot_general(q_maps[i], k, (((1,), (1,)), ((), ())),
                                preferred_element_type=F32) + bias
            m_old = m_ref[i]
            m_new = jnp.maximum(m_old, jnp.max(s, axis=-1, keepdims=True))
            alpha = jnp.exp(m_old - m_new)
            p = jnp.exp(s - m_new)
            l_ref[i] = alpha * l_ref[i] + jnp.sum(p, axis=-1, keepdims=True)
            acc_ref[i] = alpha * acc_ref[i] + jnp.dot(p.astype(BF16), v, preferred_element_type=F32)
            m_ref[i] = m_new

    far = far_ref[h]

    def far_body(j, carry):
        step(j, far)
        return carry

    lax.fori_loop(0, jnp.maximum(qi - 1, 0), far_body, 0)

    @pl.when(qi >= 1)
    def _():
        step(qi - 1, bias_ref[0, 1])

    step(qi, bias_ref[0, 0])

    lq = lam_ref[...]
    lam = (jnp.exp(jnp.sum(lq[0:1] * lq[1:2], axis=-1, keepdims=True))
           - jnp.exp(jnp.sum(lq[2:3] * lq[3:4], axis=-1, keepdims=True)) + lambda_init)
    o = acc_ref[0] / l_ref[0] - lam * (acc_ref[1] / l_ref[1])
    o = o * lax.rsqrt(jnp.mean(o * o, axis=-1, keepdims=True) + SUBLN_EPS) * g_ref[...]
    o_ref[...] = (o * (1.0 - lambda_init)).astype(o_ref.dtype)


def _t5_bucket(rel):
    max_exact = NUM_BUCKETS // 2
    n = jnp.maximum(rel, 0)
    is_small = n < max_exact
    nf = jnp.maximum(n, max_exact).astype(F32)
    large = max_exact + (jnp.log(nf / max_exact) / math.log(MAX_DISTANCE / max_exact)
                         * (NUM_BUCKETS - max_exact)).astype(jnp.int32)
    large = jnp.minimum(large, NUM_BUCKETS - 1)
    return jnp.where(is_small, n, large)


def _bias_tiles(rel_table):
    qi = jnp.arange(TQ)[:, None]
    kj = jnp.arange(TQ)[None, :]
    tiles = []
    for d in range(2):
        rel = d * TQ + qi - kj
        b = rel_table[_t5_bucket(rel)].astype(F32)
        b = jnp.where((rel >= 0)[:, :, None], b, -jnp.inf)
        tiles.append(b.transpose(2, 0, 1))
    return jnp.stack(tiles, axis=1), rel_table[NUM_BUCKETS - 1].astype(F32)


def _diff_attention(qkv, rel_table, lam_qk, subln_g, lambda_init, batch, seq, heads):
    t = qkv.shape[0]
    d = qkv.shape[1] // 3
    dv = d // heads
    assert TQ + 1 >= MAX_DISTANCE and seq % TQ == 0 and dv == LANES
    nq = seq // TQ
    bias, far = _bias_tiles(rel_table)
    grid_spec = pltpu.PrefetchScalarGridSpec(
        num_scalar_prefetch=1,
        grid=(batch, heads, nq),
        in_specs=[
            pl.BlockSpec((TQ, dv), lambda b, h, i, far: (b * nq + i, h)),
            pl.BlockSpec((seq, dv), lambda b, h, i, far: (b, heads + h)),
            pl.BlockSpec((seq, dv), lambda b, h, i, far: (b, 2 * heads + h)),
            pl.BlockSpec((1, 2, TQ, TQ), lambda b, h, i, far: (h, 0, 0, 0)),
            pl.BlockSpec(lam_qk.shape, lambda b, h, i, far: (0, 0)),
            pl.BlockSpec((1, dv), lambda b, h, i, far: (0, 0)),
        ],
        out_specs=pl.BlockSpec((TQ, dv), lambda b, h, i, far: (b * nq + i, h)),
        scratch_shapes=[
            pltpu.VMEM((2, TQ, 1), F32),
            pltpu.VMEM((2, TQ, 1), F32),
            pltpu.VMEM((2, TQ, dv), F32),
        ],
    )
    return pl.pallas_call(
        functools.partial(_diff_attn_kernel, lambda_init=lambda_init),
        grid_spec=grid_spec,
        out_shape=jax.ShapeDtypeStruct((t, d), BF16),
        compiler_params=_cparams(("parallel", "parallel", "arbitrary")),
    )(far, qkv, qkv, qkv, bias, lam_qk.astype(F32), subln_g.reshape(1, dv).astype(F32))


def _retention_kernel(q_ref, k_ref, v_ref, dec_ref, tab_ref, o_ref, state_ref):
    seq = q_ref.shape[0]
    c = dec_ref.shape[1]
    dec = dec_ref[0]
    tab = tab_ref[0]
    qd = tab[:, 0:1]
    kd = tab[:, 1:2]
    cd = tab[0:1, 2:3]
    state_ref[...] = jnp.zeros(state_ref.shape, F32)

    def body(n, carry):
        rows = pl.ds(pl.multiple_of(n * c, c), c)
        q = q_ref[rows, :]
        k = k_ref[rows, :]
        v = v_ref[rows, :]
        s = lax.dot_general(q, k, (((1,), (1,)), ((), ())), preferred_element_type=F32) * dec
        inner = jnp.dot(s.astype(BF16), v, preferred_element_type=F32)
        st = state_ref[...]
        cross = jnp.dot(q, st.astype(BF16), preferred_element_type=F32) * qd
        kt = (k.astype(F32) * kd).T.astype(BF16)
        state_ref[...] = st * cd + jnp.dot(kt, v, preferred_element_type=F32)
        o = inner + cross
        o = o * lax.rsqrt(jnp.mean(o * o, axis=-1, keepdims=True) + RMS_EPS)
        o_ref[rows, :] = o.astype(o_ref.dtype)
        return carry

    lax.fori_loop(0, seq // c, body, 0)


def _retention_tables(heads):
    c = RET_C
    log_gamma = jnp.log(1.0 - 2.0 ** (-5.0 - jnp.arange(heads, dtype=F32)))
    pos = jnp.arange(c, dtype=F32)
    dist = pos[:, None] - pos[None, :]
    inner_decay = jnp.where(dist[None] >= 0,
                            jnp.exp(jnp.maximum(dist, 0.0)[None] * log_gamma[:, None, None]), 0.0)
    q_decay = jnp.exp((pos + 1.0)[None] * log_gamma[:, None])
    k_decay = jnp.exp((c - 1.0 - pos)[None] * log_gamma[:, None])
    chunk_decay = jnp.broadcast_to(jnp.exp(c * log_gamma)[:, None], (heads, c))
    tab = jnp.stack([q_decay, k_decay, chunk_decay], axis=-1)
    tab = jnp.pad(tab, ((0, 0), (0, 0), (0, LANES - 3)))
    return inner_decay, tab


def _retention(qkvg, batch, seq, qk_dim, v_dim):
    t = qkvg.shape[0]
    dk = qk_dim // RET_HEADS
    dvh = v_dim // RET_HEADS
    inner_decay, tab = _retention_tables(RET_HEADS)
    kb = qk_dim // dk
    vb = 2 * qk_dim // dvh
    return pl.pallas_call(
        _retention_kernel,
        grid=(batch, RET_HEADS),
        in_specs=[
            pl.BlockSpec((seq, dk), lambda b, h: (b, h)),
            pl.BlockSpec((seq, dk), lambda b, h: (b, kb + h)),
            pl.BlockSpec((seq, dvh), lambda b, h: (b, vb + h)),
            pl.BlockSpec((1, RET_C, RET_C), lambda b, h: (h, 0, 0)),
            pl.BlockSpec((1, RET_C, LANES), lambda b, h: (h, 0, 0)),
        ],
        out_specs=pl.BlockSpec((seq, dvh), lambda b, h: (b, h)),
        out_shape=jax.ShapeDtypeStruct((t, v_dim), BF16),
        scratch_shapes=[pltpu.VMEM((dk, dvh), F32)],
        compiler_params=_cparams(("parallel", "arbitrary")),
    )(qkvg, qkvg, qkvg, inner_decay, tab)


def _out_router_kernel(*refs, gated, n_groups, n_experts):
    if gated:
        a_ref, gate_ref, w_ref, x_ref, nf_ref, wr_ref, br_ref = refs[:7]
        x1_ref, hn_ref, route_ref, cnt_ref, carry_ref = refs[7:]
        gate = gate_ref[...].astype(F32)
        a = (gate * jax.nn.sigmoid(gate) * a_ref[...].astype(F32)).astype(BF16)
    else:
        a_ref, w_ref, x_ref, nf_ref, wr_ref, br_ref = refs[:6]
        x1_ref, hn_ref, route_ref, cnt_ref, carry_ref = refs[6:]
        a = a_ref[...]
    epg = n_experts // n_groups

    @pl.when(pl.program_id(0) == 0)
    def _():
        carry_ref[...] = jnp.zeros(carry_ref.shape, F32)

    x1 = x_ref[...] + jnp.dot(a, w_ref[...], preferred_element_type=F32)
    x1_ref[...] = x1
    hn = x1 * lax.rsqrt(jnp.mean(x1 * x1, axis=-1, keepdims=True) + RMS_EPS) * nf_ref[...]
    hn_ref[...] = hn

    logits = jnp.dot(hn, wr_ref[...], preferred_element_type=F32,
                     precision=lax.Precision.HIGHEST) + br_ref[...]
    tm = logits.shape[0]
    lane = lax.broadcasted_iota(jnp.int32, logits.shape, 1)
    big = jnp.int32(LANES)
    neg = jnp.float32(-jnp.inf)

    gl = jnp.where(lane < n_groups, logits, neg)
    gmax = jnp.max(gl, axis=-1, keepdims=True)
    g_p = 1.0 / jnp.sum(jnp.exp(gl - gmax), axis=-1, keepdims=True)
    g_idx = jnp.min(jnp.where(gl == gmax, lane, big), axis=-1, keepdims=True)

    lo = n_groups + g_idx * epg
    el = jnp.where((lane >= lo) & (lane < lo + epg), logits, neg)
    v1 = jnp.max(el, axis=-1, keepdims=True)
    i1 = jnp.min(jnp.where(el == v1, lane, big), axis=-1, keepdims=True)
    el2 = jnp.where(lane == i1, neg, el)
    v2 = jnp.max(el2, axis=-1, keepdims=True)
    i2 = jnp.min(jnp.where(el2 == v2, lane, big), axis=-1, keepdims=True)
    tt = jnp.exp(v2 - v1)
    w1 = g_p / (1.0 + tt)
    w2 = g_p * tt / (1.0 + tt)
    e1 = i1 - n_groups
    e2 = i2 - n_groups

    oh1 = lane == e1
    oh2 = lane == e2
    onehot = jnp.where(oh1 | oh2, 1.0, 0.0).astype(BF16)
    r = lax.broadcasted_iota(jnp.int32, (tm, tm), 0)
    cidx = lax.broadcasted_iota(jnp.int32, (tm, tm), 1)
    tri = jnp.where(cidx < r, 1.0, 0.0).astype(BF16)
    base = jnp.dot(tri, onehot, preferred_element_type=F32) + carry_ref[...]
    rank1 = jnp.sum(jnp.where(oh1, base, 0.0), axis=-1, keepdims=True)
    rank2 = jnp.sum(jnp.where(oh2, base, 0.0), axis=-1, keepdims=True)
    carry = carry_ref[...] + jnp.sum(onehot.astype(F32), axis=0, keepdims=True)
    carry_ref[...] = carry
    cnt_ref[...] = carry

    out = jnp.zeros(logits.shape, F32)
    for idx, val in enumerate((e1.astype(F32), e2.astype(F32), w1, w2, rank1, rank2)):
        out = jnp.where(lane == idx, val, out)
    route_ref[...] = out


def _out_router(a, gate, w, x, nf, w_group, b_group, w_expert, b_expert):
    t, d = x.shape
    k = w.shape[0]
    n_groups = w_group.shape[1]
    n_experts = w_expert.shape[1]
    assert n_groups + n_experts <= LANES
    pad = LANES - n_groups - n_experts
    wr = jnp.pad(jnp.concatenate([w_group, w_expert], axis=1).astype(F32), ((0, 0), (0, pad)))
    br = jnp.pad(jnp.concatenate([b_group, b_expert]).astype(F32), (0, pad)).reshape(1, LANES)
    gated = gate is not None
    row = lambda i: (i, 0)
    fixed = lambda i: (0, 0)
    in_specs = [pl.BlockSpec((TM, k), row)]
    args = [a]
    if gated:
        gate_arr, gate_col = gate
        in_specs.append(pl.BlockSpec((TM, k), lambda i: (i, gate_col)))
        args.append(gate_arr)
    in_specs += [
        pl.BlockSpec((k, d), fixed),
        pl.BlockSpec((TM, d), row),
        pl.BlockSpec((1, d), fixed),
        pl.BlockSpec((d, LANES), fixed),
        pl.BlockSpec((1, LANES), fixed),
    ]
    args += [w, x, nf.reshape(1, d), wr, br]
    return pl.pallas_call(
        functools.partial(_out_router_kernel, gated=gated, n_groups=n_groups, n_experts=n_experts),
        grid=(t // TM,),
        in_specs=in_specs,
        out_specs=[
            pl.BlockSpec((TM, d), row),
            pl.BlockSpec((TM, d), row),
            pl.BlockSpec((TM, LANES), row),
            pl.BlockSpec((1, LANES), fixed),
        ],
        out_shape=[
            jax.ShapeDtypeStruct((t, d), F32),
            jax.ShapeDtypeStruct((t, d), F32),
            jax.ShapeDtypeStruct((t, LANES), F32),
            jax.ShapeDtypeStruct((1, LANES), F32),
        ],
        scratch_shapes=[pltpu.VMEM((1, LANES), F32)],
        compiler_params=_cparams(("arbitrary",)),
    )(*args)


def _row_copy(src, dst, sem):
    return pltpu.make_async_copy(src, dst, sem)


def _dispatch_kernel(d1_ref, d2_ref, hn_ref, xs_in_ref, xs_ref, sem):
    del xs_in_ref
    tm = hn_ref.shape[0]
    base = pl.program_id(0) * tm

    def copies(r):
        src = hn_ref.at[pl.ds(r, 1), :]
        return (_row_copy(src, xs_ref.at[pl.ds(d1_ref[base + r], 1), :], sem.at[0]),
                _row_copy(src, xs_ref.at[pl.ds(d2_ref[base + r], 1), :], sem.at[1]))

    def issue(r, carry):
        for cp in copies(r):
            cp.start()
        return carry

    def wait(r, carry):
        for cp in copies(r):
            cp.wait()
        return carry

    lax.fori_loop(0, tm, issue, 0)
    lax.fori_loop(0, tm, wait, 0)


def _dispatch(hn, d1, d2, n_rows):
    t, d = hn.shape
    grid_spec = pltpu.PrefetchScalarGridSpec(
        num_scalar_prefetch=2,
        grid=(t // TC,),
        in_specs=[
            pl.BlockSpec((TC, d), lambda i, d1, d2: (i, 0)),
            pl.BlockSpec(memory_space=pl.ANY),
        ],
        out_specs=pl.BlockSpec(memory_space=pl.ANY),
        scratch_shapes=[pltpu.SemaphoreType.DMA((2,))],
    )
    return pl.pallas_call(
        _dispatch_kernel,
        grid_spec=grid_spec,
        out_shape=jax.ShapeDtypeStruct((n_rows, d), hn.dtype),
        input_output_aliases={3: 0},
        compiler_params=_cparams(("arbitrary",)),
    )(d1, d2, hn, jnp.zeros((n_rows, d), hn.dtype))


def _expert_kernel(be_ref, nu_ref, x_ref, wg_ref, wu_ref, wd_ref, y_ref):
    used = pl.program_id(0) < nu_ref[0]

    @pl.when(used)
    def _():
        x = x_ref[...].astype(BF16)
        g = jnp.dot(x, wg_ref[0], preferred_element_type=F32)
        u = jnp.dot(x, wu_ref[0], preferred_element_type=F32)
        hmid = (g * jax.nn.sigmoid(g) * u).astype(BF16)
        y_ref[...] = jnp.dot(hmid, wd_ref[0], preferred_element_type=F32)

    @pl.when(jnp.logical_not(used))
    def _():
        y_ref[...] = jnp.zeros(y_ref.shape, y_ref.dtype)


def _experts(xs, block_e, n_used, w_gate, w_up, w_down):
    n_rows, d = xs.shape
    ff = w_gate.shape[2]
    nb = n_rows // ROW_BLOCK
    blk = lambda i, be, nu: (jnp.minimum(i, nu[0] - 1), 0)
    wsel = lambda i, be, nu: (be[jnp.minimum(i, nu[0] - 1)], 0, 0)
    grid_spec = pltpu.PrefetchScalarGridSpec(
        num_scalar_prefetch=2,
        grid=(nb,),
        in_specs=[
            pl.BlockSpec((ROW_BLOCK, d), blk),
            pl.BlockSpec((1, d, ff), wsel),
            pl.BlockSpec((1, d, ff), wsel),
            pl.BlockSpec((1, ff, d), wsel),
        ],
        out_specs=pl.BlockSpec((ROW_BLOCK, d), lambda i, be, nu: (i, 0)),
    )
    return pl.pallas_call(
        _expert_kernel,
        grid_spec=grid_spec,
        out_shape=jax.ShapeDtypeStruct((n_rows, d), F32),
        compiler_params=_cparams(("arbitrary",)),
    )(block_e, n_used, xs, w_gate, w_up, w_down)


def _combine_kernel(d1_ref, d2_ref, x_ref, route_ref, nf_ref, ys_ref, o_ref, buf_ref, sem, *, final_norm):
    tm = x_ref.shape[0]
    base = pl.program_id(0) * tm

    def copies(r):
        return (_row_copy(ys_ref.at[pl.ds(d1_ref[base + r], 1), :], buf_ref.at[0, pl.ds(r, 1), :], sem.at[0]),
                _row_copy(ys_ref.at[pl.ds(d2_ref[base + r], 1), :], buf_ref.at[1, pl.ds(r, 1), :], sem.at[1]))

    def issue(r, carry):
        for cp in copies(r):
            cp.start()
        return carry

    def wait(r, carry):
        for cp in copies(r):
            cp.wait()
        return carry

    lax.fori_loop(0, tm, issue, 0)
    lax.fori_loop(0, tm, wait, 0)

    route = route_ref[...]
    y = x_ref[...] + (route[:, 2:3] * buf_ref[0] + route[:, 3:4] * buf_ref[1])
    if final_norm:
        y = y * lax.rsqrt(jnp.mean(y * y, axis=-1, keepdims=True) + RMS_EPS) * nf_ref[...]
    o_ref[...] = y


def _combine(x1, route, ys, d1, d2, nf, final_norm):
    t, d = x1.shape
    grid_spec = pltpu.PrefetchScalarGridSpec(
        num_scalar_prefetch=2,
        grid=(t // TC,),
        in_specs=[
            pl.BlockSpec((TC, d), lambda i, d1, d2: (i, 0)),
            pl.BlockSpec((TC, LANES), lambda i, d1, d2: (i, 0)),
            pl.BlockSpec((1, d), lambda i, d1, d2: (0, 0)),
            pl.BlockSpec(memory_space=pl.ANY),
        ],
        out_specs=pl.BlockSpec((TC, d), lambda i, d1, d2: (i, 0)),
        scratch_shapes=[pltpu.VMEM((2, TC, d), F32), pltpu.SemaphoreType.DMA((2,))],
    )
    return pl.pallas_call(
        functools.partial(_combine_kernel, final_norm=final_norm),
        grid_spec=grid_spec,
        out_shape=jax.ShapeDtypeStruct((t, d), F32),
        compiler_params=_cparams(("arbitrary",)),
    )(d1, d2, x1, route, nf.reshape(1, d).astype(F32), ys)


def _moe(x1, hn, route, cnt, w_gate, w_up, w_down, nf, final_norm):
    t, d = x1.shape
    n_experts = w_gate.shape[0]
    e1 = route[:, 0].astype(jnp.int32)
    e2 = route[:, 1].astype(jnp.int32)
    rank1 = route[:, 4].astype(jnp.int32)
    rank2 = route[:, 5].astype(jnp.int32)
    counts = cnt[0, :n_experts].astype(jnp.int32)
    padded = ((counts + ROW_BLOCK - 1) // ROW_BLOCK) * ROW_BLOCK
    pad_end = jnp.cumsum(padded)
    pad_start = pad_end - padded
    d1 = pad_start[e1] + rank1
    d2 = pad_start[e2] + rank2
    nb = (t * TOP_K_INNER) // ROW_BLOCK + n_experts
    n_used = (pad_end[-1:] // ROW_BLOCK).astype(jnp.int32)
    block_e = jnp.clip(jnp.searchsorted(pad_end, jnp.arange(nb) * ROW_BLOCK, side='right'),
                       0, n_experts - 1).astype(jnp.int32)
    xs = _dispatch(hn, d1, d2, nb * ROW_BLOCK)
    ys = _experts(xs, block_e, n_used, w_gate, w_up, w_down)
    return _combine(x1, route, ys, d1, d2, nf, final_norm)


def _diff_lambda_init(layer_idx):
    return 0.8 - 0.6 * math.exp(-0.3 * layer_idx)


def _rope_tables(seq, dk):
    angle = 1.0 / (10000.0 ** jnp.linspace(0.0, 1.0, dk // 2, dtype=F32))
    angle = jnp.repeat(angle, 2)
    phase = jnp.arange(seq, dtype=F32)[:, None] * angle[None, :]
    sin, cos = jnp.sin(phase), jnp.cos(phase)
    even = (jnp.arange(dk) % 2 == 0)[None, :]
    return cos, jnp.where(even, -sin, 0.0), jnp.where(even, 0.0, sin)


def kernel(x, rel_bias_table, norm_mix, norm_ffn, norm_final, diff_w_in, diff_lambda_qk, diff_subln,
           diff_w_out, ret_w_in, ret_w_out, moe_w_group, moe_b_group, moe_w_expert, moe_b_expert,
           moe_w_gate, moe_w_up, moe_w_down):
    batch, seq, d = x.shape
    depth = norm_mix.shape[0]
    t = batch * seq
    heads = rel_bias_table.shape[1]
    dh = diff_lambda_qk.shape[-1]
    qk_dim = (ret_w_in.shape[2] - 2 * ret_w_out.shape[1]) // 2
    v_dim = ret_w_out.shape[1]
    dk = qk_dim // RET_HEADS
    xt = x.reshape(t, d)
    out = None
    for i in range(depth):
        j = i // N_MIXERS
        if i % N_MIXERS == 0:
            col_scale = jnp.concatenate([jnp.full((d,), dh ** -0.5, F32), jnp.ones((2 * d,), F32)])
            w_in = (diff_w_in[j] * col_scale[None, :]).astype(BF16)
            qkv = _norm_matmul(xt, norm_mix[i], w_in)
            mix = _diff_attention(qkv, rel_bias_table, diff_lambda_qk[j], diff_subln[j],
                                  _diff_lambda_init(i), batch, seq, heads)
            gate = None
            w_out = diff_w_out[j].astype(BF16)
        else:
            n_in = ret_w_in.shape[2]
            col_scale = jnp.concatenate([jnp.ones((qk_dim,), F32), jnp.full((qk_dim,), dk ** -0.5, F32),
                                         jnp.ones((n_in - 2 * qk_dim,), F32)])
            w_in = (ret_w_in[j] * col_scale[None, :]).astype(BF16)
            qkvg = _norm_matmul(xt, norm_mix[i], w_in, rope=_rope_tables(seq, dk),
                                n_rope=2 * qk_dim // TN, seq=seq)
            mix = _retention(qkvg, batch, seq, qk_dim, v_dim)
            gate = (qkvg, (2 * qk_dim + v_dim) // v_dim)
            w_out = ret_w_out[j].astype(BF16)
        x1, hn, route, cnt = _out_router(mix, gate, w_out, xt, norm_ffn[i], moe_w_group[i], moe_b_group[i],
                                         moe_w_expert[i], moe_b_expert[i])
        final = i == depth - 1
        xt = _moe(x1, hn, route, cnt, moe_w_gate[i].astype(BF16), moe_w_up[i].astype(BF16),
                  moe_w_down[i].astype(BF16), norm_final, final)
    return xt.reshape(batch, seq, d)
```

```python
import functools
import math

import jax
import jax.numpy as jnp
from jax import lax
from jax.experimental import pallas as pl
from jax.experimental.pallas import tpu as pltpu

F32 = jnp.float32
BF16 = jnp.bfloat16

N_MIXERS = 2
NUM_BUCKETS = 32
MAX_DISTANCE = 128
RET_HEADS = 4
TOP_K_INNER = 2
RMS_EPS = 1e-6
SUBLN_EPS = 1e-5

LANES = 128
SUBLANES = 8
VMEM_LIMIT = 48 * 1024 * 1024

TM = 512
TN = 1024
TQ = 256
RET_C = 256
ROW_BLOCK = 256
TC = 256
DMA_UNROLL = 8


def _cparams(sem):
    return pltpu.CompilerParams(dimension_semantics=sem, vmem_limit_bytes=VMEM_LIMIT)


def _norm_matmul_kernel(x_ref, g_ref, w_ref, o_ref, xn_ref):
    @pl.when(pl.program_id(1) == 0)
    def _():
        x = x_ref[...]
        y = x * lax.rsqrt(jnp.mean(x * x, axis=-1, keepdims=True) + RMS_EPS) * g_ref[...]
        xn_ref[...] = y.astype(BF16)

    o_ref[...] = jnp.dot(xn_ref[...], w_ref[...], preferred_element_type=F32).astype(o_ref.dtype)


def _norm_matmul_rope_kernel(x_ref, g_ref, w_ref, cos_ref, sa_ref, sb_ref, o_ref, xn_ref, *, n_rope):
    j = pl.program_id(1)

    @pl.when(j == 0)
    def _():
        x = x_ref[...]
        y = x * lax.rsqrt(jnp.mean(x * x, axis=-1, keepdims=True) + RMS_EPS) * g_ref[...]
        xn_ref[...] = y.astype(BF16)

    acc = jnp.dot(xn_ref[...], w_ref[...], preferred_element_type=F32)
    tn = acc.shape[1]

    @pl.when(j < n_rope)
    def _():
        reps = tn // cos_ref.shape[1]
        c = jnp.tile(cos_ref[...], (1, reps))
        sa = jnp.tile(sa_ref[...], (1, reps))
        sb = jnp.tile(sb_ref[...], (1, reps))
        nxt = pltpu.roll(acc, tn - 1, axis=1)
        prv = pltpu.roll(acc, 1, axis=1)
        o_ref[...] = (acc * c + nxt * sa + prv * sb).astype(o_ref.dtype)

    @pl.when(j >= n_rope)
    def _():
        o_ref[...] = acc.astype(o_ref.dtype)


def _norm_matmul(x, g, w, rope=None, n_rope=0, seq=None):
    t, d = x.shape
    n = w.shape[1]
    grid = (t // TM, n // TN)
    in_specs = [
        pl.BlockSpec((TM, d), lambda i, j: (i, 0)),
        pl.BlockSpec((1, d), lambda i, j: (0, 0)),
        pl.BlockSpec((d, TN), lambda i, j: (0, j)),
    ]
    args = [x, g.reshape(1, d), w]
    if rope is None:
        body = _norm_matmul_kernel
    else:
        nb = seq // TM
        dk = rope[0].shape[1]
        in_specs += [pl.BlockSpec((TM, dk), lambda i, j: (i % nb, 0))] * 3
        args += list(rope)
        body = functools.partial(_norm_matmul_rope_kernel, n_rope=n_rope)
    return pl.pallas_call(
        body,
        grid=grid,
        in_specs=in_specs,
        out_specs=pl.BlockSpec((TM, TN), lambda i, j: (i, j)),
        out_shape=jax.ShapeDtypeStruct((t, n), BF16),
        scratch_shapes=[pltpu.VMEM((TM, d), BF16)],
        compiler_params=_cparams(("parallel", "arbitrary")),
    )(*args)


def _diff_attn_kernel(q_ref, k_ref, v_ref, bias_ref, lam_ref, g_ref, o_ref,
                      vt_ref, m_ref, l_ref, acc_ref, *, lambda_init):
    seq, dv = q_ref.shape
    tq = bias_ref.shape[-1]
    dh = dv // 2
    nq = seq // tq

    for j in range(nq):
        vt_ref[j] = v_ref[j * tq:(j + 1) * tq, :].astype(F32).T.astype(BF16)

    lq = lam_ref[...]
    lam = (jnp.exp(jnp.sum(lq[0:1] * lq[1:2], axis=-1, keepdims=True))
           - jnp.exp(jnp.sum(lq[2:3] * lq[3:4], axis=-1, keepdims=True)) + lambda_init)
    lane = lax.broadcasted_iota(jnp.int32, (tq, dv), 1)

    def q_body(qi, carry):
        qrows = pl.ds(pl.multiple_of(qi * tq, tq), tq)
        q = q_ref[qrows, :]
        zero = jnp.zeros_like(q)
        qs = jnp.concatenate([jnp.where(lane < dh, q, zero), jnp.where(lane >= dh, q, zero)], axis=0)

        m_ref[...] = jnp.full(m_ref.shape, -jnp.inf, F32)
        l_ref[...] = jnp.zeros(l_ref.shape, F32)
        acc_ref[...] = jnp.zeros(acc_ref.shape, F32)

        def step(j, bias):
            rows = pl.ds(pl.multiple_of(j * tq, tq), tq)
            s = lax.dot_general(k_ref[rows, :], qs, (((1,), (1,)), ((), ())),
                                preferred_element_type=F32)
            if bias is not None:
                s = s + jnp.concatenate([bias, bias], axis=1)
            m_old = m_ref[...]
            m_new = jnp.maximum(m_old, jnp.max(s, axis=0, keepdims=True))
            alpha = jnp.exp(m_old - m_new)
            p = jnp.exp(s - m_new)
            l_ref[...] = alpha * l_ref[...] + jnp.sum(p, axis=0, keepdims=True)
            acc_ref[...] = alpha * acc_ref[...] + jnp.dot(vt_ref[j], p.astype(BF16),
                                                          preferred_element_type=F32)
            m_ref[...] = m_new

        def far_body(j, c):
            step(j, None)
            return c

        lax.fori_loop(0, jnp.maximum(qi - 1, 0), far_body, 0)

        @pl.when(qi >= 1)
        def _():
            step(qi - 1, bias_ref[0, 1])

        step(qi, bias_ref[0, 0])

        on = acc_ref[...] / l_ref[...]
        ot = on[:, :tq] - lam * on[:, tq:]
        ot = ot * lax.rsqrt(jnp.mean(ot * ot, axis=0, keepdims=True) + SUBLN_EPS)
        o_ref[qrows, :] = (ot.T * g_ref[...] * (1.0 - lambda_init)).astype(o_ref.dtype)
        return carry

    lax.fori_loop(0, nq, q_body, 0)


def _t5_bucket(rel):
    max_exact = NUM_BUCKETS // 2
    n = jnp.maximum(rel, 0)
    is_small = n < max_exact
    nf = jnp.maximum(n, max_exact).astype(F32)
    large = max_exact + (jnp.log(nf / max_exact) / math.log(MAX_DISTANCE / max_exact)
                         * (NUM_BUCKETS - max_exact)).astype(jnp.int32)
    large = jnp.minimum(large, NUM_BUCKETS - 1)
    return jnp.where(is_small, n, large)


def _bias_tiles(rel_table):
    kj = jnp.arange(TQ)[:, None]
    qi = jnp.arange(TQ)[None, :]
    far = rel_table[NUM_BUCKETS - 1].astype(F32)
    tiles = []
    for d in range(2):
        rel = d * TQ + qi - kj
        b = rel_table[_t5_bucket(rel)].astype(F32) - far[None, None, :]
        b = jnp.where((rel >= 0)[:, :, None], b, -jnp.inf)
        tiles.append(b.transpose(2, 0, 1))
    return jnp.stack(tiles, axis=1)


def _diff_attention(qkv, rel_table, lam_qk, subln_g, lambda_init, batch, seq, heads):
    t = qkv.shape[0]
    d = qkv.shape[1] // 3
    dv = d // heads
    assert TQ + 1 >= MAX_DISTANCE and seq % TQ == 0 and dv == LANES
    nq = seq // TQ
    bias = _bias_tiles(rel_table)
    return pl.pallas_call(
        functools.partial(_diff_attn_kernel, lambda_init=lambda_init),
        grid=(batch, heads),
        in_specs=[
            pl.BlockSpec((seq, dv), lambda b, h: (b, h)),
            pl.BlockSpec((seq, dv), lambda b, h: (b, heads + h)),
            pl.BlockSpec((seq, dv), lambda b, h: (b, 2 * heads + h)),
            pl.BlockSpec((1, 2, TQ, TQ), lambda b, h: (h, 0, 0, 0)),
            pl.BlockSpec(lam_qk.shape, lambda b, h: (0, 0)),
            pl.BlockSpec((1, dv), lambda b, h: (0, 0)),
        ],
        out_specs=pl.BlockSpec((seq, dv), lambda b, h: (b, h)),
        out_shape=jax.ShapeDtypeStruct((t, d), BF16),
        scratch_shapes=[
            pltpu.VMEM((nq, dv, TQ), BF16),
            pltpu.VMEM((1, 2 * TQ), F32),
            pltpu.VMEM((1, 2 * TQ), F32),
            pltpu.VMEM((dv, 2 * TQ), F32),
        ],
        compiler_params=_cparams(("parallel", "parallel")),
    )(qkv, qkv, qkv, bias, lam_qk.astype(F32), subln_g.reshape(1, dv).astype(F32))


def _retention_kernel(q_ref, k_ref, v_ref, dec_ref, tab_ref, o_ref, state_ref):
    seq = q_ref.shape[0]
    c = dec_ref.shape[1]
    dec = dec_ref[0]
    tab = tab_ref[0]
    qd = tab[:, 0:1]
    kd = tab[:, 1:2]
    cd = tab[0:1, 2:3]
    state_ref[...] = jnp.zeros(state_ref.shape, F32)

    def body(n, carry):
        rows = pl.ds(pl.multiple_of(n * c, c), c)
        q = q_ref[rows, :]
        k = k_ref[rows, :]
        v = v_ref[rows, :]
        s = lax.dot_general(q, k, (((1,), (1,)), ((), ())), preferred_element_type=F32) * dec
        inner = jnp.dot(s.astype(BF16), v, preferred_element_type=F32)
        st = state_ref[...]
        cross = jnp.dot(q, st.astype(BF16), preferred_element_type=F32) * qd
        kt = (k.astype(F32) * kd).T.astype(BF16)
        state_ref[...] = st * cd + jnp.dot(kt, v, preferred_element_type=F32)
        o = inner + cross
        o = o * lax.rsqrt(jnp.mean(o * o, axis=-1, keepdims=True) + RMS_EPS)
        o_ref[rows, :] = o.astype(o_ref.dtype)
        return carry

    lax.fori_loop(0, seq // c, body, 0)


def _retention_tables(heads):
    c = RET_C
    log_gamma = jnp.log(1.0 - 2.0 ** (-5.0 - jnp.arange(heads, dtype=F32)))
    pos = jnp.arange(c, dtype=F32)
    dist = pos[:, None] - pos[None, :]
    inner_decay = jnp.where(dist[None] >= 0,
                            jnp.exp(jnp.maximum(dist, 0.0)[None] * log_gamma[:, None, None]), 0.0)
    q_decay = jnp.exp((pos + 1.0)[None] * log_gamma[:, None])
    k_decay = jnp.exp((c - 1.0 - pos)[None] * log_gamma[:, None])
    chunk_decay = jnp.broadcast_to(jnp.exp(c * log_gamma)[:, None], (heads, c))
    tab = jnp.stack([q_decay, k_decay, chunk_decay], axis=-1)
    tab = jnp.pad(tab, ((0, 0), (0, 0), (0, LANES - 3)))
    return inner_decay, tab


def _retention(qkvg, batch, seq, qk_dim, v_dim):
    t = qkvg.shape[0]
    dk = qk_dim // RET_HEADS
    dvh = v_dim // RET_HEADS
    inner_decay, tab = _retention_tables(RET_HEADS)
    kb = qk_dim // dk
    vb = 2 * qk_dim // dvh
    return pl.pallas_call(
        _retention_kernel,
        grid=(batch, RET_HEADS),
        in_specs=[
            pl.BlockSpec((seq, dk), lambda b, h: (b, h)),
            pl.BlockSpec((seq, dk), lambda b, h: (b, kb + h)),
            pl.BlockSpec((seq, dvh), lambda b, h: (b, vb + h)),
            pl.BlockSpec((1, RET_C, RET_C), lambda b, h: (h, 0, 0)),
            pl.BlockSpec((1, RET_C, LANES), lambda b, h: (h, 0, 0)),
        ],
        out_specs=pl.BlockSpec((seq, dvh), lambda b, h: (b, h)),
        out_shape=jax.ShapeDtypeStruct((t, v_dim), BF16),
        scratch_shapes=[pltpu.VMEM((dk, dvh), F32)],
        compiler_params=_cparams(("parallel", "arbitrary")),
    )(qkvg, qkvg, qkvg, inner_decay, tab)


def _out_router_kernel(*refs, gated, n_groups, n_experts):
    if gated:
        a_ref, gate_ref, w_ref, x_ref, nf_ref, wr_ref, br_ref = refs[:7]
        x1_ref, hn_ref, route_ref, route_t_ref, cnt_ref, carry_ref = refs[7:]
        gate = gate_ref[...].astype(F32)
        a = (gate * jax.nn.sigmoid(gate) * a_ref[...].astype(F32)).astype(BF16)
    else:
        a_ref, w_ref, x_ref, nf_ref, wr_ref, br_ref = refs[:6]
        x1_ref, hn_ref, route_ref, route_t_ref, cnt_ref, carry_ref = refs[6:]
        a = a_ref[...]
    epg = n_experts // n_groups

    @pl.when(pl.program_id(0) == 0)
    def _():
        carry_ref[...] = jnp.zeros(carry_ref.shape, F32)

    x1 = x_ref[...] + jnp.dot(a, w_ref[...], preferred_element_type=F32)
    x1_ref[...] = x1
    hn = x1 * lax.rsqrt(jnp.mean(x1 * x1, axis=-1, keepdims=True) + RMS_EPS) * nf_ref[...]
    hn_ref[...] = hn

    logits = jnp.dot(hn, wr_ref[...], preferred_element_type=F32,
                     precision=lax.Precision.HIGHEST) + br_ref[...]
    tm = logits.shape[0]
    lane = lax.broadcasted_iota(jnp.int32, logits.shape, 1)
    big = jnp.int32(LANES)
    neg = jnp.float32(-jnp.inf)

    gl = jnp.where(lane < n_groups, logits, neg)
    gmax = jnp.max(gl, axis=-1, keepdims=True)
    g_p = 1.0 / jnp.sum(jnp.exp(gl - gmax), axis=-1, keepdims=True)
    g_idx = jnp.min(jnp.where(gl == gmax, lane, big), axis=-1, keepdims=True)

    lo = n_groups + g_idx * epg
    el = jnp.where((lane >= lo) & (lane < lo + epg), logits, neg)
    v1 = jnp.max(el, axis=-1, keepdims=True)
    i1 = jnp.min(jnp.where(el == v1, lane, big), axis=-1, keepdims=True)
    el2 = jnp.where(lane == i1, neg, el)
    v2 = jnp.max(el2, axis=-1, keepdims=True)
    i2 = jnp.min(jnp.where(el2 == v2, lane, big), axis=-1, keepdims=True)
    tt = jnp.exp(v2 - v1)
    w1 = g_p / (1.0 + tt)
    w2 = g_p * tt / (1.0 + tt)
    e1 = i1 - n_groups
    e2 = i2 - n_groups

    oh1 = lane == e1
    oh2 = lane == e2
    onehot = jnp.where(oh1 | oh2, 1.0, 0.0).astype(BF16)
    r = lax.broadcasted_iota(jnp.int32, (tm, tm), 0)
    cidx = lax.broadcasted_iota(jnp.int32, (tm, tm), 1)
    tri = jnp.where(cidx < r, 1.0, 0.0).astype(BF16)
    base = jnp.dot(tri, onehot, preferred_element_type=F32) + carry_ref[...]
    rank1 = jnp.sum(jnp.where(oh1, base, 0.0), axis=-1, keepdims=True)
    rank2 = jnp.sum(jnp.where(oh2, base, 0.0), axis=-1, keepdims=True)
    carry = carry_ref[...] + jnp.sum(onehot.astype(F32), axis=0, keepdims=True)
    carry_ref[...] = carry
    cnt_ref[...] = carry

    out = jnp.zeros(logits.shape, F32)
    for idx, val in enumerate((e1.astype(F32), e2.astype(F32), w1, w2, rank1, rank2)):
        out = jnp.where(lane == idx, val, out)
    route_ref[...] = out
    route_t_ref[...] = out.T[:route_t_ref.shape[0], :]


def _out_router(a, gate, w, x, nf, w_group, b_group, w_expert, b_expert):
    t, d = x.shape
    k = w.shape[0]
    n_groups = w_group.shape[1]
    n_experts = w_expert.shape[1]
    assert n_groups + n_experts <= LANES
    pad = LANES - n_groups - n_experts
    wr = jnp.pad(jnp.concatenate([w_group, w_expert], axis=1).astype(F32), ((0, 0), (0, pad)))
    br = jnp.pad(jnp.concatenate([b_group, b_expert]).astype(F32), (0, pad)).reshape(1, LANES)
    gated = gate is not None
    row = lambda i: (i, 0)
    fixed = lambda i: (0, 0)
    in_specs = [pl.BlockSpec((TM, k), row)]
    args = [a]
    if gated:
        gate_arr, gate_col = gate
        in_specs.append(pl.BlockSpec((TM, k), lambda i: (i, gate_col)))
        args.append(gate_arr)
    in_specs += [
        pl.BlockSpec((k, d), fixed),
        pl.BlockSpec((TM, d), row),
        pl.BlockSpec((1, d), fixed),
        pl.BlockSpec((d, LANES), fixed),
        pl.BlockSpec((1, LANES), fixed),
    ]
    args += [w, x, nf.reshape(1, d), wr, br]
    return pl.pallas_call(
        functools.partial(_out_router_kernel, gated=gated, n_groups=n_groups, n_experts=n_experts),
        grid=(t // TM,),
        in_specs=in_specs,
        out_specs=[
            pl.BlockSpec((TM, d), row),
            pl.BlockSpec((TM, d), row),
            pl.BlockSpec((TM, LANES), row),
            pl.BlockSpec((SUBLANES, TM), lambda i: (0, i)),
            pl.BlockSpec((1, LANES), fixed),
        ],
        out_shape=[
            jax.ShapeDtypeStruct((t, d), F32),
            jax.ShapeDtypeStruct((t, d), F32),
            jax.ShapeDtypeStruct((t, LANES), F32),
            jax.ShapeDtypeStruct((SUBLANES, t), F32),
            jax.ShapeDtypeStruct((1, LANES), F32),
        ],
        scratch_shapes=[pltpu.VMEM((1, LANES), F32)],
        compiler_params=_cparams(("arbitrary",)),
    )(*args)


def _dispatch_kernel(d1_ref, d2_ref, zs_ref, zf_ref, nu_ref, hn_ref, xs_ref, zero_ref, sem, zsem):
    tm = hn_ref.shape[0]
    rb = zero_ref.shape[0]
    nb = xs_ref.shape[0] // rb
    base = pl.program_id(0) * tm

    @pl.when(pl.program_id(0) == 0)
    def _():
        zero_ref[...] = jnp.zeros(zero_ref.shape, zero_ref.dtype)

        def zcopy(row0):
            rows = pl.ds(pl.multiple_of(row0, rb), rb)
            return pltpu.make_async_copy(zero_ref, xs_ref.at[rows, :], zsem)

        def zissue(e, carry):
            @pl.when(zf_ref[e] > 0)
            def _():
                zcopy(zs_ref[e]).start()
            return carry

        def zwait(e, carry):
            @pl.when(zf_ref[e] > 0)
            def _():
                zcopy(zs_ref[e]).wait()
            return carry

        def tissue(b, carry):
            zcopy(b * rb).start()
            return carry

        def twait(b, carry):
            zcopy(b * rb).wait()
            return carry

        lax.fori_loop(0, zs_ref.shape[0], zissue, 0)
        lax.fori_loop(nu_ref[0], nb, tissue, 0)
        lax.fori_loop(0, zs_ref.shape[0], zwait, 0)
        lax.fori_loop(nu_ref[0], nb, twait, 0)

    def issue(r, carry):
        src = hn_ref.at[pl.ds(r, 1), :]
        pltpu.make_async_copy(src, xs_ref.at[pl.ds(d1_ref[base + r], 1), :], sem.at[0]).start()
        pltpu.make_async_copy(src, xs_ref.at[pl.ds(d2_ref[base + r], 1), :], sem.at[1]).start()
        return carry

    lax.fori_loop(0, tm, issue, 0, unroll=DMA_UNROLL)
    for s in range(2):
        pltpu.make_async_copy(hn_ref, xs_ref.at[pl.ds(0, tm), :], sem.at[s]).wait()


def _dispatch(hn, d1, d2, zstart, zflag, n_used, n_rows):
    t, d = hn.shape
    grid_spec = pltpu.PrefetchScalarGridSpec(
        num_scalar_prefetch=5,
        grid=(t // TC,),
        in_specs=[pl.BlockSpec((TC, d), lambda i, *_: (i, 0))],
        out_specs=pl.BlockSpec(memory_space=pl.ANY),
        scratch_shapes=[pltpu.VMEM((ROW_BLOCK, d), hn.dtype),
                        pltpu.SemaphoreType.DMA((2,)), pltpu.SemaphoreType.DMA(())],
    )
    return pl.pallas_call(
        _dispatch_kernel,
        grid_spec=grid_spec,
        out_shape=jax.ShapeDtypeStruct((n_rows, d), hn.dtype),
        compiler_params=_cparams(("arbitrary",)),
    )(d1, d2, zstart, zflag, n_used, hn)


def _expert_kernel(be_ref, nu_ref, x_ref, wg_ref, wu_ref, wd_ref, y_ref, wgb_ref, wub_ref, wdb_ref):
    i = pl.program_id(0)
    used = i < nu_ref[0]
    new_expert = (i == 0) | (be_ref[i] != be_ref[jnp.maximum(i - 1, 0)])

    @pl.when(used & new_expert)
    def _():
        wgb_ref[...] = wg_ref[0].astype(BF16)
        wub_ref[...] = wu_ref[0].astype(BF16)
        wdb_ref[...] = wd_ref[0].astype(BF16)

    @pl.when(used)
    def _():
        x = x_ref[...].astype(BF16)
        g = jnp.dot(x, wgb_ref[...], preferred_element_type=F32)
        u = jnp.dot(x, wub_ref[...], preferred_element_type=F32)
        hmid = (g * jax.nn.sigmoid(g) * u).astype(BF16)
        y_ref[...] = jnp.dot(hmid, wdb_ref[...], preferred_element_type=F32)

    @pl.when(jnp.logical_not(used))
    def _():
        y_ref[...] = jnp.zeros(y_ref.shape, y_ref.dtype)


def _experts(xs, block_e, n_used, w_gate, w_up, w_down):
    n_rows, d = xs.shape
    ff = w_gate.shape[2]
    nb = n_rows // ROW_BLOCK
    blk = lambda i, be, nu: (jnp.minimum(i, nu[0] - 1), 0)
    wsel = lambda i, be, nu: (be[jnp.minimum(i, nu[0] - 1)], 0, 0)
    grid_spec = pltpu.PrefetchScalarGridSpec(
        num_scalar_prefetch=2,
        grid=(nb,),
        in_specs=[
            pl.BlockSpec((ROW_BLOCK, d), blk),
            pl.BlockSpec((1, d, ff), wsel),
            pl.BlockSpec((1, d, ff), wsel),
            pl.BlockSpec((1, ff, d), wsel),
        ],
        out_specs=pl.BlockSpec((ROW_BLOCK, d), lambda i, be, nu: (i, 0)),
        scratch_shapes=[pltpu.VMEM((d, ff), BF16), pltpu.VMEM((d, ff), BF16), pltpu.VMEM((ff, d), BF16)],
    )
    return pl.pallas_call(
        _expert_kernel,
        grid_spec=grid_spec,
        out_shape=jax.ShapeDtypeStruct((n_rows, d), F32),
        compiler_params=_cparams(("arbitrary",)),
    )(block_e, n_used, xs, w_gate, w_up, w_down)


def _combine_kernel(d1_ref, d2_ref, x_ref, route_ref, nf_ref, ys_ref, o_ref, buf_ref, sem, *, final_norm):
    tm = x_ref.shape[0]
    base = pl.program_id(0) * tm

    def issue(r, carry):
        pltpu.make_async_copy(ys_ref.at[pl.ds(d1_ref[base + r], 1), :],
                              buf_ref.at[0, pl.ds(r, 1), :], sem.at[0]).start()
        pltpu.make_async_copy(ys_ref.at[pl.ds(d2_ref[base + r], 1), :],
                              buf_ref.at[1, pl.ds(r, 1), :], sem.at[1]).start()
        return carry

    lax.fori_loop(0, tm, issue, 0, unroll=DMA_UNROLL)
    for s in range(2):
        pltpu.make_async_copy(ys_ref.at[pl.ds(0, tm), :], buf_ref.at[s], sem.at[s]).wait()

    route = route_ref[...]
    y = x_ref[...] + (route[:, 2:3] * buf_ref[0] + route[:, 3:4] * buf_ref[1])
    if final_norm:
        y = y * lax.rsqrt(jnp.mean(y * y, axis=-1, keepdims=True) + RMS_EPS) * nf_ref[...]
    o_ref[...] = y


def _combine(x1, route, ys, d1, d2, nf, final_norm):
    t, d = x1.shape
    grid_spec = pltpu.PrefetchScalarGridSpec(
        num_scalar_prefetch=2,
        grid=(t // TC,),
        in_specs=[
            pl.BlockSpec((TC, d), lambda i, d1, d2: (i, 0)),
            pl.BlockSpec((TC, LANES), lambda i, d1, d2: (i, 0)),
            pl.BlockSpec((1, d), lambda i, d1, d2: (0, 0)),
            pl.BlockSpec(memory_space=pl.ANY),
        ],
        out_specs=pl.BlockSpec((TC, d), lambda i, d1, d2: (i, 0)),
        scratch_shapes=[pltpu.VMEM((2, TC, d), F32), pltpu.SemaphoreType.DMA((2,))],
    )
    return pl.pallas_call(
        functools.partial(_combine_kernel, final_norm=final_norm),
        grid_spec=grid_spec,
        out_shape=jax.ShapeDtypeStruct((t, d), F32),
        compiler_params=_cparams(("arbitrary",)),
    )(d1, d2, x1, route, nf.reshape(1, d).astype(F32), ys)


def _moe(x1, hn, route, route_t, cnt, w_gate, w_up, w_down, nf, final_norm):
    t, d = x1.shape
    n_experts = w_gate.shape[0]
    ids = jnp.arange(n_experts, dtype=jnp.int32)
    e1, e2, rank1, rank2 = (route_t[f].astype(jnp.int32) for f in (0, 1, 4, 5))
    counts = cnt[0, :n_experts].astype(jnp.int32)
    padded = ((counts + ROW_BLOCK - 1) // ROW_BLOCK) * ROW_BLOCK
    pad_end = jnp.cumsum(padded)
    pad_start = pad_end - padded

    def dest(e, rank):
        return jnp.sum(jnp.where(e[:, None] == ids[None, :], pad_start[None, :], 0), axis=1) + rank

    d1 = dest(e1, rank1)
    d2 = dest(e2, rank2)
    nb = (t * TOP_K_INNER) // ROW_BLOCK + n_experts
    n_used = (pad_end[-1:] // ROW_BLOCK).astype(jnp.int32)
    first_row = jnp.arange(nb, dtype=jnp.int32) * ROW_BLOCK
    block_e = jnp.sum((pad_end[None, :] <= first_row[:, None]).astype(jnp.int32), axis=1)
    block_e = jnp.minimum(block_e, n_experts - 1)
    zstart = jnp.maximum(pad_end - ROW_BLOCK, 0).astype(jnp.int32)
    zflag = (padded > 0).astype(jnp.int32)
    xs = _dispatch(hn, d1, d2, zstart, zflag, n_used, nb * ROW_BLOCK)
    ys = _experts(xs, block_e, n_used, w_gate, w_up, w_down)
    return _combine(x1, route, ys, d1, d2, nf, final_norm)


def _diff_lambda_init(layer_idx):
    return 0.8 - 0.6 * math.exp(-0.3 * layer_idx)


def _rope_tables(seq, dk):
    angle = 1.0 / (10000.0 ** jnp.linspace(0.0, 1.0, dk // 2, dtype=F32))
    angle = jnp.repeat(angle, 2)
    phase = jnp.arange(seq, dtype=F32)[:, None] * angle[None, :]
    sin, cos = jnp.sin(phase), jnp.cos(phase)
    even = (jnp.arange(dk) % 2 == 0)[None, :]
    return cos, jnp.where(even, -sin, 0.0), jnp.where(even, 0.0, sin)


def kernel(x, rel_bias_table, norm_mix, norm_ffn, norm_final, diff_w_in, diff_lambda_qk, diff_subln,
           diff_w_out, ret_w_in, ret_w_out, moe_w_group, moe_b_group, moe_w_expert, moe_b_expert,
           moe_w_gate, moe_w_up, moe_w_down):
    batch, seq, d = x.shape
    depth = norm_mix.shape[0]
    t = batch * seq
    heads = rel_bias_table.shape[1]
    dh = diff_lambda_qk.shape[-1]
    qk_dim = (ret_w_in.shape[2] - 2 * ret_w_out.shape[1]) // 2
    v_dim = ret_w_out.shape[1]
    dk = qk_dim // RET_HEADS
    xt = x.reshape(t, d)
    for i in range(depth):
        j = i // N_MIXERS
        if i % N_MIXERS == 0:
            col_scale = jnp.concatenate([jnp.full((d,), dh ** -0.5, F32), jnp.ones((2 * d,), F32)])
            w_in = (diff_w_in[j] * col_scale[None, :]).astype(BF16)
            qkv = _norm_matmul(xt, norm_mix[i], w_in)
            mix = _diff_attention(qkv, rel_bias_table, diff_lambda_qk[j], diff_subln[j],
                                  _diff_lambda_init(i), batch, seq, heads)
            gate = None
            w_out = diff_w_out[j].astype(BF16)
        else:
            n_in = ret_w_in.shape[2]
            col_scale = jnp.concatenate([jnp.ones((qk_dim,), F32), jnp.full((qk_dim,), dk ** -0.5, F32),
                                         jnp.ones((n_in - 2 * qk_dim,), F32)])
            w_in = (ret_w_in[j] * col_scale[None, :]).astype(BF16)
            qkvg = _norm_matmul(xt, norm_mix[i], w_in, rope=_rope_tables(seq, dk),
                                n_rope=2 * qk_dim // TN, seq=seq)
            mix = _retention(qkvg, batch, seq, qk_dim, v_dim)
            gate = (qkvg, (2 * qk_dim + v_dim) // v_dim)
            w_out = ret_w_out[j].astype(BF16)
        x1, hn, route, route_t, cnt = _out_router(mix, gate, w_out, xt, norm_ffn[i], moe_w_group[i],
                                                  moe_b_group[i], moe_w_expert[i], moe_b_expert[i])
        xt = _moe(x1, hn, route, route_t, cnt, moe_w_gate[i], moe_w_up[i], moe_w_down[i],
                  norm_final, i == depth - 1)
    return xt.reshape(batch, seq, d)
```

```python
import functools
import math

import jax
import jax.numpy as jnp
from jax import lax
from jax.experimental import pallas as pl
from jax.experimental.pallas import tpu as pltpu

F32 = jnp.float32
BF16 = jnp.bfloat16

N_MIXERS = 2
NUM_BUCKETS = 32
MAX_DISTANCE = 128
RET_HEADS = 4
TOP_K_INNER = 2
RMS_EPS = 1e-6
SUBLN_EPS = 1e-5

LANES = 128
SUBLANES = 8
VMEM_LIMIT = 48 * 1024 * 1024

TM = 512
TN = 1024
TQ = 256
ATTN_HEADS = 2
RET_C = 256
ROW_BLOCK = 256
TC = 256
DMA_UNROLL = 8


def _cparams(sem):
    return pltpu.CompilerParams(dimension_semantics=sem, vmem_limit_bytes=VMEM_LIMIT)


def _norm_matmul_kernel(x_ref, g_ref, w_ref, *rest, n_rope):
    if n_rope:
        cos_ref, sa_ref, sb_ref, o_ref = rest
    else:
        (o_ref,) = rest
    x = x_ref[...]
    xn = (x * lax.rsqrt(jnp.mean(x * x, axis=-1, keepdims=True) + RMS_EPS) * g_ref[...]).astype(BF16)
    for j in range(o_ref.shape[1] // TN):
        cols = slice(j * TN, (j + 1) * TN)
        acc = jnp.dot(xn, w_ref[:, cols], preferred_element_type=F32)
        if j < n_rope:
            reps = TN // cos_ref.shape[1]
            c = jnp.tile(cos_ref[...], (1, reps))
            sa = jnp.tile(sa_ref[...], (1, reps))
            sb = jnp.tile(sb_ref[...], (1, reps))
            nxt = pltpu.roll(acc, TN - 1, axis=1)
            prv = pltpu.roll(acc, 1, axis=1)
            acc = acc * c + nxt * sa + prv * sb
        o_ref[:, cols] = acc.astype(o_ref.dtype)


def _norm_matmul(x, g, w, rope=None, n_rope=0, seq=None):
    t, d = x.shape
    n = w.shape[1]
    in_specs = [
        pl.BlockSpec((TM, d), lambda i: (i, 0)),
        pl.BlockSpec((1, d), lambda i: (0, 0)),
        pl.BlockSpec((d, n), lambda i: (0, 0), pipeline_mode=pl.Buffered(1)),
    ]
    args = [x, g.reshape(1, d), w]
    if rope is not None:
        nb = seq // TM
        dk = rope[0].shape[1]
        in_specs += [pl.BlockSpec((TM, dk), lambda i: (i % nb, 0))] * 3
        args += list(rope)
    return pl.pallas_call(
        functools.partial(_norm_matmul_kernel, n_rope=n_rope),
        grid=(t // TM,),
        in_specs=in_specs,
        out_specs=pl.BlockSpec((TM, n), lambda i: (i, 0)),
        out_shape=jax.ShapeDtypeStruct((t, n), BF16),
        compiler_params=_cparams(("parallel",)),
    )(*args)


def _diff_attn_kernel(q_ref, k_ref, v_ref, bias_ref, lam_ref, g_ref, o_ref,
                      vt_ref, m_ref, l_ref, acc_ref, s_ref, *, lambda_init):
    seq = q_ref.shape[0]
    hp, _, _, tq = bias_ref.shape
    dv = q_ref.shape[1] // hp
    dh = dv // 2
    nq = seq // tq

    def head(hh):
        return slice(hh * dv, (hh + 1) * dv)

    for hh in range(hp):
        for j in range(nq):
            vt_ref[hh, j] = v_ref[j * tq:(j + 1) * tq, head(hh)].astype(F32).T.astype(BF16)

    lq = lam_ref[...]
    lam = (jnp.exp(jnp.sum(lq[0:1] * lq[1:2], axis=-1, keepdims=True))
           - jnp.exp(jnp.sum(lq[2:3] * lq[3:4], axis=-1, keepdims=True)) + lambda_init)
    lane = lax.broadcasted_iota(jnp.int32, (tq, dv), 1)

    def q_body(qi, carry):
        qrows = pl.ds(pl.multiple_of(qi * tq, tq), tq)
        qs = []
        for hh in range(hp):
            q = q_ref[qrows, head(hh)]
            zero = jnp.zeros_like(q)
            qs.append(jnp.concatenate([jnp.where(lane < dh, q, zero), jnp.where(lane >= dh, q, zero)],
                                      axis=0))

        m_ref[...] = jnp.full(m_ref.shape, -jnp.inf, F32)
        l_ref[...] = jnp.zeros(l_ref.shape, F32)
        acc_ref[...] = jnp.zeros(acc_ref.shape, F32)

        def score_dots(j):
            rows = pl.ds(pl.multiple_of(j * tq, tq), tq)
            for hh in range(hp):
                s_ref[hh] = lax.dot_general(k_ref[rows, head(hh)], qs[hh], (((1,), (1,)), ((), ())),
                                            preferred_element_type=F32)

        def step(j, tile, prefetch):
            probs, alphas = [], []
            for hh in range(hp):
                s = s_ref[hh]
                if tile is not None:
                    bias = bias_ref[hh, tile]
                    s = s + jnp.concatenate([bias, bias], axis=1)
                m_old = m_ref[hh]
                m_new = jnp.maximum(m_old, jnp.max(s, axis=0, keepdims=True))
                alpha = jnp.exp(m_old - m_new)
                p = jnp.exp(s - m_new)
                l_ref[hh] = alpha * l_ref[hh] + jnp.sum(p, axis=0, keepdims=True)
                m_ref[hh] = m_new
                probs.append(p.astype(BF16))
                alphas.append(alpha)
            if prefetch:
                score_dots(j + 1)
            for hh in range(hp):
                acc_ref[hh] = alphas[hh] * acc_ref[hh] + jnp.dot(vt_ref[hh, j], probs[hh],
                                                                 preferred_element_type=F32)

        score_dots(0)

        def far_body(j, c):
            step(j, None, True)
            return c

        lax.fori_loop(0, jnp.maximum(qi - 1, 0), far_body, 0)

        @pl.when(qi >= 1)
        def _():
            step(qi - 1, 1, True)

        step(qi, 0, False)

        for hh in range(hp):
            on = acc_ref[hh] / l_ref[hh]
            ot = on[:, :tq] - lam * on[:, tq:]
            ot = ot * lax.rsqrt(jnp.mean(ot * ot, axis=0, keepdims=True) + SUBLN_EPS)
            o_ref[qrows, head(hh)] = (ot.T * g_ref[...] * (1.0 - lambda_init)).astype(o_ref.dtype)
        return carry

    lax.fori_loop(0, nq, q_body, 0)


def _t5_bucket(rel):
    max_exact = NUM_BUCKETS // 2
    n = jnp.maximum(rel, 0)
    is_small = n < max_exact
    nf = jnp.maximum(n, max_exact).astype(F32)
    large = max_exact + (jnp.log(nf / max_exact) / math.log(MAX_DISTANCE / max_exact)
                         * (NUM_BUCKETS - max_exact)).astype(jnp.int32)
    large = jnp.minimum(large, NUM_BUCKETS - 1)
    return jnp.where(is_small, n, large)


def _bias_tiles(rel_table):
    heads = rel_table.shape[1]
    span = 2 * TQ + 1
    bucket = _t5_bucket(jnp.arange(span))
    onehot = (bucket[:, None] == jnp.arange(NUM_BUCKETS)[None, :]).astype(F32)
    table = rel_table.astype(F32) - rel_table[NUM_BUCKETS - 1].astype(F32)[None, :]
    by_rel = jnp.dot(onehot, table, precision=lax.Precision.HIGHEST).T
    toep = jnp.tile(by_rel, (1, TQ))[:, :TQ * (span - 1)].reshape(heads, TQ, span - 1)
    kj = jnp.arange(TQ)[:, None]
    qi = jnp.arange(TQ)[None, :]
    diag = jnp.where((qi >= kj)[None], toep[:, :, :TQ], -jnp.inf)
    return jnp.stack([diag, toep[:, :, TQ:]], axis=1)


def _diff_attention(qkv, rel_table, lam_qk, subln_g, lambda_init, batch, seq, heads):
    t = qkv.shape[0]
    d = qkv.shape[1] // 3
    dv = d // heads
    hp = ATTN_HEADS
    assert TQ + 1 >= MAX_DISTANCE and seq % TQ == 0 and dv == LANES and heads % hp == 0
    nq = seq // TQ
    groups = heads // hp
    bias = _bias_tiles(rel_table)
    return pl.pallas_call(
        functools.partial(_diff_attn_kernel, lambda_init=lambda_init),
        grid=(batch, groups),
        in_specs=[
            pl.BlockSpec((seq, hp * dv), lambda b, g: (b, g)),
            pl.BlockSpec((seq, hp * dv), lambda b, g: (b, groups + g)),
            pl.BlockSpec((seq, hp * dv), lambda b, g: (b, 2 * groups + g)),
            pl.BlockSpec((hp, 2, TQ, TQ), lambda b, g: (g, 0, 0, 0)),
            pl.BlockSpec(lam_qk.shape, lambda b, g: (0, 0)),
            pl.BlockSpec((1, dv), lambda b, g: (0, 0)),
        ],
        out_specs=pl.BlockSpec((seq, hp * dv), lambda b, g: (b, g)),
        out_shape=jax.ShapeDtypeStruct((t, d), BF16),
        scratch_shapes=[
            pltpu.VMEM((hp, nq, dv, TQ), BF16),
            pltpu.VMEM((hp, 1, 2 * TQ), F32),
            pltpu.VMEM((hp, 1, 2 * TQ), F32),
            pltpu.VMEM((hp, dv, 2 * TQ), F32),
            pltpu.VMEM((hp, TQ, 2 * TQ), F32),
        ],
        compiler_params=_cparams(("parallel", "parallel")),
    )(qkv, qkv, qkv, bias, lam_qk.astype(F32), subln_g.reshape(1, dv).astype(F32))


def _retention_kernel(q_ref, k_ref, v_ref, dec_ref, tab_ref, o_ref, state_ref):
    seq = q_ref.shape[0]
    c = dec_ref.shape[1]
    dec = dec_ref[0]
    tab = tab_ref[0]
    qd = tab[:, 0:1]
    kd = tab[:, 1:2]
    cd = tab[0:1, 2:3]
    state_ref[...] = jnp.zeros(state_ref.shape, F32)

    def body(n, carry):
        rows = pl.ds(pl.multiple_of(n * c, c), c)
        q = q_ref[rows, :]
        k = k_ref[rows, :]
        v = v_ref[rows, :]
        s = lax.dot_general(q, k, (((1,), (1,)), ((), ())), preferred_element_type=F32) * dec
        inner = jnp.dot(s.astype(BF16), v, preferred_element_type=F32)
        st = state_ref[...]
        cross = jnp.dot(q, st.astype(BF16), preferred_element_type=F32) * qd
        kt = (k.astype(F32) * kd).T.astype(BF16)
        state_ref[...] = st * cd + jnp.dot(kt, v, preferred_element_type=F32)
        o = inner + cross
        o = o * lax.rsqrt(jnp.mean(o * o, axis=-1, keepdims=True) + RMS_EPS)
        o_ref[rows, :] = o.astype(o_ref.dtype)
        return carry

    lax.fori_loop(0, seq // c, body, 0)


def _retention_tables(heads):
    c = RET_C
    log_gamma = jnp.log(1.0 - 2.0 ** (-5.0 - jnp.arange(heads, dtype=F32)))
    pos = jnp.arange(c, dtype=F32)
    dist = pos[:, None] - pos[None, :]
    inner_decay = jnp.where(dist[None] >= 0,
                            jnp.exp(jnp.maximum(dist, 0.0)[None] * log_gamma[:, None, None]), 0.0)
    q_decay = jnp.exp((pos + 1.0)[None] * log_gamma[:, None])
    k_decay = jnp.exp((c - 1.0 - pos)[None] * log_gamma[:, None])
    chunk_decay = jnp.broadcast_to(jnp.exp(c * log_gamma)[:, None], (heads, c))
    tab = jnp.stack([q_decay, k_decay, chunk_decay], axis=-1)
    tab = jnp.pad(tab, ((0, 0), (0, 0), (0, LANES - 3)))
    return inner_decay, tab


def _retention(qkvg, batch, seq, qk_dim, v_dim):
    t = qkvg.shape[0]
    dk = qk_dim // RET_HEADS
    dvh = v_dim // RET_HEADS
    inner_decay, tab = _retention_tables(RET_HEADS)
    kb = qk_dim // dk
    vb = 2 * qk_dim // dvh
    return pl.pallas_call(
        _retention_kernel,
        grid=(batch, RET_HEADS),
        in_specs=[
            pl.BlockSpec((seq, dk), lambda b, h: (b, h)),
            pl.BlockSpec((seq, dk), lambda b, h: (b, kb + h)),
            pl.BlockSpec((seq, dvh), lambda b, h: (b, vb + h)),
            pl.BlockSpec((1, RET_C, RET_C), lambda b, h: (h, 0, 0)),
            pl.BlockSpec((1, RET_C, LANES), lambda b, h: (h, 0, 0)),
        ],
        out_specs=pl.BlockSpec((seq, dvh), lambda b, h: (b, h)),
        out_shape=jax.ShapeDtypeStruct((t, v_dim), BF16),
        scratch_shapes=[pltpu.VMEM((dk, dvh), F32)],
        compiler_params=_cparams(("parallel", "arbitrary")),
    )(qkvg, qkvg, qkvg, inner_decay, tab)


def _out_router_kernel(*refs, gated, n_groups, n_experts):
    if gated:
        a_ref, gate_ref, w_ref, x_ref, nf_ref, wr_ref, br_ref = refs[:7]
        x1_ref, hn_ref, route_ref, route_t_ref, cnt_ref, carry_ref = refs[7:]
        gate = gate_ref[...].astype(F32)
        a = (gate * jax.nn.sigmoid(gate) * a_ref[...].astype(F32)).astype(BF16)
    else:
        a_ref, w_ref, x_ref, nf_ref, wr_ref, br_ref = refs[:6]
        x1_ref, hn_ref, route_ref, route_t_ref, cnt_ref, carry_ref = refs[6:]
        a = a_ref[...]
    epg = n_experts // n_groups

    @pl.when(pl.program_id(0) == 0)
    def _():
        carry_ref[...] = jnp.zeros(carry_ref.shape, F32)

    x1 = x_ref[...] + jnp.dot(a, w_ref[...], preferred_element_type=F32)
    x1_ref[...] = x1
    hn = x1 * lax.rsqrt(jnp.mean(x1 * x1, axis=-1, keepdims=True) + RMS_EPS) * nf_ref[...]
    hn_ref[...] = hn

    logits = jnp.dot(hn, wr_ref[...], preferred_element_type=F32,
                     precision=lax.Precision.HIGHEST) + br_ref[...]
    tm = logits.shape[0]
    lane = lax.broadcasted_iota(jnp.int32, logits.shape, 1)
    big = jnp.int32(LANES)
    neg = jnp.float32(-jnp.inf)

    gl = jnp.where(lane < n_groups, logits, neg)
    gmax = jnp.max(gl, axis=-1, keepdims=True)
    g_p = 1.0 / jnp.sum(jnp.exp(gl - gmax), axis=-1, keepdims=True)
    g_idx = jnp.min(jnp.where(gl == gmax, lane, big), axis=-1, keepdims=True)

    lo = n_groups + g_idx * epg
    el = jnp.where((lane >= lo) & (lane < lo + epg), logits, neg)
    v1 = jnp.max(el, axis=-1, keepdims=True)
    i1 = jnp.min(jnp.where(el == v1, lane, big), axis=-1, keepdims=True)
    el2 = jnp.where(lane == i1, neg, el)
    v2 = jnp.max(el2, axis=-1, keepdims=True)
    i2 = jnp.min(jnp.where(el2 == v2, lane, big), axis=-1, keepdims=True)
    tt = jnp.exp(v2 - v1)
    w1 = g_p / (1.0 + tt)
    w2 = g_p * tt / (1.0 + tt)
    e1 = i1 - n_groups
    e2 = i2 - n_groups

    oh1 = lane == e1
    oh2 = lane == e2
    onehot = jnp.where(oh1 | oh2, 1.0, 0.0).astype(BF16)
    r = lax.broadcasted_iota(jnp.int32, (tm, tm), 0)
    cidx = lax.broadcasted_iota(jnp.int32, (tm, tm), 1)
    tri = jnp.where(cidx < r, 1.0, 0.0).astype(BF16)
    base = jnp.dot(tri, onehot, preferred_element_type=F32) + carry_ref[...]
    rank1 = jnp.sum(jnp.where(oh1, base, 0.0), axis=-1, keepdims=True)
    rank2 = jnp.sum(jnp.where(oh2, base, 0.0), axis=-1, keepdims=True)
    carry = carry_ref[...] + jnp.sum(onehot.astype(F32), axis=0, keepdims=True)
    carry_ref[...] = carry
    cnt_ref[...] = carry

    out = jnp.zeros(logits.shape, F32)
    for idx, val in enumerate((e1.astype(F32), e2.astype(F32), w1, w2, rank1, rank2)):
        out = jnp.where(lane == idx, val, out)
    route_ref[...] = out
    route_t_ref[...] = out.T[:route_t_ref.shape[0], :]


def _out_router(a, gate, w, x, nf, w_group, b_group, w_expert, b_expert):
    t, d = x.shape
    k = w.shape[0]
    n_groups = w_group.shape[1]
    n_experts = w_expert.shape[1]
    assert n_groups + n_experts <= LANES
    pad = LANES - n_groups - n_experts
    wr = jnp.pad(jnp.concatenate([w_group, w_expert], axis=1).astype(F32), ((0, 0), (0, pad)))
    br = jnp.pad(jnp.concatenate([b_group, b_expert]).astype(F32), (0, pad)).reshape(1, LANES)
    gated = gate is not None
    row = lambda i: (i, 0)
    fixed = lambda i: (0, 0)
    in_specs = [pl.BlockSpec((TM, k), row)]
    args = [a]
    if gated:
        gate_arr, gate_col = gate
        in_specs.append(pl.BlockSpec((TM, k), lambda i: (i, gate_col)))
        args.append(gate_arr)
    in_specs += [
        pl.BlockSpec((k, d), fixed),
        pl.BlockSpec((TM, d), row),
        pl.BlockSpec((1, d), fixed),
        pl.BlockSpec((d, LANES), fixed),
        pl.BlockSpec((1, LANES), fixed),
    ]
    args += [w, x, nf.reshape(1, d), wr, br]
    return pl.pallas_call(
        functools.partial(_out_router_kernel, gated=gated, n_groups=n_groups, n_experts=n_experts),
        grid=(t // TM,),
        in_specs=in_specs,
        out_specs=[
            pl.BlockSpec((TM, d), row),
            pl.BlockSpec((TM, d), row),
            pl.BlockSpec((TM, LANES), row),
            pl.BlockSpec((SUBLANES, TM), lambda i: (0, i)),
            pl.BlockSpec((1, LANES), fixed),
        ],
        out_shape=[
            jax.ShapeDtypeStruct((t, d), F32),
            jax.ShapeDtypeStruct((t, d), F32),
            jax.ShapeDtypeStruct((t, LANES), F32),
            jax.ShapeDtypeStruct((SUBLANES, t), F32),
            jax.ShapeDtypeStruct((1, LANES), F32),
        ],
        scratch_shapes=[pltpu.VMEM((1, LANES), F32)],
        compiler_params=_cparams(("arbitrary",)),
    )(*args)


def _dispatch_kernel(d1_ref, d2_ref, zs_ref, zf_ref, nu_ref, hn_ref, xs_ref, zero_ref, sem, zsem):
    tm = hn_ref.shape[0]
    rb = zero_ref.shape[0]
    nb = xs_ref.shape[0] // rb
    base = pl.program_id(0) * tm

    @pl.when(pl.program_id(0) == 0)
    def _():
        zero_ref[...] = jnp.zeros(zero_ref.shape, zero_ref.dtype)

        def zcopy(row0):
            rows = pl.ds(pl.multiple_of(row0, rb), rb)
            return pltpu.make_async_copy(zero_ref, xs_ref.at[rows, :], zsem)

        def zissue(e, carry):
            @pl.when(zf_ref[e] > 0)
            def _():
                zcopy(zs_ref[e]).start()
            return carry

        def zwait(e, carry):
            @pl.when(zf_ref[e] > 0)
            def _():
                zcopy(zs_ref[e]).wait()
            return carry

        def tissue(b, carry):
            zcopy(b * rb).start()
            return carry

        def twait(b, carry):
            zcopy(b * rb).wait()
            return carry

        lax.fori_loop(0, zs_ref.shape[0], zissue, 0)
        lax.fori_loop(nu_ref[0], nb, tissue, 0)
        lax.fori_loop(0, zs_ref.shape[0], zwait, 0)
        lax.fori_loop(nu_ref[0], nb, twait, 0)

    def issue(r, carry):
        src = hn_ref.at[pl.ds(r, 1), :]
        pltpu.make_async_copy(src, xs_ref.at[pl.ds(d1_ref[base + r], 1), :], sem.at[0]).start()
        pltpu.make_async_copy(src, xs_ref.at[pl.ds(d2_ref[base + r], 1), :], sem.at[1]).start()
        return carry

    lax.fori_loop(0, tm, issue, 0, unroll=DMA_UNROLL)
    for s in range(2):
        pltpu.make_async_copy(hn_ref, xs_ref.at[pl.ds(0, tm), :], sem.at[s]).wait()


def _dispatch(hn, d1, d2, zstart, zflag, n_used, n_rows):
    t, d = hn.shape
    grid_spec = pltpu.PrefetchScalarGridSpec(
        num_scalar_prefetch=5,
        grid=(t // TC,),
        in_specs=[pl.BlockSpec((TC, d), lambda i, *_: (i, 0))],
        out_specs=pl.BlockSpec(memory_space=pl.ANY),
        scratch_shapes=[pltpu.VMEM((ROW_BLOCK, d), hn.dtype),
                        pltpu.SemaphoreType.DMA((2,)), pltpu.SemaphoreType.DMA(())],
    )
    return pl.pallas_call(
        _dispatch_kernel,
        grid_spec=grid_spec,
        out_shape=jax.ShapeDtypeStruct((n_rows, d), hn.dtype),
        compiler_params=_cparams(("arbitrary",)),
    )(d1, d2, zstart, zflag, n_used, hn)


def _expert_kernel(be_ref, nu_ref, x_ref, wg_ref, wu_ref, wd_ref, y_ref, wgb_ref, wub_ref, wdb_ref):
    i = pl.program_id(0)
    used = i < nu_ref[0]
    new_expert = (i == 0) | (be_ref[i] != be_ref[jnp.maximum(i - 1, 0)])

    @pl.when(used & new_expert)
    def _():
        wgb_ref[...] = wg_ref[0, 0].astype(BF16)
        wub_ref[...] = wu_ref[0, 0].astype(BF16)
        wdb_ref[...] = wd_ref[0, 0].astype(BF16)

    @pl.when(used)
    def _():
        x = x_ref[...].astype(BF16)
        g = jnp.dot(x, wgb_ref[...], preferred_element_type=F32)
        u = jnp.dot(x, wub_ref[...], preferred_element_type=F32)
        hmid = (g * jax.nn.sigmoid(g) * u).astype(BF16)
        y_ref[...] = jnp.dot(hmid, wdb_ref[...], preferred_element_type=F32)

    @pl.when(jnp.logical_not(used))
    def _():
        y_ref[...] = jnp.zeros(y_ref.shape, y_ref.dtype)


def _experts(xs, block_e, n_used, w_gate, w_up, w_down, layer):
    n_rows, d = xs.shape
    ff = w_gate.shape[3]
    nb = n_rows // ROW_BLOCK
    last = lambda i, nu: jnp.maximum(jnp.minimum(i, nu[0] - 1), 0)
    blk = lambda i, be, nu: (last(i, nu), 0)
    wsel = lambda i, be, nu: (layer, be[last(i, nu)], 0, 0)
    grid_spec = pltpu.PrefetchScalarGridSpec(
        num_scalar_prefetch=2,
        grid=(nb,),
        in_specs=[
            pl.BlockSpec((ROW_BLOCK, d), blk),
            pl.BlockSpec((1, 1, d, ff), wsel),
            pl.BlockSpec((1, 1, d, ff), wsel),
            pl.BlockSpec((1, 1, ff, d), wsel),
        ],
        out_specs=pl.BlockSpec((ROW_BLOCK, d), lambda i, be, nu: (i, 0)),
        scratch_shapes=[pltpu.VMEM((d, ff), BF16), pltpu.VMEM((d, ff), BF16), pltpu.VMEM((ff, d), BF16)],
    )
    return pl.pallas_call(
        _expert_kernel,
        grid_spec=grid_spec,
        out_shape=jax.ShapeDtypeStruct((n_rows, d), F32),
        compiler_params=_cparams(("arbitrary",)),
    )(block_e, n_used, xs, w_gate, w_up, w_down)


def _combine_kernel(d1_ref, d2_ref, x_ref, route_ref, nf_ref, ys_ref, o_ref, buf_ref, sem, *, final_norm):
    tm = x_ref.shape[0]
    base = pl.program_id(0) * tm

    def issue(r, carry):
        pltpu.make_async_copy(ys_ref.at[pl.ds(d1_ref[base + r], 1), :],
                              buf_ref.at[0, pl.ds(r, 1), :], sem.at[0]).start()
        pltpu.make_async_copy(ys_ref.at[pl.ds(d2_ref[base + r], 1), :],
                              buf_ref.at[1, pl.ds(r, 1), :], sem.at[1]).start()
        return carry

    lax.fori_loop(0, tm, issue, 0, unroll=DMA_UNROLL)
    for s in range(2):
        pltpu.make_async_copy(ys_ref.at[pl.ds(0, tm), :], buf_ref.at[s], sem.at[s]).wait()

    route = route_ref[...]
    y = x_ref[...] + (route[:, 2:3] * buf_ref[0] + route[:, 3:4] * buf_ref[1])
    if final_norm:
        y = y * lax.rsqrt(jnp.mean(y * y, axis=-1, keepdims=True) + RMS_EPS) * nf_ref[...]
    o_ref[...] = y


def _combine(x1, route, ys, d1, d2, nf, final_norm):
    t, d = x1.shape
    grid_spec = pltpu.PrefetchScalarGridSpec(
        num_scalar_prefetch=2,
        grid=(t // TC,),
        in_specs=[
            pl.BlockSpec((TC, d), lambda i, d1, d2: (i, 0)),
            pl.BlockSpec((TC, LANES), lambda i, d1, d2: (i, 0)),
            pl.BlockSpec((1, d), lambda i, d1, d2: (0, 0)),
            pl.BlockSpec(memory_space=pl.ANY),
        ],
        out_specs=pl.BlockSpec((TC, d), lambda i, d1, d2: (i, 0)),
        scratch_shapes=[pltpu.VMEM((2, TC, d), F32), pltpu.SemaphoreType.DMA((2,))],
    )
    return pl.pallas_call(
        functools.partial(_combine_kernel, final_norm=final_norm),
        grid_spec=grid_spec,
        out_shape=jax.ShapeDtypeStruct((t, d), F32),
        compiler_params=_cparams(("arbitrary",)),
    )(d1, d2, x1, route, nf.reshape(1, d).astype(F32), ys)


def _moe(x1, hn, route, route_t, cnt, w_gate, w_up, w_down, layer, nf, final_norm):
    t, d = x1.shape
    n_experts = w_gate.shape[1]
    ids = jnp.arange(n_experts, dtype=jnp.int32)
    e1, e2, rank1, rank2 = (route_t[f].astype(jnp.int32) for f in (0, 1, 4, 5))
    counts = cnt[0, :n_experts].astype(jnp.int32)
    padded = ((counts + ROW_BLOCK - 1) // ROW_BLOCK) * ROW_BLOCK
    pad_end = jnp.cumsum(padded)
    pad_start = pad_end - padded

    def dest(e, rank):
        return jnp.sum(jnp.where(e[:, None] == ids[None, :], pad_start[None, :], 0), axis=1) + rank

    d1 = dest(e1, rank1)
    d2 = dest(e2, rank2)
    nb = (t * TOP_K_INNER) // ROW_BLOCK + n_experts
    n_used = (pad_end[-1:] // ROW_BLOCK).astype(jnp.int32)
    first_row = jnp.arange(nb, dtype=jnp.int32) * ROW_BLOCK
    block_e = jnp.sum((pad_end[None, :] <= first_row[:, None]).astype(jnp.int32), axis=1)
    block_e = jnp.minimum(block_e, n_experts - 1)
    zstart = jnp.maximum(pad_end - ROW_BLOCK, 0).astype(jnp.int32)
    zflag = (padded > 0).astype(jnp.int32)
    xs = _dispatch(hn, d1, d2, zstart, zflag, n_used, nb * ROW_BLOCK)
    ys = _experts(xs, block_e, n_used, w_gate, w_up, w_down, layer)
    return _combine(x1, route, ys, d1, d2, nf, final_norm)


def _diff_lambda_init(layer_idx):
    return 0.8 - 0.6 * math.exp(-0.3 * layer_idx)


def _rope_tables(seq, dk):
    angle = 1.0 / (10000.0 ** jnp.linspace(0.0, 1.0, dk // 2, dtype=F32))
    angle = jnp.repeat(angle, 2)
    phase = jnp.arange(seq, dtype=F32)[:, None] * angle[None, :]
    sin, cos = jnp.sin(phase), jnp.cos(phase)
    even = (jnp.arange(dk) % 2 == 0)[None, :]
    return cos, jnp.where(even, -sin, 0.0), jnp.where(even, 0.0, sin)


def kernel(x, rel_bias_table, norm_mix, norm_ffn, norm_final, diff_w_in, diff_lambda_qk, diff_subln,
           diff_w_out, ret_w_in, ret_w_out, moe_w_group, moe_b_group, moe_w_expert, moe_b_expert,
           moe_w_gate, moe_w_up, moe_w_down):
    batch, seq, d = x.shape
    depth = norm_mix.shape[0]
    t = batch * seq
    heads = rel_bias_table.shape[1]
    dh = diff_lambda_qk.shape[-1]
    qk_dim = (ret_w_in.shape[2] - 2 * ret_w_out.shape[1]) // 2
    v_dim = ret_w_out.shape[1]
    dk = qk_dim // RET_HEADS
    xt = x.reshape(t, d)
    for i in range(depth):
        j = i // N_MIXERS
        if i % N_MIXERS == 0:
            col_scale = jnp.concatenate([jnp.full((d,), dh ** -0.5, F32), jnp.ones((2 * d,), F32)])
            w_in = (diff_w_in[j] * col_scale[None, :]).astype(BF16)
            qkv = _norm_matmul(xt, norm_mix[i], w_in)
            mix = _diff_attention(qkv, rel_bias_table, diff_lambda_qk[j], diff_subln[j],
                                  _diff_lambda_init(i), batch, seq, heads)
            gate = None
            w_out = diff_w_out[j].astype(BF16)
        else:
            n_in = ret_w_in.shape[2]
            col_scale = jnp.concatenate([jnp.ones((qk_dim,), F32), jnp.full((qk_dim,), dk ** -0.5, F32),
                                         jnp.ones((n_in - 2 * qk_dim,), F32)])
            w_in = (ret_w_in[j] * col_scale[None, :]).astype(BF16)
            qkvg = _norm_matmul(xt, norm_mix[i], w_in, rope=_rope_tables(seq, dk),
                                n_rope=2 * qk_dim // TN, seq=seq)
            mix = _retention(qkvg, batch, seq, qk_dim, v_dim)
            gate = (qkvg, (2 * qk_dim + v_dim) // v_dim)
            w_out = ret_w_out[j].astype(BF16)
        x1, hn, route, route_t, cnt = _out_router(mix, gate, w_out, xt, norm_ffn[i], moe_w_group[i],
                                                  moe_b_group[i], moe_w_expert[i], moe_b_expert[i])
        xt = _moe(x1, hn, route, route_t, cnt, moe_w_gate, moe_w_up, moe_w_down, i,
                  norm_final, i == depth - 1)
    return xt.reshape(batch, seq, d)
```

```python
import functools
import math

import jax
import jax.numpy as jnp
from jax import lax
from jax.experimental import pallas as pl
from jax.experimental.pallas import tpu as pltpu

F32 = jnp.float32
BF16 = jnp.bfloat16

N_MIXERS = 2
NUM_BUCKETS = 32
MAX_DISTANCE = 128
RET_HEADS = 4
TOP_K_INNER = 2
RMS_EPS = 1e-6
SUBLN_EPS = 1e-5

LANES = 128
SUBLANES = 8
BF16_SUBLANES = 16
LOG2E = 1.4426950408889634
VMEM_LIMIT = 48 * 1024 * 1024

TM = 512
TN = 1024
TQ = 256
ATTN_HEADS = 4
RET_C = 256
ROW_BLOCK = 256
TC = 256
DMA_UNROLL = 8


def _cparams(sem):
    return pltpu.CompilerParams(dimension_semantics=sem, vmem_limit_bytes=VMEM_LIMIT)


def _norm_matmul_kernel(x_ref, g_ref, w_ref, *rest, n_rope):
    if n_rope:
        cos_ref, sa_ref, sb_ref, o_ref = rest
    else:
        (o_ref,) = rest
    x = x_ref[...]
    xn = (x * lax.rsqrt(jnp.mean(x * x, axis=-1, keepdims=True) + RMS_EPS) * g_ref[...]).astype(BF16)
    for j in range(o_ref.shape[1] // TN):
        cols = slice(j * TN, (j + 1) * TN)
        acc = jnp.dot(xn, w_ref[:, cols], preferred_element_type=F32)
        if j < n_rope:
            reps = TN // cos_ref.shape[1]
            c = jnp.tile(cos_ref[...], (1, reps))
            sa = jnp.tile(sa_ref[...], (1, reps))
            sb = jnp.tile(sb_ref[...], (1, reps))
            nxt = pltpu.roll(acc, TN - 1, axis=1)
            prv = pltpu.roll(acc, 1, axis=1)
            acc = acc * c + nxt * sa + prv * sb
        o_ref[:, cols] = acc.astype(o_ref.dtype)


def _norm_matmul(x, g, w, rope=None, n_rope=0, seq=None):
    t, d = x.shape
    n = w.shape[1]
    in_specs = [
        pl.BlockSpec((TM, d), lambda i: (i, 0)),
        pl.BlockSpec((1, d), lambda i: (0, 0)),
        pl.BlockSpec((d, n), lambda i: (0, 0), pipeline_mode=pl.Buffered(1)),
    ]
    args = [x, g.reshape(1, d), w]
    if rope is not None:
        nb = seq // TM
        dk = rope[0].shape[1]
        in_specs += [pl.BlockSpec((TM, dk), lambda i: (i % nb, 0))] * 3
        args += list(rope)
    return pl.pallas_call(
        functools.partial(_norm_matmul_kernel, n_rope=n_rope),
        grid=(t // TM,),
        in_specs=in_specs,
        out_specs=pl.BlockSpec((TM, n), lambda i: (i, 0)),
        out_shape=jax.ShapeDtypeStruct((t, n), BF16),
        compiler_params=_cparams(("parallel",)),
    )(*args)


def _diff_attn_kernel(q_ref, k_ref, v_ref, bias_ref, lam_ref, g_ref, o_ref,
                      vt_ref, m_ref, acc_ref, s_ref, *, lambda_init):
    seq = q_ref.shape[0]
    hp, _, _, tq = bias_ref.shape
    dv = q_ref.shape[1] // hp
    dh = dv // 2
    nq = seq // tq

    def head(hh):
        return slice(hh * dv, (hh + 1) * dv)

    extra = vt_ref.shape[2] - dv
    ones_row = jnp.where(lax.broadcasted_iota(jnp.int32, (extra, tq), 0) == 0, 1.0, 0.0).astype(BF16)
    for hh in range(hp):
        for j in range(nq):
            vt_ref[hh, j, :dv, :] = v_ref[j * tq:(j + 1) * tq, head(hh)].astype(F32).T.astype(BF16)
            vt_ref[hh, j, dv:, :] = ones_row

    lq = lam_ref[...]
    lam = (jnp.exp(jnp.sum(lq[0:1] * lq[1:2], axis=-1, keepdims=True))
           - jnp.exp(jnp.sum(lq[2:3] * lq[3:4], axis=-1, keepdims=True)) + lambda_init)
    lane = lax.broadcasted_iota(jnp.int32, (tq, dv), 1)

    def q_body(qi, carry):
        qrows = pl.ds(pl.multiple_of(qi * tq, tq), tq)
        qs = []
        for hh in range(hp):
            q = q_ref[qrows, head(hh)]
            zero = jnp.zeros_like(q)
            qs.append(jnp.concatenate([jnp.where(lane < dh, q, zero), jnp.where(lane >= dh, q, zero)],
                                      axis=0))

        m_ref[...] = jnp.full(m_ref.shape, -jnp.inf, F32)
        acc_ref[...] = jnp.zeros(acc_ref.shape, F32)

        def score_dots(j):
            rows = pl.ds(pl.multiple_of(j * tq, tq), tq)
            for hh in range(hp):
                s_ref[hh] = lax.dot_general(k_ref[rows, head(hh)], qs[hh], (((1,), (1,)), ((), ())),
                                            preferred_element_type=F32)

        def step(j, tile, prefetch):
            probs, alphas = [], []
            for hh in range(hp):
                s = s_ref[hh]
                if tile is not None:
                    bias = bias_ref[hh, tile]
                    s = s + jnp.concatenate([bias, bias], axis=1)
                m_old = m_ref[hh]
                m_new = jnp.maximum(m_old, jnp.max(s, axis=0, keepdims=True))
                alpha = jnp.exp2(m_old - m_new)
                m_ref[hh] = m_new
                probs.append(jnp.exp2(s - m_new).astype(BF16))
                alphas.append(alpha)
            if prefetch:
                score_dots(j + 1)
            for hh in range(hp):
                acc_ref[hh] = alphas[hh] * acc_ref[hh] + jnp.dot(vt_ref[hh, j], probs[hh],
                                                                 preferred_element_type=F32)

        score_dots(0)

        def far_body(j, c):
            step(j, None, True)
            return c

        lax.fori_loop(0, jnp.maximum(qi - 1, 0), far_body, 0)

        @pl.when(qi >= 1)
        def _():
            step(qi - 1, 1, True)

        step(qi, 0, False)

        for hh in range(hp):
            acc = acc_ref[hh]
            on = acc[:dv] / acc[dv:dv + 1]
            ot = on[:, :tq] - lam * on[:, tq:]
            ot = ot * lax.rsqrt(jnp.mean(ot * ot, axis=0, keepdims=True) + SUBLN_EPS)
            o_ref[qrows, head(hh)] = (ot.T * g_ref[...] * (1.0 - lambda_init)).astype(o_ref.dtype)
        return carry

    lax.fori_loop(0, nq, q_body, 0)


def _t5_bucket(rel):
    max_exact = NUM_BUCKETS // 2
    n = jnp.maximum(rel, 0)
    is_small = n < max_exact
    nf = jnp.maximum(n, max_exact).astype(F32)
    large = max_exact + (jnp.log(nf / max_exact) / math.log(MAX_DISTANCE / max_exact)
                         * (NUM_BUCKETS - max_exact)).astype(jnp.int32)
    large = jnp.minimum(large, NUM_BUCKETS - 1)
    return jnp.where(is_small, n, large)


def _bias_tiles(rel_table):
    heads = rel_table.shape[1]
    span = 2 * TQ + 1
    bucket = _t5_bucket(jnp.arange(span))
    onehot = (bucket[:, None] == jnp.arange(NUM_BUCKETS)[None, :]).astype(F32)
    table = (rel_table.astype(F32) - rel_table[NUM_BUCKETS - 1].astype(F32)[None, :]) * LOG2E
    by_rel = jnp.dot(onehot, table, precision=lax.Precision.HIGHEST).T
    toep = jnp.tile(by_rel, (1, TQ))[:, :TQ * (span - 1)].reshape(heads, TQ, span - 1)
    kj = jnp.arange(TQ)[:, None]
    qi = jnp.arange(TQ)[None, :]
    diag = jnp.where((qi >= kj)[None], toep[:, :, :TQ], -jnp.inf)
    return jnp.stack([diag, toep[:, :, TQ:]], axis=1)


def _diff_attention(qkv, rel_table, lam_qk, subln_g, lambda_init, batch, seq, heads):
    t = qkv.shape[0]
    d = qkv.shape[1] // 3
    dv = d // heads
    hp = ATTN_HEADS
    assert TQ + 1 >= MAX_DISTANCE and seq % TQ == 0 and dv == LANES and heads % hp == 0
    nq = seq // TQ
    groups = heads // hp
    bias = _bias_tiles(rel_table)
    return pl.pallas_call(
        functools.partial(_diff_attn_kernel, lambda_init=lambda_init),
        grid=(batch, groups),
        in_specs=[
            pl.BlockSpec((seq, hp * dv), lambda b, g: (b, g)),
            pl.BlockSpec((seq, hp * dv), lambda b, g: (b, groups + g)),
            pl.BlockSpec((seq, hp * dv), lambda b, g: (b, 2 * groups + g)),
            pl.BlockSpec((hp, 2, TQ, TQ), lambda b, g: (g, 0, 0, 0)),
            pl.BlockSpec(lam_qk.shape, lambda b, g: (0, 0)),
            pl.BlockSpec((1, dv), lambda b, g: (0, 0)),
        ],
        out_specs=pl.BlockSpec((seq, hp * dv), lambda b, g: (b, g)),
        out_shape=jax.ShapeDtypeStruct((t, d), BF16),
        scratch_shapes=[
            pltpu.VMEM((hp, nq, dv + BF16_SUBLANES, TQ), BF16),
            pltpu.VMEM((hp, 1, 2 * TQ), F32),
            pltpu.VMEM((hp, dv + BF16_SUBLANES, 2 * TQ), F32),
            pltpu.VMEM((hp, TQ, 2 * TQ), F32),
        ],
        compiler_params=_cparams(("parallel", "parallel")),
    )(qkv, qkv, qkv, bias, lam_qk.astype(F32), subln_g.reshape(1, dv).astype(F32))


def _retention_kernel(q_ref, k_ref, v_ref, dec_ref, tab_ref, o_ref, state_ref):
    seq = q_ref.shape[0]
    c = dec_ref.shape[1]
    dec = dec_ref[0]
    tab = tab_ref[0]
    qd = tab[:, 0:1]
    kd = tab[:, 1:2]
    cd = tab[0:1, 2:3]
    state_ref[...] = jnp.zeros(state_ref.shape, F32)

    def body(n, carry):
        rows = pl.ds(pl.multiple_of(n * c, c), c)
        q = q_ref[rows, :]
        k = k_ref[rows, :]
        v = v_ref[rows, :]
        s = lax.dot_general(q, k, (((1,), (1,)), ((), ())), preferred_element_type=F32) * dec
        inner = jnp.dot(s.astype(BF16), v, preferred_element_type=F32)
        st = state_ref[...]
        cross = jnp.dot(q, st.astype(BF16), preferred_element_type=F32) * qd
        kt = (k.astype(F32) * kd).T.astype(BF16)
        state_ref[...] = st * cd + jnp.dot(kt, v, preferred_element_type=F32)
        o = inner + cross
        o = o * lax.rsqrt(jnp.mean(o * o, axis=-1, keepdims=True) + RMS_EPS)
        o_ref[rows, :] = o.astype(o_ref.dtype)
        return carry

    lax.fori_loop(0, seq // c, body, 0)


def _retention_tables(heads):
    c = RET_C
    log_gamma = jnp.log(1.0 - 2.0 ** (-5.0 - jnp.arange(heads, dtype=F32)))
    pos = jnp.arange(c, dtype=F32)
    dist = pos[:, None] - pos[None, :]
    inner_decay = jnp.where(dist[None] >= 0,
                            jnp.exp(jnp.maximum(dist, 0.0)[None] * log_gamma[:, None, None]), 0.0)
    q_decay = jnp.exp((pos + 1.0)[None] * log_gamma[:, None])
    k_decay = jnp.exp((c - 1.0 - pos)[None] * log_gamma[:, None])
    chunk_decay = jnp.broadcast_to(jnp.exp(c * log_gamma)[:, None], (heads, c))
    tab = jnp.stack([q_decay, k_decay, chunk_decay], axis=-1)
    tab = jnp.pad(tab, ((0, 0), (0, 0), (0, LANES - 3)))
    return inner_decay, tab


def _retention(qkvg, batch, seq, qk_dim, v_dim):
    t = qkvg.shape[0]
    dk = qk_dim // RET_HEADS
    dvh = v_dim // RET_HEADS
    inner_decay, tab = _retention_tables(RET_HEADS)
    kb = qk_dim // dk
    vb = 2 * qk_dim // dvh
    return pl.pallas_call(
        _retention_kernel,
        grid=(batch, RET_HEADS),
        in_specs=[
            pl.BlockSpec((seq, dk), lambda b, h: (b, h)),
            pl.BlockSpec((seq, dk), lambda b, h: (b, kb + h)),
            pl.BlockSpec((seq, dvh), lambda b, h: (b, vb + h)),
            pl.BlockSpec((1, RET_C, RET_C), lambda b, h: (h, 0, 0)),
            pl.BlockSpec((1, RET_C, LANES), lambda b, h: (h, 0, 0)),
        ],
        out_specs=pl.BlockSpec((seq, dvh), lambda b, h: (b, h)),
        out_shape=jax.ShapeDtypeStruct((t, v_dim), BF16),
        scratch_shapes=[pltpu.VMEM((dk, dvh), F32)],
        compiler_params=_cparams(("parallel", "arbitrary")),
    )(qkvg, qkvg, qkvg, inner_decay, tab)


def _out_router_kernel(*refs, gated, n_groups, n_experts):
    if gated:
        a_ref, gate_ref, w_ref, x_ref, nf_ref, wr_ref, br_ref = refs[:7]
        x1_ref, hn_ref, route_ref, route_t_ref, cnt_ref, carry_ref = refs[7:]
        gate = gate_ref[...]
        a = gate * jax.nn.sigmoid(gate) * a_ref[...]
    else:
        a_ref, w_ref, x_ref, nf_ref, wr_ref, br_ref = refs[:6]
        x1_ref, hn_ref, route_ref, route_t_ref, cnt_ref, carry_ref = refs[6:]
        a = a_ref[...]
    epg = n_experts // n_groups

    @pl.when(pl.program_id(0) == 0)
    def _():
        carry_ref[...] = jnp.zeros(carry_ref.shape, F32)

    x1 = x_ref[...] + jnp.dot(a, w_ref[...], preferred_element_type=F32)
    x1_ref[...] = x1
    hn = x1 * lax.rsqrt(jnp.mean(x1 * x1, axis=-1, keepdims=True) + RMS_EPS) * nf_ref[...]
    hn_ref[...] = hn

    hn_hi = hn.astype(BF16)
    hn_lo = (hn - hn_hi.astype(F32)).astype(BF16)
    logits = (jnp.dot(hn_hi, wr_ref[0], preferred_element_type=F32)
              + jnp.dot(hn_lo, wr_ref[0], preferred_element_type=F32)
              + jnp.dot(hn_hi, wr_ref[1], preferred_element_type=F32)) + br_ref[...]
    tm = logits.shape[0]

    n_rows = -(-(n_groups + n_experts) // SUBLANES) * SUBLANES
    lt = logits.T[:n_rows, :]
    row = lax.broadcasted_iota(jnp.int32, lt.shape, 0)
    big = jnp.int32(LANES)
    neg = jnp.float32(-jnp.inf)

    gl = jnp.where(row < n_groups, lt, neg)
    gmax = jnp.max(gl, axis=0, keepdims=True)
    g_p = 1.0 / jnp.sum(jnp.exp(gl - gmax), axis=0, keepdims=True)
    g_idx = jnp.min(jnp.where(gl == gmax, row, big), axis=0, keepdims=True)

    lo = n_groups + g_idx * epg
    el = jnp.where((row >= lo) & (row < lo + epg), lt, neg)
    v1 = jnp.max(el, axis=0, keepdims=True)
    i1 = jnp.min(jnp.where(el == v1, row, big), axis=0, keepdims=True)
    el2 = jnp.where(row == i1, neg, el)
    v2 = jnp.max(el2, axis=0, keepdims=True)
    i2 = jnp.min(jnp.where(el2 == v2, row, big), axis=0, keepdims=True)
    tt = jnp.exp(v2 - v1)
    w1 = g_p / (1.0 + tt)
    w2 = g_p * tt / (1.0 + tt)
    e1 = i1 - n_groups
    e2 = i2 - n_groups

    erow = lax.broadcasted_iota(jnp.int32, (n_experts, tm), 0)
    oh1 = erow == e1
    oh2 = erow == e2
    onehot = jnp.where(oh1 | oh2, 1.0, 0.0)
    before = lax.broadcasted_iota(jnp.int32, (tm, tm), 0) < lax.broadcasted_iota(jnp.int32, (tm, tm), 1)
    tri = jnp.where(before, 1.0, 0.0).astype(BF16)
    carry = carry_ref[...]
    base = jnp.dot(onehot.astype(BF16), tri, preferred_element_type=F32) + carry[:, 0:1]
    rank1 = jnp.sum(jnp.where(oh1, base, 0.0), axis=0, keepdims=True)
    rank2 = jnp.sum(jnp.where(oh2, base, 0.0), axis=0, keepdims=True)
    carry = carry + jnp.sum(onehot, axis=1, keepdims=True)
    carry_ref[...] = carry
    cnt_ref[...] = carry

    frow = lax.broadcasted_iota(jnp.int32, (LANES, tm), 0)
    out = jnp.zeros((LANES, tm), F32)
    for idx, val in enumerate((e1.astype(F32), e2.astype(F32), w1, w2, rank1, rank2)):
        out = jnp.where(frow == idx, val, out)
    route_t_ref[...] = out[:route_t_ref.shape[0], :]
    route_ref[...] = out.T


def _out_router(a, gate, w, x, nf, w_group, b_group, w_expert, b_expert):
    t, d = x.shape
    k = w.shape[0]
    n_groups = w_group.shape[1]
    n_experts = w_expert.shape[1]
    assert n_groups + n_experts <= LANES
    pad = LANES - n_groups - n_experts
    wr = jnp.pad(jnp.concatenate([w_group, w_expert], axis=1).astype(F32), ((0, 0), (0, pad)))
    wr_hi = wr.astype(BF16)
    wr = jnp.stack([wr_hi, (wr - wr_hi.astype(F32)).astype(BF16)])
    br = jnp.pad(jnp.concatenate([b_group, b_expert]).astype(F32), (0, pad)).reshape(1, LANES)
    gated = gate is not None
    row = lambda i: (i, 0)
    fixed = lambda i: (0, 0)
    in_specs = [pl.BlockSpec((TM, k), row)]
    args = [a]
    if gated:
        gate_arr, gate_col = gate
        in_specs.append(pl.BlockSpec((TM, k), lambda i: (i, gate_col)))
        args.append(gate_arr)
    in_specs += [
        pl.BlockSpec((k, d), fixed),
        pl.BlockSpec((TM, d), row),
        pl.BlockSpec((1, d), fixed),
        pl.BlockSpec((2, d, LANES), lambda i: (0, 0, 0)),
        pl.BlockSpec((1, LANES), fixed),
    ]
    args += [w, x, nf.reshape(1, d), wr, br]
    return pl.pallas_call(
        functools.partial(_out_router_kernel, gated=gated, n_groups=n_groups, n_experts=n_experts),
        grid=(t // TM,),
        in_specs=in_specs,
        out_specs=[
            pl.BlockSpec((TM, d), row),
            pl.BlockSpec((TM, d), row),
            pl.BlockSpec((TM, LANES), row),
            pl.BlockSpec((SUBLANES, TM), lambda i: (0, i)),
            pl.BlockSpec((n_experts, LANES), fixed),
        ],
        out_shape=[
            jax.ShapeDtypeStruct((t, d), F32),
            jax.ShapeDtypeStruct((t, d), F32),
            jax.ShapeDtypeStruct((t, LANES), F32),
            jax.ShapeDtypeStruct((SUBLANES, t), F32),
            jax.ShapeDtypeStruct((n_experts, LANES), F32),
        ],
        scratch_shapes=[pltpu.VMEM((n_experts, LANES), F32)],
        compiler_params=_cparams(("arbitrary",)),
    )(*args)


def _dispatch_kernel(d1_ref, d2_ref, zs_ref, zf_ref, nu_ref, hn_ref, xs_ref, zero_ref, sem, zsem):
    tm = hn_ref.shape[0]
    rb = zero_ref.shape[0]
    nb = xs_ref.shape[0] // rb
    base = pl.program_id(0) * tm

    @pl.when(pl.program_id(0) == 0)
    def _():
        zero_ref[...] = jnp.zeros(zero_ref.shape, zero_ref.dtype)

        def zcopy(row0):
            rows = pl.ds(pl.multiple_of(row0, rb), rb)
            return pltpu.make_async_copy(zero_ref, xs_ref.at[rows, :], zsem)

        def zissue(e, carry):
            @pl.when(zf_ref[e] > 0)
            def _():
                zcopy(zs_ref[e]).start()
            return carry

        def zwait(e, carry):
            @pl.when(zf_ref[e] > 0)
            def _():
                zcopy(zs_ref[e]).wait()
            return carry

        def tissue(b, carry):
            zcopy(b * rb).start()
            return carry

        def twait(b, carry):
            zcopy(b * rb).wait()
            return carry

        lax.fori_loop(0, zs_ref.shape[0], zissue, 0)
        lax.fori_loop(nu_ref[0], nb, tissue, 0)
        lax.fori_loop(0, zs_ref.shape[0], zwait, 0)
        lax.fori_loop(nu_ref[0], nb, twait, 0)

    def issue(r, carry):
        src = hn_ref.at[pl.ds(r, 1), :]
        pltpu.make_async_copy(src, xs_ref.at[pl.ds(d1_ref[base + r], 1), :], sem.at[0]).start()
        pltpu.make_async_copy(src, xs_ref.at[pl.ds(d2_ref[base + r], 1), :], sem.at[1]).start()
        return carry

    lax.fori_loop(0, tm, issue, 0, unroll=DMA_UNROLL)
    for s in range(2):
        pltpu.make_async_copy(hn_ref, xs_ref.at[pl.ds(0, tm), :], sem.at[s]).wait()


def _dispatch(hn, d1, d2, zstart, zflag, n_used, n_rows):
    t, d = hn.shape
    grid_spec = pltpu.PrefetchScalarGridSpec(
        num_scalar_prefetch=5,
        grid=(t // TC,),
        in_specs=[pl.BlockSpec((TC, d), lambda i, *_: (i, 0))],
        out_specs=pl.BlockSpec(memory_space=pl.ANY),
        scratch_shapes=[pltpu.VMEM((ROW_BLOCK, d), hn.dtype),
                        pltpu.SemaphoreType.DMA((2,)), pltpu.SemaphoreType.DMA(())],
    )
    return pl.pallas_call(
        _dispatch_kernel,
        grid_spec=grid_spec,
        out_shape=jax.ShapeDtypeStruct((n_rows, d), hn.dtype),
        compiler_params=_cparams(("arbitrary",)),
    )(d1, d2, zstart, zflag, n_used, hn)


def _expert_kernel(be_ref, nu_ref, x_ref, wg_ref, wu_ref, wd_ref, y_ref, wgb_ref, wub_ref, wdb_ref):
    i = pl.program_id(0)
    used = i < nu_ref[0]
    new_expert = (i == 0) | (be_ref[i] != be_ref[jnp.maximum(i - 1, 0)])

    @pl.when(used & new_expert)
    def _():
        wgb_ref[...] = wg_ref[0, 0].astype(BF16)
        wub_ref[...] = wu_ref[0, 0].astype(BF16)
        wdb_ref[...] = wd_ref[0, 0].astype(BF16)

    @pl.when(used)
    def _():
        x = x_ref[...].astype(BF16)
        g = jnp.dot(x, wgb_ref[...], preferred_element_type=F32)
        u = jnp.dot(x, wub_ref[...], preferred_element_type=F32)
        hmid = (g * jax.nn.sigmoid(g) * u).astype(BF16)
        y_ref[...] = jnp.dot(hmid, wdb_ref[...], preferred_element_type=F32)

    @pl.when(jnp.logical_not(used))
    def _():
        y_ref[...] = jnp.zeros(y_ref.shape, y_ref.dtype)


def _experts(xs, block_e, n_used, w_gate, w_up, w_down, layer):
    n_rows, d = xs.shape
    ff = w_gate.shape[3]
    nb = n_rows // ROW_BLOCK
    last = lambda i, nu: jnp.maximum(jnp.minimum(i, nu[0] - 1), 0)
    blk = lambda i, be, nu: (last(i, nu), 0)
    wsel = lambda i, be, nu: (layer, be[last(i, nu)], 0, 0)
    grid_spec = pltpu.PrefetchScalarGridSpec(
        num_scalar_prefetch=2,
        grid=(nb,),
        in_specs=[
            pl.BlockSpec((ROW_BLOCK, d), blk),
            pl.BlockSpec((1, 1, d, ff), wsel),
            pl.BlockSpec((1, 1, d, ff), wsel),
            pl.BlockSpec((1, 1, ff, d), wsel),
        ],
        out_specs=pl.BlockSpec((ROW_BLOCK, d), lambda i, be, nu: (i, 0)),
        scratch_shapes=[pltpu.VMEM((d, ff), BF16), pltpu.VMEM((d, ff), BF16), pltpu.VMEM((ff, d), BF16)],
    )
    return pl.pallas_call(
        _expert_kernel,
        grid_spec=grid_spec,
        out_shape=jax.ShapeDtypeStruct((n_rows, d), F32),
        compiler_params=_cparams(("arbitrary",)),
    )(block_e, n_used, xs, w_gate, w_up, w_down)


def _combine_kernel(d1_ref, d2_ref, x_ref, route_ref, nf_ref, ys_ref, o_ref, buf_ref, sem, *, final_norm):
    tm = x_ref.shape[0]
    base = pl.program_id(0) * tm

    def issue(r, carry):
        pltpu.make_async_copy(ys_ref.at[pl.ds(d1_ref[base + r], 1), :],
                              buf_ref.at[0, pl.ds(r, 1), :], sem.at[0]).start()
        pltpu.make_async_copy(ys_ref.at[pl.ds(d2_ref[base + r], 1), :],
                              buf_ref.at[1, pl.ds(r, 1), :], sem.at[1]).start()
        return carry

    lax.fori_loop(0, tm, issue, 0, unroll=DMA_UNROLL)
    for s in range(2):
        pltpu.make_async_copy(ys_ref.at[pl.ds(0, tm), :], buf_ref.at[s], sem.at[s]).wait()

    route = route_ref[...]
    y = x_ref[...] + (route[:, 2:3] * buf_ref[0] + route[:, 3:4] * buf_ref[1])
    if final_norm:
        y = y * lax.rsqrt(jnp.mean(y * y, axis=-1, keepdims=True) + RMS_EPS) * nf_ref[...]
    o_ref[...] = y


def _combine(x1, route, ys, d1, d2, nf, final_norm):
    t, d = x1.shape
    grid_spec = pltpu.PrefetchScalarGridSpec(
        num_scalar_prefetch=2,
        grid=(t // TC,),
        in_specs=[
            pl.BlockSpec((TC, d), lambda i, d1, d2: (i, 0)),
            pl.BlockSpec((TC, LANES), lambda i, d1, d2: (i, 0)),
            pl.BlockSpec((1, d), lambda i, d1, d2: (0, 0)),
            pl.BlockSpec(memory_space=pl.ANY),
        ],
        out_specs=pl.BlockSpec((TC, d), lambda i, d1, d2: (i, 0)),
        scratch_shapes=[pltpu.VMEM((2, TC, d), F32), pltpu.SemaphoreType.DMA((2,))],
    )
    return pl.pallas_call(
        functools.partial(_combine_kernel, final_norm=final_norm),
        grid_spec=grid_spec,
        out_shape=jax.ShapeDtypeStruct((t, d), F32),
        compiler_params=_cparams(("arbitrary",)),
    )(d1, d2, x1, route, nf.reshape(1, d).astype(F32), ys)


def _moe(x1, hn, route, route_t, cnt, w_gate, w_up, w_down, layer, nf, final_norm):
    t, d = x1.shape
    n_experts = w_gate.shape[1]
    ids = jnp.arange(n_experts, dtype=jnp.int32)
    e1, e2, rank1, rank2 = (route_t[f].astype(jnp.int32) for f in (0, 1, 4, 5))
    counts = cnt[:, 0].astype(jnp.int32)
    padded = ((counts + ROW_BLOCK - 1) // ROW_BLOCK) * ROW_BLOCK
    pad_end = jnp.cumsum(padded)
    pad_start = pad_end - padded

    def dest(e, rank):
        return jnp.sum(jnp.where(e[:, None] == ids[None, :], pad_start[None, :], 0), axis=1) + rank

    d1 = dest(e1, rank1)
    d2 = dest(e2, rank2)
    nb = (t * TOP_K_INNER) // ROW_BLOCK + n_experts
    n_used = (pad_end[-1:] // ROW_BLOCK).astype(jnp.int32)
    first_row = jnp.arange(nb, dtype=jnp.int32) * ROW_BLOCK
    block_e = jnp.sum((pad_end[None, :] <= first_row[:, None]).astype(jnp.int32), axis=1)
    block_e = jnp.minimum(block_e, n_experts - 1)
    zstart = jnp.maximum(pad_end - ROW_BLOCK, 0).astype(jnp.int32)
    zflag = (padded > 0).astype(jnp.int32)
    xs = _dispatch(hn, d1, d2, zstart, zflag, n_used, nb * ROW_BLOCK)
    ys = _experts(xs, block_e, n_used, w_gate, w_up, w_down, layer)
    return _combine(x1, route, ys, d1, d2, nf, final_norm)


def _diff_lambda_init(layer_idx):
    return 0.8 - 0.6 * math.exp(-0.3 * layer_idx)


def _rope_tables(seq, dk):
    angle = 1.0 / (10000.0 ** jnp.linspace(0.0, 1.0, dk // 2, dtype=F32))
    angle = jnp.repeat(angle, 2)
    phase = jnp.arange(seq, dtype=F32)[:, None] * angle[None, :]
    sin, cos = jnp.sin(phase), jnp.cos(phase)
    even = (jnp.arange(dk) % 2 == 0)[None, :]
    return cos, jnp.where(even, -sin, 0.0), jnp.where(even, 0.0, sin)


def kernel(x, rel_bias_table, norm_mix, norm_ffn, norm_final, diff_w_in, diff_lambda_qk, diff_subln,
           diff_w_out, ret_w_in, ret_w_out, moe_w_group, moe_b_group, moe_w_expert, moe_b_expert,
           moe_w_gate, moe_w_up, moe_w_down):
    batch, seq, d = x.shape
    depth = norm_mix.shape[0]
    t = batch * seq
    heads = rel_bias_table.shape[1]
    dh = diff_lambda_qk.shape[-1]
    qk_dim = (ret_w_in.shape[2] - 2 * ret_w_out.shape[1]) // 2
    v_dim = ret_w_out.shape[1]
    dk = qk_dim // RET_HEADS
    xt = x.reshape(t, d)
    for i in range(depth):
        j = i // N_MIXERS
        if i % N_MIXERS == 0:
            col_scale = jnp.concatenate([jnp.full((d,), dh ** -0.5 * LOG2E, F32), jnp.ones((2 * d,), F32)])
            w_in = (diff_w_in[j] * col_scale[None, :]).astype(BF16)
            qkv = _norm_matmul(xt, norm_mix[i], w_in)
            mix = _diff_attention(qkv, rel_bias_table, diff_lambda_qk[j], diff_subln[j],
                                  _diff_lambda_init(i), batch, seq, heads)
            gate = None
            w_out = diff_w_out[j].astype(BF16)
        else:
            n_in = ret_w_in.shape[2]
            col_scale = jnp.concatenate([jnp.ones((qk_dim,), F32), jnp.full((qk_dim,), dk ** -0.5, F32),
                                         jnp.ones((n_in - 2 * qk_dim,), F32)])
            w_in = (ret_w_in[j] * col_scale[None, :]).astype(BF16)
            qkvg = _norm_matmul(xt, norm_mix[i], w_in, rope=_rope_tables(seq, dk),
                                n_rope=2 * qk_dim // TN, seq=seq)
            mix = _retention(qkvg, batch, seq, qk_dim, v_dim)
            gate = (qkvg, (2 * qk_dim + v_dim) // v_dim)
            w_out = ret_w_out[j].astype(BF16)
        x1, hn, route, route_t, cnt = _out_router(mix, gate, w_out, xt, norm_ffn[i], moe_w_group[i],
                                                  moe_b_group[i], moe_w_expert[i], moe_b_expert[i])
        xt = _moe(x1, hn, route, route_t, cnt, moe_w_gate, moe_w_up, moe_w_down, i,
                  norm_final, i == depth - 1)
    return xt.reshape(batch, seq, d)
```

```python
import functools
import math

import jax
import jax.numpy as jnp
from jax import lax
from jax.experimental import pallas as pl
from jax.experimental.pallas import tpu as pltpu

F32 = jnp.float32
BF16 = jnp.bfloat16

N_MIXERS = 2
NUM_BUCKETS = 32
MAX_DISTANCE = 128
RET_HEADS = 4
TOP_K_INNER = 2
RMS_EPS = 1e-6
SUBLN_EPS = 1e-5

LANES = 128
SUBLANES = 8
BF16_SUBLANES = 16
LOG2E = 1.4426950408889634
VMEM_LIMIT = 48 * 1024 * 1024

TM = 512
TN = 1024
TQ = 256
ATTN_HEADS = 4
RET_C = 256
ROW_BLOCK = 256
TC = 256
DMA_UNROLL = 8


def _cparams(sem):
    return pltpu.CompilerParams(dimension_semantics=sem, vmem_limit_bytes=VMEM_LIMIT)


def _norm_matmul_kernel(x_ref, g_ref, w_ref, *rest, n_rope):
    if n_rope:
        cos_ref, sa_ref, sb_ref, o_ref = rest
    else:
        (o_ref,) = rest
    x = x_ref[...]
    xn = (x * lax.rsqrt(jnp.mean(x * x, axis=-1, keepdims=True) + RMS_EPS) * g_ref[...]).astype(BF16)
    for j in range(o_ref.shape[1] // TN):
        cols = slice(j * TN, (j + 1) * TN)
        acc = jnp.dot(xn, w_ref[:, cols], preferred_element_type=F32)
        if j < n_rope:
            reps = TN // cos_ref.shape[1]
            c = jnp.tile(cos_ref[...], (1, reps))
            sa = jnp.tile(sa_ref[...], (1, reps))
            sb = jnp.tile(sb_ref[...], (1, reps))
            nxt = pltpu.roll(acc, TN - 1, axis=1)
            prv = pltpu.roll(acc, 1, axis=1)
            acc = acc * c + nxt * sa + prv * sb
        o_ref[:, cols] = acc.astype(o_ref.dtype)


def _norm_matmul(x, g, w, rope=None, n_rope=0, seq=None):
    t, d = x.shape
    n = w.shape[1]
    in_specs = [
        pl.BlockSpec((TM, d), lambda i: (i, 0)),
        pl.BlockSpec((1, d), lambda i: (0, 0)),
        pl.BlockSpec((d, n), lambda i: (0, 0), pipeline_mode=pl.Buffered(1)),
    ]
    args = [x, g.reshape(1, d), w]
    if rope is not None:
        nb = seq // TM
        dk = rope[0].shape[1]
        in_specs += [pl.BlockSpec((TM, dk), lambda i: (i % nb, 0))] * 3
        args += list(rope)
    return pl.pallas_call(
        functools.partial(_norm_matmul_kernel, n_rope=n_rope),
        grid=(t // TM,),
        in_specs=in_specs,
        out_specs=pl.BlockSpec((TM, n), lambda i: (i, 0)),
        out_shape=jax.ShapeDtypeStruct((t, n), BF16),
        compiler_params=_cparams(("parallel",)),
    )(*args)


def _diff_attn_kernel(q_ref, k_ref, v_ref, bias_ref, lam_ref, g_ref, o_ref,
                      vt_ref, m_ref, acc_ref, s_ref, *, lambda_init):
    seq = q_ref.shape[0]
    hp, _, _, tq = bias_ref.shape
    dv = q_ref.shape[1] // hp
    dh = dv // 2
    nq = seq // tq

    def head(hh):
        return slice(hh * dv, (hh + 1) * dv)

    extra = vt_ref.shape[2] - dv
    ones_row = jnp.where(lax.broadcasted_iota(jnp.int32, (extra, tq), 0) == 0, 1.0, 0.0).astype(BF16)
    for hh in range(hp):
        for j in range(nq):
            vt_ref[hh, j, :dv, :] = v_ref[j * tq:(j + 1) * tq, head(hh)].astype(F32).T.astype(BF16)
            vt_ref[hh, j, dv:, :] = ones_row

    lq = lam_ref[...]
    lam = (jnp.exp(jnp.sum(lq[0:1] * lq[1:2], axis=-1, keepdims=True))
           - jnp.exp(jnp.sum(lq[2:3] * lq[3:4], axis=-1, keepdims=True)) + lambda_init)
    lane = lax.broadcasted_iota(jnp.int32, (tq, dv), 1)

    def q_body(qi, carry):
        qrows = pl.ds(pl.multiple_of(qi * tq, tq), tq)
        qs = []
        for hh in range(hp):
            q = q_ref[qrows, head(hh)]
            zero = jnp.zeros_like(q)
            qs.append(jnp.concatenate([jnp.where(lane < dh, q, zero), jnp.where(lane >= dh, q, zero)],
                                      axis=0))

        m_ref[...] = jnp.full(m_ref.shape, -jnp.inf, F32)
        acc_ref[...] = jnp.zeros(acc_ref.shape, F32)

        def score_dots(j):
            rows = pl.ds(pl.multiple_of(j * tq, tq), tq)
            for hh in range(hp):
                s_ref[hh] = lax.dot_general(k_ref[rows, head(hh)], qs[hh], (((1,), (1,)), ((), ())),
                                            preferred_element_type=F32)

        def step(j, tile, prefetch):
            probs, alphas = [], []
            for hh in range(hp):
                s = s_ref[hh]
                if tile is not None:
                    bias = bias_ref[hh, tile]
                    s = s + jnp.concatenate([bias, bias], axis=1)
                m_old = m_ref[hh]
                m_new = jnp.maximum(m_old, jnp.max(s, axis=0, keepdims=True))
                alpha = jnp.exp2(m_old - m_new)
                m_ref[hh] = m_new
                probs.append(jnp.exp2(s - m_new).astype(BF16))
                alphas.append(alpha)
            if prefetch:
                score_dots(j + 1)
            for hh in range(hp):
                acc_ref[hh] = alphas[hh] * acc_ref[hh] + jnp.dot(vt_ref[hh, j], probs[hh],
                                                                 preferred_element_type=F32)

        score_dots(0)

        def far_body(j, c):
            step(j, None, True)
            return c

        lax.fori_loop(0, jnp.maximum(qi - 1, 0), far_body, 0)

        @pl.when(qi >= 1)
        def _():
            step(qi - 1, 1, True)

        step(qi, 0, False)

        for hh in range(hp):
            acc = acc_ref[hh]
            on = acc[:dv] / acc[dv:dv + 1]
            ot = on[:, :tq] - lam * on[:, tq:]
            ot = ot * lax.rsqrt(jnp.mean(ot * ot, axis=0, keepdims=True) + SUBLN_EPS)
            o_ref[qrows, head(hh)] = (ot.T * g_ref[...] * (1.0 - lambda_init)).astype(o_ref.dtype)
        return carry

    lax.fori_loop(0, nq, q_body, 0)


def _t5_bucket(rel):
    max_exact = NUM_BUCKETS // 2
    n = jnp.maximum(rel, 0)
    is_small = n < max_exact
    nf = jnp.maximum(n, max_exact).astype(F32)
    large = max_exact + (jnp.log(nf / max_exact) / math.log(MAX_DISTANCE / max_exact)
                         * (NUM_BUCKETS - max_exact)).astype(jnp.int32)
    large = jnp.minimum(large, NUM_BUCKETS - 1)
    return jnp.where(is_small, n, large)


def _bias_tiles(rel_table):
    heads = rel_table.shape[1]
    span = 2 * TQ + 1
    bucket = _t5_bucket(jnp.arange(span))
    onehot = (bucket[:, None] == jnp.arange(NUM_BUCKETS)[None, :]).astype(F32)
    table = (rel_table.astype(F32) - rel_table[NUM_BUCKETS - 1].astype(F32)[None, :]) * LOG2E
    by_rel = jnp.dot(onehot, table, precision=lax.Precision.HIGHEST).T
    toep = jnp.tile(by_rel, (1, TQ))[:, :TQ * (span - 1)].reshape(heads, TQ, span - 1)
    kj = jnp.arange(TQ)[:, None]
    qi = jnp.arange(TQ)[None, :]
    diag = jnp.where((qi >= kj)[None], toep[:, :, :TQ], -jnp.inf)
    return jnp.stack([diag, toep[:, :, TQ:]], axis=1)


def _diff_attention(qkv, rel_table, lam_qk, subln_g, lambda_init, batch, seq, heads):
    t = qkv.shape[0]
    d = qkv.shape[1] // 3
    dv = d // heads
    hp = ATTN_HEADS
    assert TQ + 1 >= MAX_DISTANCE and seq % TQ == 0 and dv == LANES and heads % hp == 0
    nq = seq // TQ
    groups = heads // hp
    bias = _bias_tiles(rel_table)
    return pl.pallas_call(
        functools.partial(_diff_attn_kernel, lambda_init=lambda_init),
        grid=(batch, groups),
        in_specs=[
            pl.BlockSpec((seq, hp * dv), lambda b, g: (b, g)),
            pl.BlockSpec((seq, hp * dv), lambda b, g: (b, groups + g)),
            pl.BlockSpec((seq, hp * dv), lambda b, g: (b, 2 * groups + g)),
            pl.BlockSpec((hp, 2, TQ, TQ), lambda b, g: (g, 0, 0, 0)),
            pl.BlockSpec(lam_qk.shape, lambda b, g: (0, 0)),
            pl.BlockSpec((1, dv), lambda b, g: (0, 0)),
        ],
        out_specs=pl.BlockSpec((seq, hp * dv), lambda b, g: (b, g)),
        out_shape=jax.ShapeDtypeStruct((t, d), BF16),
        scratch_shapes=[
            pltpu.VMEM((hp, nq, dv + BF16_SUBLANES, TQ), BF16),
            pltpu.VMEM((hp, 1, 2 * TQ), F32),
            pltpu.VMEM((hp, dv + BF16_SUBLANES, 2 * TQ), F32),
            pltpu.VMEM((hp, TQ, 2 * TQ), F32),
        ],
        compiler_params=_cparams(("parallel", "parallel")),
    )(qkv, qkv, qkv, bias, lam_qk.astype(F32), subln_g.reshape(1, dv).astype(F32))


def _retention_kernel(q_ref, k_ref, v_ref, dec_ref, tab_ref, o_ref, state_ref):
    seq = q_ref.shape[0]
    c = dec_ref.shape[1]
    dec = dec_ref[0]
    tab = tab_ref[0]
    qd = tab[:, 0:1]
    kd = tab[:, 1:2]
    cd = tab[0:1, 2:3]
    state_ref[...] = jnp.zeros(state_ref.shape, F32)

    def body(n, carry):
        rows = pl.ds(pl.multiple_of(n * c, c), c)
        q = q_ref[rows, :]
        k = k_ref[rows, :]
        v = v_ref[rows, :]
        s = lax.dot_general(q, k, (((1,), (1,)), ((), ())), preferred_element_type=F32) * dec
        inner = jnp.dot(s.astype(BF16), v, preferred_element_type=F32)
        st = state_ref[...]
        cross = jnp.dot(q, st.astype(BF16), preferred_element_type=F32) * qd
        kt = (k.astype(F32) * kd).T.astype(BF16)
        state_ref[...] = st * cd + jnp.dot(kt, v, preferred_element_type=F32)
        o = inner + cross
        o = o * lax.rsqrt(jnp.mean(o * o, axis=-1, keepdims=True) + RMS_EPS)
        o_ref[rows, :] = o.astype(o_ref.dtype)
        return carry

    lax.fori_loop(0, seq // c, body, 0)


def _retention_tables(heads):
    c = RET_C
    log_gamma = jnp.log(1.0 - 2.0 ** (-5.0 - jnp.arange(heads, dtype=F32)))
    pos = jnp.arange(c, dtype=F32)
    dist = pos[:, None] - pos[None, :]
    inner_decay = jnp.where(dist[None] >= 0,
                            jnp.exp(jnp.maximum(dist, 0.0)[None] * log_gamma[:, None, None]), 0.0)
    q_decay = jnp.exp((pos + 1.0)[None] * log_gamma[:, None])
    k_decay = jnp.exp((c - 1.0 - pos)[None] * log_gamma[:, None])
    chunk_decay = jnp.broadcast_to(jnp.exp(c * log_gamma)[:, None], (heads, c))
    tab = jnp.stack([q_decay, k_decay, chunk_decay], axis=-1)
    tab = jnp.pad(tab, ((0, 0), (0, 0), (0, LANES - 3)))
    return inner_decay, tab


def _retention(qkvg, batch, seq, qk_dim, v_dim):
    t = qkvg.shape[0]
    dk = qk_dim // RET_HEADS
    dvh = v_dim // RET_HEADS
    inner_decay, tab = _retention_tables(RET_HEADS)
    kb = qk_dim // dk
    vb = 2 * qk_dim // dvh
    return pl.pallas_call(
        _retention_kernel,
        grid=(batch, RET_HEADS),
        in_specs=[
            pl.BlockSpec((seq, dk), lambda b, h: (b, h)),
            pl.BlockSpec((seq, dk), lambda b, h: (b, kb + h)),
            pl.BlockSpec((seq, dvh), lambda b, h: (b, vb + h)),
            pl.BlockSpec((1, RET_C, RET_C), lambda b, h: (h, 0, 0)),
            pl.BlockSpec((1, RET_C, LANES), lambda b, h: (h, 0, 0)),
        ],
        out_specs=pl.BlockSpec((seq, dvh), lambda b, h: (b, h)),
        out_shape=jax.ShapeDtypeStruct((t, v_dim), BF16),
        scratch_shapes=[pltpu.VMEM((dk, dvh), F32)],
        compiler_params=_cparams(("parallel", "arbitrary")),
    )(qkvg, qkvg, qkvg, inner_decay, tab)


def _out_router_kernel(*refs, gated, n_groups, n_experts):
    if gated:
        a_ref, gate_ref, w_ref, x_ref, nf_ref, wr_ref, br_ref = refs[:7]
        x1_ref, hn_ref, route_ref, route_t_ref, cnt_ref, carry_ref = refs[7:]
        gate = gate_ref[...]
        a = gate * jax.nn.sigmoid(gate) * a_ref[...]
    else:
        a_ref, w_ref, x_ref, nf_ref, wr_ref, br_ref = refs[:6]
        x1_ref, hn_ref, route_ref, route_t_ref, cnt_ref, carry_ref = refs[6:]
        a = a_ref[...]
    epg = n_experts // n_groups

    @pl.when(pl.program_id(0) == 0)
    def _():
        carry_ref[...] = jnp.zeros(carry_ref.shape, F32)

    x1 = x_ref[...] + jnp.dot(a, w_ref[...], preferred_element_type=F32)
    x1_ref[...] = x1
    hn = x1 * lax.rsqrt(jnp.mean(x1 * x1, axis=-1, keepdims=True) + RMS_EPS) * nf_ref[...]
    hn_ref[...] = hn

    hn_hi = hn.astype(BF16)
    hn_lo = (hn - hn_hi.astype(F32)).astype(BF16)
    logits = (jnp.dot(hn_hi, wr_ref[0], preferred_element_type=F32)
              + jnp.dot(hn_lo, wr_ref[0], preferred_element_type=F32)
              + jnp.dot(hn_hi, wr_ref[1], preferred_element_type=F32)) + br_ref[...]
    tm = logits.shape[0]

    n_rows = -(-(n_groups + n_experts) // SUBLANES) * SUBLANES
    lt = logits.T[:n_rows, :]
    row = lax.broadcasted_iota(jnp.int32, lt.shape, 0)
    big = jnp.int32(LANES)
    neg = jnp.float32(-jnp.inf)

    gl = jnp.where(row < n_groups, lt, neg)
    gmax = jnp.max(gl, axis=0, keepdims=True)
    g_p = 1.0 / jnp.sum(jnp.exp(gl - gmax), axis=0, keepdims=True)
    g_idx = jnp.min(jnp.where(gl == gmax, row, big), axis=0, keepdims=True)

    lo = n_groups + g_idx * epg
    el = jnp.where((row >= lo) & (row < lo + epg), lt, neg)
    v1 = jnp.max(el, axis=0, keepdims=True)
    i1 = jnp.min(jnp.where(el == v1, row, big), axis=0, keepdims=True)
    el2 = jnp.where(row == i1, neg, el)
    v2 = jnp.max(el2, axis=0, keepdims=True)
    i2 = jnp.min(jnp.where(el2 == v2, row, big), axis=0, keepdims=True)
    tt = jnp.exp(v2 - v1)
    w1 = g_p / (1.0 + tt)
    w2 = g_p * tt / (1.0 + tt)
    e1 = i1 - n_groups
    e2 = i2 - n_groups

    erow = lax.broadcasted_iota(jnp.int32, (n_experts, tm), 0)
    oh1 = erow == e1
    oh2 = erow == e2
    onehot = jnp.where(oh1 | oh2, 1.0, 0.0)
    before = lax.broadcasted_iota(jnp.int32, (tm, tm), 0) < lax.broadcasted_iota(jnp.int32, (tm, tm), 1)
    tri = jnp.where(before, 1.0, 0.0).astype(BF16)
    carry = carry_ref[...]
    base = jnp.dot(onehot.astype(BF16), tri, preferred_element_type=F32) + carry[:, 0:1]
    rank1 = jnp.sum(jnp.where(oh1, base, 0.0), axis=0, keepdims=True)
    rank2 = jnp.sum(jnp.where(oh2, base, 0.0), axis=0, keepdims=True)
    carry = carry + jnp.sum(onehot, axis=1, keepdims=True)
    carry_ref[...] = carry
    cnt_ref[...] = carry

    frow = lax.broadcasted_iota(jnp.int32, (LANES, tm), 0)
    out = jnp.zeros((LANES, tm), F32)
    for idx, val in enumerate((e1.astype(F32), e2.astype(F32), w1, w2, rank1, rank2)):
        out = jnp.where(frow == idx, val, out)
    route_t_ref[...] = out[:route_t_ref.shape[0], :]
    route_ref[...] = out.T


def _out_router(a, gate, w, x, nf, w_group, b_group, w_expert, b_expert):
    t, d = x.shape
    k = w.shape[0]
    n_groups = w_group.shape[1]
    n_experts = w_expert.shape[1]
    assert n_groups + n_experts <= LANES
    pad = LANES - n_groups - n_experts
    wr = jnp.pad(jnp.concatenate([w_group, w_expert], axis=1).astype(F32), ((0, 0), (0, pad)))
    wr_hi = wr.astype(BF16)
    wr = jnp.stack([wr_hi, (wr - wr_hi.astype(F32)).astype(BF16)])
    br = jnp.pad(jnp.concatenate([b_group, b_expert]).astype(F32), (0, pad)).reshape(1, LANES)
    gated = gate is not None
    row = lambda i: (i, 0)
    fixed = lambda i: (0, 0)
    in_specs = [pl.BlockSpec((TM, k), row)]
    args = [a]
    if gated:
        gate_arr, gate_col = gate
        in_specs.append(pl.BlockSpec((TM, k), lambda i: (i, gate_col)))
        args.append(gate_arr)
    in_specs += [
        pl.BlockSpec((k, d), fixed),
        pl.BlockSpec((TM, d), row),
        pl.BlockSpec((1, d), fixed),
        pl.BlockSpec((2, d, LANES), lambda i: (0, 0, 0)),
        pl.BlockSpec((1, LANES), fixed),
    ]
    args += [w, x, nf.reshape(1, d), wr, br]
    return pl.pallas_call(
        functools.partial(_out_router_kernel, gated=gated, n_groups=n_groups, n_experts=n_experts),
        grid=(t // TM,),
        in_specs=in_specs,
        out_specs=[
            pl.BlockSpec((TM, d), row),
            pl.BlockSpec((TM, d), row),
            pl.BlockSpec((TM, LANES), row),
            pl.BlockSpec((SUBLANES, TM), lambda i: (0, i)),
            pl.BlockSpec((n_experts, LANES), fixed),
        ],
        out_shape=[
            jax.ShapeDtypeStruct((t, d), F32),
            jax.ShapeDtypeStruct((t, d), F32),
            jax.ShapeDtypeStruct((t, LANES), F32),
            jax.ShapeDtypeStruct((SUBLANES, t), F32),
            jax.ShapeDtypeStruct((n_experts, LANES), F32),
        ],
        scratch_shapes=[pltpu.VMEM((n_experts, LANES), F32)],
        compiler_params=_cparams(("arbitrary",)),
    )(*args)


def _row_dest_kernel(d1_ref, d2_ref, pad_ref, dst_ref):
    t = d1_ref.shape[0]
    pltpu.sync_copy(pad_ref, dst_ref)

    def fill(tok, carry):
        dst_ref[ROW_BLOCK + d1_ref[tok]] = tok
        dst_ref[ROW_BLOCK + d2_ref[tok]] = t + tok
        return carry

    lax.fori_loop(0, t, fill, 0, unroll=DMA_UNROLL)


def _row_dest(d1, d2, n_rows):
    n = n_rows + 2 * ROW_BLOCK
    pad_rows = 2 * d1.shape[0] + jnp.arange(n, dtype=jnp.int32) % ROW_BLOCK
    grid_spec = pltpu.PrefetchScalarGridSpec(
        num_scalar_prefetch=2,
        grid=(1,),
        in_specs=[pl.BlockSpec(memory_space=pl.ANY)],
        out_specs=pl.BlockSpec(memory_space=pltpu.SMEM),
    )
    return pl.pallas_call(
        _row_dest_kernel,
        grid_spec=grid_spec,
        out_shape=jax.ShapeDtypeStruct((n,), jnp.int32),
        compiler_params=_cparams(("arbitrary",)),
    )(d1, d2, pad_rows)


def _expert_kernel(be_ref, dst_ref, hn_ref, wg_ref, wu_ref, wd_ref, y_ref,
                   wgb_ref, wub_ref, wdb_ref, xa_ref, xb_ref, ya_ref, yb_ref, gsem, ssem):
    i = pl.program_id(0)
    nb = pl.num_programs(0)
    rb = xa_ref.shape[0]
    t = hn_ref.shape[0]

    def gather(blk, xbuf, sem):
        for r in range(rb):
            dst = dst_ref[blk * rb + r]
            tok = jnp.minimum(jnp.where(dst >= t, dst - t, dst), t - 1)
            pltpu.make_async_copy(hn_ref.at[pl.ds(tok, 1), :], xbuf.at[pl.ds(r, 1), :], sem).start()

    def scatter(blk, ybuf, sem):
        for r in range(rb):
            dst = dst_ref[blk * rb + r]
            pltpu.make_async_copy(ybuf.at[pl.ds(r, 1), :], y_ref.at[pl.ds(dst, 1), :], sem).start()

    def wait_gather(xbuf, sem):
        pltpu.make_async_copy(hn_ref.at[pl.ds(0, rb), :], xbuf, sem).wait()

    def wait_scatter(ybuf, sem):
        pltpu.make_async_copy(ybuf, y_ref.at[pl.ds(0, rb), :], sem).wait()

    @pl.when(i == 0)
    def _():
        yb_ref[...] = jnp.zeros(yb_ref.shape, yb_ref.dtype)
        gather(1, xa_ref, gsem.at[0])

    @pl.when((i == 0) | (be_ref[i] != be_ref[jnp.maximum(i - 1, 0)]))
    def _():
        wgb_ref[...] = wg_ref[0, 0].astype(BF16)
        wub_ref[...] = wu_ref[0, 0].astype(BF16)
        wdb_ref[...] = wd_ref[0, 0].astype(BF16)

    def step(p, xcur, xnext, ycur, yprev):
        wait_gather(xcur, gsem.at[p])

        @pl.when(i > 0)
        def _():
            wait_scatter(ycur, ssem.at[p])

        gather(i + 2, xnext, gsem.at[1 - p])
        scatter(i, yprev, ssem.at[1 - p])
        x = xcur[...].astype(BF16)
        g = jnp.dot(x, wgb_ref[...], preferred_element_type=F32)
        u = jnp.dot(x, wub_ref[...], preferred_element_type=F32)
        hmid = (g * jax.nn.sigmoid(g) * u).astype(BF16)
        ycur[...] = jnp.dot(hmid, wdb_ref[...], preferred_element_type=F32)

        @pl.when(i == nb - 1)
        def _():
            wait_scatter(yprev, ssem.at[1 - p])
            scatter(i + 1, ycur, ssem.at[p])
            wait_scatter(ycur, ssem.at[p])
            wait_gather(xnext, gsem.at[1 - p])

    @pl.when(i % 2 == 0)
    def _():
        step(0, xa_ref, xb_ref, ya_ref, yb_ref)

    @pl.when(i % 2 == 1)
    def _():
        step(1, xb_ref, xa_ref, yb_ref, ya_ref)


def _experts(hn, block_e, row_dst, w_gate, w_up, w_down, layer):
    t, d = hn.shape
    ff = w_gate.shape[3]
    nb = block_e.shape[0]
    wsel = lambda i, be, dst: (layer, be[i], 0, 0)
    grid_spec = pltpu.PrefetchScalarGridSpec(
        num_scalar_prefetch=2,
        grid=(nb,),
        in_specs=[
            pl.BlockSpec(memory_space=pl.ANY),
            pl.BlockSpec((1, 1, d, ff), wsel),
            pl.BlockSpec((1, 1, d, ff), wsel),
            pl.BlockSpec((1, 1, ff, d), wsel),
        ],
        out_specs=pl.BlockSpec(memory_space=pl.ANY),
        scratch_shapes=[pltpu.VMEM((d, ff), BF16), pltpu.VMEM((d, ff), BF16), pltpu.VMEM((ff, d), BF16)]
        + [pltpu.VMEM((ROW_BLOCK, d), F32)] * 4
        + [pltpu.SemaphoreType.DMA((2,)), pltpu.SemaphoreType.DMA((2,))],
    )
    return pl.pallas_call(
        _expert_kernel,
        grid_spec=grid_spec,
        out_shape=jax.ShapeDtypeStruct((2 * t + ROW_BLOCK, d), F32),
        compiler_params=_cparams(("arbitrary",)),
    )(block_e, row_dst, hn, w_gate, w_up, w_down)


def _combine_kernel(x_ref, route_ref, y1_ref, y2_ref, nf_ref, o_ref, *, final_norm):
    route = route_ref[...]
    y = x_ref[...] + (route[:, 2:3] * y1_ref[...] + route[:, 3:4] * y2_ref[...])
    if final_norm:
        y = y * lax.rsqrt(jnp.mean(y * y, axis=-1, keepdims=True) + RMS_EPS) * nf_ref[...]
    o_ref[...] = y


def _combine(x1, route, ys, nf, final_norm):
    t, d = x1.shape
    slot2 = t // TM
    row = lambda i: (i, 0)
    return pl.pallas_call(
        functools.partial(_combine_kernel, final_norm=final_norm),
        grid=(t // TM,),
        in_specs=[
            pl.BlockSpec((TM, d), row),
            pl.BlockSpec((TM, LANES), row),
            pl.BlockSpec((TM, d), row),
            pl.BlockSpec((TM, d), lambda i: (slot2 + i, 0)),
            pl.BlockSpec((1, d), lambda i: (0, 0)),
        ],
        out_specs=pl.BlockSpec((TM, d), row),
        out_shape=jax.ShapeDtypeStruct((t, d), F32),
        compiler_params=_cparams(("parallel",)),
    )(x1, route, ys, ys, nf.reshape(1, d).astype(F32))


def _moe(x1, hn, route, route_t, cnt, w_gate, w_up, w_down, layer, nf, final_norm):
    t, d = x1.shape
    n_experts = w_gate.shape[1]
    ids = jnp.arange(n_experts, dtype=jnp.int32)
    e1, e2, rank1, rank2 = (route_t[f].astype(jnp.int32) for f in (0, 1, 4, 5))
    counts = cnt[:, 0].astype(jnp.int32)
    padded = ((counts + ROW_BLOCK - 1) // ROW_BLOCK) * ROW_BLOCK
    pad_end = jnp.cumsum(padded)
    pad_start = pad_end - padded

    def dest(e, rank):
        return jnp.sum(jnp.where(e[:, None] == ids[None, :], pad_start[None, :], 0), axis=1) + rank

    nb = (t * TOP_K_INNER) // ROW_BLOCK + n_experts
    first_row = jnp.arange(nb, dtype=jnp.int32) * ROW_BLOCK
    block_e = jnp.sum((pad_end[None, :] <= first_row[:, None]).astype(jnp.int32), axis=1)
    block_e = jnp.minimum(block_e, n_experts - 1)
    row_dst = _row_dest(dest(e1, rank1), dest(e2, rank2), nb * ROW_BLOCK)
    ys = _experts(hn, block_e, row_dst, w_gate, w_up, w_down, layer)
    return _combine(x1, route, ys, nf, final_norm)


def _diff_lambda_init(layer_idx):
    return 0.8 - 0.6 * math.exp(-0.3 * layer_idx)


def _rope_tables(seq, dk):
    angle = 1.0 / (10000.0 ** jnp.linspace(0.0, 1.0, dk // 2, dtype=F32))
    angle = jnp.repeat(angle, 2)
    phase = jnp.arange(seq, dtype=F32)[:, None] * angle[None, :]
    sin, cos = jnp.sin(phase), jnp.cos(phase)
    even = (jnp.arange(dk) % 2 == 0)[None, :]
    return cos, jnp.where(even, -sin, 0.0), jnp.where(even, 0.0, sin)


def kernel(x, rel_bias_table, norm_mix, norm_ffn, norm_final, diff_w_in, diff_lambda_qk, diff_subln,
           diff_w_out, ret_w_in, ret_w_out, moe_w_group, moe_b_group, moe_w_expert, moe_b_expert,
           moe_w_gate, moe_w_up, moe_w_down):
    batch, seq, d = x.shape
    depth = norm_mix.shape[0]
    t = batch * seq
    heads = rel_bias_table.shape[1]
    dh = diff_lambda_qk.shape[-1]
    qk_dim = (ret_w_in.shape[2] - 2 * ret_w_out.shape[1]) // 2
    v_dim = ret_w_out.shape[1]
    dk = qk_dim // RET_HEADS
    xt = x.reshape(t, d)
    for i in range(depth):
        j = i // N_MIXERS
        if i % N_MIXERS == 0:
            col_scale = jnp.concatenate([jnp.full((d,), dh ** -0.5 * LOG2E, F32), jnp.ones((2 * d,), F32)])
            w_in = (diff_w_in[j] * col_scale[None, :]).astype(BF16)
            qkv = _norm_matmul(xt, norm_mix[i], w_in)
            mix = _diff_attention(qkv, rel_bias_table, diff_lambda_qk[j], diff_subln[j],
                                  _diff_lambda_init(i), batch, seq, heads)
            gate = None
            w_out = diff_w_out[j].astype(BF16)
        else:
            n_in = ret_w_in.shape[2]
            col_scale = jnp.concatenate([jnp.ones((qk_dim,), F32), jnp.full((qk_dim,), dk ** -0.5, F32),
                                         jnp.ones((n_in - 2 * qk_dim,), F32)])
            w_in = (ret_w_in[j] * col_scale[None, :]).astype(BF16)
            qkvg = _norm_matmul(xt, norm_mix[i], w_in, rope=_rope_tables(seq, dk),
                                n_rope=2 * qk_dim // TN, seq=seq)
            mix = _retention(qkvg, batch, seq, qk_dim, v_dim)
            gate = (qkvg, (2 * qk_dim + v_dim) // v_dim)
            w_out = ret_w_out[j].astype(BF16)
        x1, hn, route, route_t, cnt = _out_router(mix, gate, w_out, xt, norm_ffn[i], moe_w_group[i],
                                                  moe_b_group[i], moe_w_expert[i], moe_b_expert[i])
        xt = _moe(x1, hn, route, route_t, cnt, moe_w_gate, moe_w_up, moe_w_down, i,
                  norm_final, i == depth - 1)
    return xt.reshape(batch, seq, d)
```

```python
import functools
import math

import jax
import jax.numpy as jnp
from jax import lax
from jax.experimental import pallas as pl
from jax.experimental.pallas import tpu as pltpu

F32 = jnp.float32
BF16 = jnp.bfloat16

N_MIXERS = 2
NUM_BUCKETS = 32
MAX_DISTANCE = 128
RET_HEADS = 4
TOP_K_INNER = 2
RMS_EPS = 1e-6
SUBLN_EPS = 1e-5

LANES = 128
SUBLANES = 8
BF16_SUBLANES = 16
LOG2E = 1.4426950408889634
VMEM_LIMIT = 48 * 1024 * 1024

TM = 512
TN = 1024
TQ = 256
ATTN_HEADS = 4
RET_C = 256
ROW_BLOCK = 256
TC = 256
DMA_UNROLL = 8


def _cparams(sem):
    return pltpu.CompilerParams(dimension_semantics=sem, vmem_limit_bytes=VMEM_LIMIT)


def _norm_matmul_kernel(x_ref, g_ref, w_ref, *rest, n_rope):
    if n_rope:
        cos_ref, sa_ref, sb_ref, o_ref = rest
    else:
        (o_ref,) = rest
    x = x_ref[...]
    xn = (x * lax.rsqrt(jnp.mean(x * x, axis=-1, keepdims=True) + RMS_EPS) * g_ref[...]).astype(BF16)
    for j in range(o_ref.shape[1] // TN):
        cols = slice(j * TN, (j + 1) * TN)
        acc = jnp.dot(xn, w_ref[:, cols], preferred_element_type=F32)
        if j < n_rope:
            reps = TN // cos_ref.shape[1]
            c = jnp.tile(cos_ref[...], (1, reps))
            sa = jnp.tile(sa_ref[...], (1, reps))
            sb = jnp.tile(sb_ref[...], (1, reps))
            nxt = pltpu.roll(acc, TN - 1, axis=1)
            prv = pltpu.roll(acc, 1, axis=1)
            acc = acc * c + nxt * sa + prv * sb
        o_ref[:, cols] = acc.astype(o_ref.dtype)


def _norm_matmul(x, g, w, rope=None, n_rope=0, seq=None):
    t, d = x.shape
    n = w.shape[1]
    in_specs = [
        pl.BlockSpec((TM, d), lambda i: (i, 0)),
        pl.BlockSpec((1, d), lambda i: (0, 0)),
        pl.BlockSpec((d, n), lambda i: (0, 0), pipeline_mode=pl.Buffered(1)),
    ]
    args = [x, g.reshape(1, d), w]
    if rope is not None:
        nb = seq // TM
        dk = rope[0].shape[1]
        in_specs += [pl.BlockSpec((TM, dk), lambda i: (i % nb, 0))] * 3
        args += list(rope)
    return pl.pallas_call(
        functools.partial(_norm_matmul_kernel, n_rope=n_rope),
        grid=(t // TM,),
        in_specs=in_specs,
        out_specs=pl.BlockSpec((TM, n), lambda i: (i, 0)),
        out_shape=jax.ShapeDtypeStruct((t, n), BF16),
        compiler_params=_cparams(("parallel",)),
    )(*args)


def _diff_attn_kernel(q_ref, k_ref, v_ref, bias_ref, lam_ref, g_ref, o_ref,
                      vt_ref, m_ref, acc_ref, s_ref, *, lambda_init):
    seq = q_ref.shape[0]
    hp, _, _, tq = bias_ref.shape
    dv = q_ref.shape[1] // hp
    dh = dv // 2
    nq = seq // tq

    def head(hh):
        return slice(hh * dv, (hh + 1) * dv)

    extra = vt_ref.shape[2] - dv
    ones_row = jnp.where(lax.broadcasted_iota(jnp.int32, (extra, tq), 0) == 0, 1.0, 0.0).astype(BF16)
    for hh in range(hp):
        for j in range(nq):
            vt_ref[hh, j, :dv, :] = v_ref[j * tq:(j + 1) * tq, head(hh)].astype(F32).T.astype(BF16)
            vt_ref[hh, j, dv:, :] = ones_row

    lq = lam_ref[...]
    lam = (jnp.exp(jnp.sum(lq[0:1] * lq[1:2], axis=-1, keepdims=True))
           - jnp.exp(jnp.sum(lq[2:3] * lq[3:4], axis=-1, keepdims=True)) + lambda_init)
    lane = lax.broadcasted_iota(jnp.int32, (tq, dv), 1)

    def q_body(qi, carry):
        qrows = pl.ds(pl.multiple_of(qi * tq, tq), tq)
        qs = []
        for hh in range(hp):
            q = q_ref[qrows, head(hh)]
            zero = jnp.zeros_like(q)
            qs.append(jnp.concatenate([jnp.where(lane < dh, q, zero), jnp.where(lane >= dh, q, zero)],
                                      axis=0))

        m_ref[...] = jnp.full(m_ref.shape, -jnp.inf, F32)
        acc_ref[...] = jnp.zeros(acc_ref.shape, F32)

        def score_dots(j):
            rows = pl.ds(pl.multiple_of(j * tq, tq), tq)
            for hh in range(hp):
                s_ref[hh] = lax.dot_general(k_ref[rows, head(hh)], qs[hh], (((1,), (1,)), ((), ())),
                                            preferred_element_type=F32)

        def step(j, tile, prefetch):
            probs, alphas = [], []
            for hh in range(hp):
                s = s_ref[hh]
                if tile is not None:
                    bias = bias_ref[hh, tile]
                    s = s + jnp.concatenate([bias, bias], axis=1)
                m_old = m_ref[hh]
                m_new = jnp.maximum(m_old, jnp.max(s, axis=0, keepdims=True))
                alpha = jnp.exp2(m_old - m_new)
                m_ref[hh] = m_new
                probs.append(jnp.exp2(s - m_new).astype(BF16))
                alphas.append(alpha)
            if prefetch:
                score_dots(j + 1)
            for hh in range(hp):
                acc_ref[hh] = alphas[hh] * acc_ref[hh] + jnp.dot(vt_ref[hh, j], probs[hh],
                                                                 preferred_element_type=F32)

        score_dots(0)

        def far_body(j, c):
            step(j, None, True)
            return c

        lax.fori_loop(0, jnp.maximum(qi - 1, 0), far_body, 0)

        @pl.when(qi >= 1)
        def _():
            step(qi - 1, 1, True)

        step(qi, 0, False)

        for hh in range(hp):
            acc = acc_ref[hh]
            on = acc[:dv] / acc[dv:dv + 1]
            ot = on[:, :tq] - lam * on[:, tq:]
            ot = ot * lax.rsqrt(jnp.mean(ot * ot, axis=0, keepdims=True) + SUBLN_EPS)
            o_ref[qrows, head(hh)] = (ot.T * g_ref[...] * (1.0 - lambda_init)).astype(o_ref.dtype)
        return carry

    lax.fori_loop(0, nq, q_body, 0)


def _t5_bucket(rel):
    max_exact = NUM_BUCKETS // 2
    n = jnp.maximum(rel, 0)
    is_small = n < max_exact
    nf = jnp.maximum(n, max_exact).astype(F32)
    large = max_exact + (jnp.log(nf / max_exact) / math.log(MAX_DISTANCE / max_exact)
                         * (NUM_BUCKETS - max_exact)).astype(jnp.int32)
    large = jnp.minimum(large, NUM_BUCKETS - 1)
    return jnp.where(is_small, n, large)


def _bias_tiles(rel_table):
    heads = rel_table.shape[1]
    span = 2 * TQ + 1
    bucket = _t5_bucket(jnp.arange(span))
    onehot = (bucket[:, None] == jnp.arange(NUM_BUCKETS)[None, :]).astype(F32)
    table = (rel_table.astype(F32) - rel_table[NUM_BUCKETS - 1].astype(F32)[None, :]) * LOG2E
    by_rel = jnp.dot(onehot, table, precision=lax.Precision.HIGHEST).T
    toep = jnp.tile(by_rel, (1, TQ))[:, :TQ * (span - 1)].reshape(heads, TQ, span - 1)
    kj = jnp.arange(TQ)[:, None]
    qi = jnp.arange(TQ)[None, :]
    diag = jnp.where((qi >= kj)[None], toep[:, :, :TQ], -jnp.inf)
    return jnp.stack([diag, toep[:, :, TQ:]], axis=1)


def _diff_attention(qkv, rel_table, lam_qk, subln_g, lambda_init, batch, seq, heads):
    t = qkv.shape[0]
    d = qkv.shape[1] // 3
    dv = d // heads
    hp = ATTN_HEADS
    assert TQ + 1 >= MAX_DISTANCE and seq % TQ == 0 and dv == LANES and heads % hp == 0
    nq = seq // TQ
    groups = heads // hp
    bias = _bias_tiles(rel_table)
    return pl.pallas_call(
        functools.partial(_diff_attn_kernel, lambda_init=lambda_init),
        grid=(batch, groups),
        in_specs=[
            pl.BlockSpec((seq, hp * dv), lambda b, g: (b, g)),
            pl.BlockSpec((seq, hp * dv), lambda b, g: (b, groups + g)),
            pl.BlockSpec((seq, hp * dv), lambda b, g: (b, 2 * groups + g)),
            pl.BlockSpec((hp, 2, TQ, TQ), lambda b, g: (g, 0, 0, 0)),
            pl.BlockSpec(lam_qk.shape, lambda b, g: (0, 0)),
            pl.BlockSpec((1, dv), lambda b, g: (0, 0)),
        ],
        out_specs=pl.BlockSpec((seq, hp * dv), lambda b, g: (b, g)),
        out_shape=jax.ShapeDtypeStruct((t, d), BF16),
        scratch_shapes=[
            pltpu.VMEM((hp, nq, dv + BF16_SUBLANES, TQ), BF16),
            pltpu.VMEM((hp, 1, 2 * TQ), F32),
            pltpu.VMEM((hp, dv + BF16_SUBLANES, 2 * TQ), F32),
            pltpu.VMEM((hp, TQ, 2 * TQ), F32),
        ],
        compiler_params=_cparams(("parallel", "parallel")),
    )(qkv, qkv, qkv, bias, lam_qk.astype(F32), subln_g.reshape(1, dv).astype(F32))


def _retention_kernel(q_ref, k_ref, v_ref, dec_ref, tab_ref, o_ref, state_ref):
    seq = q_ref.shape[0]
    c = dec_ref.shape[1]
    dec = dec_ref[0]
    tab = tab_ref[0]
    qd = tab[:, 0:1]
    kd = tab[:, 1:2]
    cd = tab[0:1, 2:3]
    state_ref[...] = jnp.zeros(state_ref.shape, F32)

    def body(n, carry):
        rows = pl.ds(pl.multiple_of(n * c, c), c)
        q = q_ref[rows, :]
        k = k_ref[rows, :]
        v = v_ref[rows, :]
        s = lax.dot_general(q, k, (((1,), (1,)), ((), ())), preferred_element_type=F32) * dec
        inner = jnp.dot(s.astype(BF16), v, preferred_element_type=F32)
        st = state_ref[...]
        cross = jnp.dot(q, st.astype(BF16), preferred_element_type=F32) * qd
        kt = (k.astype(F32) * kd).T.astype(BF16)
        state_ref[...] = st * cd + jnp.dot(kt, v, preferred_element_type=F32)
        o = inner + cross
        o = o * lax.rsqrt(jnp.mean(o * o, axis=-1, keepdims=True) + RMS_EPS)
        o_ref[rows, :] = o.astype(o_ref.dtype)
        return carry

    lax.fori_loop(0, seq // c, body, 0)


def _retention_tables(heads):
    c = RET_C
    log_gamma = jnp.log(1.0 - 2.0 ** (-5.0 - jnp.arange(heads, dtype=F32)))
    pos = jnp.arange(c, dtype=F32)
    dist = pos[:, None] - pos[None, :]
    inner_decay = jnp.where(dist[None] >= 0,
                            jnp.exp(jnp.maximum(dist, 0.0)[None] * log_gamma[:, None, None]), 0.0)
    q_decay = jnp.exp((pos + 1.0)[None] * log_gamma[:, None])
    k_decay = jnp.exp((c - 1.0 - pos)[None] * log_gamma[:, None])
    chunk_decay = jnp.broadcast_to(jnp.exp(c * log_gamma)[:, None], (heads, c))
    tab = jnp.stack([q_decay, k_decay, chunk_decay], axis=-1)
    tab = jnp.pad(tab, ((0, 0), (0, 0), (0, LANES - 3)))
    return inner_decay, tab


def _retention(qkvg, batch, seq, qk_dim, v_dim):
    t = qkvg.shape[0]
    dk = qk_dim // RET_HEADS
    dvh = v_dim // RET_HEADS
    inner_decay, tab = _retention_tables(RET_HEADS)
    kb = qk_dim // dk
    vb = 2 * qk_dim // dvh
    return pl.pallas_call(
        _retention_kernel,
        grid=(batch, RET_HEADS),
        in_specs=[
            pl.BlockSpec((seq, dk), lambda b, h: (b, h)),
            pl.BlockSpec((seq, dk), lambda b, h: (b, kb + h)),
            pl.BlockSpec((seq, dvh), lambda b, h: (b, vb + h)),
            pl.BlockSpec((1, RET_C, RET_C), lambda b, h: (h, 0, 0)),
            pl.BlockSpec((1, RET_C, LANES), lambda b, h: (h, 0, 0)),
        ],
        out_specs=pl.BlockSpec((seq, dvh), lambda b, h: (b, h)),
        out_shape=jax.ShapeDtypeStruct((t, v_dim), BF16),
        scratch_shapes=[pltpu.VMEM((dk, dvh), F32)],
        compiler_params=_cparams(("parallel", "arbitrary")),
    )(qkvg, qkvg, qkvg, inner_decay, tab)


def _to_row_tiles(ref, val):
    for s in range(ref.shape[1]):
        ref[:, s, :] = val[:, s * LANES:(s + 1) * LANES]


def _from_row_tiles(ref):
    return jnp.concatenate([ref[:, s, :] for s in range(ref.shape[1])], axis=1)


def _out_router_kernel(*refs, gated, n_groups, n_experts):
    if gated:
        a_ref, gate_ref, w_ref, x_ref, nf_ref, wr_ref, br_ref = refs[:7]
        x1_ref, hn_ref, route_ref, route_t_ref, cnt_ref, carry_ref = refs[7:]
        gate = gate_ref[...]
        a = gate * jax.nn.sigmoid(gate) * a_ref[...]
    else:
        a_ref, w_ref, x_ref, nf_ref, wr_ref, br_ref = refs[:6]
        x1_ref, hn_ref, route_ref, route_t_ref, cnt_ref, carry_ref = refs[6:]
        a = a_ref[...]
    epg = n_experts // n_groups

    @pl.when(pl.program_id(0) == 0)
    def _():
        carry_ref[...] = jnp.zeros(carry_ref.shape, F32)

    x1 = x_ref[...] + jnp.dot(a, w_ref[...], preferred_element_type=F32)
    x1_ref[...] = x1
    hn = x1 * lax.rsqrt(jnp.mean(x1 * x1, axis=-1, keepdims=True) + RMS_EPS) * nf_ref[...]
    _to_row_tiles(hn_ref, hn)

    hn_hi = hn.astype(BF16)
    hn_lo = (hn - hn_hi.astype(F32)).astype(BF16)
    logits = (jnp.dot(hn_hi, wr_ref[0], preferred_element_type=F32)
              + jnp.dot(hn_lo, wr_ref[0], preferred_element_type=F32)
              + jnp.dot(hn_hi, wr_ref[1], preferred_element_type=F32)) + br_ref[...]
    tm = logits.shape[0]

    n_rows = -(-(n_groups + n_experts) // SUBLANES) * SUBLANES
    lt = logits.T[:n_rows, :]
    row = lax.broadcasted_iota(jnp.int32, lt.shape, 0)
    big = jnp.int32(LANES)
    neg = jnp.float32(-jnp.inf)

    gl = jnp.where(row < n_groups, lt, neg)
    gmax = jnp.max(gl, axis=0, keepdims=True)
    g_p = 1.0 / jnp.sum(jnp.exp(gl - gmax), axis=0, keepdims=True)
    g_idx = jnp.min(jnp.where(gl == gmax, row, big), axis=0, keepdims=True)

    lo = n_groups + g_idx * epg
    el = jnp.where((row >= lo) & (row < lo + epg), lt, neg)
    v1 = jnp.max(el, axis=0, keepdims=True)
    i1 = jnp.min(jnp.where(el == v1, row, big), axis=0, keepdims=True)
    el2 = jnp.where(row == i1, neg, el)
    v2 = jnp.max(el2, axis=0, keepdims=True)
    i2 = jnp.min(jnp.where(el2 == v2, row, big), axis=0, keepdims=True)
    tt = jnp.exp(v2 - v1)
    w1 = g_p / (1.0 + tt)
    w2 = g_p * tt / (1.0 + tt)
    e1 = i1 - n_groups
    e2 = i2 - n_groups

    erow = lax.broadcasted_iota(jnp.int32, (n_experts, tm), 0)
    oh1 = erow == e1
    oh2 = erow == e2
    onehot = jnp.where(oh1 | oh2, 1.0, 0.0)
    before = lax.broadcasted_iota(jnp.int32, (tm, tm), 0) < lax.broadcasted_iota(jnp.int32, (tm, tm), 1)
    tri = jnp.where(before, 1.0, 0.0).astype(BF16)
    carry = carry_ref[...]
    base = jnp.dot(onehot.astype(BF16), tri, preferred_element_type=F32) + carry[:, 0:1]
    rank1 = jnp.sum(jnp.where(oh1, base, 0.0), axis=0, keepdims=True)
    rank2 = jnp.sum(jnp.where(oh2, base, 0.0), axis=0, keepdims=True)
    carry = carry + jnp.sum(onehot, axis=1, keepdims=True)
    carry_ref[...] = carry
    cnt_ref[...] = carry

    frow = lax.broadcasted_iota(jnp.int32, (LANES, tm), 0)
    out = jnp.zeros((LANES, tm), F32)
    for idx, val in enumerate((e1.astype(F32), e2.astype(F32), w1, w2, rank1, rank2)):
        out = jnp.where(frow == idx, val, out)
    route_t_ref[...] = out[:route_t_ref.shape[0], :]
    route_ref[...] = out.T


def _out_router(a, gate, w, x, nf, w_group, b_group, w_expert, b_expert):
    t, d = x.shape
    k = w.shape[0]
    n_groups = w_group.shape[1]
    n_experts = w_expert.shape[1]
    assert n_groups + n_experts <= LANES and d % (SUBLANES * LANES) == 0
    pad = LANES - n_groups - n_experts
    wr = jnp.pad(jnp.concatenate([w_group, w_expert], axis=1).astype(F32), ((0, 0), (0, pad)))
    wr_hi = wr.astype(BF16)
    wr = jnp.stack([wr_hi, (wr - wr_hi.astype(F32)).astype(BF16)])
    br = jnp.pad(jnp.concatenate([b_group, b_expert]).astype(F32), (0, pad)).reshape(1, LANES)
    gated = gate is not None
    row = lambda i: (i, 0)
    fixed = lambda i: (0, 0)
    in_specs = [pl.BlockSpec((TM, k), row)]
    args = [a]
    if gated:
        gate_arr, gate_col = gate
        in_specs.append(pl.BlockSpec((TM, k), lambda i: (i, gate_col)))
        args.append(gate_arr)
    in_specs += [
        pl.BlockSpec((k, d), fixed),
        pl.BlockSpec((TM, d), row),
        pl.BlockSpec((1, d), fixed),
        pl.BlockSpec((2, d, LANES), lambda i: (0, 0, 0)),
        pl.BlockSpec((1, LANES), fixed),
    ]
    args += [w, x, nf.reshape(1, d), wr, br]
    return pl.pallas_call(
        functools.partial(_out_router_kernel, gated=gated, n_groups=n_groups, n_experts=n_experts),
        grid=(t // TM,),
        in_specs=in_specs,
        out_specs=[
            pl.BlockSpec((TM, d), row),
            pl.BlockSpec((TM, d // LANES, LANES), lambda i: (i, 0, 0)),
            pl.BlockSpec((TM, LANES), row),
            pl.BlockSpec((SUBLANES, TM), lambda i: (0, i)),
            pl.BlockSpec((n_experts, LANES), fixed),
        ],
        out_shape=[
            jax.ShapeDtypeStruct((t, d), F32),
            jax.ShapeDtypeStruct((t, d // LANES, LANES), F32),
            jax.ShapeDtypeStruct((t, LANES), F32),
            jax.ShapeDtypeStruct((SUBLANES, t), F32),
            jax.ShapeDtypeStruct((n_experts, LANES), F32),
        ],
        scratch_shapes=[pltpu.VMEM((n_experts, LANES), F32)],
        compiler_params=_cparams(("arbitrary",)),
    )(*args)


def _dispatch_kernel(d1_ref, d2_ref, zs_ref, zf_ref, nu_ref, hn_ref, xs_ref, zero_ref, sem, zsem):
    tm = hn_ref.shape[0]
    rb = zero_ref.shape[0]
    nb = xs_ref.shape[0] // rb
    base = pl.program_id(0) * tm

    @pl.when(pl.program_id(0) == 0)
    def _():
        zero_ref[...] = jnp.zeros(zero_ref.shape, zero_ref.dtype)

        def zcopy(row0):
            rows = pl.ds(pl.multiple_of(row0, rb), rb)
            return pltpu.make_async_copy(zero_ref, xs_ref.at[rows], zsem)

        def zissue(e, carry):
            @pl.when(zf_ref[e] > 0)
            def _():
                zcopy(zs_ref[e]).start()
            return carry

        def zwait(e, carry):
            @pl.when(zf_ref[e] > 0)
            def _():
                zcopy(zs_ref[e]).wait()
            return carry

        def tissue(b, carry):
            zcopy(b * rb).start()
            return carry

        def twait(b, carry):
            zcopy(b * rb).wait()
            return carry

        lax.fori_loop(0, zs_ref.shape[0], zissue, 0)
        lax.fori_loop(nu_ref[0], nb, tissue, 0)
        lax.fori_loop(0, zs_ref.shape[0], zwait, 0)
        lax.fori_loop(nu_ref[0], nb, twait, 0)

    def issue(r, carry):
        src = hn_ref.at[pl.ds(r, 1)]
        pltpu.make_async_copy(src, xs_ref.at[pl.ds(d1_ref[base + r], 1)], sem.at[0]).start()
        pltpu.make_async_copy(src, xs_ref.at[pl.ds(d2_ref[base + r], 1)], sem.at[1]).start()
        return carry

    lax.fori_loop(0, tm, issue, 0, unroll=DMA_UNROLL)
    for s in range(2):
        pltpu.make_async_copy(hn_ref, xs_ref.at[pl.ds(0, tm)], sem.at[s]).wait()


def _dispatch(hn, d1, d2, zstart, zflag, n_used, n_rows):
    t, sl, _ = hn.shape
    grid_spec = pltpu.PrefetchScalarGridSpec(
        num_scalar_prefetch=5,
        grid=(t // TC,),
        in_specs=[pl.BlockSpec((TC, sl, LANES), lambda i, *_: (i, 0, 0))],
        out_specs=pl.BlockSpec(memory_space=pl.ANY),
        scratch_shapes=[pltpu.VMEM((ROW_BLOCK, sl, LANES), hn.dtype),
                        pltpu.SemaphoreType.DMA((2,)), pltpu.SemaphoreType.DMA(())],
    )
    return pl.pallas_call(
        _dispatch_kernel,
        grid_spec=grid_spec,
        out_shape=jax.ShapeDtypeStruct((n_rows, sl, LANES), hn.dtype),
        compiler_params=_cparams(("arbitrary",)),
    )(d1, d2, zstart, zflag, n_used, hn)


def _expert_kernel(be_ref, nu_ref, x_ref, wg_ref, wu_ref, wd_ref, y_ref, wgb_ref, wub_ref, wdb_ref):
    i = pl.program_id(0)
    used = i < nu_ref[0]
    new_expert = (i == 0) | (be_ref[i] != be_ref[jnp.maximum(i - 1, 0)])

    @pl.when(used & new_expert)
    def _():
        wgb_ref[...] = wg_ref[0, 0].astype(BF16)
        wub_ref[...] = wu_ref[0, 0].astype(BF16)
        wdb_ref[...] = wd_ref[0, 0].astype(BF16)

    @pl.when(used)
    def _():
        x = _from_row_tiles(x_ref).astype(BF16)
        g = jnp.dot(x, wgb_ref[...], preferred_element_type=F32)
        u = jnp.dot(x, wub_ref[...], preferred_element_type=F32)
        hmid = (g * jax.nn.sigmoid(g) * u).astype(BF16)
        _to_row_tiles(y_ref, jnp.dot(hmid, wdb_ref[...], preferred_element_type=F32))

    @pl.when(jnp.logical_not(used))
    def _():
        y_ref[...] = jnp.zeros(y_ref.shape, y_ref.dtype)


def _experts(xs, block_e, n_used, w_gate, w_up, w_down, layer):
    n_rows, sl, _ = xs.shape
    d = sl * LANES
    ff = w_gate.shape[3]
    nb = n_rows // ROW_BLOCK
    last = lambda i, nu: jnp.maximum(jnp.minimum(i, nu[0] - 1), 0)
    blk = lambda i, be, nu: (last(i, nu), 0, 0)
    wsel = lambda i, be, nu: (layer, be[last(i, nu)], 0, 0)
    grid_spec = pltpu.PrefetchScalarGridSpec(
        num_scalar_prefetch=2,
        grid=(nb,),
        in_specs=[
            pl.BlockSpec((ROW_BLOCK, sl, LANES), blk),
            pl.BlockSpec((1, 1, d, ff), wsel),
            pl.BlockSpec((1, 1, d, ff), wsel),
            pl.BlockSpec((1, 1, ff, d), wsel),
        ],
        out_specs=pl.BlockSpec((ROW_BLOCK, sl, LANES), lambda i, be, nu: (i, 0, 0)),
        scratch_shapes=[pltpu.VMEM((d, ff), BF16), pltpu.VMEM((d, ff), BF16), pltpu.VMEM((ff, d), BF16)],
    )
    return pl.pallas_call(
        _expert_kernel,
        grid_spec=grid_spec,
        out_shape=jax.ShapeDtypeStruct((n_rows, sl, LANES), F32),
        compiler_params=_cparams(("arbitrary",)),
    )(block_e, n_used, xs, w_gate, w_up, w_down)


def _combine_kernel(d1_ref, d2_ref, x_ref, route_ref, nf_ref, ys_ref, o_ref, buf_ref, sem, *, final_norm):
    tm = x_ref.shape[0]
    base = pl.program_id(0) * tm

    def issue(r, carry):
        pltpu.make_async_copy(ys_ref.at[pl.ds(d1_ref[base + r], 1)],
                              buf_ref.at[0, pl.ds(r, 1)], sem.at[0]).start()
        pltpu.make_async_copy(ys_ref.at[pl.ds(d2_ref[base + r], 1)],
                              buf_ref.at[1, pl.ds(r, 1)], sem.at[1]).start()
        return carry

    lax.fori_loop(0, tm, issue, 0, unroll=DMA_UNROLL)
    for s in range(2):
        pltpu.make_async_copy(ys_ref.at[pl.ds(0, tm)], buf_ref.at[s], sem.at[s]).wait()

    route = route_ref[...]
    y = x_ref[...] + (route[:, 2:3] * _from_row_tiles(buf_ref.at[0])
                      + route[:, 3:4] * _from_row_tiles(buf_ref.at[1]))
    if final_norm:
        y = y * lax.rsqrt(jnp.mean(y * y, axis=-1, keepdims=True) + RMS_EPS) * nf_ref[...]
    o_ref[...] = y


def _combine(x1, route, ys, d1, d2, nf, final_norm):
    t, d = x1.shape
    sl = ys.shape[1]
    grid_spec = pltpu.PrefetchScalarGridSpec(
        num_scalar_prefetch=2,
        grid=(t // TC,),
        in_specs=[
            pl.BlockSpec((TC, d), lambda i, d1, d2: (i, 0)),
            pl.BlockSpec((TC, LANES), lambda i, d1, d2: (i, 0)),
            pl.BlockSpec((1, d), lambda i, d1, d2: (0, 0)),
            pl.BlockSpec(memory_space=pl.ANY),
        ],
        out_specs=pl.BlockSpec((TC, d), lambda i, d1, d2: (i, 0)),
        scratch_shapes=[pltpu.VMEM((2, TC, sl, LANES), F32), pltpu.SemaphoreType.DMA((2,))],
    )
    return pl.pallas_call(
        functools.partial(_combine_kernel, final_norm=final_norm),
        grid_spec=grid_spec,
        out_shape=jax.ShapeDtypeStruct((t, d), F32),
        compiler_params=_cparams(("arbitrary",)),
    )(d1, d2, x1, route, nf.reshape(1, d).astype(F32), ys)


def _moe(x1, hn, route, route_t, cnt, w_gate, w_up, w_down, layer, nf, final_norm):
    t, d = x1.shape
    n_experts = w_gate.shape[1]
    ids = jnp.arange(n_experts, dtype=jnp.int32)
    e1, e2, rank1, rank2 = (route_t[f].astype(jnp.int32) for f in (0, 1, 4, 5))
    counts = cnt[:, 0].astype(jnp.int32)
    padded = ((counts + ROW_BLOCK - 1) // ROW_BLOCK) * ROW_BLOCK
    pad_end = jnp.cumsum(padded)
    pad_start = pad_end - padded

    def dest(e, rank):
        return jnp.sum(jnp.where(e[:, None] == ids[None, :], pad_start[None, :], 0), axis=1) + rank

    d1 = dest(e1, rank1)
    d2 = dest(e2, rank2)
    nb = (t * TOP_K_INNER) // ROW_BLOCK + n_experts
    n_used = (pad_end[-1:] // ROW_BLOCK).astype(jnp.int32)
    first_row = jnp.arange(nb, dtype=jnp.int32) * ROW_BLOCK
    block_e = jnp.sum((pad_end[None, :] <= first_row[:, None]).astype(jnp.int32), axis=1)
    block_e = jnp.minimum(block_e, n_experts - 1)
    zstart = jnp.maximum(pad_end - ROW_BLOCK, 0).astype(jnp.int32)
    zflag = (padded > 0).astype(jnp.int32)
    xs = _dispatch(hn, d1, d2, zstart, zflag, n_used, nb * ROW_BLOCK)
    ys = _experts(xs, block_e, n_used, w_gate, w_up, w_down, layer)
    return _combine(x1, route, ys, d1, d2, nf, final_norm)


def _diff_lambda_init(layer_idx):
    return 0.8 - 0.6 * math.exp(-0.3 * layer_idx)


def _rope_tables(seq, dk):
    angle = 1.0 / (10000.0 ** jnp.linspace(0.0, 1.0, dk // 2, dtype=F32))
    angle = jnp.repeat(angle, 2)
    phase = jnp.arange(seq, dtype=F32)[:, None] * angle[None, :]
    sin, cos = jnp.sin(phase), jnp.cos(phase)
    even = (jnp.arange(dk) % 2 == 0)[None, :]
    return cos, jnp.where(even, -sin, 0.0), jnp.where(even, 0.0, sin)


def kernel(x, rel_bias_table, norm_mix, norm_ffn, norm_final, diff_w_in, diff_lambda_qk, diff_subln,
           diff_w_out, ret_w_in, ret_w_out, moe_w_group, moe_b_group, moe_w_expert, moe_b_expert,
           moe_w_gate, moe_w_up, moe_w_down):
    batch, seq, d = x.shape
    depth = norm_mix.shape[0]
    t = batch * seq
    heads = rel_bias_table.shape[1]
    dh = diff_lambda_qk.shape[-1]
    qk_dim = (ret_w_in.shape[2] - 2 * ret_w_out.shape[1]) // 2
    v_dim = ret_w_out.shape[1]
    dk = qk_dim // RET_HEADS
    xt = x.reshape(t, d)
    for i in range(depth):
        j = i // N_MIXERS
        if i % N_MIXERS == 0:
            col_scale = jnp.concatenate([jnp.full((d,), dh ** -0.5 * LOG2E, F32), jnp.ones((2 * d,), F32)])
            w_in = (diff_w_in[j] * col_scale[None, :]).astype(BF16)
            qkv = _norm_matmul(xt, norm_mix[i], w_in)
            mix = _diff_attention(qkv, rel_bias_table, diff_lambda_qk[j], diff_subln[j],
                                  _diff_lambda_init(i), batch, seq, heads)
            gate = None
            w_out = diff_w_out[j].astype(BF16)
        else:
            n_in = ret_w_in.shape[2]
            col_scale = jnp.concatenate([jnp.ones((qk_dim,), F32), jnp.full((qk_dim,), dk ** -0.5, F32),
                                         jnp.ones((n_in - 2 * qk_dim,), F32)])
            w_in = (ret_w_in[j] * col_scale[None, :]).astype(BF16)
            qkvg = _norm_matmul(xt, norm_mix[i], w_in, rope=_rope_tables(seq, dk),
                                n_rope=2 * qk_dim // TN, seq=seq)
            mix = _retention(qkvg, batch, seq, qk_dim, v_dim)
            gate = (qkvg, (2 * qk_dim + v_dim) // v_dim)
            w_out = ret_w_out[j].astype(BF16)
        x1, hn, route, route_t, cnt = _out_router(mix, gate, w_out, xt, norm_ffn[i], moe_w_group[i],
                                                  moe_b_group[i], moe_w_expert[i], moe_b_expert[i])
        xt = _moe(x1, hn, route, route_t, cnt, moe_w_gate, moe_w_up, moe_w_down, i,
                  norm_final, i == depth - 1)
    return xt.reshape(batch, seq, d)
```

```python
import functools
import math

import jax
import jax.numpy as jnp
from jax import lax
from jax.experimental import pallas as pl
from jax.experimental.pallas import tpu as pltpu
from jax.experimental.pallas import tpu_sc as plsc

F32 = jnp.float32
BF16 = jnp.bfloat16

N_MIXERS = 2
NUM_BUCKETS = 32
MAX_DISTANCE = 128
RET_HEADS = 4
TOP_K_INNER = 2
RMS_EPS = 1e-6
SUBLN_EPS = 1e-5

LANES = 128
SUBLANES = 8
BF16_SUBLANES = 16
LOG2E = 1.4426950408889634
VMEM_LIMIT = 48 * 1024 * 1024
SC_CORES = 2
SC_SUBCORES = 16

TM = 512
TN = 1024
TQ = 256
ATTN_HEADS = 4
RET_C = 256
ROW_BLOCK = 256
TC = 256
DMA_UNROLL = 8
SC_CHUNK = 32


def _cparams(sem):
    return pltpu.CompilerParams(dimension_semantics=sem, vmem_limit_bytes=VMEM_LIMIT)


def _norm_matmul_kernel(x_ref, g_ref, w_ref, *rest, n_rope):
    if n_rope:
        cos_ref, sa_ref, sb_ref, o_ref = rest
    else:
        (o_ref,) = rest
    x = x_ref[...]
    xn = (x * lax.rsqrt(jnp.mean(x * x, axis=-1, keepdims=True) + RMS_EPS) * g_ref[...]).astype(BF16)
    for j in range(o_ref.shape[1] // TN):
        cols = slice(j * TN, (j + 1) * TN)
        acc = jnp.dot(xn, w_ref[:, cols], preferred_element_type=F32)
        if j < n_rope:
            reps = TN // cos_ref.shape[1]
            c = jnp.tile(cos_ref[...], (1, reps))
            sa = jnp.tile(sa_ref[...], (1, reps))
            sb = jnp.tile(sb_ref[...], (1, reps))
            nxt = pltpu.roll(acc, TN - 1, axis=1)
            prv = pltpu.roll(acc, 1, axis=1)
            acc = acc * c + nxt * sa + prv * sb
        o_ref[:, cols] = acc.astype(o_ref.dtype)


def _norm_matmul(x, g, w, rope=None, n_rope=0, seq=None):
    t, d = x.shape
    n = w.shape[1]
    in_specs = [
        pl.BlockSpec((TM, d), lambda i: (i, 0)),
        pl.BlockSpec((1, d), lambda i: (0, 0)),
        pl.BlockSpec((d, n), lambda i: (0, 0), pipeline_mode=pl.Buffered(1)),
    ]
    args = [x, g.reshape(1, d), w]
    if rope is not None:
        nb = seq // TM
        dk = rope[0].shape[1]
        in_specs += [pl.BlockSpec((TM, dk), lambda i: (i % nb, 0))] * 3
        args += list(rope)
    return pl.pallas_call(
        functools.partial(_norm_matmul_kernel, n_rope=n_rope),
        grid=(t // TM,),
        in_specs=in_specs,
        out_specs=pl.BlockSpec((TM, n), lambda i: (i, 0)),
        out_shape=jax.ShapeDtypeStruct((t, n), BF16),
        compiler_params=_cparams(("parallel",)),
    )(*args)


def _diff_attn_kernel(q_ref, k_ref, v_ref, bias_ref, lam_ref, g_ref, o_ref,
                      vt_ref, m_ref, acc_ref, s_ref, *, lambda_init):
    seq = q_ref.shape[0]
    hp, _, _, tq = bias_ref.shape
    dv = q_ref.shape[1] // hp
    dh = dv // 2
    nq = seq // tq

    def head(hh):
        return slice(hh * dv, (hh + 1) * dv)

    extra = vt_ref.shape[2] - dv
    ones_row = jnp.where(lax.broadcasted_iota(jnp.int32, (extra, tq), 0) == 0, 1.0, 0.0).astype(BF16)
    for hh in range(hp):
        for j in range(nq):
            vt_ref[hh, j, :dv, :] = v_ref[j * tq:(j + 1) * tq, head(hh)].astype(F32).T.astype(BF16)
            vt_ref[hh, j, dv:, :] = ones_row

    lq = lam_ref[...]
    lam = (jnp.exp(jnp.sum(lq[0:1] * lq[1:2], axis=-1, keepdims=True))
           - jnp.exp(jnp.sum(lq[2:3] * lq[3:4], axis=-1, keepdims=True)) + lambda_init)
    lane = lax.broadcasted_iota(jnp.int32, (tq, dv), 1)

    def q_body(qi, carry):
        qrows = pl.ds(pl.multiple_of(qi * tq, tq), tq)
        qs = []
        for hh in range(hp):
            q = q_ref[qrows, head(hh)]
            zero = jnp.zeros_like(q)
            qs.append(jnp.concatenate([jnp.where(lane < dh, q, zero), jnp.where(lane >= dh, q, zero)],
                                      axis=0))

        m_ref[...] = jnp.full(m_ref.shape, -jnp.inf, F32)
        acc_ref[...] = jnp.zeros(acc_ref.shape, F32)

        def score_dots(j):
            rows = pl.ds(pl.multiple_of(j * tq, tq), tq)
            for hh in range(hp):
                s_ref[hh] = lax.dot_general(k_ref[rows, head(hh)], qs[hh], (((1,), (1,)), ((), ())),
                                            preferred_element_type=F32)

        def step(j, tile, prefetch):
            probs, alphas = [], []
            for hh in range(hp):
                s = s_ref[hh]
                if tile is not None:
                    bias = bias_ref[hh, tile]
                    s = s + jnp.concatenate([bias, bias], axis=1)
                m_old = m_ref[hh]
                m_new = jnp.maximum(m_old, jnp.max(s, axis=0, keepdims=True))
                alpha = jnp.exp2(m_old - m_new)
                m_ref[hh] = m_new
                probs.append(jnp.exp2(s - m_new).astype(BF16))
                alphas.append(alpha)
            if prefetch:
                score_dots(j + 1)
            for hh in range(hp):
                acc_ref[hh] = alphas[hh] * acc_ref[hh] + jnp.dot(vt_ref[hh, j], probs[hh],
                                                                 preferred_element_type=F32)

        score_dots(0)

        def far_body(j, c):
            step(j, None, True)
            return c

        lax.fori_loop(0, jnp.maximum(qi - 1, 0), far_body, 0)

        @pl.when(qi >= 1)
        def _():
            step(qi - 1, 1, True)

        step(qi, 0, False)

        for hh in range(hp):
            acc = acc_ref[hh]
            on = acc[:dv] / acc[dv:dv + 1]
            ot = on[:, :tq] - lam * on[:, tq:]
            ot = ot * lax.rsqrt(jnp.mean(ot * ot, axis=0, keepdims=True) + SUBLN_EPS)
            o_ref[qrows, head(hh)] = (ot.T * g_ref[...] * (1.0 - lambda_init)).astype(o_ref.dtype)
        return carry

    lax.fori_loop(0, nq, q_body, 0)


def _t5_bucket(rel):
    max_exact = NUM_BUCKETS // 2
    n = jnp.maximum(rel, 0)
    is_small = n < max_exact
    nf = jnp.maximum(n, max_exact).astype(F32)
    large = max_exact + (jnp.log(nf / max_exact) / math.log(MAX_DISTANCE / max_exact)
                         * (NUM_BUCKETS - max_exact)).astype(jnp.int32)
    large = jnp.minimum(large, NUM_BUCKETS - 1)
    return jnp.where(is_small, n, large)


def _bias_tiles(rel_table):
    heads = rel_table.shape[1]
    span = 2 * TQ + 1
    bucket = _t5_bucket(jnp.arange(span))
    onehot = (bucket[:, None] == jnp.arange(NUM_BUCKETS)[None, :]).astype(F32)
    table = (rel_table.astype(F32) - rel_table[NUM_BUCKETS - 1].astype(F32)[None, :]) * LOG2E
    by_rel = jnp.dot(onehot, table, precision=lax.Precision.HIGHEST).T
    toep = jnp.tile(by_rel, (1, TQ))[:, :TQ * (span - 1)].reshape(heads, TQ, span - 1)
    kj = jnp.arange(TQ)[:, None]
    qi = jnp.arange(TQ)[None, :]
    diag = jnp.where((qi >= kj)[None], toep[:, :, :TQ], -jnp.inf)
    return jnp.stack([diag, toep[:, :, TQ:]], axis=1)


def _diff_attention(qkv, rel_table, lam_qk, subln_g, lambda_init, batch, seq, heads):
    t = qkv.shape[0]
    d = qkv.shape[1] // 3
    dv = d // heads
    hp = ATTN_HEADS
    assert TQ + 1 >= MAX_DISTANCE and seq % TQ == 0 and dv == LANES and heads % hp == 0
    nq = seq // TQ
    groups = heads // hp
    bias = _bias_tiles(rel_table)
    return pl.pallas_call(
        functools.partial(_diff_attn_kernel, lambda_init=lambda_init),
        grid=(batch, groups),
        in_specs=[
            pl.BlockSpec((seq, hp * dv), lambda b, g: (b, g)),
            pl.BlockSpec((seq, hp * dv), lambda b, g: (b, groups + g)),
            pl.BlockSpec((seq, hp * dv), lambda b, g: (b, 2 * groups + g)),
            pl.BlockSpec((hp, 2, TQ, TQ), lambda b, g: (g, 0, 0, 0)),
            pl.BlockSpec(lam_qk.shape, lambda b, g: (0, 0)),
            pl.BlockSpec((1, dv), lambda b, g: (0, 0)),
        ],
        out_specs=pl.BlockSpec((seq, hp * dv), lambda b, g: (b, g)),
        out_shape=jax.ShapeDtypeStruct((t, d), BF16),
        scratch_shapes=[
            pltpu.VMEM((hp, nq, dv + BF16_SUBLANES, TQ), BF16),
            pltpu.VMEM((hp, 1, 2 * TQ), F32),
            pltpu.VMEM((hp, dv + BF16_SUBLANES, 2 * TQ), F32),
            pltpu.VMEM((hp, TQ, 2 * TQ), F32),
        ],
        compiler_params=_cparams(("parallel", "parallel")),
    )(qkv, qkv, qkv, bias, lam_qk.astype(F32), subln_g.reshape(1, dv).astype(F32))


def _retention_kernel(q_ref, k_ref, v_ref, dec_ref, tab_ref, o_ref, state_ref):
    seq = q_ref.shape[0]
    c = dec_ref.shape[1]
    dec = dec_ref[0]
    tab = tab_ref[0]
    qd = tab[:, 0:1]
    kd = tab[:, 1:2]
    cd = tab[0:1, 2:3]
    state_ref[...] = jnp.zeros(state_ref.shape, F32)

    def body(n, carry):
        rows = pl.ds(pl.multiple_of(n * c, c), c)
        q = q_ref[rows, :]
        k = k_ref[rows, :]
        v = v_ref[rows, :]
        s = lax.dot_general(q, k, (((1,), (1,)), ((), ())), preferred_element_type=F32) * dec
        inner = jnp.dot(s.astype(BF16), v, preferred_element_type=F32)
        st = state_ref[...]
        cross = jnp.dot(q, st.astype(BF16), preferred_element_type=F32) * qd
        kt = (k.astype(F32) * kd).T.astype(BF16)
        state_ref[...] = st * cd + jnp.dot(kt, v, preferred_element_type=F32)
        o = inner + cross
        o = o * lax.rsqrt(jnp.mean(o * o, axis=-1, keepdims=True) + RMS_EPS)
        o_ref[rows, :] = o.astype(o_ref.dtype)
        return carry

    lax.fori_loop(0, seq // c, body, 0)


def _retention_tables(heads):
    c = RET_C
    log_gamma = jnp.log(1.0 - 2.0 ** (-5.0 - jnp.arange(heads, dtype=F32)))
    pos = jnp.arange(c, dtype=F32)
    dist = pos[:, None] - pos[None, :]
    inner_decay = jnp.where(dist[None] >= 0,
                            jnp.exp(jnp.maximum(dist, 0.0)[None] * log_gamma[:, None, None]), 0.0)
    q_decay = jnp.exp((pos + 1.0)[None] * log_gamma[:, None])
    k_decay = jnp.exp((c - 1.0 - pos)[None] * log_gamma[:, None])
    chunk_decay = jnp.broadcast_to(jnp.exp(c * log_gamma)[:, None], (heads, c))
    tab = jnp.stack([q_decay, k_decay, chunk_decay], axis=-1)
    tab = jnp.pad(tab, ((0, 0), (0, 0), (0, LANES - 3)))
    return inner_decay, tab


def _retention(qkvg, batch, seq, qk_dim, v_dim):
    t = qkvg.shape[0]
    dk = qk_dim // RET_HEADS
    dvh = v_dim // RET_HEADS
    inner_decay, tab = _retention_tables(RET_HEADS)
    kb = qk_dim // dk
    vb = 2 * qk_dim // dvh
    return pl.pallas_call(
        _retention_kernel,
        grid=(batch, RET_HEADS),
        in_specs=[
            pl.BlockSpec((seq, dk), lambda b, h: (b, h)),
            pl.BlockSpec((seq, dk), lambda b, h: (b, kb + h)),
            pl.BlockSpec((seq, dvh), lambda b, h: (b, vb + h)),
            pl.BlockSpec((1, RET_C, RET_C), lambda b, h: (h, 0, 0)),
            pl.BlockSpec((1, RET_C, LANES), lambda b, h: (h, 0, 0)),
        ],
        out_specs=pl.BlockSpec((seq, dvh), lambda b, h: (b, h)),
        out_shape=jax.ShapeDtypeStruct((t, v_dim), BF16),
        scratch_shapes=[pltpu.VMEM((dk, dvh), F32)],
        compiler_params=_cparams(("parallel", "arbitrary")),
    )(qkvg, qkvg, qkvg, inner_decay, tab)


def _out_router_kernel(*refs, gated, n_groups, n_experts):
    if gated:
        a_ref, gate_ref, w_ref, x_ref, nf_ref, wr_ref, br_ref = refs[:7]
        x1_ref, hn_ref, route_ref, route_t_ref, cnt_ref, carry_ref = refs[7:]
        gate = gate_ref[...]
        a = gate * jax.nn.sigmoid(gate) * a_ref[...]
    else:
        a_ref, w_ref, x_ref, nf_ref, wr_ref, br_ref = refs[:6]
        x1_ref, hn_ref, route_ref, route_t_ref, cnt_ref, carry_ref = refs[6:]
        a = a_ref[...]
    epg = n_experts // n_groups

    @pl.when(pl.program_id(0) == 0)
    def _():
        carry_ref[...] = jnp.zeros(carry_ref.shape, F32)

    x1 = x_ref[...] + jnp.dot(a, w_ref[...], preferred_element_type=F32)
    x1_ref[...] = x1
    hn = x1 * lax.rsqrt(jnp.mean(x1 * x1, axis=-1, keepdims=True) + RMS_EPS) * nf_ref[...]
    hn_ref[...] = hn

    hn_hi = hn.astype(BF16)
    hn_lo = (hn - hn_hi.astype(F32)).astype(BF16)
    logits = (jnp.dot(hn_hi, wr_ref[0], preferred_element_type=F32)
              + jnp.dot(hn_lo, wr_ref[0], preferred_element_type=F32)
              + jnp.dot(hn_hi, wr_ref[1], preferred_element_type=F32)) + br_ref[...]
    tm = logits.shape[0]

    n_rows = -(-(n_groups + n_experts) // SUBLANES) * SUBLANES
    lt = logits.T[:n_rows, :]
    row = lax.broadcasted_iota(jnp.int32, lt.shape, 0)
    big = jnp.int32(LANES)
    neg = jnp.float32(-jnp.inf)

    gl = jnp.where(row < n_groups, lt, neg)
    gmax = jnp.max(gl, axis=0, keepdims=True)
    g_p = 1.0 / jnp.sum(jnp.exp(gl - gmax), axis=0, keepdims=True)
    g_idx = jnp.min(jnp.where(gl == gmax, row, big), axis=0, keepdims=True)

    lo = n_groups + g_idx * epg
    el = jnp.where((row >= lo) & (row < lo + epg), lt, neg)
    v1 = jnp.max(el, axis=0, keepdims=True)
    i1 = jnp.min(jnp.where(el == v1, row, big), axis=0, keepdims=True)
    el2 = jnp.where(row == i1, neg, el)
    v2 = jnp.max(el2, axis=0, keepdims=True)
    i2 = jnp.min(jnp.where(el2 == v2, row, big), axis=0, keepdims=True)
    tt = jnp.exp(v2 - v1)
    w1 = g_p / (1.0 + tt)
    w2 = g_p * tt / (1.0 + tt)
    e1 = i1 - n_groups
    e2 = i2 - n_groups

    erow = lax.broadcasted_iota(jnp.int32, (n_experts, tm), 0)
    oh1 = erow == e1
    oh2 = erow == e2
    onehot = jnp.where(oh1 | oh2, 1.0, 0.0)
    before = lax.broadcasted_iota(jnp.int32, (tm, tm), 0) < lax.broadcasted_iota(jnp.int32, (tm, tm), 1)
    tri = jnp.where(before, 1.0, 0.0).astype(BF16)
    carry = carry_ref[...]
    base = jnp.dot(onehot.astype(BF16), tri, preferred_element_type=F32) + carry[:, 0:1]
    rank1 = jnp.sum(jnp.where(oh1, base, 0.0), axis=0, keepdims=True)
    rank2 = jnp.sum(jnp.where(oh2, base, 0.0), axis=0, keepdims=True)
    carry = carry + jnp.sum(onehot, axis=1, keepdims=True)
    carry_ref[...] = carry
    cnt_ref[...] = carry

    frow = lax.broadcasted_iota(jnp.int32, (LANES, tm), 0)
    out = jnp.zeros((LANES, tm), F32)
    for idx, val in enumerate((e1.astype(F32), e2.astype(F32), w1, w2, rank1, rank2)):
        out = jnp.where(frow == idx, val, out)
    route_t_ref[...] = out[:route_t_ref.shape[0], :]
    route_ref[...] = out.T


def _out_router(a, gate, w, x, nf, w_group, b_group, w_expert, b_expert):
    t, d = x.shape
    k = w.shape[0]
    n_groups = w_group.shape[1]
    n_experts = w_expert.shape[1]
    assert n_groups + n_experts <= LANES
    pad = LANES - n_groups - n_experts
    wr = jnp.pad(jnp.concatenate([w_group, w_expert], axis=1).astype(F32), ((0, 0), (0, pad)))
    wr_hi = wr.astype(BF16)
    wr = jnp.stack([wr_hi, (wr - wr_hi.astype(F32)).astype(BF16)])
    br = jnp.pad(jnp.concatenate([b_group, b_expert]).astype(F32), (0, pad)).reshape(1, LANES)
    gated = gate is not None
    row = lambda i: (i, 0)
    fixed = lambda i: (0, 0)
    in_specs = [pl.BlockSpec((TM, k), row)]
    args = [a]
    if gated:
        gate_arr, gate_col = gate
        in_specs.append(pl.BlockSpec((TM, k), lambda i: (i, gate_col)))
        args.append(gate_arr)
    in_specs += [
        pl.BlockSpec((k, d), fixed),
        pl.BlockSpec((TM, d), row),
        pl.BlockSpec((1, d), fixed),
        pl.BlockSpec((2, d, LANES), lambda i: (0, 0, 0)),
        pl.BlockSpec((1, LANES), fixed),
    ]
    args += [w, x, nf.reshape(1, d), wr, br]
    return pl.pallas_call(
        functools.partial(_out_router_kernel, gated=gated, n_groups=n_groups, n_experts=n_experts),
        grid=(t // TM,),
        in_specs=in_specs,
        out_specs=[
            pl.BlockSpec((TM, d), row),
            pl.BlockSpec((TM, d), row),
            pl.BlockSpec((TM, LANES), row),
            pl.BlockSpec((SUBLANES, TM), lambda i: (0, i)),
            pl.BlockSpec((n_experts, LANES), fixed),
        ],
        out_shape=[
            jax.ShapeDtypeStruct((t, d), F32),
            jax.ShapeDtypeStruct((t, d), F32),
            jax.ShapeDtypeStruct((t, LANES), F32),
            jax.ShapeDtypeStruct((SUBLANES, t), F32),
            jax.ShapeDtypeStruct((n_experts, LANES), F32),
        ],
        scratch_shapes=[pltpu.VMEM((n_experts, LANES), F32)],
        compiler_params=_cparams(("arbitrary",)),
    )(*args)


def _dispatch_kernel(d1_ref, d2_ref, zs_ref, zf_ref, nu_ref, hn_ref, xs_ref, zero_ref, sem, zsem):
    tm = hn_ref.shape[0]
    rb = zero_ref.shape[0]
    nb = xs_ref.shape[0] // rb
    base = pl.program_id(0) * tm

    @pl.when(pl.program_id(0) == 0)
    def _():
        zero_ref[...] = jnp.zeros(zero_ref.shape, zero_ref.dtype)

        def zcopy(row0):
            rows = pl.ds(pl.multiple_of(row0, rb), rb)
            return pltpu.make_async_copy(zero_ref, xs_ref.at[rows, :], zsem)

        def zissue(e, carry):
            @pl.when(zf_ref[e] > 0)
            def _():
                zcopy(zs_ref[e]).start()
            return carry

        def zwait(e, carry):
            @pl.when(zf_ref[e] > 0)
            def _():
                zcopy(zs_ref[e]).wait()
            return carry

        def tissue(b, carry):
            zcopy(b * rb).start()
            return carry

        def twait(b, carry):
            zcopy(b * rb).wait()
            return carry

        lax.fori_loop(0, zs_ref.shape[0], zissue, 0)
        lax.fori_loop(nu_ref[0], nb, tissue, 0)
        lax.fori_loop(0, zs_ref.shape[0], zwait, 0)
        lax.fori_loop(nu_ref[0], nb, twait, 0)

    def issue(r, carry):
        src = hn_ref.at[pl.ds(r, 1), :]
        pltpu.make_async_copy(src, xs_ref.at[pl.ds(d1_ref[base + r], 1), :], sem.at[0]).start()
        pltpu.make_async_copy(src, xs_ref.at[pl.ds(d2_ref[base + r], 1), :], sem.at[1]).start()
        return carry

    lax.fori_loop(0, tm, issue, 0, unroll=DMA_UNROLL)
    for s in range(2):
        pltpu.make_async_copy(hn_ref, xs_ref.at[pl.ds(0, tm), :], sem.at[s]).wait()


def _dispatch(hn, d1, d2, zstart, zflag, n_used, n_rows):
    t, d = hn.shape
    grid_spec = pltpu.PrefetchScalarGridSpec(
        num_scalar_prefetch=5,
        grid=(t // TC,),
        in_specs=[pl.BlockSpec((TC, d), lambda i, *_: (i, 0))],
        out_specs=pl.BlockSpec(memory_space=pl.ANY),
        scratch_shapes=[pltpu.VMEM((ROW_BLOCK, d), hn.dtype),
                        pltpu.SemaphoreType.DMA((2,)), pltpu.SemaphoreType.DMA(())],
    )
    return pl.pallas_call(
        _dispatch_kernel,
        grid_spec=grid_spec,
        out_shape=jax.ShapeDtypeStruct((n_rows, d), hn.dtype),
        compiler_params=_cparams(("arbitrary",)),
    )(d1, d2, zstart, zflag, n_used, hn)


def _expert_kernel(be_ref, nu_ref, x_ref, wg_ref, wu_ref, wd_ref, y_ref, wgb_ref, wub_ref, wdb_ref):
    i = pl.program_id(0)
    used = i < nu_ref[0]
    new_expert = (i == 0) | (be_ref[i] != be_ref[jnp.maximum(i - 1, 0)])

    @pl.when(used & new_expert)
    def _():
        wgb_ref[...] = wg_ref[0, 0].astype(BF16)
        wub_ref[...] = wu_ref[0, 0].astype(BF16)
        wdb_ref[...] = wd_ref[0, 0].astype(BF16)

    @pl.when(used)
    def _():
        x = x_ref[...].astype(BF16)
        g = jnp.dot(x, wgb_ref[...], preferred_element_type=F32)
        u = jnp.dot(x, wub_ref[...], preferred_element_type=F32)
        hmid = (g * jax.nn.sigmoid(g) * u).astype(BF16)
        y_ref[...] = jnp.dot(hmid, wdb_ref[...], preferred_element_type=F32)

    @pl.when(jnp.logical_not(used))
    def _():
        y_ref[...] = jnp.zeros(y_ref.shape, y_ref.dtype)


def _experts(xs, block_e, n_used, w_gate, w_up, w_down, layer):
    n_rows, d = xs.shape
    ff = w_gate.shape[3]
    nb = n_rows // ROW_BLOCK
    last = lambda i, nu: jnp.maximum(jnp.minimum(i, nu[0] - 1), 0)
    blk = lambda i, be, nu: (last(i, nu), 0)
    wsel = lambda i, be, nu: (layer, be[last(i, nu)], 0, 0)
    grid_spec = pltpu.PrefetchScalarGridSpec(
        num_scalar_prefetch=2,
        grid=(nb,),
        in_specs=[
            pl.BlockSpec((ROW_BLOCK, d), blk),
            pl.BlockSpec((1, 1, d, ff), wsel),
            pl.BlockSpec((1, 1, d, ff), wsel),
            pl.BlockSpec((1, 1, ff, d), wsel),
        ],
        out_specs=pl.BlockSpec((ROW_BLOCK, d), lambda i, be, nu: (i, 0)),
        scratch_shapes=[pltpu.VMEM((d, ff), BF16), pltpu.VMEM((d, ff), BF16), pltpu.VMEM((ff, d), BF16)],
    )
    return pl.pallas_call(
        _expert_kernel,
        grid_spec=grid_spec,
        out_shape=jax.ShapeDtypeStruct((n_rows, d), F32),
        compiler_params=_cparams(("arbitrary",)),
    )(block_e, n_used, xs, w_gate, w_up, w_down)


def _sc_gather_rows(table, idx):
    b = idx.shape[0]
    d = table.shape[1]
    workers = SC_CORES * SC_SUBCORES
    per_w = b // workers
    assert b % (workers * SC_CHUNK) == 0
    mesh = plsc.VectorSubcoreMesh(core_axis_name="c", subcore_axis_name="s",
                                  num_cores=SC_CORES, num_subcores=SC_SUBCORES)

    @functools.partial(
        pl.kernel, mesh=mesh,
        out_type=jax.ShapeDtypeStruct((b, d), table.dtype),
        scratch_types=[pltpu.VMEM((per_w,), jnp.int32),
                       pltpu.VMEM((SC_CHUNK, d), table.dtype),
                       pltpu.SemaphoreType.DMA],
    )
    def gather(table_hbm, idx_hbm, out_hbm, idx_v, rows_v, sem):
        wid = lax.axis_index("s") * SC_CORES + lax.axis_index("c")
        base = wid * per_w
        pltpu.sync_copy(idx_hbm.at[pl.ds(base, per_w)], idx_v)

        @pl.loop(0, per_w // SC_CHUNK)
        def _(c):
            pltpu.async_copy(table_hbm.at[idx_v.at[pl.ds(c * SC_CHUNK, SC_CHUNK)]], rows_v, sem).wait()
            pltpu.sync_copy(rows_v, out_hbm.at[pl.ds(base + c * SC_CHUNK, SC_CHUNK)])

    return gather(table, idx)


def _combine_kernel(x_ref, route_ref, y1_ref, y2_ref, nf_ref, o_ref, *, final_norm):
    route = route_ref[...]
    y = x_ref[...] + (route[:, 2:3] * y1_ref[...] + route[:, 3:4] * y2_ref[...])
    if final_norm:
        y = y * lax.rsqrt(jnp.mean(y * y, axis=-1, keepdims=True) + RMS_EPS) * nf_ref[...]
    o_ref[...] = y


def _combine(x1, route, ys, d1, d2, nf, final_norm):
    t, d = x1.shape
    yg = _sc_gather_rows(ys, jnp.concatenate([d1, d2]))
    slot2 = t // TM
    row = lambda i: (i, 0)
    return pl.pallas_call(
        functools.partial(_combine_kernel, final_norm=final_norm),
        grid=(t // TM,),
        in_specs=[
            pl.BlockSpec((TM, d), row),
            pl.BlockSpec((TM, LANES), row),
            pl.BlockSpec((TM, d), row),
            pl.BlockSpec((TM, d), lambda i: (slot2 + i, 0)),
            pl.BlockSpec((1, d), lambda i: (0, 0)),
        ],
        out_specs=pl.BlockSpec((TM, d), row),
        out_shape=jax.ShapeDtypeStruct((t, d), F32),
        compiler_params=_cparams(("parallel",)),
    )(x1, route, yg, yg, nf.reshape(1, d).astype(F32))


def _moe(x1, hn, route, route_t, cnt, w_gate, w_up, w_down, layer, nf, final_norm):
    t, d = x1.shape
    n_experts = w_gate.shape[1]
    ids = jnp.arange(n_experts, dtype=jnp.int32)
    e1, e2, rank1, rank2 = (route_t[f].astype(jnp.int32) for f in (0, 1, 4, 5))
    counts = cnt[:, 0].astype(jnp.int32)
    padded = ((counts + ROW_BLOCK - 1) // ROW_BLOCK) * ROW_BLOCK
    pad_end = jnp.cumsum(padded)
    pad_start = pad_end - padded

    def dest(e, rank):
        return jnp.sum(jnp.where(e[:, None] == ids[None, :], pad_start[None, :], 0), axis=1) + rank

    d1 = dest(e1, rank1)
    d2 = dest(e2, rank2)
    nb = (t * TOP_K_INNER) // ROW_BLOCK + n_experts
    n_used = (pad_end[-1:] // ROW_BLOCK).astype(jnp.int32)
    first_row = jnp.arange(nb, dtype=jnp.int32) * ROW_BLOCK
    block_e = jnp.sum((pad_end[None, :] <= first_row[:, None]).astype(jnp.int32), axis=1)
    block_e = jnp.minimum(block_e, n_experts - 1)
    zstart = jnp.maximum(pad_end - ROW_BLOCK, 0).astype(jnp.int32)
    zflag = (padded > 0).astype(jnp.int32)
    xs = _dispatch(hn, d1, d2, zstart, zflag, n_used, nb * ROW_BLOCK)
    ys = _experts(xs, block_e, n_used, w_gate, w_up, w_down, layer)
    return _combine(x1, route, ys, d1, d2, nf, final_norm)


def _diff_lambda_init(layer_idx):
    return 0.8 - 0.6 * math.exp(-0.3 * layer_idx)


def _rope_tables(seq, dk):
    angle = 1.0 / (10000.0 ** jnp.linspace(0.0, 1.0, dk // 2, dtype=F32))
    angle = jnp.repeat(angle, 2)
    phase = jnp.arange(seq, dtype=F32)[:, None] * angle[None, :]
    sin, cos = jnp.sin(phase), jnp.cos(phase)
    even = (jnp.arange(dk) % 2 == 0)[None, :]
    return cos, jnp.where(even, -sin, 0.0), jnp.where(even, 0.0, sin)


def kernel(x, rel_bias_table, norm_mix, norm_ffn, norm_final, diff_w_in, diff_lambda_qk, diff_subln,
           diff_w_out, ret_w_in, ret_w_out, moe_w_group, moe_b_group, moe_w_expert, moe_b_expert,
           moe_w_gate, moe_w_up, moe_w_down):
    batch, seq, d = x.shape
    depth = norm_mix.shape[0]
    t = batch * seq
    heads = rel_bias_table.shape[1]
    dh = diff_lambda_qk.shape[-1]
    qk_dim = (ret_w_in.shape[2] - 2 * ret_w_out.shape[1]) // 2
    v_dim = ret_w_out.shape[1]
    dk = qk_dim // RET_HEADS
    xt = x.reshape(t, d)
    for i in range(depth):
        j = i // N_MIXERS
        if i % N_MIXERS == 0:
            col_scale = jnp.concatenate([jnp.full((d,), dh ** -0.5 * LOG2E, F32), jnp.ones((2 * d,), F32)])
            w_in = (diff_w_in[j] * col_scale[None, :]).astype(BF16)
            qkv = _norm_matmul(xt, norm_mix[i], w_in)
            mix = _diff_attention(qkv, rel_bias_table, diff_lambda_qk[j], diff_subln[j],
                                  _diff_lambda_init(i), batch, seq, heads)
            gate = None
            w_out = diff_w_out[j].astype(BF16)
        else:
            n_in = ret_w_in.shape[2]
            col_scale = jnp.concatenate([jnp.ones((qk_dim,), F32), jnp.full((qk_dim,), dk ** -0.5, F32),
                                         jnp.ones((n_in - 2 * qk_dim,), F32)])
            w_in = (ret_w_in[j] * col_scale[None, :]).astype(BF16)
            qkvg = _norm_matmul(xt, norm_mix[i], w_in, rope=_rope_tables(seq, dk),
                                n_rope=2 * qk_dim // TN, seq=seq)
            mix = _retention(qkvg, batch, seq, qk_dim, v_dim)
            gate = (qkvg, (2 * qk_dim + v_dim) // v_dim)
            w_out = ret_w_out[j].astype(BF16)
        x1, hn, route, route_t, cnt = _out_router(mix, gate, w_out, xt, norm_ffn[i], moe_w_group[i],
                                                  moe_b_group[i], moe_w_expert[i], moe_b_expert[i])
        xt = _moe(x1, hn, route, route_t, cnt, moe_w_gate, moe_w_up, moe_w_down, i,
                  norm_final, i == depth - 1)
    return xt.reshape(batch, seq, d)
```

```python
import functools
import math

import jax
import jax.numpy as jnp
from jax import lax
from jax.experimental import pallas as pl
from jax.experimental.pallas import tpu as pltpu
from jax.experimental.pallas import tpu_sc as plsc

F32 = jnp.float32
BF16 = jnp.bfloat16

N_MIXERS = 2
NUM_BUCKETS = 32
MAX_DISTANCE = 128
RET_HEADS = 4
TOP_K_INNER = 2
RMS_EPS = 1e-6
SUBLN_EPS = 1e-5

LANES = 128
SUBLANES = 8
BF16_SUBLANES = 16
LOG2E = 1.4426950408889634
VMEM_LIMIT = 48 * 1024 * 1024
SC_CORES = 2
SC_SUBCORES = 16

TM = 512
TN = 1024
TQ = 256
ATTN_HEADS = 4
RET_C = 256
ROW_BLOCK = 256
SC_CHUNK = 32


def _cparams(sem):
    return pltpu.CompilerParams(dimension_semantics=sem, vmem_limit_bytes=VMEM_LIMIT)


def _norm_matmul_kernel(x_ref, g_ref, w_ref, *rest, n_rope):
    if n_rope:
        cos_ref, sa_ref, sb_ref, o_ref = rest
    else:
        (o_ref,) = rest
    x = x_ref[...]
    xn = (x * lax.rsqrt(jnp.mean(x * x, axis=-1, keepdims=True) + RMS_EPS) * g_ref[...]).astype(BF16)
    for j in range(o_ref.shape[1] // TN):
        cols = slice(j * TN, (j + 1) * TN)
        acc = jnp.dot(xn, w_ref[:, cols], preferred_element_type=F32)
        if j < n_rope:
            reps = TN // cos_ref.shape[1]
            c = jnp.tile(cos_ref[...], (1, reps))
            sa = jnp.tile(sa_ref[...], (1, reps))
            sb = jnp.tile(sb_ref[...], (1, reps))
            nxt = pltpu.roll(acc, TN - 1, axis=1)
            prv = pltpu.roll(acc, 1, axis=1)
            acc = acc * c + nxt * sa + prv * sb
        o_ref[:, cols] = acc.astype(o_ref.dtype)


def _norm_matmul(x, g, w, rope=None, n_rope=0, seq=None):
    t, d = x.shape
    n = w.shape[1]
    in_specs = [
        pl.BlockSpec((TM, d), lambda i: (i, 0)),
        pl.BlockSpec((1, d), lambda i: (0, 0)),
        pl.BlockSpec((d, n), lambda i: (0, 0), pipeline_mode=pl.Buffered(1)),
    ]
    args = [x, g.reshape(1, d), w]
    if rope is not None:
        nb = seq // TM
        dk = rope[0].shape[1]
        in_specs += [pl.BlockSpec((TM, dk), lambda i: (i % nb, 0))] * 3
        args += list(rope)
    return pl.pallas_call(
        functools.partial(_norm_matmul_kernel, n_rope=n_rope),
        grid=(t // TM,),
        in_specs=in_specs,
        out_specs=pl.BlockSpec((TM, n), lambda i: (i, 0)),
        out_shape=jax.ShapeDtypeStruct((t, n), BF16),
        compiler_params=_cparams(("parallel",)),
    )(*args)


def _diff_attn_kernel(q_ref, k_ref, v_ref, bias_ref, lam_ref, g_ref, o_ref,
                      vt_ref, m_ref, acc_ref, s_ref, *, lambda_init):
    seq = q_ref.shape[0]
    hp, _, _, tq = bias_ref.shape
    dv = q_ref.shape[1] // hp
    dh = dv // 2
    nq = seq // tq

    def head(hh):
        return slice(hh * dv, (hh + 1) * dv)

    extra = vt_ref.shape[2] - dv
    ones_row = jnp.where(lax.broadcasted_iota(jnp.int32, (extra, tq), 0) == 0, 1.0, 0.0).astype(BF16)
    for hh in range(hp):
        for j in range(nq):
            vt_ref[hh, j, :dv, :] = v_ref[j * tq:(j + 1) * tq, head(hh)].astype(F32).T.astype(BF16)
            vt_ref[hh, j, dv:, :] = ones_row

    lq = lam_ref[...]
    lam = (jnp.exp(jnp.sum(lq[0:1] * lq[1:2], axis=-1, keepdims=True))
           - jnp.exp(jnp.sum(lq[2:3] * lq[3:4], axis=-1, keepdims=True)) + lambda_init)
    lane = lax.broadcasted_iota(jnp.int32, (tq, dv), 1)

    def q_body(qi, carry):
        qrows = pl.ds(pl.multiple_of(qi * tq, tq), tq)
        qs = []
        for hh in range(hp):
            q = q_ref[qrows, head(hh)]
            zero = jnp.zeros_like(q)
            qs.append(jnp.concatenate([jnp.where(lane < dh, q, zero), jnp.where(lane >= dh, q, zero)],
                                      axis=0))

        m_ref[...] = jnp.full(m_ref.shape, -jnp.inf, F32)
        acc_ref[...] = jnp.zeros(acc_ref.shape, F32)

        def score_dots(j):
            rows = pl.ds(pl.multiple_of(j * tq, tq), tq)
            for hh in range(hp):
                s_ref[hh] = lax.dot_general(k_ref[rows, head(hh)], qs[hh], (((1,), (1,)), ((), ())),
                                            preferred_element_type=F32)

        def step(j, tile, prefetch):
            probs, alphas = [], []
            for hh in range(hp):
                s = s_ref[hh]
                if tile is not None:
                    bias = bias_ref[hh, tile]
                    s = s + jnp.concatenate([bias, bias], axis=1)
                m_old = m_ref[hh]
                m_new = jnp.maximum(m_old, jnp.max(s, axis=0, keepdims=True))
                alpha = jnp.exp2(m_old - m_new)
                m_ref[hh] = m_new
                probs.append(jnp.exp2(s - m_new).astype(BF16))
                alphas.append(alpha)
            if prefetch:
                score_dots(j + 1)
            for hh in range(hp):
                acc_ref[hh] = alphas[hh] * acc_ref[hh] + jnp.dot(vt_ref[hh, j], probs[hh],
                                                                 preferred_element_type=F32)

        score_dots(0)

        def far_body(j, c):
            step(j, None, True)
            return c

        lax.fori_loop(0, jnp.maximum(qi - 1, 0), far_body, 0)

        @pl.when(qi >= 1)
        def _():
            step(qi - 1, 1, True)

        step(qi, 0, False)

        for hh in range(hp):
            acc = acc_ref[hh]
            on = acc[:dv] / acc[dv:dv + 1]
            ot = on[:, :tq] - lam * on[:, tq:]
            ot = ot * lax.rsqrt(jnp.mean(ot * ot, axis=0, keepdims=True) + SUBLN_EPS)
            o_ref[qrows, head(hh)] = (ot.T * g_ref[...] * (1.0 - lambda_init)).astype(o_ref.dtype)
        return carry

    lax.fori_loop(0, nq, q_body, 0)


def _t5_bucket(rel):
    max_exact = NUM_BUCKETS // 2
    n = jnp.maximum(rel, 0)
    is_small = n < max_exact
    nf = jnp.maximum(n, max_exact).astype(F32)
    large = max_exact + (jnp.log(nf / max_exact) / math.log(MAX_DISTANCE / max_exact)
                         * (NUM_BUCKETS - max_exact)).astype(jnp.int32)
    large = jnp.minimum(large, NUM_BUCKETS - 1)
    return jnp.where(is_small, n, large)


def _bias_tiles(rel_table):
    heads = rel_table.shape[1]
    span = 2 * TQ + 1
    bucket = _t5_bucket(jnp.arange(span))
    onehot = (bucket[:, None] == jnp.arange(NUM_BUCKETS)[None, :]).astype(F32)
    table = (rel_table.astype(F32) - rel_table[NUM_BUCKETS - 1].astype(F32)[None, :]) * LOG2E
    by_rel = jnp.dot(onehot, table, precision=lax.Precision.HIGHEST).T
    toep = jnp.tile(by_rel, (1, TQ))[:, :TQ * (span - 1)].reshape(heads, TQ, span - 1)
    kj = jnp.arange(TQ)[:, None]
    qi = jnp.arange(TQ)[None, :]
    diag = jnp.where((qi >= kj)[None], toep[:, :, :TQ], -jnp.inf)
    return jnp.stack([diag, toep[:, :, TQ:]], axis=1)


def _diff_attention(qkv, rel_table, lam_qk, subln_g, lambda_init, batch, seq, heads):
    t = qkv.shape[0]
    d = qkv.shape[1] // 3
    dv = d // heads
    hp = ATTN_HEADS
    assert TQ + 1 >= MAX_DISTANCE and seq % TQ == 0 and dv == LANES and heads % hp == 0
    nq = seq // TQ
    groups = heads // hp
    bias = _bias_tiles(rel_table)
    return pl.pallas_call(
        functools.partial(_diff_attn_kernel, lambda_init=lambda_init),
        grid=(batch, groups),
        in_specs=[
            pl.BlockSpec((seq, hp * dv), lambda b, g: (b, g)),
            pl.BlockSpec((seq, hp * dv), lambda b, g: (b, groups + g)),
            pl.BlockSpec((seq, hp * dv), lambda b, g: (b, 2 * groups + g)),
            pl.BlockSpec((hp, 2, TQ, TQ), lambda b, g: (g, 0, 0, 0)),
            pl.BlockSpec(lam_qk.shape, lambda b, g: (0, 0)),
            pl.BlockSpec((1, dv), lambda b, g: (0, 0)),
        ],
        out_specs=pl.BlockSpec((seq, hp * dv), lambda b, g: (b, g)),
        out_shape=jax.ShapeDtypeStruct((t, d), BF16),
        scratch_shapes=[
            pltpu.VMEM((hp, nq, dv + BF16_SUBLANES, TQ), BF16),
            pltpu.VMEM((hp, 1, 2 * TQ), F32),
            pltpu.VMEM((hp, dv + BF16_SUBLANES, 2 * TQ), F32),
            pltpu.VMEM((hp, TQ, 2 * TQ), F32),
        ],
        compiler_params=_cparams(("parallel", "parallel")),
    )(qkv, qkv, qkv, bias, lam_qk.astype(F32), subln_g.reshape(1, dv).astype(F32))


def _retention_kernel(q_ref, k_ref, v_ref, dec_ref, tab_ref, o_ref, state_ref):
    seq = q_ref.shape[0]
    c = dec_ref.shape[1]
    dec = dec_ref[0]
    tab = tab_ref[0]
    qd = tab[:, 0:1]
    kd = tab[:, 1:2]
    cd = tab[0:1, 2:3]
    state_ref[...] = jnp.zeros(state_ref.shape, F32)

    def body(n, carry):
        rows = pl.ds(pl.multiple_of(n * c, c), c)
        q = q_ref[rows, :]
        k = k_ref[rows, :]
        v = v_ref[rows, :]
        s = lax.dot_general(q, k, (((1,), (1,)), ((), ())), preferred_element_type=F32) * dec
        inner = jnp.dot(s.astype(BF16), v, preferred_element_type=F32)
        st = state_ref[...]
        cross = jnp.dot(q, st.astype(BF16), preferred_element_type=F32) * qd
        kt = (k.astype(F32) * kd).T.astype(BF16)
        state_ref[...] = st * cd + jnp.dot(kt, v, preferred_element_type=F32)
        o = inner + cross
        o = o * lax.rsqrt(jnp.mean(o * o, axis=-1, keepdims=True) + RMS_EPS)
        o_ref[rows, :] = o.astype(o_ref.dtype)
        return carry

    lax.fori_loop(0, seq // c, body, 0)


def _retention_tables(heads):
    c = RET_C
    log_gamma = jnp.log(1.0 - 2.0 ** (-5.0 - jnp.arange(heads, dtype=F32)))
    pos = jnp.arange(c, dtype=F32)
    dist = pos[:, None] - pos[None, :]
    inner_decay = jnp.where(dist[None] >= 0,
                            jnp.exp(jnp.maximum(dist, 0.0)[None] * log_gamma[:, None, None]), 0.0)
    q_decay = jnp.exp((pos + 1.0)[None] * log_gamma[:, None])
    k_decay = jnp.exp((c - 1.0 - pos)[None] * log_gamma[:, None])
    chunk_decay = jnp.broadcast_to(jnp.exp(c * log_gamma)[:, None], (heads, c))
    tab = jnp.stack([q_decay, k_decay, chunk_decay], axis=-1)
    tab = jnp.pad(tab, ((0, 0), (0, 0), (0, LANES - 3)))
    return inner_decay, tab


def _retention(qkvg, batch, seq, qk_dim, v_dim):
    t = qkvg.shape[0]
    dk = qk_dim // RET_HEADS
    dvh = v_dim // RET_HEADS
    inner_decay, tab = _retention_tables(RET_HEADS)
    kb = qk_dim // dk
    vb = 2 * qk_dim // dvh
    return pl.pallas_call(
        _retention_kernel,
        grid=(batch, RET_HEADS),
        in_specs=[
            pl.BlockSpec((seq, dk), lambda b, h: (b, h)),
            pl.BlockSpec((seq, dk), lambda b, h: (b, kb + h)),
            pl.BlockSpec((seq, dvh), lambda b, h: (b, vb + h)),
            pl.BlockSpec((1, RET_C, RET_C), lambda b, h: (h, 0, 0)),
            pl.BlockSpec((1, RET_C, LANES), lambda b, h: (h, 0, 0)),
        ],
        out_specs=pl.BlockSpec((seq, dvh), lambda b, h: (b, h)),
        out_shape=jax.ShapeDtypeStruct((t, v_dim), BF16),
        scratch_shapes=[pltpu.VMEM((dk, dvh), F32)],
        compiler_params=_cparams(("parallel", "arbitrary")),
    )(qkvg, qkvg, qkvg, inner_decay, tab)


def _out_router_kernel(*refs, gated, n_groups, n_experts):
    if gated:
        a_ref, gate_ref, w_ref, x_ref, nf_ref, wr_ref, br_ref = refs[:7]
        x1_ref, hn_ref, route_ref, route_t_ref, cnt_ref, carry_ref = refs[7:]
        gate = gate_ref[...]
        a = gate * jax.nn.sigmoid(gate) * a_ref[...]
    else:
        a_ref, w_ref, x_ref, nf_ref, wr_ref, br_ref = refs[:6]
        x1_ref, hn_ref, route_ref, route_t_ref, cnt_ref, carry_ref = refs[6:]
        a = a_ref[...]
    epg = n_experts // n_groups

    @pl.when(pl.program_id(0) == 0)
    def _():
        carry_ref[...] = jnp.zeros(carry_ref.shape, F32)

    x1 = x_ref[...] + jnp.dot(a, w_ref[...], preferred_element_type=F32)
    x1_ref[...] = x1
    hn = x1 * lax.rsqrt(jnp.mean(x1 * x1, axis=-1, keepdims=True) + RMS_EPS) * nf_ref[...]
    hn_ref[...] = hn

    hn_hi = hn.astype(BF16)
    hn_lo = (hn - hn_hi.astype(F32)).astype(BF16)
    logits = (jnp.dot(hn_hi, wr_ref[0], preferred_element_type=F32)
              + jnp.dot(hn_lo, wr_ref[0], preferred_element_type=F32)
              + jnp.dot(hn_hi, wr_ref[1], preferred_element_type=F32)) + br_ref[...]
    tm = logits.shape[0]

    n_rows = -(-(n_groups + n_experts) // SUBLANES) * SUBLANES
    lt = logits.T[:n_rows, :]
    row = lax.broadcasted_iota(jnp.int32, lt.shape, 0)
    big = jnp.int32(LANES)
    neg = jnp.float32(-jnp.inf)

    gl = jnp.where(row < n_groups, lt, neg)
    gmax = jnp.max(gl, axis=0, keepdims=True)
    g_p = 1.0 / jnp.sum(jnp.exp(gl - gmax), axis=0, keepdims=True)
    g_idx = jnp.min(jnp.where(gl == gmax, row, big), axis=0, keepdims=True)

    lo = n_groups + g_idx * epg
    el = jnp.where((row >= lo) & (row < lo + epg), lt, neg)
    v1 = jnp.max(el, axis=0, keepdims=True)
    i1 = jnp.min(jnp.where(el == v1, row, big), axis=0, keepdims=True)
    el2 = jnp.where(row == i1, neg, el)
    v2 = jnp.max(el2, axis=0, keepdims=True)
    i2 = jnp.min(jnp.where(el2 == v2, row, big), axis=0, keepdims=True)
    tt = jnp.exp(v2 - v1)
    w1 = g_p / (1.0 + tt)
    w2 = g_p * tt / (1.0 + tt)
    e1 = i1 - n_groups
    e2 = i2 - n_groups

    erow = lax.broadcasted_iota(jnp.int32, (n_experts, tm), 0)
    oh1 = erow == e1
    oh2 = erow == e2
    onehot = jnp.where(oh1 | oh2, 1.0, 0.0)
    before = lax.broadcasted_iota(jnp.int32, (tm, tm), 0) < lax.broadcasted_iota(jnp.int32, (tm, tm), 1)
    tri = jnp.where(before, 1.0, 0.0).astype(BF16)
    carry = carry_ref[...]
    base = jnp.dot(onehot.astype(BF16), tri, preferred_element_type=F32) + carry[:, 0:1]
    rank1 = jnp.sum(jnp.where(oh1, base, 0.0), axis=0, keepdims=True)
    rank2 = jnp.sum(jnp.where(oh2, base, 0.0), axis=0, keepdims=True)
    carry = carry + jnp.sum(onehot, axis=1, keepdims=True)
    carry_ref[...] = carry
    cnt_ref[...] = carry

    frow = lax.broadcasted_iota(jnp.int32, (LANES, tm), 0)
    out = jnp.zeros((LANES, tm), F32)
    for idx, val in enumerate((e1.astype(F32), e2.astype(F32), w1, w2, rank1, rank2)):
        out = jnp.where(frow == idx, val, out)
    route_t_ref[...] = out[:route_t_ref.shape[0], :]
    route_ref[...] = out.T


def _out_router(a, gate, w, x, nf, w_group, b_group, w_expert, b_expert):
    t, d = x.shape
    k = w.shape[0]
    n_groups = w_group.shape[1]
    n_experts = w_expert.shape[1]
    assert n_groups + n_experts <= LANES
    pad = LANES - n_groups - n_experts
    wr = jnp.pad(jnp.concatenate([w_group, w_expert], axis=1).astype(F32), ((0, 0), (0, pad)))
    wr_hi = wr.astype(BF16)
    wr = jnp.stack([wr_hi, (wr - wr_hi.astype(F32)).astype(BF16)])
    br = jnp.pad(jnp.concatenate([b_group, b_expert]).astype(F32), (0, pad)).reshape(1, LANES)
    gated = gate is not None
    row = lambda i: (i, 0)
    fixed = lambda i: (0, 0)
    in_specs = [pl.BlockSpec((TM, k), row)]
    args = [a]
    if gated:
        gate_arr, gate_col = gate
        in_specs.append(pl.BlockSpec((TM, k), lambda i: (i, gate_col)))
        args.append(gate_arr)
    in_specs += [
        pl.BlockSpec((k, d), fixed),
        pl.BlockSpec((TM, d), row),
        pl.BlockSpec((1, d), fixed),
        pl.BlockSpec((2, d, LANES), lambda i: (0, 0, 0)),
        pl.BlockSpec((1, LANES), fixed),
    ]
    args += [w, x, nf.reshape(1, d), wr, br]
    return pl.pallas_call(
        functools.partial(_out_router_kernel, gated=gated, n_groups=n_groups, n_experts=n_experts),
        grid=(t // TM,),
        in_specs=in_specs,
        out_specs=[
            pl.BlockSpec((TM, d), row),
            pl.BlockSpec((TM, d), row),
            pl.BlockSpec((TM, LANES), row),
            pl.BlockSpec((SUBLANES, TM), lambda i: (0, i)),
            pl.BlockSpec((n_experts, LANES), fixed),
        ],
        out_shape=[
            jax.ShapeDtypeStruct((t, d), F32),
            jax.ShapeDtypeStruct((t, d), F32),
            jax.ShapeDtypeStruct((t, LANES), F32),
            jax.ShapeDtypeStruct((SUBLANES, t), F32),
            jax.ShapeDtypeStruct((n_experts, LANES), F32),
        ],
        scratch_shapes=[pltpu.VMEM((n_experts, LANES), F32)],
        compiler_params=_cparams(("arbitrary",)),
    )(*args)


def _sc_mesh():
    return plsc.VectorSubcoreMesh(core_axis_name="c", subcore_axis_name="s",
                                  num_cores=SC_CORES, num_subcores=SC_SUBCORES)


def _sc_worker():
    return lax.axis_index("s") * SC_CORES + lax.axis_index("c")


def _sc_scatter_rows(src, d1, d2, n_rows):
    t, d = src.shape
    workers = SC_CORES * SC_SUBCORES
    per_w = t // workers
    n_ch = per_w // SC_CHUNK
    assert t % (workers * SC_CHUNK) == 0
    idx = jnp.stack([d1, d2]).reshape(2, workers, n_ch, SC_CHUNK)

    @functools.partial(
        pl.kernel, mesh=_sc_mesh(),
        out_type=jax.ShapeDtypeStruct((n_rows, d), src.dtype),
        scratch_types=[pltpu.VMEM((2, n_ch, SC_CHUNK), jnp.int32),
                       pltpu.VMEM((SC_CHUNK, d), src.dtype),
                       pltpu.VMEM((SC_CHUNK, d), src.dtype),
                       pltpu.SemaphoreType.DMA, pltpu.SemaphoreType.DMA],
    )
    def scatter(src_hbm, idx_hbm, out_hbm, idx_v, rows_a, rows_b, sem_a, sem_b):
        wid = _sc_worker()
        base = wid * per_w
        for s in range(2):
            pltpu.sync_copy(idx_hbm.at[s, wid], idx_v.at[s])

        def move(c, rows_v, sem):
            pltpu.sync_copy(src_hbm.at[pl.ds(base + c * SC_CHUNK, SC_CHUNK)], rows_v)
            return [pltpu.async_copy(rows_v, out_hbm.at[idx_v.at[s, c]], sem) for s in range(2)]

        @pl.loop(0, n_ch, step=2)
        def _(c):
            first = move(c, rows_a, sem_a)
            second = move(c + 1, rows_b, sem_b)
            for cp in first + second:
                cp.wait()

    return scatter(src, idx)


def _sc_gather_rows(table, idx):
    b = idx.shape[0]
    d = table.shape[1]
    workers = SC_CORES * SC_SUBCORES
    per_w = b // workers
    n_ch = per_w // SC_CHUNK
    assert b % (workers * SC_CHUNK * 2) == 0

    @functools.partial(
        pl.kernel, mesh=_sc_mesh(),
        out_type=jax.ShapeDtypeStruct((b, d), table.dtype),
        scratch_types=[pltpu.VMEM((per_w,), jnp.int32),
                       pltpu.VMEM((SC_CHUNK, d), table.dtype),
                       pltpu.VMEM((SC_CHUNK, d), table.dtype),
                       pltpu.SemaphoreType.DMA, pltpu.SemaphoreType.DMA],
    )
    def gather(table_hbm, idx_hbm, out_hbm, idx_v, rows_a, rows_b, sem_a, sem_b):
        base = _sc_worker() * per_w
        pltpu.sync_copy(idx_hbm.at[pl.ds(base, per_w)], idx_v)

        def fetch(c, rows_v, sem):
            return pltpu.async_copy(table_hbm.at[idx_v.at[pl.ds(c * SC_CHUNK, SC_CHUNK)]], rows_v, sem)

        def store(c, rows_v):
            pltpu.sync_copy(rows_v, out_hbm.at[pl.ds(base + c * SC_CHUNK, SC_CHUNK)])

        @pl.loop(0, n_ch, step=2)
        def _(c):
            first = fetch(c, rows_a, sem_a)
            second = fetch(c + 1, rows_b, sem_b)
            first.wait()
            store(c, rows_a)
            second.wait()
            store(c + 1, rows_b)

    return gather(table, idx)


def _expert_kernel(be_ref, nu_ref, nv_ref, x_ref, wg_ref, wu_ref, wd_ref, y_ref, wgb_ref, wub_ref, wdb_ref):
    i = pl.program_id(0)
    used = i < nu_ref[0]
    new_expert = (i == 0) | (be_ref[i] != be_ref[jnp.maximum(i - 1, 0)])

    @pl.when(used & new_expert)
    def _():
        wgb_ref[...] = wg_ref[0, 0].astype(BF16)
        wub_ref[...] = wu_ref[0, 0].astype(BF16)
        wdb_ref[...] = wd_ref[0, 0].astype(BF16)

    @pl.when(used)
    def _():
        valid = lax.broadcasted_iota(jnp.int32, (x_ref.shape[0], 1), 0) < nv_ref[i]
        x = jnp.where(valid, x_ref[...], 0.0).astype(BF16)
        g = jnp.dot(x, wgb_ref[...], preferred_element_type=F32)
        u = jnp.dot(x, wub_ref[...], preferred_element_type=F32)
        hmid = (g * jax.nn.sigmoid(g) * u).astype(BF16)
        y_ref[...] = jnp.dot(hmid, wdb_ref[...], preferred_element_type=F32)

    @pl.when(jnp.logical_not(used))
    def _():
        y_ref[...] = jnp.zeros(y_ref.shape, y_ref.dtype)


def _experts(xs, block_e, n_used, n_valid, w_gate, w_up, w_down, layer):
    n_rows, d = xs.shape
    ff = w_gate.shape[3]
    nb = n_rows // ROW_BLOCK
    last = lambda i, nu: jnp.maximum(jnp.minimum(i, nu[0] - 1), 0)
    blk = lambda i, be, nu, nv: (last(i, nu), 0)
    wsel = lambda i, be, nu, nv: (layer, be[last(i, nu)], 0, 0)
    grid_spec = pltpu.PrefetchScalarGridSpec(
        num_scalar_prefetch=3,
        grid=(nb,),
        in_specs=[
            pl.BlockSpec((ROW_BLOCK, d), blk),
            pl.BlockSpec((1, 1, d, ff), wsel),
            pl.BlockSpec((1, 1, d, ff), wsel),
            pl.BlockSpec((1, 1, ff, d), wsel),
        ],
        out_specs=pl.BlockSpec((ROW_BLOCK, d), lambda i, be, nu, nv: (i, 0)),
        scratch_shapes=[pltpu.VMEM((d, ff), BF16), pltpu.VMEM((d, ff), BF16), pltpu.VMEM((ff, d), BF16)],
    )
    return pl.pallas_call(
        _expert_kernel,
        grid_spec=grid_spec,
        out_shape=jax.ShapeDtypeStruct((n_rows, d), F32),
        compiler_params=_cparams(("arbitrary",)),
    )(block_e, n_used, n_valid, xs, w_gate, w_up, w_down)


def _combine_kernel(x_ref, route_ref, y1_ref, y2_ref, nf_ref, o_ref, *, final_norm):
    route = route_ref[...]
    y = x_ref[...] + (route[:, 2:3] * y1_ref[...] + route[:, 3:4] * y2_ref[...])
    if final_norm:
        y = y * lax.rsqrt(jnp.mean(y * y, axis=-1, keepdims=True) + RMS_EPS) * nf_ref[...]
    o_ref[...] = y


def _combine(x1, route, ys, d1, d2, nf, final_norm):
    t, d = x1.shape
    yg = _sc_gather_rows(ys, jnp.concatenate([d1, d2]))
    slot2 = t // TM
    row = lambda i: (i, 0)
    return pl.pallas_call(
        functools.partial(_combine_kernel, final_norm=final_norm),
        grid=(t // TM,),
        in_specs=[
            pl.BlockSpec((TM, d), row),
            pl.BlockSpec((TM, LANES), row),
            pl.BlockSpec((TM, d), row),
            pl.BlockSpec((TM, d), lambda i: (slot2 + i, 0)),
            pl.BlockSpec((1, d), lambda i: (0, 0)),
        ],
        out_specs=pl.BlockSpec((TM, d), row),
        out_shape=jax.ShapeDtypeStruct((t, d), F32),
        compiler_params=_cparams(("parallel",)),
    )(x1, route, yg, yg, nf.reshape(1, d).astype(F32))


def _moe(x1, hn, route, route_t, cnt, w_gate, w_up, w_down, layer, nf, final_norm):
    t, d = x1.shape
    n_experts = w_gate.shape[1]
    ids = jnp.arange(n_experts, dtype=jnp.int32)
    e1, e2, rank1, rank2 = (route_t[f].astype(jnp.int32) for f in (0, 1, 4, 5))
    counts = cnt[:, 0].astype(jnp.int32)
    padded = ((counts + ROW_BLOCK - 1) // ROW_BLOCK) * ROW_BLOCK
    pad_end = jnp.cumsum(padded)
    pad_start = pad_end - padded

    def dest(e, rank):
        return jnp.sum(jnp.where(e[:, None] == ids[None, :], pad_start[None, :], 0), axis=1) + rank

    d1 = dest(e1, rank1)
    d2 = dest(e2, rank2)
    nb = (t * TOP_K_INNER) // ROW_BLOCK + n_experts
    n_used = (pad_end[-1:] // ROW_BLOCK).astype(jnp.int32)
    first_row = jnp.arange(nb, dtype=jnp.int32) * ROW_BLOCK
    block_e = jnp.sum((pad_end[None, :] <= first_row[:, None]).astype(jnp.int32), axis=1)
    block_e = jnp.minimum(block_e, n_experts - 1)
    row_end = jnp.sum(jnp.where(block_e[:, None] == ids[None, :], (pad_start + counts)[None, :], 0), axis=1)
    n_valid = jnp.clip(row_end - first_row, 0, ROW_BLOCK).astype(jnp.int32)
    xs = _sc_scatter_rows(hn, d1, d2, nb * ROW_BLOCK)
    ys = _experts(xs, block_e, n_used, n_valid, w_gate, w_up, w_down, layer)
    return _combine(x1, route, ys, d1, d2, nf, final_norm)


def _diff_lambda_init(layer_idx):
    return 0.8 - 0.6 * math.exp(-0.3 * layer_idx)


def _rope_tables(seq, dk):
    angle = 1.0 / (10000.0 ** jnp.linspace(0.0, 1.0, dk // 2, dtype=F32))
    angle = jnp.repeat(angle, 2)
    phase = jnp.arange(seq, dtype=F32)[:, None] * angle[None, :]
    sin, cos = jnp.sin(phase), jnp.cos(phase)
    even = (jnp.arange(dk) % 2 == 0)[None, :]
    return cos, jnp.where(even, -sin, 0.0), jnp.where(even, 0.0, sin)


def kernel(x, rel_bias_table, norm_mix, norm_ffn, norm_final, diff_w_in, diff_lambda_qk, diff_subln,
           diff_w_out, ret_w_in, ret_w_out, moe_w_group, moe_b_group, moe_w_expert, moe_b_expert,
           moe_w_gate, moe_w_up, moe_w_down):
    batch, seq, d = x.shape
    depth = norm_mix.shape[0]
    t = batch * seq
    heads = rel_bias_table.shape[1]
    dh = diff_lambda_qk.shape[-1]
    qk_dim = (ret_w_in.shape[2] - 2 * ret_w_out.shape[1]) // 2
    v_dim = ret_w_out.shape[1]
    dk = qk_dim // RET_HEADS
    xt = x.reshape(t, d)
    for i in range(depth):
        j = i // N_MIXERS
        if i % N_MIXERS == 0:
            col_scale = jnp.concatenate([jnp.full((d,), dh ** -0.5 * LOG2E, F32), jnp.ones((2 * d,), F32)])
            w_in = (diff_w_in[j] * col_scale[None, :]).astype(BF16)
            qkv = _norm_matmul(xt, norm_mix[i], w_in)
            mix = _diff_attention(qkv, rel_bias_table, diff_lambda_qk[j], diff_subln[j],
                                  _diff_lambda_init(i), batch, seq, heads)
            gate = None
            w_out = diff_w_out[j].astype(BF16)
        else:
            n_in = ret_w_in.shape[2]
            col_scale = jnp.concatenate([jnp.ones((qk_dim,), F32), jnp.full((qk_dim,), dk ** -0.5, F32),
                                         jnp.ones((n_in - 2 * qk_dim,), F32)])
            w_in = (ret_w_in[j] * col_scale[None, :]).astype(BF16)
            qkvg = _norm_matmul(xt, norm_mix[i], w_in, rope=_rope_tables(seq, dk),
                                n_rope=2 * qk_dim // TN, seq=seq)
            mix = _retention(qkvg, batch, seq, qk_dim, v_dim)
            gate = (qkvg, (2 * qk_dim + v_dim) // v_dim)
            w_out = ret_w_out[j].astype(BF16)
        x1, hn, route, route_t, cnt = _out_router(mix, gate, w_out, xt, norm_ffn[i], moe_w_group[i],
                                                  moe_b_group[i], moe_w_expert[i], moe_b_expert[i])
        xt = _moe(x1, hn, route, route_t, cnt, moe_w_gate, moe_w_up, moe_w_down, i,
                  norm_final, i == depth - 1)
    return xt.reshape(batch, seq, d)
```

```python
import functools
import math

import jax
import jax.numpy as jnp
from jax import lax
from jax.experimental import pallas as pl
from jax.experimental.pallas import tpu as pltpu
from jax.experimental.pallas import tpu_sc as plsc

F32 = jnp.float32
BF16 = jnp.bfloat16

N_MIXERS = 2
NUM_BUCKETS = 32
MAX_DISTANCE = 128
RET_HEADS = 4
TOP_K_INNER = 2
RMS_EPS = 1e-6
SUBLN_EPS = 1e-5

LANES = 128
SUBLANES = 8
BF16_SUBLANES = 16
LOG2E = 1.4426950408889634
VMEM_LIMIT = 56 * 1024 * 1024
SC_CORES = 2
SC_SUBCORES = 16

TM = 512
TN = 1024
TQ = 256
ATTN_HEADS = 4
RET_C = 256
ROW_BLOCK = 256
SC_CHUNK = 32


def _cparams(sem):
    return pltpu.CompilerParams(dimension_semantics=sem, vmem_limit_bytes=VMEM_LIMIT)


def _norm_matmul_kernel(x_ref, *rest, n_rope, moe):
    rest = list(rest)
    if moe:
        route_ref, y1_ref, y2_ref = rest[:3]
        x_out_ref = rest.pop()
        rest = rest[3:]
    g_ref, w_ref = rest[:2]
    if n_rope:
        cos_ref, sa_ref, sb_ref, o_ref = rest[2:]
    else:
        (o_ref,) = rest[2:]
    x = x_ref[...]
    if moe:
        route = route_ref[...]
        x = x + (route[:, 2:3] * y1_ref[...] + route[:, 3:4] * y2_ref[...])
        x_out_ref[...] = x
    xn = (x * lax.rsqrt(jnp.mean(x * x, axis=-1, keepdims=True) + RMS_EPS) * g_ref[...]).astype(BF16)
    for j in range(o_ref.shape[1] // TN):
        cols = slice(j * TN, (j + 1) * TN)
        acc = jnp.dot(xn, w_ref[:, cols], preferred_element_type=F32)
        if j < n_rope:
            reps = TN // cos_ref.shape[1]
            c = jnp.tile(cos_ref[...], (1, reps))
            sa = jnp.tile(sa_ref[...], (1, reps))
            sb = jnp.tile(sb_ref[...], (1, reps))
            nxt = pltpu.roll(acc, TN - 1, axis=1)
            prv = pltpu.roll(acc, 1, axis=1)
            acc = acc * c + nxt * sa + prv * sb
        o_ref[:, cols] = acc.astype(o_ref.dtype)


def _norm_matmul(x, g, w, rope=None, n_rope=0, seq=None, moe=None):
    t, d = x.shape
    n = w.shape[1]
    row = lambda i: (i, 0)
    in_specs = [pl.BlockSpec((TM, d), row)]
    args = [x]
    if moe is not None:
        route, yg = moe
        slot2 = t // TM
        in_specs += [pl.BlockSpec((TM, LANES), row), pl.BlockSpec((TM, d), row),
                     pl.BlockSpec((TM, d), lambda i: (slot2 + i, 0))]
        args += [route, yg, yg]
    in_specs += [
        pl.BlockSpec((1, d), lambda i: (0, 0)),
        pl.BlockSpec((d, n), lambda i: (0, 0), pipeline_mode=pl.Buffered(1)),
    ]
    args += [g.reshape(1, d), w]
    if rope is not None:
        nb = seq // TM
        dk = rope[0].shape[1]
        in_specs += [pl.BlockSpec((TM, dk), lambda i: (i % nb, 0))] * 3
        args += list(rope)
    out_specs = [pl.BlockSpec((TM, n), row)]
    out_shape = [jax.ShapeDtypeStruct((t, n), BF16)]
    if moe is not None:
        out_specs.append(pl.BlockSpec((TM, d), row))
        out_shape.append(jax.ShapeDtypeStruct((t, d), F32))
    out = pl.pallas_call(
        functools.partial(_norm_matmul_kernel, n_rope=n_rope, moe=moe is not None),
        grid=(t // TM,),
        in_specs=in_specs,
        out_specs=out_specs,
        out_shape=out_shape,
        compiler_params=_cparams(("parallel",)),
    )(*args)
    return out if moe is not None else out[0]


def _diff_attn_kernel(q_ref, k_ref, v_ref, bias_ref, lam_ref, g_ref, o_ref,
                      vt_ref, m_ref, acc_ref, s_ref, *, lambda_init):
    seq = q_ref.shape[0]
    hp, _, _, tq = bias_ref.shape
    dv = q_ref.shape[1] // hp
    dh = dv // 2
    nq = seq // tq

    def head(hh):
        return slice(hh * dv, (hh + 1) * dv)

    extra = vt_ref.shape[2] - dv
    ones_row = jnp.where(lax.broadcasted_iota(jnp.int32, (extra, tq), 0) == 0, 1.0, 0.0).astype(BF16)
    for hh in range(hp):
        for j in range(nq):
            vt_ref[hh, j, :dv, :] = v_ref[j * tq:(j + 1) * tq, head(hh)].astype(F32).T.astype(BF16)
            vt_ref[hh, j, dv:, :] = ones_row

    lq = lam_ref[...]
    lam = (jnp.exp(jnp.sum(lq[0:1] * lq[1:2], axis=-1, keepdims=True))
           - jnp.exp(jnp.sum(lq[2:3] * lq[3:4], axis=-1, keepdims=True)) + lambda_init)
    lane = lax.broadcasted_iota(jnp.int32, (tq, dv), 1)

    def q_body(qi, carry):
        qrows = pl.ds(pl.multiple_of(qi * tq, tq), tq)
        qs = []
        for hh in range(hp):
            q = q_ref[qrows, head(hh)]
            zero = jnp.zeros_like(q)
            qs.append(jnp.concatenate([jnp.where(lane < dh, q, zero), jnp.where(lane >= dh, q, zero)],
                                      axis=0))

        m_ref[...] = jnp.full(m_ref.shape, -jnp.inf, F32)
        acc_ref[...] = jnp.zeros(acc_ref.shape, F32)

        def score_dots(j):
            rows = pl.ds(pl.multiple_of(j * tq, tq), tq)
            for hh in range(hp):
                s_ref[hh] = lax.dot_general(k_ref[rows, head(hh)], qs[hh], (((1,), (1,)), ((), ())),
                                            preferred_element_type=F32)

        def step(j, tile, prefetch):
            probs, alphas = [], []
            for hh in range(hp):
                s = s_ref[hh]
                if tile is not None:
                    bias = bias_ref[hh, tile]
                    s = s + jnp.concatenate([bias, bias], axis=1)
                m_old = m_ref[hh]
                m_new = jnp.maximum(m_old, jnp.max(s, axis=0, keepdims=True))
                alpha = jnp.exp2(m_old - m_new)
                m_ref[hh] = m_new
                probs.append(jnp.exp2(s - m_new).astype(BF16))
                alphas.append(alpha)
            if prefetch:
                score_dots(j + 1)
            for hh in range(hp):
                acc_ref[hh] = alphas[hh] * acc_ref[hh] + jnp.dot(vt_ref[hh, j], probs[hh],
                                                                 preferred_element_type=F32)

        score_dots(0)

        def far_body(j, c):
            step(j, None, True)
            return c

        lax.fori_loop(0, jnp.maximum(qi - 1, 0), far_body, 0)

        @pl.when(qi >= 1)
        def _():
            step(qi - 1, 1, True)

        step(qi, 0, False)

        for hh in range(hp):
            acc = acc_ref[hh]
            on = acc[:dv] / acc[dv:dv + 1]
            ot = on[:, :tq] - lam * on[:, tq:]
            ot = ot * lax.rsqrt(jnp.mean(ot * ot, axis=0, keepdims=True) + SUBLN_EPS)
            o_ref[qrows, head(hh)] = (ot.T * g_ref[...] * (1.0 - lambda_init)).astype(o_ref.dtype)
        return carry

    lax.fori_loop(0, nq, q_body, 0)


def _t5_bucket(rel):
    max_exact = NUM_BUCKETS // 2
    n = jnp.maximum(rel, 0)
    is_small = n < max_exact
    nf = jnp.maximum(n, max_exact).astype(F32)
    large = max_exact + (jnp.log(nf / max_exact) / math.log(MAX_DISTANCE / max_exact)
                         * (NUM_BUCKETS - max_exact)).astype(jnp.int32)
    large = jnp.minimum(large, NUM_BUCKETS - 1)
    return jnp.where(is_small, n, large)


def _bias_tiles(rel_table):
    heads = rel_table.shape[1]
    span = 2 * TQ + 1
    bucket = _t5_bucket(jnp.arange(span))
    onehot = (bucket[:, None] == jnp.arange(NUM_BUCKETS)[None, :]).astype(F32)
    table = (rel_table.astype(F32) - rel_table[NUM_BUCKETS - 1].astype(F32)[None, :]) * LOG2E
    by_rel = jnp.dot(onehot, table, precision=lax.Precision.HIGHEST).T
    toep = jnp.tile(by_rel, (1, TQ))[:, :TQ * (span - 1)].reshape(heads, TQ, span - 1)
    kj = jnp.arange(TQ)[:, None]
    qi = jnp.arange(TQ)[None, :]
    diag = jnp.where((qi >= kj)[None], toep[:, :, :TQ], -jnp.inf)
    return jnp.stack([diag, toep[:, :, TQ:]], axis=1)


def _diff_attention(qkv, rel_table, lam_qk, subln_g, lambda_init, batch, seq, heads):
    t = qkv.shape[0]
    d = qkv.shape[1] // 3
    dv = d // heads
    hp = ATTN_HEADS
    assert TQ + 1 >= MAX_DISTANCE and seq % TQ == 0 and dv == LANES and heads % hp == 0
    nq = seq // TQ
    groups = heads // hp
    bias = _bias_tiles(rel_table)
    return pl.pallas_call(
        functools.partial(_diff_attn_kernel, lambda_init=lambda_init),
        grid=(batch, groups),
        in_specs=[
            pl.BlockSpec((seq, hp * dv), lambda b, g: (b, g)),
            pl.BlockSpec((seq, hp * dv), lambda b, g: (b, groups + g)),
            pl.BlockSpec((seq, hp * dv), lambda b, g: (b, 2 * groups + g)),
            pl.BlockSpec((hp, 2, TQ, TQ), lambda b, g: (g, 0, 0, 0)),
            pl.BlockSpec(lam_qk.shape, lambda b, g: (0, 0)),
            pl.BlockSpec((1, dv), lambda b, g: (0, 0)),
        ],
        out_specs=pl.BlockSpec((seq, hp * dv), lambda b, g: (b, g)),
        out_shape=jax.ShapeDtypeStruct((t, d), BF16),
        scratch_shapes=[
            pltpu.VMEM((hp, nq, dv + BF16_SUBLANES, TQ), BF16),
            pltpu.VMEM((hp, 1, 2 * TQ), F32),
            pltpu.VMEM((hp, dv + BF16_SUBLANES, 2 * TQ), F32),
            pltpu.VMEM((hp, TQ, 2 * TQ), F32),
        ],
        compiler_params=_cparams(("parallel", "parallel")),
    )(qkv, qkv, qkv, bias, lam_qk.astype(F32), subln_g.reshape(1, dv).astype(F32))


def _retention_kernel(q_ref, k_ref, v_ref, dec_ref, tab_ref, o_ref, state_ref):
    seq = q_ref.shape[0]
    c = dec_ref.shape[1]
    dec = dec_ref[0]
    tab = tab_ref[0]
    qd = tab[:, 0:1]
    kd = tab[:, 1:2]
    cd = tab[0:1, 2:3]
    state_ref[...] = jnp.zeros(state_ref.shape, F32)

    def body(n, carry):
        rows = pl.ds(pl.multiple_of(n * c, c), c)
        q = q_ref[rows, :]
        k = k_ref[rows, :]
        v = v_ref[rows, :]
        s = lax.dot_general(q, k, (((1,), (1,)), ((), ())), preferred_element_type=F32) * dec
        inner = jnp.dot(s.astype(BF16), v, preferred_element_type=F32)
        st = state_ref[...]
        cross = jnp.dot(q, st.astype(BF16), preferred_element_type=F32) * qd
        kt = (k.astype(F32) * kd).T.astype(BF16)
        state_ref[...] = st * cd + jnp.dot(kt, v, preferred_element_type=F32)
        o = inner + cross
        o = o * lax.rsqrt(jnp.mean(o * o, axis=-1, keepdims=True) + RMS_EPS)
        o_ref[rows, :] = o.astype(o_ref.dtype)
        return carry

    lax.fori_loop(0, seq // c, body, 0)


def _retention_tables(heads):
    c = RET_C
    log_gamma = jnp.log(1.0 - 2.0 ** (-5.0 - jnp.arange(heads, dtype=F32)))
    pos = jnp.arange(c, dtype=F32)
    dist = pos[:, None] - pos[None, :]
    inner_decay = jnp.where(dist[None] >= 0,
                            jnp.exp(jnp.maximum(dist, 0.0)[None] * log_gamma[:, None, None]), 0.0)
    q_decay = jnp.exp((pos + 1.0)[None] * log_gamma[:, None])
    k_decay = jnp.exp((c - 1.0 - pos)[None] * log_gamma[:, None])
    chunk_decay = jnp.broadcast_to(jnp.exp(c * log_gamma)[:, None], (heads, c))
    tab = jnp.stack([q_decay, k_decay, chunk_decay], axis=-1)
    tab = jnp.pad(tab, ((0, 0), (0, 0), (0, LANES - 3)))
    return inner_decay, tab


def _retention(qkvg, batch, seq, qk_dim, v_dim):
    t = qkvg.shape[0]
    dk = qk_dim // RET_HEADS
    dvh = v_dim // RET_HEADS
    inner_decay, tab = _retention_tables(RET_HEADS)
    kb = qk_dim // dk
    vb = 2 * qk_dim // dvh
    return pl.pallas_call(
        _retention_kernel,
        grid=(batch, RET_HEADS),
        in_specs=[
            pl.BlockSpec((seq, dk), lambda b, h: (b, h)),
            pl.BlockSpec((seq, dk), lambda b, h: (b, kb + h)),
            pl.BlockSpec((seq, dvh), lambda b, h: (b, vb + h)),
            pl.BlockSpec((1, RET_C, RET_C), lambda b, h: (h, 0, 0)),
            pl.BlockSpec((1, RET_C, LANES), lambda b, h: (h, 0, 0)),
        ],
        out_specs=pl.BlockSpec((seq, dvh), lambda b, h: (b, h)),
        out_shape=jax.ShapeDtypeStruct((t, v_dim), BF16),
        scratch_shapes=[pltpu.VMEM((dk, dvh), F32)],
        compiler_params=_cparams(("parallel", "arbitrary")),
    )(qkvg, qkvg, qkvg, inner_decay, tab)


def _out_router_kernel(*refs, gated, n_groups, n_experts):
    if gated:
        a_ref, gate_ref, w_ref, x_ref, nf_ref, wr_ref, br_ref = refs[:7]
        x1_ref, hn_ref, route_ref, route_t_ref, cnt_ref, carry_ref = refs[7:]
        gate = gate_ref[...]
        a = gate * jax.nn.sigmoid(gate) * a_ref[...]
    else:
        a_ref, w_ref, x_ref, nf_ref, wr_ref, br_ref = refs[:6]
        x1_ref, hn_ref, route_ref, route_t_ref, cnt_ref, carry_ref = refs[6:]
        a = a_ref[...]
    epg = n_experts // n_groups

    @pl.when(pl.program_id(0) == 0)
    def _():
        carry_ref[...] = jnp.zeros(carry_ref.shape, F32)

    x1 = x_ref[...] + jnp.dot(a, w_ref[...], preferred_element_type=F32)
    x1_ref[...] = x1
    hn = x1 * lax.rsqrt(jnp.mean(x1 * x1, axis=-1, keepdims=True) + RMS_EPS) * nf_ref[...]
    hn_ref[...] = hn

    hn_hi = hn.astype(BF16)
    hn_lo = (hn - hn_hi.astype(F32)).astype(BF16)
    logits = (jnp.dot(hn_hi, wr_ref[0], preferred_element_type=F32)
              + jnp.dot(hn_lo, wr_ref[0], preferred_element_type=F32)
              + jnp.dot(hn_hi, wr_ref[1], preferred_element_type=F32)) + br_ref[...]
    tm = logits.shape[0]

    n_rows = -(-(n_groups + n_experts) // SUBLANES) * SUBLANES
    lt = logits.T[:n_rows, :]
    row = lax.broadcasted_iota(jnp.int32, lt.shape, 0)
    big = jnp.int32(LANES)
    neg = jnp.float32(-jnp.inf)

    gl = jnp.where(row < n_groups, lt, neg)
    gmax = jnp.max(gl, axis=0, keepdims=True)
    g_p = 1.0 / jnp.sum(jnp.exp(gl - gmax), axis=0, keepdims=True)
    g_idx = jnp.min(jnp.where(gl == gmax, row, big), axis=0, keepdims=True)

    lo = n_groups + g_idx * epg
    el = jnp.where((row >= lo) & (row < lo + epg), lt, neg)
    v1 = jnp.max(el, axis=0, keepdims=True)
    i1 = jnp.min(jnp.where(el == v1, row, big), axis=0, keepdims=True)
    el2 = jnp.where(row == i1, neg, el)
    v2 = jnp.max(el2, axis=0, keepdims=True)
    i2 = jnp.min(jnp.where(el2 == v2, row, big), axis=0, keepdims=True)
    tt = jnp.exp(v2 - v1)
    w1 = g_p / (1.0 + tt)
    w2 = g_p * tt / (1.0 + tt)
    e1 = i1 - n_groups
    e2 = i2 - n_groups

    erow = lax.broadcasted_iota(jnp.int32, (n_experts, tm), 0)
    oh1 = erow == e1
    oh2 = erow == e2
    onehot = jnp.where(oh1 | oh2, 1.0, 0.0)
    before = lax.broadcasted_iota(jnp.int32, (tm, tm), 0) < lax.broadcasted_iota(jnp.int32, (tm, tm), 1)
    tri = jnp.where(before, 1.0, 0.0).astype(BF16)
    carry = carry_ref[...]
    base = jnp.dot(onehot.astype(BF16), tri, preferred_element_type=F32) + carry[:, 0:1]
    rank1 = jnp.sum(jnp.where(oh1, base, 0.0), axis=0, keepdims=True)
    rank2 = jnp.sum(jnp.where(oh2, base, 0.0), axis=0, keepdims=True)
    carry = carry + jnp.sum(onehot, axis=1, keepdims=True)
    carry_ref[...] = carry
    cnt_ref[...] = carry

    frow = lax.broadcasted_iota(jnp.int32, (LANES, tm), 0)
    out = jnp.zeros((LANES, tm), F32)
    for idx, val in enumerate((e1.astype(F32), e2.astype(F32), w1, w2, rank1, rank2)):
        out = jnp.where(frow == idx, val, out)
    route_t_ref[...] = out[:route_t_ref.shape[0], :]
    route_ref[...] = out.T


def _out_router(a, gate, w, x, nf, w_group, b_group, w_expert, b_expert):
    t, d = x.shape
    k = w.shape[0]
    n_groups = w_group.shape[1]
    n_experts = w_expert.shape[1]
    assert n_groups + n_experts <= LANES
    pad = LANES - n_groups - n_experts
    wr = jnp.pad(jnp.concatenate([w_group, w_expert], axis=1).astype(F32), ((0, 0), (0, pad)))
    wr_hi = wr.astype(BF16)
    wr = jnp.stack([wr_hi, (wr - wr_hi.astype(F32)).astype(BF16)])
    br = jnp.pad(jnp.concatenate([b_group, b_expert]).astype(F32), (0, pad)).reshape(1, LANES)
    gated = gate is not None
    row = lambda i: (i, 0)
    fixed = lambda i: (0, 0)
    in_specs = [pl.BlockSpec((TM, k), row)]
    args = [a]
    if gated:
        gate_arr, gate_col = gate
        in_specs.append(pl.BlockSpec((TM, k), lambda i: (i, gate_col)))
        args.append(gate_arr)
    in_specs += [
        pl.BlockSpec((k, d), fixed),
        pl.BlockSpec((TM, d), row),
        pl.BlockSpec((1, d), fixed),
        pl.BlockSpec((2, d, LANES), lambda i: (0, 0, 0)),
        pl.BlockSpec((1, LANES), fixed),
    ]
    args += [w, x, nf.reshape(1, d), wr, br]
    return pl.pallas_call(
        functools.partial(_out_router_kernel, gated=gated, n_groups=n_groups, n_experts=n_experts),
        grid=(t // TM,),
        in_specs=in_specs,
        out_specs=[
            pl.BlockSpec((TM, d), row),
            pl.BlockSpec((TM, d), row),
            pl.BlockSpec((TM, LANES), row),
            pl.BlockSpec((SUBLANES, TM), lambda i: (0, i)),
            pl.BlockSpec((n_experts, LANES), fixed),
        ],
        out_shape=[
            jax.ShapeDtypeStruct((t, d), F32),
            jax.ShapeDtypeStruct((t, d), F32),
            jax.ShapeDtypeStruct((t, LANES), F32),
            jax.ShapeDtypeStruct((SUBLANES, t), F32),
            jax.ShapeDtypeStruct((n_experts, LANES), F32),
        ],
        scratch_shapes=[pltpu.VMEM((n_experts, LANES), F32)],
        compiler_params=_cparams(("arbitrary",)),
    )(*args)


def _sc_mesh():
    return plsc.VectorSubcoreMesh(core_axis_name="c", subcore_axis_name="s",
                                  num_cores=SC_CORES, num_subcores=SC_SUBCORES)


def _sc_worker():
    return lax.axis_index("s") * SC_CORES + lax.axis_index("c")


def _sc_scatter_rows(src, d1, d2, n_rows):
    t, d = src.shape
    workers = SC_CORES * SC_SUBCORES
    per_w = t // workers
    n_ch = per_w // SC_CHUNK
    assert t % (workers * SC_CHUNK) == 0
    idx = jnp.stack([d1, d2]).reshape(2, workers, n_ch, SC_CHUNK)

    @functools.partial(
        pl.kernel, mesh=_sc_mesh(),
        out_type=jax.ShapeDtypeStruct((n_rows, d), src.dtype),
        scratch_types=[pltpu.VMEM((2, n_ch, SC_CHUNK), jnp.int32),
                       pltpu.VMEM((SC_CHUNK, d), src.dtype),
                       pltpu.VMEM((SC_CHUNK, d), src.dtype),
                       pltpu.SemaphoreType.DMA, pltpu.SemaphoreType.DMA],
    )
    def scatter(src_hbm, idx_hbm, out_hbm, idx_v, rows_a, rows_b, sem_a, sem_b):
        wid = _sc_worker()
        base = wid * per_w
        for s in range(2):
            pltpu.sync_copy(idx_hbm.at[s, wid], idx_v.at[s])

        def move(c, rows_v, sem):
            pltpu.sync_copy(src_hbm.at[pl.ds(base + c * SC_CHUNK, SC_CHUNK)], rows_v)
            return [pltpu.async_copy(rows_v, out_hbm.at[idx_v.at[s, c]], sem) for s in range(2)]

        @pl.loop(0, n_ch, step=2)
        def _(c):
            first = move(c, rows_a, sem_a)
            second = move(c + 1, rows_b, sem_b)
            for cp in first + second:
                cp.wait()

    return scatter(src, idx)


def _sc_gather_rows(table, idx):
    b = idx.shape[0]
    d = table.shape[1]
    workers = SC_CORES * SC_SUBCORES
    per_w = b // workers
    n_ch = per_w // SC_CHUNK
    assert b % (workers * SC_CHUNK * 2) == 0

    @functools.partial(
        pl.kernel, mesh=_sc_mesh(),
        out_type=jax.ShapeDtypeStruct((b, d), table.dtype),
        scratch_types=[pltpu.VMEM((per_w,), jnp.int32),
                       pltpu.VMEM((SC_CHUNK, d), table.dtype),
                       pltpu.VMEM((SC_CHUNK, d), table.dtype),
                       pltpu.SemaphoreType.DMA, pltpu.SemaphoreType.DMA],
    )
    def gather(table_hbm, idx_hbm, out_hbm, idx_v, rows_a, rows_b, sem_a, sem_b):
        base = _sc_worker() * per_w
        pltpu.sync_copy(idx_hbm.at[pl.ds(base, per_w)], idx_v)

        def fetch(c, rows_v, sem):
            return pltpu.async_copy(table_hbm.at[idx_v.at[pl.ds(c * SC_CHUNK, SC_CHUNK)]], rows_v, sem)

        def store(c, rows_v):
            pltpu.sync_copy(rows_v, out_hbm.at[pl.ds(base + c * SC_CHUNK, SC_CHUNK)])

        @pl.loop(0, n_ch, step=2)
        def _(c):
            first = fetch(c, rows_a, sem_a)
            second = fetch(c + 1, rows_b, sem_b)
            first.wait()
            store(c, rows_a)
            second.wait()
            store(c + 1, rows_b)

    return gather(table, idx)


def _expert_kernel(be_ref, nu_ref, nv_ref, x_ref, wg_ref, wu_ref, wd_ref, y_ref, wgb_ref, wub_ref, wdb_ref):
    i = pl.program_id(0)
    used = i < nu_ref[0]
    new_expert = (i == 0) | (be_ref[i] != be_ref[jnp.maximum(i - 1, 0)])

    @pl.when(used & new_expert)
    def _():
        wgb_ref[...] = wg_ref[0, 0].astype(BF16)
        wub_ref[...] = wu_ref[0, 0].astype(BF16)
        wdb_ref[...] = wd_ref[0, 0].astype(BF16)

    @pl.when(used)
    def _():
        valid = lax.broadcasted_iota(jnp.int32, (x_ref.shape[0], 1), 0) < nv_ref[i]
        x = jnp.where(valid, x_ref[...], 0.0).astype(BF16)
        g = jnp.dot(x, wgb_ref[...], preferred_element_type=F32)
        u = jnp.dot(x, wub_ref[...], preferred_element_type=F32)
        hmid = (g * jax.nn.sigmoid(g) * u).astype(BF16)
        y_ref[...] = jnp.dot(hmid, wdb_ref[...], preferred_element_type=F32)

    @pl.when(jnp.logical_not(used))
    def _():
        y_ref[...] = jnp.zeros(y_ref.shape, y_ref.dtype)


def _experts(xs, block_e, n_used, n_valid, w_gate, w_up, w_down, layer):
    n_rows, d = xs.shape
    ff = w_gate.shape[3]
    nb = n_rows // ROW_BLOCK
    last = lambda i, nu: jnp.maximum(jnp.minimum(i, nu[0] - 1), 0)
    blk = lambda i, be, nu, nv: (last(i, nu), 0)
    wsel = lambda i, be, nu, nv: (layer, be[last(i, nu)], 0, 0)
    grid_spec = pltpu.PrefetchScalarGridSpec(
        num_scalar_prefetch=3,
        grid=(nb,),
        in_specs=[
            pl.BlockSpec((ROW_BLOCK, d), blk),
            pl.BlockSpec((1, 1, d, ff), wsel),
            pl.BlockSpec((1, 1, d, ff), wsel),
            pl.BlockSpec((1, 1, ff, d), wsel),
        ],
        out_specs=pl.BlockSpec((ROW_BLOCK, d), lambda i, be, nu, nv: (i, 0)),
        scratch_shapes=[pltpu.VMEM((d, ff), BF16), pltpu.VMEM((d, ff), BF16), pltpu.VMEM((ff, d), BF16)],
    )
    return pl.pallas_call(
        _expert_kernel,
        grid_spec=grid_spec,
        out_shape=jax.ShapeDtypeStruct((n_rows, d), F32),
        compiler_params=_cparams(("arbitrary",)),
    )(block_e, n_used, n_valid, xs, w_gate, w_up, w_down)


def _final_combine_kernel(x_ref, route_ref, y1_ref, y2_ref, nf_ref, o_ref):
    route = route_ref[...]
    y = x_ref[...] + (route[:, 2:3] * y1_ref[...] + route[:, 3:4] * y2_ref[...])
    o_ref[...] = y * lax.rsqrt(jnp.mean(y * y, axis=-1, keepdims=True) + RMS_EPS) * nf_ref[...]


def _final_combine(x1, route, yg, nf):
    t, d = x1.shape
    slot2 = t // TM
    row = lambda i: (i, 0)
    return pl.pallas_call(
        _final_combine_kernel,
        grid=(t // TM,),
        in_specs=[
            pl.BlockSpec((TM, d), row),
            pl.BlockSpec((TM, LANES), row),
            pl.BlockSpec((TM, d), row),
            pl.BlockSpec((TM, d), lambda i: (slot2 + i, 0)),
            pl.BlockSpec((1, d), lambda i: (0, 0)),
        ],
        out_specs=pl.BlockSpec((TM, d), row),
        out_shape=jax.ShapeDtypeStruct((t, d), F32),
        compiler_params=_cparams(("parallel",)),
    )(x1, route, yg, yg, nf.reshape(1, d).astype(F32))


def _moe(hn, route_t, cnt, w_gate, w_up, w_down, layer):
    t, d = hn.shape
    n_experts = w_gate.shape[1]
    ids = jnp.arange(n_experts, dtype=jnp.int32)
    e1, e2, rank1, rank2 = (route_t[f].astype(jnp.int32) for f in (0, 1, 4, 5))
    counts = cnt[:, 0].astype(jnp.int32)
    padded = ((counts + ROW_BLOCK - 1) // ROW_BLOCK) * ROW_BLOCK
    pad_end = jnp.cumsum(padded)
    pad_start = pad_end - padded

    def dest(e, rank):
        return jnp.sum(jnp.where(e[:, None] == ids[None, :], pad_start[None, :], 0), axis=1) + rank

    d1 = dest(e1, rank1)
    d2 = dest(e2, rank2)
    nb = (t * TOP_K_INNER) // ROW_BLOCK + n_experts
    n_used = (pad_end[-1:] // ROW_BLOCK).astype(jnp.int32)
    first_row = jnp.arange(nb, dtype=jnp.int32) * ROW_BLOCK
    block_e = jnp.sum((pad_end[None, :] <= first_row[:, None]).astype(jnp.int32), axis=1)
    block_e = jnp.minimum(block_e, n_experts - 1)
    row_end = jnp.sum(jnp.where(block_e[:, None] == ids[None, :], (pad_start + counts)[None, :], 0), axis=1)
    n_valid = jnp.clip(row_end - first_row, 0, ROW_BLOCK).astype(jnp.int32)
    xs = _sc_scatter_rows(hn, d1, d2, nb * ROW_BLOCK)
    ys = _experts(xs, block_e, n_used, n_valid, w_gate, w_up, w_down, layer)
    return _sc_gather_rows(ys, jnp.concatenate([d1, d2]))


def _diff_lambda_init(layer_idx):
    return 0.8 - 0.6 * math.exp(-0.3 * layer_idx)


def _rope_tables(seq, dk):
    angle = 1.0 / (10000.0 ** jnp.linspace(0.0, 1.0, dk // 2, dtype=F32))
    angle = jnp.repeat(angle, 2)
    phase = jnp.arange(seq, dtype=F32)[:, None] * angle[None, :]
    sin, cos = jnp.sin(phase), jnp.cos(phase)
    even = (jnp.arange(dk) % 2 == 0)[None, :]
    return cos, jnp.where(even, -sin, 0.0), jnp.where(even, 0.0, sin)


def kernel(x, rel_bias_table, norm_mix, norm_ffn, norm_final, diff_w_in, diff_lambda_qk, diff_subln,
           diff_w_out, ret_w_in, ret_w_out, moe_w_group, moe_b_group, moe_w_expert, moe_b_expert,
           moe_w_gate, moe_w_up, moe_w_down):
    batch, seq, d = x.shape
    depth = norm_mix.shape[0]
    t = batch * seq
    heads = rel_bias_table.shape[1]
    dh = diff_lambda_qk.shape[-1]
    qk_dim = (ret_w_in.shape[2] - 2 * ret_w_out.shape[1]) // 2
    v_dim = ret_w_out.shape[1]
    dk = qk_dim // RET_HEADS
    xt = x.reshape(t, d)
    moe = None
    for i in range(depth):
        j = i // N_MIXERS
        if i % N_MIXERS == 0:
            col_scale = jnp.concatenate([jnp.full((d,), dh ** -0.5 * LOG2E, F32), jnp.ones((2 * d,), F32)])
            w_in = (diff_w_in[j] * col_scale[None, :]).astype(BF16)
            qkv = _norm_matmul(xt, norm_mix[i], w_in, moe=moe)
            if moe is not None:
                qkv, xt = qkv
            mix = _diff_attention(qkv, rel_bias_table, diff_lambda_qk[j], diff_subln[j],
                                  _diff_lambda_init(i), batch, seq, heads)
            gate = None
            w_out = diff_w_out[j].astype(BF16)
        else:
            n_in = ret_w_in.shape[2]
            col_scale = jnp.concatenate([jnp.ones((qk_dim,), F32), jnp.full((qk_dim,), dk ** -0.5, F32),
                                         jnp.ones((n_in - 2 * qk_dim,), F32)])
            w_in = (ret_w_in[j] * col_scale[None, :]).astype(BF16)
            qkvg = _norm_matmul(xt, norm_mix[i], w_in, rope=_rope_tables(seq, dk),
                                n_rope=2 * qk_dim // TN, seq=seq, moe=moe)
            if moe is not None:
                qkvg, xt = qkvg
            mix = _retention(qkvg, batch, seq, qk_dim, v_dim)
            gate = (qkvg, (2 * qk_dim + v_dim) // v_dim)
            w_out = ret_w_out[j].astype(BF16)
        x1, hn, route, route_t, cnt = _out_router(mix, gate, w_out, xt, norm_ffn[i], moe_w_group[i],
                                                  moe_b_group[i], moe_w_expert[i], moe_b_expert[i])
        xt = x1
        moe = (route, _moe(hn, route_t, cnt, moe_w_gate, moe_w_up, moe_w_down, i))
    return _final_combine(xt, moe[0], moe[1], norm_final).reshape(batch, seq, d)
```

```python
import functools
import math

import jax
import jax.numpy as jnp
from jax import lax
from jax.experimental import pallas as pl
from jax.experimental.pallas import tpu as pltpu
from jax.experimental.pallas import tpu_sc as plsc

F32 = jnp.float32
BF16 = jnp.bfloat16

N_MIXERS = 2
NUM_BUCKETS = 32
MAX_DISTANCE = 128
RET_HEADS = 4
TOP_K_INNER = 2
RMS_EPS = 1e-6
SUBLN_EPS = 1e-5

LANES = 128
SUBLANES = 8
BF16_SUBLANES = 16
LOG2E = 1.4426950408889634
VMEM_LIMIT = 56 * 1024 * 1024
SC_CORES = 2
SC_SUBCORES = 16

TM = 512
TN = 1024
TQ = 256
ATTN_HEADS = 4
RET_C = 256
ROW_BLOCK = 256
SC_CHUNK = 64


def _cparams(sem):
    return pltpu.CompilerParams(dimension_semantics=sem, vmem_limit_bytes=VMEM_LIMIT)


def _pack_bf16_pairs(x):
    h = x.shape[1] // 2
    hi = lax.bitcast_convert_type(x[:, :h].astype(BF16).astype(F32), jnp.uint32)
    lo = lax.bitcast_convert_type(x[:, h:].astype(BF16).astype(F32), jnp.uint32)
    return hi | (lo >> 16)


def _unpack_bf16_pairs(w):
    hi = lax.bitcast_convert_type(w & jnp.uint32(0xFFFF0000), F32)
    lo = lax.bitcast_convert_type(w << 16, F32)
    return hi, lo


def _norm_matmul_kernel(x_ref, *rest, n_rope, moe):
    rest = list(rest)
    if moe:
        route_ref, y1_ref, y2_ref = rest[:3]
        x_out_ref = rest.pop()
        rest = rest[3:]
    g_ref, w_ref = rest[:2]
    if n_rope:
        cos_ref, sa_ref, sb_ref, o_ref = rest[2:]
    else:
        (o_ref,) = rest[2:]
    x = x_ref[...]
    if moe:
        route = route_ref[...]
        y1 = jnp.concatenate(_unpack_bf16_pairs(y1_ref[...]), axis=1)
        y2 = jnp.concatenate(_unpack_bf16_pairs(y2_ref[...]), axis=1)
        x = x + (route[:, 2:3] * y1 + route[:, 3:4] * y2)
        x_out_ref[...] = x
    xn = (x * lax.rsqrt(jnp.mean(x * x, axis=-1, keepdims=True) + RMS_EPS) * g_ref[...]).astype(BF16)
    for j in range(o_ref.shape[1] // TN):
        cols = slice(j * TN, (j + 1) * TN)
        acc = jnp.dot(xn, w_ref[:, cols], preferred_element_type=F32)
        if j < n_rope:
            reps = TN // cos_ref.shape[1]
            c = jnp.tile(cos_ref[...], (1, reps))
            sa = jnp.tile(sa_ref[...], (1, reps))
            sb = jnp.tile(sb_ref[...], (1, reps))
            nxt = pltpu.roll(acc, TN - 1, axis=1)
            prv = pltpu.roll(acc, 1, axis=1)
            acc = acc * c + nxt * sa + prv * sb
        o_ref[:, cols] = acc.astype(o_ref.dtype)


def _norm_matmul(x, g, w, rope=None, n_rope=0, seq=None, moe=None):
    t, d = x.shape
    n = w.shape[1]
    row = lambda i: (i, 0)
    in_specs = [pl.BlockSpec((TM, d), row)]
    args = [x]
    if moe is not None:
        route, yg = moe
        slot2 = t // TM
        in_specs += [pl.BlockSpec((TM, LANES), row), pl.BlockSpec((TM, d // 2), row),
                     pl.BlockSpec((TM, d // 2), lambda i: (slot2 + i, 0))]
        args += [route, yg, yg]
    in_specs += [
        pl.BlockSpec((1, d), lambda i: (0, 0)),
        pl.BlockSpec((d, n), lambda i: (0, 0), pipeline_mode=pl.Buffered(1)),
    ]
    args += [g.reshape(1, d), w]
    if rope is not None:
        nb = seq // TM
        dk = rope[0].shape[1]
        in_specs += [pl.BlockSpec((TM, dk), lambda i: (i % nb, 0))] * 3
        args += list(rope)
    out_specs = [pl.BlockSpec((TM, n), row)]
    out_shape = [jax.ShapeDtypeStruct((t, n), BF16)]
    if moe is not None:
        out_specs.append(pl.BlockSpec((TM, d), row))
        out_shape.append(jax.ShapeDtypeStruct((t, d), F32))
    out = pl.pallas_call(
        functools.partial(_norm_matmul_kernel, n_rope=n_rope, moe=moe is not None),
        grid=(t // TM,),
        in_specs=in_specs,
        out_specs=out_specs,
        out_shape=out_shape,
        compiler_params=_cparams(("parallel",)),
    )(*args)
    return out if moe is not None else out[0]


def _diff_attn_kernel(q_ref, k_ref, v_ref, bias_ref, lam_ref, g_ref, o_ref,
                      vt_ref, m_ref, acc_ref, s_ref, *, lambda_init):
    seq = q_ref.shape[0]
    hp, _, _, tq = bias_ref.shape
    dv = q_ref.shape[1] // hp
    dh = dv // 2
    nq = seq // tq

    def head(hh):
        return slice(hh * dv, (hh + 1) * dv)

    extra = vt_ref.shape[2] - dv
    ones_row = jnp.where(lax.broadcasted_iota(jnp.int32, (extra, tq), 0) == 0, 1.0, 0.0).astype(BF16)
    for hh in range(hp):
        for j in range(nq):
            vt_ref[hh, j, :dv, :] = v_ref[j * tq:(j + 1) * tq, head(hh)].astype(F32).T.astype(BF16)
            vt_ref[hh, j, dv:, :] = ones_row

    lq = lam_ref[...]
    lam = (jnp.exp(jnp.sum(lq[0:1] * lq[1:2], axis=-1, keepdims=True))
           - jnp.exp(jnp.sum(lq[2:3] * lq[3:4], axis=-1, keepdims=True)) + lambda_init)
    lane = lax.broadcasted_iota(jnp.int32, (tq, dv), 1)

    def q_body(qi, carry):
        qrows = pl.ds(pl.multiple_of(qi * tq, tq), tq)
        qs = []
        for hh in range(hp):
            q = q_ref[qrows, head(hh)]
            zero = jnp.zeros_like(q)
            qs.append(jnp.concatenate([jnp.where(lane < dh, q, zero), jnp.where(lane >= dh, q, zero)],
                                      axis=0))

        m_ref[...] = jnp.full(m_ref.shape, -jnp.inf, F32)
        acc_ref[...] = jnp.zeros(acc_ref.shape, F32)

        def score_dots(j):
            rows = pl.ds(pl.multiple_of(j * tq, tq), tq)
            for hh in range(hp):
                s_ref[hh] = lax.dot_general(k_ref[rows, head(hh)], qs[hh], (((1,), (1,)), ((), ())),
                                            preferred_element_type=F32)

        def step(j, tile, prefetch):
            probs, alphas = [], []
            for hh in range(hp):
                s = s_ref[hh]
                if tile is not None:
                    bias = bias_ref[hh, tile]
                    s = s + jnp.concatenate([bias, bias], axis=1)
                m_old = m_ref[hh]
                m_new = jnp.maximum(m_old, jnp.max(s, axis=0, keepdims=True))
                alpha = jnp.exp2(m_old - m_new)
                m_ref[hh] = m_new
                probs.append(jnp.exp2(s - m_new).astype(BF16))
                alphas.append(alpha)
            if prefetch:
                score_dots(j + 1)
            for hh in range(hp):
                acc_ref[hh] = alphas[hh] * acc_ref[hh] + jnp.dot(vt_ref[hh, j], probs[hh],
                                                                 preferred_element_type=F32)

        score_dots(0)

        def far_body(j, c):
            step(j, None, True)
            return c

        lax.fori_loop(0, jnp.maximum(qi - 1, 0), far_body, 0)

        @pl.when(qi >= 1)
        def _():
            step(qi - 1, 1, True)

        step(qi, 0, False)

        for hh in range(hp):
            acc = acc_ref[hh]
            on = acc[:dv] / acc[dv:dv + 1]
            ot = on[:, :tq] - lam * on[:, tq:]
            ot = ot * lax.rsqrt(jnp.mean(ot * ot, axis=0, keepdims=True) + SUBLN_EPS)
            o_ref[qrows, head(hh)] = (ot.T * g_ref[...] * (1.0 - lambda_init)).astype(o_ref.dtype)
        return carry

    lax.fori_loop(0, nq, q_body, 0)


def _t5_bucket(rel):
    max_exact = NUM_BUCKETS // 2
    n = jnp.maximum(rel, 0)
    is_small = n < max_exact
    nf = jnp.maximum(n, max_exact).astype(F32)
    large = max_exact + (jnp.log(nf / max_exact) / math.log(MAX_DISTANCE / max_exact)
                         * (NUM_BUCKETS - max_exact)).astype(jnp.int32)
    large = jnp.minimum(large, NUM_BUCKETS - 1)
    return jnp.where(is_small, n, large)


def _bias_tiles(rel_table):
    heads = rel_table.shape[1]
    span = 2 * TQ + 1
    bucket = _t5_bucket(jnp.arange(span))
    onehot = (bucket[:, None] == jnp.arange(NUM_BUCKETS)[None, :]).astype(F32)
    table = (rel_table.astype(F32) - rel_table[NUM_BUCKETS - 1].astype(F32)[None, :]) * LOG2E
    by_rel = jnp.dot(onehot, table, precision=lax.Precision.HIGHEST).T
    toep = jnp.tile(by_rel, (1, TQ))[:, :TQ * (span - 1)].reshape(heads, TQ, span - 1)
    kj = jnp.arange(TQ)[:, None]
    qi = jnp.arange(TQ)[None, :]
    diag = jnp.where((qi >= kj)[None], toep[:, :, :TQ], -jnp.inf)
    return jnp.stack([diag, toep[:, :, TQ:]], axis=1)


def _diff_attention(qkv, rel_table, lam_qk, subln_g, lambda_init, batch, seq, heads):
    t = qkv.shape[0]
    d = qkv.shape[1] // 3
    dv = d // heads
    hp = ATTN_HEADS
    assert TQ + 1 >= MAX_DISTANCE and seq % TQ == 0 and dv == LANES and heads % hp == 0
    nq = seq // TQ
    groups = heads // hp
    bias = _bias_tiles(rel_table)
    return pl.pallas_call(
        functools.partial(_diff_attn_kernel, lambda_init=lambda_init),
        grid=(batch, groups),
        in_specs=[
            pl.BlockSpec((seq, hp * dv), lambda b, g: (b, g)),
            pl.BlockSpec((seq, hp * dv), lambda b, g: (b, groups + g)),
            pl.BlockSpec((seq, hp * dv), lambda b, g: (b, 2 * groups + g)),
            pl.BlockSpec((hp, 2, TQ, TQ), lambda b, g: (g, 0, 0, 0)),
            pl.BlockSpec(lam_qk.shape, lambda b, g: (0, 0)),
            pl.BlockSpec((1, dv), lambda b, g: (0, 0)),
        ],
        out_specs=pl.BlockSpec((seq, hp * dv), lambda b, g: (b, g)),
        out_shape=jax.ShapeDtypeStruct((t, d), BF16),
        scratch_shapes=[
            pltpu.VMEM((hp, nq, dv + BF16_SUBLANES, TQ), BF16),
            pltpu.VMEM((hp, 1, 2 * TQ), F32),
            pltpu.VMEM((hp, dv + BF16_SUBLANES, 2 * TQ), F32),
            pltpu.VMEM((hp, TQ, 2 * TQ), F32),
        ],
        compiler_params=_cparams(("parallel", "parallel")),
    )(qkv, qkv, qkv, bias, lam_qk.astype(F32), subln_g.reshape(1, dv).astype(F32))


def _retention_kernel(q_ref, k_ref, v_ref, dec_ref, tab_ref, o_ref, state_ref):
    seq = q_ref.shape[0]
    c = dec_ref.shape[1]
    dec = dec_ref[0]
    tab = tab_ref[0]
    qd = tab[:, 0:1]
    kd = tab[:, 1:2]
    cd = tab[0:1, 2:3]
    state_ref[...] = jnp.zeros(state_ref.shape, F32)

    def body(n, carry):
        rows = pl.ds(pl.multiple_of(n * c, c), c)
        q = q_ref[rows, :]
        k = k_ref[rows, :]
        v = v_ref[rows, :]
        s = lax.dot_general(q, k, (((1,), (1,)), ((), ())), preferred_element_type=F32) * dec
        inner = jnp.dot(s.astype(BF16), v, preferred_element_type=F32)
        st = state_ref[...]
        cross = jnp.dot(q, st.astype(BF16), preferred_element_type=F32) * qd
        kt = (k.astype(F32) * kd).T.astype(BF16)
        state_ref[...] = st * cd + jnp.dot(kt, v, preferred_element_type=F32)
        o = inner + cross
        o = o * lax.rsqrt(jnp.mean(o * o, axis=-1, keepdims=True) + RMS_EPS)
        o_ref[rows, :] = o.astype(o_ref.dtype)
        return carry

    lax.fori_loop(0, seq // c, body, 0)


def _retention_tables(heads):
    c = RET_C
    log_gamma = jnp.log(1.0 - 2.0 ** (-5.0 - jnp.arange(heads, dtype=F32)))
    pos = jnp.arange(c, dtype=F32)
    dist = pos[:, None] - pos[None, :]
    inner_decay = jnp.where(dist[None] >= 0,
                            jnp.exp(jnp.maximum(dist, 0.0)[None] * log_gamma[:, None, None]), 0.0)
    q_decay = jnp.exp((pos + 1.0)[None] * log_gamma[:, None])
    k_decay = jnp.exp((c - 1.0 - pos)[None] * log_gamma[:, None])
    chunk_decay = jnp.broadcast_to(jnp.exp(c * log_gamma)[:, None], (heads, c))
    tab = jnp.stack([q_decay, k_decay, chunk_decay], axis=-1)
    tab = jnp.pad(tab, ((0, 0), (0, 0), (0, LANES - 3)))
    return inner_decay, tab


def _retention(qkvg, batch, seq, qk_dim, v_dim):
    t = qkvg.shape[0]
    dk = qk_dim // RET_HEADS
    dvh = v_dim // RET_HEADS
    inner_decay, tab = _retention_tables(RET_HEADS)
    kb = qk_dim // dk
    vb = 2 * qk_dim // dvh
    return pl.pallas_call(
        _retention_kernel,
        grid=(batch, RET_HEADS),
        in_specs=[
            pl.BlockSpec((seq, dk), lambda b, h: (b, h)),
            pl.BlockSpec((seq, dk), lambda b, h: (b, kb + h)),
            pl.BlockSpec((seq, dvh), lambda b, h: (b, vb + h)),
            pl.BlockSpec((1, RET_C, RET_C), lambda b, h: (h, 0, 0)),
            pl.BlockSpec((1, RET_C, LANES), lambda b, h: (h, 0, 0)),
        ],
        out_specs=pl.BlockSpec((seq, dvh), lambda b, h: (b, h)),
        out_shape=jax.ShapeDtypeStruct((t, v_dim), BF16),
        scratch_shapes=[pltpu.VMEM((dk, dvh), F32)],
        compiler_params=_cparams(("parallel", "arbitrary")),
    )(qkvg, qkvg, qkvg, inner_decay, tab)


def _out_router_kernel(*refs, gated, n_groups, n_experts):
    if gated:
        a_ref, gate_ref, w_ref, x_ref, nf_ref, wr_ref, br_ref = refs[:7]
        x1_ref, hn_ref, route_ref, route_t_ref, cnt_ref, carry_ref = refs[7:]
        gate = gate_ref[...]
        a = gate * jax.nn.sigmoid(gate) * a_ref[...]
    else:
        a_ref, w_ref, x_ref, nf_ref, wr_ref, br_ref = refs[:6]
        x1_ref, hn_ref, route_ref, route_t_ref, cnt_ref, carry_ref = refs[6:]
        a = a_ref[...]
    epg = n_experts // n_groups

    @pl.when(pl.program_id(0) == 0)
    def _():
        carry_ref[...] = jnp.zeros(carry_ref.shape, F32)

    x1 = x_ref[...] + jnp.dot(a, w_ref[...], preferred_element_type=F32)
    x1_ref[...] = x1
    hn = x1 * lax.rsqrt(jnp.mean(x1 * x1, axis=-1, keepdims=True) + RMS_EPS) * nf_ref[...]
    hn_ref[...] = _pack_bf16_pairs(hn)

    hn_hi = hn.astype(BF16)
    hn_lo = (hn - hn_hi.astype(F32)).astype(BF16)
    logits = (jnp.dot(hn_hi, wr_ref[0], preferred_element_type=F32)
              + jnp.dot(hn_lo, wr_ref[0], preferred_element_type=F32)
              + jnp.dot(hn_hi, wr_ref[1], preferred_element_type=F32)) + br_ref[...]
    tm = logits.shape[0]

    n_rows = -(-(n_groups + n_experts) // SUBLANES) * SUBLANES
    lt = logits.T[:n_rows, :]
    row = lax.broadcasted_iota(jnp.int32, lt.shape, 0)
    big = jnp.int32(LANES)
    neg = jnp.float32(-jnp.inf)

    gl = jnp.where(row < n_groups, lt, neg)
    gmax = jnp.max(gl, axis=0, keepdims=True)
    g_p = 1.0 / jnp.sum(jnp.exp(gl - gmax), axis=0, keepdims=True)
    g_idx = jnp.min(jnp.where(gl == gmax, row, big), axis=0, keepdims=True)

    lo = n_groups + g_idx * epg
    el = jnp.where((row >= lo) & (row < lo + epg), lt, neg)
    v1 = jnp.max(el, axis=0, keepdims=True)
    i1 = jnp.min(jnp.where(el == v1, row, big), axis=0, keepdims=True)
    el2 = jnp.where(row == i1, neg, el)
    v2 = jnp.max(el2, axis=0, keepdims=True)
    i2 = jnp.min(jnp.where(el2 == v2, row, big), axis=0, keepdims=True)
    tt = jnp.exp(v2 - v1)
    w1 = g_p / (1.0 + tt)
    w2 = g_p * tt / (1.0 + tt)
    e1 = i1 - n_groups
    e2 = i2 - n_groups

    erow = lax.broadcasted_iota(jnp.int32, (n_experts, tm), 0)
    oh1 = erow == e1
    oh2 = erow == e2
    onehot = jnp.where(oh1 | oh2, 1.0, 0.0)
    before = lax.broadcasted_iota(jnp.int32, (tm, tm), 0) < lax.broadcasted_iota(jnp.int32, (tm, tm), 1)
    tri = jnp.where(before, 1.0, 0.0).astype(BF16)
    carry = carry_ref[...]
    base = jnp.dot(onehot.astype(BF16), tri, preferred_element_type=F32) + carry[:, 0:1]
    rank1 = jnp.sum(jnp.where(oh1, base, 0.0), axis=0, keepdims=True)
    rank2 = jnp.sum(jnp.where(oh2, base, 0.0), axis=0, keepdims=True)
    carry = carry + jnp.sum(onehot, axis=1, keepdims=True)
    carry_ref[...] = carry
    cnt_ref[...] = carry

    frow = lax.broadcasted_iota(jnp.int32, (LANES, tm), 0)
    out = jnp.zeros((LANES, tm), F32)
    for idx, val in enumerate((e1.astype(F32), e2.astype(F32), w1, w2, rank1, rank2)):
        out = jnp.where(frow == idx, val, out)
    route_t_ref[...] = out[:route_t_ref.shape[0], :]
    route_ref[...] = out.T


def _out_router(a, gate, w, x, nf, w_group, b_group, w_expert, b_expert):
    t, d = x.shape
    k = w.shape[0]
    n_groups = w_group.shape[1]
    n_experts = w_expert.shape[1]
    assert n_groups + n_experts <= LANES
    pad = LANES - n_groups - n_experts
    wr = jnp.pad(jnp.concatenate([w_group, w_expert], axis=1).astype(F32), ((0, 0), (0, pad)))
    wr_hi = wr.astype(BF16)
    wr = jnp.stack([wr_hi, (wr - wr_hi.astype(F32)).astype(BF16)])
    br = jnp.pad(jnp.concatenate([b_group, b_expert]).astype(F32), (0, pad)).reshape(1, LANES)
    gated = gate is not None
    row = lambda i: (i, 0)
    fixed = lambda i: (0, 0)
    in_specs = [pl.BlockSpec((TM, k), row)]
    args = [a]
    if gated:
        gate_arr, gate_col = gate
        in_specs.append(pl.BlockSpec((TM, k), lambda i: (i, gate_col)))
        args.append(gate_arr)
    in_specs += [
        pl.BlockSpec((k, d), fixed),
        pl.BlockSpec((TM, d), row),
        pl.BlockSpec((1, d), fixed),
        pl.BlockSpec((2, d, LANES), lambda i: (0, 0, 0)),
        pl.BlockSpec((1, LANES), fixed),
    ]
    args += [w, x, nf.reshape(1, d), wr, br]
    return pl.pallas_call(
        functools.partial(_out_router_kernel, gated=gated, n_groups=n_groups, n_experts=n_experts),
        grid=(t // TM,),
        in_specs=in_specs,
        out_specs=[
            pl.BlockSpec((TM, d), row),
            pl.BlockSpec((TM, d // 2), row),
            pl.BlockSpec((TM, LANES), row),
            pl.BlockSpec((SUBLANES, TM), lambda i: (0, i)),
            pl.BlockSpec((n_experts, LANES), fixed),
        ],
        out_shape=[
            jax.ShapeDtypeStruct((t, d), F32),
            jax.ShapeDtypeStruct((t, d // 2), jnp.uint32),
            jax.ShapeDtypeStruct((t, LANES), F32),
            jax.ShapeDtypeStruct((SUBLANES, t), F32),
            jax.ShapeDtypeStruct((n_experts, LANES), F32),
        ],
        scratch_shapes=[pltpu.VMEM((n_experts, LANES), F32)],
        compiler_params=_cparams(("arbitrary",)),
    )(*args)


def _sc_mesh():
    return plsc.VectorSubcoreMesh(core_axis_name="c", subcore_axis_name="s",
                                  num_cores=SC_CORES, num_subcores=SC_SUBCORES)


def _sc_worker():
    return lax.axis_index("s") * SC_CORES + lax.axis_index("c")


def _sc_scatter_rows(src, d1, d2, n_rows):
    t, d = src.shape
    workers = SC_CORES * SC_SUBCORES
    per_w = t // workers
    n_ch = per_w // SC_CHUNK
    assert t % (workers * SC_CHUNK) == 0
    idx = jnp.stack([d1, d2]).reshape(2, workers, n_ch, SC_CHUNK)

    @functools.partial(
        pl.kernel, mesh=_sc_mesh(),
        out_type=jax.ShapeDtypeStruct((n_rows, d), src.dtype),
        scratch_types=[pltpu.VMEM((2, n_ch, SC_CHUNK), jnp.int32),
                       pltpu.VMEM((SC_CHUNK, d), src.dtype),
                       pltpu.VMEM((SC_CHUNK, d), src.dtype),
                       pltpu.SemaphoreType.DMA, pltpu.SemaphoreType.DMA],
    )
    def scatter(src_hbm, idx_hbm, out_hbm, idx_v, rows_a, rows_b, sem_a, sem_b):
        wid = _sc_worker()
        base = wid * per_w
        for s in range(2):
            pltpu.sync_copy(idx_hbm.at[s, wid], idx_v.at[s])

        def move(c, rows_v, sem):
            pltpu.sync_copy(src_hbm.at[pl.ds(base + c * SC_CHUNK, SC_CHUNK)], rows_v)
            return [pltpu.async_copy(rows_v, out_hbm.at[idx_v.at[s, c]], sem) for s in range(2)]

        @pl.loop(0, n_ch, step=2)
        def _(c):
            first = move(c, rows_a, sem_a)
            second = move(c + 1, rows_b, sem_b)
            for cp in first + second:
                cp.wait()

    return scatter(src, idx)


def _sc_gather_rows(table, idx):
    b = idx.shape[0]
    d = table.shape[1]
    workers = SC_CORES * SC_SUBCORES
    per_w = b // workers
    n_ch = per_w // SC_CHUNK
    assert b % (workers * SC_CHUNK * 2) == 0

    @functools.partial(
        pl.kernel, mesh=_sc_mesh(),
        out_type=jax.ShapeDtypeStruct((b, d), table.dtype),
        scratch_types=[pltpu.VMEM((per_w,), jnp.int32),
                       pltpu.VMEM((SC_CHUNK, d), table.dtype),
                       pltpu.VMEM((SC_CHUNK, d), table.dtype),
                       pltpu.SemaphoreType.DMA, pltpu.SemaphoreType.DMA],
    )
    def gather(table_hbm, idx_hbm, out_hbm, idx_v, rows_a, rows_b, sem_a, sem_b):
        base = _sc_worker() * per_w
        pltpu.sync_copy(idx_hbm.at[pl.ds(base, per_w)], idx_v)

        def fetch(c, rows_v, sem):
            return pltpu.async_copy(table_hbm.at[idx_v.at[pl.ds(c * SC_CHUNK, SC_CHUNK)]], rows_v, sem)

        def store(c, rows_v):
            pltpu.sync_copy(rows_v, out_hbm.at[pl.ds(base + c * SC_CHUNK, SC_CHUNK)])

        @pl.loop(0, n_ch, step=2)
        def _(c):
            first = fetch(c, rows_a, sem_a)
            second = fetch(c + 1, rows_b, sem_b)
            first.wait()
            store(c, rows_a)
            second.wait()
            store(c + 1, rows_b)

    return gather(table, idx)


def _expert_kernel(be_ref, nu_ref, nv_ref, x_ref, wg_ref, wu_ref, wd_ref, y_ref, wgb_ref, wub_ref, wdb_ref):
    i = pl.program_id(0)
    used = i < nu_ref[0]
    new_expert = (i == 0) | (be_ref[i] != be_ref[jnp.maximum(i - 1, 0)])

    @pl.when(used & new_expert)
    def _():
        wgb_ref[...] = wg_ref[0, 0].astype(BF16)
        wub_ref[...] = wu_ref[0, 0].astype(BF16)
        wdb_ref[...] = wd_ref[0, 0].astype(BF16)

    @pl.when(used)
    def _():
        valid = lax.broadcasted_iota(jnp.int32, (x_ref.shape[0], 1), 0) < nv_ref[i]
        x_hi, x_lo = _unpack_bf16_pairs(jnp.where(valid, x_ref[...], jnp.uint32(0)))
        x_hi, x_lo = x_hi.astype(BF16), x_lo.astype(BF16)
        h = x_hi.shape[1]

        def in_dot(w_ref):
            return (jnp.dot(x_hi, w_ref[:h, :], preferred_element_type=F32)
                    + jnp.dot(x_lo, w_ref[h:, :], preferred_element_type=F32))

        g = in_dot(wgb_ref)
        u = in_dot(wub_ref)
        hmid = (g * jax.nn.sigmoid(g) * u).astype(BF16)
        y_ref[...] = _pack_bf16_pairs(jnp.dot(hmid, wdb_ref[...], preferred_element_type=F32))

    @pl.when(jnp.logical_not(used))
    def _():
        y_ref[...] = jnp.zeros(y_ref.shape, y_ref.dtype)


def _experts(xs, block_e, n_used, n_valid, w_gate, w_up, w_down, layer):
    n_rows, dp = xs.shape
    d = 2 * dp
    ff = w_gate.shape[3]
    nb = n_rows // ROW_BLOCK
    last = lambda i, nu: jnp.maximum(jnp.minimum(i, nu[0] - 1), 0)
    blk = lambda i, be, nu, nv: (last(i, nu), 0)
    wsel = lambda i, be, nu, nv: (layer, be[last(i, nu)], 0, 0)
    grid_spec = pltpu.PrefetchScalarGridSpec(
        num_scalar_prefetch=3,
        grid=(nb,),
        in_specs=[
            pl.BlockSpec((ROW_BLOCK, dp), blk),
            pl.BlockSpec((1, 1, d, ff), wsel),
            pl.BlockSpec((1, 1, d, ff), wsel),
            pl.BlockSpec((1, 1, ff, d), wsel),
        ],
        out_specs=pl.BlockSpec((ROW_BLOCK, dp), lambda i, be, nu, nv: (i, 0)),
        scratch_shapes=[pltpu.VMEM((d, ff), BF16), pltpu.VMEM((d, ff), BF16), pltpu.VMEM((ff, d), BF16)],
    )
    return pl.pallas_call(
        _expert_kernel,
        grid_spec=grid_spec,
        out_shape=jax.ShapeDtypeStruct((n_rows, dp), jnp.uint32),
        compiler_params=_cparams(("arbitrary",)),
    )(block_e, n_used, n_valid, xs, w_gate, w_up, w_down)


def _final_combine_kernel(x_ref, route_ref, y1_ref, y2_ref, nf_ref, o_ref):
    route = route_ref[...]
    y1 = jnp.concatenate(_unpack_bf16_pairs(y1_ref[...]), axis=1)
    y2 = jnp.concatenate(_unpack_bf16_pairs(y2_ref[...]), axis=1)
    y = x_ref[...] + (route[:, 2:3] * y1 + route[:, 3:4] * y2)
    o_ref[...] = y * lax.rsqrt(jnp.mean(y * y, axis=-1, keepdims=True) + RMS_EPS) * nf_ref[...]


def _final_combine(x1, route, yg, nf):
    t, d = x1.shape
    slot2 = t // TM
    row = lambda i: (i, 0)
    return pl.pallas_call(
        _final_combine_kernel,
        grid=(t // TM,),
        in_specs=[
            pl.BlockSpec((TM, d), row),
            pl.BlockSpec((TM, LANES), row),
            pl.BlockSpec((TM, d // 2), row),
            pl.BlockSpec((TM, d // 2), lambda i: (slot2 + i, 0)),
            pl.BlockSpec((1, d), lambda i: (0, 0)),
        ],
        out_specs=pl.BlockSpec((TM, d), row),
        out_shape=jax.ShapeDtypeStruct((t, d), F32),
        compiler_params=_cparams(("parallel",)),
    )(x1, route, yg, yg, nf.reshape(1, d).astype(F32))


def _moe(hn, route_t, cnt, w_gate, w_up, w_down, layer):
    t, d = hn.shape
    n_experts = w_gate.shape[1]
    ids = jnp.arange(n_experts, dtype=jnp.int32)
    e1, e2, rank1, rank2 = (route_t[f].astype(jnp.int32) for f in (0, 1, 4, 5))
    counts = cnt[:, 0].astype(jnp.int32)
    padded = ((counts + ROW_BLOCK - 1) // ROW_BLOCK) * ROW_BLOCK
    pad_end = jnp.cumsum(padded)
    pad_start = pad_end - padded

    def dest(e, rank):
        return jnp.sum(jnp.where(e[:, None] == ids[None, :], pad_start[None, :], 0), axis=1) + rank

    d1 = dest(e1, rank1)
    d2 = dest(e2, rank2)
    nb = (t * TOP_K_INNER) // ROW_BLOCK + n_experts
    n_used = (pad_end[-1:] // ROW_BLOCK).astype(jnp.int32)
    first_row = jnp.arange(nb, dtype=jnp.int32) * ROW_BLOCK
    block_e = jnp.sum((pad_end[None, :] <= first_row[:, None]).astype(jnp.int32), axis=1)
    block_e = jnp.minimum(block_e, n_experts - 1)
    row_end = jnp.sum(jnp.where(block_e[:, None] == ids[None, :], (pad_start + counts)[None, :], 0), axis=1)
    n_valid = jnp.clip(row_end - first_row, 0, ROW_BLOCK).astype(jnp.int32)
    xs = _sc_scatter_rows(hn, d1, d2, nb * ROW_BLOCK)
    ys = _experts(xs, block_e, n_used, n_valid, w_gate, w_up, w_down, layer)
    return _sc_gather_rows(ys, jnp.concatenate([d1, d2]))


def _diff_lambda_init(layer_idx):
    return 0.8 - 0.6 * math.exp(-0.3 * layer_idx)


def _rope_tables(seq, dk):
    angle = 1.0 / (10000.0 ** jnp.linspace(0.0, 1.0, dk // 2, dtype=F32))
    angle = jnp.repeat(angle, 2)
    phase = jnp.arange(seq, dtype=F32)[:, None] * angle[None, :]
    sin, cos = jnp.sin(phase), jnp.cos(phase)
    even = (jnp.arange(dk) % 2 == 0)[None, :]
    return cos, jnp.where(even, -sin, 0.0), jnp.where(even, 0.0, sin)


def kernel(x, rel_bias_table, norm_mix, norm_ffn, norm_final, diff_w_in, diff_lambda_qk, diff_subln,
           diff_w_out, ret_w_in, ret_w_out, moe_w_group, moe_b_group, moe_w_expert, moe_b_expert,
           moe_w_gate, moe_w_up, moe_w_down):
    batch, seq, d = x.shape
    depth = norm_mix.shape[0]
    t = batch * seq
    heads = rel_bias_table.shape[1]
    dh = diff_lambda_qk.shape[-1]
    qk_dim = (ret_w_in.shape[2] - 2 * ret_w_out.shape[1]) // 2
    v_dim = ret_w_out.shape[1]
    dk = qk_dim // RET_HEADS
    xt = x.reshape(t, d)
    moe = None
    for i in range(depth):
        j = i // N_MIXERS
        if i % N_MIXERS == 0:
            col_scale = jnp.concatenate([jnp.full((d,), dh ** -0.5 * LOG2E, F32), jnp.ones((2 * d,), F32)])
            w_in = (diff_w_in[j] * col_scale[None, :]).astype(BF16)
            qkv = _norm_matmul(xt, norm_mix[i], w_in, moe=moe)
            if moe is not None:
                qkv, xt = qkv
            mix = _diff_attention(qkv, rel_bias_table, diff_lambda_qk[j], diff_subln[j],
                                  _diff_lambda_init(i), batch, seq, heads)
            gate = None
            w_out = diff_w_out[j].astype(BF16)
        else:
            n_in = ret_w_in.shape[2]
            col_scale = jnp.concatenate([jnp.ones((qk_dim,), F32), jnp.full((qk_dim,), dk ** -0.5, F32),
                                         jnp.ones((n_in - 2 * qk_dim,), F32)])
            w_in = (ret_w_in[j] * col_scale[None, :]).astype(BF16)
            qkvg = _norm_matmul(xt, norm_mix[i], w_in, rope=_rope_tables(seq, dk),
                                n_rope=2 * qk_dim // TN, seq=seq, moe=moe)
            if moe is not None:
                qkvg, xt = qkvg
            mix = _retention(qkvg, batch, seq, qk_dim, v_dim)
            gate = (qkvg, (2 * qk_dim + v_dim) // v_dim)
            w_out = ret_w_out[j].astype(BF16)
        x1, hn, route, route_t, cnt = _out_router(mix, gate, w_out, xt, norm_ffn[i], moe_w_group[i],
                                                  moe_b_group[i], moe_w_expert[i], moe_b_expert[i])
        xt = x1
        moe = (route, _moe(hn, route_t, cnt, moe_w_gate, moe_w_up, moe_w_down, i))
    return _final_combine(xt, moe[0], moe[1], norm_final).reshape(batch, seq, d)
```

```python
import functools
import math

import jax
import jax.numpy as jnp
from jax import lax
from jax.experimental import pallas as pl
from jax.experimental.pallas import tpu as pltpu
from jax.experimental.pallas import tpu_sc as plsc

F32 = jnp.float32
BF16 = jnp.bfloat16

N_MIXERS = 2
NUM_BUCKETS = 32
MAX_DISTANCE = 128
RET_HEADS = 4
TOP_K_INNER = 2
RMS_EPS = 1e-6
SUBLN_EPS = 1e-5

LANES = 128
SUBLANES = 8
BF16_SUBLANES = 16
LOG2E = 1.4426950408889634
VMEM_LIMIT = 56 * 1024 * 1024
SC_CORES = 2
SC_SUBCORES = 16

TM = 512
TN = 1024
TQ = 256
ATTN_HEADS = 4
RET_C = 256
ROW_BLOCK = 256
SC_CHUNK = 64


def _cparams(sem):
    return pltpu.CompilerParams(dimension_semantics=sem, vmem_limit_bytes=VMEM_LIMIT)


def _pack_bf16_pairs(x):
    h = x.shape[1] // 2
    hi = lax.bitcast_convert_type(x[:, :h].astype(BF16).astype(F32), jnp.uint32)
    lo = lax.bitcast_convert_type(x[:, h:].astype(BF16).astype(F32), jnp.uint32)
    return hi | (lo >> 16)


def _unpack_bf16_pairs(w):
    hi = lax.bitcast_convert_type(w & jnp.uint32(0xFFFF0000), F32)
    lo = lax.bitcast_convert_type(w << 16, F32)
    return hi, lo


def _norm_matmul_kernel(x_ref, *rest, n_rope, moe):
    rest = list(rest)
    if moe:
        route_ref, y1_ref, y2_ref = rest[:3]
        x_out_ref = rest.pop()
        rest = rest[3:]
    g_ref, w_ref = rest[:2]
    if n_rope:
        cos_ref, sa_ref, sb_ref, o_ref = rest[2:]
    else:
        (o_ref,) = rest[2:]
    x = x_ref[...]
    if moe:
        route = route_ref[...]
        y1 = jnp.concatenate(_unpack_bf16_pairs(y1_ref[...]), axis=1)
        y2 = jnp.concatenate(_unpack_bf16_pairs(y2_ref[...]), axis=1)
        x = x + (route[:, 2:3] * y1 + route[:, 3:4] * y2)
        x_out_ref[...] = x
    xn = (x * lax.rsqrt(jnp.mean(x * x, axis=-1, keepdims=True) + RMS_EPS) * g_ref[...]).astype(BF16)
    for j in range(o_ref.shape[1] // TN):
        cols = slice(j * TN, (j + 1) * TN)
        acc = jnp.dot(xn, w_ref[:, cols], preferred_element_type=F32)
        if j < n_rope:
            reps = TN // cos_ref.shape[1]
            c = jnp.tile(cos_ref[...], (1, reps))
            sa = jnp.tile(sa_ref[...], (1, reps))
            sb = jnp.tile(sb_ref[...], (1, reps))
            nxt = pltpu.roll(acc, TN - 1, axis=1)
            prv = pltpu.roll(acc, 1, axis=1)
            acc = acc * c + nxt * sa + prv * sb
        o_ref[:, cols] = acc.astype(o_ref.dtype)


def _norm_matmul(x, g, w, rope=None, n_rope=0, seq=None, moe=None):
    t, d = x.shape
    n = w.shape[1]
    row = lambda i: (i, 0)
    in_specs = [pl.BlockSpec((TM, d), row)]
    args = [x]
    if moe is not None:
        route, yg = moe
        slot2 = t // TM
        in_specs += [pl.BlockSpec((TM, LANES), row), pl.BlockSpec((TM, d // 2), row),
                     pl.BlockSpec((TM, d // 2), lambda i: (slot2 + i, 0))]
        args += [route, yg, yg]
    in_specs += [
        pl.BlockSpec((1, d), lambda i: (0, 0)),
        pl.BlockSpec((d, n), lambda i: (0, 0), pipeline_mode=pl.Buffered(1)),
    ]
    args += [g.reshape(1, d), w]
    if rope is not None:
        nb = seq // TM
        dk = rope[0].shape[1]
        in_specs += [pl.BlockSpec((TM, dk), lambda i: (i % nb, 0))] * 3
        args += list(rope)
    out_specs = [pl.BlockSpec((TM, n), row)]
    out_shape = [jax.ShapeDtypeStruct((t, n), BF16)]
    if moe is not None:
        out_specs.append(pl.BlockSpec((TM, d), row))
        out_shape.append(jax.ShapeDtypeStruct((t, d), F32))
    out = pl.pallas_call(
        functools.partial(_norm_matmul_kernel, n_rope=n_rope, moe=moe is not None),
        grid=(t // TM,),
        in_specs=in_specs,
        out_specs=out_specs,
        out_shape=out_shape,
        compiler_params=_cparams(("parallel",)),
    )(*args)
    return out if moe is not None else out[0]


def _diff_attn_kernel(q_ref, k_ref, v_ref, bias_ref, lam_ref, g_ref, o_ref,
                      vt_ref, m_ref, acc_ref, s_ref, *, lambda_init):
    seq = q_ref.shape[0]
    hp, _, _, tq = bias_ref.shape
    dv = q_ref.shape[1] // hp
    dh = dv // 2
    nq = seq // tq

    def head(hh):
        return slice(hh * dv, (hh + 1) * dv)

    extra = vt_ref.shape[2] - dv
    ones_row = jnp.where(lax.broadcasted_iota(jnp.int32, (extra, tq), 0) == 0, 1.0, 0.0).astype(BF16)
    for hh in range(hp):
        for j in range(nq):
            vt_ref[hh, j, :dv, :] = v_ref[j * tq:(j + 1) * tq, head(hh)].astype(F32).T.astype(BF16)
            vt_ref[hh, j, dv:, :] = ones_row

    lq = lam_ref[...]
    lam = (jnp.exp(jnp.sum(lq[0:1] * lq[1:2], axis=-1, keepdims=True))
           - jnp.exp(jnp.sum(lq[2:3] * lq[3:4], axis=-1, keepdims=True)) + lambda_init)
    lane = lax.broadcasted_iota(jnp.int32, (tq, dv), 1)

    def q_body(qi, carry):
        qrows = pl.ds(pl.multiple_of(qi * tq, tq), tq)
        qs = []
        for hh in range(hp):
            q = q_ref[qrows, head(hh)]
            zero = jnp.zeros_like(q)
            qs.append(jnp.concatenate([jnp.where(lane < dh, q, zero), jnp.where(lane >= dh, q, zero)],
                                      axis=0))

        m_ref[...] = jnp.full(m_ref.shape, -jnp.inf, F32)
        acc_ref[...] = jnp.zeros(acc_ref.shape, F32)

        def score_dots(j):
            rows = pl.ds(pl.multiple_of(j * tq, tq), tq)
            for hh in range(hp):
                s_ref[hh] = lax.dot_general(k_ref[rows, head(hh)], qs[hh], (((1,), (1,)), ((), ())),
                                            preferred_element_type=F32)

        def step(j, tile, prefetch):
            probs, alphas = [], []
            for hh in range(hp):
                s = s_ref[hh]
                if tile is not None:
                    bias = bias_ref[hh, tile]
                    s = s + jnp.concatenate([bias, bias], axis=1)
                m_old = m_ref[hh]
                m_new = jnp.maximum(m_old, jnp.max(s, axis=0, keepdims=True))
                alpha = jnp.exp2(m_old - m_new)
                m_ref[hh] = m_new
                probs.append(jnp.exp2(s - m_new).astype(BF16))
                alphas.append(alpha)
            if prefetch:
                score_dots(j + 1)
            for hh in range(hp):
                acc_ref[hh] = alphas[hh] * acc_ref[hh] + jnp.dot(vt_ref[hh, j], probs[hh],
                                                                 preferred_element_type=F32)

        score_dots(0)

        def far_body(j, c):
            step(j, None, True)
            return c

        lax.fori_loop(0, jnp.maximum(qi - 1, 0), far_body, 0)

        @pl.when(qi >= 1)
        def _():
            step(qi - 1, 1, True)

        step(qi, 0, False)

        for hh in range(hp):
            acc = acc_ref[hh]
            on = acc[:dv] / acc[dv:dv + 1]
            ot = on[:, :tq] - lam * on[:, tq:]
            ot = ot * lax.rsqrt(jnp.mean(ot * ot, axis=0, keepdims=True) + SUBLN_EPS)
            o_ref[qrows, head(hh)] = (ot.T * g_ref[...] * (1.0 - lambda_init)).astype(o_ref.dtype)
        return carry

    lax.fori_loop(0, nq, q_body, 0)


def _t5_bucket(rel):
    max_exact = NUM_BUCKETS // 2
    n = jnp.maximum(rel, 0)
    is_small = n < max_exact
    nf = jnp.maximum(n, max_exact).astype(F32)
    large = max_exact + (jnp.log(nf / max_exact) / math.log(MAX_DISTANCE / max_exact)
                         * (NUM_BUCKETS - max_exact)).astype(jnp.int32)
    large = jnp.minimum(large, NUM_BUCKETS - 1)
    return jnp.where(is_small, n, large)


def _bias_tiles(rel_table):
    heads = rel_table.shape[1]
    span = 2 * TQ + 1
    bucket = _t5_bucket(jnp.arange(span))
    onehot = (bucket[:, None] == jnp.arange(NUM_BUCKETS)[None, :]).astype(F32)
    table = (rel_table.astype(F32) - rel_table[NUM_BUCKETS - 1].astype(F32)[None, :]) * LOG2E
    by_rel = jnp.dot(onehot, table, precision=lax.Precision.HIGHEST).T
    toep = jnp.tile(by_rel, (1, TQ))[:, :TQ * (span - 1)].reshape(heads, TQ, span - 1)
    kj = jnp.arange(TQ)[:, None]
    qi = jnp.arange(TQ)[None, :]
    diag = jnp.where((qi >= kj)[None], toep[:, :, :TQ], -jnp.inf)
    return jnp.stack([diag, toep[:, :, TQ:]], axis=1)


def _diff_attention(qkv, rel_table, lam_qk, subln_g, lambda_init, batch, seq, heads):
    t = qkv.shape[0]
    d = qkv.shape[1] // 3
    dv = d // heads
    hp = ATTN_HEADS
    assert TQ + 1 >= MAX_DISTANCE and seq % TQ == 0 and dv == LANES and heads % hp == 0
    nq = seq // TQ
    groups = heads // hp
    bias = _bias_tiles(rel_table)
    return pl.pallas_call(
        functools.partial(_diff_attn_kernel, lambda_init=lambda_init),
        grid=(batch, groups),
        in_specs=[
            pl.BlockSpec((seq, hp * dv), lambda b, g: (b, g)),
            pl.BlockSpec((seq, hp * dv), lambda b, g: (b, groups + g)),
            pl.BlockSpec((seq, hp * dv), lambda b, g: (b, 2 * groups + g)),
            pl.BlockSpec((hp, 2, TQ, TQ), lambda b, g: (g, 0, 0, 0)),
            pl.BlockSpec(lam_qk.shape, lambda b, g: (0, 0)),
            pl.BlockSpec((1, dv), lambda b, g: (0, 0)),
        ],
        out_specs=pl.BlockSpec((seq, hp * dv), lambda b, g: (b, g)),
        out_shape=jax.ShapeDtypeStruct((t, d), BF16),
        scratch_shapes=[
            pltpu.VMEM((hp, nq, dv + BF16_SUBLANES, TQ), BF16),
            pltpu.VMEM((hp, 1, 2 * TQ), F32),
            pltpu.VMEM((hp, dv + BF16_SUBLANES, 2 * TQ), F32),
            pltpu.VMEM((hp, TQ, 2 * TQ), F32),
        ],
        compiler_params=_cparams(("parallel", "parallel")),
    )(qkv, qkv, qkv, bias, lam_qk.astype(F32), subln_g.reshape(1, dv).astype(F32))


def _retention_kernel(q_ref, k_ref, v_ref, dec_ref, tab_ref, o_ref, state_ref):
    seq = q_ref.shape[0]
    c = dec_ref.shape[1]
    dec = dec_ref[0]
    tab = tab_ref[0]
    qd = tab[:, 0:1]
    kd = tab[:, 1:2]
    cd = tab[0:1, 2:3]
    state_ref[...] = jnp.zeros(state_ref.shape, F32)

    def body(n, carry):
        rows = pl.ds(pl.multiple_of(n * c, c), c)
        q = q_ref[rows, :]
        k = k_ref[rows, :]
        v = v_ref[rows, :]
        s = lax.dot_general(q, k, (((1,), (1,)), ((), ())), preferred_element_type=F32) * dec
        inner = jnp.dot(s.astype(BF16), v, preferred_element_type=F32)
        st = state_ref[...]
        cross = jnp.dot(q, st.astype(BF16), preferred_element_type=F32) * qd
        kt = (k.astype(F32) * kd).T.astype(BF16)
        state_ref[...] = st * cd + jnp.dot(kt, v, preferred_element_type=F32)
        o = inner + cross
        o = o * lax.rsqrt(jnp.mean(o * o, axis=-1, keepdims=True) + RMS_EPS)
        o_ref[rows, :] = o.astype(o_ref.dtype)
        return carry

    lax.fori_loop(0, seq // c, body, 0)


def _retention_tables(heads):
    c = RET_C
    log_gamma = jnp.log(1.0 - 2.0 ** (-5.0 - jnp.arange(heads, dtype=F32)))
    pos = jnp.arange(c, dtype=F32)
    dist = pos[:, None] - pos[None, :]
    inner_decay = jnp.where(dist[None] >= 0,
                            jnp.exp(jnp.maximum(dist, 0.0)[None] * log_gamma[:, None, None]), 0.0)
    q_decay = jnp.exp((pos + 1.0)[None] * log_gamma[:, None])
    k_decay = jnp.exp((c - 1.0 - pos)[None] * log_gamma[:, None])
    chunk_decay = jnp.broadcast_to(jnp.exp(c * log_gamma)[:, None], (heads, c))
    tab = jnp.stack([q_decay, k_decay, chunk_decay], axis=-1)
    tab = jnp.pad(tab, ((0, 0), (0, 0), (0, LANES - 3)))
    return inner_decay, tab


def _retention(qkvg, batch, seq, qk_dim, v_dim):
    t = qkvg.shape[0]
    dk = qk_dim // RET_HEADS
    dvh = v_dim // RET_HEADS
    inner_decay, tab = _retention_tables(RET_HEADS)
    kb = qk_dim // dk
    vb = 2 * qk_dim // dvh
    return pl.pallas_call(
        _retention_kernel,
        grid=(batch, RET_HEADS),
        in_specs=[
            pl.BlockSpec((seq, dk), lambda b, h: (b, h)),
            pl.BlockSpec((seq, dk), lambda b, h: (b, kb + h)),
            pl.BlockSpec((seq, dvh), lambda b, h: (b, vb + h)),
            pl.BlockSpec((1, RET_C, RET_C), lambda b, h: (h, 0, 0)),
            pl.BlockSpec((1, RET_C, LANES), lambda b, h: (h, 0, 0)),
        ],
        out_specs=pl.BlockSpec((seq, dvh), lambda b, h: (b, h)),
        out_shape=jax.ShapeDtypeStruct((t, v_dim), BF16),
        scratch_shapes=[pltpu.VMEM((dk, dvh), F32)],
        compiler_params=_cparams(("parallel", "arbitrary")),
    )(qkvg, qkvg, qkvg, inner_decay, tab)


def _out_router_kernel(*refs, gated, n_groups, n_experts):
    if gated:
        a_ref, gate_ref, w_ref, x_ref, nf_ref, wr_ref, br_ref = refs[:7]
        x1_ref, hn_ref, route_ref, route_t_ref, cnt_ref, carry_ref = refs[7:]
        gate = gate_ref[...]
        a = gate * jax.nn.sigmoid(gate) * a_ref[...]
    else:
        a_ref, w_ref, x_ref, nf_ref, wr_ref, br_ref = refs[:6]
        x1_ref, hn_ref, route_ref, route_t_ref, cnt_ref, carry_ref = refs[6:]
        a = a_ref[...]
    epg = n_experts // n_groups

    @pl.when(pl.program_id(0) == 0)
    def _():
        carry_ref[...] = jnp.zeros(carry_ref.shape, F32)

    x1 = x_ref[...] + jnp.dot(a, w_ref[...], preferred_element_type=F32)
    x1_ref[...] = x1
    hn = x1 * lax.rsqrt(jnp.mean(x1 * x1, axis=-1, keepdims=True) + RMS_EPS) * nf_ref[...]
    hn_ref[...] = _pack_bf16_pairs(hn)

    hn_hi = hn.astype(BF16)
    hn_lo = (hn - hn_hi.astype(F32)).astype(BF16)
    logits = (jnp.dot(hn_hi, wr_ref[0], preferred_element_type=F32)
              + jnp.dot(hn_lo, wr_ref[0], preferred_element_type=F32)
              + jnp.dot(hn_hi, wr_ref[1], preferred_element_type=F32)) + br_ref[...]
    tm = logits.shape[0]

    n_rows = -(-(n_groups + n_experts) // SUBLANES) * SUBLANES
    lt = logits.T[:n_rows, :]
    row = lax.broadcasted_iota(jnp.int32, lt.shape, 0)
    big = jnp.int32(LANES)
    neg = jnp.float32(-jnp.inf)

    gl = jnp.where(row < n_groups, lt, neg)
    gmax = jnp.max(gl, axis=0, keepdims=True)
    g_p = 1.0 / jnp.sum(jnp.exp(gl - gmax), axis=0, keepdims=True)
    g_idx = jnp.min(jnp.where(gl == gmax, row, big), axis=0, keepdims=True)

    lo = n_groups + g_idx * epg
    el = jnp.where((row >= lo) & (row < lo + epg), lt, neg)
    v1 = jnp.max(el, axis=0, keepdims=True)
    i1 = jnp.min(jnp.where(el == v1, row, big), axis=0, keepdims=True)
    el2 = jnp.where(row == i1, neg, el)
    v2 = jnp.max(el2, axis=0, keepdims=True)
    i2 = jnp.min(jnp.where(el2 == v2, row, big), axis=0, keepdims=True)
    tt = jnp.exp(v2 - v1)
    w1 = g_p / (1.0 + tt)
    w2 = g_p * tt / (1.0 + tt)
    e1 = i1 - n_groups
    e2 = i2 - n_groups

    erow = lax.broadcasted_iota(jnp.int32, (n_experts, tm), 0)
    oh1 = erow == e1
    oh2 = erow == e2
    onehot = jnp.where(oh1 | oh2, 1.0, 0.0)
    before = lax.broadcasted_iota(jnp.int32, (tm, tm), 0) < lax.broadcasted_iota(jnp.int32, (tm, tm), 1)
    tri = jnp.where(before, 1.0, 0.0).astype(BF16)
    carry = carry_ref[...]
    base = jnp.dot(onehot.astype(BF16), tri, preferred_element_type=F32) + carry[:, 0:1]
    rank1 = jnp.sum(jnp.where(oh1, base, 0.0), axis=0, keepdims=True)
    rank2 = jnp.sum(jnp.where(oh2, base, 0.0), axis=0, keepdims=True)
    carry = carry + jnp.sum(onehot, axis=1, keepdims=True)
    carry_ref[...] = carry
    cnt_ref[...] = carry

    frow = lax.broadcasted_iota(jnp.int32, (LANES, tm), 0)
    out = jnp.zeros((LANES, tm), F32)
    for idx, val in enumerate((e1.astype(F32), e2.astype(F32), w1, w2, rank1, rank2)):
        out = jnp.where(frow == idx, val, out)
    route_t_ref[...] = out[:route_t_ref.shape[0], :]
    route_ref[...] = out.T


def _out_router(a, gate, w, x, nf, w_group, b_group, w_expert, b_expert):
    t, d = x.shape
    k = w.shape[0]
    n_groups = w_group.shape[1]
    n_experts = w_expert.shape[1]
    assert n_groups + n_experts <= LANES
    pad = LANES - n_groups - n_experts
    wr = jnp.pad(jnp.concatenate([w_group, w_expert], axis=1).astype(F32), ((0, 0), (0, pad)))
    wr_hi = wr.astype(BF16)
    wr = jnp.stack([wr_hi, (wr - wr_hi.astype(F32)).astype(BF16)])
    br = jnp.pad(jnp.concatenate([b_group, b_expert]).astype(F32), (0, pad)).reshape(1, LANES)
    gated = gate is not None
    row = lambda i: (i, 0)
    fixed = lambda i: (0, 0)
    in_specs = [pl.BlockSpec((TM, k), row)]
    args = [a]
    if gated:
        gate_arr, gate_col = gate
        in_specs.append(pl.BlockSpec((TM, k), lambda i: (i, gate_col)))
        args.append(gate_arr)
    in_specs += [
        pl.BlockSpec((k, d), fixed),
        pl.BlockSpec((TM, d), row),
        pl.BlockSpec((1, d), fixed),
        pl.BlockSpec((2, d, LANES), lambda i: (0, 0, 0)),
        pl.BlockSpec((1, LANES), fixed),
    ]
    args += [w, x, nf.reshape(1, d), wr, br]
    return pl.pallas_call(
        functools.partial(_out_router_kernel, gated=gated, n_groups=n_groups, n_experts=n_experts),
        grid=(t // TM,),
        in_specs=in_specs,
        out_specs=[
            pl.BlockSpec((TM, d), row),
            pl.BlockSpec((TM, d // 2), row),
            pl.BlockSpec((TM, LANES), row),
            pl.BlockSpec((SUBLANES, TM), lambda i: (0, i)),
            pl.BlockSpec((n_experts, LANES), fixed),
        ],
        out_shape=[
            jax.ShapeDtypeStruct((t, d), F32),
            jax.ShapeDtypeStruct((t, d // 2), jnp.uint32),
            jax.ShapeDtypeStruct((t, LANES), F32),
            jax.ShapeDtypeStruct((SUBLANES, t), F32),
            jax.ShapeDtypeStruct((n_experts, LANES), F32),
        ],
        scratch_shapes=[pltpu.VMEM((n_experts, LANES), F32)],
        compiler_params=_cparams(("arbitrary",)),
    )(*args)


def _sc_mesh():
    return plsc.VectorSubcoreMesh(core_axis_name="c", subcore_axis_name="s",
                                  num_cores=SC_CORES, num_subcores=SC_SUBCORES)


def _sc_worker():
    return lax.axis_index("s") * SC_CORES + lax.axis_index("c")


def _sc_scatter_rows(src, d1, d2, n_rows):
    t, d = src.shape
    workers = SC_CORES * SC_SUBCORES
    per_w = t // workers
    n_ch = per_w // SC_CHUNK
    assert t % (workers * SC_CHUNK) == 0
    idx = jnp.stack([d1, d2]).reshape(2, workers, n_ch, SC_CHUNK)

    @functools.partial(
        pl.kernel, mesh=_sc_mesh(),
        out_type=jax.ShapeDtypeStruct((n_rows, d), src.dtype),
        scratch_types=[pltpu.VMEM((2, n_ch, SC_CHUNK), jnp.int32),
                       pltpu.VMEM((SC_CHUNK, d), src.dtype),
                       pltpu.VMEM((SC_CHUNK, d), src.dtype),
                       pltpu.SemaphoreType.DMA, pltpu.SemaphoreType.DMA],
    )
    def scatter(src_hbm, idx_hbm, out_hbm, idx_v, rows_a, rows_b, sem_a, sem_b):
        wid = _sc_worker()
        base = wid * per_w
        for s in range(2):
            pltpu.sync_copy(idx_hbm.at[s, wid], idx_v.at[s])

        def move(c, rows_v, sem):
            pltpu.sync_copy(src_hbm.at[pl.ds(base + c * SC_CHUNK, SC_CHUNK)], rows_v)
            return [pltpu.async_copy(rows_v, out_hbm.at[idx_v.at[s, c]], sem) for s in range(2)]

        @pl.loop(0, n_ch, step=2)
        def _(c):
            first = move(c, rows_a, sem_a)
            second = move(c + 1, rows_b, sem_b)
            for cp in first + second:
                cp.wait()

    return scatter(src, idx)


def _sc_gather_rows(table, idx):
    b = idx.shape[0]
    d = table.shape[1]
    workers = SC_CORES * SC_SUBCORES
    per_w = b // workers
    n_ch = per_w // SC_CHUNK
    assert b % (workers * SC_CHUNK * 2) == 0

    @functools.partial(
        pl.kernel, mesh=_sc_mesh(),
        out_type=jax.ShapeDtypeStruct((b, d), table.dtype),
        scratch_types=[pltpu.VMEM((per_w,), jnp.int32),
                       pltpu.VMEM((SC_CHUNK, d), table.dtype),
                       pltpu.VMEM((SC_CHUNK, d), table.dtype),
                       pltpu.SemaphoreType.DMA, pltpu.SemaphoreType.DMA],
    )
    def gather(table_hbm, idx_hbm, out_hbm, idx_v, rows_a, rows_b, sem_a, sem_b):
        base = _sc_worker() * per_w
        pltpu.sync_copy(idx_hbm.at[pl.ds(base, per_w)], idx_v)

        def fetch(c, rows_v, sem):
            return pltpu.async_copy(table_hbm.at[idx_v.at[pl.ds(c * SC_CHUNK, SC_CHUNK)]], rows_v, sem)

        def store(c, rows_v):
            pltpu.sync_copy(rows_v, out_hbm.at[pl.ds(base + c * SC_CHUNK, SC_CHUNK)])

        @pl.loop(0, n_ch, step=2)
        def _(c):
            first = fetch(c, rows_a, sem_a)
            second = fetch(c + 1, rows_b, sem_b)
            first.wait()
            store(c, rows_a)
            second.wait()
            store(c + 1, rows_b)

    return gather(table, idx)


def _expert_kernel(fb_ref, nblk_ref, nv_ref, nu_ref, xs_ref, wg_ref, wu_ref, wd_ref, ys_ref,
                   wgb_ref, wub_ref, wdb_ref, xbuf, ybuf, xsem, ysem):
    e = pl.program_id(0)
    rb = xbuf.shape[1]
    nb = ys_ref.shape[0] // rb
    n_used = nu_ref[0]

    def x_copy(g, slot):
        rows = pl.ds(pl.multiple_of(g * rb, rb), rb)
        return pltpu.make_async_copy(xs_ref.at[rows], xbuf.at[slot], xsem.at[slot])

    def y_copy(g, slot):
        rows = pl.ds(pl.multiple_of(g * rb, rb), rb)
        return pltpu.make_async_copy(ybuf.at[slot], ys_ref.at[rows], ysem.at[slot])

    @pl.when(e == 0)
    def _():
        x_copy(0, 0).start()

    @pl.when(nblk_ref[e] > 0)
    def _():
        wgb_ref[...] = wg_ref[0, 0].astype(BF16)
        wub_ref[...] = wu_ref[0, 0].astype(BF16)
        wdb_ref[...] = wd_ref[0, 0].astype(BF16)

    def block(b, carry):
        g = fb_ref[e] + b
        slot = lax.rem(g, 2)
        x_copy(g, slot).wait()

        @pl.when(g + 1 < n_used)
        def _():
            x_copy(g + 1, 1 - slot).start()

        @pl.when(g >= 2)
        def _():
            y_copy(g - 2, slot).wait()

        valid = lax.broadcasted_iota(jnp.int32, (rb, 1), 0) < nv_ref[g]
        x_hi, x_lo = _unpack_bf16_pairs(jnp.where(valid, xbuf[slot], jnp.uint32(0)))
        x_hi, x_lo = x_hi.astype(BF16), x_lo.astype(BF16)
        h = x_hi.shape[1]

        def in_dot(w_ref):
            return (jnp.dot(x_hi, w_ref[:h, :], preferred_element_type=F32)
                    + jnp.dot(x_lo, w_ref[h:, :], preferred_element_type=F32))

        gate = in_dot(wgb_ref)
        up = in_dot(wub_ref)
        hmid = (gate * jax.nn.sigmoid(gate) * up).astype(BF16)
        ybuf[slot] = _pack_bf16_pairs(jnp.dot(hmid, wdb_ref[...], preferred_element_type=F32))
        y_copy(g, slot).start()
        return carry

    lax.fori_loop(0, nblk_ref[e], block, 0)

    @pl.when(e == pl.num_programs(0) - 1)
    def _():
        @pl.when(n_used >= 2)
        def _():
            y_copy(n_used - 2, lax.rem(n_used, 2)).wait()

        y_copy(n_used - 1, lax.rem(n_used - 1, 2)).wait()
        ybuf[0] = jnp.zeros(ybuf.shape[1:], ybuf.dtype)

        def tail_start(g, carry):
            y_copy(g, 0).start()
            return carry

        def tail_wait(g, carry):
            y_copy(g, 0).wait()
            return carry

        lax.fori_loop(n_used, nb, tail_start, 0)
        lax.fori_loop(n_used, nb, tail_wait, 0)


def _experts(xs, first_block, n_blocks, n_valid, n_used, w_gate, w_up, w_down, layer):
    n_rows, dp = xs.shape
    d = 2 * dp
    n_experts, ff = w_gate.shape[1], w_gate.shape[3]
    wsel = lambda e, *_: (layer, e, 0, 0)
    grid_spec = pltpu.PrefetchScalarGridSpec(
        num_scalar_prefetch=4,
        grid=(n_experts,),
        in_specs=[
            pl.BlockSpec(memory_space=pl.ANY),
            pl.BlockSpec((1, 1, d, ff), wsel),
            pl.BlockSpec((1, 1, d, ff), wsel),
            pl.BlockSpec((1, 1, ff, d), wsel),
        ],
        out_specs=pl.BlockSpec(memory_space=pl.ANY),
        scratch_shapes=[pltpu.VMEM((d, ff), BF16), pltpu.VMEM((d, ff), BF16), pltpu.VMEM((ff, d), BF16),
                        pltpu.VMEM((2, ROW_BLOCK, dp), xs.dtype), pltpu.VMEM((2, ROW_BLOCK, dp), xs.dtype),
                        pltpu.SemaphoreType.DMA((2,)), pltpu.SemaphoreType.DMA((2,))],
    )
    return pl.pallas_call(
        _expert_kernel,
        grid_spec=grid_spec,
        out_shape=jax.ShapeDtypeStruct((n_rows, dp), xs.dtype),
        compiler_params=_cparams(("arbitrary",)),
    )(first_block, n_blocks, n_valid, n_used, xs, w_gate, w_up, w_down)


def _final_combine_kernel(x_ref, route_ref, y1_ref, y2_ref, nf_ref, o_ref):
    route = route_ref[...]
    y1 = jnp.concatenate(_unpack_bf16_pairs(y1_ref[...]), axis=1)
    y2 = jnp.concatenate(_unpack_bf16_pairs(y2_ref[...]), axis=1)
    y = x_ref[...] + (route[:, 2:3] * y1 + route[:, 3:4] * y2)
    o_ref[...] = y * lax.rsqrt(jnp.mean(y * y, axis=-1, keepdims=True) + RMS_EPS) * nf_ref[...]


def _final_combine(x1, route, yg, nf):
    t, d = x1.shape
    slot2 = t // TM
    row = lambda i: (i, 0)
    return pl.pallas_call(
        _final_combine_kernel,
        grid=(t // TM,),
        in_specs=[
            pl.BlockSpec((TM, d), row),
            pl.BlockSpec((TM, LANES), row),
            pl.BlockSpec((TM, d // 2), row),
            pl.BlockSpec((TM, d // 2), lambda i: (slot2 + i, 0)),
            pl.BlockSpec((1, d), lambda i: (0, 0)),
        ],
        out_specs=pl.BlockSpec((TM, d), row),
        out_shape=jax.ShapeDtypeStruct((t, d), F32),
        compiler_params=_cparams(("parallel",)),
    )(x1, route, yg, yg, nf.reshape(1, d).astype(F32))


def _moe(hn, route_t, cnt, w_gate, w_up, w_down, layer):
    t, d = hn.shape
    n_experts = w_gate.shape[1]
    ids = jnp.arange(n_experts, dtype=jnp.int32)
    e1, e2, rank1, rank2 = (route_t[f].astype(jnp.int32) for f in (0, 1, 4, 5))
    counts = cnt[:, 0].astype(jnp.int32)
    padded = ((counts + ROW_BLOCK - 1) // ROW_BLOCK) * ROW_BLOCK
    pad_end = jnp.cumsum(padded)
    pad_start = pad_end - padded

    def dest(e, rank):
        return jnp.sum(jnp.where(e[:, None] == ids[None, :], pad_start[None, :], 0), axis=1) + rank

    d1 = dest(e1, rank1)
    d2 = dest(e2, rank2)
    nb = (t * TOP_K_INNER) // ROW_BLOCK + n_experts
    n_used = (pad_end[-1:] // ROW_BLOCK).astype(jnp.int32)
    first_row = jnp.arange(nb, dtype=jnp.int32) * ROW_BLOCK
    block_e = jnp.sum((pad_end[None, :] <= first_row[:, None]).astype(jnp.int32), axis=1)
    block_e = jnp.minimum(block_e, n_experts - 1)
    row_end = jnp.sum(jnp.where(block_e[:, None] == ids[None, :], (pad_start + counts)[None, :], 0), axis=1)
    n_valid = jnp.clip(row_end - first_row, 0, ROW_BLOCK).astype(jnp.int32)
    xs = _sc_scatter_rows(hn, d1, d2, nb * ROW_BLOCK)
    ys = _experts(xs, (pad_start // ROW_BLOCK).astype(jnp.int32), (padded // ROW_BLOCK).astype(jnp.int32),
                  n_valid, n_used, w_gate, w_up, w_down, layer)
    return _sc_gather_rows(ys, jnp.concatenate([d1, d2]))


def _diff_lambda_init(layer_idx):
    return 0.8 - 0.6 * math.exp(-0.3 * layer_idx)


def _rope_tables(seq, dk):
    angle = 1.0 / (10000.0 ** jnp.linspace(0.0, 1.0, dk // 2, dtype=F32))
    angle = jnp.repeat(angle, 2)
    phase = jnp.arange(seq, dtype=F32)[:, None] * angle[None, :]
    sin, cos = jnp.sin(phase), jnp.cos(phase)
    even = (jnp.arange(dk) % 2 == 0)[None, :]
    return cos, jnp.where(even, -sin, 0.0), jnp.where(even, 0.0, sin)


def kernel(x, rel_bias_table, norm_mix, norm_ffn, norm_final, diff_w_in, diff_lambda_qk, diff_subln,
           diff_w_out, ret_w_in, ret_w_out, moe_w_group, moe_b_group, moe_w_expert, moe_b_expert,
           moe_w_gate, moe_w_up, moe_w_down):
    batch, seq, d = x.shape
    depth = norm_mix.shape[0]
    t = batch * seq
    heads = rel_bias_table.shape[1]
    dh = diff_lambda_qk.shape[-1]
    qk_dim = (ret_w_in.shape[2] - 2 * ret_w_out.shape[1]) // 2
    v_dim = ret_w_out.shape[1]
    dk = qk_dim // RET_HEADS
    xt = x.reshape(t, d)
    moe = None
    for i in range(depth):
        j = i // N_MIXERS
        if i % N_MIXERS == 0:
            col_scale = jnp.concatenate([jnp.full((d,), dh ** -0.5 * LOG2E, F32), jnp.ones((2 * d,), F32)])
            w_in = (diff_w_in[j] * col_scale[None, :]).astype(BF16)
            qkv = _norm_matmul(xt, norm_mix[i], w_in, moe=moe)
            if moe is not None:
                qkv, xt = qkv
            mix = _diff_attention(qkv, rel_bias_table, diff_lambda_qk[j], diff_subln[j],
                                  _diff_lambda_init(i), batch, seq, heads)
            gate = None
            w_out = diff_w_out[j].astype(BF16)
        else:
            n_in = ret_w_in.shape[2]
            col_scale = jnp.concatenate([jnp.ones((qk_dim,), F32), jnp.full((qk_dim,), dk ** -0.5, F32),
                                         jnp.ones((n_in - 2 * qk_dim,), F32)])
            w_in = (ret_w_in[j] * col_scale[None, :]).astype(BF16)
            qkvg = _norm_matmul(xt, norm_mix[i], w_in, rope=_rope_tables(seq, dk),
                                n_rope=2 * qk_dim // TN, seq=seq, moe=moe)
            if moe is not None:
                qkvg, xt = qkvg
            mix = _retention(qkvg, batch, seq, qk_dim, v_dim)
            gate = (qkvg, (2 * qk_dim + v_dim) // v_dim)
            w_out = ret_w_out[j].astype(BF16)
        x1, hn, route, route_t, cnt = _out_router(mix, gate, w_out, xt, norm_ffn[i], moe_w_group[i],
                                                  moe_b_group[i], moe_w_expert[i], moe_b_expert[i])
        xt = x1
        moe = (route, _moe(hn, route_t, cnt, moe_w_gate, moe_w_up, moe_w_down, i))
    return _final_combine(xt, moe[0], moe[1], norm_final).reshape(batch, seq, d)
```

```python
import functools
import math

import jax
import jax.numpy as jnp
from jax import lax
from jax.experimental import pallas as pl
from jax.experimental.pallas import tpu as pltpu
from jax.experimental.pallas import tpu_sc as plsc

F32 = jnp.float32
BF16 = jnp.bfloat16

N_MIXERS = 2
NUM_BUCKETS = 32
MAX_DISTANCE = 128
RET_HEADS = 4
TOP_K_INNER = 2
RMS_EPS = 1e-6
SUBLN_EPS = 1e-5

LANES = 128
SUBLANES = 8
BF16_SUBLANES = 16
LOG2E = 1.4426950408889634
VMEM_LIMIT = 56 * 1024 * 1024
SC_CORES = 2
SC_SUBCORES = 16

TM = 512
TN = 1024
TQ = 256
ATTN_HEADS = 4
RET_C = 256
ROW_BLOCK = 512
SC_CHUNK = 64


def _cparams(sem):
    return pltpu.CompilerParams(dimension_semantics=sem, vmem_limit_bytes=VMEM_LIMIT)


def _pack_bf16_pairs(x):
    h = x.shape[1] // 2
    hi = lax.bitcast_convert_type(x[:, :h].astype(BF16).astype(F32), jnp.uint32)
    lo = lax.bitcast_convert_type(x[:, h:].astype(BF16).astype(F32), jnp.uint32)
    return hi | (lo >> 16)


def _unpack_bf16_pairs(w):
    hi = lax.bitcast_convert_type(w & jnp.uint32(0xFFFF0000), F32)
    lo = lax.bitcast_convert_type(w << 16, F32)
    return hi, lo


def _norm_matmul_kernel(x_ref, *rest, n_rope, moe):
    rest = list(rest)
    if moe:
        route_ref, y1_ref, y2_ref = rest[:3]
        x_out_ref = rest.pop()
        rest = rest[3:]
    g_ref, w_ref = rest[:2]
    if n_rope:
        cos_ref, sa_ref, sb_ref, o_ref = rest[2:]
    else:
        (o_ref,) = rest[2:]
    x = x_ref[...]
    if moe:
        route = route_ref[...]
        y1 = jnp.concatenate(_unpack_bf16_pairs(y1_ref[...]), axis=1)
        y2 = jnp.concatenate(_unpack_bf16_pairs(y2_ref[...]), axis=1)
        x = x + (route[:, 2:3] * y1 + route[:, 3:4] * y2)
        x_out_ref[...] = x
    xn = (x * lax.rsqrt(jnp.mean(x * x, axis=-1, keepdims=True) + RMS_EPS) * g_ref[...]).astype(BF16)
    for j in range(o_ref.shape[1] // TN):
        cols = slice(j * TN, (j + 1) * TN)
        acc = jnp.dot(xn, w_ref[:, cols], preferred_element_type=F32)
        if j < n_rope:
            reps = TN // cos_ref.shape[1]
            c = jnp.tile(cos_ref[...], (1, reps))
            sa = jnp.tile(sa_ref[...], (1, reps))
            sb = jnp.tile(sb_ref[...], (1, reps))
            nxt = pltpu.roll(acc, TN - 1, axis=1)
            prv = pltpu.roll(acc, 1, axis=1)
            acc = acc * c + nxt * sa + prv * sb
        o_ref[:, cols] = acc.astype(o_ref.dtype)


def _norm_matmul(x, g, w, rope=None, n_rope=0, seq=None, moe=None):
    t, d = x.shape
    n = w.shape[1]
    row = lambda i: (i, 0)
    in_specs = [pl.BlockSpec((TM, d), row)]
    args = [x]
    if moe is not None:
        route, yg = moe
        slot2 = t // TM
        in_specs += [pl.BlockSpec((TM, LANES), row), pl.BlockSpec((TM, d // 2), row),
                     pl.BlockSpec((TM, d // 2), lambda i: (slot2 + i, 0))]
        args += [route, yg, yg]
    in_specs += [
        pl.BlockSpec((1, d), lambda i: (0, 0)),
        pl.BlockSpec((d, n), lambda i: (0, 0), pipeline_mode=pl.Buffered(1)),
    ]
    args += [g.reshape(1, d), w]
    if rope is not None:
        nb = seq // TM
        dk = rope[0].shape[1]
        in_specs += [pl.BlockSpec((TM, dk), lambda i: (i % nb, 0))] * 3
        args += list(rope)
    out_specs = [pl.BlockSpec((TM, n), row)]
    out_shape = [jax.ShapeDtypeStruct((t, n), BF16)]
    if moe is not None:
        out_specs.append(pl.BlockSpec((TM, d), row))
        out_shape.append(jax.ShapeDtypeStruct((t, d), F32))
    out = pl.pallas_call(
        functools.partial(_norm_matmul_kernel, n_rope=n_rope, moe=moe is not None),
        grid=(t // TM,),
        in_specs=in_specs,
        out_specs=out_specs,
        out_shape=out_shape,
        compiler_params=_cparams(("parallel",)),
    )(*args)
    return out if moe is not None else out[0]


def _diff_attn_kernel(q_ref, k_ref, v_ref, bias_ref, lam_ref, g_ref, o_ref,
                      vt_ref, m_ref, acc_ref, s_ref, *, lambda_init):
    seq = q_ref.shape[0]
    hp, _, _, tq = bias_ref.shape
    dv = q_ref.shape[1] // hp
    dh = dv // 2
    nq = seq // tq

    def head(hh):
        return slice(hh * dv, (hh + 1) * dv)

    extra = vt_ref.shape[2] - dv
    ones_row = jnp.where(lax.broadcasted_iota(jnp.int32, (extra, tq), 0) == 0, 1.0, 0.0).astype(BF16)
    for hh in range(hp):
        for j in range(nq):
            vt_ref[hh, j, :dv, :] = v_ref[j * tq:(j + 1) * tq, head(hh)].astype(F32).T.astype(BF16)
            vt_ref[hh, j, dv:, :] = ones_row

    lq = lam_ref[...]
    lam = (jnp.exp(jnp.sum(lq[0:1] * lq[1:2], axis=-1, keepdims=True))
           - jnp.exp(jnp.sum(lq[2:3] * lq[3:4], axis=-1, keepdims=True)) + lambda_init)
    lane = lax.broadcasted_iota(jnp.int32, (tq, dv), 1)

    def q_body(qi, carry):
        qrows = pl.ds(pl.multiple_of(qi * tq, tq), tq)
        qs = []
        for hh in range(hp):
            q = q_ref[qrows, head(hh)]
            zero = jnp.zeros_like(q)
            qs.append(jnp.concatenate([jnp.where(lane < dh, q, zero), jnp.where(lane >= dh, q, zero)],
                                      axis=0))

        m_ref[...] = jnp.full(m_ref.shape, -jnp.inf, F32)
        acc_ref[...] = jnp.zeros(acc_ref.shape, F32)

        def score_dots(j):
            rows = pl.ds(pl.multiple_of(j * tq, tq), tq)
            for hh in range(hp):
                s_ref[hh] = lax.dot_general(k_ref[rows, head(hh)], qs[hh], (((1,), (1,)), ((), ())),
                                            preferred_element_type=F32)

        def step(j, tile, prefetch):
            probs, alphas = [], []
            for hh in range(hp):
                s = s_ref[hh]
                if tile is not None:
                    bias = bias_ref[hh, tile]
                    s = s + jnp.concatenate([bias, bias], axis=1)
                m_old = m_ref[hh]
                m_new = jnp.maximum(m_old, jnp.max(s, axis=0, keepdims=True))
                alpha = jnp.exp2(m_old - m_new)
                m_ref[hh] = m_new
                probs.append(jnp.exp2(s - m_new).astype(BF16))
                alphas.append(alpha)
            if prefetch:
                score_dots(j + 1)
            for hh in range(hp):
                acc_ref[hh] = alphas[hh] * acc_ref[hh] + jnp.dot(vt_ref[hh, j], probs[hh],
                                                                 preferred_element_type=F32)

        score_dots(0)

        def far_body(j, c):
            step(j, None, True)
            return c

        lax.fori_loop(0, jnp.maximum(qi - 1, 0), far_body, 0)

        @pl.when(qi >= 1)
        def _():
            step(qi - 1, 1, True)

        step(qi, 0, False)

        for hh in range(hp):
            acc = acc_ref[hh]
            on = acc[:dv] / acc[dv:dv + 1]
            ot = on[:, :tq] - lam * on[:, tq:]
            ot = ot * lax.rsqrt(jnp.mean(ot * ot, axis=0, keepdims=True) + SUBLN_EPS)
            o_ref[qrows, head(hh)] = (ot.T * g_ref[...] * (1.0 - lambda_init)).astype(o_ref.dtype)
        return carry

    lax.fori_loop(0, nq, q_body, 0)


def _t5_bucket(rel):
    max_exact = NUM_BUCKETS // 2
    n = jnp.maximum(rel, 0)
    is_small = n < max_exact
    nf = jnp.maximum(n, max_exact).astype(F32)
    large = max_exact + (jnp.log(nf / max_exact) / math.log(MAX_DISTANCE / max_exact)
                         * (NUM_BUCKETS - max_exact)).astype(jnp.int32)
    large = jnp.minimum(large, NUM_BUCKETS - 1)
    return jnp.where(is_small, n, large)


def _bias_tiles(rel_table):
    heads = rel_table.shape[1]
    span = 2 * TQ + 1
    bucket = _t5_bucket(jnp.arange(span))
    onehot = (bucket[:, None] == jnp.arange(NUM_BUCKETS)[None, :]).astype(F32)
    table = (rel_table.astype(F32) - rel_table[NUM_BUCKETS - 1].astype(F32)[None, :]) * LOG2E
    by_rel = jnp.dot(onehot, table, precision=lax.Precision.HIGHEST).T
    toep = jnp.tile(by_rel, (1, TQ))[:, :TQ * (span - 1)].reshape(heads, TQ, span - 1)
    kj = jnp.arange(TQ)[:, None]
    qi = jnp.arange(TQ)[None, :]
    diag = jnp.where((qi >= kj)[None], toep[:, :, :TQ], -jnp.inf)
    return jnp.stack([diag, toep[:, :, TQ:]], axis=1)


def _diff_attention(qkv, rel_table, lam_qk, subln_g, lambda_init, batch, seq, heads):
    t = qkv.shape[0]
    d = qkv.shape[1] // 3
    dv = d // heads
    hp = ATTN_HEADS
    assert TQ + 1 >= MAX_DISTANCE and seq % TQ == 0 and dv == LANES and heads % hp == 0
    nq = seq // TQ
    groups = heads // hp
    bias = _bias_tiles(rel_table)
    return pl.pallas_call(
        functools.partial(_diff_attn_kernel, lambda_init=lambda_init),
        grid=(batch, groups),
        in_specs=[
            pl.BlockSpec((seq, hp * dv), lambda b, g: (b, g)),
            pl.BlockSpec((seq, hp * dv), lambda b, g: (b, groups + g)),
            pl.BlockSpec((seq, hp * dv), lambda b, g: (b, 2 * groups + g)),
            pl.BlockSpec((hp, 2, TQ, TQ), lambda b, g: (g, 0, 0, 0)),
            pl.BlockSpec(lam_qk.shape, lambda b, g: (0, 0)),
            pl.BlockSpec((1, dv), lambda b, g: (0, 0)),
        ],
        out_specs=pl.BlockSpec((seq, hp * dv), lambda b, g: (b, g)),
        out_shape=jax.ShapeDtypeStruct((t, d), BF16),
        scratch_shapes=[
            pltpu.VMEM((hp, nq, dv + BF16_SUBLANES, TQ), BF16),
            pltpu.VMEM((hp, 1, 2 * TQ), F32),
            pltpu.VMEM((hp, dv + BF16_SUBLANES, 2 * TQ), F32),
            pltpu.VMEM((hp, TQ, 2 * TQ), F32),
        ],
        compiler_params=_cparams(("parallel", "parallel")),
    )(qkv, qkv, qkv, bias, lam_qk.astype(F32), subln_g.reshape(1, dv).astype(F32))


def _retention_kernel(q_ref, k_ref, v_ref, dec_ref, tab_ref, o_ref, state_ref):
    seq = q_ref.shape[0]
    c = dec_ref.shape[1]
    dec = dec_ref[0]
    tab = tab_ref[0]
    qd = tab[:, 0:1]
    kd = tab[:, 1:2]
    cd = tab[0:1, 2:3]
    state_ref[...] = jnp.zeros(state_ref.shape, F32)

    def body(n, carry):
        rows = pl.ds(pl.multiple_of(n * c, c), c)
        q = q_ref[rows, :]
        k = k_ref[rows, :]
        v = v_ref[rows, :]
        s = lax.dot_general(q, k, (((1,), (1,)), ((), ())), preferred_element_type=F32) * dec
        inner = jnp.dot(s.astype(BF16), v, preferred_element_type=F32)
        st = state_ref[...]
        cross = jnp.dot(q, st.astype(BF16), preferred_element_type=F32) * qd
        kt = (k.astype(F32) * kd).T.astype(BF16)
        state_ref[...] = st * cd + jnp.dot(kt, v, preferred_element_type=F32)
        o = inner + cross
        o = o * lax.rsqrt(jnp.mean(o * o, axis=-1, keepdims=True) + RMS_EPS)
        o_ref[rows, :] = o.astype(o_ref.dtype)
        return carry

    lax.fori_loop(0, seq // c, body, 0)


def _retention_tables(heads):
    c = RET_C
    log_gamma = jnp.log(1.0 - 2.0 ** (-5.0 - jnp.arange(heads, dtype=F32)))
    pos = jnp.arange(c, dtype=F32)
    dist = pos[:, None] - pos[None, :]
    inner_decay = jnp.where(dist[None] >= 0,
                            jnp.exp(jnp.maximum(dist, 0.0)[None] * log_gamma[:, None, None]), 0.0)
    q_decay = jnp.exp((pos + 1.0)[None] * log_gamma[:, None])
    k_decay = jnp.exp((c - 1.0 - pos)[None] * log_gamma[:, None])
    chunk_decay = jnp.broadcast_to(jnp.exp(c * log_gamma)[:, None], (heads, c))
    tab = jnp.stack([q_decay, k_decay, chunk_decay], axis=-1)
    tab = jnp.pad(tab, ((0, 0), (0, 0), (0, LANES - 3)))
    return inner_decay, tab


def _retention(qkvg, batch, seq, qk_dim, v_dim):
    t = qkvg.shape[0]
    dk = qk_dim // RET_HEADS
    dvh = v_dim // RET_HEADS
    inner_decay, tab = _retention_tables(RET_HEADS)
    kb = qk_dim // dk
    vb = 2 * qk_dim // dvh
    return pl.pallas_call(
        _retention_kernel,
        grid=(batch, RET_HEADS),
        in_specs=[
            pl.BlockSpec((seq, dk), lambda b, h: (b, h)),
            pl.BlockSpec((seq, dk), lambda b, h: (b, kb + h)),
            pl.BlockSpec((seq, dvh), lambda b, h: (b, vb + h)),
            pl.BlockSpec((1, RET_C, RET_C), lambda b, h: (h, 0, 0)),
            pl.BlockSpec((1, RET_C, LANES), lambda b, h: (h, 0, 0)),
        ],
        out_specs=pl.BlockSpec((seq, dvh), lambda b, h: (b, h)),
        out_shape=jax.ShapeDtypeStruct((t, v_dim), BF16),
        scratch_shapes=[pltpu.VMEM((dk, dvh), F32)],
        compiler_params=_cparams(("parallel", "arbitrary")),
    )(qkvg, qkvg, qkvg, inner_decay, tab)


def _out_router_kernel(*refs, gated, n_groups, n_experts):
    if gated:
        a_ref, gate_ref, w_ref, x_ref, nf_ref, wr_ref, br_ref = refs[:7]
        x1_ref, hn_ref, route_ref, route_t_ref, cnt_ref, carry_ref = refs[7:]
        gate = gate_ref[...]
        a = gate * jax.nn.sigmoid(gate) * a_ref[...]
    else:
        a_ref, w_ref, x_ref, nf_ref, wr_ref, br_ref = refs[:6]
        x1_ref, hn_ref, route_ref, route_t_ref, cnt_ref, carry_ref = refs[6:]
        a = a_ref[...]
    epg = n_experts // n_groups

    @pl.when(pl.program_id(0) == 0)
    def _():
        carry_ref[...] = jnp.zeros(carry_ref.shape, F32)

    x1 = x_ref[...] + jnp.dot(a, w_ref[...], preferred_element_type=F32)
    x1_ref[...] = x1
    hn = x1 * lax.rsqrt(jnp.mean(x1 * x1, axis=-1, keepdims=True) + RMS_EPS) * nf_ref[...]
    hn_ref[...] = _pack_bf16_pairs(hn)

    hn_hi = hn.astype(BF16)
    hn_lo = (hn - hn_hi.astype(F32)).astype(BF16)
    logits = (jnp.dot(hn_hi, wr_ref[0], preferred_element_type=F32)
              + jnp.dot(hn_lo, wr_ref[0], preferred_element_type=F32)
              + jnp.dot(hn_hi, wr_ref[1], preferred_element_type=F32)) + br_ref[...]
    tm = logits.shape[0]

    n_rows = -(-(n_groups + n_experts) // SUBLANES) * SUBLANES
    lt = logits.T[:n_rows, :]
    row = lax.broadcasted_iota(jnp.int32, lt.shape, 0)
    big = jnp.int32(LANES)
    neg = jnp.float32(-jnp.inf)

    gl = jnp.where(row < n_groups, lt, neg)
    gmax = jnp.max(gl, axis=0, keepdims=True)
    g_p = 1.0 / jnp.sum(jnp.exp(gl - gmax), axis=0, keepdims=True)
    g_idx = jnp.min(jnp.where(gl == gmax, row, big), axis=0, keepdims=True)

    lo = n_groups + g_idx * epg
    el = jnp.where((row >= lo) & (row < lo + epg), lt, neg)
    v1 = jnp.max(el, axis=0, keepdims=True)
    i1 = jnp.min(jnp.where(el == v1, row, big), axis=0, keepdims=True)
    el2 = jnp.where(row == i1, neg, el)
    v2 = jnp.max(el2, axis=0, keepdims=True)
    i2 = jnp.min(jnp.where(el2 == v2, row, big), axis=0, keepdims=True)
    tt = jnp.exp(v2 - v1)
    w1 = g_p / (1.0 + tt)
    w2 = g_p * tt / (1.0 + tt)
    e1 = i1 - n_groups
    e2 = i2 - n_groups

    erow = lax.broadcasted_iota(jnp.int32, (n_experts, tm), 0)
    oh1 = erow == e1
    oh2 = erow == e2
    onehot = jnp.where(oh1 | oh2, 1.0, 0.0)
    before = lax.broadcasted_iota(jnp.int32, (tm, tm), 0) < lax.broadcasted_iota(jnp.int32, (tm, tm), 1)
    tri = jnp.where(before, 1.0, 0.0).astype(BF16)
    carry = carry_ref[...]
    base = jnp.dot(onehot.astype(BF16), tri, preferred_element_type=F32) + carry[:, 0:1]
    rank1 = jnp.sum(jnp.where(oh1, base, 0.0), axis=0, keepdims=True)
    rank2 = jnp.sum(jnp.where(oh2, base, 0.0), axis=0, keepdims=True)
    carry = carry + jnp.sum(onehot, axis=1, keepdims=True)
    carry_ref[...] = carry
    cnt_ref[...] = carry

    frow = lax.broadcasted_iota(jnp.int32, (LANES, tm), 0)
    out = jnp.zeros((LANES, tm), F32)
    for idx, val in enumerate((e1.astype(F32), e2.astype(F32), w1, w2, rank1, rank2)):
        out = jnp.where(frow == idx, val, out)
    route_t_ref[...] = out[:route_t_ref.shape[0], :]
    route_ref[...] = out.T


def _out_router(a, gate, w, x, nf, w_group, b_group, w_expert, b_expert):
    t, d = x.shape
    k = w.shape[0]
    n_groups = w_group.shape[1]
    n_experts = w_expert.shape[1]
    assert n_groups + n_experts <= LANES
    pad = LANES - n_groups - n_experts
    wr = jnp.pad(jnp.concatenate([w_group, w_expert], axis=1).astype(F32), ((0, 0), (0, pad)))
    wr_hi = wr.astype(BF16)
    wr = jnp.stack([wr_hi, (wr - wr_hi.astype(F32)).astype(BF16)])
    br = jnp.pad(jnp.concatenate([b_group, b_expert]).astype(F32), (0, pad)).reshape(1, LANES)
    gated = gate is not None
    row = lambda i: (i, 0)
    fixed = lambda i: (0, 0)
    in_specs = [pl.BlockSpec((TM, k), row)]
    args = [a]
    if gated:
        gate_arr, gate_col = gate
        in_specs.append(pl.BlockSpec((TM, k), lambda i: (i, gate_col)))
        args.append(gate_arr)
    in_specs += [
        pl.BlockSpec((k, d), fixed),
        pl.BlockSpec((TM, d), row),
        pl.BlockSpec((1, d), fixed),
        pl.BlockSpec((2, d, LANES), lambda i: (0, 0, 0)),
        pl.BlockSpec((1, LANES), fixed),
    ]
    args += [w, x, nf.reshape(1, d), wr, br]
    return pl.pallas_call(
        functools.partial(_out_router_kernel, gated=gated, n_groups=n_groups, n_experts=n_experts),
        grid=(t // TM,),
        in_specs=in_specs,
        out_specs=[
            pl.BlockSpec((TM, d), row),
            pl.BlockSpec((TM, d // 2), row),
            pl.BlockSpec((TM, LANES), row),
            pl.BlockSpec((SUBLANES, TM), lambda i: (0, i)),
            pl.BlockSpec((n_experts, LANES), fixed),
        ],
        out_shape=[
            jax.ShapeDtypeStruct((t, d), F32),
            jax.ShapeDtypeStruct((t, d // 2), jnp.uint32),
            jax.ShapeDtypeStruct((t, LANES), F32),
            jax.ShapeDtypeStruct((SUBLANES, t), F32),
            jax.ShapeDtypeStruct((n_experts, LANES), F32),
        ],
        scratch_shapes=[pltpu.VMEM((n_experts, LANES), F32)],
        compiler_params=_cparams(("arbitrary",)),
    )(*args)


def _sc_mesh():
    return plsc.VectorSubcoreMesh(core_axis_name="c", subcore_axis_name="s",
                                  num_cores=SC_CORES, num_subcores=SC_SUBCORES)


def _sc_worker():
    return lax.axis_index("s") * SC_CORES + lax.axis_index("c")


def _sc_scatter_rows(src, d1, d2, n_rows):
    t, d = src.shape
    workers = SC_CORES * SC_SUBCORES
    per_w = t // workers
    n_ch = per_w // SC_CHUNK
    assert t % (workers * SC_CHUNK) == 0
    idx = jnp.stack([d1, d2]).reshape(2, workers, n_ch, SC_CHUNK)

    @functools.partial(
        pl.kernel, mesh=_sc_mesh(),
        out_type=jax.ShapeDtypeStruct((n_rows, d), src.dtype),
        scratch_types=[pltpu.VMEM((2, n_ch, SC_CHUNK), jnp.int32),
                       pltpu.VMEM((SC_CHUNK, d), src.dtype),
                       pltpu.VMEM((SC_CHUNK, d), src.dtype),
                       pltpu.SemaphoreType.DMA, pltpu.SemaphoreType.DMA],
    )
    def scatter(src_hbm, idx_hbm, out_hbm, idx_v, rows_a, rows_b, sem_a, sem_b):
        wid = _sc_worker()
        base = wid * per_w
        for s in range(2):
            pltpu.sync_copy(idx_hbm.at[s, wid], idx_v.at[s])

        def move(c, rows_v, sem):
            pltpu.sync_copy(src_hbm.at[pl.ds(base + c * SC_CHUNK, SC_CHUNK)], rows_v)
            return [pltpu.async_copy(rows_v, out_hbm.at[idx_v.at[s, c]], sem) for s in range(2)]

        @pl.loop(0, n_ch, step=2)
        def _(c):
            first = move(c, rows_a, sem_a)
            second = move(c + 1, rows_b, sem_b)
            for cp in first + second:
                cp.wait()

    return scatter(src, idx)


def _sc_gather_rows(table, idx):
    b = idx.shape[0]
    d = table.shape[1]
    workers = SC_CORES * SC_SUBCORES
    per_w = b // workers
    n_ch = per_w // SC_CHUNK
    assert b % (workers * SC_CHUNK * 2) == 0

    @functools.partial(
        pl.kernel, mesh=_sc_mesh(),
        out_type=jax.ShapeDtypeStruct((b, d), table.dtype),
        scratch_types=[pltpu.VMEM((per_w,), jnp.int32),
                       pltpu.VMEM((SC_CHUNK, d), table.dtype),
                       pltpu.VMEM((SC_CHUNK, d), table.dtype),
                       pltpu.SemaphoreType.DMA, pltpu.SemaphoreType.DMA],
    )
    def gather(table_hbm, idx_hbm, out_hbm, idx_v, rows_a, rows_b, sem_a, sem_b):
        base = _sc_worker() * per_w
        pltpu.sync_copy(idx_hbm.at[pl.ds(base, per_w)], idx_v)

        def fetch(c, rows_v, sem):
            return pltpu.async_copy(table_hbm.at[idx_v.at[pl.ds(c * SC_CHUNK, SC_CHUNK)]], rows_v, sem)

        def store(c, rows_v):
            pltpu.sync_copy(rows_v, out_hbm.at[pl.ds(base + c * SC_CHUNK, SC_CHUNK)])

        @pl.loop(0, n_ch, step=2)
        def _(c):
            first = fetch(c, rows_a, sem_a)
            second = fetch(c + 1, rows_b, sem_b)
            first.wait()
            store(c, rows_a)
            second.wait()
            store(c + 1, rows_b)

    return gather(table, idx)


def _expert_kernel(fb_ref, nblk_ref, nv_ref, nu_ref, xs_ref, wg_ref, wu_ref, wd_ref, ys_ref,
                   wgb_ref, wub_ref, wdb_ref, xbuf, ybuf, xsem, ysem):
    e = pl.program_id(0)
    rb = xbuf.shape[1]
    nb = ys_ref.shape[0] // rb
    n_used = nu_ref[0]

    def x_copy(g, slot):
        rows = pl.ds(pl.multiple_of(g * rb, rb), rb)
        return pltpu.make_async_copy(xs_ref.at[rows], xbuf.at[slot], xsem.at[slot])

    def y_copy(g, slot):
        rows = pl.ds(pl.multiple_of(g * rb, rb), rb)
        return pltpu.make_async_copy(ybuf.at[slot], ys_ref.at[rows], ysem.at[slot])

    @pl.when(e == 0)
    def _():
        x_copy(0, 0).start()

    @pl.when(nblk_ref[e] > 0)
    def _():
        wgb_ref[...] = wg_ref[0, 0].astype(BF16)
        wub_ref[...] = wu_ref[0, 0].astype(BF16)
        wdb_ref[...] = wd_ref[0, 0].astype(BF16)

    def block(b, carry):
        g = fb_ref[e] + b
        slot = lax.rem(g, 2)
        x_copy(g, slot).wait()

        @pl.when(g + 1 < n_used)
        def _():
            x_copy(g + 1, 1 - slot).start()

        @pl.when(g >= 2)
        def _():
            y_copy(g - 2, slot).wait()

        valid = lax.broadcasted_iota(jnp.int32, (rb, 1), 0) < nv_ref[g]
        x_hi, x_lo = _unpack_bf16_pairs(jnp.where(valid, xbuf[slot], jnp.uint32(0)))
        x_hi, x_lo = x_hi.astype(BF16), x_lo.astype(BF16)
        h = x_hi.shape[1]

        def in_dot(w_ref):
            return (jnp.dot(x_hi, w_ref[:h, :], preferred_element_type=F32)
                    + jnp.dot(x_lo, w_ref[h:, :], preferred_element_type=F32))

        gate = in_dot(wgb_ref)
        up = in_dot(wub_ref)
        hmid = (gate * jax.nn.sigmoid(gate) * up).astype(BF16)
        ybuf[slot] = _pack_bf16_pairs(jnp.dot(hmid, wdb_ref[...], preferred_element_type=F32))
        y_copy(g, slot).start()
        return carry

    lax.fori_loop(0, nblk_ref[e], block, 0)

    @pl.when(e == pl.num_programs(0) - 1)
    def _():
        @pl.when(n_used >= 2)
        def _():
            y_copy(n_used - 2, lax.rem(n_used, 2)).wait()

        y_copy(n_used - 1, lax.rem(n_used - 1, 2)).wait()
        ybuf[0] = jnp.zeros(ybuf.shape[1:], ybuf.dtype)

        def tail_start(g, carry):
            y_copy(g, 0).start()
            return carry

        def tail_wait(g, carry):
            y_copy(g, 0).wait()
            return carry

        lax.fori_loop(n_used, nb, tail_start, 0)
        lax.fori_loop(n_used, nb, tail_wait, 0)


def _experts(xs, first_block, n_blocks, n_valid, n_used, w_gate, w_up, w_down, layer):
    n_rows, dp = xs.shape
    d = 2 * dp
    n_experts, ff = w_gate.shape[1], w_gate.shape[3]
    wsel = lambda e, *_: (layer, e, 0, 0)
    grid_spec = pltpu.PrefetchScalarGridSpec(
        num_scalar_prefetch=4,
        grid=(n_experts,),
        in_specs=[
            pl.BlockSpec(memory_space=pl.ANY),
            pl.BlockSpec((1, 1, d, ff), wsel),
            pl.BlockSpec((1, 1, d, ff), wsel),
            pl.BlockSpec((1, 1, ff, d), wsel),
        ],
        out_specs=pl.BlockSpec(memory_space=pl.ANY),
        scratch_shapes=[pltpu.VMEM((d, ff), BF16), pltpu.VMEM((d, ff), BF16), pltpu.VMEM((ff, d), BF16),
                        pltpu.VMEM((2, ROW_BLOCK, dp), xs.dtype), pltpu.VMEM((2, ROW_BLOCK, dp), xs.dtype),
                        pltpu.SemaphoreType.DMA((2,)), pltpu.SemaphoreType.DMA((2,))],
    )
    return pl.pallas_call(
        _expert_kernel,
        grid_spec=grid_spec,
        out_shape=jax.ShapeDtypeStruct((n_rows, dp), xs.dtype),
        compiler_params=_cparams(("arbitrary",)),
    )(first_block, n_blocks, n_valid, n_used, xs, w_gate, w_up, w_down)


def _final_combine_kernel(x_ref, route_ref, y1_ref, y2_ref, nf_ref, o_ref):
    route = route_ref[...]
    y1 = jnp.concatenate(_unpack_bf16_pairs(y1_ref[...]), axis=1)
    y2 = jnp.concatenate(_unpack_bf16_pairs(y2_ref[...]), axis=1)
    y = x_ref[...] + (route[:, 2:3] * y1 + route[:, 3:4] * y2)
    o_ref[...] = y * lax.rsqrt(jnp.mean(y * y, axis=-1, keepdims=True) + RMS_EPS) * nf_ref[...]


def _final_combine(x1, route, yg, nf):
    t, d = x1.shape
    slot2 = t // TM
    row = lambda i: (i, 0)
    return pl.pallas_call(
        _final_combine_kernel,
        grid=(t // TM,),
        in_specs=[
            pl.BlockSpec((TM, d), row),
            pl.BlockSpec((TM, LANES), row),
            pl.BlockSpec((TM, d // 2), row),
            pl.BlockSpec((TM, d // 2), lambda i: (slot2 + i, 0)),
            pl.BlockSpec((1, d), lambda i: (0, 0)),
        ],
        out_specs=pl.BlockSpec((TM, d), row),
        out_shape=jax.ShapeDtypeStruct((t, d), F32),
        compiler_params=_cparams(("parallel",)),
    )(x1, route, yg, yg, nf.reshape(1, d).astype(F32))


def _moe(hn, route_t, cnt, w_gate, w_up, w_down, layer):
    t, d = hn.shape
    n_experts = w_gate.shape[1]
    ids = jnp.arange(n_experts, dtype=jnp.int32)
    e1, e2, rank1, rank2 = (route_t[f].astype(jnp.int32) for f in (0, 1, 4, 5))
    counts = cnt[:, 0].astype(jnp.int32)
    padded = ((counts + ROW_BLOCK - 1) // ROW_BLOCK) * ROW_BLOCK
    pad_end = jnp.cumsum(padded)
    pad_start = pad_end - padded

    def dest(e, rank):
        return jnp.sum(jnp.where(e[:, None] == ids[None, :], pad_start[None, :], 0), axis=1) + rank

    d1 = dest(e1, rank1)
    d2 = dest(e2, rank2)
    nb = (t * TOP_K_INNER) // ROW_BLOCK + n_experts
    n_used = (pad_end[-1:] // ROW_BLOCK).astype(jnp.int32)
    first_row = jnp.arange(nb, dtype=jnp.int32) * ROW_BLOCK
    block_e = jnp.sum((pad_end[None, :] <= first_row[:, None]).astype(jnp.int32), axis=1)
    block_e = jnp.minimum(block_e, n_experts - 1)
    row_end = jnp.sum(jnp.where(block_e[:, None] == ids[None, :], (pad_start + counts)[None, :], 0), axis=1)
    n_valid = jnp.clip(row_end - first_row, 0, ROW_BLOCK).astype(jnp.int32)
    xs = _sc_scatter_rows(hn, d1, d2, nb * ROW_BLOCK)
    ys = _experts(xs, (pad_start // ROW_BLOCK).astype(jnp.int32), (padded // ROW_BLOCK).astype(jnp.int32),
                  n_valid, n_used, w_gate, w_up, w_down, layer)
    return _sc_gather_rows(ys, jnp.concatenate([d1, d2]))


def _diff_lambda_init(layer_idx):
    return 0.8 - 0.6 * math.exp(-0.3 * layer_idx)


def _rope_tables(seq, dk):
    angle = 1.0 / (10000.0 ** jnp.linspace(0.0, 1.0, dk // 2, dtype=F32))
    angle = jnp.repeat(angle, 2)
    phase = jnp.arange(seq, dtype=F32)[:, None] * angle[None, :]
    sin, cos = jnp.sin(phase), jnp.cos(phase)
    even = (jnp.arange(dk) % 2 == 0)[None, :]
    return cos, jnp.where(even, -sin, 0.0), jnp.where(even, 0.0, sin)


def kernel(x, rel_bias_table, norm_mix, norm_ffn, norm_final, diff_w_in, diff_lambda_qk, diff_subln,
           diff_w_out, ret_w_in, ret_w_out, moe_w_group, moe_b_group, moe_w_expert, moe_b_expert,
           moe_w_gate, moe_w_up, moe_w_down):
    batch, seq, d = x.shape
    depth = norm_mix.shape[0]
    t = batch * seq
    heads = rel_bias_table.shape[1]
    dh = diff_lambda_qk.shape[-1]
    qk_dim = (ret_w_in.shape[2] - 2 * ret_w_out.shape[1]) // 2
    v_dim = ret_w_out.shape[1]
    dk = qk_dim // RET_HEADS
    xt = x.reshape(t, d)
    moe = None
    for i in range(depth):
        j = i // N_MIXERS
        if i % N_MIXERS == 0:
            col_scale = jnp.concatenate([jnp.full((d,), dh ** -0.5 * LOG2E, F32), jnp.ones((2 * d,), F32)])
            w_in = (diff_w_in[j] * col_scale[None, :]).astype(BF16)
            qkv = _norm_matmul(xt, norm_mix[i], w_in, moe=moe)
            if moe is not None:
                qkv, xt = qkv
            mix = _diff_attention(qkv, rel_bias_table, diff_lambda_qk[j], diff_subln[j],
                                  _diff_lambda_init(i), batch, seq, heads)
            gate = None
            w_out = diff_w_out[j].astype(BF16)
        else:
            n_in = ret_w_in.shape[2]
            col_scale = jnp.concatenate([jnp.ones((qk_dim,), F32), jnp.full((qk_dim,), dk ** -0.5, F32),
                                         jnp.ones((n_in - 2 * qk_dim,), F32)])
            w_in = (ret_w_in[j] * col_scale[None, :]).astype(BF16)
            qkvg = _norm_matmul(xt, norm_mix[i], w_in, rope=_rope_tables(seq, dk),
                                n_rope=2 * qk_dim // TN, seq=seq, moe=moe)
            if moe is not None:
                qkvg, xt = qkvg
            mix = _retention(qkvg, batch, seq, qk_dim, v_dim)
            gate = (qkvg, (2 * qk_dim + v_dim) // v_dim)
            w_out = ret_w_out[j].astype(BF16)
        x1, hn, route, route_t, cnt = _out_router(mix, gate, w_out, xt, norm_ffn[i], moe_w_group[i],
                                                  moe_b_group[i], moe_w_expert[i], moe_b_expert[i])
        xt = x1
        moe = (route, _moe(hn, route_t, cnt, moe_w_gate, moe_w_up, moe_w_down, i))
    return _final_combine(xt, moe[0], moe[1], norm_final).reshape(batch, seq, d)
```

```python
import functools
import math

import jax
import jax.numpy as jnp
from jax import lax
from jax.experimental import pallas as pl
from jax.experimental.pallas import tpu as pltpu
from jax.experimental.pallas import tpu_sc as plsc

F32 = jnp.float32
BF16 = jnp.bfloat16

N_MIXERS = 2
NUM_BUCKETS = 32
MAX_DISTANCE = 128
RET_HEADS = 4
TOP_K_INNER = 2
RMS_EPS = 1e-6
SUBLN_EPS = 1e-5

LANES = 128
SUBLANES = 8
BF16_SUBLANES = 16
LOG2E = 1.4426950408889634
VMEM_LIMIT = 56 * 1024 * 1024
SC_CORES = 2
SC_SUBCORES = 16

TM = 512
TN = 1024
TQ = 256
ATTN_HEADS = 4
RET_C = 256
RET_HP = 2
ROW_BLOCK = 512
SC_CHUNK = 64


def _cparams(sem):
    return pltpu.CompilerParams(dimension_semantics=sem, vmem_limit_bytes=VMEM_LIMIT)


def _pack_bf16_pairs(x):
    h = x.shape[1] // 2
    hi = lax.bitcast_convert_type(x[:, :h].astype(BF16).astype(F32), jnp.uint32)
    lo = lax.bitcast_convert_type(x[:, h:].astype(BF16).astype(F32), jnp.uint32)
    return hi | (lo >> 16)


def _unpack_bf16_pairs(w):
    hi = lax.bitcast_convert_type(w & jnp.uint32(0xFFFF0000), F32)
    lo = lax.bitcast_convert_type(w << 16, F32)
    return hi, lo


def _norm_matmul_kernel(x_ref, *rest, n_rope, moe):
    rest = list(rest)
    if moe:
        route_ref, y1_ref, y2_ref = rest[:3]
        x_out_ref = rest.pop()
        rest = rest[3:]
    g_ref, w_ref = rest[:2]
    if n_rope:
        cos_ref, sa_ref, sb_ref, o_ref = rest[2:]
    else:
        (o_ref,) = rest[2:]
    x = x_ref[...]
    if moe:
        route = route_ref[...]
        y1 = jnp.concatenate(_unpack_bf16_pairs(y1_ref[...]), axis=1)
        y2 = jnp.concatenate(_unpack_bf16_pairs(y2_ref[...]), axis=1)
        x = x + (route[:, 2:3] * y1 + route[:, 3:4] * y2)
        x_out_ref[...] = x
    xn = (x * lax.rsqrt(jnp.mean(x * x, axis=-1, keepdims=True) + RMS_EPS) * g_ref[...]).astype(BF16)
    for j in range(o_ref.shape[1] // TN):
        cols = slice(j * TN, (j + 1) * TN)
        acc = jnp.dot(xn, w_ref[:, cols], preferred_element_type=F32)
        if j < n_rope:
            reps = TN // cos_ref.shape[1]
            c = jnp.tile(cos_ref[...], (1, reps))
            sa = jnp.tile(sa_ref[...], (1, reps))
            sb = jnp.tile(sb_ref[...], (1, reps))
            nxt = pltpu.roll(acc, TN - 1, axis=1)
            prv = pltpu.roll(acc, 1, axis=1)
            acc = acc * c + nxt * sa + prv * sb
        o_ref[:, cols] = acc.astype(o_ref.dtype)


def _norm_matmul(x, g, w, rope=None, n_rope=0, seq=None, moe=None):
    t, d = x.shape
    n = w.shape[1]
    row = lambda i: (i, 0)
    in_specs = [pl.BlockSpec((TM, d), row)]
    args = [x]
    if moe is not None:
        route, yg = moe
        slot2 = t // TM
        in_specs += [pl.BlockSpec((TM, LANES), row), pl.BlockSpec((TM, d // 2), row),
                     pl.BlockSpec((TM, d // 2), lambda i: (slot2 + i, 0))]
        args += [route, yg, yg]
    in_specs += [
        pl.BlockSpec((1, d), lambda i: (0, 0)),
        pl.BlockSpec((d, n), lambda i: (0, 0), pipeline_mode=pl.Buffered(1)),
    ]
    args += [g.reshape(1, d), w]
    if rope is not None:
        nb = seq // TM
        dk = rope[0].shape[1]
        in_specs += [pl.BlockSpec((TM, dk), lambda i: (i % nb, 0))] * 3
        args += list(rope)
    out_specs = [pl.BlockSpec((TM, n), row)]
    out_shape = [jax.ShapeDtypeStruct((t, n), BF16)]
    if moe is not None:
        out_specs.append(pl.BlockSpec((TM, d), row))
        out_shape.append(jax.ShapeDtypeStruct((t, d), F32))
    out = pl.pallas_call(
        functools.partial(_norm_matmul_kernel, n_rope=n_rope, moe=moe is not None),
        grid=(t // TM,),
        in_specs=in_specs,
        out_specs=out_specs,
        out_shape=out_shape,
        compiler_params=_cparams(("parallel",)),
    )(*args)
    return out if moe is not None else out[0]


def _diff_attn_kernel(q_ref, k_ref, v_ref, bias_ref, lam_ref, g_ref, o_ref,
                      vt_ref, m_ref, acc_ref, s_ref, *, lambda_init):
    seq = q_ref.shape[0]
    hp, _, _, tq = bias_ref.shape
    dv = q_ref.shape[1] // hp
    dh = dv // 2
    nq = seq // tq

    def head(hh):
        return slice(hh * dv, (hh + 1) * dv)

    extra = vt_ref.shape[2] - dv
    ones_row = jnp.where(lax.broadcasted_iota(jnp.int32, (extra, tq), 0) == 0, 1.0, 0.0).astype(BF16)
    for hh in range(hp):
        for j in range(nq):
            vt_ref[hh, j, :dv, :] = v_ref[j * tq:(j + 1) * tq, head(hh)].astype(F32).T.astype(BF16)
            vt_ref[hh, j, dv:, :] = ones_row

    lq = lam_ref[...]
    lam = (jnp.exp(jnp.sum(lq[0:1] * lq[1:2], axis=-1, keepdims=True))
           - jnp.exp(jnp.sum(lq[2:3] * lq[3:4], axis=-1, keepdims=True)) + lambda_init)
    lane = lax.broadcasted_iota(jnp.int32, (tq, dv), 1)

    def query_maps(qi):
        qrows = pl.ds(pl.multiple_of(qi * tq, tq), tq)
        qs = []
        for hh in range(hp):
            q = q_ref[qrows, head(hh)]
            zero = jnp.zeros_like(q)
            qs.append(jnp.concatenate([jnp.where(lane < dh, q, zero), jnp.where(lane >= dh, q, zero)],
                                      axis=0))
        return qs

    def score_dots(j, qs):
        rows = pl.ds(pl.multiple_of(j * tq, tq), tq)
        for hh in range(hp):
            s_ref[hh] = lax.dot_general(k_ref[rows, head(hh)], qs[hh], (((1,), (1,)), ((), ())),
                                        preferred_element_type=F32)

    score_dots(0, query_maps(0))

    def q_body(qi, carry):
        qrows = pl.ds(pl.multiple_of(qi * tq, tq), tq)
        qs = query_maps(qi)
        qs_next = query_maps(jnp.minimum(qi + 1, nq - 1))

        m_ref[...] = jnp.full(m_ref.shape, -jnp.inf, F32)
        acc_ref[...] = jnp.zeros(acc_ref.shape, F32)

        def step(j, tile, prefetch):
            probs, alphas = [], []
            for hh in range(hp):
                s = s_ref[hh]
                if tile is not None:
                    bias = bias_ref[hh, tile]
                    s = s + jnp.concatenate([bias, bias], axis=1)
                m_old = m_ref[hh]
                m_new = jnp.maximum(m_old, jnp.max(s, axis=0, keepdims=True))
                alpha = jnp.exp2(m_old - m_new)
                m_ref[hh] = m_new
                probs.append(jnp.exp2(s - m_new).astype(BF16))
                alphas.append(alpha)
            if prefetch == "block":
                score_dots(j + 1, qs)
            else:
                score_dots(0, qs_next)
            for hh in range(hp):
                acc_ref[hh] = alphas[hh] * acc_ref[hh] + jnp.dot(vt_ref[hh, j], probs[hh],
                                                                 preferred_element_type=F32)

        def far_body(j, c):
            step(j, None, "block")
            return c

        lax.fori_loop(0, jnp.maximum(qi - 1, 0), far_body, 0)

        @pl.when(qi >= 1)
        def _():
            step(qi - 1, 1, "block")

        step(qi, 0, "tile")

        for hh in range(hp):
            acc = acc_ref[hh]
            on = acc[:dv] / acc[dv:dv + 1]
            ot = on[:, :tq] - lam * on[:, tq:]
            ot = ot * lax.rsqrt(jnp.mean(ot * ot, axis=0, keepdims=True) + SUBLN_EPS)
            o_ref[qrows, head(hh)] = (ot.T * g_ref[...] * (1.0 - lambda_init)).astype(o_ref.dtype)
        return carry

    lax.fori_loop(0, nq, q_body, 0)


def _t5_bucket(rel):
    max_exact = NUM_BUCKETS // 2
    n = jnp.maximum(rel, 0)
    is_small = n < max_exact
    nf = jnp.maximum(n, max_exact).astype(F32)
    large = max_exact + (jnp.log(nf / max_exact) / math.log(MAX_DISTANCE / max_exact)
                         * (NUM_BUCKETS - max_exact)).astype(jnp.int32)
    large = jnp.minimum(large, NUM_BUCKETS - 1)
    return jnp.where(is_small, n, large)


def _bias_tiles(rel_table):
    heads = rel_table.shape[1]
    span = 2 * TQ + 1
    bucket = _t5_bucket(jnp.arange(span))
    onehot = (bucket[:, None] == jnp.arange(NUM_BUCKETS)[None, :]).astype(F32)
    table = (rel_table.astype(F32) - rel_table[NUM_BUCKETS - 1].astype(F32)[None, :]) * LOG2E
    by_rel = jnp.dot(onehot, table, precision=lax.Precision.HIGHEST).T
    toep = jnp.tile(by_rel, (1, TQ))[:, :TQ * (span - 1)].reshape(heads, TQ, span - 1)
    kj = jnp.arange(TQ)[:, None]
    qi = jnp.arange(TQ)[None, :]
    diag = jnp.where((qi >= kj)[None], toep[:, :, :TQ], -jnp.inf)
    return jnp.stack([diag, toep[:, :, TQ:]], axis=1)


def _diff_attention(qkv, rel_table, lam_qk, subln_g, lambda_init, batch, seq, heads):
    t = qkv.shape[0]
    d = qkv.shape[1] // 3
    dv = d // heads
    hp = ATTN_HEADS
    assert TQ + 1 >= MAX_DISTANCE and seq % TQ == 0 and dv == LANES and heads % hp == 0
    nq = seq // TQ
    groups = heads // hp
    bias = _bias_tiles(rel_table)
    return pl.pallas_call(
        functools.partial(_diff_attn_kernel, lambda_init=lambda_init),
        grid=(batch, groups),
        in_specs=[
            pl.BlockSpec((seq, hp * dv), lambda b, g: (b, g)),
            pl.BlockSpec((seq, hp * dv), lambda b, g: (b, groups + g)),
            pl.BlockSpec((seq, hp * dv), lambda b, g: (b, 2 * groups + g)),
            pl.BlockSpec((hp, 2, TQ, TQ), lambda b, g: (g, 0, 0, 0)),
            pl.BlockSpec(lam_qk.shape, lambda b, g: (0, 0)),
            pl.BlockSpec((1, dv), lambda b, g: (0, 0)),
        ],
        out_specs=pl.BlockSpec((seq, hp * dv), lambda b, g: (b, g)),
        out_shape=jax.ShapeDtypeStruct((t, d), BF16),
        scratch_shapes=[
            pltpu.VMEM((hp, nq, dv + BF16_SUBLANES, TQ), BF16),
            pltpu.VMEM((hp, 1, 2 * TQ), F32),
            pltpu.VMEM((hp, dv + BF16_SUBLANES, 2 * TQ), F32),
            pltpu.VMEM((hp, TQ, 2 * TQ), F32),
        ],
        compiler_params=_cparams(("parallel", "parallel")),
    )(qkv, qkv, qkv, bias, lam_qk.astype(F32), subln_g.reshape(1, dv).astype(F32))


def _retention_kernel(q_ref, k_ref, v_ref, dec_ref, tab_ref, o_ref, state_ref):
    seq = q_ref.shape[0]
    hp, c, _ = dec_ref.shape
    dk = q_ref.shape[1] // hp
    dv = v_ref.shape[1] // hp
    heads = range(hp)
    state_ref[...] = jnp.zeros(state_ref.shape, F32)

    def body(n, carry):
        rows = pl.ds(pl.multiple_of(n * c, c), c)
        q = [q_ref[rows, hh * dk:(hh + 1) * dk] for hh in heads]
        k = [k_ref[rows, hh * dk:(hh + 1) * dk] for hh in heads]
        v = [v_ref[rows, hh * dv:(hh + 1) * dv] for hh in heads]
        tab = [tab_ref[hh] for hh in heads]
        st = [state_ref[hh] for hh in heads]
        s = [lax.dot_general(q[hh], k[hh], (((1,), (1,)), ((), ())), preferred_element_type=F32)
             for hh in heads]
        cross = [jnp.dot(q[hh], st[hh].astype(BF16), preferred_element_type=F32) for hh in heads]
        kt = [(k[hh].astype(F32) * tab[hh][:, 1:2]).T.astype(BF16) for hh in heads]
        inner = [jnp.dot((s[hh] * dec_ref[hh]).astype(BF16), v[hh], preferred_element_type=F32)
                 for hh in heads]
        for hh in heads:
            state_ref[hh] = st[hh] * tab[hh][0:1, 2:3] + jnp.dot(kt[hh], v[hh], preferred_element_type=F32)
        for hh in heads:
            o = inner[hh] + cross[hh] * tab[hh][:, 0:1]
            o = o * lax.rsqrt(jnp.mean(o * o, axis=-1, keepdims=True) + RMS_EPS)
            o_ref[rows, hh * dv:(hh + 1) * dv] = o.astype(o_ref.dtype)
        return carry

    lax.fori_loop(0, seq // c, body, 0)


def _retention_tables(heads):
    c = RET_C
    log_gamma = jnp.log(1.0 - 2.0 ** (-5.0 - jnp.arange(heads, dtype=F32)))
    pos = jnp.arange(c, dtype=F32)
    dist = pos[:, None] - pos[None, :]
    inner_decay = jnp.where(dist[None] >= 0,
                            jnp.exp(jnp.maximum(dist, 0.0)[None] * log_gamma[:, None, None]), 0.0)
    q_decay = jnp.exp((pos + 1.0)[None] * log_gamma[:, None])
    k_decay = jnp.exp((c - 1.0 - pos)[None] * log_gamma[:, None])
    chunk_decay = jnp.broadcast_to(jnp.exp(c * log_gamma)[:, None], (heads, c))
    tab = jnp.stack([q_decay, k_decay, chunk_decay], axis=-1)
    tab = jnp.pad(tab, ((0, 0), (0, 0), (0, LANES - 3)))
    return inner_decay, tab


def _retention(qkvg, batch, seq, qk_dim, v_dim):
    t = qkvg.shape[0]
    dk = qk_dim // RET_HEADS
    dvh = v_dim // RET_HEADS
    inner_decay, tab = _retention_tables(RET_HEADS)
    hp = RET_HP
    assert RET_HEADS % hp == 0
    groups = RET_HEADS // hp
    kb = qk_dim // (hp * dk)
    vb = 2 * qk_dim // (hp * dvh)
    return pl.pallas_call(
        _retention_kernel,
        grid=(batch, groups),
        in_specs=[
            pl.BlockSpec((seq, hp * dk), lambda b, g: (b, g)),
            pl.BlockSpec((seq, hp * dk), lambda b, g: (b, kb + g)),
            pl.BlockSpec((seq, hp * dvh), lambda b, g: (b, vb + g)),
            pl.BlockSpec((hp, RET_C, RET_C), lambda b, g: (g, 0, 0)),
            pl.BlockSpec((hp, RET_C, LANES), lambda b, g: (g, 0, 0)),
        ],
        out_specs=pl.BlockSpec((seq, hp * dvh), lambda b, g: (b, g)),
        out_shape=jax.ShapeDtypeStruct((t, v_dim), BF16),
        scratch_shapes=[pltpu.VMEM((hp, dk, dvh), F32)],
        compiler_params=_cparams(("parallel", "parallel")),
    )(qkvg, qkvg, qkvg, inner_decay, tab)


def _out_router_kernel(*refs, gated, n_groups, n_experts):
    if gated:
        a_ref, gate_ref, w_ref, x_ref, nf_ref, wr_ref, br_ref = refs[:7]
        x1_ref, hn_ref, route_ref, route_t_ref, cnt_ref, carry_ref = refs[7:]
        gate = gate_ref[...]
        a = gate * jax.nn.sigmoid(gate) * a_ref[...]
    else:
        a_ref, w_ref, x_ref, nf_ref, wr_ref, br_ref = refs[:6]
        x1_ref, hn_ref, route_ref, route_t_ref, cnt_ref, carry_ref = refs[6:]
        a = a_ref[...]
    epg = n_experts // n_groups

    @pl.when(pl.program_id(0) == 0)
    def _():
        carry_ref[...] = jnp.zeros(carry_ref.shape, F32)

    x1 = x_ref[...] + jnp.dot(a, w_ref[...], preferred_element_type=F32)
    x1_ref[...] = x1
    hn = x1 * lax.rsqrt(jnp.mean(x1 * x1, axis=-1, keepdims=True) + RMS_EPS) * nf_ref[...]
    hn_ref[...] = _pack_bf16_pairs(hn)

    hn_hi = hn.astype(BF16)
    hn_lo = (hn - hn_hi.astype(F32)).astype(BF16)
    logits = (jnp.dot(hn_hi, wr_ref[0], preferred_element_type=F32)
              + jnp.dot(hn_lo, wr_ref[0], preferred_element_type=F32)
              + jnp.dot(hn_hi, wr_ref[1], preferred_element_type=F32)) + br_ref[...]
    tm = logits.shape[0]

    n_rows = -(-(n_groups + n_experts) // SUBLANES) * SUBLANES
    lt = logits.T[:n_rows, :]
    row = lax.broadcasted_iota(jnp.int32, lt.shape, 0)
    big = jnp.int32(LANES)
    neg = jnp.float32(-jnp.inf)

    gl = jnp.where(row < n_groups, lt, neg)
    gmax = jnp.max(gl, axis=0, keepdims=True)
    g_p = 1.0 / jnp.sum(jnp.exp(gl - gmax), axis=0, keepdims=True)
    g_idx = jnp.min(jnp.where(gl == gmax, row, big), axis=0, keepdims=True)

    lo = n_groups + g_idx * epg
    el = jnp.where((row >= lo) & (row < lo + epg), lt, neg)
    v1 = jnp.max(el, axis=0, keepdims=True)
    i1 = jnp.min(jnp.where(el == v1, row, big), axis=0, keepdims=True)
    el2 = jnp.where(row == i1, neg, el)
    v2 = jnp.max(el2, axis=0, keepdims=True)
    i2 = jnp.min(jnp.where(el2 == v2, row, big), axis=0, keepdims=True)
    tt = jnp.exp(v2 - v1)
    w1 = g_p / (1.0 + tt)
    w2 = g_p * tt / (1.0 + tt)
    e1 = i1 - n_groups
    e2 = i2 - n_groups

    erow = lax.broadcasted_iota(jnp.int32, (n_experts, tm), 0)
    oh1 = erow == e1
    oh2 = erow == e2
    onehot = jnp.where(oh1 | oh2, 1.0, 0.0)
    before = lax.broadcasted_iota(jnp.int32, (tm, tm), 0) < lax.broadcasted_iota(jnp.int32, (tm, tm), 1)
    tri = jnp.where(before, 1.0, 0.0).astype(BF16)
    carry = carry_ref[...]
    base = jnp.dot(onehot.astype(BF16), tri, preferred_element_type=F32) + carry[:, 0:1]
    rank1 = jnp.sum(jnp.where(oh1, base, 0.0), axis=0, keepdims=True)
    rank2 = jnp.sum(jnp.where(oh2, base, 0.0), axis=0, keepdims=True)
    carry = carry + jnp.sum(onehot, axis=1, keepdims=True)
    carry_ref[...] = carry
    cnt_ref[...] = carry

    frow = lax.broadcasted_iota(jnp.int32, (LANES, tm), 0)
    out = jnp.zeros((LANES, tm), F32)
    for idx, val in enumerate((e1.astype(F32), e2.astype(F32), w1, w2, rank1, rank2)):
        out = jnp.where(frow == idx, val, out)
    route_t_ref[...] = out[:route_t_ref.shape[0], :]
    route_ref[...] = out.T


def _out_router(a, gate, w, x, nf, w_group, b_group, w_expert, b_expert):
    t, d = x.shape
    k = w.shape[0]
    n_groups = w_group.shape[1]
    n_experts = w_expert.shape[1]
    assert n_groups + n_experts <= LANES
    pad = LANES - n_groups - n_experts
    wr = jnp.pad(jnp.concatenate([w_group, w_expert], axis=1).astype(F32), ((0, 0), (0, pad)))
    wr_hi = wr.astype(BF16)
    wr = jnp.stack([wr_hi, (wr - wr_hi.astype(F32)).astype(BF16)])
    br = jnp.pad(jnp.concatenate([b_group, b_expert]).astype(F32), (0, pad)).reshape(1, LANES)
    gated = gate is not None
    row = lambda i: (i, 0)
    fixed = lambda i: (0, 0)
    in_specs = [pl.BlockSpec((TM, k), row)]
    args = [a]
    if gated:
        gate_arr, gate_col = gate
        in_specs.append(pl.BlockSpec((TM, k), lambda i: (i, gate_col)))
        args.append(gate_arr)
    in_specs += [
        pl.BlockSpec((k, d), fixed),
        pl.BlockSpec((TM, d), row),
        pl.BlockSpec((1, d), fixed),
        pl.BlockSpec((2, d, LANES), lambda i: (0, 0, 0)),
        pl.BlockSpec((1, LANES), fixed),
    ]
    args += [w, x, nf.reshape(1, d), wr, br]
    return pl.pallas_call(
        functools.partial(_out_router_kernel, gated=gated, n_groups=n_groups, n_experts=n_experts),
        grid=(t // TM,),
        in_specs=in_specs,
        out_specs=[
            pl.BlockSpec((TM, d), row),
            pl.BlockSpec((TM, d // 2), row),
            pl.BlockSpec((TM, LANES), row),
            pl.BlockSpec((SUBLANES, TM), lambda i: (0, i)),
            pl.BlockSpec((n_experts, LANES), fixed),
        ],
        out_shape=[
            jax.ShapeDtypeStruct((t, d), F32),
            jax.ShapeDtypeStruct((t, d // 2), jnp.uint32),
            jax.ShapeDtypeStruct((t, LANES), F32),
            jax.ShapeDtypeStruct((SUBLANES, t), F32),
            jax.ShapeDtypeStruct((n_experts, LANES), F32),
        ],
        scratch_shapes=[pltpu.VMEM((n_experts, LANES), F32)],
        compiler_params=_cparams(("arbitrary",)),
    )(*args)


def _sc_mesh():
    return plsc.VectorSubcoreMesh(core_axis_name="c", subcore_axis_name="s",
                                  num_cores=SC_CORES, num_subcores=SC_SUBCORES)


def _sc_worker():
    return lax.axis_index("s") * SC_CORES + lax.axis_index("c")


def _sc_scatter_rows(src, d1, d2, n_rows):
    t, d = src.shape
    workers = SC_CORES * SC_SUBCORES
    per_w = t // workers
    n_ch = per_w // SC_CHUNK
    assert t % (workers * SC_CHUNK) == 0
    idx = jnp.stack([d1, d2]).reshape(2, workers, n_ch, SC_CHUNK)

    @functools.partial(
        pl.kernel, mesh=_sc_mesh(),
        out_type=jax.ShapeDtypeStruct((n_rows, d), src.dtype),
        scratch_types=[pltpu.VMEM((2, n_ch, SC_CHUNK), jnp.int32),
                       pltpu.VMEM((SC_CHUNK, d), src.dtype),
                       pltpu.VMEM((SC_CHUNK, d), src.dtype),
                       pltpu.SemaphoreType.DMA, pltpu.SemaphoreType.DMA],
    )
    def scatter(src_hbm, idx_hbm, out_hbm, idx_v, rows_a, rows_b, sem_a, sem_b):
        wid = _sc_worker()
        base = wid * per_w
        for s in range(2):
            pltpu.sync_copy(idx_hbm.at[s, wid], idx_v.at[s])

        def move(c, rows_v, sem):
            pltpu.sync_copy(src_hbm.at[pl.ds(base + c * SC_CHUNK, SC_CHUNK)], rows_v)
            return [pltpu.async_copy(rows_v, out_hbm.at[idx_v.at[s, c]], sem) for s in range(2)]

        @pl.loop(0, n_ch, step=2)
        def _(c):
            first = move(c, rows_a, sem_a)
            second = move(c + 1, rows_b, sem_b)
            for cp in first + second:
                cp.wait()

    return scatter(src, idx)


def _sc_gather_rows(table, idx):
    b = idx.shape[0]
    d = table.shape[1]
    workers = SC_CORES * SC_SUBCORES
    per_w = b // workers
    n_ch = per_w // SC_CHUNK
    assert b % (workers * SC_CHUNK * 2) == 0

    @functools.partial(
        pl.kernel, mesh=_sc_mesh(),
        out_type=jax.ShapeDtypeStruct((b, d), table.dtype),
        scratch_types=[pltpu.VMEM((per_w,), jnp.int32),
                       pltpu.VMEM((SC_CHUNK, d), table.dtype),
                       pltpu.VMEM((SC_CHUNK, d), table.dtype),
                       pltpu.SemaphoreType.DMA, pltpu.SemaphoreType.DMA],
    )
    def gather(table_hbm, idx_hbm, out_hbm, idx_v, rows_a, rows_b, sem_a, sem_b):
        base = _sc_worker() * per_w
        pltpu.sync_copy(idx_hbm.at[pl.ds(base, per_w)], idx_v)

        def fetch(c, rows_v, sem):
            return pltpu.async_copy(table_hbm.at[idx_v.at[pl.ds(c * SC_CHUNK, SC_CHUNK)]], rows_v, sem)

        def store(c, rows_v):
            pltpu.sync_copy(rows_v, out_hbm.at[pl.ds(base + c * SC_CHUNK, SC_CHUNK)])

        @pl.loop(0, n_ch, step=2)
        def _(c):
            first = fetch(c, rows_a, sem_a)
            second = fetch(c + 1, rows_b, sem_b)
            first.wait()
            store(c, rows_a)
            second.wait()
            store(c + 1, rows_b)

    return gather(table, idx)


def _expert_kernel(fb_ref, nblk_ref, nv_ref, nu_ref, xs_ref, wg_ref, wu_ref, wd_ref, ys_ref,
                   wgb_ref, wub_ref, wdb_ref, xbuf, ybuf, xsem, ysem):
    e = pl.program_id(0)
    rb = xbuf.shape[1]
    nb = ys_ref.shape[0] // rb
    n_used = nu_ref[0]

    def x_copy(g, slot):
        rows = pl.ds(pl.multiple_of(g * rb, rb), rb)
        return pltpu.make_async_copy(xs_ref.at[rows], xbuf.at[slot], xsem.at[slot])

    def y_copy(g, slot):
        rows = pl.ds(pl.multiple_of(g * rb, rb), rb)
        return pltpu.make_async_copy(ybuf.at[slot], ys_ref.at[rows], ysem.at[slot])

    @pl.when(e == 0)
    def _():
        x_copy(0, 0).start()

    @pl.when(nblk_ref[e] > 0)
    def _():
        wgb_ref[...] = wg_ref[0, 0].astype(BF16)
        wub_ref[...] = wu_ref[0, 0].astype(BF16)
        wdb_ref[...] = wd_ref[0, 0].astype(BF16)

    def block(b, carry):
        g = fb_ref[e] + b
        slot = lax.rem(g, 2)
        x_copy(g, slot).wait()

        @pl.when(g + 1 < n_used)
        def _():
            x_copy(g + 1, 1 - slot).start()

        @pl.when(g >= 2)
        def _():
            y_copy(g - 2, slot).wait()

        valid = lax.broadcasted_iota(jnp.int32, (rb, 1), 0) < nv_ref[g]
        x_hi, x_lo = _unpack_bf16_pairs(jnp.where(valid, xbuf[slot], jnp.uint32(0)))
        x_hi, x_lo = x_hi.astype(BF16), x_lo.astype(BF16)
        h = x_hi.shape[1]

        def in_dot(w_ref):
            return (jnp.dot(x_hi, w_ref[:h, :], preferred_element_type=F32)
                    + jnp.dot(x_lo, w_ref[h:, :], preferred_element_type=F32))

        gate = in_dot(wgb_ref)
        up = in_dot(wub_ref)
        hmid = (gate * jax.nn.sigmoid(gate) * up).astype(BF16)
        ybuf[slot] = _pack_bf16_pairs(jnp.dot(hmid, wdb_ref[...], preferred_element_type=F32))
        y_copy(g, slot).start()
        return carry

    lax.fori_loop(0, nblk_ref[e], block, 0)

    @pl.when(e == pl.num_programs(0) - 1)
    def _():
        @pl.when(n_used >= 2)
        def _():
            y_copy(n_used - 2, lax.rem(n_used, 2)).wait()

        y_copy(n_used - 1, lax.rem(n_used - 1, 2)).wait()
        ybuf[0] = jnp.zeros(ybuf.shape[1:], ybuf.dtype)

        def tail_start(g, carry):
            y_copy(g, 0).start()
            return carry

        def tail_wait(g, carry):
            y_copy(g, 0).wait()
            return carry

        lax.fori_loop(n_used, nb, tail_start, 0)
        lax.fori_loop(n_used, nb, tail_wait, 0)


def _experts(xs, first_block, n_blocks, n_valid, n_used, w_gate, w_up, w_down, layer):
    n_rows, dp = xs.shape
    d = 2 * dp
    n_experts, ff = w_gate.shape[1], w_gate.shape[3]
    wsel = lambda e, *_: (layer, e, 0, 0)
    grid_spec = pltpu.PrefetchScalarGridSpec(
        num_scalar_prefetch=4,
        grid=(n_experts,),
        in_specs=[
            pl.BlockSpec(memory_space=pl.ANY),
            pl.BlockSpec((1, 1, d, ff), wsel),
            pl.BlockSpec((1, 1, d, ff), wsel),
            pl.BlockSpec((1, 1, ff, d), wsel),
        ],
        out_specs=pl.BlockSpec(memory_space=pl.ANY),
        scratch_shapes=[pltpu.VMEM((d, ff), BF16), pltpu.VMEM((d, ff), BF16), pltpu.VMEM((ff, d), BF16),
                        pltpu.VMEM((2, ROW_BLOCK, dp), xs.dtype), pltpu.VMEM((2, ROW_BLOCK, dp), xs.dtype),
                        pltpu.SemaphoreType.DMA((2,)), pltpu.SemaphoreType.DMA((2,))],
    )
    return pl.pallas_call(
        _expert_kernel,
        grid_spec=grid_spec,
        out_shape=jax.ShapeDtypeStruct((n_rows, dp), xs.dtype),
        compiler_params=_cparams(("arbitrary",)),
    )(first_block, n_blocks, n_valid, n_used, xs, w_gate, w_up, w_down)


def _final_combine_kernel(x_ref, route_ref, y1_ref, y2_ref, nf_ref, o_ref):
    route = route_ref[...]
    y1 = jnp.concatenate(_unpack_bf16_pairs(y1_ref[...]), axis=1)
    y2 = jnp.concatenate(_unpack_bf16_pairs(y2_ref[...]), axis=1)
    y = x_ref[...] + (route[:, 2:3] * y1 + route[:, 3:4] * y2)
    o_ref[...] = y * lax.rsqrt(jnp.mean(y * y, axis=-1, keepdims=True) + RMS_EPS) * nf_ref[...]


def _final_combine(x1, route, yg, nf):
    t, d = x1.shape
    slot2 = t // TM
    row = lambda i: (i, 0)
    return pl.pallas_call(
        _final_combine_kernel,
        grid=(t // TM,),
        in_specs=[
            pl.BlockSpec((TM, d), row),
            pl.BlockSpec((TM, LANES), row),
            pl.BlockSpec((TM, d // 2), row),
            pl.BlockSpec((TM, d // 2), lambda i: (slot2 + i, 0)),
            pl.BlockSpec((1, d), lambda i: (0, 0)),
        ],
        out_specs=pl.BlockSpec((TM, d), row),
        out_shape=jax.ShapeDtypeStruct((t, d), F32),
        compiler_params=_cparams(("parallel",)),
    )(x1, route, yg, yg, nf.reshape(1, d).astype(F32))


def _moe(hn, route_t, cnt, w_gate, w_up, w_down, layer):
    t, d = hn.shape
    n_experts = w_gate.shape[1]
    ids = jnp.arange(n_experts, dtype=jnp.int32)
    e1, e2, rank1, rank2 = (route_t[f].astype(jnp.int32) for f in (0, 1, 4, 5))
    counts = cnt[:, 0].astype(jnp.int32)
    padded = ((counts + ROW_BLOCK - 1) // ROW_BLOCK) * ROW_BLOCK
    pad_end = jnp.cumsum(padded)
    pad_start = pad_end - padded

    def dest(e, rank):
        return jnp.sum(jnp.where(e[:, None] == ids[None, :], pad_start[None, :], 0), axis=1) + rank

    d1 = dest(e1, rank1)
    d2 = dest(e2, rank2)
    nb = (t * TOP_K_INNER) // ROW_BLOCK + n_experts
    n_used = (pad_end[-1:] // ROW_BLOCK).astype(jnp.int32)
    first_row = jnp.arange(nb, dtype=jnp.int32) * ROW_BLOCK
    block_e = jnp.sum((pad_end[None, :] <= first_row[:, None]).astype(jnp.int32), axis=1)
    block_e = jnp.minimum(block_e, n_experts - 1)
    row_end = jnp.sum(jnp.where(block_e[:, None] == ids[None, :], (pad_start + counts)[None, :], 0), axis=1)
    n_valid = jnp.clip(row_end - first_row, 0, ROW_BLOCK).astype(jnp.int32)
    xs = _sc_scatter_rows(hn, d1, d2, nb * ROW_BLOCK)
    ys = _experts(xs, (pad_start // ROW_BLOCK).astype(jnp.int32), (padded // ROW_BLOCK).astype(jnp.int32),
                  n_valid, n_used, w_gate, w_up, w_down, layer)
    return _sc_gather_rows(ys, jnp.concatenate([d1, d2]))


def _diff_lambda_init(layer_idx):
    return 0.8 - 0.6 * math.exp(-0.3 * layer_idx)


def _rope_tables(seq, dk):
    angle = 1.0 / (10000.0 ** jnp.linspace(0.0, 1.0, dk // 2, dtype=F32))
    angle = jnp.repeat(angle, 2)
    phase = jnp.arange(seq, dtype=F32)[:, None] * angle[None, :]
    sin, cos = jnp.sin(phase), jnp.cos(phase)
    even = (jnp.arange(dk) % 2 == 0)[None, :]
    return cos, jnp.where(even, -sin, 0.0), jnp.where(even, 0.0, sin)


def kernel(x, rel_bias_table, norm_mix, norm_ffn, norm_final, diff_w_in, diff_lambda_qk, diff_subln,
           diff_w_out, ret_w_in, ret_w_out, moe_w_group, moe_b_group, moe_w_expert, moe_b_expert,
           moe_w_gate, moe_w_up, moe_w_down):
    batch, seq, d = x.shape
    depth = norm_mix.shape[0]
    t = batch * seq
    heads = rel_bias_table.shape[1]
    dh = diff_lambda_qk.shape[-1]
    qk_dim = (ret_w_in.shape[2] - 2 * ret_w_out.shape[1]) // 2
    v_dim = ret_w_out.shape[1]
    dk = qk_dim // RET_HEADS
    xt = x.reshape(t, d)
    moe = None
    for i in range(depth):
        j = i // N_MIXERS
        if i % N_MIXERS == 0:
            col_scale = jnp.concatenate([jnp.full((d,), dh ** -0.5 * LOG2E, F32), jnp.ones((2 * d,), F32)])
            w_in = (diff_w_in[j] * col_scale[None, :]).astype(BF16)
            qkv = _norm_matmul(xt, norm_mix[i], w_in, moe=moe)
            if moe is not None:
                qkv, xt = qkv
            mix = _diff_attention(qkv, rel_bias_table, diff_lambda_qk[j], diff_subln[j],
                                  _diff_lambda_init(i), batch, seq, heads)
            gate = None
            w_out = diff_w_out[j].astype(BF16)
        else:
            n_in = ret_w_in.shape[2]
            col_scale = jnp.concatenate([jnp.ones((qk_dim,), F32), jnp.full((qk_dim,), dk ** -0.5, F32),
                                         jnp.ones((n_in - 2 * qk_dim,), F32)])
            w_in = (ret_w_in[j] * col_scale[None, :]).astype(BF16)
            qkvg = _norm_matmul(xt, norm_mix[i], w_in, rope=_rope_tables(seq, dk),
                                n_rope=2 * qk_dim // TN, seq=seq, moe=moe)
            if moe is not None:
                qkvg, xt = qkvg
            mix = _retention(qkvg, batch, seq, qk_dim, v_dim)
            gate = (qkvg, (2 * qk_dim + v_dim) // v_dim)
            w_out = ret_w_out[j].astype(BF16)
        x1, hn, route, route_t, cnt = _out_router(mix, gate, w_out, xt, norm_ffn[i], moe_w_group[i],
                                                  moe_b_group[i], moe_w_expert[i], moe_b_expert[i])
        xt = x1
        moe = (route, _moe(hn, route_t, cnt, moe_w_gate, moe_w_up, moe_w_down, i))
    return _final_combine(xt, moe[0], moe[1], norm_final).reshape(batch, seq, d)
```

```python
import functools
import math

import jax
import jax.numpy as jnp
from jax import lax
from jax.experimental import pallas as pl
from jax.experimental.pallas import tpu as pltpu
from jax.experimental.pallas import tpu_sc as plsc

F32 = jnp.float32
BF16 = jnp.bfloat16

N_MIXERS = 2
NUM_BUCKETS = 32
MAX_DISTANCE = 128
RET_HEADS = 4
TOP_K_INNER = 2
RMS_EPS = 1e-6
SUBLN_EPS = 1e-5

LANES = 128
SUBLANES = 8
BF16_SUBLANES = 16
LOG2E = 1.4426950408889634
VMEM_LIMIT = 56 * 1024 * 1024
SC_CORES = 2
SC_SUBCORES = 16

TM = 512
TN = 1024
TQ = 256
ATTN_HEADS = 4
RET_C = 256
RET_HP = 2
ROUTER_SPLIT = 2
ROW_BLOCK = 512
SC_CHUNK = 64


def _cparams(sem):
    return pltpu.CompilerParams(dimension_semantics=sem, vmem_limit_bytes=VMEM_LIMIT)


def _pack_bf16_pairs(x):
    h = x.shape[1] // 2
    hi = lax.bitcast_convert_type(x[:, :h].astype(BF16).astype(F32), jnp.uint32)
    lo = lax.bitcast_convert_type(x[:, h:].astype(BF16).astype(F32), jnp.uint32)
    return hi | (lo >> 16)


def _unpack_bf16_pairs(w):
    hi = lax.bitcast_convert_type(w & jnp.uint32(0xFFFF0000), F32)
    lo = lax.bitcast_convert_type(w << 16, F32)
    return hi, lo


def _norm_matmul_kernel(x_ref, *rest, n_rope, moe):
    rest = list(rest)
    if moe:
        route_ref, y1_ref, y2_ref = rest[:3]
        x_out_ref = rest.pop()
        rest = rest[3:]
    g_ref, w_ref = rest[:2]
    if n_rope:
        cos_ref, sa_ref, sb_ref, o_ref = rest[2:]
    else:
        (o_ref,) = rest[2:]
    x = x_ref[...]
    if moe:
        route = route_ref[...]
        y1 = jnp.concatenate(_unpack_bf16_pairs(y1_ref[...]), axis=1)
        y2 = jnp.concatenate(_unpack_bf16_pairs(y2_ref[...]), axis=1)
        x = x + (route[:, 2:3] * y1 + route[:, 3:4] * y2)
        x_out_ref[...] = x
    xn = (x * lax.rsqrt(jnp.mean(x * x, axis=-1, keepdims=True) + RMS_EPS) * g_ref[...]).astype(BF16)
    for j in range(o_ref.shape[1] // TN):
        cols = slice(j * TN, (j + 1) * TN)
        acc = jnp.dot(xn, w_ref[:, cols], preferred_element_type=F32)
        if j < n_rope:
            reps = TN // cos_ref.shape[1]
            c = jnp.tile(cos_ref[...], (1, reps))
            sa = jnp.tile(sa_ref[...], (1, reps))
            sb = jnp.tile(sb_ref[...], (1, reps))
            nxt = pltpu.roll(acc, TN - 1, axis=1)
            prv = pltpu.roll(acc, 1, axis=1)
            acc = acc * c + nxt * sa + prv * sb
        o_ref[:, cols] = acc.astype(o_ref.dtype)


def _norm_matmul(x, g, w, rope=None, n_rope=0, seq=None, moe=None):
    t, d = x.shape
    n = w.shape[1]
    row = lambda i: (i, 0)
    in_specs = [pl.BlockSpec((TM, d), row)]
    args = [x]
    if moe is not None:
        route, yg = moe
        slot2 = t // TM
        in_specs += [pl.BlockSpec((TM, LANES), row), pl.BlockSpec((TM, d // 2), row),
                     pl.BlockSpec((TM, d // 2), lambda i: (slot2 + i, 0))]
        args += [route, yg, yg]
    in_specs += [
        pl.BlockSpec((1, d), lambda i: (0, 0)),
        pl.BlockSpec((d, n), lambda i: (0, 0), pipeline_mode=pl.Buffered(1)),
    ]
    args += [g.reshape(1, d), w]
    if rope is not None:
        nb = seq // TM
        dk = rope[0].shape[1]
        in_specs += [pl.BlockSpec((TM, dk), lambda i: (i % nb, 0))] * 3
        args += list(rope)
    out_specs = [pl.BlockSpec((TM, n), row)]
    out_shape = [jax.ShapeDtypeStruct((t, n), BF16)]
    if moe is not None:
        out_specs.append(pl.BlockSpec((TM, d), row))
        out_shape.append(jax.ShapeDtypeStruct((t, d), F32))
    out = pl.pallas_call(
        functools.partial(_norm_matmul_kernel, n_rope=n_rope, moe=moe is not None),
        grid=(t // TM,),
        in_specs=in_specs,
        out_specs=out_specs,
        out_shape=out_shape,
        compiler_params=_cparams(("parallel",)),
    )(*args)
    return out if moe is not None else out[0]


def _diff_attn_kernel(q_ref, k_ref, v_ref, bias_ref, lam_ref, g_ref, o_ref,
                      vt_ref, m_ref, acc_ref, s_ref, *, lambda_init):
    seq = q_ref.shape[0]
    hp, _, _, tq = bias_ref.shape
    dv = q_ref.shape[1] // hp
    dh = dv // 2
    nq = seq // tq

    def head(hh):
        return slice(hh * dv, (hh + 1) * dv)

    extra = vt_ref.shape[2] - dv
    ones_row = jnp.where(lax.broadcasted_iota(jnp.int32, (extra, tq), 0) == 0, 1.0, 0.0).astype(BF16)
    for hh in range(hp):
        for j in range(nq):
            vt_ref[hh, j, :dv, :] = v_ref[j * tq:(j + 1) * tq, head(hh)].astype(F32).T.astype(BF16)
            vt_ref[hh, j, dv:, :] = ones_row

    lq = lam_ref[...]
    lam = (jnp.exp(jnp.sum(lq[0:1] * lq[1:2], axis=-1, keepdims=True))
           - jnp.exp(jnp.sum(lq[2:3] * lq[3:4], axis=-1, keepdims=True)) + lambda_init)
    lane = lax.broadcasted_iota(jnp.int32, (tq, dv), 1)

    def query_maps(qi):
        qrows = pl.ds(pl.multiple_of(qi * tq, tq), tq)
        qs = []
        for hh in range(hp):
            q = q_ref[qrows, head(hh)]
            zero = jnp.zeros_like(q)
            qs.append(jnp.concatenate([jnp.where(lane < dh, q, zero), jnp.where(lane >= dh, q, zero)],
                                      axis=0))
        return qs

    def score_dots(j, qs):
        rows = pl.ds(pl.multiple_of(j * tq, tq), tq)
        for hh in range(hp):
            s_ref[hh] = lax.dot_general(k_ref[rows, head(hh)], qs[hh], (((1,), (1,)), ((), ())),
                                        preferred_element_type=F32)

    score_dots(0, query_maps(0))

    def q_body(qi, carry):
        qrows = pl.ds(pl.multiple_of(qi * tq, tq), tq)
        qs = query_maps(qi)
        qs_next = query_maps(jnp.minimum(qi + 1, nq - 1))

        m_ref[...] = jnp.full(m_ref.shape, -jnp.inf, F32)
        acc_ref[...] = jnp.zeros(acc_ref.shape, F32)

        def step(j, tile, prefetch):
            probs, alphas = [], []
            for hh in range(hp):
                s = s_ref[hh]
                if tile is not None:
                    bias = bias_ref[hh, tile]
                    s = s + jnp.concatenate([bias, bias], axis=1)
                m_old = m_ref[hh]
                m_new = jnp.maximum(m_old, jnp.max(s, axis=0, keepdims=True))
                alpha = jnp.exp2(m_old - m_new)
                m_ref[hh] = m_new
                probs.append(jnp.exp2(s - m_new).astype(BF16))
                alphas.append(alpha)
            if prefetch == "block":
                score_dots(j + 1, qs)
            else:
                score_dots(0, qs_next)
            for hh in range(hp):
                acc_ref[hh] = alphas[hh] * acc_ref[hh] + jnp.dot(vt_ref[hh, j], probs[hh],
                                                                 preferred_element_type=F32)

        def far_body(j, c):
            step(j, None, "block")
            return c

        lax.fori_loop(0, jnp.maximum(qi - 1, 0), far_body, 0)

        @pl.when(qi >= 1)
        def _():
            step(qi - 1, 1, "block")

        step(qi, 0, "tile")

        for hh in range(hp):
            acc = acc_ref[hh]
            on = acc[:dv] / acc[dv:dv + 1]
            ot = on[:, :tq] - lam * on[:, tq:]
            ot = ot * lax.rsqrt(jnp.mean(ot * ot, axis=0, keepdims=True) + SUBLN_EPS)
            o_ref[qrows, head(hh)] = (ot.T * g_ref[...] * (1.0 - lambda_init)).astype(o_ref.dtype)
        return carry

    lax.fori_loop(0, nq, q_body, 0)


def _t5_bucket(rel):
    max_exact = NUM_BUCKETS // 2
    n = jnp.maximum(rel, 0)
    is_small = n < max_exact
    nf = jnp.maximum(n, max_exact).astype(F32)
    large = max_exact + (jnp.log(nf / max_exact) / math.log(MAX_DISTANCE / max_exact)
                         * (NUM_BUCKETS - max_exact)).astype(jnp.int32)
    large = jnp.minimum(large, NUM_BUCKETS - 1)
    return jnp.where(is_small, n, large)


def _bias_tiles(rel_table):
    heads = rel_table.shape[1]
    span = 2 * TQ + 1
    bucket = _t5_bucket(jnp.arange(span))
    onehot = (bucket[:, None] == jnp.arange(NUM_BUCKETS)[None, :]).astype(F32)
    table = (rel_table.astype(F32) - rel_table[NUM_BUCKETS - 1].astype(F32)[None, :]) * LOG2E
    by_rel = jnp.dot(onehot, table, precision=lax.Precision.HIGHEST).T
    toep = jnp.tile(by_rel, (1, TQ))[:, :TQ * (span - 1)].reshape(heads, TQ, span - 1)
    kj = jnp.arange(TQ)[:, None]
    qi = jnp.arange(TQ)[None, :]
    diag = jnp.where((qi >= kj)[None], toep[:, :, :TQ], -jnp.inf)
    return jnp.stack([diag, toep[:, :, TQ:]], axis=1)


def _diff_attention(qkv, rel_table, lam_qk, subln_g, lambda_init, batch, seq, heads):
    t = qkv.shape[0]
    d = qkv.shape[1] // 3
    dv = d // heads
    hp = ATTN_HEADS
    assert TQ + 1 >= MAX_DISTANCE and seq % TQ == 0 and dv == LANES and heads % hp == 0
    nq = seq // TQ
    groups = heads // hp
    bias = _bias_tiles(rel_table)
    return pl.pallas_call(
        functools.partial(_diff_attn_kernel, lambda_init=lambda_init),
        grid=(batch, groups),
        in_specs=[
            pl.BlockSpec((seq, hp * dv), lambda b, g: (b, g)),
            pl.BlockSpec((seq, hp * dv), lambda b, g: (b, groups + g)),
            pl.BlockSpec((seq, hp * dv), lambda b, g: (b, 2 * groups + g)),
            pl.BlockSpec((hp, 2, TQ, TQ), lambda b, g: (g, 0, 0, 0)),
            pl.BlockSpec(lam_qk.shape, lambda b, g: (0, 0)),
            pl.BlockSpec((1, dv), lambda b, g: (0, 0)),
        ],
        out_specs=pl.BlockSpec((seq, hp * dv), lambda b, g: (b, g)),
        out_shape=jax.ShapeDtypeStruct((t, d), BF16),
        scratch_shapes=[
            pltpu.VMEM((hp, nq, dv + BF16_SUBLANES, TQ), BF16),
            pltpu.VMEM((hp, 1, 2 * TQ), F32),
            pltpu.VMEM((hp, dv + BF16_SUBLANES, 2 * TQ), F32),
            pltpu.VMEM((hp, TQ, 2 * TQ), F32),
        ],
        compiler_params=_cparams(("parallel", "parallel")),
    )(qkv, qkv, qkv, bias, lam_qk.astype(F32), subln_g.reshape(1, dv).astype(F32))


def _retention_kernel(q_ref, k_ref, v_ref, dec_ref, tab_ref, o_ref, state_ref):
    seq = q_ref.shape[0]
    hp, c, _ = dec_ref.shape
    dk = q_ref.shape[1] // hp
    dv = v_ref.shape[1] // hp
    heads = range(hp)
    state_ref[...] = jnp.zeros(state_ref.shape, F32)

    def body(n, carry):
        rows = pl.ds(pl.multiple_of(n * c, c), c)
        q = [q_ref[rows, hh * dk:(hh + 1) * dk] for hh in heads]
        k = [k_ref[rows, hh * dk:(hh + 1) * dk] for hh in heads]
        v = [v_ref[rows, hh * dv:(hh + 1) * dv] for hh in heads]
        tab = [tab_ref[hh] for hh in heads]
        st = [state_ref[hh] for hh in heads]
        s = [lax.dot_general(q[hh], k[hh], (((1,), (1,)), ((), ())), preferred_element_type=F32)
             for hh in heads]
        cross = [jnp.dot(q[hh], st[hh].astype(BF16), preferred_element_type=F32) for hh in heads]
        kt = [(k[hh].astype(F32) * tab[hh][:, 1:2]).T.astype(BF16) for hh in heads]
        inner = [jnp.dot((s[hh] * dec_ref[hh]).astype(BF16), v[hh], preferred_element_type=F32)
                 for hh in heads]
        for hh in heads:
            state_ref[hh] = st[hh] * tab[hh][0:1, 2:3] + jnp.dot(kt[hh], v[hh], preferred_element_type=F32)
        for hh in heads:
            o = inner[hh] + cross[hh] * tab[hh][:, 0:1]
            o = o * lax.rsqrt(jnp.mean(o * o, axis=-1, keepdims=True) + RMS_EPS)
            o_ref[rows, hh * dv:(hh + 1) * dv] = o.astype(o_ref.dtype)
        return carry

    lax.fori_loop(0, seq // c, body, 0)


def _retention_tables(heads):
    c = RET_C
    log_gamma = jnp.log(1.0 - 2.0 ** (-5.0 - jnp.arange(heads, dtype=F32)))
    pos = jnp.arange(c, dtype=F32)
    dist = pos[:, None] - pos[None, :]
    inner_decay = jnp.where(dist[None] >= 0,
                            jnp.exp(jnp.maximum(dist, 0.0)[None] * log_gamma[:, None, None]), 0.0)
    q_decay = jnp.exp((pos + 1.0)[None] * log_gamma[:, None])
    k_decay = jnp.exp((c - 1.0 - pos)[None] * log_gamma[:, None])
    chunk_decay = jnp.broadcast_to(jnp.exp(c * log_gamma)[:, None], (heads, c))
    tab = jnp.stack([q_decay, k_decay, chunk_decay], axis=-1)
    tab = jnp.pad(tab, ((0, 0), (0, 0), (0, LANES - 3)))
    return inner_decay, tab


def _retention(qkvg, batch, seq, qk_dim, v_dim):
    t = qkvg.shape[0]
    dk = qk_dim // RET_HEADS
    dvh = v_dim // RET_HEADS
    inner_decay, tab = _retention_tables(RET_HEADS)
    hp = RET_HP
    assert RET_HEADS % hp == 0
    groups = RET_HEADS // hp
    kb = qk_dim // (hp * dk)
    vb = 2 * qk_dim // (hp * dvh)
    return pl.pallas_call(
        _retention_kernel,
        grid=(batch, groups),
        in_specs=[
            pl.BlockSpec((seq, hp * dk), lambda b, g: (b, g)),
            pl.BlockSpec((seq, hp * dk), lambda b, g: (b, kb + g)),
            pl.BlockSpec((seq, hp * dvh), lambda b, g: (b, vb + g)),
            pl.BlockSpec((hp, RET_C, RET_C), lambda b, g: (g, 0, 0)),
            pl.BlockSpec((hp, RET_C, LANES), lambda b, g: (g, 0, 0)),
        ],
        out_specs=pl.BlockSpec((seq, hp * dvh), lambda b, g: (b, g)),
        out_shape=jax.ShapeDtypeStruct((t, v_dim), BF16),
        scratch_shapes=[pltpu.VMEM((hp, dk, dvh), F32)],
        compiler_params=_cparams(("parallel", "parallel")),
    )(qkvg, qkvg, qkvg, inner_decay, tab)


def _out_router_kernel(*refs, gated, n_groups, n_experts):
    if gated:
        a_ref, gate_ref, w_ref, x_ref, nf_ref, wr_ref, br_ref = refs[:7]
        x1_ref, hn_ref, route_ref, route_t_ref, cnt_ref, carry_ref = refs[7:]
    else:
        a_ref, w_ref, x_ref, nf_ref, wr_ref, br_ref = refs[:6]
        x1_ref, hn_ref, route_ref, route_t_ref, cnt_ref, carry_ref = refs[6:]
    epg = n_experts // n_groups
    tm = x_ref.shape[0] // ROUTER_SPLIT

    @pl.when(pl.program_id(0) == 0)
    def _():
        carry_ref[...] = jnp.zeros(carry_ref.shape, F32)

    x1_parts = []
    for part in range(ROUTER_SPLIT):
        rows = slice(part * tm, (part + 1) * tm)
        a = a_ref[rows, :]
        if gated:
            gate = gate_ref[rows, :]
            a = gate * jax.nn.sigmoid(gate) * a
        x1_parts.append(x_ref[rows, :] + jnp.dot(a, w_ref[...], preferred_element_type=F32))
    for part in range(ROUTER_SPLIT):
        rows = slice(part * tm, (part + 1) * tm)
        _route_rows(x1_parts[part], rows, nf_ref, wr_ref, br_ref, x1_ref, hn_ref, route_ref, route_t_ref,
                    cnt_ref, carry_ref, n_groups=n_groups, n_experts=n_experts, epg=epg)


def _route_rows(x1, rows, nf_ref, wr_ref, br_ref, x1_ref, hn_ref, route_ref, route_t_ref, cnt_ref,
                carry_ref, *, n_groups, n_experts, epg):
    tm = x1.shape[0]
    x1_ref[rows, :] = x1
    hn = x1 * lax.rsqrt(jnp.mean(x1 * x1, axis=-1, keepdims=True) + RMS_EPS) * nf_ref[...]
    hn_ref[rows, :] = _pack_bf16_pairs(hn)

    hn_hi = hn.astype(BF16)
    hn_lo = (hn - hn_hi.astype(F32)).astype(BF16)
    logits = (jnp.dot(hn_hi, wr_ref[0], preferred_element_type=F32)
              + jnp.dot(hn_lo, wr_ref[0], preferred_element_type=F32)
              + jnp.dot(hn_hi, wr_ref[1], preferred_element_type=F32)) + br_ref[...]

    n_rows = -(-(n_groups + n_experts) // SUBLANES) * SUBLANES
    lt = logits.T[:n_rows, :]
    row = lax.broadcasted_iota(jnp.int32, lt.shape, 0)
    big = jnp.int32(LANES)
    neg = jnp.float32(-jnp.inf)

    gl = jnp.where(row < n_groups, lt, neg)
    gmax = jnp.max(gl, axis=0, keepdims=True)
    g_p = 1.0 / jnp.sum(jnp.exp(gl - gmax), axis=0, keepdims=True)
    g_idx = jnp.min(jnp.where(gl == gmax, row, big), axis=0, keepdims=True)

    lo = n_groups + g_idx * epg
    el = jnp.where((row >= lo) & (row < lo + epg), lt, neg)
    v1 = jnp.max(el, axis=0, keepdims=True)
    i1 = jnp.min(jnp.where(el == v1, row, big), axis=0, keepdims=True)
    el2 = jnp.where(row == i1, neg, el)
    v2 = jnp.max(el2, axis=0, keepdims=True)
    i2 = jnp.min(jnp.where(el2 == v2, row, big), axis=0, keepdims=True)
    tt = jnp.exp(v2 - v1)
    w1 = g_p / (1.0 + tt)
    w2 = g_p * tt / (1.0 + tt)
    e1 = i1 - n_groups
    e2 = i2 - n_groups

    erow = lax.broadcasted_iota(jnp.int32, (n_experts, tm), 0)
    oh1 = erow == e1
    oh2 = erow == e2
    onehot = jnp.where(oh1 | oh2, 1.0, 0.0)
    before = lax.broadcasted_iota(jnp.int32, (tm, tm), 0) < lax.broadcasted_iota(jnp.int32, (tm, tm), 1)
    tri = jnp.where(before, 1.0, 0.0).astype(BF16)
    carry = carry_ref[...]
    base = jnp.dot(onehot.astype(BF16), tri, preferred_element_type=F32) + carry[:, 0:1]
    rank1 = jnp.sum(jnp.where(oh1, base, 0.0), axis=0, keepdims=True)
    rank2 = jnp.sum(jnp.where(oh2, base, 0.0), axis=0, keepdims=True)
    carry = carry + jnp.sum(onehot, axis=1, keepdims=True)
    carry_ref[...] = carry
    cnt_ref[...] = carry

    frow = lax.broadcasted_iota(jnp.int32, (LANES, tm), 0)
    out = jnp.zeros((LANES, tm), F32)
    for idx, val in enumerate((e1.astype(F32), e2.astype(F32), w1, w2, rank1, rank2)):
        out = jnp.where(frow == idx, val, out)
    route_t_ref[:, rows] = out[:route_t_ref.shape[0], :]
    route_ref[rows, :] = out.T


def _out_router(a, gate, w, x, nf, w_group, b_group, w_expert, b_expert):
    t, d = x.shape
    k = w.shape[0]
    n_groups = w_group.shape[1]
    n_experts = w_expert.shape[1]
    assert n_groups + n_experts <= LANES
    pad = LANES - n_groups - n_experts
    wr = jnp.pad(jnp.concatenate([w_group, w_expert], axis=1).astype(F32), ((0, 0), (0, pad)))
    wr_hi = wr.astype(BF16)
    wr = jnp.stack([wr_hi, (wr - wr_hi.astype(F32)).astype(BF16)])
    br = jnp.pad(jnp.concatenate([b_group, b_expert]).astype(F32), (0, pad)).reshape(1, LANES)
    gated = gate is not None
    row = lambda i: (i, 0)
    fixed = lambda i: (0, 0)
    in_specs = [pl.BlockSpec((TM, k), row)]
    args = [a]
    if gated:
        gate_arr, gate_col = gate
        in_specs.append(pl.BlockSpec((TM, k), lambda i: (i, gate_col)))
        args.append(gate_arr)
    in_specs += [
        pl.BlockSpec((k, d), fixed),
        pl.BlockSpec((TM, d), row),
        pl.BlockSpec((1, d), fixed),
        pl.BlockSpec((2, d, LANES), lambda i: (0, 0, 0)),
        pl.BlockSpec((1, LANES), fixed),
    ]
    args += [w, x, nf.reshape(1, d), wr, br]
    return pl.pallas_call(
        functools.partial(_out_router_kernel, gated=gated, n_groups=n_groups, n_experts=n_experts),
        grid=(t // TM,),
        in_specs=in_specs,
        out_specs=[
            pl.BlockSpec((TM, d), row),
            pl.BlockSpec((TM, d // 2), row),
            pl.BlockSpec((TM, LANES), row),
            pl.BlockSpec((SUBLANES, TM), lambda i: (0, i)),
            pl.BlockSpec((n_experts, LANES), fixed),
        ],
        out_shape=[
            jax.ShapeDtypeStruct((t, d), F32),
            jax.ShapeDtypeStruct((t, d // 2), jnp.uint32),
            jax.ShapeDtypeStruct((t, LANES), F32),
            jax.ShapeDtypeStruct((SUBLANES, t), F32),
            jax.ShapeDtypeStruct((n_experts, LANES), F32),
        ],
        scratch_shapes=[pltpu.VMEM((n_experts, LANES), F32)],
        compiler_params=_cparams(("arbitrary",)),
    )(*args)


def _sc_mesh():
    return plsc.VectorSubcoreMesh(core_axis_name="c", subcore_axis_name="s",
                                  num_cores=SC_CORES, num_subcores=SC_SUBCORES)


def _sc_worker():
    return lax.axis_index("s") * SC_CORES + lax.axis_index("c")


def _sc_scatter_rows(src, d1, d2, n_rows):
    t, d = src.shape
    workers = SC_CORES * SC_SUBCORES
    per_w = t // workers
    n_ch = per_w // SC_CHUNK
    assert t % (workers * SC_CHUNK) == 0
    idx = jnp.stack([d1, d2]).reshape(2, workers, n_ch, SC_CHUNK)

    @functools.partial(
        pl.kernel, mesh=_sc_mesh(),
        out_type=jax.ShapeDtypeStruct((n_rows, d), src.dtype),
        scratch_types=[pltpu.VMEM((2, n_ch, SC_CHUNK), jnp.int32),
                       pltpu.VMEM((SC_CHUNK, d), src.dtype),
                       pltpu.VMEM((SC_CHUNK, d), src.dtype),
                       pltpu.SemaphoreType.DMA, pltpu.SemaphoreType.DMA],
    )
    def scatter(src_hbm, idx_hbm, out_hbm, idx_v, rows_a, rows_b, sem_a, sem_b):
        wid = _sc_worker()
        base = wid * per_w
        for s in range(2):
            pltpu.sync_copy(idx_hbm.at[s, wid], idx_v.at[s])

        def move(c, rows_v, sem):
            pltpu.sync_copy(src_hbm.at[pl.ds(base + c * SC_CHUNK, SC_CHUNK)], rows_v)
            return [pltpu.async_copy(rows_v, out_hbm.at[idx_v.at[s, c]], sem) for s in range(2)]

        @pl.loop(0, n_ch, step=2)
        def _(c):
            first = move(c, rows_a, sem_a)
            second = move(c + 1, rows_b, sem_b)
            for cp in first + second:
                cp.wait()

    return scatter(src, idx)


def _sc_gather_rows(table, idx):
    b = idx.shape[0]
    d = table.shape[1]
    workers = SC_CORES * SC_SUBCORES
    per_w = b // workers
    n_ch = per_w // SC_CHUNK
    assert b % (workers * SC_CHUNK * 2) == 0

    @functools.partial(
        pl.kernel, mesh=_sc_mesh(),
        out_type=jax.ShapeDtypeStruct((b, d), table.dtype),
        scratch_types=[pltpu.VMEM((per_w,), jnp.int32),
                       pltpu.VMEM((SC_CHUNK, d), table.dtype),
                       pltpu.VMEM((SC_CHUNK, d), table.dtype),
                       pltpu.SemaphoreType.DMA, pltpu.SemaphoreType.DMA],
    )
    def gather(table_hbm, idx_hbm, out_hbm, idx_v, rows_a, rows_b, sem_a, sem_b):
        base = _sc_worker() * per_w
        pltpu.sync_copy(idx_hbm.at[pl.ds(base, per_w)], idx_v)

        def fetch(c, rows_v, sem):
            return pltpu.async_copy(table_hbm.at[idx_v.at[pl.ds(c * SC_CHUNK, SC_CHUNK)]], rows_v, sem)

        def store(c, rows_v):
            pltpu.sync_copy(rows_v, out_hbm.at[pl.ds(base + c * SC_CHUNK, SC_CHUNK)])

        @pl.loop(0, n_ch, step=2)
        def _(c):
            first = fetch(c, rows_a, sem_a)
            second = fetch(c + 1, rows_b, sem_b)
            first.wait()
            store(c, rows_a)
            second.wait()
            store(c + 1, rows_b)

    return gather(table, idx)


def _expert_kernel(fb_ref, nblk_ref, nv_ref, nu_ref, xs_ref, wg_ref, wu_ref, wd_ref, ys_ref,
                   wgb_ref, wub_ref, wdb_ref, xbuf, ybuf, xsem, ysem):
    e = pl.program_id(0)
    rb = xbuf.shape[1]
    nb = ys_ref.shape[0] // rb
    n_used = nu_ref[0]

    def x_copy(g, slot):
        rows = pl.ds(pl.multiple_of(g * rb, rb), rb)
        return pltpu.make_async_copy(xs_ref.at[rows], xbuf.at[slot], xsem.at[slot])

    def y_copy(g, slot):
        rows = pl.ds(pl.multiple_of(g * rb, rb), rb)
        return pltpu.make_async_copy(ybuf.at[slot], ys_ref.at[rows], ysem.at[slot])

    @pl.when(e == 0)
    def _():
        x_copy(0, 0).start()

    @pl.when(nblk_ref[e] > 0)
    def _():
        wgb_ref[...] = wg_ref[0, 0].astype(BF16)
        wub_ref[...] = wu_ref[0, 0].astype(BF16)
        wdb_ref[...] = wd_ref[0, 0].astype(BF16)

    def block(b, carry):
        g = fb_ref[e] + b
        slot = lax.rem(g, 2)
        x_copy(g, slot).wait()

        @pl.when(g + 1 < n_used)
        def _():
            x_copy(g + 1, 1 - slot).start()

        @pl.when(g >= 2)
        def _():
            y_copy(g - 2, slot).wait()

        valid = lax.broadcasted_iota(jnp.int32, (rb, 1), 0) < nv_ref[g]
        x_hi, x_lo = _unpack_bf16_pairs(jnp.where(valid, xbuf[slot], jnp.uint32(0)))
        x_hi, x_lo = x_hi.astype(BF16), x_lo.astype(BF16)
        h = x_hi.shape[1]

        def in_dot(w_ref):
            return (jnp.dot(x_hi, w_ref[:h, :], preferred_element_type=F32)
                    + jnp.dot(x_lo, w_ref[h:, :], preferred_element_type=F32))

        gate = in_dot(wgb_ref)
        up = in_dot(wub_ref)
        hmid = (gate * jax.nn.sigmoid(gate) * up).astype(BF16)
        ybuf[slot] = _pack_bf16_pairs(jnp.dot(hmid, wdb_ref[...], preferred_element_type=F32))
        y_copy(g, slot).start()
        return carry

    lax.fori_loop(0, nblk_ref[e], block, 0)

    @pl.when(e == pl.num_programs(0) - 1)
    def _():
        @pl.when(n_used >= 2)
        def _():
            y_copy(n_used - 2, lax.rem(n_used, 2)).wait()

        y_copy(n_used - 1, lax.rem(n_used - 1, 2)).wait()
        ybuf[0] = jnp.zeros(ybuf.shape[1:], ybuf.dtype)

        def tail_start(g, carry):
            y_copy(g, 0).start()
            return carry

        def tail_wait(g, carry):
            y_copy(g, 0).wait()
            return carry

        lax.fori_loop(n_used, nb, tail_start, 0)
        lax.fori_loop(n_used, nb, tail_wait, 0)


def _experts(xs, first_block, n_blocks, n_valid, n_used, w_gate, w_up, w_down, layer):
    n_rows, dp = xs.shape
    d = 2 * dp
    n_experts, ff = w_gate.shape[1], w_gate.shape[3]
    wsel = lambda e, *_: (layer, e, 0, 0)
    grid_spec = pltpu.PrefetchScalarGridSpec(
        num_scalar_prefetch=4,
        grid=(n_experts,),
        in_specs=[
            pl.BlockSpec(memory_space=pl.ANY),
            pl.BlockSpec((1, 1, d, ff), wsel),
            pl.BlockSpec((1, 1, d, ff), wsel),
            pl.BlockSpec((1, 1, ff, d), wsel),
        ],
        out_specs=pl.BlockSpec(memory_space=pl.ANY),
        scratch_shapes=[pltpu.VMEM((d, ff), BF16), pltpu.VMEM((d, ff), BF16), pltpu.VMEM((ff, d), BF16),
                        pltpu.VMEM((2, ROW_BLOCK, dp), xs.dtype), pltpu.VMEM((2, ROW_BLOCK, dp), xs.dtype),
                        pltpu.SemaphoreType.DMA((2,)), pltpu.SemaphoreType.DMA((2,))],
    )
    return pl.pallas_call(
        _expert_kernel,
        grid_spec=grid_spec,
        out_shape=jax.ShapeDtypeStruct((n_rows, dp), xs.dtype),
        compiler_params=_cparams(("arbitrary",)),
    )(first_block, n_blocks, n_valid, n_used, xs, w_gate, w_up, w_down)


def _final_combine_kernel(x_ref, route_ref, y1_ref, y2_ref, nf_ref, o_ref):
    route = route_ref[...]
    y1 = jnp.concatenate(_unpack_bf16_pairs(y1_ref[...]), axis=1)
    y2 = jnp.concatenate(_unpack_bf16_pairs(y2_ref[...]), axis=1)
    y = x_ref[...] + (route[:, 2:3] * y1 + route[:, 3:4] * y2)
    o_ref[...] = y * lax.rsqrt(jnp.mean(y * y, axis=-1, keepdims=True) + RMS_EPS) * nf_ref[...]


def _final_combine(x1, route, yg, nf):
    t, d = x1.shape
    slot2 = t // TM
    row = lambda i: (i, 0)
    return pl.pallas_call(
        _final_combine_kernel,
        grid=(t // TM,),
        in_specs=[
            pl.BlockSpec((TM, d), row),
            pl.BlockSpec((TM, LANES), row),
            pl.BlockSpec((TM, d // 2), row),
            pl.BlockSpec((TM, d // 2), lambda i: (slot2 + i, 0)),
            pl.BlockSpec((1, d), lambda i: (0, 0)),
        ],
        out_specs=pl.BlockSpec((TM, d), row),
        out_shape=jax.ShapeDtypeStruct((t, d), F32),
        compiler_params=_cparams(("parallel",)),
    )(x1, route, yg, yg, nf.reshape(1, d).astype(F32))


def _moe(hn, route_t, cnt, w_gate, w_up, w_down, layer):
    t, d = hn.shape
    n_experts = w_gate.shape[1]
    ids = jnp.arange(n_experts, dtype=jnp.int32)
    e1, e2, rank1, rank2 = (route_t[f].astype(jnp.int32) for f in (0, 1, 4, 5))
    counts = cnt[:, 0].astype(jnp.int32)
    padded = ((counts + ROW_BLOCK - 1) // ROW_BLOCK) * ROW_BLOCK
    pad_end = jnp.cumsum(padded)
    pad_start = pad_end - padded

    def dest(e, rank):
        return jnp.sum(jnp.where(e[:, None] == ids[None, :], pad_start[None, :], 0), axis=1) + rank

    d1 = dest(e1, rank1)
    d2 = dest(e2, rank2)
    nb = (t * TOP_K_INNER) // ROW_BLOCK + n_experts
    n_used = (pad_end[-1:] // ROW_BLOCK).astype(jnp.int32)
    first_row = jnp.arange(nb, dtype=jnp.int32) * ROW_BLOCK
    block_e = jnp.sum((pad_end[None, :] <= first_row[:, None]).astype(jnp.int32), axis=1)
    block_e = jnp.minimum(block_e, n_experts - 1)
    row_end = jnp.sum(jnp.where(block_e[:, None] == ids[None, :], (pad_start + counts)[None, :], 0), axis=1)
    n_valid = jnp.clip(row_end - first_row, 0, ROW_BLOCK).astype(jnp.int32)
    xs = _sc_scatter_rows(hn, d1, d2, nb * ROW_BLOCK)
    ys = _experts(xs, (pad_start // ROW_BLOCK).astype(jnp.int32), (padded // ROW_BLOCK).astype(jnp.int32),
                  n_valid, n_used, w_gate, w_up, w_down, layer)
    return _sc_gather_rows(ys, jnp.concatenate([d1, d2]))


def _diff_lambda_init(layer_idx):
    return 0.8 - 0.6 * math.exp(-0.3 * layer_idx)


def _rope_tables(seq, dk):
    angle = 1.0 / (10000.0 ** jnp.linspace(0.0, 1.0, dk // 2, dtype=F32))
    angle = jnp.repeat(angle, 2)
    phase = jnp.arange(seq, dtype=F32)[:, None] * angle[None, :]
    sin, cos = jnp.sin(phase), jnp.cos(phase)
    even = (jnp.arange(dk) % 2 == 0)[None, :]
    return cos, jnp.where(even, -sin, 0.0), jnp.where(even, 0.0, sin)


def kernel(x, rel_bias_table, norm_mix, norm_ffn, norm_final, diff_w_in, diff_lambda_qk, diff_subln,
           diff_w_out, ret_w_in, ret_w_out, moe_w_group, moe_b_group, moe_w_expert, moe_b_expert,
           moe_w_gate, moe_w_up, moe_w_down):
    batch, seq, d = x.shape
    depth = norm_mix.shape[0]
    t = batch * seq
    heads = rel_bias_table.shape[1]
    dh = diff_lambda_qk.shape[-1]
    qk_dim = (ret_w_in.shape[2] - 2 * ret_w_out.shape[1]) // 2
    v_dim = ret_w_out.shape[1]
    dk = qk_dim // RET_HEADS
    xt = x.reshape(t, d)
    moe = None
    for i in range(depth):
        j = i // N_MIXERS
        if i % N_MIXERS == 0:
            col_scale = jnp.concatenate([jnp.full((d,), dh ** -0.5 * LOG2E, F32), jnp.ones((2 * d,), F32)])
            w_in = (diff_w_in[j] * col_scale[None, :]).astype(BF16)
            qkv = _norm_matmul(xt, norm_mix[i], w_in, moe=moe)
            if moe is not None:
                qkv, xt = qkv
            mix = _diff_attention(qkv, rel_bias_table, diff_lambda_qk[j], diff_subln[j],
                                  _diff_lambda_init(i), batch, seq, heads)
            gate = None
            w_out = diff_w_out[j].astype(BF16)
        else:
            n_in = ret_w_in.shape[2]
            col_scale = jnp.concatenate([jnp.ones((qk_dim,), F32), jnp.full((qk_dim,), dk ** -0.5, F32),
                                         jnp.ones((n_in - 2 * qk_dim,), F32)])
            w_in = (ret_w_in[j] * col_scale[None, :]).astype(BF16)
            qkvg = _norm_matmul(xt, norm_mix[i], w_in, rope=_rope_tables(seq, dk),
                                n_rope=2 * qk_dim // TN, seq=seq, moe=moe)
            if moe is not None:
                qkvg, xt = qkvg
            mix = _retention(qkvg, batch, seq, qk_dim, v_dim)
            gate = (qkvg, (2 * qk_dim + v_dim) // v_dim)
            w_out = ret_w_out[j].astype(BF16)
        x1, hn, route, route_t, cnt = _out_router(mix, gate, w_out, xt, norm_ffn[i], moe_w_group[i],
                                                  moe_b_group[i], moe_w_expert[i], moe_b_expert[i])
        xt = x1
        moe = (route, _moe(hn, route_t, cnt, moe_w_gate, moe_w_up, moe_w_down, i))
    return _final_combine(xt, moe[0], moe[1], norm_final).reshape(batch, seq, d)
```

```python
import functools
import math

import jax
import jax.numpy as jnp
from jax import lax
from jax.experimental import pallas as pl
from jax.experimental.pallas import tpu as pltpu
from jax.experimental.pallas import tpu_sc as plsc

F32 = jnp.float32
BF16 = jnp.bfloat16

N_MIXERS = 2
NUM_BUCKETS = 32
MAX_DISTANCE = 128
RET_HEADS = 4
TOP_K_INNER = 2
RMS_EPS = 1e-6
SUBLN_EPS = 1e-5

LANES = 128
SUBLANES = 8
BF16_SUBLANES = 16
LOG2E = 1.4426950408889634
VMEM_LIMIT = 56 * 1024 * 1024
SC_CORES = 2
SC_SUBCORES = 16

TM = 512
TN = 1024
TQ = 256
ATTN_HEADS = 4
RET_C = 256
RET_HP = 2
SCALE_CAST_ROWS = 256
ROUTER_SPLIT = 2
ROW_BLOCK = 512
SC_CHUNK = 64


def _cparams(sem):
    return pltpu.CompilerParams(dimension_semantics=sem, vmem_limit_bytes=VMEM_LIMIT)


def _pack_bf16_pairs(x):
    h = x.shape[1] // 2
    hi = lax.bitcast_convert_type(x[:, :h].astype(BF16).astype(F32), jnp.uint32)
    lo = lax.bitcast_convert_type(x[:, h:].astype(BF16).astype(F32), jnp.uint32)
    return hi | (lo >> 16)


def _unpack_bf16_pairs(w):
    hi = lax.bitcast_convert_type(w & jnp.uint32(0xFFFF0000), F32)
    lo = lax.bitcast_convert_type(w << 16, F32)
    return hi, lo


def _scale_cast_kernel(w_ref, s_ref, o_ref):
    o_ref[...] = (w_ref[0] * s_ref[...]).astype(o_ref.dtype)


def _scale_cast(w, layer, col_scale):
    _, k, n = w.shape
    rows = SCALE_CAST_ROWS
    return pl.pallas_call(
        _scale_cast_kernel,
        grid=(k // rows,),
        in_specs=[pl.BlockSpec((1, rows, n), lambda i: (layer, i, 0)),
                  pl.BlockSpec((1, n), lambda i: (0, 0))],
        out_specs=pl.BlockSpec((rows, n), lambda i: (i, 0)),
        out_shape=jax.ShapeDtypeStruct((k, n), BF16),
        compiler_params=_cparams(("parallel",)),
    )(w, col_scale.reshape(1, n).astype(F32))


def _norm_matmul_kernel(x_ref, *rest, n_rope, moe):
    rest = list(rest)
    if moe:
        route_ref, y1_ref, y2_ref = rest[:3]
        x_out_ref = rest.pop()
        rest = rest[3:]
    g_ref, w_ref = rest[:2]
    if n_rope:
        cos_ref, sa_ref, sb_ref, o_ref = rest[2:]
    else:
        (o_ref,) = rest[2:]
    x = x_ref[...]
    if moe:
        route = route_ref[...]
        y1 = jnp.concatenate(_unpack_bf16_pairs(y1_ref[...]), axis=1)
        y2 = jnp.concatenate(_unpack_bf16_pairs(y2_ref[...]), axis=1)
        x = x + (route[:, 2:3] * y1 + route[:, 3:4] * y2)
        x_out_ref[...] = x
    xn = (x * lax.rsqrt(jnp.mean(x * x, axis=-1, keepdims=True) + RMS_EPS) * g_ref[...]).astype(BF16)
    for j in range(o_ref.shape[1] // TN):
        cols = slice(j * TN, (j + 1) * TN)
        acc = jnp.dot(xn, w_ref[:, cols], preferred_element_type=F32)
        if j < n_rope:
            reps = TN // cos_ref.shape[1]
            c = jnp.tile(cos_ref[...], (1, reps))
            sa = jnp.tile(sa_ref[...], (1, reps))
            sb = jnp.tile(sb_ref[...], (1, reps))
            nxt = pltpu.roll(acc, TN - 1, axis=1)
            prv = pltpu.roll(acc, 1, axis=1)
            acc = acc * c + nxt * sa + prv * sb
        o_ref[:, cols] = acc.astype(o_ref.dtype)


def _norm_matmul(x, g, w, rope=None, n_rope=0, seq=None, moe=None):
    t, d = x.shape
    n = w.shape[1]
    row = lambda i: (i, 0)
    in_specs = [pl.BlockSpec((TM, d), row)]
    args = [x]
    if moe is not None:
        route, yg = moe
        slot2 = t // TM
        in_specs += [pl.BlockSpec((TM, LANES), row), pl.BlockSpec((TM, d // 2), row),
                     pl.BlockSpec((TM, d // 2), lambda i: (slot2 + i, 0))]
        args += [route, yg, yg]
    in_specs += [
        pl.BlockSpec((1, d), lambda i: (0, 0)),
        pl.BlockSpec((d, n), lambda i: (0, 0), pipeline_mode=pl.Buffered(1)),
    ]
    args += [g.reshape(1, d), w]
    if rope is not None:
        nb = seq // TM
        dk = rope[0].shape[1]
        in_specs += [pl.BlockSpec((TM, dk), lambda i: (i % nb, 0))] * 3
        args += list(rope)
    out_specs = [pl.BlockSpec((TM, n), row)]
    out_shape = [jax.ShapeDtypeStruct((t, n), BF16)]
    if moe is not None:
        out_specs.append(pl.BlockSpec((TM, d), row))
        out_shape.append(jax.ShapeDtypeStruct((t, d), F32))
    out = pl.pallas_call(
        functools.partial(_norm_matmul_kernel, n_rope=n_rope, moe=moe is not None),
        grid=(t // TM,),
        in_specs=in_specs,
        out_specs=out_specs,
        out_shape=out_shape,
        compiler_params=_cparams(("parallel",)),
    )(*args)
    return out if moe is not None else out[0]


def _diff_attn_kernel(q_ref, k_ref, v_ref, bias_ref, lam_ref, g_ref, o_ref,
                      vt_ref, m_ref, acc_ref, s_ref, *, lambda_init):
    seq = q_ref.shape[0]
    hp, _, _, tq = bias_ref.shape
    dv = q_ref.shape[1] // hp
    dh = dv // 2
    nq = seq // tq

    def head(hh):
        return slice(hh * dv, (hh + 1) * dv)

    extra = vt_ref.shape[2] - dv
    ones_row = jnp.where(lax.broadcasted_iota(jnp.int32, (extra, tq), 0) == 0, 1.0, 0.0).astype(BF16)
    for hh in range(hp):
        for j in range(nq):
            vt_ref[hh, j, :dv, :] = v_ref[j * tq:(j + 1) * tq, head(hh)].astype(F32).T.astype(BF16)
            vt_ref[hh, j, dv:, :] = ones_row

    lq = lam_ref[...]
    lam = (jnp.exp(jnp.sum(lq[0:1] * lq[1:2], axis=-1, keepdims=True))
           - jnp.exp(jnp.sum(lq[2:3] * lq[3:4], axis=-1, keepdims=True)) + lambda_init)
    lane = lax.broadcasted_iota(jnp.int32, (tq, dv), 1)

    def query_maps(qi):
        qrows = pl.ds(pl.multiple_of(qi * tq, tq), tq)
        qs = []
        for hh in range(hp):
            q = q_ref[qrows, head(hh)]
            zero = jnp.zeros_like(q)
            qs.append(jnp.concatenate([jnp.where(lane < dh, q, zero), jnp.where(lane >= dh, q, zero)],
                                      axis=0))
        return qs

    def score_dots(j, qs):
        rows = pl.ds(pl.multiple_of(j * tq, tq), tq)
        for hh in range(hp):
            s_ref[hh] = lax.dot_general(k_ref[rows, head(hh)], qs[hh], (((1,), (1,)), ((), ())),
                                        preferred_element_type=F32)

    score_dots(0, query_maps(0))

    def q_body(qi, carry):
        qrows = pl.ds(pl.multiple_of(qi * tq, tq), tq)
        qs = query_maps(qi)
        qs_next = query_maps(jnp.minimum(qi + 1, nq - 1))

        m_ref[...] = jnp.full(m_ref.shape, -jnp.inf, F32)
        acc_ref[...] = jnp.zeros(acc_ref.shape, F32)

        def step(j, tile, prefetch):
            probs, alphas = [], []
            for hh in range(hp):
                s = s_ref[hh]
                if tile is not None:
                    bias = bias_ref[hh, tile]
                    s = s + jnp.concatenate([bias, bias], axis=1)
                m_old = m_ref[hh]
                m_new = jnp.maximum(m_old, jnp.max(s, axis=0, keepdims=True))
                alpha = jnp.exp2(m_old - m_new)
                m_ref[hh] = m_new
                probs.append(jnp.exp2(s - m_new).astype(BF16))
                alphas.append(alpha)
            if prefetch == "block":
                score_dots(j + 1, qs)
            else:
                score_dots(0, qs_next)
            for hh in range(hp):
                acc_ref[hh] = alphas[hh] * acc_ref[hh] + jnp.dot(vt_ref[hh, j], probs[hh],
                                                                 preferred_element_type=F32)

        def far_body(j, c):
            step(j, None, "block")
            return c

        lax.fori_loop(0, jnp.maximum(qi - 1, 0), far_body, 0)

        @pl.when(qi >= 1)
        def _():
            step(qi - 1, 1, "block")

        step(qi, 0, "tile")

        for hh in range(hp):
            acc = acc_ref[hh]
            on = acc[:dv] / acc[dv:dv + 1]
            ot = on[:, :tq] - lam * on[:, tq:]
            ot = ot * lax.rsqrt(jnp.mean(ot * ot, axis=0, keepdims=True) + SUBLN_EPS)
            o_ref[qrows, head(hh)] = (ot.T * g_ref[...] * (1.0 - lambda_init)).astype(o_ref.dtype)
        return carry

    lax.fori_loop(0, nq, q_body, 0)


def _t5_bucket(rel):
    max_exact = NUM_BUCKETS // 2
    n = jnp.maximum(rel, 0)
    is_small = n < max_exact
    nf = jnp.maximum(n, max_exact).astype(F32)
    large = max_exact + (jnp.log(nf / max_exact) / math.log(MAX_DISTANCE / max_exact)
                         * (NUM_BUCKETS - max_exact)).astype(jnp.int32)
    large = jnp.minimum(large, NUM_BUCKETS - 1)
    return jnp.where(is_small, n, large)


def _bias_tiles(rel_table):
    heads = rel_table.shape[1]
    span = 2 * TQ + 1
    bucket = _t5_bucket(jnp.arange(span))
    onehot = (bucket[:, None] == jnp.arange(NUM_BUCKETS)[None, :]).astype(F32)
    table = (rel_table.astype(F32) - rel_table[NUM_BUCKETS - 1].astype(F32)[None, :]) * LOG2E
    by_rel = jnp.dot(onehot, table, precision=lax.Precision.HIGHEST).T
    toep = jnp.tile(by_rel, (1, TQ))[:, :TQ * (span - 1)].reshape(heads, TQ, span - 1)
    kj = jnp.arange(TQ)[:, None]
    qi = jnp.arange(TQ)[None, :]
    diag = jnp.where((qi >= kj)[None], toep[:, :, :TQ], -jnp.inf)
    return jnp.stack([diag, toep[:, :, TQ:]], axis=1)


def _diff_attention(qkv, rel_table, lam_qk, subln_g, lambda_init, batch, seq, heads):
    t = qkv.shape[0]
    d = qkv.shape[1] // 3
    dv = d // heads
    hp = ATTN_HEADS
    assert TQ + 1 >= MAX_DISTANCE and seq % TQ == 0 and dv == LANES and heads % hp == 0
    nq = seq // TQ
    groups = heads // hp
    bias = _bias_tiles(rel_table)
    return pl.pallas_call(
        functools.partial(_diff_attn_kernel, lambda_init=lambda_init),
        grid=(batch, groups),
        in_specs=[
            pl.BlockSpec((seq, hp * dv), lambda b, g: (b, g)),
            pl.BlockSpec((seq, hp * dv), lambda b, g: (b, groups + g)),
            pl.BlockSpec((seq, hp * dv), lambda b, g: (b, 2 * groups + g)),
            pl.BlockSpec((hp, 2, TQ, TQ), lambda b, g: (g, 0, 0, 0)),
            pl.BlockSpec(lam_qk.shape, lambda b, g: (0, 0)),
            pl.BlockSpec((1, dv), lambda b, g: (0, 0)),
        ],
        out_specs=pl.BlockSpec((seq, hp * dv), lambda b, g: (b, g)),
        out_shape=jax.ShapeDtypeStruct((t, d), BF16),
        scratch_shapes=[
            pltpu.VMEM((hp, nq, dv + BF16_SUBLANES, TQ), BF16),
            pltpu.VMEM((hp, 1, 2 * TQ), F32),
            pltpu.VMEM((hp, dv + BF16_SUBLANES, 2 * TQ), F32),
            pltpu.VMEM((hp, TQ, 2 * TQ), F32),
        ],
        compiler_params=_cparams(("parallel", "parallel")),
    )(qkv, qkv, qkv, bias, lam_qk.astype(F32), subln_g.reshape(1, dv).astype(F32))


def _retention_kernel(q_ref, k_ref, v_ref, dec_ref, tab_ref, o_ref, state_ref):
    seq = q_ref.shape[0]
    hp, c, _ = dec_ref.shape
    dk = q_ref.shape[1] // hp
    dv = v_ref.shape[1] // hp
    heads = range(hp)
    state_ref[...] = jnp.zeros(state_ref.shape, F32)

    def body(n, carry):
        rows = pl.ds(pl.multiple_of(n * c, c), c)
        q = [q_ref[rows, hh * dk:(hh + 1) * dk] for hh in heads]
        k = [k_ref[rows, hh * dk:(hh + 1) * dk] for hh in heads]
        v = [v_ref[rows, hh * dv:(hh + 1) * dv] for hh in heads]
        tab = [tab_ref[hh] for hh in heads]
        st = [state_ref[hh] for hh in heads]
        s = [lax.dot_general(q[hh], k[hh], (((1,), (1,)), ((), ())), preferred_element_type=F32)
             for hh in heads]
        cross = [jnp.dot(q[hh], st[hh].astype(BF16), preferred_element_type=F32) for hh in heads]
        kt = [(k[hh].astype(F32) * tab[hh][:, 1:2]).T.astype(BF16) for hh in heads]
        inner = [jnp.dot((s[hh] * dec_ref[hh]).astype(BF16), v[hh], preferred_element_type=F32)
                 for hh in heads]
        for hh in heads:
            state_ref[hh] = st[hh] * tab[hh][0:1, 2:3] + jnp.dot(kt[hh], v[hh], preferred_element_type=F32)
        for hh in heads:
            o = inner[hh] + cross[hh] * tab[hh][:, 0:1]
            o = o * lax.rsqrt(jnp.mean(o * o, axis=-1, keepdims=True) + RMS_EPS)
            o_ref[rows, hh * dv:(hh + 1) * dv] = o.astype(o_ref.dtype)
        return carry

    lax.fori_loop(0, seq // c, body, 0)


def _retention_tables(heads):
    c = RET_C
    log_gamma = jnp.log(1.0 - 2.0 ** (-5.0 - jnp.arange(heads, dtype=F32)))
    pos = jnp.arange(c, dtype=F32)
    dist = pos[:, None] - pos[None, :]
    inner_decay = jnp.where(dist[None] >= 0,
                            jnp.exp(jnp.maximum(dist, 0.0)[None] * log_gamma[:, None, None]), 0.0)
    q_decay = jnp.exp((pos + 1.0)[None] * log_gamma[:, None])
    k_decay = jnp.exp((c - 1.0 - pos)[None] * log_gamma[:, None])
    chunk_decay = jnp.broadcast_to(jnp.exp(c * log_gamma)[:, None], (heads, c))
    tab = jnp.stack([q_decay, k_decay, chunk_decay], axis=-1)
    tab = jnp.pad(tab, ((0, 0), (0, 0), (0, LANES - 3)))
    return inner_decay, tab


def _retention(qkvg, batch, seq, qk_dim, v_dim):
    t = qkvg.shape[0]
    dk = qk_dim // RET_HEADS
    dvh = v_dim // RET_HEADS
    inner_decay, tab = _retention_tables(RET_HEADS)
    hp = RET_HP
    assert RET_HEADS % hp == 0
    groups = RET_HEADS // hp
    kb = qk_dim // (hp * dk)
    vb = 2 * qk_dim // (hp * dvh)
    return pl.pallas_call(
        _retention_kernel,
        grid=(batch, groups),
        in_specs=[
            pl.BlockSpec((seq, hp * dk), lambda b, g: (b, g)),
            pl.BlockSpec((seq, hp * dk), lambda b, g: (b, kb + g)),
            pl.BlockSpec((seq, hp * dvh), lambda b, g: (b, vb + g)),
            pl.BlockSpec((hp, RET_C, RET_C), lambda b, g: (g, 0, 0)),
            pl.BlockSpec((hp, RET_C, LANES), lambda b, g: (g, 0, 0)),
        ],
        out_specs=pl.BlockSpec((seq, hp * dvh), lambda b, g: (b, g)),
        out_shape=jax.ShapeDtypeStruct((t, v_dim), BF16),
        scratch_shapes=[pltpu.VMEM((hp, dk, dvh), F32)],
        compiler_params=_cparams(("parallel", "parallel")),
    )(qkvg, qkvg, qkvg, inner_decay, tab)


def _out_router_kernel(*refs, gated, n_groups, n_experts):
    if gated:
        a_ref, gate_ref, w_ref, x_ref, nf_ref, wr_ref, br_ref = refs[:7]
        x1_ref, hn_ref, route_ref, route_t_ref, cnt_ref, carry_ref = refs[7:]
    else:
        a_ref, w_ref, x_ref, nf_ref, wr_ref, br_ref = refs[:6]
        x1_ref, hn_ref, route_ref, route_t_ref, cnt_ref, carry_ref = refs[6:]
    epg = n_experts // n_groups
    tm = x_ref.shape[0] // ROUTER_SPLIT

    @pl.when(pl.program_id(0) == 0)
    def _():
        carry_ref[...] = jnp.zeros(carry_ref.shape, F32)

    x1_parts = []
    for part in range(ROUTER_SPLIT):
        rows = slice(part * tm, (part + 1) * tm)
        a = a_ref[rows, :]
        if gated:
            gate = gate_ref[rows, :]
            a = gate * jax.nn.sigmoid(gate) * a
        x1_parts.append(x_ref[rows, :] + jnp.dot(a, w_ref[...], preferred_element_type=F32))
    for part in range(ROUTER_SPLIT):
        rows = slice(part * tm, (part + 1) * tm)
        _route_rows(x1_parts[part], rows, nf_ref, wr_ref, br_ref, x1_ref, hn_ref, route_ref, route_t_ref,
                    cnt_ref, carry_ref, n_groups=n_groups, n_experts=n_experts, epg=epg)


def _route_rows(x1, rows, nf_ref, wr_ref, br_ref, x1_ref, hn_ref, route_ref, route_t_ref, cnt_ref,
                carry_ref, *, n_groups, n_experts, epg):
    tm = x1.shape[0]
    x1_ref[rows, :] = x1
    hn = x1 * lax.rsqrt(jnp.mean(x1 * x1, axis=-1, keepdims=True) + RMS_EPS) * nf_ref[...]
    hn_ref[rows, :] = _pack_bf16_pairs(hn)

    hn_hi = hn.astype(BF16)
    hn_lo = (hn - hn_hi.astype(F32)).astype(BF16)
    logits = (jnp.dot(hn_hi, wr_ref[0], preferred_element_type=F32)
              + jnp.dot(hn_lo, wr_ref[0], preferred_element_type=F32)
              + jnp.dot(hn_hi, wr_ref[1], preferred_element_type=F32)) + br_ref[...]

    n_rows = -(-(n_groups + n_experts) // SUBLANES) * SUBLANES
    lt = logits.T[:n_rows, :]
    row = lax.broadcasted_iota(jnp.int32, lt.shape, 0)
    big = jnp.int32(LANES)
    neg = jnp.float32(-jnp.inf)

    gl = jnp.where(row < n_groups, lt, neg)
    gmax = jnp.max(gl, axis=0, keepdims=True)
    g_p = 1.0 / jnp.sum(jnp.exp(gl - gmax), axis=0, keepdims=True)
    g_idx = jnp.min(jnp.where(gl == gmax, row, big), axis=0, keepdims=True)

    lo = n_groups + g_idx * epg
    el = jnp.where((row >= lo) & (row < lo + epg), lt, neg)
    v1 = jnp.max(el, axis=0, keepdims=True)
    i1 = jnp.min(jnp.where(el == v1, row, big), axis=0, keepdims=True)
    el2 = jnp.where(row == i1, neg, el)
    v2 = jnp.max(el2, axis=0, keepdims=True)
    i2 = jnp.min(jnp.where(el2 == v2, row, big), axis=0, keepdims=True)
    tt = jnp.exp(v2 - v1)
    w1 = g_p / (1.0 + tt)
    w2 = g_p * tt / (1.0 + tt)
    e1 = i1 - n_groups
    e2 = i2 - n_groups

    erow = lax.broadcasted_iota(jnp.int32, (n_experts, tm), 0)
    oh1 = erow == e1
    oh2 = erow == e2
    onehot = jnp.where(oh1 | oh2, 1.0, 0.0)
    before = lax.broadcasted_iota(jnp.int32, (tm, tm), 0) < lax.broadcasted_iota(jnp.int32, (tm, tm), 1)
    tri = jnp.where(before, 1.0, 0.0).astype(BF16)
    carry = carry_ref[...]
    base = jnp.dot(onehot.astype(BF16), tri, preferred_element_type=F32) + carry[:, 0:1]
    rank1 = jnp.sum(jnp.where(oh1, base, 0.0), axis=0, keepdims=True)
    rank2 = jnp.sum(jnp.where(oh2, base, 0.0), axis=0, keepdims=True)
    carry = carry + jnp.sum(onehot, axis=1, keepdims=True)
    carry_ref[...] = carry
    cnt_ref[...] = carry

    frow = lax.broadcasted_iota(jnp.int32, (LANES, tm), 0)
    out = jnp.zeros((LANES, tm), F32)
    for idx, val in enumerate((e1.astype(F32), e2.astype(F32), w1, w2, rank1, rank2)):
        out = jnp.where(frow == idx, val, out)
    route_t_ref[:, rows] = out[:route_t_ref.shape[0], :]
    route_ref[rows, :] = out.T


def _out_router(a, gate, w, x, nf, w_group, b_group, w_expert, b_expert):
    t, d = x.shape
    k = w.shape[0]
    n_groups = w_group.shape[1]
    n_experts = w_expert.shape[1]
    assert n_groups + n_experts <= LANES
    pad = LANES - n_groups - n_experts
    wr = jnp.pad(jnp.concatenate([w_group, w_expert], axis=1).astype(F32), ((0, 0), (0, pad)))
    wr_hi = wr.astype(BF16)
    wr = jnp.stack([wr_hi, (wr - wr_hi.astype(F32)).astype(BF16)])
    br = jnp.pad(jnp.concatenate([b_group, b_expert]).astype(F32), (0, pad)).reshape(1, LANES)
    gated = gate is not None
    row = lambda i: (i, 0)
    fixed = lambda i: (0, 0)
    in_specs = [pl.BlockSpec((TM, k), row)]
    args = [a]
    if gated:
        gate_arr, gate_col = gate
        in_specs.append(pl.BlockSpec((TM, k), lambda i: (i, gate_col)))
        args.append(gate_arr)
    in_specs += [
        pl.BlockSpec((k, d), fixed),
        pl.BlockSpec((TM, d), row),
        pl.BlockSpec((1, d), fixed),
        pl.BlockSpec((2, d, LANES), lambda i: (0, 0, 0)),
        pl.BlockSpec((1, LANES), fixed),
    ]
    args += [w, x, nf.reshape(1, d), wr, br]
    return pl.pallas_call(
        functools.partial(_out_router_kernel, gated=gated, n_groups=n_groups, n_experts=n_experts),
        grid=(t // TM,),
        in_specs=in_specs,
        out_specs=[
            pl.BlockSpec((TM, d), row),
            pl.BlockSpec((TM, d // 2), row),
            pl.BlockSpec((TM, LANES), row),
            pl.BlockSpec((SUBLANES, TM), lambda i: (0, i)),
            pl.BlockSpec((n_experts, LANES), fixed),
        ],
        out_shape=[
            jax.ShapeDtypeStruct((t, d), F32),
            jax.ShapeDtypeStruct((t, d // 2), jnp.uint32),
            jax.ShapeDtypeStruct((t, LANES), F32),
            jax.ShapeDtypeStruct((SUBLANES, t), F32),
            jax.ShapeDtypeStruct((n_experts, LANES), F32),
        ],
        scratch_shapes=[pltpu.VMEM((n_experts, LANES), F32)],
        compiler_params=_cparams(("arbitrary",)),
    )(*args)


def _sc_mesh():
    return plsc.VectorSubcoreMesh(core_axis_name="c", subcore_axis_name="s",
                                  num_cores=SC_CORES, num_subcores=SC_SUBCORES)


def _sc_worker():
    return lax.axis_index("s") * SC_CORES + lax.axis_index("c")


def _sc_scatter_rows(src, d1, d2, n_rows):
    t, d = src.shape
    workers = SC_CORES * SC_SUBCORES
    per_w = t // workers
    n_ch = per_w // SC_CHUNK
    assert t % (workers * SC_CHUNK) == 0
    idx = jnp.stack([d1, d2]).reshape(2, workers, n_ch, SC_CHUNK)

    @functools.partial(
        pl.kernel, mesh=_sc_mesh(),
        out_type=jax.ShapeDtypeStruct((n_rows, d), src.dtype),
        scratch_types=[pltpu.VMEM((2, n_ch, SC_CHUNK), jnp.int32),
                       pltpu.VMEM((SC_CHUNK, d), src.dtype),
                       pltpu.VMEM((SC_CHUNK, d), src.dtype),
                       pltpu.SemaphoreType.DMA, pltpu.SemaphoreType.DMA],
    )
    def scatter(src_hbm, idx_hbm, out_hbm, idx_v, rows_a, rows_b, sem_a, sem_b):
        wid = _sc_worker()
        base = wid * per_w
        for s in range(2):
            pltpu.sync_copy(idx_hbm.at[s, wid], idx_v.at[s])

        def move(c, rows_v, sem):
            pltpu.sync_copy(src_hbm.at[pl.ds(base + c * SC_CHUNK, SC_CHUNK)], rows_v)
            return [pltpu.async_copy(rows_v, out_hbm.at[idx_v.at[s, c]], sem) for s in range(2)]

        @pl.loop(0, n_ch, step=2)
        def _(c):
            first = move(c, rows_a, sem_a)
            second = move(c + 1, rows_b, sem_b)
            for cp in first + second:
                cp.wait()

    return scatter(src, idx)


def _sc_gather_rows(table, idx):
    b = idx.shape[0]
    d = table.shape[1]
    workers = SC_CORES * SC_SUBCORES
    per_w = b // workers
    n_ch = per_w // SC_CHUNK
    assert b % (workers * SC_CHUNK * 2) == 0

    @functools.partial(
        pl.kernel, mesh=_sc_mesh(),
        out_type=jax.ShapeDtypeStruct((b, d), table.dtype),
        scratch_types=[pltpu.VMEM((per_w,), jnp.int32),
                       pltpu.VMEM((SC_CHUNK, d), table.dtype),
                       pltpu.VMEM((SC_CHUNK, d), table.dtype),
                       pltpu.SemaphoreType.DMA, pltpu.SemaphoreType.DMA],
    )
    def gather(table_hbm, idx_hbm, out_hbm, idx_v, rows_a, rows_b, sem_a, sem_b):
        base = _sc_worker() * per_w
        pltpu.sync_copy(idx_hbm.at[pl.ds(base, per_w)], idx_v)

        def fetch(c, rows_v, sem):
            return pltpu.async_copy(table_hbm.at[idx_v.at[pl.ds(c * SC_CHUNK, SC_CHUNK)]], rows_v, sem)

        def store(c, rows_v):
            pltpu.sync_copy(rows_v, out_hbm.at[pl.ds(base + c * SC_CHUNK, SC_CHUNK)])

        @pl.loop(0, n_ch, step=2)
        def _(c):
            first = fetch(c, rows_a, sem_a)
            second = fetch(c + 1, rows_b, sem_b)
            first.wait()
            store(c, rows_a)
            second.wait()
            store(c + 1, rows_b)

    return gather(table, idx)


def _expert_kernel(fb_ref, nblk_ref, nv_ref, nu_ref, xs_ref, wg_ref, wu_ref, wd_ref, ys_ref,
                   wgb_ref, wub_ref, wdb_ref, xbuf, ybuf, xsem, ysem):
    e = pl.program_id(0)
    rb = xbuf.shape[1]
    nb = ys_ref.shape[0] // rb
    n_used = nu_ref[0]

    def x_copy(g, slot):
        rows = pl.ds(pl.multiple_of(g * rb, rb), rb)
        return pltpu.make_async_copy(xs_ref.at[rows], xbuf.at[slot], xsem.at[slot])

    def y_copy(g, slot):
        rows = pl.ds(pl.multiple_of(g * rb, rb), rb)
        return pltpu.make_async_copy(ybuf.at[slot], ys_ref.at[rows], ysem.at[slot])

    @pl.when(e == 0)
    def _():
        x_copy(0, 0).start()

    @pl.when(nblk_ref[e] > 0)
    def _():
        wgb_ref[...] = wg_ref[0, 0].astype(BF16)
        wub_ref[...] = wu_ref[0, 0].astype(BF16)
        wdb_ref[...] = wd_ref[0, 0].astype(BF16)

    def block(b, carry):
        g = fb_ref[e] + b
        slot = lax.rem(g, 2)
        x_copy(g, slot).wait()

        @pl.when(g + 1 < n_used)
        def _():
            x_copy(g + 1, 1 - slot).start()

        @pl.when(g >= 2)
        def _():
            y_copy(g - 2, slot).wait()

        valid = lax.broadcasted_iota(jnp.int32, (rb, 1), 0) < nv_ref[g]
        x_hi, x_lo = _unpack_bf16_pairs(jnp.where(valid, xbuf[slot], jnp.uint32(0)))
        x_hi, x_lo = x_hi.astype(BF16), x_lo.astype(BF16)
        h = x_hi.shape[1]

        def in_dot(w_ref):
            return (jnp.dot(x_hi, w_ref[:h, :], preferred_element_type=F32)
                    + jnp.dot(x_lo, w_ref[h:, :], preferred_element_type=F32))

        gate = in_dot(wgb_ref)
        up = in_dot(wub_ref)
        hmid = (gate * jax.nn.sigmoid(gate) * up).astype(BF16)
        ybuf[slot] = _pack_bf16_pairs(jnp.dot(hmid, wdb_ref[...], preferred_element_type=F32))
        y_copy(g, slot).start()
        return carry

    lax.fori_loop(0, nblk_ref[e], block, 0)

    @pl.when(e == pl.num_programs(0) - 1)
    def _():
        @pl.when(n_used >= 2)
        def _():
            y_copy(n_used - 2, lax.rem(n_used, 2)).wait()

        y_copy(n_used - 1, lax.rem(n_used - 1, 2)).wait()
        ybuf[0] = jnp.zeros(ybuf.shape[1:], ybuf.dtype)

        def tail_start(g, carry):
            y_copy(g, 0).start()
            return carry

        def tail_wait(g, carry):
            y_copy(g, 0).wait()
            return carry

        lax.fori_loop(n_used, nb, tail_start, 0)
        lax.fori_loop(n_used, nb, tail_wait, 0)


def _experts(xs, first_block, n_blocks, n_valid, n_used, w_gate, w_up, w_down, layer):
    n_rows, dp = xs.shape
    d = 2 * dp
    n_experts, ff = w_gate.shape[1], w_gate.shape[3]
    wsel = lambda e, *_: (layer, e, 0, 0)
    grid_spec = pltpu.PrefetchScalarGridSpec(
        num_scalar_prefetch=4,
        grid=(n_experts,),
        in_specs=[
            pl.BlockSpec(memory_space=pl.ANY),
            pl.BlockSpec((1, 1, d, ff), wsel),
            pl.BlockSpec((1, 1, d, ff), wsel),
            pl.BlockSpec((1, 1, ff, d), wsel),
        ],
        out_specs=pl.BlockSpec(memory_space=pl.ANY),
        scratch_shapes=[pltpu.VMEM((d, ff), BF16), pltpu.VMEM((d, ff), BF16), pltpu.VMEM((ff, d), BF16),
                        pltpu.VMEM((2, ROW_BLOCK, dp), xs.dtype), pltpu.VMEM((2, ROW_BLOCK, dp), xs.dtype),
                        pltpu.SemaphoreType.DMA((2,)), pltpu.SemaphoreType.DMA((2,))],
    )
    return pl.pallas_call(
        _expert_kernel,
        grid_spec=grid_spec,
        out_shape=jax.ShapeDtypeStruct((n_rows, dp), xs.dtype),
        compiler_params=_cparams(("arbitrary",)),
    )(first_block, n_blocks, n_valid, n_used, xs, w_gate, w_up, w_down)


def _final_combine_kernel(x_ref, route_ref, y1_ref, y2_ref, nf_ref, o_ref):
    route = route_ref[...]
    y1 = jnp.concatenate(_unpack_bf16_pairs(y1_ref[...]), axis=1)
    y2 = jnp.concatenate(_unpack_bf16_pairs(y2_ref[...]), axis=1)
    y = x_ref[...] + (route[:, 2:3] * y1 + route[:, 3:4] * y2)
    o_ref[...] = y * lax.rsqrt(jnp.mean(y * y, axis=-1, keepdims=True) + RMS_EPS) * nf_ref[...]


def _final_combine(x1, route, yg, nf):
    t, d = x1.shape
    slot2 = t // TM
    row = lambda i: (i, 0)
    return pl.pallas_call(
        _final_combine_kernel,
        grid=(t // TM,),
        in_specs=[
            pl.BlockSpec((TM, d), row),
            pl.BlockSpec((TM, LANES), row),
            pl.BlockSpec((TM, d // 2), row),
            pl.BlockSpec((TM, d // 2), lambda i: (slot2 + i, 0)),
            pl.BlockSpec((1, d), lambda i: (0, 0)),
        ],
        out_specs=pl.BlockSpec((TM, d), row),
        out_shape=jax.ShapeDtypeStruct((t, d), F32),
        compiler_params=_cparams(("parallel",)),
    )(x1, route, yg, yg, nf.reshape(1, d).astype(F32))


def _moe(hn, route_t, cnt, w_gate, w_up, w_down, layer):
    t, d = hn.shape
    n_experts = w_gate.shape[1]
    ids = jnp.arange(n_experts, dtype=jnp.int32)
    e1, e2, rank1, rank2 = (route_t[f].astype(jnp.int32) for f in (0, 1, 4, 5))
    counts = cnt[:, 0].astype(jnp.int32)
    padded = ((counts + ROW_BLOCK - 1) // ROW_BLOCK) * ROW_BLOCK
    pad_end = jnp.cumsum(padded)
    pad_start = pad_end - padded

    def dest(e, rank):
        return jnp.sum(jnp.where(e[:, None] == ids[None, :], pad_start[None, :], 0), axis=1) + rank

    d1 = dest(e1, rank1)
    d2 = dest(e2, rank2)
    nb = (t * TOP_K_INNER) // ROW_BLOCK + n_experts
    n_used = (pad_end[-1:] // ROW_BLOCK).astype(jnp.int32)
    first_row = jnp.arange(nb, dtype=jnp.int32) * ROW_BLOCK
    block_e = jnp.sum((pad_end[None, :] <= first_row[:, None]).astype(jnp.int32), axis=1)
    block_e = jnp.minimum(block_e, n_experts - 1)
    row_end = jnp.sum(jnp.where(block_e[:, None] == ids[None, :], (pad_start + counts)[None, :], 0), axis=1)
    n_valid = jnp.clip(row_end - first_row, 0, ROW_BLOCK).astype(jnp.int32)
    xs = _sc_scatter_rows(hn, d1, d2, nb * ROW_BLOCK)
    ys = _experts(xs, (pad_start // ROW_BLOCK).astype(jnp.int32), (padded // ROW_BLOCK).astype(jnp.int32),
                  n_valid, n_used, w_gate, w_up, w_down, layer)
    return _sc_gather_rows(ys, jnp.concatenate([d1, d2]))


def _diff_lambda_init(layer_idx):
    return 0.8 - 0.6 * math.exp(-0.3 * layer_idx)


def _rope_tables(seq, dk):
    angle = 1.0 / (10000.0 ** jnp.linspace(0.0, 1.0, dk // 2, dtype=F32))
    angle = jnp.repeat(angle, 2)
    phase = jnp.arange(seq, dtype=F32)[:, None] * angle[None, :]
    sin, cos = jnp.sin(phase), jnp.cos(phase)
    even = (jnp.arange(dk) % 2 == 0)[None, :]
    return cos, jnp.where(even, -sin, 0.0), jnp.where(even, 0.0, sin)


def kernel(x, rel_bias_table, norm_mix, norm_ffn, norm_final, diff_w_in, diff_lambda_qk, diff_subln,
           diff_w_out, ret_w_in, ret_w_out, moe_w_group, moe_b_group, moe_w_expert, moe_b_expert,
           moe_w_gate, moe_w_up, moe_w_down):
    batch, seq, d = x.shape
    depth = norm_mix.shape[0]
    t = batch * seq
    heads = rel_bias_table.shape[1]
    dh = diff_lambda_qk.shape[-1]
    qk_dim = (ret_w_in.shape[2] - 2 * ret_w_out.shape[1]) // 2
    v_dim = ret_w_out.shape[1]
    dk = qk_dim // RET_HEADS
    xt = x.reshape(t, d)
    moe = None
    for i in range(depth):
        j = i // N_MIXERS
        if i % N_MIXERS == 0:
            col_scale = jnp.concatenate([jnp.full((d,), dh ** -0.5 * LOG2E, F32), jnp.ones((2 * d,), F32)])
            w_in = _scale_cast(diff_w_in, j, col_scale)
            qkv = _norm_matmul(xt, norm_mix[i], w_in, moe=moe)
            if moe is not None:
                qkv, xt = qkv
            mix = _diff_attention(qkv, rel_bias_table, diff_lambda_qk[j], diff_subln[j],
                                  _diff_lambda_init(i), batch, seq, heads)
            gate = None
            w_out = _scale_cast(diff_w_out, j, jnp.ones((d,), F32))
        else:
            n_in = ret_w_in.shape[2]
            col_scale = jnp.concatenate([jnp.ones((qk_dim,), F32), jnp.full((qk_dim,), dk ** -0.5, F32),
                                         jnp.ones((n_in - 2 * qk_dim,), F32)])
            w_in = _scale_cast(ret_w_in, j, col_scale)
            qkvg = _norm_matmul(xt, norm_mix[i], w_in, rope=_rope_tables(seq, dk),
                                n_rope=2 * qk_dim // TN, seq=seq, moe=moe)
            if moe is not None:
                qkvg, xt = qkvg
            mix = _retention(qkvg, batch, seq, qk_dim, v_dim)
            gate = (qkvg, (2 * qk_dim + v_dim) // v_dim)
            w_out = _scale_cast(ret_w_out, j, jnp.ones((d,), F32))
        x1, hn, route, route_t, cnt = _out_router(mix, gate, w_out, xt, norm_ffn[i], moe_w_group[i],
                                                  moe_b_group[i], moe_w_expert[i], moe_b_expert[i])
        xt = x1
        moe = (route, _moe(hn, route_t, cnt, moe_w_gate, moe_w_up, moe_w_down, i))
    return _final_combine(xt, moe[0], moe[1], norm_final).reshape(batch, seq, d)
```

```python
import functools
import math

import jax
import jax.numpy as jnp
from jax import lax
from jax.experimental import pallas as pl
from jax.experimental.pallas import tpu as pltpu
from jax.experimental.pallas import tpu_sc as plsc

F32 = jnp.float32
BF16 = jnp.bfloat16

N_MIXERS = 2
NUM_BUCKETS = 32
MAX_DISTANCE = 128
RET_HEADS = 4
TOP_K_INNER = 2
RMS_EPS = 1e-6
SUBLN_EPS = 1e-5

LANES = 128
SUBLANES = 8
BF16_SUBLANES = 16
LOG2E = 1.4426950408889634
VMEM_LIMIT = 56 * 1024 * 1024
SC_CORES = 2
SC_SUBCORES = 16

TM = 512
TN = 1024
TQ = 256
ATTN_HEADS = 4
RET_C = 256
RET_HP = 2
WEIGHT_PARTS = 4
ROUTER_SPLIT = 2
ROW_BLOCK = 512
SC_CHUNK = 64


def _cparams(sem):
    return pltpu.CompilerParams(dimension_semantics=sem, vmem_limit_bytes=VMEM_LIMIT)


def _pack_bf16_pairs(x):
    h = x.shape[1] // 2
    hi = lax.bitcast_convert_type(x[:, :h].astype(BF16).astype(F32), jnp.uint32)
    lo = lax.bitcast_convert_type(x[:, h:].astype(BF16).astype(F32), jnp.uint32)
    return hi | (lo >> 16)


def _unpack_bf16_pairs(w):
    hi = lax.bitcast_convert_type(w & jnp.uint32(0xFFFF0000), F32)
    lo = lax.bitcast_convert_type(w << 16, F32)
    return hi, lo


def _norm_matmul_kernel(x_ref, *rest, n_rope, moe):
    rest = list(rest)
    if moe:
        route_ref, y1_ref, y2_ref = rest[:3]
        x_out_ref = rest.pop()
        rest = rest[3:]
    g_ref, w_ref = rest[:2]
    if n_rope:
        cos_ref, sa_ref, sb_ref, o_ref = rest[2:]
    else:
        (o_ref,) = rest[2:]
    x = x_ref[...]
    if moe:
        route = route_ref[...]
        y1 = jnp.concatenate(_unpack_bf16_pairs(y1_ref[...]), axis=1)
        y2 = jnp.concatenate(_unpack_bf16_pairs(y2_ref[...]), axis=1)
        x = x + (route[:, 2:3] * y1 + route[:, 3:4] * y2)
        x_out_ref[...] = x
    xn = (x * lax.rsqrt(jnp.mean(x * x, axis=-1, keepdims=True) + RMS_EPS) * g_ref[...]).astype(BF16)
    for j in range(o_ref.shape[1] // TN):
        cols = slice(j * TN, (j + 1) * TN)
        acc = jnp.dot(xn, w_ref[:, cols], preferred_element_type=F32)
        if j < n_rope:
            reps = TN // cos_ref.shape[1]
            c = jnp.tile(cos_ref[...], (1, reps))
            sa = jnp.tile(sa_ref[...], (1, reps))
            sb = jnp.tile(sb_ref[...], (1, reps))
            nxt = pltpu.roll(acc, TN - 1, axis=1)
            prv = pltpu.roll(acc, 1, axis=1)
            acc = acc * c + nxt * sa + prv * sb
        o_ref[:, cols] = acc.astype(o_ref.dtype)


def _norm_matmul(x, g, w, rope=None, n_rope=0, seq=None, moe=None):
    t, d = x.shape
    n = w.shape[1]
    row = lambda i: (i, 0)
    in_specs = [pl.BlockSpec((TM, d), row)]
    args = [x]
    if moe is not None:
        route, yg = moe
        slot2 = t // TM
        in_specs += [pl.BlockSpec((TM, LANES), row), pl.BlockSpec((TM, d // 2), row),
                     pl.BlockSpec((TM, d // 2), lambda i: (slot2 + i, 0))]
        args += [route, yg, yg]
    in_specs += [
        pl.BlockSpec((1, d), lambda i: (0, 0)),
        pl.BlockSpec((d, n), lambda i: (0, 0), pipeline_mode=pl.Buffered(1)),
    ]
    args += [g.reshape(1, d), w]
    if rope is not None:
        nb = seq // TM
        dk = rope[0].shape[1]
        in_specs += [pl.BlockSpec((TM, dk), lambda i: (i % nb, 0))] * 3
        args += list(rope)
    out_specs = [pl.BlockSpec((TM, n), row)]
    out_shape = [jax.ShapeDtypeStruct((t, n), BF16)]
    if moe is not None:
        out_specs.append(pl.BlockSpec((TM, d), row))
        out_shape.append(jax.ShapeDtypeStruct((t, d), F32))
    out = pl.pallas_call(
        functools.partial(_norm_matmul_kernel, n_rope=n_rope, moe=moe is not None),
        grid=(t // TM,),
        in_specs=in_specs,
        out_specs=out_specs,
        out_shape=out_shape,
        compiler_params=_cparams(("parallel",)),
    )(*args)
    return out if moe is not None else out[0]


def _diff_attn_kernel(q_ref, k_ref, v_ref, bias_ref, lam_ref, g_ref, o_ref,
                      vt_ref, m_ref, acc_ref, s_ref, *, lambda_init):
    seq = q_ref.shape[0]
    hp, _, _, tq = bias_ref.shape
    dv = q_ref.shape[1] // hp
    dh = dv // 2
    nq = seq // tq

    def head(hh):
        return slice(hh * dv, (hh + 1) * dv)

    extra = vt_ref.shape[2] - dv
    ones_row = jnp.where(lax.broadcasted_iota(jnp.int32, (extra, tq), 0) == 0, 1.0, 0.0).astype(BF16)
    for hh in range(hp):
        for j in range(nq):
            vt_ref[hh, j, :dv, :] = v_ref[j * tq:(j + 1) * tq, head(hh)].astype(F32).T.astype(BF16)
            vt_ref[hh, j, dv:, :] = ones_row

    lq = lam_ref[...]
    lam = (jnp.exp(jnp.sum(lq[0:1] * lq[1:2], axis=-1, keepdims=True))
           - jnp.exp(jnp.sum(lq[2:3] * lq[3:4], axis=-1, keepdims=True)) + lambda_init)
    lane = lax.broadcasted_iota(jnp.int32, (tq, dv), 1)

    def query_maps(qi):
        qrows = pl.ds(pl.multiple_of(qi * tq, tq), tq)
        qs = []
        for hh in range(hp):
            q = q_ref[qrows, head(hh)]
            zero = jnp.zeros_like(q)
            qs.append(jnp.concatenate([jnp.where(lane < dh, q, zero), jnp.where(lane >= dh, q, zero)],
                                      axis=0))
        return qs

    def score_dots(j, qs):
        rows = pl.ds(pl.multiple_of(j * tq, tq), tq)
        for hh in range(hp):
            s_ref[hh] = lax.dot_general(k_ref[rows, head(hh)], qs[hh], (((1,), (1,)), ((), ())),
                                        preferred_element_type=F32)

    score_dots(0, query_maps(0))

    def q_body(qi, carry):
        qrows = pl.ds(pl.multiple_of(qi * tq, tq), tq)
        qs = query_maps(qi)
        qs_next = query_maps(jnp.minimum(qi + 1, nq - 1))

        m_ref[...] = jnp.full(m_ref.shape, -jnp.inf, F32)
        acc_ref[...] = jnp.zeros(acc_ref.shape, F32)

        def step(j, tile, prefetch):
            probs, alphas = [], []
            for hh in range(hp):
                s = s_ref[hh]
                if tile is not None:
                    bias = bias_ref[hh, tile]
                    s = s + jnp.concatenate([bias, bias], axis=1)
                m_old = m_ref[hh]
                m_new = jnp.maximum(m_old, jnp.max(s, axis=0, keepdims=True))
                alpha = jnp.exp2(m_old - m_new)
                m_ref[hh] = m_new
                probs.append(jnp.exp2(s - m_new).astype(BF16))
                alphas.append(alpha)
            if prefetch == "block":
                score_dots(j + 1, qs)
            else:
                score_dots(0, qs_next)
            for hh in range(hp):
                acc_ref[hh] = alphas[hh] * acc_ref[hh] + jnp.dot(vt_ref[hh, j], probs[hh],
                                                                 preferred_element_type=F32)

        def far_body(j, c):
            step(j, None, "block")
            return c

        lax.fori_loop(0, jnp.maximum(qi - 1, 0), far_body, 0)

        @pl.when(qi >= 1)
        def _():
            step(qi - 1, 1, "block")

        step(qi, 0, "tile")

        for hh in range(hp):
            acc = acc_ref[hh]
            on = acc[:dv] / acc[dv:dv + 1]
            ot = on[:, :tq] - lam * on[:, tq:]
            ot = ot * lax.rsqrt(jnp.mean(ot * ot, axis=0, keepdims=True) + SUBLN_EPS)
            o_ref[qrows, head(hh)] = (ot.T * g_ref[...] * (1.0 - lambda_init)).astype(o_ref.dtype)
        return carry

    lax.fori_loop(0, nq, q_body, 0)


def _t5_bucket(rel):
    max_exact = NUM_BUCKETS // 2
    n = jnp.maximum(rel, 0)
    is_small = n < max_exact
    nf = jnp.maximum(n, max_exact).astype(F32)
    large = max_exact + (jnp.log(nf / max_exact) / math.log(MAX_DISTANCE / max_exact)
                         * (NUM_BUCKETS - max_exact)).astype(jnp.int32)
    large = jnp.minimum(large, NUM_BUCKETS - 1)
    return jnp.where(is_small, n, large)


def _bias_tiles(rel_table):
    heads = rel_table.shape[1]
    span = 2 * TQ + 1
    bucket = _t5_bucket(jnp.arange(span))
    onehot = (bucket[:, None] == jnp.arange(NUM_BUCKETS)[None, :]).astype(F32)
    table = (rel_table.astype(F32) - rel_table[NUM_BUCKETS - 1].astype(F32)[None, :]) * LOG2E
    by_rel = jnp.dot(onehot, table, precision=lax.Precision.HIGHEST).T
    toep = jnp.tile(by_rel, (1, TQ))[:, :TQ * (span - 1)].reshape(heads, TQ, span - 1)
    kj = jnp.arange(TQ)[:, None]
    qi = jnp.arange(TQ)[None, :]
    diag = jnp.where((qi >= kj)[None], toep[:, :, :TQ], -jnp.inf)
    return jnp.stack([diag, toep[:, :, TQ:]], axis=1)


def _diff_attention(qkv, rel_table, lam_qk, subln_g, lambda_init, batch, seq, heads):
    t = qkv.shape[0]
    d = qkv.shape[1] // 3
    dv = d // heads
    hp = ATTN_HEADS
    assert TQ + 1 >= MAX_DISTANCE and seq % TQ == 0 and dv == LANES and heads % hp == 0
    nq = seq // TQ
    groups = heads // hp
    bias = _bias_tiles(rel_table)
    return pl.pallas_call(
        functools.partial(_diff_attn_kernel, lambda_init=lambda_init),
        grid=(batch, groups),
        in_specs=[
            pl.BlockSpec((seq, hp * dv), lambda b, g: (b, g)),
            pl.BlockSpec((seq, hp * dv), lambda b, g: (b, groups + g)),
            pl.BlockSpec((seq, hp * dv), lambda b, g: (b, 2 * groups + g)),
            pl.BlockSpec((hp, 2, TQ, TQ), lambda b, g: (g, 0, 0, 0)),
            pl.BlockSpec(lam_qk.shape, lambda b, g: (0, 0)),
            pl.BlockSpec((1, dv), lambda b, g: (0, 0)),
        ],
        out_specs=pl.BlockSpec((seq, hp * dv), lambda b, g: (b, g)),
        out_shape=jax.ShapeDtypeStruct((t, d), BF16),
        scratch_shapes=[
            pltpu.VMEM((hp, nq, dv + BF16_SUBLANES, TQ), BF16),
            pltpu.VMEM((hp, 1, 2 * TQ), F32),
            pltpu.VMEM((hp, dv + BF16_SUBLANES, 2 * TQ), F32),
            pltpu.VMEM((hp, TQ, 2 * TQ), F32),
        ],
        compiler_params=_cparams(("parallel", "parallel")),
    )(qkv, qkv, qkv, bias, lam_qk.astype(F32), subln_g.reshape(1, dv).astype(F32))


def _retention_kernel(q_ref, k_ref, v_ref, dec_ref, tab_ref, o_ref, state_ref):
    seq = q_ref.shape[0]
    hp, c, _ = dec_ref.shape
    dk = q_ref.shape[1] // hp
    dv = v_ref.shape[1] // hp
    heads = range(hp)
    state_ref[...] = jnp.zeros(state_ref.shape, F32)

    def body(n, carry):
        rows = pl.ds(pl.multiple_of(n * c, c), c)
        q = [q_ref[rows, hh * dk:(hh + 1) * dk] for hh in heads]
        k = [k_ref[rows, hh * dk:(hh + 1) * dk] for hh in heads]
        v = [v_ref[rows, hh * dv:(hh + 1) * dv] for hh in heads]
        tab = [tab_ref[hh] for hh in heads]
        st = [state_ref[hh] for hh in heads]
        s = [lax.dot_general(q[hh], k[hh], (((1,), (1,)), ((), ())), preferred_element_type=F32)
             for hh in heads]
        cross = [jnp.dot(q[hh], st[hh].astype(BF16), preferred_element_type=F32) for hh in heads]
        kt = [(k[hh].astype(F32) * tab[hh][:, 1:2]).T.astype(BF16) for hh in heads]
        inner = [jnp.dot((s[hh] * dec_ref[hh]).astype(BF16), v[hh], preferred_element_type=F32)
                 for hh in heads]
        for hh in heads:
            state_ref[hh] = st[hh] * tab[hh][0:1, 2:3] + jnp.dot(kt[hh], v[hh], preferred_element_type=F32)
        for hh in heads:
            o = inner[hh] + cross[hh] * tab[hh][:, 0:1]
            o = o * lax.rsqrt(jnp.mean(o * o, axis=-1, keepdims=True) + RMS_EPS)
            o_ref[rows, hh * dv:(hh + 1) * dv] = o.astype(o_ref.dtype)
        return carry

    lax.fori_loop(0, seq // c, body, 0)


def _retention_tables(heads):
    c = RET_C
    log_gamma = jnp.log(1.0 - 2.0 ** (-5.0 - jnp.arange(heads, dtype=F32)))
    pos = jnp.arange(c, dtype=F32)
    dist = pos[:, None] - pos[None, :]
    inner_decay = jnp.where(dist[None] >= 0,
                            jnp.exp(jnp.maximum(dist, 0.0)[None] * log_gamma[:, None, None]), 0.0)
    q_decay = jnp.exp((pos + 1.0)[None] * log_gamma[:, None])
    k_decay = jnp.exp((c - 1.0 - pos)[None] * log_gamma[:, None])
    chunk_decay = jnp.broadcast_to(jnp.exp(c * log_gamma)[:, None], (heads, c))
    tab = jnp.stack([q_decay, k_decay, chunk_decay], axis=-1)
    tab = jnp.pad(tab, ((0, 0), (0, 0), (0, LANES - 3)))
    return inner_decay, tab


def _retention(qkvg, batch, seq, qk_dim, v_dim):
    t = qkvg.shape[0]
    dk = qk_dim // RET_HEADS
    dvh = v_dim // RET_HEADS
    inner_decay, tab = _retention_tables(RET_HEADS)
    hp = RET_HP
    assert RET_HEADS % hp == 0
    groups = RET_HEADS // hp
    kb = qk_dim // (hp * dk)
    vb = 2 * qk_dim // (hp * dvh)
    return pl.pallas_call(
        _retention_kernel,
        grid=(batch, groups),
        in_specs=[
            pl.BlockSpec((seq, hp * dk), lambda b, g: (b, g)),
            pl.BlockSpec((seq, hp * dk), lambda b, g: (b, kb + g)),
            pl.BlockSpec((seq, hp * dvh), lambda b, g: (b, vb + g)),
            pl.BlockSpec((hp, RET_C, RET_C), lambda b, g: (g, 0, 0)),
            pl.BlockSpec((hp, RET_C, LANES), lambda b, g: (g, 0, 0)),
        ],
        out_specs=pl.BlockSpec((seq, hp * dvh), lambda b, g: (b, g)),
        out_shape=jax.ShapeDtypeStruct((t, v_dim), BF16),
        scratch_shapes=[pltpu.VMEM((hp, dk, dvh), F32)],
        compiler_params=_cparams(("parallel", "parallel")),
    )(qkvg, qkvg, qkvg, inner_decay, tab)


def _out_router_kernel(*refs, gated, n_groups, n_experts):
    if gated:
        a_ref, gate_ref, w_ref, x_ref, nf_ref, wr_ref, br_ref = refs[:7]
        x1_ref, hn_ref, route_ref, route_t_ref, cnt_ref, carry_ref = refs[7:]
    else:
        a_ref, w_ref, x_ref, nf_ref, wr_ref, br_ref = refs[:6]
        x1_ref, hn_ref, route_ref, route_t_ref, cnt_ref, carry_ref = refs[6:]
    epg = n_experts // n_groups
    tm = x_ref.shape[0] // ROUTER_SPLIT

    @pl.when(pl.program_id(0) == 0)
    def _():
        carry_ref[...] = jnp.zeros(carry_ref.shape, F32)

    x1_parts = []
    for part in range(ROUTER_SPLIT):
        rows = slice(part * tm, (part + 1) * tm)
        a = a_ref[rows, :]
        if gated:
            gate = gate_ref[rows, :]
            a = gate * jax.nn.sigmoid(gate) * a
        x1_parts.append(x_ref[rows, :] + jnp.dot(a, w_ref[...], preferred_element_type=F32))
    for part in range(ROUTER_SPLIT):
        rows = slice(part * tm, (part + 1) * tm)
        _route_rows(x1_parts[part], rows, nf_ref, wr_ref, br_ref, x1_ref, hn_ref, route_ref, route_t_ref,
                    cnt_ref, carry_ref, n_groups=n_groups, n_experts=n_experts, epg=epg)


def _route_rows(x1, rows, nf_ref, wr_ref, br_ref, x1_ref, hn_ref, route_ref, route_t_ref, cnt_ref,
                carry_ref, *, n_groups, n_experts, epg):
    tm = x1.shape[0]
    x1_ref[rows, :] = x1
    hn = x1 * lax.rsqrt(jnp.mean(x1 * x1, axis=-1, keepdims=True) + RMS_EPS) * nf_ref[...]
    hn_ref[rows, :] = _pack_bf16_pairs(hn)

    hn_hi = hn.astype(BF16)
    hn_lo = (hn - hn_hi.astype(F32)).astype(BF16)
    logits = (jnp.dot(hn_hi, wr_ref[0], preferred_element_type=F32)
              + jnp.dot(hn_lo, wr_ref[0], preferred_element_type=F32)
              + jnp.dot(hn_hi, wr_ref[1], preferred_element_type=F32)) + br_ref[...]

    n_rows = -(-(n_groups + n_experts) // SUBLANES) * SUBLANES
    lt = logits.T[:n_rows, :]
    row = lax.broadcasted_iota(jnp.int32, lt.shape, 0)
    big = jnp.int32(LANES)
    neg = jnp.float32(-jnp.inf)

    gl = jnp.where(row < n_groups, lt, neg)
    gmax = jnp.max(gl, axis=0, keepdims=True)
    g_p = 1.0 / jnp.sum(jnp.exp(gl - gmax), axis=0, keepdims=True)
    g_idx = jnp.min(jnp.where(gl == gmax, row, big), axis=0, keepdims=True)

    lo = n_groups + g_idx * epg
    el = jnp.where((row >= lo) & (row < lo + epg), lt, neg)
    v1 = jnp.max(el, axis=0, keepdims=True)
    i1 = jnp.min(jnp.where(el == v1, row, big), axis=0, keepdims=True)
    el2 = jnp.where(row == i1, neg, el)
    v2 = jnp.max(el2, axis=0, keepdims=True)
    i2 = jnp.min(jnp.where(el2 == v2, row, big), axis=0, keepdims=True)
    tt = jnp.exp(v2 - v1)
    w1 = g_p / (1.0 + tt)
    w2 = g_p * tt / (1.0 + tt)
    e1 = i1 - n_groups
    e2 = i2 - n_groups

    erow = lax.broadcasted_iota(jnp.int32, (n_experts, tm), 0)
    oh1 = erow == e1
    oh2 = erow == e2
    onehot = jnp.where(oh1 | oh2, 1.0, 0.0)
    before = lax.broadcasted_iota(jnp.int32, (tm, tm), 0) < lax.broadcasted_iota(jnp.int32, (tm, tm), 1)
    tri = jnp.where(before, 1.0, 0.0).astype(BF16)
    carry = carry_ref[...]
    base = jnp.dot(onehot.astype(BF16), tri, preferred_element_type=F32) + carry[:, 0:1]
    rank1 = jnp.sum(jnp.where(oh1, base, 0.0), axis=0, keepdims=True)
    rank2 = jnp.sum(jnp.where(oh2, base, 0.0), axis=0, keepdims=True)
    carry = carry + jnp.sum(onehot, axis=1, keepdims=True)
    carry_ref[...] = carry
    cnt_ref[...] = carry

    frow = lax.broadcasted_iota(jnp.int32, (LANES, tm), 0)
    out = jnp.zeros((LANES, tm), F32)
    for idx, val in enumerate((e1.astype(F32), e2.astype(F32), w1, w2, rank1, rank2)):
        out = jnp.where(frow == idx, val, out)
    route_t_ref[:, rows] = out[:route_t_ref.shape[0], :]
    route_ref[rows, :] = out.T


def _out_router(a, gate, w, x, nf, w_group, b_group, w_expert, b_expert):
    t, d = x.shape
    k = w.shape[0]
    n_groups = w_group.shape[1]
    n_experts = w_expert.shape[1]
    assert n_groups + n_experts <= LANES
    pad = LANES - n_groups - n_experts
    wr = jnp.pad(jnp.concatenate([w_group, w_expert], axis=1).astype(F32), ((0, 0), (0, pad)))
    wr_hi = wr.astype(BF16)
    wr = jnp.stack([wr_hi, (wr - wr_hi.astype(F32)).astype(BF16)])
    br = jnp.pad(jnp.concatenate([b_group, b_expert]).astype(F32), (0, pad)).reshape(1, LANES)
    gated = gate is not None
    row = lambda i: (i, 0)
    fixed = lambda i: (0, 0)
    in_specs = [pl.BlockSpec((TM, k), row)]
    args = [a]
    if gated:
        gate_arr, gate_col = gate
        in_specs.append(pl.BlockSpec((TM, k), lambda i: (i, gate_col)))
        args.append(gate_arr)
    in_specs += [
        pl.BlockSpec((k, d), fixed),
        pl.BlockSpec((TM, d), row),
        pl.BlockSpec((1, d), fixed),
        pl.BlockSpec((2, d, LANES), lambda i: (0, 0, 0)),
        pl.BlockSpec((1, LANES), fixed),
    ]
    args += [w, x, nf.reshape(1, d), wr, br]
    return pl.pallas_call(
        functools.partial(_out_router_kernel, gated=gated, n_groups=n_groups, n_experts=n_experts),
        grid=(t // TM,),
        in_specs=in_specs,
        out_specs=[
            pl.BlockSpec((TM, d), row),
            pl.BlockSpec((TM, d // 2), row),
            pl.BlockSpec((TM, LANES), row),
            pl.BlockSpec((SUBLANES, TM), lambda i: (0, i)),
            pl.BlockSpec((n_experts, LANES), fixed),
        ],
        out_shape=[
            jax.ShapeDtypeStruct((t, d), F32),
            jax.ShapeDtypeStruct((t, d // 2), jnp.uint32),
            jax.ShapeDtypeStruct((t, LANES), F32),
            jax.ShapeDtypeStruct((SUBLANES, t), F32),
            jax.ShapeDtypeStruct((n_experts, LANES), F32),
        ],
        scratch_shapes=[pltpu.VMEM((n_experts, LANES), F32)],
        compiler_params=_cparams(("arbitrary",)),
    )(*args)


def _sc_mesh():
    return plsc.VectorSubcoreMesh(core_axis_name="c", subcore_axis_name="s",
                                  num_cores=SC_CORES, num_subcores=SC_SUBCORES)


def _sc_worker():
    return lax.axis_index("s") * SC_CORES + lax.axis_index("c")


def _sc_scatter_rows(src, d1, d2, n_rows):
    t, d = src.shape
    workers = SC_CORES * SC_SUBCORES
    per_w = t // workers
    n_ch = per_w // SC_CHUNK
    assert t % (workers * SC_CHUNK) == 0
    idx = jnp.stack([d1, d2]).reshape(2, workers, n_ch, SC_CHUNK)

    @functools.partial(
        pl.kernel, mesh=_sc_mesh(),
        out_type=jax.ShapeDtypeStruct((n_rows, d), src.dtype),
        scratch_types=[pltpu.VMEM((2, n_ch, SC_CHUNK), jnp.int32),
                       pltpu.VMEM((SC_CHUNK, d), src.dtype),
                       pltpu.VMEM((SC_CHUNK, d), src.dtype),
                       pltpu.SemaphoreType.DMA, pltpu.SemaphoreType.DMA],
    )
    def scatter(src_hbm, idx_hbm, out_hbm, idx_v, rows_a, rows_b, sem_a, sem_b):
        wid = _sc_worker()
        base = wid * per_w
        for s in range(2):
            pltpu.sync_copy(idx_hbm.at[s, wid], idx_v.at[s])

        def move(c, rows_v, sem):
            pltpu.sync_copy(src_hbm.at[pl.ds(base + c * SC_CHUNK, SC_CHUNK)], rows_v)
            return [pltpu.async_copy(rows_v, out_hbm.at[idx_v.at[s, c]], sem) for s in range(2)]

        @pl.loop(0, n_ch, step=2)
        def _(c):
            first = move(c, rows_a, sem_a)
            second = move(c + 1, rows_b, sem_b)
            for cp in first + second:
                cp.wait()

    return scatter(src, idx)


def _sc_gather_rows(table, idx):
    b = idx.shape[0]
    d = table.shape[1]
    workers = SC_CORES * SC_SUBCORES
    per_w = b // workers
    n_ch = per_w // SC_CHUNK
    assert b % (workers * SC_CHUNK * 2) == 0

    @functools.partial(
        pl.kernel, mesh=_sc_mesh(),
        out_type=jax.ShapeDtypeStruct((b, d), table.dtype),
        scratch_types=[pltpu.VMEM((per_w,), jnp.int32),
                       pltpu.VMEM((SC_CHUNK, d), table.dtype),
                       pltpu.VMEM((SC_CHUNK, d), table.dtype),
                       pltpu.SemaphoreType.DMA, pltpu.SemaphoreType.DMA],
    )
    def gather(table_hbm, idx_hbm, out_hbm, idx_v, rows_a, rows_b, sem_a, sem_b):
        base = _sc_worker() * per_w
        pltpu.sync_copy(idx_hbm.at[pl.ds(base, per_w)], idx_v)

        def fetch(c, rows_v, sem):
            return pltpu.async_copy(table_hbm.at[idx_v.at[pl.ds(c * SC_CHUNK, SC_CHUNK)]], rows_v, sem)

        def store(c, rows_v):
            pltpu.sync_copy(rows_v, out_hbm.at[pl.ds(base + c * SC_CHUNK, SC_CHUNK)])

        @pl.loop(0, n_ch, step=2)
        def _(c):
            first = fetch(c, rows_a, sem_a)
            second = fetch(c + 1, rows_b, sem_b)
            first.wait()
            store(c, rows_a)
            second.wait()
            store(c + 1, rows_b)

    return gather(table, idx)


def _expert_kernel(fb_ref, nblk_ref, nv_ref, nu_ref, xs_ref, *rest):
    w_refs = rest[:3 * WEIGHT_PARTS]
    ys_ref, wgb_ref, wub_ref, wdb_ref, xbuf, ybuf, xsem, ysem = rest[3 * WEIGHT_PARTS:]
    e = pl.program_id(0)
    rb = xbuf.shape[1]
    nb = ys_ref.shape[0] // rb
    n_used = nu_ref[0]

    def x_copy(g, slot):
        rows = pl.ds(pl.multiple_of(g * rb, rb), rb)
        return pltpu.make_async_copy(xs_ref.at[rows], xbuf.at[slot], xsem.at[slot])

    def y_copy(g, slot):
        rows = pl.ds(pl.multiple_of(g * rb, rb), rb)
        return pltpu.make_async_copy(ybuf.at[slot], ys_ref.at[rows], ysem.at[slot])

    @pl.when(e == 0)
    def _():
        x_copy(0, 0).start()

    @pl.when(nblk_ref[e] > 0)
    def _():
        for i, dst in enumerate((wgb_ref, wub_ref, wdb_ref)):
            slab = dst.shape[0] // WEIGHT_PARTS
            for p in range(WEIGHT_PARTS):
                dst[p * slab:(p + 1) * slab, :] = w_refs[i * WEIGHT_PARTS + p][0, 0].astype(BF16)

    def block(b, carry):
        g = fb_ref[e] + b
        slot = lax.rem(g, 2)
        x_copy(g, slot).wait()

        @pl.when(g + 1 < n_used)
        def _():
            x_copy(g + 1, 1 - slot).start()

        @pl.when(g >= 2)
        def _():
            y_copy(g - 2, slot).wait()

        valid = lax.broadcasted_iota(jnp.int32, (rb, 1), 0) < nv_ref[g]
        x_hi, x_lo = _unpack_bf16_pairs(jnp.where(valid, xbuf[slot], jnp.uint32(0)))
        x_hi, x_lo = x_hi.astype(BF16), x_lo.astype(BF16)
        h = x_hi.shape[1]

        def in_dot(w_ref):
            return (jnp.dot(x_hi, w_ref[:h, :], preferred_element_type=F32)
                    + jnp.dot(x_lo, w_ref[h:, :], preferred_element_type=F32))

        gate = in_dot(wgb_ref)
        up = in_dot(wub_ref)
        hmid = (gate * jax.nn.sigmoid(gate) * up).astype(BF16)
        ybuf[slot] = _pack_bf16_pairs(jnp.dot(hmid, wdb_ref[...], preferred_element_type=F32))
        y_copy(g, slot).start()
        return carry

    lax.fori_loop(0, nblk_ref[e], block, 0)

    @pl.when(e == pl.num_programs(0) - 1)
    def _():
        @pl.when(n_used >= 2)
        def _():
            y_copy(n_used - 2, lax.rem(n_used, 2)).wait()

        y_copy(n_used - 1, lax.rem(n_used - 1, 2)).wait()
        ybuf[0] = jnp.zeros(ybuf.shape[1:], ybuf.dtype)

        def tail_start(g, carry):
            y_copy(g, 0).start()
            return carry

        def tail_wait(g, carry):
            y_copy(g, 0).wait()
            return carry

        lax.fori_loop(n_used, nb, tail_start, 0)
        lax.fori_loop(n_used, nb, tail_wait, 0)


def _experts(xs, first_block, n_blocks, n_valid, n_used, w_gate, w_up, w_down, layer):
    n_rows, dp = xs.shape
    d = 2 * dp
    n_experts, ff = w_gate.shape[1], w_gate.shape[3]
    parts = range(WEIGHT_PARTS)
    w_specs, w_args = [], []
    for w in (w_gate, w_up, w_down):
        k, n = w.shape[2:]
        w_specs += [pl.BlockSpec((1, 1, k // WEIGHT_PARTS, n), lambda e, *_, p=p: (layer, e, p, 0)) for p in parts]
        w_args += [w] * WEIGHT_PARTS
    grid_spec = pltpu.PrefetchScalarGridSpec(
        num_scalar_prefetch=4,
        grid=(n_experts,),
        in_specs=[pl.BlockSpec(memory_space=pl.ANY)] + w_specs,
        out_specs=pl.BlockSpec(memory_space=pl.ANY),
        scratch_shapes=[pltpu.VMEM((d, ff), BF16), pltpu.VMEM((d, ff), BF16), pltpu.VMEM((ff, d), BF16),
                        pltpu.VMEM((2, ROW_BLOCK, dp), xs.dtype), pltpu.VMEM((2, ROW_BLOCK, dp), xs.dtype),
                        pltpu.SemaphoreType.DMA((2,)), pltpu.SemaphoreType.DMA((2,))],
    )
    return pl.pallas_call(
        _expert_kernel,
        grid_spec=grid_spec,
        out_shape=jax.ShapeDtypeStruct((n_rows, dp), xs.dtype),
        compiler_params=_cparams(("arbitrary",)),
    )(first_block, n_blocks, n_valid, n_used, xs, *w_args)


def _final_combine_kernel(x_ref, route_ref, y1_ref, y2_ref, nf_ref, o_ref):
    route = route_ref[...]
    y1 = jnp.concatenate(_unpack_bf16_pairs(y1_ref[...]), axis=1)
    y2 = jnp.concatenate(_unpack_bf16_pairs(y2_ref[...]), axis=1)
    y = x_ref[...] + (route[:, 2:3] * y1 + route[:, 3:4] * y2)
    o_ref[...] = y * lax.rsqrt(jnp.mean(y * y, axis=-1, keepdims=True) + RMS_EPS) * nf_ref[...]


def _final_combine(x1, route, yg, nf):
    t, d = x1.shape
    slot2 = t // TM
    row = lambda i: (i, 0)
    return pl.pallas_call(
        _final_combine_kernel,
        grid=(t // TM,),
        in_specs=[
            pl.BlockSpec((TM, d), row),
            pl.BlockSpec((TM, LANES), row),
            pl.BlockSpec((TM, d // 2), row),
            pl.BlockSpec((TM, d // 2), lambda i: (slot2 + i, 0)),
            pl.BlockSpec((1, d), lambda i: (0, 0)),
        ],
        out_specs=pl.BlockSpec((TM, d), row),
        out_shape=jax.ShapeDtypeStruct((t, d), F32),
        compiler_params=_cparams(("parallel",)),
    )(x1, route, yg, yg, nf.reshape(1, d).astype(F32))


def _moe(hn, route_t, cnt, w_gate, w_up, w_down, layer):
    t, d = hn.shape
    n_experts = w_gate.shape[1]
    ids = jnp.arange(n_experts, dtype=jnp.int32)
    e1, e2, rank1, rank2 = (route_t[f].astype(jnp.int32) for f in (0, 1, 4, 5))
    counts = cnt[:, 0].astype(jnp.int32)
    padded = ((counts + ROW_BLOCK - 1) // ROW_BLOCK) * ROW_BLOCK
    pad_end = jnp.cumsum(padded)
    pad_start = pad_end - padded

    def dest(e, rank):
        return jnp.sum(jnp.where(e[:, None] == ids[None, :], pad_start[None, :], 0), axis=1) + rank

    d1 = dest(e1, rank1)
    d2 = dest(e2, rank2)
    nb = (t * TOP_K_INNER) // ROW_BLOCK + n_experts
    n_used = (pad_end[-1:] // ROW_BLOCK).astype(jnp.int32)
    first_row = jnp.arange(nb, dtype=jnp.int32) * ROW_BLOCK
    block_e = jnp.sum((pad_end[None, :] <= first_row[:, None]).astype(jnp.int32), axis=1)
    block_e = jnp.minimum(block_e, n_experts - 1)
    row_end = jnp.sum(jnp.where(block_e[:, None] == ids[None, :], (pad_start + counts)[None, :], 0), axis=1)
    n_valid = jnp.clip(row_end - first_row, 0, ROW_BLOCK).astype(jnp.int32)
    xs = _sc_scatter_rows(hn, d1, d2, nb * ROW_BLOCK)
    ys = _experts(xs, (pad_start // ROW_BLOCK).astype(jnp.int32), (padded // ROW_BLOCK).astype(jnp.int32),
                  n_valid, n_used, w_gate, w_up, w_down, layer)
    return _sc_gather_rows(ys, jnp.concatenate([d1, d2]))


def _diff_lambda_init(layer_idx):
    return 0.8 - 0.6 * math.exp(-0.3 * layer_idx)


def _rope_tables(seq, dk):
    angle = 1.0 / (10000.0 ** jnp.linspace(0.0, 1.0, dk // 2, dtype=F32))
    angle = jnp.repeat(angle, 2)
    phase = jnp.arange(seq, dtype=F32)[:, None] * angle[None, :]
    sin, cos = jnp.sin(phase), jnp.cos(phase)
    even = (jnp.arange(dk) % 2 == 0)[None, :]
    return cos, jnp.where(even, -sin, 0.0), jnp.where(even, 0.0, sin)


def kernel(x, rel_bias_table, norm_mix, norm_ffn, norm_final, diff_w_in, diff_lambda_qk, diff_subln,
           diff_w_out, ret_w_in, ret_w_out, moe_w_group, moe_b_group, moe_w_expert, moe_b_expert,
           moe_w_gate, moe_w_up, moe_w_down):
    batch, seq, d = x.shape
    depth = norm_mix.shape[0]
    t = batch * seq
    heads = rel_bias_table.shape[1]
    dh = diff_lambda_qk.shape[-1]
    qk_dim = (ret_w_in.shape[2] - 2 * ret_w_out.shape[1]) // 2
    v_dim = ret_w_out.shape[1]
    dk = qk_dim // RET_HEADS
    xt = x.reshape(t, d)
    moe = None
    for i in range(depth):
        j = i // N_MIXERS
        if i % N_MIXERS == 0:
            col_scale = jnp.concatenate([jnp.full((d,), dh ** -0.5 * LOG2E, F32), jnp.ones((2 * d,), F32)])
            w_in = (diff_w_in[j] * col_scale[None, :]).astype(BF16)
            qkv = _norm_matmul(xt, norm_mix[i], w_in, moe=moe)
            if moe is not None:
                qkv, xt = qkv
            mix = _diff_attention(qkv, rel_bias_table, diff_lambda_qk[j], diff_subln[j],
                                  _diff_lambda_init(i), batch, seq, heads)
            gate = None
            w_out = diff_w_out[j].astype(BF16)
        else:
            n_in = ret_w_in.shape[2]
            col_scale = jnp.concatenate([jnp.ones((qk_dim,), F32), jnp.full((qk_dim,), dk ** -0.5, F32),
                                         jnp.ones((n_in - 2 * qk_dim,), F32)])
            w_in = (ret_w_in[j] * col_scale[None, :]).astype(BF16)
            qkvg = _norm_matmul(xt, norm_mix[i], w_in, rope=_rope_tables(seq, dk),
                                n_rope=2 * qk_dim // TN, seq=seq, moe=moe)
            if moe is not None:
                qkvg, xt = qkvg
            mix = _retention(qkvg, batch, seq, qk_dim, v_dim)
            gate = (qkvg, (2 * qk_dim + v_dim) // v_dim)
            w_out = ret_w_out[j].astype(BF16)
        x1, hn, route, route_t, cnt = _out_router(mix, gate, w_out, xt, norm_ffn[i], moe_w_group[i],
                                                  moe_b_group[i], moe_w_expert[i], moe_b_expert[i])
        xt = x1
        moe = (route, _moe(hn, route_t, cnt, moe_w_gate, moe_w_up, moe_w_down, i))
    return _final_combine(xt, moe[0], moe[1], norm_final).reshape(batch, seq, d)
```

```python
import functools
import math

import jax
import jax.numpy as jnp
from jax import lax
from jax.experimental import pallas as pl
from jax.experimental.pallas import tpu as pltpu
from jax.experimental.pallas import tpu_sc as plsc

F32 = jnp.float32
BF16 = jnp.bfloat16

N_MIXERS = 2
NUM_BUCKETS = 32
MAX_DISTANCE = 128
RET_HEADS = 4
TOP_K_INNER = 2
RMS_EPS = 1e-6
SUBLN_EPS = 1e-5

LANES = 128
SUBLANES = 8
BF16_SUBLANES = 16
LOG2E = 1.4426950408889634
VMEM_LIMIT = 56 * 1024 * 1024
SC_CORES = 2
SC_SUBCORES = 16

TM = 512
TN = 1024
TQ = 256
ATTN_HEADS = 4
RET_C = 256
RET_HP = 2
ROUTER_SPLIT = 2
ROW_BLOCK = 512
SC_CHUNK = 64


def _cparams(sem):
    return pltpu.CompilerParams(dimension_semantics=sem, vmem_limit_bytes=VMEM_LIMIT)


def _pack_bf16_pairs(x):
    h = x.shape[1] // 2
    hi = lax.bitcast_convert_type(x[:, :h].astype(BF16).astype(F32), jnp.uint32)
    lo = lax.bitcast_convert_type(x[:, h:].astype(BF16).astype(F32), jnp.uint32)
    return hi | (lo >> 16)


def _unpack_bf16_pairs(w):
    hi = lax.bitcast_convert_type(w & jnp.uint32(0xFFFF0000), F32)
    lo = lax.bitcast_convert_type(w << 16, F32)
    return hi, lo


def _norm_matmul_kernel(x_ref, *rest, n_rope, moe):
    rest = list(rest)
    if moe:
        route_ref, y1_ref, y2_ref = rest[:3]
        x_out_ref = rest.pop()
        rest = rest[3:]
    g_ref, w_ref = rest[:2]
    if n_rope:
        cos_ref, sa_ref, sb_ref, o_ref = rest[2:]
    else:
        (o_ref,) = rest[2:]
    x = x_ref[...]
    if moe:
        route = route_ref[...]
        y1 = jnp.concatenate(_unpack_bf16_pairs(y1_ref[...]), axis=1)
        y2 = jnp.concatenate(_unpack_bf16_pairs(y2_ref[...]), axis=1)
        x = x + (route[:, 2:3] * y1 + route[:, 3:4] * y2)
        x_out_ref[...] = x
    xn = (x * lax.rsqrt(jnp.mean(x * x, axis=-1, keepdims=True) + RMS_EPS) * g_ref[...]).astype(BF16)
    for j in range(o_ref.shape[1] // TN):
        cols = slice(j * TN, (j + 1) * TN)
        acc = jnp.dot(xn, w_ref[:, cols], preferred_element_type=F32)
        if j < n_rope:
            reps = TN // cos_ref.shape[1]
            c = jnp.tile(cos_ref[...], (1, reps))
            sa = jnp.tile(sa_ref[...], (1, reps))
            sb = jnp.tile(sb_ref[...], (1, reps))
            nxt = pltpu.roll(acc, TN - 1, axis=1)
            prv = pltpu.roll(acc, 1, axis=1)
            acc = acc * c + nxt * sa + prv * sb
        o_ref[:, cols] = acc.astype(o_ref.dtype)


def _norm_matmul(x, g, w, rope=None, n_rope=0, seq=None, moe=None):
    t, d = x.shape
    n = w.shape[1]
    row = lambda i: (i, 0)
    in_specs = [pl.BlockSpec((TM, d), row)]
    args = [x]
    if moe is not None:
        route, yg = moe
        slot2 = t // TM
        in_specs += [pl.BlockSpec((TM, LANES), row), pl.BlockSpec((TM, d // 2), row),
                     pl.BlockSpec((TM, d // 2), lambda i: (slot2 + i, 0))]
        args += [route, yg, yg]
    in_specs += [
        pl.BlockSpec((1, d), lambda i: (0, 0)),
        pl.BlockSpec((d, n), lambda i: (0, 0), pipeline_mode=pl.Buffered(1)),
    ]
    args += [g.reshape(1, d), w]
    if rope is not None:
        nb = seq // TM
        dk = rope[0].shape[1]
        in_specs += [pl.BlockSpec((TM, dk), lambda i: (i % nb, 0))] * 3
        args += list(rope)
    out_specs = [pl.BlockSpec((TM, n), row)]
    out_shape = [jax.ShapeDtypeStruct((t, n), BF16)]
    if moe is not None:
        out_specs.append(pl.BlockSpec((TM, d), row))
        out_shape.append(jax.ShapeDtypeStruct((t, d), F32))
    out = pl.pallas_call(
        functools.partial(_norm_matmul_kernel, n_rope=n_rope, moe=moe is not None),
        grid=(t // TM,),
        in_specs=in_specs,
        out_specs=out_specs,
        out_shape=out_shape,
        compiler_params=_cparams(("parallel",)),
    )(*args)
    return out if moe is not None else out[0]


def _diff_attn_kernel(q_ref, k_ref, v_ref, bias_ref, lam_ref, g_ref, o_ref,
                      vt_ref, m_ref, acc_ref, s_ref, *, lambda_init):
    seq = q_ref.shape[0]
    hp, _, _, tq = bias_ref.shape
    dv = q_ref.shape[1] // hp
    dh = dv // 2
    nq = seq // tq

    def head(hh):
        return slice(hh * dv, (hh + 1) * dv)

    extra = vt_ref.shape[2] - dv
    ones_row = jnp.where(lax.broadcasted_iota(jnp.int32, (extra, tq), 0) == 0, 1.0, 0.0).astype(BF16)
    for hh in range(hp):
        for j in range(nq):
            vt_ref[hh, j, :dv, :] = v_ref[j * tq:(j + 1) * tq, head(hh)].astype(F32).T.astype(BF16)
            vt_ref[hh, j, dv:, :] = ones_row

    lq = lam_ref[...]
    lam = (jnp.exp(jnp.sum(lq[0:1] * lq[1:2], axis=-1, keepdims=True))
           - jnp.exp(jnp.sum(lq[2:3] * lq[3:4], axis=-1, keepdims=True)) + lambda_init)
    lane = lax.broadcasted_iota(jnp.int32, (tq, dv), 1)

    def query_maps(qi):
        qrows = pl.ds(pl.multiple_of(qi * tq, tq), tq)
        qs = []
        for hh in range(hp):
            q = q_ref[qrows, head(hh)]
            zero = jnp.zeros_like(q)
            qs.append(jnp.concatenate([jnp.where(lane < dh, q, zero), jnp.where(lane >= dh, q, zero)],
                                      axis=0))
        return qs

    def score_dots(j, qs):
        rows = pl.ds(pl.multiple_of(j * tq, tq), tq)
        for hh in range(hp):
            s_ref[hh] = lax.dot_general(k_ref[rows, head(hh)], qs[hh], (((1,), (1,)), ((), ())),
                                        preferred_element_type=F32)

    score_dots(0, query_maps(0))

    def q_body(qi, carry):
        qrows = pl.ds(pl.multiple_of(qi * tq, tq), tq)
        qs = query_maps(qi)
        qs_next = query_maps(jnp.minimum(qi + 1, nq - 1))

        m_ref[...] = jnp.full(m_ref.shape, -jnp.inf, F32)
        acc_ref[...] = jnp.zeros(acc_ref.shape, F32)

        def step(j, tile, prefetch):
            probs, alphas = [], []
            for hh in range(hp):
                s = s_ref[hh]
                if tile is not None:
                    bias = bias_ref[hh, tile]
                    s = s + jnp.concatenate([bias, bias], axis=1)
                m_old = m_ref[hh]
                m_new = jnp.maximum(m_old, jnp.max(s, axis=0, keepdims=True))
                alpha = jnp.exp2(m_old - m_new)
                m_ref[hh] = m_new
                probs.append(jnp.exp2(s - m_new).astype(BF16))
                alphas.append(alpha)
            if prefetch == "block":
                score_dots(j + 1, qs)
            else:
                score_dots(0, qs_next)
            for hh in range(hp):
                acc_ref[hh] = alphas[hh] * acc_ref[hh] + jnp.dot(vt_ref[hh, j], probs[hh],
                                                                 preferred_element_type=F32)

        def far_body(j, c):
            step(j, None, "block")
            return c

        lax.fori_loop(0, jnp.maximum(qi - 1, 0), far_body, 0)

        @pl.when(qi >= 1)
        def _():
            step(qi - 1, 1, "block")

        step(qi, 0, "tile")

        for hh in range(hp):
            acc = acc_ref[hh]
            on = acc[:dv] / acc[dv:dv + 1]
            ot = on[:, :tq] - lam * on[:, tq:]
            ot = ot * lax.rsqrt(jnp.mean(ot * ot, axis=0, keepdims=True) + SUBLN_EPS)
            o_ref[qrows, head(hh)] = (ot.T * g_ref[...] * (1.0 - lambda_init)).astype(o_ref.dtype)
        return carry

    lax.fori_loop(0, nq, q_body, 0)


def _t5_bucket(rel):
    max_exact = NUM_BUCKETS // 2
    n = jnp.maximum(rel, 0)
    is_small = n < max_exact
    nf = jnp.maximum(n, max_exact).astype(F32)
    large = max_exact + (jnp.log(nf / max_exact) / math.log(MAX_DISTANCE / max_exact)
                         * (NUM_BUCKETS - max_exact)).astype(jnp.int32)
    large = jnp.minimum(large, NUM_BUCKETS - 1)
    return jnp.where(is_small, n, large)


def _bias_tiles(rel_table):
    heads = rel_table.shape[1]
    span = 2 * TQ + 1
    bucket = _t5_bucket(jnp.arange(span))
    onehot = (bucket[:, None] == jnp.arange(NUM_BUCKETS)[None, :]).astype(F32)
    table = (rel_table.astype(F32) - rel_table[NUM_BUCKETS - 1].astype(F32)[None, :]) * LOG2E
    by_rel = jnp.dot(onehot, table, precision=lax.Precision.HIGHEST).T
    toep = jnp.tile(by_rel, (1, TQ))[:, :TQ * (span - 1)].reshape(heads, TQ, span - 1)
    kj = jnp.arange(TQ)[:, None]
    qi = jnp.arange(TQ)[None, :]
    diag = jnp.where((qi >= kj)[None], toep[:, :, :TQ], -jnp.inf)
    return jnp.stack([diag, toep[:, :, TQ:]], axis=1)


def _diff_attention(qkv, rel_table, lam_qk, subln_g, lambda_init, batch, seq, heads):
    t = qkv.shape[0]
    d = qkv.shape[1] // 3
    dv = d // heads
    hp = ATTN_HEADS
    assert TQ + 1 >= MAX_DISTANCE and seq % TQ == 0 and dv == LANES and heads % hp == 0
    nq = seq // TQ
    groups = heads // hp
    bias = _bias_tiles(rel_table)
    return pl.pallas_call(
        functools.partial(_diff_attn_kernel, lambda_init=lambda_init),
        grid=(batch, groups),
        in_specs=[
            pl.BlockSpec((seq, hp * dv), lambda b, g: (b, g)),
            pl.BlockSpec((seq, hp * dv), lambda b, g: (b, groups + g)),
            pl.BlockSpec((seq, hp * dv), lambda b, g: (b, 2 * groups + g)),
            pl.BlockSpec((hp, 2, TQ, TQ), lambda b, g: (g, 0, 0, 0)),
            pl.BlockSpec(lam_qk.shape, lambda b, g: (0, 0)),
            pl.BlockSpec((1, dv), lambda b, g: (0, 0)),
        ],
        out_specs=pl.BlockSpec((seq, hp * dv), lambda b, g: (b, g)),
        out_shape=jax.ShapeDtypeStruct((t, d), BF16),
        scratch_shapes=[
            pltpu.VMEM((hp, nq, dv + BF16_SUBLANES, TQ), BF16),
            pltpu.VMEM((hp, 1, 2 * TQ), F32),
            pltpu.VMEM((hp, dv + BF16_SUBLANES, 2 * TQ), F32),
            pltpu.VMEM((hp, TQ, 2 * TQ), F32),
        ],
        compiler_params=_cparams(("parallel", "parallel")),
    )(qkv, qkv, qkv, bias, lam_qk.astype(F32), subln_g.reshape(1, dv).astype(F32))


def _retention_kernel(q_ref, k_ref, v_ref, dec_ref, tab_ref, o_ref, state_ref):
    seq = q_ref.shape[0]
    hp, c, _ = dec_ref.shape
    dk = q_ref.shape[1] // hp
    dv = v_ref.shape[1] // hp
    heads = range(hp)
    state_ref[...] = jnp.zeros(state_ref.shape, F32)

    def body(n, carry):
        rows = pl.ds(pl.multiple_of(n * c, c), c)
        q = [q_ref[rows, hh * dk:(hh + 1) * dk] for hh in heads]
        k = [k_ref[rows, hh * dk:(hh + 1) * dk] for hh in heads]
        v = [v_ref[rows, hh * dv:(hh + 1) * dv] for hh in heads]
        tab = [tab_ref[hh] for hh in heads]
        st = [state_ref[hh] for hh in heads]
        s = [lax.dot_general(q[hh], k[hh], (((1,), (1,)), ((), ())), preferred_element_type=F32)
             for hh in heads]
        cross = [jnp.dot(q[hh], st[hh].astype(BF16), preferred_element_type=F32) for hh in heads]
        kt = [(k[hh].astype(F32) * tab[hh][:, 1:2]).T.astype(BF16) for hh in heads]
        inner = [jnp.dot((s[hh] * dec_ref[hh]).astype(BF16), v[hh], preferred_element_type=F32)
                 for hh in heads]
        for hh in heads:
            state_ref[hh] = st[hh] * tab[hh][0:1, 2:3] + jnp.dot(kt[hh], v[hh], preferred_element_type=F32)
        for hh in heads:
            o = inner[hh] + cross[hh] * tab[hh][:, 0:1]
            o = o * lax.rsqrt(jnp.mean(o * o, axis=-1, keepdims=True) + RMS_EPS)
            o_ref[rows, hh * dv:(hh + 1) * dv] = o.astype(o_ref.dtype)
        return carry

    lax.fori_loop(0, seq // c, body, 0)


def _retention_tables(heads):
    c = RET_C
    log_gamma = jnp.log(1.0 - 2.0 ** (-5.0 - jnp.arange(heads, dtype=F32)))
    pos = jnp.arange(c, dtype=F32)
    dist = pos[:, None] - pos[None, :]
    inner_decay = jnp.where(dist[None] >= 0,
                            jnp.exp(jnp.maximum(dist, 0.0)[None] * log_gamma[:, None, None]), 0.0)
    q_decay = jnp.exp((pos + 1.0)[None] * log_gamma[:, None])
    k_decay = jnp.exp((c - 1.0 - pos)[None] * log_gamma[:, None])
    chunk_decay = jnp.broadcast_to(jnp.exp(c * log_gamma)[:, None], (heads, c))
    tab = jnp.stack([q_decay, k_decay, chunk_decay], axis=-1)
    tab = jnp.pad(tab, ((0, 0), (0, 0), (0, LANES - 3)))
    return inner_decay, tab


def _retention(qkvg, batch, seq, qk_dim, v_dim):
    t = qkvg.shape[0]
    dk = qk_dim // RET_HEADS
    dvh = v_dim // RET_HEADS
    inner_decay, tab = _retention_tables(RET_HEADS)
    hp = RET_HP
    assert RET_HEADS % hp == 0
    groups = RET_HEADS // hp
    kb = qk_dim // (hp * dk)
    vb = 2 * qk_dim // (hp * dvh)
    return pl.pallas_call(
        _retention_kernel,
        grid=(batch, groups),
        in_specs=[
            pl.BlockSpec((seq, hp * dk), lambda b, g: (b, g)),
            pl.BlockSpec((seq, hp * dk), lambda b, g: (b, kb + g)),
            pl.BlockSpec((seq, hp * dvh), lambda b, g: (b, vb + g)),
            pl.BlockSpec((hp, RET_C, RET_C), lambda b, g: (g, 0, 0)),
            pl.BlockSpec((hp, RET_C, LANES), lambda b, g: (g, 0, 0)),
        ],
        out_specs=pl.BlockSpec((seq, hp * dvh), lambda b, g: (b, g)),
        out_shape=jax.ShapeDtypeStruct((t, v_dim), BF16),
        scratch_shapes=[pltpu.VMEM((hp, dk, dvh), F32)],
        compiler_params=_cparams(("parallel", "parallel")),
    )(qkvg, qkvg, qkvg, inner_decay, tab)


def _out_router_kernel(*refs, gated, n_groups, n_experts):
    if gated:
        a_ref, gate_ref, w_ref, x_ref, nf_ref, wr_ref, br_ref = refs[:7]
        x1_ref, hn_ref, route_ref, route_t_ref, cnt_ref, carry_ref = refs[7:]
    else:
        a_ref, w_ref, x_ref, nf_ref, wr_ref, br_ref = refs[:6]
        x1_ref, hn_ref, route_ref, route_t_ref, cnt_ref, carry_ref = refs[6:]
    epg = n_experts // n_groups
    tm = x_ref.shape[0] // ROUTER_SPLIT

    @pl.when(pl.program_id(0) == 0)
    def _():
        carry_ref[...] = jnp.zeros(carry_ref.shape, F32)

    x1_parts = []
    for part in range(ROUTER_SPLIT):
        rows = slice(part * tm, (part + 1) * tm)
        a = a_ref[rows, :]
        if gated:
            gate = gate_ref[rows, :]
            a = gate * jax.nn.sigmoid(gate) * a
        x1_parts.append(x_ref[rows, :] + jnp.dot(a, w_ref[...], preferred_element_type=F32))
    for part in range(ROUTER_SPLIT):
        rows = slice(part * tm, (part + 1) * tm)
        _route_rows(x1_parts[part], rows, nf_ref, wr_ref, br_ref, x1_ref, hn_ref, route_ref, route_t_ref,
                    cnt_ref, carry_ref, n_groups=n_groups, n_experts=n_experts, epg=epg)


def _route_rows(x1, rows, nf_ref, wr_ref, br_ref, x1_ref, hn_ref, route_ref, route_t_ref, cnt_ref,
                carry_ref, *, n_groups, n_experts, epg):
    tm = x1.shape[0]
    x1_ref[rows, :] = x1
    hn = x1 * lax.rsqrt(jnp.mean(x1 * x1, axis=-1, keepdims=True) + RMS_EPS) * nf_ref[...]
    hn_ref[rows, :] = _pack_bf16_pairs(hn)

    hn_hi = hn.astype(BF16)
    hn_lo = (hn - hn_hi.astype(F32)).astype(BF16)
    logits = (jnp.dot(hn_hi, wr_ref[0], preferred_element_type=F32)
              + jnp.dot(hn_lo, wr_ref[0], preferred_element_type=F32)
              + jnp.dot(hn_hi, wr_ref[1], preferred_element_type=F32)) + br_ref[...]

    n_rows = -(-(n_groups + n_experts) // SUBLANES) * SUBLANES
    lt = logits.T[:n_rows, :]
    row = lax.broadcasted_iota(jnp.int32, lt.shape, 0)
    big = jnp.int32(LANES)
    neg = jnp.float32(-jnp.inf)

    gl = jnp.where(row < n_groups, lt, neg)
    gmax = jnp.max(gl, axis=0, keepdims=True)
    g_p = 1.0 / jnp.sum(jnp.exp(gl - gmax), axis=0, keepdims=True)
    g_idx = jnp.min(jnp.where(gl == gmax, row, big), axis=0, keepdims=True)

    lo = n_groups + g_idx * epg
    el = jnp.where((row >= lo) & (row < lo + epg), lt, neg)
    v1 = jnp.max(el, axis=0, keepdims=True)
    i1 = jnp.min(jnp.where(el == v1, row, big), axis=0, keepdims=True)
    el2 = jnp.where(row == i1, neg, el)
    v2 = jnp.max(el2, axis=0, keepdims=True)
    i2 = jnp.min(jnp.where(el2 == v2, row, big), axis=0, keepdims=True)
    tt = jnp.exp(v2 - v1)
    w1 = g_p / (1.0 + tt)
    w2 = g_p * tt / (1.0 + tt)
    e1 = i1 - n_groups
    e2 = i2 - n_groups

    erow = lax.broadcasted_iota(jnp.int32, (n_experts, tm), 0)
    oh1 = erow == e1
    oh2 = erow == e2
    onehot = jnp.where(oh1 | oh2, 1.0, 0.0)
    before = lax.broadcasted_iota(jnp.int32, (tm, tm), 0) < lax.broadcasted_iota(jnp.int32, (tm, tm), 1)
    tri = jnp.where(before, 1.0, 0.0).astype(BF16)
    carry = carry_ref[...]
    base = jnp.dot(onehot.astype(BF16), tri, preferred_element_type=F32) + carry[:, 0:1]
    rank1 = jnp.sum(jnp.where(oh1, base, 0.0), axis=0, keepdims=True)
    rank2 = jnp.sum(jnp.where(oh2, base, 0.0), axis=0, keepdims=True)
    carry = carry + jnp.sum(onehot, axis=1, keepdims=True)
    carry_ref[...] = carry
    cnt_ref[...] = carry

    frow = lax.broadcasted_iota(jnp.int32, (LANES, tm), 0)
    out = jnp.zeros((LANES, tm), F32)
    for idx, val in enumerate((e1.astype(F32), e2.astype(F32), w1, w2, rank1, rank2)):
        out = jnp.where(frow == idx, val, out)
    route_t_ref[:, rows] = out[:route_t_ref.shape[0], :]
    route_ref[rows, :] = out.T


def _out_router(a, gate, w, x, nf, w_group, b_group, w_expert, b_expert):
    t, d = x.shape
    k = w.shape[0]
    n_groups = w_group.shape[1]
    n_experts = w_expert.shape[1]
    assert n_groups + n_experts <= LANES
    pad = LANES - n_groups - n_experts
    wr = jnp.pad(jnp.concatenate([w_group, w_expert], axis=1).astype(F32), ((0, 0), (0, pad)))
    wr_hi = wr.astype(BF16)
    wr = jnp.stack([wr_hi, (wr - wr_hi.astype(F32)).astype(BF16)])
    br = jnp.pad(jnp.concatenate([b_group, b_expert]).astype(F32), (0, pad)).reshape(1, LANES)
    gated = gate is not None
    row = lambda i: (i, 0)
    fixed = lambda i: (0, 0)
    in_specs = [pl.BlockSpec((TM, k), row)]
    args = [a]
    if gated:
        gate_arr, gate_col = gate
        in_specs.append(pl.BlockSpec((TM, k), lambda i: (i, gate_col)))
        args.append(gate_arr)
    in_specs += [
        pl.BlockSpec((k, d), fixed),
        pl.BlockSpec((TM, d), row),
        pl.BlockSpec((1, d), fixed),
        pl.BlockSpec((2, d, LANES), lambda i: (0, 0, 0)),
        pl.BlockSpec((1, LANES), fixed),
    ]
    args += [w, x, nf.reshape(1, d), wr, br]
    return pl.pallas_call(
        functools.partial(_out_router_kernel, gated=gated, n_groups=n_groups, n_experts=n_experts),
        grid=(t // TM,),
        in_specs=in_specs,
        out_specs=[
            pl.BlockSpec((TM, d), row),
            pl.BlockSpec((TM, d // 2), row),
            pl.BlockSpec((TM, LANES), row),
            pl.BlockSpec((SUBLANES, TM), lambda i: (0, i)),
            pl.BlockSpec((n_experts, LANES), fixed),
        ],
        out_shape=[
            jax.ShapeDtypeStruct((t, d), F32),
            jax.ShapeDtypeStruct((t, d // 2), jnp.uint32),
            jax.ShapeDtypeStruct((t, LANES), F32),
            jax.ShapeDtypeStruct((SUBLANES, t), F32),
            jax.ShapeDtypeStruct((n_experts, LANES), F32),
        ],
        scratch_shapes=[pltpu.VMEM((n_experts, LANES), F32)],
        compiler_params=_cparams(("arbitrary",)),
    )(*args)


def _sc_mesh():
    return plsc.VectorSubcoreMesh(core_axis_name="c", subcore_axis_name="s",
                                  num_cores=SC_CORES, num_subcores=SC_SUBCORES)


def _sc_worker():
    return lax.axis_index("s") * SC_CORES + lax.axis_index("c")


def _sc_scatter_rows(src, d1, d2, n_rows):
    t, d = src.shape
    workers = SC_CORES * SC_SUBCORES
    per_w = t // workers
    n_ch = per_w // SC_CHUNK
    assert t % (workers * SC_CHUNK) == 0
    idx = jnp.stack([d1, d2]).reshape(2, workers, n_ch, SC_CHUNK)

    @functools.partial(
        pl.kernel, mesh=_sc_mesh(),
        out_type=jax.ShapeDtypeStruct((n_rows, d), src.dtype),
        scratch_types=[pltpu.VMEM((2, n_ch, SC_CHUNK), jnp.int32),
                       pltpu.VMEM((SC_CHUNK, d), src.dtype),
                       pltpu.VMEM((SC_CHUNK, d), src.dtype),
                       pltpu.SemaphoreType.DMA, pltpu.SemaphoreType.DMA],
    )
    def scatter(src_hbm, idx_hbm, out_hbm, idx_v, rows_a, rows_b, sem_a, sem_b):
        wid = _sc_worker()
        base = wid * per_w
        for s in range(2):
            pltpu.sync_copy(idx_hbm.at[s, wid], idx_v.at[s])

        def move(c, rows_v, sem):
            pltpu.sync_copy(src_hbm.at[pl.ds(base + c * SC_CHUNK, SC_CHUNK)], rows_v)
            return [pltpu.async_copy(rows_v, out_hbm.at[idx_v.at[s, c]], sem) for s in range(2)]

        @pl.loop(0, n_ch, step=2)
        def _(c):
            first = move(c, rows_a, sem_a)
            second = move(c + 1, rows_b, sem_b)
            for cp in first + second:
                cp.wait()

    return scatter(src, idx)


def _sc_gather_rows(table, idx):
    b = idx.shape[0]
    d = table.shape[1]
    workers = SC_CORES * SC_SUBCORES
    per_w = b // workers
    n_ch = per_w // SC_CHUNK
    assert b % (workers * SC_CHUNK * 2) == 0

    @functools.partial(
        pl.kernel, mesh=_sc_mesh(),
        out_type=jax.ShapeDtypeStruct((b, d), table.dtype),
        scratch_types=[pltpu.VMEM((per_w,), jnp.int32),
                       pltpu.VMEM((SC_CHUNK, d), table.dtype),
                       pltpu.VMEM((SC_CHUNK, d), table.dtype),
                       pltpu.SemaphoreType.DMA, pltpu.SemaphoreType.DMA],
    )
    def gather(table_hbm, idx_hbm, out_hbm, idx_v, rows_a, rows_b, sem_a, sem_b):
        base = _sc_worker() * per_w
        pltpu.sync_copy(idx_hbm.at[pl.ds(base, per_w)], idx_v)

        def fetch(c, rows_v, sem):
            return pltpu.async_copy(table_hbm.at[idx_v.at[pl.ds(c * SC_CHUNK, SC_CHUNK)]], rows_v, sem)

        def store(c, rows_v):
            pltpu.sync_copy(rows_v, out_hbm.at[pl.ds(base + c * SC_CHUNK, SC_CHUNK)])

        @pl.loop(0, n_ch, step=2)
        def _(c):
            first = fetch(c, rows_a, sem_a)
            second = fetch(c + 1, rows_b, sem_b)
            first.wait()
            store(c, rows_a)
            second.wait()
            store(c + 1, rows_b)

    return gather(table, idx)


def _expert_kernel(fb_ref, nblk_ref, nv_ref, nu_ref, xs_ref, wg_ref, wu_ref, wd_ref, ys_ref,
                   wgb_ref, wub_ref, wdb_ref, xbuf, ybuf, xsem, ysem):
    e = pl.program_id(0)
    rb = xbuf.shape[1]
    nb = ys_ref.shape[0] // rb
    n_used = nu_ref[0]

    def x_copy(g, slot):
        rows = pl.ds(pl.multiple_of(g * rb, rb), rb)
        return pltpu.make_async_copy(xs_ref.at[rows], xbuf.at[slot], xsem.at[slot])

    def y_copy(g, slot):
        rows = pl.ds(pl.multiple_of(g * rb, rb), rb)
        return pltpu.make_async_copy(ybuf.at[slot], ys_ref.at[rows], ysem.at[slot])

    @pl.when(e == 0)
    def _():
        x_copy(0, 0).start()

    @pl.when(nblk_ref[e] > 0)
    def _():
        wgb_ref[...] = wg_ref[0, 0].astype(BF16)
        wub_ref[...] = wu_ref[0, 0].astype(BF16)
        wdb_ref[...] = wd_ref[0, 0].astype(BF16)

    def block(b, carry):
        g = fb_ref[e] + b
        slot = lax.rem(g, 2)
        x_copy(g, slot).wait()

        @pl.when(g + 1 < n_used)
        def _():
            x_copy(g + 1, 1 - slot).start()

        @pl.when(g >= 2)
        def _():
            y_copy(g - 2, slot).wait()

        valid = lax.broadcasted_iota(jnp.int32, (rb, 1), 0) < nv_ref[g]
        x_hi, x_lo = _unpack_bf16_pairs(jnp.where(valid, xbuf[slot], jnp.uint32(0)))
        x_hi, x_lo = x_hi.astype(BF16), x_lo.astype(BF16)
        h = x_hi.shape[1]

        def in_dot(w_ref):
            return (jnp.dot(x_hi, w_ref[:h, :], preferred_element_type=F32)
                    + jnp.dot(x_lo, w_ref[h:, :], preferred_element_type=F32))

        gate = in_dot(wgb_ref)
        up = in_dot(wub_ref)
        hmid = (gate * jax.nn.sigmoid(gate) * up).astype(BF16)
        ybuf[slot] = _pack_bf16_pairs(jnp.dot(hmid, wdb_ref[...], preferred_element_type=F32))
        y_copy(g, slot).start()
        return carry

    lax.fori_loop(0, nblk_ref[e], block, 0)

    @pl.when(e == pl.num_programs(0) - 1)
    def _():
        @pl.when(n_used >= 2)
        def _():
            y_copy(n_used - 2, lax.rem(n_used, 2)).wait()

        y_copy(n_used - 1, lax.rem(n_used - 1, 2)).wait()
        ybuf[0] = jnp.zeros(ybuf.shape[1:], ybuf.dtype)

        def tail_start(g, carry):
            y_copy(g, 0).start()
            return carry

        def tail_wait(g, carry):
            y_copy(g, 0).wait()
            return carry

        lax.fori_loop(n_used, nb, tail_start, 0)
        lax.fori_loop(n_used, nb, tail_wait, 0)


def _experts(xs, first_block, n_blocks, n_valid, n_used, w_gate, w_up, w_down, layer):
    n_rows, dp = xs.shape
    d = 2 * dp
    n_experts, ff = w_gate.shape[1], w_gate.shape[3]
    wsel = lambda e, *_: (layer, e, 0, 0)
    grid_spec = pltpu.PrefetchScalarGridSpec(
        num_scalar_prefetch=4,
        grid=(n_experts,),
        in_specs=[
            pl.BlockSpec(memory_space=pl.ANY),
            pl.BlockSpec((1, 1, d, ff), wsel),
            pl.BlockSpec((1, 1, d, ff), wsel),
            pl.BlockSpec((1, 1, ff, d), wsel),
        ],
        out_specs=pl.BlockSpec(memory_space=pl.ANY),
        scratch_shapes=[pltpu.VMEM((d, ff), BF16), pltpu.VMEM((d, ff), BF16), pltpu.VMEM((ff, d), BF16),
                        pltpu.VMEM((2, ROW_BLOCK, dp), xs.dtype), pltpu.VMEM((2, ROW_BLOCK, dp), xs.dtype),
                        pltpu.SemaphoreType.DMA((2,)), pltpu.SemaphoreType.DMA((2,))],
    )
    return pl.pallas_call(
        _expert_kernel,
        grid_spec=grid_spec,
        out_shape=jax.ShapeDtypeStruct((n_rows, dp), xs.dtype),
        compiler_params=_cparams(("arbitrary",)),
    )(first_block, n_blocks, n_valid, n_used, xs, w_gate, w_up, w_down)


def _final_combine_kernel(x_ref, route_ref, y1_ref, y2_ref, nf_ref, o_ref):
    route = route_ref[...]
    y1 = jnp.concatenate(_unpack_bf16_pairs(y1_ref[...]), axis=1)
    y2 = jnp.concatenate(_unpack_bf16_pairs(y2_ref[...]), axis=1)
    y = x_ref[...] + (route[:, 2:3] * y1 + route[:, 3:4] * y2)
    o_ref[...] = y * lax.rsqrt(jnp.mean(y * y, axis=-1, keepdims=True) + RMS_EPS) * nf_ref[...]


def _final_combine(x1, route, yg, nf):
    t, d = x1.shape
    slot2 = t // TM
    row = lambda i: (i, 0)
    return pl.pallas_call(
        _final_combine_kernel,
        grid=(t // TM,),
        in_specs=[
            pl.BlockSpec((TM, d), row),
            pl.BlockSpec((TM, LANES), row),
            pl.BlockSpec((TM, d // 2), row),
            pl.BlockSpec((TM, d // 2), lambda i: (slot2 + i, 0)),
            pl.BlockSpec((1, d), lambda i: (0, 0)),
        ],
        out_specs=pl.BlockSpec((TM, d), row),
        out_shape=jax.ShapeDtypeStruct((t, d), F32),
        compiler_params=_cparams(("parallel",)),
    )(x1, route, yg, yg, nf.reshape(1, d).astype(F32))


def _moe(hn, route_t, cnt, w_gate, w_up, w_down, layer):
    t, d = hn.shape
    n_experts = w_gate.shape[1]
    ids = jnp.arange(n_experts, dtype=jnp.int32)
    e1, e2, rank1, rank2 = (route_t[f].astype(jnp.int32) for f in (0, 1, 4, 5))
    counts = cnt[:, 0].astype(jnp.int32)
    padded = ((counts + ROW_BLOCK - 1) // ROW_BLOCK) * ROW_BLOCK
    pad_end = jnp.cumsum(padded)
    pad_start = pad_end - padded

    def dest(e, rank):
        return jnp.sum(jnp.where(e[:, None] == ids[None, :], pad_start[None, :], 0), axis=1) + rank

    d1 = dest(e1, rank1)
    d2 = dest(e2, rank2)
    nb = (t * TOP_K_INNER) // ROW_BLOCK + n_experts
    n_used = (pad_end[-1:] // ROW_BLOCK).astype(jnp.int32)
    first_row = jnp.arange(nb, dtype=jnp.int32) * ROW_BLOCK
    block_e = jnp.sum((pad_end[None, :] <= first_row[:, None]).astype(jnp.int32), axis=1)
    block_e = jnp.minimum(block_e, n_experts - 1)
    row_end = jnp.sum(jnp.where(block_e[:, None] == ids[None, :], (pad_start + counts)[None, :], 0), axis=1)
    n_valid = jnp.clip(row_end - first_row, 0, ROW_BLOCK).astype(jnp.int32)
    xs = _sc_scatter_rows(hn, d1, d2, nb * ROW_BLOCK)
    ys = _experts(xs, (pad_start // ROW_BLOCK).astype(jnp.int32), (padded // ROW_BLOCK).astype(jnp.int32),
                  n_valid, n_used, w_gate, w_up, w_down, layer)
    return _sc_gather_rows(ys, jnp.concatenate([d1, d2]))


def _diff_lambda_init(layer_idx):
    return 0.8 - 0.6 * math.exp(-0.3 * layer_idx)


def _rope_tables(seq, dk):
    angle = 1.0 / (10000.0 ** jnp.linspace(0.0, 1.0, dk // 2, dtype=F32))
    angle = jnp.repeat(angle, 2)
    phase = jnp.arange(seq, dtype=F32)[:, None] * angle[None, :]
    sin, cos = jnp.sin(phase), jnp.cos(phase)
    even = (jnp.arange(dk) % 2 == 0)[None, :]
    return cos, jnp.where(even, -sin, 0.0), jnp.where(even, 0.0, sin)


def kernel(x, rel_bias_table, norm_mix, norm_ffn, norm_final, diff_w_in, diff_lambda_qk, diff_subln,
           diff_w_out, ret_w_in, ret_w_out, moe_w_group, moe_b_group, moe_w_expert, moe_b_expert,
           moe_w_gate, moe_w_up, moe_w_down):
    batch, seq, d = x.shape
    depth = norm_mix.shape[0]
    t = batch * seq
    heads = rel_bias_table.shape[1]
    dh = diff_lambda_qk.shape[-1]
    qk_dim = (ret_w_in.shape[2] - 2 * ret_w_out.shape[1]) // 2
    v_dim = ret_w_out.shape[1]
    dk = qk_dim // RET_HEADS
    xt = x.reshape(t, d)
    moe = None
    for i in range(depth):
        j = i // N_MIXERS
        if i % N_MIXERS == 0:
            col_scale = jnp.concatenate([jnp.full((d,), dh ** -0.5 * LOG2E, F32), jnp.ones((2 * d,), F32)])
            w_in = (diff_w_in[j] * col_scale[None, :]).astype(BF16)
            qkv = _norm_matmul(xt, norm_mix[i], w_in, moe=moe)
            if moe is not None:
                qkv, xt = qkv
            mix = _diff_attention(qkv, rel_bias_table, diff_lambda_qk[j], diff_subln[j],
                                  _diff_lambda_init(i), batch, seq, heads)
            gate = None
            w_out = diff_w_out[j].astype(BF16)
        else:
            n_in = ret_w_in.shape[2]
            col_scale = jnp.concatenate([jnp.ones((qk_dim,), F32), jnp.full((qk_dim,), dk ** -0.5, F32),
                                         jnp.ones((n_in - 2 * qk_dim,), F32)])
            w_in = (ret_w_in[j] * col_scale[None, :]).astype(BF16)
            qkvg = _norm_matmul(xt, norm_mix[i], w_in, rope=_rope_tables(seq, dk),
                                n_rope=2 * qk_dim // TN, seq=seq, moe=moe)
            if moe is not None:
                qkvg, xt = qkvg
            mix = _retention(qkvg, batch, seq, qk_dim, v_dim)
            gate = (qkvg, (2 * qk_dim + v_dim) // v_dim)
            w_out = ret_w_out[j].astype(BF16)
        x1, hn, route, route_t, cnt = _out_router(mix, gate, w_out, xt, norm_ffn[i], moe_w_group[i],
                                                  moe_b_group[i], moe_w_expert[i], moe_b_expert[i])
        xt = x1
        moe = (route, _moe(hn, route_t, cnt, moe_w_gate[i:i + 1].astype(BF16), moe_w_up[i:i + 1].astype(BF16),
                           moe_w_down[i:i + 1].astype(BF16), 0))
    return _final_combine(xt, moe[0], moe[1], norm_final).reshape(batch, seq, d)
```

```python
import functools
import math

import jax
import jax.numpy as jnp
from jax import lax
from jax.experimental import pallas as pl
from jax.experimental.pallas import tpu as pltpu
from jax.experimental.pallas import tpu_sc as plsc

F32 = jnp.float32
BF16 = jnp.bfloat16

N_MIXERS = 2
NUM_BUCKETS = 32
MAX_DISTANCE = 128
RET_HEADS = 4
TOP_K_INNER = 2
RMS_EPS = 1e-6
SUBLN_EPS = 1e-5

LANES = 128
SUBLANES = 8
BF16_SUBLANES = 16
LOG2E = 1.4426950408889634
VMEM_LIMIT = 56 * 1024 * 1024
SC_CORES = 2
SC_SUBCORES = 16
SC_LANES = 16
SC_PACK_BYTES = 128 * 1024

TM = 512
TN = 1024
TQ = 256
ATTN_HEADS = 4
RET_C = 256
RET_HP = 2
ROUTER_SPLIT = 2
ROW_BLOCK = 512
SC_CHUNK = 64


def _cparams(sem):
    return pltpu.CompilerParams(dimension_semantics=sem, vmem_limit_bytes=VMEM_LIMIT)


def _pack_bf16_pairs(x):
    h = x.shape[1] // 2
    hi = lax.bitcast_convert_type(x[:, :h].astype(BF16).astype(F32), jnp.uint32)
    lo = lax.bitcast_convert_type(x[:, h:].astype(BF16).astype(F32), jnp.uint32)
    return hi | (lo >> 16)


def _unpack_bf16_pairs(w):
    hi = lax.bitcast_convert_type(w & jnp.uint32(0xFFFF0000), F32)
    lo = lax.bitcast_convert_type(w << 16, F32)
    return hi, lo


def _norm_matmul_kernel(x_ref, *rest, n_rope, moe):
    rest = list(rest)
    if moe:
        route_ref, y1_ref, y2_ref = rest[:3]
        x_out_ref = rest.pop()
        rest = rest[3:]
    g_ref, w_ref = rest[:2]
    if n_rope:
        cos_ref, sa_ref, sb_ref, o_ref = rest[2:]
    else:
        (o_ref,) = rest[2:]
    x = x_ref[...]
    if moe:
        route = route_ref[...]
        y1 = jnp.concatenate(_unpack_bf16_pairs(y1_ref[...]), axis=1)
        y2 = jnp.concatenate(_unpack_bf16_pairs(y2_ref[...]), axis=1)
        x = x + (route[:, 2:3] * y1 + route[:, 3:4] * y2)
        x_out_ref[...] = x
    xn = (x * lax.rsqrt(jnp.mean(x * x, axis=-1, keepdims=True) + RMS_EPS) * g_ref[...]).astype(BF16)
    for j in range(o_ref.shape[1] // TN):
        cols = slice(j * TN, (j + 1) * TN)
        acc = jnp.dot(xn, w_ref[:, cols], preferred_element_type=F32)
        if j < n_rope:
            reps = TN // cos_ref.shape[1]
            c = jnp.tile(cos_ref[...], (1, reps))
            sa = jnp.tile(sa_ref[...], (1, reps))
            sb = jnp.tile(sb_ref[...], (1, reps))
            nxt = pltpu.roll(acc, TN - 1, axis=1)
            prv = pltpu.roll(acc, 1, axis=1)
            acc = acc * c + nxt * sa + prv * sb
        o_ref[:, cols] = acc.astype(o_ref.dtype)


def _norm_matmul(x, g, w, rope=None, n_rope=0, seq=None, moe=None):
    t, d = x.shape
    n = w.shape[1]
    row = lambda i: (i, 0)
    in_specs = [pl.BlockSpec((TM, d), row)]
    args = [x]
    if moe is not None:
        route, yg = moe
        slot2 = t // TM
        in_specs += [pl.BlockSpec((TM, LANES), row), pl.BlockSpec((TM, d // 2), row),
                     pl.BlockSpec((TM, d // 2), lambda i: (slot2 + i, 0))]
        args += [route, yg, yg]
    in_specs += [
        pl.BlockSpec((1, d), lambda i: (0, 0)),
        pl.BlockSpec((d, n), lambda i: (0, 0), pipeline_mode=pl.Buffered(1)),
    ]
    args += [g.reshape(1, d), w]
    if rope is not None:
        nb = seq // TM
        dk = rope[0].shape[1]
        in_specs += [pl.BlockSpec((TM, dk), lambda i: (i % nb, 0))] * 3
        args += list(rope)
    out_specs = [pl.BlockSpec((TM, n), row)]
    out_shape = [jax.ShapeDtypeStruct((t, n), BF16)]
    if moe is not None:
        out_specs.append(pl.BlockSpec((TM, d), row))
        out_shape.append(jax.ShapeDtypeStruct((t, d), F32))
    out = pl.pallas_call(
        functools.partial(_norm_matmul_kernel, n_rope=n_rope, moe=moe is not None),
        grid=(t // TM,),
        in_specs=in_specs,
        out_specs=out_specs,
        out_shape=out_shape,
        compiler_params=_cparams(("parallel",)),
    )(*args)
    return out if moe is not None else out[0]


def _diff_attn_kernel(q_ref, k_ref, v_ref, bias_ref, lam_ref, g_ref, o_ref,
                      vt_ref, m_ref, acc_ref, s_ref, *, lambda_init):
    seq = q_ref.shape[0]
    hp, _, _, tq = bias_ref.shape
    dv = q_ref.shape[1] // hp
    dh = dv // 2
    nq = seq // tq

    def head(hh):
        return slice(hh * dv, (hh + 1) * dv)

    extra = vt_ref.shape[2] - dv
    ones_row = jnp.where(lax.broadcasted_iota(jnp.int32, (extra, tq), 0) == 0, 1.0, 0.0).astype(BF16)
    for hh in range(hp):
        for j in range(nq):
            vt_ref[hh, j, :dv, :] = v_ref[j * tq:(j + 1) * tq, head(hh)].astype(F32).T.astype(BF16)
            vt_ref[hh, j, dv:, :] = ones_row

    lq = lam_ref[...]
    lam = (jnp.exp(jnp.sum(lq[0:1] * lq[1:2], axis=-1, keepdims=True))
           - jnp.exp(jnp.sum(lq[2:3] * lq[3:4], axis=-1, keepdims=True)) + lambda_init)
    lane = lax.broadcasted_iota(jnp.int32, (tq, dv), 1)

    def query_maps(qi):
        qrows = pl.ds(pl.multiple_of(qi * tq, tq), tq)
        qs = []
        for hh in range(hp):
            q = q_ref[qrows, head(hh)]
            zero = jnp.zeros_like(q)
            qs.append(jnp.concatenate([jnp.where(lane < dh, q, zero), jnp.where(lane >= dh, q, zero)],
                                      axis=0))
        return qs

    def score_dots(j, qs):
        rows = pl.ds(pl.multiple_of(j * tq, tq), tq)
        for hh in range(hp):
            s_ref[hh] = lax.dot_general(k_ref[rows, head(hh)], qs[hh], (((1,), (1,)), ((), ())),
                                        preferred_element_type=F32)

    score_dots(0, query_maps(0))

    def q_body(qi, carry):
        qrows = pl.ds(pl.multiple_of(qi * tq, tq), tq)
        qs = query_maps(qi)
        qs_next = query_maps(jnp.minimum(qi + 1, nq - 1))

        m_ref[...] = jnp.full(m_ref.shape, -jnp.inf, F32)
        acc_ref[...] = jnp.zeros(acc_ref.shape, F32)

        def step(j, tile, prefetch):
            probs, alphas = [], []
            for hh in range(hp):
                s = s_ref[hh]
                if tile is not None:
                    bias = bias_ref[hh, tile]
                    s = s + jnp.concatenate([bias, bias], axis=1)
                m_old = m_ref[hh]
                m_new = jnp.maximum(m_old, jnp.max(s, axis=0, keepdims=True))
                alpha = jnp.exp2(m_old - m_new)
                m_ref[hh] = m_new
                probs.append(jnp.exp2(s - m_new).astype(BF16))
                alphas.append(alpha)
            if prefetch == "block":
                score_dots(j + 1, qs)
            else:
                score_dots(0, qs_next)
            for hh in range(hp):
                acc_ref[hh] = alphas[hh] * acc_ref[hh] + jnp.dot(vt_ref[hh, j], probs[hh],
                                                                 preferred_element_type=F32)

        def far_body(j, c):
            step(j, None, "block")
            return c

        lax.fori_loop(0, jnp.maximum(qi - 1, 0), far_body, 0)

        @pl.when(qi >= 1)
        def _():
            step(qi - 1, 1, "block")

        step(qi, 0, "tile")

        for hh in range(hp):
            acc = acc_ref[hh]
            on = acc[:dv] / acc[dv:dv + 1]
            ot = on[:, :tq] - lam * on[:, tq:]
            ot = ot * lax.rsqrt(jnp.mean(ot * ot, axis=0, keepdims=True) + SUBLN_EPS)
            o_ref[qrows, head(hh)] = (ot.T * g_ref[...] * (1.0 - lambda_init)).astype(o_ref.dtype)
        return carry

    lax.fori_loop(0, nq, q_body, 0)


def _t5_bucket(rel):
    max_exact = NUM_BUCKETS // 2
    n = jnp.maximum(rel, 0)
    is_small = n < max_exact
    nf = jnp.maximum(n, max_exact).astype(F32)
    large = max_exact + (jnp.log(nf / max_exact) / math.log(MAX_DISTANCE / max_exact)
                         * (NUM_BUCKETS - max_exact)).astype(jnp.int32)
    large = jnp.minimum(large, NUM_BUCKETS - 1)
    return jnp.where(is_small, n, large)


def _bias_tiles(rel_table):
    heads = rel_table.shape[1]
    span = 2 * TQ + 1
    bucket = _t5_bucket(jnp.arange(span))
    onehot = (bucket[:, None] == jnp.arange(NUM_BUCKETS)[None, :]).astype(F32)
    table = (rel_table.astype(F32) - rel_table[NUM_BUCKETS - 1].astype(F32)[None, :]) * LOG2E
    by_rel = jnp.dot(onehot, table, precision=lax.Precision.HIGHEST).T
    toep = jnp.tile(by_rel, (1, TQ))[:, :TQ * (span - 1)].reshape(heads, TQ, span - 1)
    kj = jnp.arange(TQ)[:, None]
    qi = jnp.arange(TQ)[None, :]
    diag = jnp.where((qi >= kj)[None], toep[:, :, :TQ], -jnp.inf)
    return jnp.stack([diag, toep[:, :, TQ:]], axis=1)


def _diff_attention(qkv, rel_table, lam_qk, subln_g, lambda_init, batch, seq, heads):
    t = qkv.shape[0]
    d = qkv.shape[1] // 3
    dv = d // heads
    hp = ATTN_HEADS
    assert TQ + 1 >= MAX_DISTANCE and seq % TQ == 0 and dv == LANES and heads % hp == 0
    nq = seq // TQ
    groups = heads // hp
    bias = _bias_tiles(rel_table)
    return pl.pallas_call(
        functools.partial(_diff_attn_kernel, lambda_init=lambda_init),
        grid=(batch, groups),
        in_specs=[
            pl.BlockSpec((seq, hp * dv), lambda b, g: (b, g)),
            pl.BlockSpec((seq, hp * dv), lambda b, g: (b, groups + g)),
            pl.BlockSpec((seq, hp * dv), lambda b, g: (b, 2 * groups + g)),
            pl.BlockSpec((hp, 2, TQ, TQ), lambda b, g: (g, 0, 0, 0)),
            pl.BlockSpec(lam_qk.shape, lambda b, g: (0, 0)),
            pl.BlockSpec((1, dv), lambda b, g: (0, 0)),
        ],
        out_specs=pl.BlockSpec((seq, hp * dv), lambda b, g: (b, g)),
        out_shape=jax.ShapeDtypeStruct((t, d), BF16),
        scratch_shapes=[
            pltpu.VMEM((hp, nq, dv + BF16_SUBLANES, TQ), BF16),
            pltpu.VMEM((hp, 1, 2 * TQ), F32),
            pltpu.VMEM((hp, dv + BF16_SUBLANES, 2 * TQ), F32),
            pltpu.VMEM((hp, TQ, 2 * TQ), F32),
        ],
        compiler_params=_cparams(("parallel", "parallel")),
    )(qkv, qkv, qkv, bias, lam_qk.astype(F32), subln_g.reshape(1, dv).astype(F32))


def _retention_kernel(q_ref, k_ref, v_ref, dec_ref, tab_ref, o_ref, state_ref):
    seq = q_ref.shape[0]
    hp, c, _ = dec_ref.shape
    dk = q_ref.shape[1] // hp
    dv = v_ref.shape[1] // hp
    heads = range(hp)
    state_ref[...] = jnp.zeros(state_ref.shape, F32)

    def body(n, carry):
        rows = pl.ds(pl.multiple_of(n * c, c), c)
        q = [q_ref[rows, hh * dk:(hh + 1) * dk] for hh in heads]
        k = [k_ref[rows, hh * dk:(hh + 1) * dk] for hh in heads]
        v = [v_ref[rows, hh * dv:(hh + 1) * dv] for hh in heads]
        tab = [tab_ref[hh] for hh in heads]
        st = [state_ref[hh] for hh in heads]
        s = [lax.dot_general(q[hh], k[hh], (((1,), (1,)), ((), ())), preferred_element_type=F32)
             for hh in heads]
        cross = [jnp.dot(q[hh], st[hh].astype(BF16), preferred_element_type=F32) for hh in heads]
        kt = [(k[hh].astype(F32) * tab[hh][:, 1:2]).T.astype(BF16) for hh in heads]
        inner = [jnp.dot((s[hh] * dec_ref[hh]).astype(BF16), v[hh], preferred_element_type=F32)
                 for hh in heads]
        for hh in heads:
            state_ref[hh] = st[hh] * tab[hh][0:1, 2:3] + jnp.dot(kt[hh], v[hh], preferred_element_type=F32)
        for hh in heads:
            o = inner[hh] + cross[hh] * tab[hh][:, 0:1]
            o = o * lax.rsqrt(jnp.mean(o * o, axis=-1, keepdims=True) + RMS_EPS)
            o_ref[rows, hh * dv:(hh + 1) * dv] = o.astype(o_ref.dtype)
        return carry

    lax.fori_loop(0, seq // c, body, 0)


def _retention_tables(heads):
    c = RET_C
    log_gamma = jnp.log(1.0 - 2.0 ** (-5.0 - jnp.arange(heads, dtype=F32)))
    pos = jnp.arange(c, dtype=F32)
    dist = pos[:, None] - pos[None, :]
    inner_decay = jnp.where(dist[None] >= 0,
                            jnp.exp(jnp.maximum(dist, 0.0)[None] * log_gamma[:, None, None]), 0.0)
    q_decay = jnp.exp((pos + 1.0)[None] * log_gamma[:, None])
    k_decay = jnp.exp((c - 1.0 - pos)[None] * log_gamma[:, None])
    chunk_decay = jnp.broadcast_to(jnp.exp(c * log_gamma)[:, None], (heads, c))
    tab = jnp.stack([q_decay, k_decay, chunk_decay], axis=-1)
    tab = jnp.pad(tab, ((0, 0), (0, 0), (0, LANES - 3)))
    return inner_decay, tab


def _retention(qkvg, batch, seq, qk_dim, v_dim):
    t = qkvg.shape[0]
    dk = qk_dim // RET_HEADS
    dvh = v_dim // RET_HEADS
    inner_decay, tab = _retention_tables(RET_HEADS)
    hp = RET_HP
    assert RET_HEADS % hp == 0
    groups = RET_HEADS // hp
    kb = qk_dim // (hp * dk)
    vb = 2 * qk_dim // (hp * dvh)
    return pl.pallas_call(
        _retention_kernel,
        grid=(batch, groups),
        in_specs=[
            pl.BlockSpec((seq, hp * dk), lambda b, g: (b, g)),
            pl.BlockSpec((seq, hp * dk), lambda b, g: (b, kb + g)),
            pl.BlockSpec((seq, hp * dvh), lambda b, g: (b, vb + g)),
            pl.BlockSpec((hp, RET_C, RET_C), lambda b, g: (g, 0, 0)),
            pl.BlockSpec((hp, RET_C, LANES), lambda b, g: (g, 0, 0)),
        ],
        out_specs=pl.BlockSpec((seq, hp * dvh), lambda b, g: (b, g)),
        out_shape=jax.ShapeDtypeStruct((t, v_dim), BF16),
        scratch_shapes=[pltpu.VMEM((hp, dk, dvh), F32)],
        compiler_params=_cparams(("parallel", "parallel")),
    )(qkvg, qkvg, qkvg, inner_decay, tab)


def _out_router_kernel(*refs, gated, n_groups, n_experts):
    if gated:
        a_ref, gate_ref, w_ref, x_ref, nf_ref, wr_ref, br_ref = refs[:7]
        x1_ref, hn_ref, route_ref, route_t_ref, cnt_ref, carry_ref = refs[7:]
    else:
        a_ref, w_ref, x_ref, nf_ref, wr_ref, br_ref = refs[:6]
        x1_ref, hn_ref, route_ref, route_t_ref, cnt_ref, carry_ref = refs[6:]
    epg = n_experts // n_groups
    tm = x_ref.shape[0] // ROUTER_SPLIT

    @pl.when(pl.program_id(0) == 0)
    def _():
        carry_ref[...] = jnp.zeros(carry_ref.shape, F32)

    x1_parts = []
    for part in range(ROUTER_SPLIT):
        rows = slice(part * tm, (part + 1) * tm)
        a = a_ref[rows, :]
        if gated:
            gate = gate_ref[rows, :]
            a = gate * jax.nn.sigmoid(gate) * a
        x1_parts.append(x_ref[rows, :] + jnp.dot(a, w_ref[...], preferred_element_type=F32))
    for part in range(ROUTER_SPLIT):
        rows = slice(part * tm, (part + 1) * tm)
        _route_rows(x1_parts[part], rows, nf_ref, wr_ref, br_ref, x1_ref, hn_ref, route_ref, route_t_ref,
                    cnt_ref, carry_ref, n_groups=n_groups, n_experts=n_experts, epg=epg)


def _route_rows(x1, rows, nf_ref, wr_ref, br_ref, x1_ref, hn_ref, route_ref, route_t_ref, cnt_ref,
                carry_ref, *, n_groups, n_experts, epg):
    tm = x1.shape[0]
    x1_ref[rows, :] = x1
    hn = x1 * lax.rsqrt(jnp.mean(x1 * x1, axis=-1, keepdims=True) + RMS_EPS) * nf_ref[...]
    hn_ref[rows, :] = _pack_bf16_pairs(hn)

    hn_hi = hn.astype(BF16)
    hn_lo = (hn - hn_hi.astype(F32)).astype(BF16)
    logits = (jnp.dot(hn_hi, wr_ref[0], preferred_element_type=F32)
              + jnp.dot(hn_lo, wr_ref[0], preferred_element_type=F32)
              + jnp.dot(hn_hi, wr_ref[1], preferred_element_type=F32)) + br_ref[...]

    n_rows = -(-(n_groups + n_experts) // SUBLANES) * SUBLANES
    lt = logits.T[:n_rows, :]
    row = lax.broadcasted_iota(jnp.int32, lt.shape, 0)
    big = jnp.int32(LANES)
    neg = jnp.float32(-jnp.inf)

    gl = jnp.where(row < n_groups, lt, neg)
    gmax = jnp.max(gl, axis=0, keepdims=True)
    g_p = 1.0 / jnp.sum(jnp.exp(gl - gmax), axis=0, keepdims=True)
    g_idx = jnp.min(jnp.where(gl == gmax, row, big), axis=0, keepdims=True)

    lo = n_groups + g_idx * epg
    el = jnp.where((row >= lo) & (row < lo + epg), lt, neg)
    v1 = jnp.max(el, axis=0, keepdims=True)
    i1 = jnp.min(jnp.where(el == v1, row, big), axis=0, keepdims=True)
    el2 = jnp.where(row == i1, neg, el)
    v2 = jnp.max(el2, axis=0, keepdims=True)
    i2 = jnp.min(jnp.where(el2 == v2, row, big), axis=0, keepdims=True)
    tt = jnp.exp(v2 - v1)
    w1 = g_p / (1.0 + tt)
    w2 = g_p * tt / (1.0 + tt)
    e1 = i1 - n_groups
    e2 = i2 - n_groups

    erow = lax.broadcasted_iota(jnp.int32, (n_experts, tm), 0)
    oh1 = erow == e1
    oh2 = erow == e2
    onehot = jnp.where(oh1 | oh2, 1.0, 0.0)
    before = lax.broadcasted_iota(jnp.int32, (tm, tm), 0) < lax.broadcasted_iota(jnp.int32, (tm, tm), 1)
    tri = jnp.where(before, 1.0, 0.0).astype(BF16)
    carry = carry_ref[...]
    base = jnp.dot(onehot.astype(BF16), tri, preferred_element_type=F32) + carry[:, 0:1]
    rank1 = jnp.sum(jnp.where(oh1, base, 0.0), axis=0, keepdims=True)
    rank2 = jnp.sum(jnp.where(oh2, base, 0.0), axis=0, keepdims=True)
    carry = carry + jnp.sum(onehot, axis=1, keepdims=True)
    carry_ref[...] = carry
    cnt_ref[...] = carry

    frow = lax.broadcasted_iota(jnp.int32, (LANES, tm), 0)
    out = jnp.zeros((LANES, tm), F32)
    for idx, val in enumerate((e1.astype(F32), e2.astype(F32), w1, w2, rank1, rank2)):
        out = jnp.where(frow == idx, val, out)
    route_t_ref[:, rows] = out[:route_t_ref.shape[0], :]
    route_ref[rows, :] = out.T


def _out_router(a, gate, w, x, nf, w_group, b_group, w_expert, b_expert):
    t, d = x.shape
    k = w.shape[0]
    n_groups = w_group.shape[1]
    n_experts = w_expert.shape[1]
    assert n_groups + n_experts <= LANES
    pad = LANES - n_groups - n_experts
    wr = jnp.pad(jnp.concatenate([w_group, w_expert], axis=1).astype(F32), ((0, 0), (0, pad)))
    wr_hi = wr.astype(BF16)
    wr = jnp.stack([wr_hi, (wr - wr_hi.astype(F32)).astype(BF16)])
    br = jnp.pad(jnp.concatenate([b_group, b_expert]).astype(F32), (0, pad)).reshape(1, LANES)
    gated = gate is not None
    row = lambda i: (i, 0)
    fixed = lambda i: (0, 0)
    in_specs = [pl.BlockSpec((TM, k), row)]
    args = [a]
    if gated:
        gate_arr, gate_col = gate
        in_specs.append(pl.BlockSpec((TM, k), lambda i: (i, gate_col)))
        args.append(gate_arr)
    in_specs += [
        pl.BlockSpec((k, d), fixed),
        pl.BlockSpec((TM, d), row),
        pl.BlockSpec((1, d), fixed),
        pl.BlockSpec((2, d, LANES), lambda i: (0, 0, 0)),
        pl.BlockSpec((1, LANES), fixed),
    ]
    args += [w, x, nf.reshape(1, d), wr, br]
    return pl.pallas_call(
        functools.partial(_out_router_kernel, gated=gated, n_groups=n_groups, n_experts=n_experts),
        grid=(t // TM,),
        in_specs=in_specs,
        out_specs=[
            pl.BlockSpec((TM, d), row),
            pl.BlockSpec((TM, d // 2), row),
            pl.BlockSpec((TM, LANES), row),
            pl.BlockSpec((SUBLANES, TM), lambda i: (0, i)),
            pl.BlockSpec((n_experts, LANES), fixed),
        ],
        out_shape=[
            jax.ShapeDtypeStruct((t, d), F32),
            jax.ShapeDtypeStruct((t, d // 2), jnp.uint32),
            jax.ShapeDtypeStruct((t, LANES), F32),
            jax.ShapeDtypeStruct((SUBLANES, t), F32),
            jax.ShapeDtypeStruct((n_experts, LANES), F32),
        ],
        scratch_shapes=[pltpu.VMEM((n_experts, LANES), F32)],
        compiler_params=_cparams(("arbitrary",)),
    )(*args)


def _sc_mesh():
    return plsc.VectorSubcoreMesh(core_axis_name="c", subcore_axis_name="s",
                                  num_cores=SC_CORES, num_subcores=SC_SUBCORES)


def _sc_worker():
    return lax.axis_index("s") * SC_CORES + lax.axis_index("c")


def _sc_scatter_rows(src, d1, d2, n_rows):
    t, d = src.shape
    workers = SC_CORES * SC_SUBCORES
    per_w = t // workers
    n_ch = per_w // SC_CHUNK
    assert t % (workers * SC_CHUNK) == 0
    idx = jnp.stack([d1, d2]).reshape(2, workers, n_ch, SC_CHUNK)

    @functools.partial(
        pl.kernel, mesh=_sc_mesh(),
        out_type=jax.ShapeDtypeStruct((n_rows, d), src.dtype),
        scratch_types=[pltpu.VMEM((2, n_ch, SC_CHUNK), jnp.int32),
                       pltpu.VMEM((SC_CHUNK, d), src.dtype),
                       pltpu.VMEM((SC_CHUNK, d), src.dtype),
                       pltpu.SemaphoreType.DMA, pltpu.SemaphoreType.DMA],
    )
    def scatter(src_hbm, idx_hbm, out_hbm, idx_v, rows_a, rows_b, sem_a, sem_b):
        wid = _sc_worker()
        base = wid * per_w
        for s in range(2):
            pltpu.sync_copy(idx_hbm.at[s, wid], idx_v.at[s])

        def move(c, rows_v, sem):
            pltpu.sync_copy(src_hbm.at[pl.ds(base + c * SC_CHUNK, SC_CHUNK)], rows_v)
            return [pltpu.async_copy(rows_v, out_hbm.at[idx_v.at[s, c]], sem) for s in range(2)]

        @pl.loop(0, n_ch, step=2)
        def _(c):
            first = move(c, rows_a, sem_a)
            second = move(c + 1, rows_b, sem_b)
            for cp in first + second:
                cp.wait()

    return scatter(src, idx)


def _sc_pack_weight(w, layer):
    _, n_e, k, n = w.shape
    h = n // 2
    rows = n_e * k
    w2d = w.reshape(-1, n)
    workers = SC_CORES * SC_SUBCORES
    per_w = rows // workers
    ch = SC_PACK_BYTES // (4 * n)
    assert rows % (workers * ch) == 0 and h % SC_LANES == 0

    @functools.partial(
        pl.kernel, mesh=_sc_mesh(),
        out_type=jax.ShapeDtypeStruct((rows, h), jnp.uint32),
        scratch_types=[pltpu.VMEM((ch, n), F32), pltpu.VMEM((ch, h), jnp.uint32)],
        compiler_params=pltpu.CompilerParams(needs_layout_passes=False),
    )
    def pack(w_hbm, o_hbm, inb, outb):
        base = _sc_worker() * per_w

        @pl.loop(0, per_w // ch)
        def _(c):
            pltpu.sync_copy(w_hbm.at[pl.ds(layer * rows + base + c * ch, ch)], inb)

            @pl.loop(0, ch)
            def _(row):
                @pl.loop(0, h, step=SC_LANES)
                def _(col):
                    a = inb[row, pl.ds(col, SC_LANES)]
                    b = inb[row, pl.ds(h + col, SC_LANES)]
                    packed = plsc.pack(a, b, format=plsc.PackFormat.INTERLEAVED)
                    outb[row, pl.ds(col, SC_LANES)] = plsc.bitcast(packed, jnp.uint32)

            pltpu.sync_copy(outb, o_hbm.at[pl.ds(base + c * ch, ch)])

    return pack(w2d).reshape(n_e, k, h)


def _sc_gather_rows(table, idx):
    b = idx.shape[0]
    d = table.shape[1]
    workers = SC_CORES * SC_SUBCORES
    per_w = b // workers
    n_ch = per_w // SC_CHUNK
    assert b % (workers * SC_CHUNK * 2) == 0

    @functools.partial(
        pl.kernel, mesh=_sc_mesh(),
        out_type=jax.ShapeDtypeStruct((b, d), table.dtype),
        scratch_types=[pltpu.VMEM((per_w,), jnp.int32),
                       pltpu.VMEM((SC_CHUNK, d), table.dtype),
                       pltpu.VMEM((SC_CHUNK, d), table.dtype),
                       pltpu.SemaphoreType.DMA, pltpu.SemaphoreType.DMA],
    )
    def gather(table_hbm, idx_hbm, out_hbm, idx_v, rows_a, rows_b, sem_a, sem_b):
        base = _sc_worker() * per_w
        pltpu.sync_copy(idx_hbm.at[pl.ds(base, per_w)], idx_v)

        def fetch(c, rows_v, sem):
            return pltpu.async_copy(table_hbm.at[idx_v.at[pl.ds(c * SC_CHUNK, SC_CHUNK)]], rows_v, sem)

        def store(c, rows_v):
            pltpu.sync_copy(rows_v, out_hbm.at[pl.ds(base + c * SC_CHUNK, SC_CHUNK)])

        @pl.loop(0, n_ch, step=2)
        def _(c):
            first = fetch(c, rows_a, sem_a)
            second = fetch(c + 1, rows_b, sem_b)
            first.wait()
            store(c, rows_a)
            second.wait()
            store(c + 1, rows_b)

    return gather(table, idx)


def _expert_kernel(fb_ref, nblk_ref, nv_ref, nu_ref, xs_ref, wg_ref, wu_ref, wd_ref, ys_ref,
                   wgb_ref, wub_ref, wdb_ref, xbuf, ybuf, xsem, ysem):
    e = pl.program_id(0)
    rb = xbuf.shape[1]
    nb = ys_ref.shape[0] // rb
    n_used = nu_ref[0]

    def x_copy(g, slot):
        rows = pl.ds(pl.multiple_of(g * rb, rb), rb)
        return pltpu.make_async_copy(xs_ref.at[rows], xbuf.at[slot], xsem.at[slot])

    def y_copy(g, slot):
        rows = pl.ds(pl.multiple_of(g * rb, rb), rb)
        return pltpu.make_async_copy(ybuf.at[slot], ys_ref.at[rows], ysem.at[slot])

    @pl.when(e == 0)
    def _():
        x_copy(0, 0).start()

    @pl.when(nblk_ref[e] > 0)
    def _():
        for src, dst in ((wg_ref, wgb_ref), (wu_ref, wub_ref), (wd_ref, wdb_ref)):
            words = src[0]
            half = words.shape[1]
            dst[:, :half] = lax.bitcast_convert_type(words << 16, F32).astype(BF16)
            dst[:, half:] = lax.bitcast_convert_type(words & jnp.uint32(0xFFFF0000), F32).astype(BF16)

    def block(b, carry):
        g = fb_ref[e] + b
        slot = lax.rem(g, 2)
        x_copy(g, slot).wait()

        @pl.when(g + 1 < n_used)
        def _():
            x_copy(g + 1, 1 - slot).start()

        @pl.when(g >= 2)
        def _():
            y_copy(g - 2, slot).wait()

        valid = lax.broadcasted_iota(jnp.int32, (rb, 1), 0) < nv_ref[g]
        x_hi, x_lo = _unpack_bf16_pairs(jnp.where(valid, xbuf[slot], jnp.uint32(0)))
        x_hi, x_lo = x_hi.astype(BF16), x_lo.astype(BF16)
        h = x_hi.shape[1]

        def in_dot(w_ref):
            return (jnp.dot(x_hi, w_ref[:h, :], preferred_element_type=F32)
                    + jnp.dot(x_lo, w_ref[h:, :], preferred_element_type=F32))

        gate = in_dot(wgb_ref)
        up = in_dot(wub_ref)
        hmid = (gate * jax.nn.sigmoid(gate) * up).astype(BF16)
        ybuf[slot] = _pack_bf16_pairs(jnp.dot(hmid, wdb_ref[...], preferred_element_type=F32))
        y_copy(g, slot).start()
        return carry

    lax.fori_loop(0, nblk_ref[e], block, 0)

    @pl.when(e == pl.num_programs(0) - 1)
    def _():
        @pl.when(n_used >= 2)
        def _():
            y_copy(n_used - 2, lax.rem(n_used, 2)).wait()

        y_copy(n_used - 1, lax.rem(n_used - 1, 2)).wait()
        ybuf[0] = jnp.zeros(ybuf.shape[1:], ybuf.dtype)

        def tail_start(g, carry):
            y_copy(g, 0).start()
            return carry

        def tail_wait(g, carry):
            y_copy(g, 0).wait()
            return carry

        lax.fori_loop(n_used, nb, tail_start, 0)
        lax.fori_loop(n_used, nb, tail_wait, 0)


def _experts(xs, first_block, n_blocks, n_valid, n_used, w_gate, w_up, w_down):
    n_rows, dp = xs.shape
    d = 2 * dp
    n_experts, ff = w_gate.shape[0], 2 * w_gate.shape[2]
    wsel = lambda e, *_: (e, 0, 0)
    grid_spec = pltpu.PrefetchScalarGridSpec(
        num_scalar_prefetch=4,
        grid=(n_experts,),
        in_specs=[
            pl.BlockSpec(memory_space=pl.ANY),
            pl.BlockSpec((1, d, ff // 2), wsel),
            pl.BlockSpec((1, d, ff // 2), wsel),
            pl.BlockSpec((1, ff, d // 2), wsel),
        ],
        out_specs=pl.BlockSpec(memory_space=pl.ANY),
        scratch_shapes=[pltpu.VMEM((d, ff), BF16), pltpu.VMEM((d, ff), BF16), pltpu.VMEM((ff, d), BF16),
                        pltpu.VMEM((2, ROW_BLOCK, dp), xs.dtype), pltpu.VMEM((2, ROW_BLOCK, dp), xs.dtype),
                        pltpu.SemaphoreType.DMA((2,)), pltpu.SemaphoreType.DMA((2,))],
    )
    return pl.pallas_call(
        _expert_kernel,
        grid_spec=grid_spec,
        out_shape=jax.ShapeDtypeStruct((n_rows, dp), xs.dtype),
        compiler_params=_cparams(("arbitrary",)),
    )(first_block, n_blocks, n_valid, n_used, xs, w_gate, w_up, w_down)


def _final_combine_kernel(x_ref, route_ref, y1_ref, y2_ref, nf_ref, o_ref):
    route = route_ref[...]
    y1 = jnp.concatenate(_unpack_bf16_pairs(y1_ref[...]), axis=1)
    y2 = jnp.concatenate(_unpack_bf16_pairs(y2_ref[...]), axis=1)
    y = x_ref[...] + (route[:, 2:3] * y1 + route[:, 3:4] * y2)
    o_ref[...] = y * lax.rsqrt(jnp.mean(y * y, axis=-1, keepdims=True) + RMS_EPS) * nf_ref[...]


def _final_combine(x1, route, yg, nf):
    t, d = x1.shape
    slot2 = t // TM
    row = lambda i: (i, 0)
    return pl.pallas_call(
        _final_combine_kernel,
        grid=(t // TM,),
        in_specs=[
            pl.BlockSpec((TM, d), row),
            pl.BlockSpec((TM, LANES), row),
            pl.BlockSpec((TM, d // 2), row),
            pl.BlockSpec((TM, d // 2), lambda i: (slot2 + i, 0)),
            pl.BlockSpec((1, d), lambda i: (0, 0)),
        ],
        out_specs=pl.BlockSpec((TM, d), row),
        out_shape=jax.ShapeDtypeStruct((t, d), F32),
        compiler_params=_cparams(("parallel",)),
    )(x1, route, yg, yg, nf.reshape(1, d).astype(F32))


def _moe(hn, route_t, cnt, w_gate, w_up, w_down):
    t, d = hn.shape
    n_experts = w_gate.shape[0]
    ids = jnp.arange(n_experts, dtype=jnp.int32)
    e1, e2, rank1, rank2 = (route_t[f].astype(jnp.int32) for f in (0, 1, 4, 5))
    counts = cnt[:, 0].astype(jnp.int32)
    padded = ((counts + ROW_BLOCK - 1) // ROW_BLOCK) * ROW_BLOCK
    pad_end = jnp.cumsum(padded)
    pad_start = pad_end - padded

    def dest(e, rank):
        return jnp.sum(jnp.where(e[:, None] == ids[None, :], pad_start[None, :], 0), axis=1) + rank

    d1 = dest(e1, rank1)
    d2 = dest(e2, rank2)
    nb = (t * TOP_K_INNER) // ROW_BLOCK + n_experts
    n_used = (pad_end[-1:] // ROW_BLOCK).astype(jnp.int32)
    first_row = jnp.arange(nb, dtype=jnp.int32) * ROW_BLOCK
    block_e = jnp.sum((pad_end[None, :] <= first_row[:, None]).astype(jnp.int32), axis=1)
    block_e = jnp.minimum(block_e, n_experts - 1)
    row_end = jnp.sum(jnp.where(block_e[:, None] == ids[None, :], (pad_start + counts)[None, :], 0), axis=1)
    n_valid = jnp.clip(row_end - first_row, 0, ROW_BLOCK).astype(jnp.int32)
    xs = _sc_scatter_rows(hn, d1, d2, nb * ROW_BLOCK)
    ys = _experts(xs, (pad_start // ROW_BLOCK).astype(jnp.int32), (padded // ROW_BLOCK).astype(jnp.int32),
                  n_valid, n_used, w_gate, w_up, w_down)
    return _sc_gather_rows(ys, jnp.concatenate([d1, d2]))


def _diff_lambda_init(layer_idx):
    return 0.8 - 0.6 * math.exp(-0.3 * layer_idx)


def _rope_tables(seq, dk):
    angle = 1.0 / (10000.0 ** jnp.linspace(0.0, 1.0, dk // 2, dtype=F32))
    angle = jnp.repeat(angle, 2)
    phase = jnp.arange(seq, dtype=F32)[:, None] * angle[None, :]
    sin, cos = jnp.sin(phase), jnp.cos(phase)
    even = (jnp.arange(dk) % 2 == 0)[None, :]
    return cos, jnp.where(even, -sin, 0.0), jnp.where(even, 0.0, sin)


def kernel(x, rel_bias_table, norm_mix, norm_ffn, norm_final, diff_w_in, diff_lambda_qk, diff_subln,
           diff_w_out, ret_w_in, ret_w_out, moe_w_group, moe_b_group, moe_w_expert, moe_b_expert,
           moe_w_gate, moe_w_up, moe_w_down):
    batch, seq, d = x.shape
    depth = norm_mix.shape[0]
    t = batch * seq
    heads = rel_bias_table.shape[1]
    dh = diff_lambda_qk.shape[-1]
    qk_dim = (ret_w_in.shape[2] - 2 * ret_w_out.shape[1]) // 2
    v_dim = ret_w_out.shape[1]
    dk = qk_dim // RET_HEADS
    xt = x.reshape(t, d)
    moe = None
    expert_w = [[_sc_pack_weight(w, i) for w in (moe_w_gate, moe_w_up, moe_w_down)] for i in range(depth)]
    for i in range(depth):
        j = i // N_MIXERS
        if i % N_MIXERS == 0:
            col_scale = jnp.concatenate([jnp.full((d,), dh ** -0.5 * LOG2E, F32), jnp.ones((2 * d,), F32)])
            w_in = (diff_w_in[j] * col_scale[None, :]).astype(BF16)
            qkv = _norm_matmul(xt, norm_mix[i], w_in, moe=moe)
            if moe is not None:
                qkv, xt = qkv
            mix = _diff_attention(qkv, rel_bias_table, diff_lambda_qk[j], diff_subln[j],
                                  _diff_lambda_init(i), batch, seq, heads)
            gate = None
            w_out = diff_w_out[j].astype(BF16)
        else:
            n_in = ret_w_in.shape[2]
            col_scale = jnp.concatenate([jnp.ones((qk_dim,), F32), jnp.full((qk_dim,), dk ** -0.5, F32),
                                         jnp.ones((n_in - 2 * qk_dim,), F32)])
            w_in = (ret_w_in[j] * col_scale[None, :]).astype(BF16)
            qkvg = _norm_matmul(xt, norm_mix[i], w_in, rope=_rope_tables(seq, dk),
                                n_rope=2 * qk_dim // TN, seq=seq, moe=moe)
            if moe is not None:
                qkvg, xt = qkvg
            mix = _retention(qkvg, batch, seq, qk_dim, v_dim)
            gate = (qkvg, (2 * qk_dim + v_dim) // v_dim)
            w_out = ret_w_out[j].astype(BF16)
        x1, hn, route, route_t, cnt = _out_router(mix, gate, w_out, xt, norm_ffn[i], moe_w_group[i],
                                                  moe_b_group[i], moe_w_expert[i], moe_b_expert[i])
        xt = x1
        moe = (route, _moe(hn, route_t, cnt, *expert_w[i]))
    return _final_combine(xt, moe[0], moe[1], norm_final).reshape(batch, seq, d)
```

```python
import functools
import math

import jax
import jax.numpy as jnp
from jax import lax
from jax.experimental import pallas as pl
from jax.experimental.pallas import tpu as pltpu
from jax.experimental.pallas import tpu_sc as plsc

F32 = jnp.float32
BF16 = jnp.bfloat16

N_MIXERS = 2
NUM_BUCKETS = 32
MAX_DISTANCE = 128
RET_HEADS = 4
TOP_K_INNER = 2
RMS_EPS = 1e-6
SUBLN_EPS = 1e-5

LANES = 128
SUBLANES = 8
BF16_SUBLANES = 16
LOG2E = 1.4426950408889634
VMEM_LIMIT = 56 * 1024 * 1024
SC_CORES = 2
SC_SUBCORES = 16
SC_LANES = 16
SC_PACK_BYTES = 64 * 1024

TM = 512
TN = 1024
TQ = 256
ATTN_HEADS = 4
RET_C = 256
RET_HP = 2
ROUTER_SPLIT = 2
ROW_BLOCK = 512
SC_CHUNK = 64


def _cparams(sem):
    return pltpu.CompilerParams(dimension_semantics=sem, vmem_limit_bytes=VMEM_LIMIT)


def _pack_bf16_pairs(x):
    h = x.shape[1] // 2
    hi = lax.bitcast_convert_type(x[:, :h].astype(BF16).astype(F32), jnp.uint32)
    lo = lax.bitcast_convert_type(x[:, h:].astype(BF16).astype(F32), jnp.uint32)
    return hi | (lo >> 16)


def _unpack_bf16_pairs(w):
    hi = lax.bitcast_convert_type(w & jnp.uint32(0xFFFF0000), F32)
    lo = lax.bitcast_convert_type(w << 16, F32)
    return hi, lo


def _norm_matmul_kernel(x_ref, *rest, n_rope, moe):
    rest = list(rest)
    if moe:
        route_ref, y1_ref, y2_ref = rest[:3]
        x_out_ref = rest.pop()
        rest = rest[3:]
    g_ref, w_ref = rest[:2]
    if n_rope:
        cos_ref, sa_ref, sb_ref, o_ref = rest[2:]
    else:
        (o_ref,) = rest[2:]
    x = x_ref[...]
    if moe:
        route = route_ref[...]
        y1 = jnp.concatenate(_unpack_bf16_pairs(y1_ref[...]), axis=1)
        y2 = jnp.concatenate(_unpack_bf16_pairs(y2_ref[...]), axis=1)
        x = x + (route[:, 2:3] * y1 + route[:, 3:4] * y2)
        x_out_ref[...] = x
    xn = (x * lax.rsqrt(jnp.mean(x * x, axis=-1, keepdims=True) + RMS_EPS) * g_ref[...]).astype(BF16)
    for j in range(o_ref.shape[1] // TN):
        cols = slice(j * TN, (j + 1) * TN)
        acc = jnp.dot(xn, w_ref[:, cols], preferred_element_type=F32)
        if j < n_rope:
            reps = TN // cos_ref.shape[1]
            c = jnp.tile(cos_ref[...], (1, reps))
            sa = jnp.tile(sa_ref[...], (1, reps))
            sb = jnp.tile(sb_ref[...], (1, reps))
            nxt = pltpu.roll(acc, TN - 1, axis=1)
            prv = pltpu.roll(acc, 1, axis=1)
            acc = acc * c + nxt * sa + prv * sb
        o_ref[:, cols] = acc.astype(o_ref.dtype)


def _norm_matmul(x, g, w, rope=None, n_rope=0, seq=None, moe=None):
    t, d = x.shape
    n = w.shape[1]
    row = lambda i: (i, 0)
    in_specs = [pl.BlockSpec((TM, d), row)]
    args = [x]
    if moe is not None:
        route, yg = moe
        slot2 = t // TM
        in_specs += [pl.BlockSpec((TM, LANES), row), pl.BlockSpec((TM, d // 2), row),
                     pl.BlockSpec((TM, d // 2), lambda i: (slot2 + i, 0))]
        args += [route, yg, yg]
    in_specs += [
        pl.BlockSpec((1, d), lambda i: (0, 0)),
        pl.BlockSpec((d, n), lambda i: (0, 0), pipeline_mode=pl.Buffered(1)),
    ]
    args += [g.reshape(1, d), w]
    if rope is not None:
        nb = seq // TM
        dk = rope[0].shape[1]
        in_specs += [pl.BlockSpec((TM, dk), lambda i: (i % nb, 0))] * 3
        args += list(rope)
    out_specs = [pl.BlockSpec((TM, n), row)]
    out_shape = [jax.ShapeDtypeStruct((t, n), BF16)]
    if moe is not None:
        out_specs.append(pl.BlockSpec((TM, d), row))
        out_shape.append(jax.ShapeDtypeStruct((t, d), F32))
    out = pl.pallas_call(
        functools.partial(_norm_matmul_kernel, n_rope=n_rope, moe=moe is not None),
        grid=(t // TM,),
        in_specs=in_specs,
        out_specs=out_specs,
        out_shape=out_shape,
        compiler_params=_cparams(("parallel",)),
    )(*args)
    return out if moe is not None else out[0]


def _diff_attn_kernel(q_ref, k_ref, v_ref, bias_ref, lam_ref, g_ref, o_ref,
                      vt_ref, m_ref, acc_ref, s_ref, *, lambda_init):
    seq = q_ref.shape[0]
    hp, _, _, tq = bias_ref.shape
    dv = q_ref.shape[1] // hp
    dh = dv // 2
    nq = seq // tq

    def head(hh):
        return slice(hh * dv, (hh + 1) * dv)

    extra = vt_ref.shape[2] - dv
    ones_row = jnp.where(lax.broadcasted_iota(jnp.int32, (extra, tq), 0) == 0, 1.0, 0.0).astype(BF16)
    for hh in range(hp):
        for j in range(nq):
            vt_ref[hh, j, :dv, :] = v_ref[j * tq:(j + 1) * tq, head(hh)].astype(F32).T.astype(BF16)
            vt_ref[hh, j, dv:, :] = ones_row

    lq = lam_ref[...]
    lam = (jnp.exp(jnp.sum(lq[0:1] * lq[1:2], axis=-1, keepdims=True))
           - jnp.exp(jnp.sum(lq[2:3] * lq[3:4], axis=-1, keepdims=True)) + lambda_init)
    lane = lax.broadcasted_iota(jnp.int32, (tq, dv), 1)

    def query_maps(qi):
        qrows = pl.ds(pl.multiple_of(qi * tq, tq), tq)
        qs = []
        for hh in range(hp):
            q = q_ref[qrows, head(hh)]
            zero = jnp.zeros_like(q)
            qs.append(jnp.concatenate([jnp.where(lane < dh, q, zero), jnp.where(lane >= dh, q, zero)],
                                      axis=0))
        return qs

    def score_dots(j, qs):
        rows = pl.ds(pl.multiple_of(j * tq, tq), tq)
        for hh in range(hp):
            s_ref[hh] = lax.dot_general(k_ref[rows, head(hh)], qs[hh], (((1,), (1,)), ((), ())),
                                        preferred_element_type=F32)

    score_dots(0, query_maps(0))

    def q_body(qi, carry):
        qrows = pl.ds(pl.multiple_of(qi * tq, tq), tq)
        qs = query_maps(qi)
        qs_next = query_maps(jnp.minimum(qi + 1, nq - 1))

        m_ref[...] = jnp.full(m_ref.shape, -jnp.inf, F32)
        acc_ref[...] = jnp.zeros(acc_ref.shape, F32)

        def step(j, tile, prefetch):
            probs, alphas = [], []
            for hh in range(hp):
                s = s_ref[hh]
                if tile is not None:
                    bias = bias_ref[hh, tile]
                    s = s + jnp.concatenate([bias, bias], axis=1)
                m_old = m_ref[hh]
                m_new = jnp.maximum(m_old, jnp.max(s, axis=0, keepdims=True))
                alpha = jnp.exp2(m_old - m_new)
                m_ref[hh] = m_new
                probs.append(jnp.exp2(s - m_new).astype(BF16))
                alphas.append(alpha)
            if prefetch == "block":
                score_dots(j + 1, qs)
            else:
                score_dots(0, qs_next)
            for hh in range(hp):
                acc_ref[hh] = alphas[hh] * acc_ref[hh] + jnp.dot(vt_ref[hh, j], probs[hh],
                                                                 preferred_element_type=F32)

        def far_body(j, c):
            step(j, None, "block")
            return c

        lax.fori_loop(0, jnp.maximum(qi - 1, 0), far_body, 0)

        @pl.when(qi >= 1)
        def _():
            step(qi - 1, 1, "block")

        step(qi, 0, "tile")

        for hh in range(hp):
            acc = acc_ref[hh]
            on = acc[:dv] / acc[dv:dv + 1]
            ot = on[:, :tq] - lam * on[:, tq:]
            ot = ot * lax.rsqrt(jnp.mean(ot * ot, axis=0, keepdims=True) + SUBLN_EPS)
            o_ref[qrows, head(hh)] = (ot.T * g_ref[...] * (1.0 - lambda_init)).astype(o_ref.dtype)
        return carry

    lax.fori_loop(0, nq, q_body, 0)


def _t5_bucket(rel):
    max_exact = NUM_BUCKETS // 2
    n = jnp.maximum(rel, 0)
    is_small = n < max_exact
    nf = jnp.maximum(n, max_exact).astype(F32)
    large = max_exact + (jnp.log(nf / max_exact) / math.log(MAX_DISTANCE / max_exact)
                         * (NUM_BUCKETS - max_exact)).astype(jnp.int32)
    large = jnp.minimum(large, NUM_BUCKETS - 1)
    return jnp.where(is_small, n, large)


def _bias_tiles(rel_table):
    heads = rel_table.shape[1]
    span = 2 * TQ + 1
    bucket = _t5_bucket(jnp.arange(span))
    onehot = (bucket[:, None] == jnp.arange(NUM_BUCKETS)[None, :]).astype(F32)
    table = (rel_table.astype(F32) - rel_table[NUM_BUCKETS - 1].astype(F32)[None, :]) * LOG2E
    by_rel = jnp.dot(onehot, table, precision=lax.Precision.HIGHEST).T
    toep = jnp.tile(by_rel, (1, TQ))[:, :TQ * (span - 1)].reshape(heads, TQ, span - 1)
    kj = jnp.arange(TQ)[:, None]
    qi = jnp.arange(TQ)[None, :]
    diag = jnp.where((qi >= kj)[None], toep[:, :, :TQ], -jnp.inf)
    return jnp.stack([diag, toep[:, :, TQ:]], axis=1)


def _diff_attention(qkv, rel_table, lam_qk, subln_g, lambda_init, batch, seq, heads):
    t = qkv.shape[0]
    d = qkv.shape[1] // 3
    dv = d // heads
    hp = ATTN_HEADS
    assert TQ + 1 >= MAX_DISTANCE and seq % TQ == 0 and dv == LANES and heads % hp == 0
    nq = seq // TQ
    groups = heads // hp
    bias = _bias_tiles(rel_table)
    return pl.pallas_call(
        functools.partial(_diff_attn_kernel, lambda_init=lambda_init),
        grid=(batch, groups),
        in_specs=[
            pl.BlockSpec((seq, hp * dv), lambda b, g: (b, g)),
            pl.BlockSpec((seq, hp * dv), lambda b, g: (b, groups + g)),
            pl.BlockSpec((seq, hp * dv), lambda b, g: (b, 2 * groups + g)),
            pl.BlockSpec((hp, 2, TQ, TQ), lambda b, g: (g, 0, 0, 0)),
            pl.BlockSpec(lam_qk.shape, lambda b, g: (0, 0)),
            pl.BlockSpec((1, dv), lambda b, g: (0, 0)),
        ],
        out_specs=pl.BlockSpec((seq, hp * dv), lambda b, g: (b, g)),
        out_shape=jax.ShapeDtypeStruct((t, d), BF16),
        scratch_shapes=[
            pltpu.VMEM((hp, nq, dv + BF16_SUBLANES, TQ), BF16),
            pltpu.VMEM((hp, 1, 2 * TQ), F32),
            pltpu.VMEM((hp, dv + BF16_SUBLANES, 2 * TQ), F32),
            pltpu.VMEM((hp, TQ, 2 * TQ), F32),
        ],
        compiler_params=_cparams(("parallel", "parallel")),
    )(qkv, qkv, qkv, bias, lam_qk.astype(F32), subln_g.reshape(1, dv).astype(F32))


def _retention_kernel(q_ref, k_ref, v_ref, dec_ref, tab_ref, o_ref, state_ref):
    seq = q_ref.shape[0]
    hp, c, _ = dec_ref.shape
    dk = q_ref.shape[1] // hp
    dv = v_ref.shape[1] // hp
    heads = range(hp)
    state_ref[...] = jnp.zeros(state_ref.shape, F32)

    def body(n, carry):
        rows = pl.ds(pl.multiple_of(n * c, c), c)
        q = [q_ref[rows, hh * dk:(hh + 1) * dk] for hh in heads]
        k = [k_ref[rows, hh * dk:(hh + 1) * dk] for hh in heads]
        v = [v_ref[rows, hh * dv:(hh + 1) * dv] for hh in heads]
        tab = [tab_ref[hh] for hh in heads]
        st = [state_ref[hh] for hh in heads]
        s = [lax.dot_general(q[hh], k[hh], (((1,), (1,)), ((), ())), preferred_element_type=F32)
             for hh in heads]
        cross = [jnp.dot(q[hh], st[hh].astype(BF16), preferred_element_type=F32) for hh in heads]
        kt = [(k[hh].astype(F32) * tab[hh][:, 1:2]).T.astype(BF16) for hh in heads]
        inner = [jnp.dot((s[hh] * dec_ref[hh]).astype(BF16), v[hh], preferred_element_type=F32)
                 for hh in heads]
        for hh in heads:
            state_ref[hh] = st[hh] * tab[hh][0:1, 2:3] + jnp.dot(kt[hh], v[hh], preferred_element_type=F32)
        for hh in heads:
            o = inner[hh] + cross[hh] * tab[hh][:, 0:1]
            o = o * lax.rsqrt(jnp.mean(o * o, axis=-1, keepdims=True) + RMS_EPS)
            o_ref[rows, hh * dv:(hh + 1) * dv] = o.astype(o_ref.dtype)
        return carry

    lax.fori_loop(0, seq // c, body, 0)


def _retention_tables(heads):
    c = RET_C
    log_gamma = jnp.log(1.0 - 2.0 ** (-5.0 - jnp.arange(heads, dtype=F32)))
    pos = jnp.arange(c, dtype=F32)
    dist = pos[:, None] - pos[None, :]
    inner_decay = jnp.where(dist[None] >= 0,
                            jnp.exp(jnp.maximum(dist, 0.0)[None] * log_gamma[:, None, None]), 0.0)
    q_decay = jnp.exp((pos + 1.0)[None] * log_gamma[:, None])
    k_decay = jnp.exp((c - 1.0 - pos)[None] * log_gamma[:, None])
    chunk_decay = jnp.broadcast_to(jnp.exp(c * log_gamma)[:, None], (heads, c))
    tab = jnp.stack([q_decay, k_decay, chunk_decay], axis=-1)
    tab = jnp.pad(tab, ((0, 0), (0, 0), (0, LANES - 3)))
    return inner_decay, tab


def _retention(qkvg, batch, seq, qk_dim, v_dim):
    t = qkvg.shape[0]
    dk = qk_dim // RET_HEADS
    dvh = v_dim // RET_HEADS
    inner_decay, tab = _retention_tables(RET_HEADS)
    hp = RET_HP
    assert RET_HEADS % hp == 0
    groups = RET_HEADS // hp
    kb = qk_dim // (hp * dk)
    vb = 2 * qk_dim // (hp * dvh)
    return pl.pallas_call(
        _retention_kernel,
        grid=(batch, groups),
        in_specs=[
            pl.BlockSpec((seq, hp * dk), lambda b, g: (b, g)),
            pl.BlockSpec((seq, hp * dk), lambda b, g: (b, kb + g)),
            pl.BlockSpec((seq, hp * dvh), lambda b, g: (b, vb + g)),
            pl.BlockSpec((hp, RET_C, RET_C), lambda b, g: (g, 0, 0)),
            pl.BlockSpec((hp, RET_C, LANES), lambda b, g: (g, 0, 0)),
        ],
        out_specs=pl.BlockSpec((seq, hp * dvh), lambda b, g: (b, g)),
        out_shape=jax.ShapeDtypeStruct((t, v_dim), BF16),
        scratch_shapes=[pltpu.VMEM((hp, dk, dvh), F32)],
        compiler_params=_cparams(("parallel", "parallel")),
    )(qkvg, qkvg, qkvg, inner_decay, tab)


def _out_router_kernel(*refs, gated, n_groups, n_experts):
    if gated:
        a_ref, gate_ref, w_ref, x_ref, nf_ref, wr_ref, br_ref = refs[:7]
        x1_ref, hn_ref, route_ref, route_t_ref, cnt_ref, carry_ref = refs[7:]
    else:
        a_ref, w_ref, x_ref, nf_ref, wr_ref, br_ref = refs[:6]
        x1_ref, hn_ref, route_ref, route_t_ref, cnt_ref, carry_ref = refs[6:]
    epg = n_experts // n_groups
    tm = x_ref.shape[0] // ROUTER_SPLIT

    @pl.when(pl.program_id(0) == 0)
    def _():
        carry_ref[...] = jnp.zeros(carry_ref.shape, F32)

    x1_parts = []
    for part in range(ROUTER_SPLIT):
        rows = slice(part * tm, (part + 1) * tm)
        a = a_ref[rows, :]
        if gated:
            gate = gate_ref[rows, :]
            a = gate * jax.nn.sigmoid(gate) * a
        x1_parts.append(x_ref[rows, :] + jnp.dot(a, w_ref[...], preferred_element_type=F32))
    for part in range(ROUTER_SPLIT):
        rows = slice(part * tm, (part + 1) * tm)
        _route_rows(x1_parts[part], rows, nf_ref, wr_ref, br_ref, x1_ref, hn_ref, route_ref, route_t_ref,
                    cnt_ref, carry_ref, n_groups=n_groups, n_experts=n_experts, epg=epg)


def _route_rows(x1, rows, nf_ref, wr_ref, br_ref, x1_ref, hn_ref, route_ref, route_t_ref, cnt_ref,
                carry_ref, *, n_groups, n_experts, epg):
    tm = x1.shape[0]
    x1_ref[rows, :] = x1
    hn = x1 * lax.rsqrt(jnp.mean(x1 * x1, axis=-1, keepdims=True) + RMS_EPS) * nf_ref[...]
    hn_ref[rows, :] = _pack_bf16_pairs(hn)

    hn_hi = hn.astype(BF16)
    hn_lo = (hn - hn_hi.astype(F32)).astype(BF16)
    logits = (jnp.dot(hn_hi, wr_ref[0], preferred_element_type=F32)
              + jnp.dot(hn_lo, wr_ref[0], preferred_element_type=F32)
              + jnp.dot(hn_hi, wr_ref[1], preferred_element_type=F32)) + br_ref[...]

    n_rows = -(-(n_groups + n_experts) // SUBLANES) * SUBLANES
    lt = logits.T[:n_rows, :]
    row = lax.broadcasted_iota(jnp.int32, lt.shape, 0)
    big = jnp.int32(LANES)
    neg = jnp.float32(-jnp.inf)

    gl = jnp.where(row < n_groups, lt, neg)
    gmax = jnp.max(gl, axis=0, keepdims=True)
    g_p = 1.0 / jnp.sum(jnp.exp(gl - gmax), axis=0, keepdims=True)
    g_idx = jnp.min(jnp.where(gl == gmax, row, big), axis=0, keepdims=True)

    lo = n_groups + g_idx * epg
    el = jnp.where((row >= lo) & (row < lo + epg), lt, neg)
    v1 = jnp.max(el, axis=0, keepdims=True)
    i1 = jnp.min(jnp.where(el == v1, row, big), axis=0, keepdims=True)
    el2 = jnp.where(row == i1, neg, el)
    v2 = jnp.max(el2, axis=0, keepdims=True)
    i2 = jnp.min(jnp.where(el2 == v2, row, big), axis=0, keepdims=True)
    tt = jnp.exp(v2 - v1)
    w1 = g_p / (1.0 + tt)
    w2 = g_p * tt / (1.0 + tt)
    e1 = i1 - n_groups
    e2 = i2 - n_groups

    erow = lax.broadcasted_iota(jnp.int32, (n_experts, tm), 0)
    oh1 = erow == e1
    oh2 = erow == e2
    onehot = jnp.where(oh1 | oh2, 1.0, 0.0)
    before = lax.broadcasted_iota(jnp.int32, (tm, tm), 0) < lax.broadcasted_iota(jnp.int32, (tm, tm), 1)
    tri = jnp.where(before, 1.0, 0.0).astype(BF16)
    carry = carry_ref[...]
    base = jnp.dot(onehot.astype(BF16), tri, preferred_element_type=F32) + carry[:, 0:1]
    rank1 = jnp.sum(jnp.where(oh1, base, 0.0), axis=0, keepdims=True)
    rank2 = jnp.sum(jnp.where(oh2, base, 0.0), axis=0, keepdims=True)
    carry = carry + jnp.sum(onehot, axis=1, keepdims=True)
    carry_ref[...] = carry
    cnt_ref[...] = carry

    frow = lax.broadcasted_iota(jnp.int32, (LANES, tm), 0)
    out = jnp.zeros((LANES, tm), F32)
    for idx, val in enumerate((e1.astype(F32), e2.astype(F32), w1, w2, rank1, rank2)):
        out = jnp.where(frow == idx, val, out)
    route_t_ref[:, rows] = out[:route_t_ref.shape[0], :]
    route_ref[rows, :] = out.T


def _out_router(a, gate, w, x, nf, w_group, b_group, w_expert, b_expert):
    t, d = x.shape
    k = w.shape[0]
    n_groups = w_group.shape[1]
    n_experts = w_expert.shape[1]
    assert n_groups + n_experts <= LANES
    pad = LANES - n_groups - n_experts
    wr = jnp.pad(jnp.concatenate([w_group, w_expert], axis=1).astype(F32), ((0, 0), (0, pad)))
    wr_hi = wr.astype(BF16)
    wr = jnp.stack([wr_hi, (wr - wr_hi.astype(F32)).astype(BF16)])
    br = jnp.pad(jnp.concatenate([b_group, b_expert]).astype(F32), (0, pad)).reshape(1, LANES)
    gated = gate is not None
    row = lambda i: (i, 0)
    fixed = lambda i: (0, 0)
    in_specs = [pl.BlockSpec((TM, k), row)]
    args = [a]
    if gated:
        gate_arr, gate_col = gate
        in_specs.append(pl.BlockSpec((TM, k), lambda i: (i, gate_col)))
        args.append(gate_arr)
    in_specs += [
        pl.BlockSpec((k, d), fixed),
        pl.BlockSpec((TM, d), row),
        pl.BlockSpec((1, d), fixed),
        pl.BlockSpec((2, d, LANES), lambda i: (0, 0, 0)),
        pl.BlockSpec((1, LANES), fixed),
    ]
    args += [w, x, nf.reshape(1, d), wr, br]
    return pl.pallas_call(
        functools.partial(_out_router_kernel, gated=gated, n_groups=n_groups, n_experts=n_experts),
        grid=(t // TM,),
        in_specs=in_specs,
        out_specs=[
            pl.BlockSpec((TM, d), row),
            pl.BlockSpec((TM, d // 2), row),
            pl.BlockSpec((TM, LANES), row),
            pl.BlockSpec((SUBLANES, TM), lambda i: (0, i)),
            pl.BlockSpec((n_experts, LANES), fixed),
        ],
        out_shape=[
            jax.ShapeDtypeStruct((t, d), F32),
            jax.ShapeDtypeStruct((t, d // 2), jnp.uint32),
            jax.ShapeDtypeStruct((t, LANES), F32),
            jax.ShapeDtypeStruct((SUBLANES, t), F32),
            jax.ShapeDtypeStruct((n_experts, LANES), F32),
        ],
        scratch_shapes=[pltpu.VMEM((n_experts, LANES), F32)],
        compiler_params=_cparams(("arbitrary",)),
    )(*args)


def _sc_mesh():
    return plsc.VectorSubcoreMesh(core_axis_name="c", subcore_axis_name="s",
                                  num_cores=SC_CORES, num_subcores=SC_SUBCORES)


def _sc_worker():
    return lax.axis_index("s") * SC_CORES + lax.axis_index("c")


def _sc_scatter_rows(src, d1, d2, n_rows):
    t, d = src.shape
    workers = SC_CORES * SC_SUBCORES
    per_w = t // workers
    n_ch = per_w // SC_CHUNK
    assert t % (workers * SC_CHUNK) == 0
    idx = jnp.stack([d1, d2]).reshape(2, workers, n_ch, SC_CHUNK)

    @functools.partial(
        pl.kernel, mesh=_sc_mesh(),
        out_type=jax.ShapeDtypeStruct((n_rows, d), src.dtype),
        scratch_types=[pltpu.VMEM((2, n_ch, SC_CHUNK), jnp.int32),
                       pltpu.VMEM((SC_CHUNK, d), src.dtype),
                       pltpu.VMEM((SC_CHUNK, d), src.dtype),
                       pltpu.SemaphoreType.DMA, pltpu.SemaphoreType.DMA],
    )
    def scatter(src_hbm, idx_hbm, out_hbm, idx_v, rows_a, rows_b, sem_a, sem_b):
        wid = _sc_worker()
        base = wid * per_w
        for s in range(2):
            pltpu.sync_copy(idx_hbm.at[s, wid], idx_v.at[s])

        def move(c, rows_v, sem):
            pltpu.sync_copy(src_hbm.at[pl.ds(base + c * SC_CHUNK, SC_CHUNK)], rows_v)
            return [pltpu.async_copy(rows_v, out_hbm.at[idx_v.at[s, c]], sem) for s in range(2)]

        @pl.loop(0, n_ch, step=2)
        def _(c):
            first = move(c, rows_a, sem_a)
            second = move(c + 1, rows_b, sem_b)
            for cp in first + second:
                cp.wait()

    return scatter(src, idx)


def _sc_pack_weights(ws, layer):
    workers = SC_CORES * SC_SUBCORES
    shapes = [w.shape[1:] for w in ws]
    rows = [n_e * k for n_e, k, _ in shapes]
    chunks = [SC_PACK_BYTES // (4 * n) for _, _, n in shapes]
    for r, ch, (_, _, n) in zip(rows, chunks, shapes):
        assert r % (workers * ch) == 0 and (n // 2) % SC_LANES == 0
    scratch = []
    for ch, (_, _, n) in zip(chunks, shapes):
        scratch += [pltpu.VMEM((ch, n), F32), pltpu.VMEM((ch, n // 2), jnp.uint32)]

    @functools.partial(
        pl.kernel, mesh=_sc_mesh(),
        out_type=[jax.ShapeDtypeStruct((r, n // 2), jnp.uint32) for r, (_, _, n) in zip(rows, shapes)],
        scratch_types=scratch,
        compiler_params=pltpu.CompilerParams(needs_layout_passes=False),
    )
    def pack(*refs):
        w_refs, o_refs, bufs = refs[:len(ws)], refs[len(ws):2 * len(ws)], refs[2 * len(ws):]
        wid = _sc_worker()
        for i, (w_hbm, o_hbm) in enumerate(zip(w_refs, o_refs)):
            inb, outb = bufs[2 * i], bufs[2 * i + 1]
            ch, h = chunks[i], shapes[i][2] // 2
            per_w = rows[i] // workers
            base = wid * per_w

            @pl.loop(0, per_w // ch)
            def _(c):
                pltpu.sync_copy(w_hbm.at[pl.ds(layer * rows[i] + base + c * ch, ch)], inb)

                @pl.loop(0, ch)
                def _(row):
                    for col in range(0, h, SC_LANES):
                        a = inb[row, pl.ds(col, SC_LANES)]
                        b = inb[row, pl.ds(h + col, SC_LANES)]
                        packed = plsc.pack(a, b, format=plsc.PackFormat.INTERLEAVED)
                        outb[row, pl.ds(col, SC_LANES)] = plsc.bitcast(packed, jnp.uint32)

                pltpu.sync_copy(outb, o_hbm.at[pl.ds(base + c * ch, ch)])

    outs = pack(*[w.reshape(-1, w.shape[-1]) for w in ws])
    return [o.reshape(n_e, k, n // 2) for o, (n_e, k, n) in zip(outs, shapes)]


def _sc_gather_rows(table, idx):
    b = idx.shape[0]
    d = table.shape[1]
    workers = SC_CORES * SC_SUBCORES
    per_w = b // workers
    n_ch = per_w // SC_CHUNK
    assert b % (workers * SC_CHUNK * 2) == 0

    @functools.partial(
        pl.kernel, mesh=_sc_mesh(),
        out_type=jax.ShapeDtypeStruct((b, d), table.dtype),
        scratch_types=[pltpu.VMEM((per_w,), jnp.int32),
                       pltpu.VMEM((SC_CHUNK, d), table.dtype),
                       pltpu.VMEM((SC_CHUNK, d), table.dtype),
                       pltpu.SemaphoreType.DMA, pltpu.SemaphoreType.DMA],
    )
    def gather(table_hbm, idx_hbm, out_hbm, idx_v, rows_a, rows_b, sem_a, sem_b):
        base = _sc_worker() * per_w
        pltpu.sync_copy(idx_hbm.at[pl.ds(base, per_w)], idx_v)

        def fetch(c, rows_v, sem):
            return pltpu.async_copy(table_hbm.at[idx_v.at[pl.ds(c * SC_CHUNK, SC_CHUNK)]], rows_v, sem)

        def store(c, rows_v):
            pltpu.sync_copy(rows_v, out_hbm.at[pl.ds(base + c * SC_CHUNK, SC_CHUNK)])

        @pl.loop(0, n_ch, step=2)
        def _(c):
            first = fetch(c, rows_a, sem_a)
            second = fetch(c + 1, rows_b, sem_b)
            first.wait()
            store(c, rows_a)
            second.wait()
            store(c + 1, rows_b)

    return gather(table, idx)


def _expert_kernel(fb_ref, nblk_ref, nv_ref, nu_ref, xs_ref, wg_ref, wu_ref, wd_ref, ys_ref,
                   wgb_ref, wub_ref, wdb_ref, xbuf, ybuf, xsem, ysem):
    e = pl.program_id(0)
    rb = xbuf.shape[1]
    nb = ys_ref.shape[0] // rb
    n_used = nu_ref[0]

    def x_copy(g, slot):
        rows = pl.ds(pl.multiple_of(g * rb, rb), rb)
        return pltpu.make_async_copy(xs_ref.at[rows], xbuf.at[slot], xsem.at[slot])

    def y_copy(g, slot):
        rows = pl.ds(pl.multiple_of(g * rb, rb), rb)
        return pltpu.make_async_copy(ybuf.at[slot], ys_ref.at[rows], ysem.at[slot])

    @pl.when(e == 0)
    def _():
        x_copy(0, 0).start()

    @pl.when(nblk_ref[e] > 0)
    def _():
        for src, dst in ((wg_ref, wgb_ref), (wu_ref, wub_ref), (wd_ref, wdb_ref)):
            words = src[0]
            half = words.shape[1]
            dst[:, :half] = lax.bitcast_convert_type(words << 16, F32).astype(BF16)
            dst[:, half:] = lax.bitcast_convert_type(words & jnp.uint32(0xFFFF0000), F32).astype(BF16)

    def block(b, carry):
        g = fb_ref[e] + b
        slot = lax.rem(g, 2)
        x_copy(g, slot).wait()

        @pl.when(g + 1 < n_used)
        def _():
            x_copy(g + 1, 1 - slot).start()

        @pl.when(g >= 2)
        def _():
            y_copy(g - 2, slot).wait()

        valid = lax.broadcasted_iota(jnp.int32, (rb, 1), 0) < nv_ref[g]
        x_hi, x_lo = _unpack_bf16_pairs(jnp.where(valid, xbuf[slot], jnp.uint32(0)))
        x_hi, x_lo = x_hi.astype(BF16), x_lo.astype(BF16)
        h = x_hi.shape[1]

        def in_dot(w_ref):
            return (jnp.dot(x_hi, w_ref[:h, :], preferred_element_type=F32)
                    + jnp.dot(x_lo, w_ref[h:, :], preferred_element_type=F32))

        gate = in_dot(wgb_ref)
        up = in_dot(wub_ref)
        hmid = (gate * jax.nn.sigmoid(gate) * up).astype(BF16)
        ybuf[slot] = _pack_bf16_pairs(jnp.dot(hmid, wdb_ref[...], preferred_element_type=F32))
        y_copy(g, slot).start()
        return carry

    lax.fori_loop(0, nblk_ref[e], block, 0)

    @pl.when(e == pl.num_programs(0) - 1)
    def _():
        @pl.when(n_used >= 2)
        def _():
            y_copy(n_used - 2, lax.rem(n_used, 2)).wait()

        y_copy(n_used - 1, lax.rem(n_used - 1, 2)).wait()
        ybuf[0] = jnp.zeros(ybuf.shape[1:], ybuf.dtype)

        def tail_start(g, carry):
            y_copy(g, 0).start()
            return carry

        def tail_wait(g, carry):
            y_copy(g, 0).wait()
            return carry

        lax.fori_loop(n_used, nb, tail_start, 0)
        lax.fori_loop(n_used, nb, tail_wait, 0)


def _experts(xs, first_block, n_blocks, n_valid, n_used, w_gate, w_up, w_down):
    n_rows, dp = xs.shape
    d = 2 * dp
    n_experts, ff = w_gate.shape[0], 2 * w_gate.shape[2]
    wsel = lambda e, *_: (e, 0, 0)
    grid_spec = pltpu.PrefetchScalarGridSpec(
        num_scalar_prefetch=4,
        grid=(n_experts,),
        in_specs=[
            pl.BlockSpec(memory_space=pl.ANY),
            pl.BlockSpec((1, d, ff // 2), wsel),
            pl.BlockSpec((1, d, ff // 2), wsel),
            pl.BlockSpec((1, ff, d // 2), wsel),
        ],
        out_specs=pl.BlockSpec(memory_space=pl.ANY),
        scratch_shapes=[pltpu.VMEM((d, ff), BF16), pltpu.VMEM((d, ff), BF16), pltpu.VMEM((ff, d), BF16),
                        pltpu.VMEM((2, ROW_BLOCK, dp), xs.dtype), pltpu.VMEM((2, ROW_BLOCK, dp), xs.dtype),
                        pltpu.SemaphoreType.DMA((2,)), pltpu.SemaphoreType.DMA((2,))],
    )
    return pl.pallas_call(
        _expert_kernel,
        grid_spec=grid_spec,
        out_shape=jax.ShapeDtypeStruct((n_rows, dp), xs.dtype),
        compiler_params=_cparams(("arbitrary",)),
    )(first_block, n_blocks, n_valid, n_used, xs, w_gate, w_up, w_down)


def _final_combine_kernel(x_ref, route_ref, y1_ref, y2_ref, nf_ref, o_ref):
    route = route_ref[...]
    y1 = jnp.concatenate(_unpack_bf16_pairs(y1_ref[...]), axis=1)
    y2 = jnp.concatenate(_unpack_bf16_pairs(y2_ref[...]), axis=1)
    y = x_ref[...] + (route[:, 2:3] * y1 + route[:, 3:4] * y2)
    o_ref[...] = y * lax.rsqrt(jnp.mean(y * y, axis=-1, keepdims=True) + RMS_EPS) * nf_ref[...]


def _final_combine(x1, route, yg, nf):
    t, d = x1.shape
    slot2 = t // TM
    row = lambda i: (i, 0)
    return pl.pallas_call(
        _final_combine_kernel,
        grid=(t // TM,),
        in_specs=[
            pl.BlockSpec((TM, d), row),
            pl.BlockSpec((TM, LANES), row),
            pl.BlockSpec((TM, d // 2), row),
            pl.BlockSpec((TM, d // 2), lambda i: (slot2 + i, 0)),
            pl.BlockSpec((1, d), lambda i: (0, 0)),
        ],
        out_specs=pl.BlockSpec((TM, d), row),
        out_shape=jax.ShapeDtypeStruct((t, d), F32),
        compiler_params=_cparams(("parallel",)),
    )(x1, route, yg, yg, nf.reshape(1, d).astype(F32))


def _moe(hn, route_t, cnt, w_gate, w_up, w_down):
    t, d = hn.shape
    n_experts = w_gate.shape[0]
    ids = jnp.arange(n_experts, dtype=jnp.int32)
    e1, e2, rank1, rank2 = (route_t[f].astype(jnp.int32) for f in (0, 1, 4, 5))
    counts = cnt[:, 0].astype(jnp.int32)
    padded = ((counts + ROW_BLOCK - 1) // ROW_BLOCK) * ROW_BLOCK
    pad_end = jnp.cumsum(padded)
    pad_start = pad_end - padded

    def dest(e, rank):
        return jnp.sum(jnp.where(e[:, None] == ids[None, :], pad_start[None, :], 0), axis=1) + rank

    d1 = dest(e1, rank1)
    d2 = dest(e2, rank2)
    nb = (t * TOP_K_INNER) // ROW_BLOCK + n_experts
    n_used = (pad_end[-1:] // ROW_BLOCK).astype(jnp.int32)
    first_row = jnp.arange(nb, dtype=jnp.int32) * ROW_BLOCK
    block_e = jnp.sum((pad_end[None, :] <= first_row[:, None]).astype(jnp.int32), axis=1)
    block_e = jnp.minimum(block_e, n_experts - 1)
    row_end = jnp.sum(jnp.where(block_e[:, None] == ids[None, :], (pad_start + counts)[None, :], 0), axis=1)
    n_valid = jnp.clip(row_end - first_row, 0, ROW_BLOCK).astype(jnp.int32)
    xs = _sc_scatter_rows(hn, d1, d2, nb * ROW_BLOCK)
    ys = _experts(xs, (pad_start // ROW_BLOCK).astype(jnp.int32), (padded // ROW_BLOCK).astype(jnp.int32),
                  n_valid, n_used, w_gate, w_up, w_down)
    return _sc_gather_rows(ys, jnp.concatenate([d1, d2]))


def _diff_lambda_init(layer_idx):
    return 0.8 - 0.6 * math.exp(-0.3 * layer_idx)


def _rope_tables(seq, dk):
    angle = 1.0 / (10000.0 ** jnp.linspace(0.0, 1.0, dk // 2, dtype=F32))
    angle = jnp.repeat(angle, 2)
    phase = jnp.arange(seq, dtype=F32)[:, None] * angle[None, :]
    sin, cos = jnp.sin(phase), jnp.cos(phase)
    even = (jnp.arange(dk) % 2 == 0)[None, :]
    return cos, jnp.where(even, -sin, 0.0), jnp.where(even, 0.0, sin)


def kernel(x, rel_bias_table, norm_mix, norm_ffn, norm_final, diff_w_in, diff_lambda_qk, diff_subln,
           diff_w_out, ret_w_in, ret_w_out, moe_w_group, moe_b_group, moe_w_expert, moe_b_expert,
           moe_w_gate, moe_w_up, moe_w_down):
    batch, seq, d = x.shape
    depth = norm_mix.shape[0]
    t = batch * seq
    heads = rel_bias_table.shape[1]
    dh = diff_lambda_qk.shape[-1]
    qk_dim = (ret_w_in.shape[2] - 2 * ret_w_out.shape[1]) // 2
    v_dim = ret_w_out.shape[1]
    dk = qk_dim // RET_HEADS
    xt = x.reshape(t, d)
    moe = None
    expert_w = [_sc_pack_weights((moe_w_gate, moe_w_up, moe_w_down), i) for i in range(depth)]
    for i in range(depth):
        j = i // N_MIXERS
        if i % N_MIXERS == 0:
            col_scale = jnp.concatenate([jnp.full((d,), dh ** -0.5 * LOG2E, F32), jnp.ones((2 * d,), F32)])
            w_in = (diff_w_in[j] * col_scale[None, :]).astype(BF16)
            qkv = _norm_matmul(xt, norm_mix[i], w_in, moe=moe)
            if moe is not None:
                qkv, xt = qkv
            mix = _diff_attention(qkv, rel_bias_table, diff_lambda_qk[j], diff_subln[j],
                                  _diff_lambda_init(i), batch, seq, heads)
            gate = None
            w_out = diff_w_out[j].astype(BF16)
        else:
            n_in = ret_w_in.shape[2]
            col_scale = jnp.concatenate([jnp.ones((qk_dim,), F32), jnp.full((qk_dim,), dk ** -0.5, F32),
                                         jnp.ones((n_in - 2 * qk_dim,), F32)])
            w_in = (ret_w_in[j] * col_scale[None, :]).astype(BF16)
            qkvg = _norm_matmul(xt, norm_mix[i], w_in, rope=_rope_tables(seq, dk),
                                n_rope=2 * qk_dim // TN, seq=seq, moe=moe)
            if moe is not None:
                qkvg, xt = qkvg
            mix = _retention(qkvg, batch, seq, qk_dim, v_dim)
            gate = (qkvg, (2 * qk_dim + v_dim) // v_dim)
            w_out = ret_w_out[j].astype(BF16)
        x1, hn, route, route_t, cnt = _out_router(mix, gate, w_out, xt, norm_ffn[i], moe_w_group[i],
                                                  moe_b_group[i], moe_w_expert[i], moe_b_expert[i])
        xt = x1
        moe = (route, _moe(hn, route_t, cnt, *expert_w[i]))
    return _final_combine(xt, moe[0], moe[1], norm_final).reshape(batch, seq, d)
```

```python
import functools
import math

import jax
import jax.numpy as jnp
from jax import lax
from jax.experimental import pallas as pl
from jax.experimental.pallas import tpu as pltpu
from jax.experimental.pallas import tpu_sc as plsc

F32 = jnp.float32
BF16 = jnp.bfloat16

N_MIXERS = 2
NUM_BUCKETS = 32
MAX_DISTANCE = 128
RET_HEADS = 4
TOP_K_INNER = 2
RMS_EPS = 1e-6
SUBLN_EPS = 1e-5

LANES = 128
SUBLANES = 8
BF16_SUBLANES = 16
LOG2E = 1.4426950408889634
VMEM_LIMIT = 56 * 1024 * 1024
SC_CORES = 2
SC_SUBCORES = 16
SC_LANES = 16
SC_PACK_BYTES = 64 * 1024

TM = 512
TN = 1024
TQ = 256
ATTN_HEADS = 4
RET_C = 256
RET_HP = 2
ROUTER_SPLIT = 2
ROW_BLOCK = 512
SC_CHUNK = 64


def _cparams(sem):
    return pltpu.CompilerParams(dimension_semantics=sem, vmem_limit_bytes=VMEM_LIMIT)


def _pack_bf16_pairs(x):
    h = x.shape[1] // 2
    hi = lax.bitcast_convert_type(x[:, :h].astype(BF16).astype(F32), jnp.uint32)
    lo = lax.bitcast_convert_type(x[:, h:].astype(BF16).astype(F32), jnp.uint32)
    return hi | (lo >> 16)


def _unpack_bf16_pairs(w):
    hi = lax.bitcast_convert_type(w & jnp.uint32(0xFFFF0000), F32)
    lo = lax.bitcast_convert_type(w << 16, F32)
    return hi, lo


def _norm_matmul_kernel(x_ref, *rest, n_rope, moe):
    rest = list(rest)
    if moe:
        route_ref, y1_ref, y2_ref = rest[:3]
        x_out_ref = rest.pop()
        rest = rest[3:]
    g_ref, w_ref = rest[:2]
    if n_rope:
        cos_ref, sa_ref, sb_ref, o_ref = rest[2:]
    else:
        (o_ref,) = rest[2:]
    x = x_ref[...]
    if moe:
        route = route_ref[...]
        y1 = jnp.concatenate(_unpack_bf16_pairs(y1_ref[...]), axis=1)
        y2 = jnp.concatenate(_unpack_bf16_pairs(y2_ref[...]), axis=1)
        x = x + (route[:, 2:3] * y1 + route[:, 3:4] * y2)
        x_out_ref[...] = x
    xn = (x * lax.rsqrt(jnp.mean(x * x, axis=-1, keepdims=True) + RMS_EPS) * g_ref[...]).astype(BF16)
    for j in range(o_ref.shape[1] // TN):
        cols = slice(j * TN, (j + 1) * TN)
        acc = jnp.dot(xn, w_ref[:, cols], preferred_element_type=F32)
        if j < n_rope:
            reps = TN // cos_ref.shape[1]
            c = jnp.tile(cos_ref[...], (1, reps))
            sa = jnp.tile(sa_ref[...], (1, reps))
            sb = jnp.tile(sb_ref[...], (1, reps))
            nxt = pltpu.roll(acc, TN - 1, axis=1)
            prv = pltpu.roll(acc, 1, axis=1)
            acc = acc * c + nxt * sa + prv * sb
        o_ref[:, cols] = acc.astype(o_ref.dtype)


def _norm_matmul(x, g, w, rope=None, n_rope=0, seq=None, moe=None):
    t, d = x.shape
    n = w.shape[1]
    row = lambda i: (i, 0)
    in_specs = [pl.BlockSpec((TM, d), row)]
    args = [x]
    if moe is not None:
        route, yg = moe
        slot2 = t // TM
        in_specs += [pl.BlockSpec((TM, LANES), row), pl.BlockSpec((TM, d // 2), row),
                     pl.BlockSpec((TM, d // 2), lambda i: (slot2 + i, 0))]
        args += [route, yg, yg]
    in_specs += [
        pl.BlockSpec((1, d), lambda i: (0, 0)),
        pl.BlockSpec((d, n), lambda i: (0, 0), pipeline_mode=pl.Buffered(1)),
    ]
    args += [g.reshape(1, d), w]
    if rope is not None:
        nb = seq // TM
        dk = rope[0].shape[1]
        in_specs += [pl.BlockSpec((TM, dk), lambda i: (i % nb, 0))] * 3
        args += list(rope)
    out_specs = [pl.BlockSpec((TM, n), row)]
    out_shape = [jax.ShapeDtypeStruct((t, n), BF16)]
    if moe is not None:
        out_specs.append(pl.BlockSpec((TM, d), row))
        out_shape.append(jax.ShapeDtypeStruct((t, d), F32))
    out = pl.pallas_call(
        functools.partial(_norm_matmul_kernel, n_rope=n_rope, moe=moe is not None),
        grid=(t // TM,),
        in_specs=in_specs,
        out_specs=out_specs,
        out_shape=out_shape,
        compiler_params=_cparams(("parallel",)),
    )(*args)
    return out if moe is not None else out[0]


def _diff_attn_kernel(q_ref, k_ref, v_ref, bias_ref, lam_ref, g_ref, o_ref,
                      vt_ref, m_ref, acc_ref, s_ref, *, lambda_init):
    seq = q_ref.shape[0]
    hp, _, _, tq = bias_ref.shape
    dv = q_ref.shape[1] // hp
    dh = dv // 2
    nq = seq // tq

    def head(hh):
        return slice(hh * dv, (hh + 1) * dv)

    extra = vt_ref.shape[2] - dv
    ones_row = jnp.where(lax.broadcasted_iota(jnp.int32, (extra, tq), 0) == 0, 1.0, 0.0).astype(BF16)
    for hh in range(hp):
        for j in range(nq):
            vt_ref[hh, j, :dv, :] = v_ref[j * tq:(j + 1) * tq, head(hh)].astype(F32).T.astype(BF16)
            vt_ref[hh, j, dv:, :] = ones_row

    lq = lam_ref[...]
    lam = (jnp.exp(jnp.sum(lq[0:1] * lq[1:2], axis=-1, keepdims=True))
           - jnp.exp(jnp.sum(lq[2:3] * lq[3:4], axis=-1, keepdims=True)) + lambda_init)
    lane = lax.broadcasted_iota(jnp.int32, (tq, dv), 1)

    def query_maps(qi):
        qrows = pl.ds(pl.multiple_of(qi * tq, tq), tq)
        qs = []
        for hh in range(hp):
            q = q_ref[qrows, head(hh)]
            zero = jnp.zeros_like(q)
            qs.append(jnp.concatenate([jnp.where(lane < dh, q, zero), jnp.where(lane >= dh, q, zero)],
                                      axis=0))
        return qs

    def score_dots(j, qs):
        rows = pl.ds(pl.multiple_of(j * tq, tq), tq)
        for hh in range(hp):
            s_ref[hh] = lax.dot_general(k_ref[rows, head(hh)], qs[hh], (((1,), (1,)), ((), ())),
                                        preferred_element_type=F32)

    score_dots(0, query_maps(0))

    def q_body(qi, carry):
        qrows = pl.ds(pl.multiple_of(qi * tq, tq), tq)
        qs = query_maps(qi)
        qs_next = query_maps(jnp.minimum(qi + 1, nq - 1))

        m_ref[...] = jnp.full(m_ref.shape, -jnp.inf, F32)
        acc_ref[...] = jnp.zeros(acc_ref.shape, F32)

        def step(j, tile, prefetch):
            probs, alphas = [], []
            for hh in range(hp):
                s = s_ref[hh]
                if tile is not None:
                    bias = bias_ref[hh, tile]
                    s = s + jnp.concatenate([bias, bias], axis=1)
                m_old = m_ref[hh]
                m_new = jnp.maximum(m_old, jnp.max(s, axis=0, keepdims=True))
                alpha = jnp.exp2(m_old - m_new)
                m_ref[hh] = m_new
                probs.append(jnp.exp2(s - m_new).astype(BF16))
                alphas.append(alpha)
            if prefetch == "block":
                score_dots(j + 1, qs)
            else:
                score_dots(0, qs_next)
            for hh in range(hp):
                acc_ref[hh] = alphas[hh] * acc_ref[hh] + jnp.dot(vt_ref[hh, j], probs[hh],
                                                                 preferred_element_type=F32)

        def far_body(j, c):
            step(j, None, "block")
            return c

        lax.fori_loop(0, jnp.maximum(qi - 1, 0), far_body, 0)

        @pl.when(qi >= 1)
        def _():
            step(qi - 1, 1, "block")

        step(qi, 0, "tile")

        for hh in range(hp):
            acc = acc_ref[hh]
            on = acc[:dv] / acc[dv:dv + 1]
            ot = on[:, :tq] - lam * on[:, tq:]
            ot = ot * lax.rsqrt(jnp.mean(ot * ot, axis=0, keepdims=True) + SUBLN_EPS)
            o_ref[qrows, head(hh)] = (ot.T * g_ref[...] * (1.0 - lambda_init)).astype(o_ref.dtype)
        return carry

    lax.fori_loop(0, nq, q_body, 0)


def _t5_bucket(rel):
    max_exact = NUM_BUCKETS // 2
    n = jnp.maximum(rel, 0)
    is_small = n < max_exact
    nf = jnp.maximum(n, max_exact).astype(F32)
    large = max_exact + (jnp.log(nf / max_exact) / math.log(MAX_DISTANCE / max_exact)
                         * (NUM_BUCKETS - max_exact)).astype(jnp.int32)
    large = jnp.minimum(large, NUM_BUCKETS - 1)
    return jnp.where(is_small, n, large)


def _bias_tiles(rel_table):
    heads = rel_table.shape[1]
    span = 2 * TQ + 1
    bucket = _t5_bucket(jnp.arange(span))
    onehot = (bucket[:, None] == jnp.arange(NUM_BUCKETS)[None, :]).astype(F32)
    table = (rel_table.astype(F32) - rel_table[NUM_BUCKETS - 1].astype(F32)[None, :]) * LOG2E
    by_rel = jnp.dot(onehot, table, precision=lax.Precision.HIGHEST).T
    toep = jnp.tile(by_rel, (1, TQ))[:, :TQ * (span - 1)].reshape(heads, TQ, span - 1)
    kj = jnp.arange(TQ)[:, None]
    qi = jnp.arange(TQ)[None, :]
    diag = jnp.where((qi >= kj)[None], toep[:, :, :TQ], -jnp.inf)
    return jnp.stack([diag, toep[:, :, TQ:]], axis=1)


def _diff_attention(qkv, rel_table, lam_qk, subln_g, lambda_init, batch, seq, heads):
    t = qkv.shape[0]
    d = qkv.shape[1] // 3
    dv = d // heads
    hp = ATTN_HEADS
    assert TQ + 1 >= MAX_DISTANCE and seq % TQ == 0 and dv == LANES and heads % hp == 0
    nq = seq // TQ
    groups = heads // hp
    bias = _bias_tiles(rel_table)
    return pl.pallas_call(
        functools.partial(_diff_attn_kernel, lambda_init=lambda_init),
        grid=(batch, groups),
        in_specs=[
            pl.BlockSpec((seq, hp * dv), lambda b, g: (b, g)),
            pl.BlockSpec((seq, hp * dv), lambda b, g: (b, groups + g)),
            pl.BlockSpec((seq, hp * dv), lambda b, g: (b, 2 * groups + g)),
            pl.BlockSpec((hp, 2, TQ, TQ), lambda b, g: (g, 0, 0, 0)),
            pl.BlockSpec(lam_qk.shape, lambda b, g: (0, 0)),
            pl.BlockSpec((1, dv), lambda b, g: (0, 0)),
        ],
        out_specs=pl.BlockSpec((seq, hp * dv), lambda b, g: (b, g)),
        out_shape=jax.ShapeDtypeStruct((t, d), BF16),
        scratch_shapes=[
            pltpu.VMEM((hp, nq, dv + BF16_SUBLANES, TQ), BF16),
            pltpu.VMEM((hp, 1, 2 * TQ), F32),
            pltpu.VMEM((hp, dv + BF16_SUBLANES, 2 * TQ), F32),
            pltpu.VMEM((hp, TQ, 2 * TQ), F32),
        ],
        compiler_params=_cparams(("parallel", "parallel")),
    )(qkv, qkv, qkv, bias, lam_qk.astype(F32), subln_g.reshape(1, dv).astype(F32))


def _retention_kernel(q_ref, k_ref, v_ref, dec_ref, tab_ref, o_ref, state_ref):
    seq = q_ref.shape[0]
    hp, c, _ = dec_ref.shape
    dk = q_ref.shape[1] // hp
    dv = v_ref.shape[1] // hp
    heads = range(hp)
    state_ref[...] = jnp.zeros(state_ref.shape, F32)

    def body(n, carry):
        rows = pl.ds(pl.multiple_of(n * c, c), c)
        q = [q_ref[rows, hh * dk:(hh + 1) * dk] for hh in heads]
        k = [k_ref[rows, hh * dk:(hh + 1) * dk] for hh in heads]
        v = [v_ref[rows, hh * dv:(hh + 1) * dv] for hh in heads]
        tab = [tab_ref[hh] for hh in heads]
        st = [state_ref[hh] for hh in heads]
        s = [lax.dot_general(q[hh], k[hh], (((1,), (1,)), ((), ())), preferred_element_type=F32)
             for hh in heads]
        cross = [jnp.dot(q[hh], st[hh].astype(BF16), preferred_element_type=F32) for hh in heads]
        kt = [(k[hh].astype(F32) * tab[hh][:, 1:2]).T.astype(BF16) for hh in heads]
        inner = [jnp.dot((s[hh] * dec_ref[hh]).astype(BF16), v[hh], preferred_element_type=F32)
                 for hh in heads]
        for hh in heads:
            state_ref[hh] = st[hh] * tab[hh][0:1, 2:3] + jnp.dot(kt[hh], v[hh], preferred_element_type=F32)
        for hh in heads:
            o = inner[hh] + cross[hh] * tab[hh][:, 0:1]
            o = o * lax.rsqrt(jnp.mean(o * o, axis=-1, keepdims=True) + RMS_EPS)
            o_ref[rows, hh * dv:(hh + 1) * dv] = o.astype(o_ref.dtype)
        return carry

    lax.fori_loop(0, seq // c, body, 0)


def _retention_tables(heads):
    c = RET_C
    log_gamma = jnp.log(1.0 - 2.0 ** (-5.0 - jnp.arange(heads, dtype=F32)))
    pos = jnp.arange(c, dtype=F32)
    dist = pos[:, None] - pos[None, :]
    inner_decay = jnp.where(dist[None] >= 0,
                            jnp.exp(jnp.maximum(dist, 0.0)[None] * log_gamma[:, None, None]), 0.0)
    q_decay = jnp.exp((pos + 1.0)[None] * log_gamma[:, None])
    k_decay = jnp.exp((c - 1.0 - pos)[None] * log_gamma[:, None])
    chunk_decay = jnp.broadcast_to(jnp.exp(c * log_gamma)[:, None], (heads, c))
    tab = jnp.stack([q_decay, k_decay, chunk_decay], axis=-1)
    tab = jnp.pad(tab, ((0, 0), (0, 0), (0, LANES - 3)))
    return inner_decay, tab


def _retention(qkvg, batch, seq, qk_dim, v_dim):
    t = qkvg.shape[0]
    dk = qk_dim // RET_HEADS
    dvh = v_dim // RET_HEADS
    inner_decay, tab = _retention_tables(RET_HEADS)
    hp = RET_HP
    assert RET_HEADS % hp == 0
    groups = RET_HEADS // hp
    kb = qk_dim // (hp * dk)
    vb = 2 * qk_dim // (hp * dvh)
    return pl.pallas_call(
        _retention_kernel,
        grid=(batch, groups),
        in_specs=[
            pl.BlockSpec((seq, hp * dk), lambda b, g: (b, g)),
            pl.BlockSpec((seq, hp * dk), lambda b, g: (b, kb + g)),
            pl.BlockSpec((seq, hp * dvh), lambda b, g: (b, vb + g)),
            pl.BlockSpec((hp, RET_C, RET_C), lambda b, g: (g, 0, 0)),
            pl.BlockSpec((hp, RET_C, LANES), lambda b, g: (g, 0, 0)),
        ],
        out_specs=pl.BlockSpec((seq, hp * dvh), lambda b, g: (b, g)),
        out_shape=jax.ShapeDtypeStruct((t, v_dim), BF16),
        scratch_shapes=[pltpu.VMEM((hp, dk, dvh), F32)],
        compiler_params=_cparams(("parallel", "parallel")),
    )(qkvg, qkvg, qkvg, inner_decay, tab)


def _out_router_kernel(*refs, gated, n_groups, n_experts):
    if gated:
        a_ref, gate_ref, w_ref, x_ref, nf_ref, wr_ref, br_ref = refs[:7]
        x1_ref, hn_ref, route_ref, route_t_ref, cnt_ref, carry_ref = refs[7:]
    else:
        a_ref, w_ref, x_ref, nf_ref, wr_ref, br_ref = refs[:6]
        x1_ref, hn_ref, route_ref, route_t_ref, cnt_ref, carry_ref = refs[6:]
    epg = n_experts // n_groups
    tm = x_ref.shape[0] // ROUTER_SPLIT

    @pl.when(pl.program_id(0) == 0)
    def _():
        carry_ref[...] = jnp.zeros(carry_ref.shape, F32)

    x1_parts = []
    for part in range(ROUTER_SPLIT):
        rows = slice(part * tm, (part + 1) * tm)
        a = a_ref[rows, :]
        if gated:
            gate = gate_ref[rows, :]
            a = gate * jax.nn.sigmoid(gate) * a
        x1_parts.append(x_ref[rows, :] + jnp.dot(a, w_ref[...], preferred_element_type=F32))
    for part in range(ROUTER_SPLIT):
        rows = slice(part * tm, (part + 1) * tm)
        _route_rows(x1_parts[part], rows, nf_ref, wr_ref, br_ref, x1_ref, hn_ref, route_ref, route_t_ref,
                    cnt_ref, carry_ref, n_groups=n_groups, n_experts=n_experts, epg=epg)


def _route_rows(x1, rows, nf_ref, wr_ref, br_ref, x1_ref, hn_ref, route_ref, route_t_ref, cnt_ref,
                carry_ref, *, n_groups, n_experts, epg):
    tm = x1.shape[0]
    x1_ref[rows, :] = x1
    hn = x1 * lax.rsqrt(jnp.mean(x1 * x1, axis=-1, keepdims=True) + RMS_EPS) * nf_ref[...]
    hn_ref[rows, :] = _pack_bf16_pairs(hn)

    hn_hi = hn.astype(BF16)
    hn_lo = (hn - hn_hi.astype(F32)).astype(BF16)
    logits = (jnp.dot(hn_hi, wr_ref[0], preferred_element_type=F32)
              + jnp.dot(hn_lo, wr_ref[0], preferred_element_type=F32)
              + jnp.dot(hn_hi, wr_ref[1], preferred_element_type=F32)) + br_ref[...]

    n_rows = -(-(n_groups + n_experts) // SUBLANES) * SUBLANES
    lt = logits.T[:n_rows, :]
    row = lax.broadcasted_iota(jnp.int32, lt.shape, 0)
    big = jnp.int32(LANES)
    neg = jnp.float32(-jnp.inf)

    gl = jnp.where(row < n_groups, lt, neg)
    gmax = jnp.max(gl, axis=0, keepdims=True)
    g_p = 1.0 / jnp.sum(jnp.exp(gl - gmax), axis=0, keepdims=True)
    g_idx = jnp.min(jnp.where(gl == gmax, row, big), axis=0, keepdims=True)

    lo = n_groups + g_idx * epg
    el = jnp.where((row >= lo) & (row < lo + epg), lt, neg)
    v1 = jnp.max(el, axis=0, keepdims=True)
    i1 = jnp.min(jnp.where(el == v1, row, big), axis=0, keepdims=True)
    el2 = jnp.where(row == i1, neg, el)
    v2 = jnp.max(el2, axis=0, keepdims=True)
    i2 = jnp.min(jnp.where(el2 == v2, row, big), axis=0, keepdims=True)
    tt = jnp.exp(v2 - v1)
    w1 = g_p / (1.0 + tt)
    w2 = g_p * tt / (1.0 + tt)
    e1 = i1 - n_groups
    e2 = i2 - n_groups

    erow = lax.broadcasted_iota(jnp.int32, (n_experts, tm), 0)
    oh1 = erow == e1
    oh2 = erow == e2
    onehot = jnp.where(oh1 | oh2, 1.0, 0.0)
    before = lax.broadcasted_iota(jnp.int32, (tm, tm), 0) < lax.broadcasted_iota(jnp.int32, (tm, tm), 1)
    tri = jnp.where(before, 1.0, 0.0).astype(BF16)
    carry = carry_ref[...]
    base = jnp.dot(onehot.astype(BF16), tri, preferred_element_type=F32) + carry[:, 0:1]
    rank1 = jnp.sum(jnp.where(oh1, base, 0.0), axis=0, keepdims=True)
    rank2 = jnp.sum(jnp.where(oh2, base, 0.0), axis=0, keepdims=True)
    carry = carry + jnp.sum(onehot, axis=1, keepdims=True)
    carry_ref[...] = carry
    cnt_ref[...] = carry

    frow = lax.broadcasted_iota(jnp.int32, (LANES, tm), 0)
    out = jnp.zeros((LANES, tm), F32)
    for idx, val in enumerate((e1.astype(F32), e2.astype(F32), w1, w2, rank1, rank2)):
        out = jnp.where(frow == idx, val, out)
    route_t_ref[:, rows] = out[:route_t_ref.shape[0], :]
    route_ref[rows, :] = out.T


def _out_router(a, gate, w, x, nf, w_group, b_group, w_expert, b_expert):
    t, d = x.shape
    k = w.shape[0]
    n_groups = w_group.shape[1]
    n_experts = w_expert.shape[1]
    assert n_groups + n_experts <= LANES
    pad = LANES - n_groups - n_experts
    wr = jnp.pad(jnp.concatenate([w_group, w_expert], axis=1).astype(F32), ((0, 0), (0, pad)))
    wr_hi = wr.astype(BF16)
    wr = jnp.stack([wr_hi, (wr - wr_hi.astype(F32)).astype(BF16)])
    br = jnp.pad(jnp.concatenate([b_group, b_expert]).astype(F32), (0, pad)).reshape(1, LANES)
    gated = gate is not None
    row = lambda i: (i, 0)
    fixed = lambda i: (0, 0)
    in_specs = [pl.BlockSpec((TM, k), row)]
    args = [a]
    if gated:
        gate_arr, gate_col = gate
        in_specs.append(pl.BlockSpec((TM, k), lambda i: (i, gate_col)))
        args.append(gate_arr)
    in_specs += [
        pl.BlockSpec((k, d), fixed),
        pl.BlockSpec((TM, d), row),
        pl.BlockSpec((1, d), fixed),
        pl.BlockSpec((2, d, LANES), lambda i: (0, 0, 0)),
        pl.BlockSpec((1, LANES), fixed),
    ]
    args += [w, x, nf.reshape(1, d), wr, br]
    return pl.pallas_call(
        functools.partial(_out_router_kernel, gated=gated, n_groups=n_groups, n_experts=n_experts),
        grid=(t // TM,),
        in_specs=in_specs,
        out_specs=[
            pl.BlockSpec((TM, d), row),
            pl.BlockSpec((TM, d // 2), row),
            pl.BlockSpec((TM, LANES), row),
            pl.BlockSpec((SUBLANES, TM), lambda i: (0, i)),
            pl.BlockSpec((n_experts, LANES), fixed),
        ],
        out_shape=[
            jax.ShapeDtypeStruct((t, d), F32),
            jax.ShapeDtypeStruct((t, d // 2), jnp.uint32),
            jax.ShapeDtypeStruct((t, LANES), F32),
            jax.ShapeDtypeStruct((SUBLANES, t), F32),
            jax.ShapeDtypeStruct((n_experts, LANES), F32),
        ],
        scratch_shapes=[pltpu.VMEM((n_experts, LANES), F32)],
        compiler_params=_cparams(("arbitrary",)),
    )(*args)


def _sc_mesh():
    return plsc.VectorSubcoreMesh(core_axis_name="c", subcore_axis_name="s",
                                  num_cores=SC_CORES, num_subcores=SC_SUBCORES)


def _sc_worker():
    return lax.axis_index("s") * SC_CORES + lax.axis_index("c")


def _sc_scatter_rows(src, d1, d2, n_rows):
    t, d = src.shape
    workers = SC_CORES * SC_SUBCORES
    per_w = t // workers
    n_ch = per_w // SC_CHUNK
    assert t % (workers * SC_CHUNK) == 0
    idx = jnp.stack([d1, d2]).reshape(2, workers, n_ch, SC_CHUNK)

    @functools.partial(
        pl.kernel, mesh=_sc_mesh(),
        out_type=jax.ShapeDtypeStruct((n_rows, d), src.dtype),
        scratch_types=[pltpu.VMEM((2, n_ch, SC_CHUNK), jnp.int32),
                       pltpu.VMEM((SC_CHUNK, d), src.dtype),
                       pltpu.VMEM((SC_CHUNK, d), src.dtype),
                       pltpu.SemaphoreType.DMA, pltpu.SemaphoreType.DMA],
    )
    def scatter(src_hbm, idx_hbm, out_hbm, idx_v, rows_a, rows_b, sem_a, sem_b):
        wid = _sc_worker()
        base = wid * per_w
        for s in range(2):
            pltpu.sync_copy(idx_hbm.at[s, wid], idx_v.at[s])

        def move(c, rows_v, sem):
            pltpu.sync_copy(src_hbm.at[pl.ds(base + c * SC_CHUNK, SC_CHUNK)], rows_v)
            return [pltpu.async_copy(rows_v, out_hbm.at[idx_v.at[s, c]], sem) for s in range(2)]

        @pl.loop(0, n_ch, step=2)
        def _(c):
            first = move(c, rows_a, sem_a)
            second = move(c + 1, rows_b, sem_b)
            for cp in first + second:
                cp.wait()

    return scatter(src, idx)


def _sc_pack_weights(ws, layer, after):
    workers = SC_CORES * SC_SUBCORES
    shapes = [w.shape[1:] for w in ws]
    rows = [n_e * k for n_e, k, _ in shapes]
    chunks = [SC_PACK_BYTES // (4 * n) for _, _, n in shapes]
    for r, ch, (_, _, n) in zip(rows, chunks, shapes):
        assert r % (workers * ch) == 0 and (n // 2) % SC_LANES == 0
    scratch = []
    for ch, (_, _, n) in zip(chunks, shapes):
        scratch += [pltpu.VMEM((ch, n), F32), pltpu.VMEM((ch, n // 2), jnp.uint32)]

    @functools.partial(
        pl.kernel, mesh=_sc_mesh(),
        out_type=[jax.ShapeDtypeStruct((r, n // 2), jnp.uint32) for r, (_, _, n) in zip(rows, shapes)],
        scratch_types=scratch,
        compiler_params=pltpu.CompilerParams(needs_layout_passes=False),
    )
    def pack(after_hbm, *refs):
        del after_hbm
        w_refs, o_refs, bufs = refs[:len(ws)], refs[len(ws):2 * len(ws)], refs[2 * len(ws):]
        wid = _sc_worker()
        for i, (w_hbm, o_hbm) in enumerate(zip(w_refs, o_refs)):
            inb, outb = bufs[2 * i], bufs[2 * i + 1]
            ch, h = chunks[i], shapes[i][2] // 2
            per_w = rows[i] // workers
            base = wid * per_w

            @pl.loop(0, per_w // ch)
            def _(c):
                pltpu.sync_copy(w_hbm.at[pl.ds(layer * rows[i] + base + c * ch, ch)], inb)

                @pl.loop(0, ch)
                def _(row):
                    for col in range(0, h, SC_LANES):
                        a = inb[row, pl.ds(col, SC_LANES)]
                        b = inb[row, pl.ds(h + col, SC_LANES)]
                        packed = plsc.pack(a, b, format=plsc.PackFormat.INTERLEAVED)
                        outb[row, pl.ds(col, SC_LANES)] = plsc.bitcast(packed, jnp.uint32)

                pltpu.sync_copy(outb, o_hbm.at[pl.ds(base + c * ch, ch)])

    outs = pack(after, *[w.reshape(-1, w.shape[-1]) for w in ws])
    return [o.reshape(n_e, k, n // 2) for o, (n_e, k, n) in zip(outs, shapes)]


def _sc_gather_rows(table, idx):
    b = idx.shape[0]
    d = table.shape[1]
    workers = SC_CORES * SC_SUBCORES
    per_w = b // workers
    n_ch = per_w // SC_CHUNK
    assert b % (workers * SC_CHUNK * 2) == 0

    @functools.partial(
        pl.kernel, mesh=_sc_mesh(),
        out_type=jax.ShapeDtypeStruct((b, d), table.dtype),
        scratch_types=[pltpu.VMEM((per_w,), jnp.int32),
                       pltpu.VMEM((SC_CHUNK, d), table.dtype),
                       pltpu.VMEM((SC_CHUNK, d), table.dtype),
                       pltpu.SemaphoreType.DMA, pltpu.SemaphoreType.DMA],
    )
    def gather(table_hbm, idx_hbm, out_hbm, idx_v, rows_a, rows_b, sem_a, sem_b):
        base = _sc_worker() * per_w
        pltpu.sync_copy(idx_hbm.at[pl.ds(base, per_w)], idx_v)

        def fetch(c, rows_v, sem):
            return pltpu.async_copy(table_hbm.at[idx_v.at[pl.ds(c * SC_CHUNK, SC_CHUNK)]], rows_v, sem)

        def store(c, rows_v):
            pltpu.sync_copy(rows_v, out_hbm.at[pl.ds(base + c * SC_CHUNK, SC_CHUNK)])

        @pl.loop(0, n_ch, step=2)
        def _(c):
            first = fetch(c, rows_a, sem_a)
            second = fetch(c + 1, rows_b, sem_b)
            first.wait()
            store(c, rows_a)
            second.wait()
            store(c + 1, rows_b)

    return gather(table, idx)


def _expert_kernel(fb_ref, nblk_ref, nv_ref, nu_ref, xs_ref, wg_ref, wu_ref, wd_ref, ys_ref,
                   wgb_ref, wub_ref, wdb_ref, xbuf, ybuf, xsem, ysem):
    e = pl.program_id(0)
    rb = xbuf.shape[1]
    nb = ys_ref.shape[0] // rb
    n_used = nu_ref[0]

    def x_copy(g, slot):
        rows = pl.ds(pl.multiple_of(g * rb, rb), rb)
        return pltpu.make_async_copy(xs_ref.at[rows], xbuf.at[slot], xsem.at[slot])

    def y_copy(g, slot):
        rows = pl.ds(pl.multiple_of(g * rb, rb), rb)
        return pltpu.make_async_copy(ybuf.at[slot], ys_ref.at[rows], ysem.at[slot])

    @pl.when(e == 0)
    def _():
        x_copy(0, 0).start()

    @pl.when(nblk_ref[e] > 0)
    def _():
        for src, dst in ((wg_ref, wgb_ref), (wu_ref, wub_ref), (wd_ref, wdb_ref)):
            words = src[0]
            half = words.shape[1]
            dst[:, :half] = lax.bitcast_convert_type(words << 16, F32).astype(BF16)
            dst[:, half:] = lax.bitcast_convert_type(words & jnp.uint32(0xFFFF0000), F32).astype(BF16)

    def block(b, carry):
        g = fb_ref[e] + b
        slot = lax.rem(g, 2)
        x_copy(g, slot).wait()

        @pl.when(g + 1 < n_used)
        def _():
            x_copy(g + 1, 1 - slot).start()

        @pl.when(g >= 2)
        def _():
            y_copy(g - 2, slot).wait()

        valid = lax.broadcasted_iota(jnp.int32, (rb, 1), 0) < nv_ref[g]
        x_hi, x_lo = _unpack_bf16_pairs(jnp.where(valid, xbuf[slot], jnp.uint32(0)))
        x_hi, x_lo = x_hi.astype(BF16), x_lo.astype(BF16)
        h = x_hi.shape[1]

        def in_dot(w_ref):
            return (jnp.dot(x_hi, w_ref[:h, :], preferred_element_type=F32)
                    + jnp.dot(x_lo, w_ref[h:, :], preferred_element_type=F32))

        gate = in_dot(wgb_ref)
        up = in_dot(wub_ref)
        hmid = (gate * jax.nn.sigmoid(gate) * up).astype(BF16)
        ybuf[slot] = _pack_bf16_pairs(jnp.dot(hmid, wdb_ref[...], preferred_element_type=F32))
        y_copy(g, slot).start()
        return carry

    lax.fori_loop(0, nblk_ref[e], block, 0)

    @pl.when(e == pl.num_programs(0) - 1)
    def _():
        @pl.when(n_used >= 2)
        def _():
            y_copy(n_used - 2, lax.rem(n_used, 2)).wait()

        y_copy(n_used - 1, lax.rem(n_used - 1, 2)).wait()
        ybuf[0] = jnp.zeros(ybuf.shape[1:], ybuf.dtype)

        def tail_start(g, carry):
            y_copy(g, 0).start()
            return carry

        def tail_wait(g, carry):
            y_copy(g, 0).wait()
            return carry

        lax.fori_loop(n_used, nb, tail_start, 0)
        lax.fori_loop(n_used, nb, tail_wait, 0)


def _experts(xs, first_block, n_blocks, n_valid, n_used, w_gate, w_up, w_down):
    n_rows, dp = xs.shape
    d = 2 * dp
    n_experts, ff = w_gate.shape[0], 2 * w_gate.shape[2]
    wsel = lambda e, *_: (e, 0, 0)
    grid_spec = pltpu.PrefetchScalarGridSpec(
        num_scalar_prefetch=4,
        grid=(n_experts,),
        in_specs=[
            pl.BlockSpec(memory_space=pl.ANY),
            pl.BlockSpec((1, d, ff // 2), wsel),
            pl.BlockSpec((1, d, ff // 2), wsel),
            pl.BlockSpec((1, ff, d // 2), wsel),
        ],
        out_specs=pl.BlockSpec(memory_space=pl.ANY),
        scratch_shapes=[pltpu.VMEM((d, ff), BF16), pltpu.VMEM((d, ff), BF16), pltpu.VMEM((ff, d), BF16),
                        pltpu.VMEM((2, ROW_BLOCK, dp), xs.dtype), pltpu.VMEM((2, ROW_BLOCK, dp), xs.dtype),
                        pltpu.SemaphoreType.DMA((2,)), pltpu.SemaphoreType.DMA((2,))],
    )
    return pl.pallas_call(
        _expert_kernel,
        grid_spec=grid_spec,
        out_shape=jax.ShapeDtypeStruct((n_rows, dp), xs.dtype),
        compiler_params=_cparams(("arbitrary",)),
    )(first_block, n_blocks, n_valid, n_used, xs, w_gate, w_up, w_down)


def _final_combine_kernel(x_ref, route_ref, y1_ref, y2_ref, nf_ref, o_ref):
    route = route_ref[...]
    y1 = jnp.concatenate(_unpack_bf16_pairs(y1_ref[...]), axis=1)
    y2 = jnp.concatenate(_unpack_bf16_pairs(y2_ref[...]), axis=1)
    y = x_ref[...] + (route[:, 2:3] * y1 + route[:, 3:4] * y2)
    o_ref[...] = y * lax.rsqrt(jnp.mean(y * y, axis=-1, keepdims=True) + RMS_EPS) * nf_ref[...]


def _final_combine(x1, route, yg, nf):
    t, d = x1.shape
    slot2 = t // TM
    row = lambda i: (i, 0)
    return pl.pallas_call(
        _final_combine_kernel,
        grid=(t // TM,),
        in_specs=[
            pl.BlockSpec((TM, d), row),
            pl.BlockSpec((TM, LANES), row),
            pl.BlockSpec((TM, d // 2), row),
            pl.BlockSpec((TM, d // 2), lambda i: (slot2 + i, 0)),
            pl.BlockSpec((1, d), lambda i: (0, 0)),
        ],
        out_specs=pl.BlockSpec((TM, d), row),
        out_shape=jax.ShapeDtypeStruct((t, d), F32),
        compiler_params=_cparams(("parallel",)),
    )(x1, route, yg, yg, nf.reshape(1, d).astype(F32))


def _moe(hn, route_t, cnt, w_gate, w_up, w_down):
    t, d = hn.shape
    n_experts = w_gate.shape[0]
    ids = jnp.arange(n_experts, dtype=jnp.int32)
    e1, e2, rank1, rank2 = (route_t[f].astype(jnp.int32) for f in (0, 1, 4, 5))
    counts = cnt[:, 0].astype(jnp.int32)
    padded = ((counts + ROW_BLOCK - 1) // ROW_BLOCK) * ROW_BLOCK
    pad_end = jnp.cumsum(padded)
    pad_start = pad_end - padded

    def dest(e, rank):
        return jnp.sum(jnp.where(e[:, None] == ids[None, :], pad_start[None, :], 0), axis=1) + rank

    d1 = dest(e1, rank1)
    d2 = dest(e2, rank2)
    nb = (t * TOP_K_INNER) // ROW_BLOCK + n_experts
    n_used = (pad_end[-1:] // ROW_BLOCK).astype(jnp.int32)
    first_row = jnp.arange(nb, dtype=jnp.int32) * ROW_BLOCK
    block_e = jnp.sum((pad_end[None, :] <= first_row[:, None]).astype(jnp.int32), axis=1)
    block_e = jnp.minimum(block_e, n_experts - 1)
    row_end = jnp.sum(jnp.where(block_e[:, None] == ids[None, :], (pad_start + counts)[None, :], 0), axis=1)
    n_valid = jnp.clip(row_end - first_row, 0, ROW_BLOCK).astype(jnp.int32)
    xs = _sc_scatter_rows(hn, d1, d2, nb * ROW_BLOCK)
    ys = _experts(xs, (pad_start // ROW_BLOCK).astype(jnp.int32), (padded // ROW_BLOCK).astype(jnp.int32),
                  n_valid, n_used, w_gate, w_up, w_down)
    return _sc_gather_rows(ys, jnp.concatenate([d1, d2]))


def _diff_lambda_init(layer_idx):
    return 0.8 - 0.6 * math.exp(-0.3 * layer_idx)


def _rope_tables(seq, dk):
    angle = 1.0 / (10000.0 ** jnp.linspace(0.0, 1.0, dk // 2, dtype=F32))
    angle = jnp.repeat(angle, 2)
    phase = jnp.arange(seq, dtype=F32)[:, None] * angle[None, :]
    sin, cos = jnp.sin(phase), jnp.cos(phase)
    even = (jnp.arange(dk) % 2 == 0)[None, :]
    return cos, jnp.where(even, -sin, 0.0), jnp.where(even, 0.0, sin)


def kernel(x, rel_bias_table, norm_mix, norm_ffn, norm_final, diff_w_in, diff_lambda_qk, diff_subln,
           diff_w_out, ret_w_in, ret_w_out, moe_w_group, moe_b_group, moe_w_expert, moe_b_expert,
           moe_w_gate, moe_w_up, moe_w_down):
    batch, seq, d = x.shape
    depth = norm_mix.shape[0]
    t = batch * seq
    heads = rel_bias_table.shape[1]
    dh = diff_lambda_qk.shape[-1]
    qk_dim = (ret_w_in.shape[2] - 2 * ret_w_out.shape[1]) // 2
    v_dim = ret_w_out.shape[1]
    dk = qk_dim // RET_HEADS
    xt = x.reshape(t, d)
    moe = None
    expert_ws = (moe_w_gate, moe_w_up, moe_w_down)
    expert_w = _sc_pack_weights(expert_ws, 0, xt)
    for i in range(depth):
        j = i // N_MIXERS
        if i % N_MIXERS == 0:
            col_scale = jnp.concatenate([jnp.full((d,), dh ** -0.5 * LOG2E, F32), jnp.ones((2 * d,), F32)])
            w_in = (diff_w_in[j] * col_scale[None, :]).astype(BF16)
            qkv = _norm_matmul(xt, norm_mix[i], w_in, moe=moe)
            if moe is not None:
                qkv, xt = qkv
            mix = _diff_attention(qkv, rel_bias_table, diff_lambda_qk[j], diff_subln[j],
                                  _diff_lambda_init(i), batch, seq, heads)
            gate = None
            w_out = diff_w_out[j].astype(BF16)
        else:
            n_in = ret_w_in.shape[2]
            col_scale = jnp.concatenate([jnp.ones((qk_dim,), F32), jnp.full((qk_dim,), dk ** -0.5, F32),
                                         jnp.ones((n_in - 2 * qk_dim,), F32)])
            w_in = (ret_w_in[j] * col_scale[None, :]).astype(BF16)
            qkvg = _norm_matmul(xt, norm_mix[i], w_in, rope=_rope_tables(seq, dk),
                                n_rope=2 * qk_dim // TN, seq=seq, moe=moe)
            if moe is not None:
                qkvg, xt = qkvg
            mix = _retention(qkvg, batch, seq, qk_dim, v_dim)
            gate = (qkvg, (2 * qk_dim + v_dim) // v_dim)
            w_out = ret_w_out[j].astype(BF16)
        x1, hn, route, route_t, cnt = _out_router(mix, gate, w_out, xt, norm_ffn[i], moe_w_group[i],
                                                  moe_b_group[i], moe_w_expert[i], moe_b_expert[i])
        xt = x1
        moe = (route, _moe(hn, route_t, cnt, *expert_w))
        if i + 1 < depth:
            expert_w = _sc_pack_weights(expert_ws, i + 1, moe[1])
    return _final_combine(xt, moe[0], moe[1], norm_final).reshape(batch, seq, d)
```

```python
import functools
import math

import jax
import jax.numpy as jnp
from jax import lax
from jax.experimental import pallas as pl
from jax.experimental.pallas import tpu as pltpu
from jax.experimental.pallas import tpu_sc as plsc

F32 = jnp.float32
BF16 = jnp.bfloat16

N_MIXERS = 2
NUM_BUCKETS = 32
MAX_DISTANCE = 128
RET_HEADS = 4
TOP_K_INNER = 2
RMS_EPS = 1e-6
SUBLN_EPS = 1e-5

LANES = 128
SUBLANES = 8
BF16_SUBLANES = 16
LOG2E = 1.4426950408889634
VMEM_LIMIT = 56 * 1024 * 1024
SC_CORES = 2
SC_SUBCORES = 16

TM = 512
TN = 1024
TQ = 256
ATTN_HEADS = 4
RET_C = 256
RET_HP = 2
ROUTER_SPLIT = 2
ROW_BLOCK = 512
SC_CHUNK = 64


def _cparams(sem):
    return pltpu.CompilerParams(dimension_semantics=sem, vmem_limit_bytes=VMEM_LIMIT)


def _pack_bf16_pairs(x):
    h = x.shape[1] // 2
    hi = lax.bitcast_convert_type(x[:, :h].astype(BF16).astype(F32), jnp.uint32)
    lo = lax.bitcast_convert_type(x[:, h:].astype(BF16).astype(F32), jnp.uint32)
    return hi | (lo >> 16)


def _unpack_bf16_pairs(w):
    hi = lax.bitcast_convert_type(w & jnp.uint32(0xFFFF0000), F32)
    lo = lax.bitcast_convert_type(w << 16, F32)
    return hi, lo


def _norm_matmul_kernel(x_ref, *rest, n_rope, moe):
    rest = list(rest)
    if moe:
        route_ref, y1_ref, y2_ref = rest[:3]
        x_out_ref = rest.pop()
        rest = rest[3:]
    g_ref, w_ref = rest[:2]
    if n_rope:
        cos_ref, sa_ref, sb_ref, o_ref = rest[2:]
    else:
        (o_ref,) = rest[2:]
    x = x_ref[...]
    if moe:
        route = route_ref[...]
        y1 = jnp.concatenate(_unpack_bf16_pairs(y1_ref[...]), axis=1)
        y2 = jnp.concatenate(_unpack_bf16_pairs(y2_ref[...]), axis=1)
        x = x + (route[:, 2:3] * y1 + route[:, 3:4] * y2)
        x_out_ref[...] = x
    xn = (x * lax.rsqrt(jnp.mean(x * x, axis=-1, keepdims=True) + RMS_EPS) * g_ref[...]).astype(BF16)
    for j in range(o_ref.shape[1] // TN):
        cols = slice(j * TN, (j + 1) * TN)
        acc = jnp.dot(xn, w_ref[:, cols], preferred_element_type=F32)
        if j < n_rope:
            reps = TN // cos_ref.shape[1]
            c = jnp.tile(cos_ref[...], (1, reps))
            sa = jnp.tile(sa_ref[...], (1, reps))
            sb = jnp.tile(sb_ref[...], (1, reps))
            nxt = pltpu.roll(acc, TN - 1, axis=1)
            prv = pltpu.roll(acc, 1, axis=1)
            acc = acc * c + nxt * sa + prv * sb
        o_ref[:, cols] = acc.astype(o_ref.dtype)


def _norm_matmul(x, g, w, rope=None, n_rope=0, seq=None, moe=None):
    t, d = x.shape
    n = w.shape[1]
    row = lambda i: (i, 0)
    in_specs = [pl.BlockSpec((TM, d), row)]
    args = [x]
    if moe is not None:
        route, yg = moe
        slot2 = t // TM
        in_specs += [pl.BlockSpec((TM, LANES), row), pl.BlockSpec((TM, d // 2), row),
                     pl.BlockSpec((TM, d // 2), lambda i: (slot2 + i, 0))]
        args += [route, yg, yg]
    in_specs += [
        pl.BlockSpec((1, d), lambda i: (0, 0)),
        pl.BlockSpec((d, n), lambda i: (0, 0), pipeline_mode=pl.Buffered(1)),
    ]
    args += [g.reshape(1, d), w]
    if rope is not None:
        nb = seq // TM
        dk = rope[0].shape[1]
        in_specs += [pl.BlockSpec((TM, dk), lambda i: (i % nb, 0))] * 3
        args += list(rope)
    out_specs = [pl.BlockSpec((TM, n), row)]
    out_shape = [jax.ShapeDtypeStruct((t, n), BF16)]
    if moe is not None:
        out_specs.append(pl.BlockSpec((TM, d), row))
        out_shape.append(jax.ShapeDtypeStruct((t, d), F32))
    out = pl.pallas_call(
        functools.partial(_norm_matmul_kernel, n_rope=n_rope, moe=moe is not None),
        grid=(t // TM,),
        in_specs=in_specs,
        out_specs=out_specs,
        out_shape=out_shape,
        compiler_params=_cparams(("parallel",)),
    )(*args)
    return out if moe is not None else out[0]


def _diff_attn_kernel(q_ref, k_ref, v_ref, bias_ref, lam_ref, g_ref, o_ref,
                      vt_ref, m_ref, acc_ref, s_ref, *, lambda_init):
    seq = q_ref.shape[0]
    hp, _, _, tq = bias_ref.shape
    dv = q_ref.shape[1] // hp
    dh = dv // 2
    nq = seq // tq

    def head(hh):
        return slice(hh * dv, (hh + 1) * dv)

    extra = vt_ref.shape[2] - dv
    ones_row = jnp.where(lax.broadcasted_iota(jnp.int32, (extra, tq), 0) == 0, 1.0, 0.0).astype(BF16)
    for hh in range(hp):
        for j in range(nq):
            vt_ref[hh, j, :dv, :] = v_ref[j * tq:(j + 1) * tq, head(hh)].astype(F32).T.astype(BF16)
            vt_ref[hh, j, dv:, :] = ones_row

    lq = lam_ref[...]
    lam = (jnp.exp(jnp.sum(lq[0:1] * lq[1:2], axis=-1, keepdims=True))
           - jnp.exp(jnp.sum(lq[2:3] * lq[3:4], axis=-1, keepdims=True)) + lambda_init)
    lane = lax.broadcasted_iota(jnp.int32, (tq, dv), 1)

    def query_maps(qi):
        qrows = pl.ds(pl.multiple_of(qi * tq, tq), tq)
        qs = []
        for hh in range(hp):
            q = q_ref[qrows, head(hh)]
            zero = jnp.zeros_like(q)
            qs.append(jnp.concatenate([jnp.where(lane < dh, q, zero), jnp.where(lane >= dh, q, zero)],
                                      axis=0))
        return qs

    def score_dots(j, qs):
        rows = pl.ds(pl.multiple_of(j * tq, tq), tq)
        for hh in range(hp):
            s_ref[hh] = lax.dot_general(k_ref[rows, head(hh)], qs[hh], (((1,), (1,)), ((), ())),
                                        preferred_element_type=F32)

    score_dots(0, query_maps(0))

    def q_body(qi, carry):
        qrows = pl.ds(pl.multiple_of(qi * tq, tq), tq)
        qs = query_maps(qi)
        qs_next = query_maps(jnp.minimum(qi + 1, nq - 1))

        m_ref[...] = jnp.full(m_ref.shape, -jnp.inf, F32)
        acc_ref[...] = jnp.zeros(acc_ref.shape, F32)

        def step(j, tile, prefetch):
            probs, alphas = [], []
            for hh in range(hp):
                s = s_ref[hh]
                if tile is not None:
                    bias = bias_ref[hh, tile]
                    s = s + jnp.concatenate([bias, bias], axis=1)
                m_old = m_ref[hh]
                m_new = jnp.maximum(m_old, jnp.max(s, axis=0, keepdims=True))
                alpha = jnp.exp2(m_old - m_new)
                m_ref[hh] = m_new
                probs.append(jnp.exp2(s - m_new).astype(BF16))
                alphas.append(alpha)
            if prefetch == "block":
                score_dots(j + 1, qs)
            else:
                score_dots(0, qs_next)
            for hh in range(hp):
                acc_ref[hh] = alphas[hh] * acc_ref[hh] + jnp.dot(vt_ref[hh, j], probs[hh],
                                                                 preferred_element_type=F32)

        def far_body(j, c):
            step(j, None, "block")
            return c

        lax.fori_loop(0, jnp.maximum(qi - 1, 0), far_body, 0)

        @pl.when(qi >= 1)
        def _():
            step(qi - 1, 1, "block")

        step(qi, 0, "tile")

        for hh in range(hp):
            acc = acc_ref[hh]
            on = acc[:dv] / acc[dv:dv + 1]
            ot = on[:, :tq] - lam * on[:, tq:]
            ot = ot * lax.rsqrt(jnp.mean(ot * ot, axis=0, keepdims=True) + SUBLN_EPS)
            o_ref[qrows, head(hh)] = (ot.T * g_ref[...] * (1.0 - lambda_init)).astype(o_ref.dtype)
        return carry

    lax.fori_loop(0, nq, q_body, 0)


def _t5_bucket(rel):
    max_exact = NUM_BUCKETS // 2
    n = jnp.maximum(rel, 0)
    is_small = n < max_exact
    nf = jnp.maximum(n, max_exact).astype(F32)
    large = max_exact + (jnp.log(nf / max_exact) / math.log(MAX_DISTANCE / max_exact)
                         * (NUM_BUCKETS - max_exact)).astype(jnp.int32)
    large = jnp.minimum(large, NUM_BUCKETS - 1)
    return jnp.where(is_small, n, large)


def _bias_tiles(rel_table):
    heads = rel_table.shape[1]
    span = 2 * TQ + 1
    bucket = _t5_bucket(jnp.arange(span))
    onehot = (bucket[:, None] == jnp.arange(NUM_BUCKETS)[None, :]).astype(F32)
    table = (rel_table.astype(F32) - rel_table[NUM_BUCKETS - 1].astype(F32)[None, :]) * LOG2E
    by_rel = jnp.dot(onehot, table, precision=lax.Precision.HIGHEST).T
    toep = jnp.tile(by_rel, (1, TQ))[:, :TQ * (span - 1)].reshape(heads, TQ, span - 1)
    kj = jnp.arange(TQ)[:, None]
    qi = jnp.arange(TQ)[None, :]
    diag = jnp.where((qi >= kj)[None], toep[:, :, :TQ], -jnp.inf)
    return jnp.stack([diag, toep[:, :, TQ:]], axis=1)


def _diff_attention(qkv, rel_table, lam_qk, subln_g, lambda_init, batch, seq, heads):
    t = qkv.shape[0]
    d = qkv.shape[1] // 3
    dv = d // heads
    hp = ATTN_HEADS
    assert TQ + 1 >= MAX_DISTANCE and seq % TQ == 0 and dv == LANES and heads % hp == 0
    nq = seq // TQ
    groups = heads // hp
    bias = _bias_tiles(rel_table)
    return pl.pallas_call(
        functools.partial(_diff_attn_kernel, lambda_init=lambda_init),
        grid=(batch, groups),
        in_specs=[
            pl.BlockSpec((seq, hp * dv), lambda b, g: (b, g)),
            pl.BlockSpec((seq, hp * dv), lambda b, g: (b, groups + g)),
            pl.BlockSpec((seq, hp * dv), lambda b, g: (b, 2 * groups + g)),
            pl.BlockSpec((hp, 2, TQ, TQ), lambda b, g: (g, 0, 0, 0)),
            pl.BlockSpec(lam_qk.shape, lambda b, g: (0, 0)),
            pl.BlockSpec((1, dv), lambda b, g: (0, 0)),
        ],
        out_specs=pl.BlockSpec((seq, hp * dv), lambda b, g: (b, g)),
        out_shape=jax.ShapeDtypeStruct((t, d), BF16),
        scratch_shapes=[
            pltpu.VMEM((hp, nq, dv + BF16_SUBLANES, TQ), BF16),
            pltpu.VMEM((hp, 1, 2 * TQ), F32),
            pltpu.VMEM((hp, dv + BF16_SUBLANES, 2 * TQ), F32),
            pltpu.VMEM((hp, TQ, 2 * TQ), F32),
        ],
        compiler_params=_cparams(("parallel", "parallel")),
    )(qkv, qkv, qkv, bias, lam_qk.astype(F32), subln_g.reshape(1, dv).astype(F32))


def _retention_kernel(q_ref, k_ref, v_ref, dec_ref, tab_ref, o_ref, state_ref):
    seq = q_ref.shape[0]
    hp, c, _ = dec_ref.shape
    dk = q_ref.shape[1] // hp
    dv = v_ref.shape[1] // hp
    heads = range(hp)
    state_ref[...] = jnp.zeros(state_ref.shape, F32)

    def body(n, carry):
        rows = pl.ds(pl.multiple_of(n * c, c), c)
        q = [q_ref[rows, hh * dk:(hh + 1) * dk] for hh in heads]
        k = [k_ref[rows, hh * dk:(hh + 1) * dk] for hh in heads]
        v = [v_ref[rows, hh * dv:(hh + 1) * dv] for hh in heads]
        tab = [tab_ref[hh] for hh in heads]
        st = [state_ref[hh] for hh in heads]
        s = [lax.dot_general(q[hh], k[hh], (((1,), (1,)), ((), ())), preferred_element_type=F32)
             for hh in heads]
        cross = [jnp.dot(q[hh], st[hh].astype(BF16), preferred_element_type=F32) for hh in heads]
        kt = [(k[hh].astype(F32) * tab[hh][:, 1:2]).T.astype(BF16) for hh in heads]
        inner = [jnp.dot((s[hh] * dec_ref[hh]).astype(BF16), v[hh], preferred_element_type=F32)
                 for hh in heads]
        for hh in heads:
            state_ref[hh] = st[hh] * tab[hh][0:1, 2:3] + jnp.dot(kt[hh], v[hh], preferred_element_type=F32)
        for hh in heads:
            o = inner[hh] + cross[hh] * tab[hh][:, 0:1]
            o = o * lax.rsqrt(jnp.mean(o * o, axis=-1, keepdims=True) + RMS_EPS)
            o_ref[rows, hh * dv:(hh + 1) * dv] = o.astype(o_ref.dtype)
        return carry

    lax.fori_loop(0, seq // c, body, 0)


def _retention_tables(heads):
    c = RET_C
    log_gamma = jnp.log(1.0 - 2.0 ** (-5.0 - jnp.arange(heads, dtype=F32)))
    pos = jnp.arange(c, dtype=F32)
    dist = pos[:, None] - pos[None, :]
    inner_decay = jnp.where(dist[None] >= 0,
                            jnp.exp(jnp.maximum(dist, 0.0)[None] * log_gamma[:, None, None]), 0.0)
    q_decay = jnp.exp((pos + 1.0)[None] * log_gamma[:, None])
    k_decay = jnp.exp((c - 1.0 - pos)[None] * log_gamma[:, None])
    chunk_decay = jnp.broadcast_to(jnp.exp(c * log_gamma)[:, None], (heads, c))
    tab = jnp.stack([q_decay, k_decay, chunk_decay], axis=-1)
    tab = jnp.pad(tab, ((0, 0), (0, 0), (0, LANES - 3)))
    return inner_decay, tab


def _retention(qkvg, batch, seq, qk_dim, v_dim):
    t = qkvg.shape[0]
    dk = qk_dim // RET_HEADS
    dvh = v_dim // RET_HEADS
    inner_decay, tab = _retention_tables(RET_HEADS)
    hp = RET_HP
    assert RET_HEADS % hp == 0
    groups = RET_HEADS // hp
    kb = qk_dim // (hp * dk)
    vb = 2 * qk_dim // (hp * dvh)
    return pl.pallas_call(
        _retention_kernel,
        grid=(batch, groups),
        in_specs=[
            pl.BlockSpec((seq, hp * dk), lambda b, g: (b, g)),
            pl.BlockSpec((seq, hp * dk), lambda b, g: (b, kb + g)),
            pl.BlockSpec((seq, hp * dvh), lambda b, g: (b, vb + g)),
            pl.BlockSpec((hp, RET_C, RET_C), lambda b, g: (g, 0, 0)),
            pl.BlockSpec((hp, RET_C, LANES), lambda b, g: (g, 0, 0)),
        ],
        out_specs=pl.BlockSpec((seq, hp * dvh), lambda b, g: (b, g)),
        out_shape=jax.ShapeDtypeStruct((t, v_dim), BF16),
        scratch_shapes=[pltpu.VMEM((hp, dk, dvh), F32)],
        compiler_params=_cparams(("parallel", "parallel")),
    )(qkvg, qkvg, qkvg, inner_decay, tab)


def _out_router_kernel(*refs, gated, n_groups, n_experts):
    if gated:
        a_ref, gate_ref, w_ref, x_ref, nf_ref, wr_ref, br_ref = refs[:7]
        x1_ref, hn_ref, route_ref, route_t_ref, cnt_ref, carry_ref = refs[7:]
    else:
        a_ref, w_ref, x_ref, nf_ref, wr_ref, br_ref = refs[:6]
        x1_ref, hn_ref, route_ref, route_t_ref, cnt_ref, carry_ref = refs[6:]
    epg = n_experts // n_groups
    tm = x_ref.shape[0] // ROUTER_SPLIT

    @pl.when(pl.program_id(0) == 0)
    def _():
        carry_ref[...] = jnp.zeros(carry_ref.shape, F32)

    x1_parts = []
    for part in range(ROUTER_SPLIT):
        rows = slice(part * tm, (part + 1) * tm)
        a = a_ref[rows, :]
        if gated:
            gate = gate_ref[rows, :]
            a = gate * jax.nn.sigmoid(gate) * a
        x1_parts.append(x_ref[rows, :] + jnp.dot(a, w_ref[...], preferred_element_type=F32))
    for part in range(ROUTER_SPLIT):
        rows = slice(part * tm, (part + 1) * tm)
        _route_rows(x1_parts[part], rows, nf_ref, wr_ref, br_ref, x1_ref, hn_ref, route_ref, route_t_ref,
                    cnt_ref, carry_ref, n_groups=n_groups, n_experts=n_experts, epg=epg)


def _route_rows(x1, rows, nf_ref, wr_ref, br_ref, x1_ref, hn_ref, route_ref, route_t_ref, cnt_ref,
                carry_ref, *, n_groups, n_experts, epg):
    tm = x1.shape[0]
    x1_ref[rows, :] = x1
    hn = x1 * lax.rsqrt(jnp.mean(x1 * x1, axis=-1, keepdims=True) + RMS_EPS) * nf_ref[...]
    hn_ref[rows, :] = _pack_bf16_pairs(hn)

    hn_hi = hn.astype(BF16)
    hn_lo = (hn - hn_hi.astype(F32)).astype(BF16)
    logits = (jnp.dot(hn_hi, wr_ref[0], preferred_element_type=F32)
              + jnp.dot(hn_lo, wr_ref[0], preferred_element_type=F32)
              + jnp.dot(hn_hi, wr_ref[1], preferred_element_type=F32)) + br_ref[...]

    n_rows = -(-(n_groups + n_experts) // SUBLANES) * SUBLANES
    lt = logits.T[:n_rows, :]
    row = lax.broadcasted_iota(jnp.int32, lt.shape, 0)
    big = jnp.int32(LANES)
    neg = jnp.float32(-jnp.inf)

    gl = jnp.where(row < n_groups, lt, neg)
    gmax = jnp.max(gl, axis=0, keepdims=True)
    g_p = 1.0 / jnp.sum(jnp.exp(gl - gmax), axis=0, keepdims=True)
    g_idx = jnp.min(jnp.where(gl == gmax, row, big), axis=0, keepdims=True)

    lo = n_groups + g_idx * epg
    el = jnp.where((row >= lo) & (row < lo + epg), lt, neg)
    v1 = jnp.max(el, axis=0, keepdims=True)
    i1 = jnp.min(jnp.where(el == v1, row, big), axis=0, keepdims=True)
    el2 = jnp.where(row == i1, neg, el)
    v2 = jnp.max(el2, axis=0, keepdims=True)
    i2 = jnp.min(jnp.where(el2 == v2, row, big), axis=0, keepdims=True)
    tt = jnp.exp(v2 - v1)
    w1 = g_p / (1.0 + tt)
    w2 = g_p * tt / (1.0 + tt)
    e1 = i1 - n_groups
    e2 = i2 - n_groups

    erow = lax.broadcasted_iota(jnp.int32, (n_experts, tm), 0)
    oh1 = erow == e1
    oh2 = erow == e2
    onehot = jnp.where(oh1 | oh2, 1.0, 0.0)
    before = lax.broadcasted_iota(jnp.int32, (tm, tm), 0) < lax.broadcasted_iota(jnp.int32, (tm, tm), 1)
    tri = jnp.where(before, 1.0, 0.0).astype(BF16)
    carry = carry_ref[...]
    base = jnp.dot(onehot.astype(BF16), tri, preferred_element_type=F32) + carry[:, 0:1]
    rank1 = jnp.sum(jnp.where(oh1, base, 0.0), axis=0, keepdims=True)
    rank2 = jnp.sum(jnp.where(oh2, base, 0.0), axis=0, keepdims=True)
    carry = carry + jnp.sum(onehot, axis=1, keepdims=True)
    carry_ref[...] = carry
    cnt_ref[...] = carry

    frow = lax.broadcasted_iota(jnp.int32, (LANES, tm), 0)
    out = jnp.zeros((LANES, tm), F32)
    for idx, val in enumerate((e1.astype(F32), e2.astype(F32), w1, w2, rank1, rank2)):
        out = jnp.where(frow == idx, val, out)
    route_t_ref[:, rows] = out[:route_t_ref.shape[0], :]
    route_ref[rows, :] = out.T


def _out_router(a, gate, w, x, nf, w_group, b_group, w_expert, b_expert):
    t, d = x.shape
    k = w.shape[0]
    n_groups = w_group.shape[1]
    n_experts = w_expert.shape[1]
    assert n_groups + n_experts <= LANES
    pad = LANES - n_groups - n_experts
    wr = jnp.pad(jnp.concatenate([w_group, w_expert], axis=1).astype(F32), ((0, 0), (0, pad)))
    wr_hi = wr.astype(BF16)
    wr = jnp.stack([wr_hi, (wr - wr_hi.astype(F32)).astype(BF16)])
    br = jnp.pad(jnp.concatenate([b_group, b_expert]).astype(F32), (0, pad)).reshape(1, LANES)
    gated = gate is not None
    row = lambda i: (i, 0)
    fixed = lambda i: (0, 0)
    in_specs = [pl.BlockSpec((TM, k), row)]
    args = [a]
    if gated:
        gate_arr, gate_col = gate
        in_specs.append(pl.BlockSpec((TM, k), lambda i: (i, gate_col)))
        args.append(gate_arr)
    in_specs += [
        pl.BlockSpec((k, d), fixed),
        pl.BlockSpec((TM, d), row),
        pl.BlockSpec((1, d), fixed),
        pl.BlockSpec((2, d, LANES), lambda i: (0, 0, 0)),
        pl.BlockSpec((1, LANES), fixed),
    ]
    args += [w, x, nf.reshape(1, d), wr, br]
    return pl.pallas_call(
        functools.partial(_out_router_kernel, gated=gated, n_groups=n_groups, n_experts=n_experts),
        grid=(t // TM,),
        in_specs=in_specs,
        out_specs=[
            pl.BlockSpec((TM, d), row),
            pl.BlockSpec((TM, d // 2), row),
            pl.BlockSpec((TM, LANES), row),
            pl.BlockSpec((SUBLANES, TM), lambda i: (0, i)),
            pl.BlockSpec((n_experts, LANES), fixed),
        ],
        out_shape=[
            jax.ShapeDtypeStruct((t, d), F32),
            jax.ShapeDtypeStruct((t, d // 2), jnp.uint32),
            jax.ShapeDtypeStruct((t, LANES), F32),
            jax.ShapeDtypeStruct((SUBLANES, t), F32),
            jax.ShapeDtypeStruct((n_experts, LANES), F32),
        ],
        scratch_shapes=[pltpu.VMEM((n_experts, LANES), F32)],
        compiler_params=_cparams(("arbitrary",)),
    )(*args)


def _sc_mesh():
    return plsc.VectorSubcoreMesh(core_axis_name="c", subcore_axis_name="s",
                                  num_cores=SC_CORES, num_subcores=SC_SUBCORES)


def _sc_worker():
    return lax.axis_index("s") * SC_CORES + lax.axis_index("c")


def _sc_scatter_rows(src, d1, d2, n_rows):
    t, d = src.shape
    workers = SC_CORES * SC_SUBCORES
    per_w = t // workers
    n_ch = per_w // SC_CHUNK
    assert t % (workers * SC_CHUNK) == 0
    idx = jnp.stack([d1, d2]).reshape(2, workers, n_ch, SC_CHUNK)

    @functools.partial(
        pl.kernel, mesh=_sc_mesh(),
        out_type=jax.ShapeDtypeStruct((n_rows, d), src.dtype),
        scratch_types=[pltpu.VMEM((2, n_ch, SC_CHUNK), jnp.int32),
                       pltpu.VMEM((SC_CHUNK, d), src.dtype),
                       pltpu.VMEM((SC_CHUNK, d), src.dtype),
                       pltpu.SemaphoreType.DMA, pltpu.SemaphoreType.DMA],
    )
    def scatter(src_hbm, idx_hbm, out_hbm, idx_v, rows_a, rows_b, sem_a, sem_b):
        wid = _sc_worker()
        base = wid * per_w
        for s in range(2):
            pltpu.sync_copy(idx_hbm.at[s, wid], idx_v.at[s])

        def move(c, rows_v, sem):
            pltpu.sync_copy(src_hbm.at[pl.ds(base + c * SC_CHUNK, SC_CHUNK)], rows_v)
            return [pltpu.async_copy(rows_v, out_hbm.at[idx_v.at[s, c]], sem) for s in range(2)]

        @pl.loop(0, n_ch, step=2)
        def _(c):
            first = move(c, rows_a, sem_a)
            second = move(c + 1, rows_b, sem_b)
            for cp in first + second:
                cp.wait()

    return scatter(src, idx)


def _sc_gather_rows(table, idx):
    b = idx.shape[0]
    d = table.shape[1]
    workers = SC_CORES * SC_SUBCORES
    per_w = b // workers
    n_ch = per_w // SC_CHUNK
    assert b % (workers * SC_CHUNK * 2) == 0

    @functools.partial(
        pl.kernel, mesh=_sc_mesh(),
        out_type=jax.ShapeDtypeStruct((b, d), table.dtype),
        scratch_types=[pltpu.VMEM((per_w,), jnp.int32),
                       pltpu.VMEM((SC_CHUNK, d), table.dtype),
                       pltpu.VMEM((SC_CHUNK, d), table.dtype),
                       pltpu.SemaphoreType.DMA, pltpu.SemaphoreType.DMA],
    )
    def gather(table_hbm, idx_hbm, out_hbm, idx_v, rows_a, rows_b, sem_a, sem_b):
        base = _sc_worker() * per_w
        pltpu.sync_copy(idx_hbm.at[pl.ds(base, per_w)], idx_v)

        def fetch(c, rows_v, sem):
            return pltpu.async_copy(table_hbm.at[idx_v.at[pl.ds(c * SC_CHUNK, SC_CHUNK)]], rows_v, sem)

        def store(c, rows_v):
            pltpu.sync_copy(rows_v, out_hbm.at[pl.ds(base + c * SC_CHUNK, SC_CHUNK)])

        @pl.loop(0, n_ch, step=2)
        def _(c):
            first = fetch(c, rows_a, sem_a)
            second = fetch(c + 1, rows_b, sem_b)
            first.wait()
            store(c, rows_a)
            second.wait()
            store(c + 1, rows_b)

    return gather(table, idx)


def _expert_kernel(fb_ref, nblk_ref, nv_ref, nu_ref, xs_ref, wg_ref, wu_ref, wd_ref, ys_ref,
                   wgb_ref, wub_ref, wdb_ref, xbuf, ybuf, xsem, ysem):
    e = pl.program_id(0)
    rb = xbuf.shape[1]
    nb = ys_ref.shape[0] // rb
    n_used = nu_ref[0]

    def x_copy(g, slot):
        rows = pl.ds(pl.multiple_of(g * rb, rb), rb)
        return pltpu.make_async_copy(xs_ref.at[rows], xbuf.at[slot], xsem.at[slot])

    def y_copy(g, slot):
        rows = pl.ds(pl.multiple_of(g * rb, rb), rb)
        return pltpu.make_async_copy(ybuf.at[slot], ys_ref.at[rows], ysem.at[slot])

    @pl.when(e == 0)
    def _():
        x_copy(0, 0).start()

    @pl.when(nblk_ref[e] > 0)
    def _():
        wgb_ref[...] = wg_ref[0, 0].astype(BF16)
        wub_ref[...] = wu_ref[0, 0].astype(BF16)
        wdb_ref[...] = wd_ref[0, 0].astype(BF16)

    def block(b, carry):
        g = fb_ref[e] + b
        slot = lax.rem(g, 2)
        x_copy(g, slot).wait()

        @pl.when(g + 1 < n_used)
        def _():
            x_copy(g + 1, 1 - slot).start()

        @pl.when(g >= 2)
        def _():
            y_copy(g - 2, slot).wait()

        nv = nv_ref[g]

        def ffn(rows):
            valid = lax.broadcasted_iota(jnp.int32, (rows, 1), 0) < nv
            x_hi, x_lo = _unpack_bf16_pairs(jnp.where(valid, xbuf[slot, :rows, :], jnp.uint32(0)))
            x_hi, x_lo = x_hi.astype(BF16), x_lo.astype(BF16)
            h = x_hi.shape[1]

            def in_dot(w_ref):
                return (jnp.dot(x_hi, w_ref[:h, :], preferred_element_type=F32)
                        + jnp.dot(x_lo, w_ref[h:, :], preferred_element_type=F32))

            gate = in_dot(wgb_ref)
            up = in_dot(wub_ref)
            hmid = (gate * jax.nn.sigmoid(gate) * up).astype(BF16)
            ybuf[slot, :rows, :] = _pack_bf16_pairs(jnp.dot(hmid, wdb_ref[...], preferred_element_type=F32))
            if rows < rb:
                ybuf[slot, rows:, :] = jnp.zeros((rb - rows, ybuf.shape[2]), ybuf.dtype)

        @pl.when(nv > rb // 2)
        def _():
            ffn(rb)

        @pl.when(nv <= rb // 2)
        def _():
            ffn(rb // 2)

        y_copy(g, slot).start()
        return carry

    lax.fori_loop(0, nblk_ref[e], block, 0)

    @pl.when(e == pl.num_programs(0) - 1)
    def _():
        @pl.when(n_used >= 2)
        def _():
            y_copy(n_used - 2, lax.rem(n_used, 2)).wait()

        y_copy(n_used - 1, lax.rem(n_used - 1, 2)).wait()
        ybuf[0] = jnp.zeros(ybuf.shape[1:], ybuf.dtype)

        def tail_start(g, carry):
            y_copy(g, 0).start()
            return carry

        def tail_wait(g, carry):
            y_copy(g, 0).wait()
            return carry

        lax.fori_loop(n_used, nb, tail_start, 0)
        lax.fori_loop(n_used, nb, tail_wait, 0)


def _experts(xs, first_block, n_blocks, n_valid, n_used, w_gate, w_up, w_down, layer):
    n_rows, dp = xs.shape
    d = 2 * dp
    n_experts, ff = w_gate.shape[1], w_gate.shape[3]
    wsel = lambda e, *_: (layer, e, 0, 0)
    grid_spec = pltpu.PrefetchScalarGridSpec(
        num_scalar_prefetch=4,
        grid=(n_experts,),
        in_specs=[
            pl.BlockSpec(memory_space=pl.ANY),
            pl.BlockSpec((1, 1, d, ff), wsel),
            pl.BlockSpec((1, 1, d, ff), wsel),
            pl.BlockSpec((1, 1, ff, d), wsel),
        ],
        out_specs=pl.BlockSpec(memory_space=pl.ANY),
        scratch_shapes=[pltpu.VMEM((d, ff), BF16), pltpu.VMEM((d, ff), BF16), pltpu.VMEM((ff, d), BF16),
                        pltpu.VMEM((2, ROW_BLOCK, dp), xs.dtype), pltpu.VMEM((2, ROW_BLOCK, dp), xs.dtype),
                        pltpu.SemaphoreType.DMA((2,)), pltpu.SemaphoreType.DMA((2,))],
    )
    return pl.pallas_call(
        _expert_kernel,
        grid_spec=grid_spec,
        out_shape=jax.ShapeDtypeStruct((n_rows, dp), xs.dtype),
        compiler_params=_cparams(("arbitrary",)),
    )(first_block, n_blocks, n_valid, n_used, xs, w_gate, w_up, w_down)


def _final_combine_kernel(x_ref, route_ref, y1_ref, y2_ref, nf_ref, o_ref):
    route = route_ref[...]
    y1 = jnp.concatenate(_unpack_bf16_pairs(y1_ref[...]), axis=1)
    y2 = jnp.concatenate(_unpack_bf16_pairs(y2_ref[...]), axis=1)
    y = x_ref[...] + (route[:, 2:3] * y1 + route[:, 3:4] * y2)
    o_ref[...] = y * lax.rsqrt(jnp.mean(y * y, axis=-1, keepdims=True) + RMS_EPS) * nf_ref[...]


def _final_combine(x1, route, yg, nf):
    t, d = x1.shape
    slot2 = t // TM
    row = lambda i: (i, 0)
    return pl.pallas_call(
        _final_combine_kernel,
        grid=(t // TM,),
        in_specs=[
            pl.BlockSpec((TM, d), row),
            pl.BlockSpec((TM, LANES), row),
            pl.BlockSpec((TM, d // 2), row),
            pl.BlockSpec((TM, d // 2), lambda i: (slot2 + i, 0)),
            pl.BlockSpec((1, d), lambda i: (0, 0)),
        ],
        out_specs=pl.BlockSpec((TM, d), row),
        out_shape=jax.ShapeDtypeStruct((t, d), F32),
        compiler_params=_cparams(("parallel",)),
    )(x1, route, yg, yg, nf.reshape(1, d).astype(F32))


def _moe(hn, route_t, cnt, w_gate, w_up, w_down, layer):
    t, d = hn.shape
    n_experts = w_gate.shape[1]
    ids = jnp.arange(n_experts, dtype=jnp.int32)
    e1, e2, rank1, rank2 = (route_t[f].astype(jnp.int32) for f in (0, 1, 4, 5))
    counts = cnt[:, 0].astype(jnp.int32)
    padded = ((counts + ROW_BLOCK - 1) // ROW_BLOCK) * ROW_BLOCK
    pad_end = jnp.cumsum(padded)
    pad_start = pad_end - padded

    def dest(e, rank):
        return jnp.sum(jnp.where(e[:, None] == ids[None, :], pad_start[None, :], 0), axis=1) + rank

    d1 = dest(e1, rank1)
    d2 = dest(e2, rank2)
    nb = (t * TOP_K_INNER) // ROW_BLOCK + n_experts
    n_used = (pad_end[-1:] // ROW_BLOCK).astype(jnp.int32)
    first_row = jnp.arange(nb, dtype=jnp.int32) * ROW_BLOCK
    block_e = jnp.sum((pad_end[None, :] <= first_row[:, None]).astype(jnp.int32), axis=1)
    block_e = jnp.minimum(block_e, n_experts - 1)
    row_end = jnp.sum(jnp.where(block_e[:, None] == ids[None, :], (pad_start + counts)[None, :], 0), axis=1)
    n_valid = jnp.clip(row_end - first_row, 0, ROW_BLOCK).astype(jnp.int32)
    xs = _sc_scatter_rows(hn, d1, d2, nb * ROW_BLOCK)
    ys = _experts(xs, (pad_start // ROW_BLOCK).astype(jnp.int32), (padded // ROW_BLOCK).astype(jnp.int32),
                  n_valid, n_used, w_gate, w_up, w_down, layer)
    return _sc_gather_rows(ys, jnp.concatenate([d1, d2]))


def _diff_lambda_init(layer_idx):
    return 0.8 - 0.6 * math.exp(-0.3 * layer_idx)


def _rope_tables(seq, dk):
    angle = 1.0 / (10000.0 ** jnp.linspace(0.0, 1.0, dk // 2, dtype=F32))
    angle = jnp.repeat(angle, 2)
    phase = jnp.arange(seq, dtype=F32)[:, None] * angle[None, :]
    sin, cos = jnp.sin(phase), jnp.cos(phase)
    even = (jnp.arange(dk) % 2 == 0)[None, :]
    return cos, jnp.where(even, -sin, 0.0), jnp.where(even, 0.0, sin)


def kernel(x, rel_bias_table, norm_mix, norm_ffn, norm_final, diff_w_in, diff_lambda_qk, diff_subln,
           diff_w_out, ret_w_in, ret_w_out, moe_w_group, moe_b_group, moe_w_expert, moe_b_expert,
           moe_w_gate, moe_w_up, moe_w_down):
    batch, seq, d = x.shape
    depth = norm_mix.shape[0]
    t = batch * seq
    heads = rel_bias_table.shape[1]
    dh = diff_lambda_qk.shape[-1]
    qk_dim = (ret_w_in.shape[2] - 2 * ret_w_out.shape[1]) // 2
    v_dim = ret_w_out.shape[1]
    dk = qk_dim // RET_HEADS
    xt = x.reshape(t, d)
    moe = None
    for i in range(depth):
        j = i // N_MIXERS
        if i % N_MIXERS == 0:
            col_scale = jnp.concatenate([jnp.full((d,), dh ** -0.5 * LOG2E, F32), jnp.ones((2 * d,), F32)])
            w_in = (diff_w_in[j] * col_scale[None, :]).astype(BF16)
            qkv = _norm_matmul(xt, norm_mix[i], w_in, moe=moe)
            if moe is not None:
                qkv, xt = qkv
            mix = _diff_attention(qkv, rel_bias_table, diff_lambda_qk[j], diff_subln[j],
                                  _diff_lambda_init(i), batch, seq, heads)
            gate = None
            w_out = diff_w_out[j].astype(BF16)
        else:
            n_in = ret_w_in.shape[2]
            col_scale = jnp.concatenate([jnp.ones((qk_dim,), F32), jnp.full((qk_dim,), dk ** -0.5, F32),
                                         jnp.ones((n_in - 2 * qk_dim,), F32)])
            w_in = (ret_w_in[j] * col_scale[None, :]).astype(BF16)
            qkvg = _norm_matmul(xt, norm_mix[i], w_in, rope=_rope_tables(seq, dk),
                                n_rope=2 * qk_dim // TN, seq=seq, moe=moe)
            if moe is not None:
                qkvg, xt = qkvg
            mix = _retention(qkvg, batch, seq, qk_dim, v_dim)
            gate = (qkvg, (2 * qk_dim + v_dim) // v_dim)
            w_out = ret_w_out[j].astype(BF16)
        x1, hn, route, route_t, cnt = _out_router(mix, gate, w_out, xt, norm_ffn[i], moe_w_group[i],
                                                  moe_b_group[i], moe_w_expert[i], moe_b_expert[i])
        xt = x1
        moe = (route, _moe(hn, route_t, cnt, moe_w_gate, moe_w_up, moe_w_down, i))
    return _final_combine(xt, moe[0], moe[1], norm_final).reshape(batch, seq, d)
```

```python
import functools
import math

import jax
import jax.numpy as jnp
from jax import lax
from jax.experimental import pallas as pl
from jax.experimental.pallas import tpu as pltpu
from jax.experimental.pallas import tpu_sc as plsc

F32 = jnp.float32
BF16 = jnp.bfloat16

N_MIXERS = 2
NUM_BUCKETS = 32
MAX_DISTANCE = 128
RET_HEADS = 4
TOP_K_INNER = 2
RMS_EPS = 1e-6
SUBLN_EPS = 1e-5

LANES = 128
SUBLANES = 8
BF16_SUBLANES = 16
LOG2E = 1.4426950408889634
VMEM_LIMIT = 56 * 1024 * 1024
SC_CORES = 2
SC_SUBCORES = 16

TM = 512
TN = 1024
TQ = 256
ATTN_HEADS = 4
RET_C = 256
RET_HP = 2
ROUTER_SPLIT = 2
ROW_BLOCK = 512
SC_CHUNK = 64


def _cparams(sem):
    return pltpu.CompilerParams(dimension_semantics=sem, vmem_limit_bytes=VMEM_LIMIT)


def _pack_bf16_pairs(x):
    h = x.shape[1] // 2
    hi = lax.bitcast_convert_type(x[:, :h].astype(BF16).astype(F32), jnp.uint32)
    lo = lax.bitcast_convert_type(x[:, h:].astype(BF16).astype(F32), jnp.uint32)
    return hi | (lo >> 16)


def _unpack_bf16_pairs(w):
    hi = lax.bitcast_convert_type(w & jnp.uint32(0xFFFF0000), F32)
    lo = lax.bitcast_convert_type(w << 16, F32)
    return hi, lo


def _norm_matmul_kernel(x_ref, *rest, n_rope, moe, layer):
    rest = list(rest)
    w_ref, stage_ref, sem = rest[-3:]
    rest = rest[:-3]
    if moe:
        route_ref, y1_ref, y2_ref = rest[:3]
        x_out_ref = rest.pop()
        rest = rest[3:]
    g_ref, scale_ref, w_hbm = rest[:3]
    if n_rope:
        cos_ref, sa_ref, sb_ref, o_ref = rest[3:]
    else:
        (o_ref,) = rest[3:]
    n_chunks = o_ref.shape[1] // TN

    @pl.when(pl.program_id(0) == 0)
    def _():
        def fetch(j):
            return pltpu.make_async_copy(w_hbm.at[layer, :, pl.ds(j * TN, TN)], stage_ref.at[j % 2],
                                         sem.at[j % 2])

        fetch(0).start()
        for j in range(n_chunks):
            if j + 1 < n_chunks:
                fetch(j + 1).start()
            fetch(j).wait()
            cols = slice(j * TN, (j + 1) * TN)
            w_ref[:, cols] = (stage_ref[j % 2] * scale_ref[:, cols]).astype(BF16)

    x = x_ref[...]
    if moe:
        route = route_ref[...]
        y1 = jnp.concatenate(_unpack_bf16_pairs(y1_ref[...]), axis=1)
        y2 = jnp.concatenate(_unpack_bf16_pairs(y2_ref[...]), axis=1)
        x = x + (route[:, 2:3] * y1 + route[:, 3:4] * y2)
        x_out_ref[...] = x
    xn = (x * lax.rsqrt(jnp.mean(x * x, axis=-1, keepdims=True) + RMS_EPS) * g_ref[...]).astype(BF16)
    for j in range(n_chunks):
        cols = slice(j * TN, (j + 1) * TN)
        acc = jnp.dot(xn, w_ref[:, cols], preferred_element_type=F32)
        if j < n_rope:
            reps = TN // cos_ref.shape[1]
            c = jnp.tile(cos_ref[...], (1, reps))
            sa = jnp.tile(sa_ref[...], (1, reps))
            sb = jnp.tile(sb_ref[...], (1, reps))
            nxt = pltpu.roll(acc, TN - 1, axis=1)
            prv = pltpu.roll(acc, 1, axis=1)
            acc = acc * c + nxt * sa + prv * sb
        o_ref[:, cols] = acc.astype(o_ref.dtype)


def _norm_matmul(x, g, w, layer, col_scale, rope=None, n_rope=0, seq=None, moe=None):
    t, d = x.shape
    n = w.shape[2]
    row = lambda i: (i, 0)
    in_specs = [pl.BlockSpec((TM, d), row)]
    args = [x]
    if moe is not None:
        route, yg = moe
        slot2 = t // TM
        in_specs += [pl.BlockSpec((TM, LANES), row), pl.BlockSpec((TM, d // 2), row),
                     pl.BlockSpec((TM, d // 2), lambda i: (slot2 + i, 0))]
        args += [route, yg, yg]
    in_specs += [
        pl.BlockSpec((1, d), lambda i: (0, 0)),
        pl.BlockSpec((1, n), lambda i: (0, 0)),
        pl.BlockSpec(memory_space=pl.ANY),
    ]
    args += [g.reshape(1, d), col_scale.reshape(1, n).astype(F32), w]
    if rope is not None:
        nb = seq // TM
        dk = rope[0].shape[1]
        in_specs += [pl.BlockSpec((TM, dk), lambda i: (i % nb, 0))] * 3
        args += list(rope)
    out_specs = [pl.BlockSpec((TM, n), row)]
    out_shape = [jax.ShapeDtypeStruct((t, n), BF16)]
    if moe is not None:
        out_specs.append(pl.BlockSpec((TM, d), row))
        out_shape.append(jax.ShapeDtypeStruct((t, d), F32))
    out = pl.pallas_call(
        functools.partial(_norm_matmul_kernel, n_rope=n_rope, moe=moe is not None, layer=layer),
        grid=(t // TM,),
        in_specs=in_specs,
        out_specs=out_specs,
        out_shape=out_shape,
        scratch_shapes=[pltpu.VMEM((d, n), BF16), pltpu.VMEM((2, d, TN), F32), pltpu.SemaphoreType.DMA((2,))],
        compiler_params=_cparams(("arbitrary",)),
    )(*args)
    return out if moe is not None else out[0]


def _diff_attn_kernel(q_ref, k_ref, v_ref, bias_ref, lam_ref, g_ref, o_ref,
                      vt_ref, m_ref, acc_ref, s_ref, *, lambda_init):
    seq = q_ref.shape[0]
    hp, _, _, tq = bias_ref.shape
    dv = q_ref.shape[1] // hp
    dh = dv // 2
    nq = seq // tq

    def head(hh):
        return slice(hh * dv, (hh + 1) * dv)

    extra = vt_ref.shape[2] - dv
    ones_row = jnp.where(lax.broadcasted_iota(jnp.int32, (extra, tq), 0) == 0, 1.0, 0.0).astype(BF16)
    for hh in range(hp):
        for j in range(nq):
            vt_ref[hh, j, :dv, :] = v_ref[j * tq:(j + 1) * tq, head(hh)].astype(F32).T.astype(BF16)
            vt_ref[hh, j, dv:, :] = ones_row

    lq = lam_ref[...]
    lam = (jnp.exp(jnp.sum(lq[0:1] * lq[1:2], axis=-1, keepdims=True))
           - jnp.exp(jnp.sum(lq[2:3] * lq[3:4], axis=-1, keepdims=True)) + lambda_init)
    lane = lax.broadcasted_iota(jnp.int32, (tq, dv), 1)

    def query_maps(qi):
        qrows = pl.ds(pl.multiple_of(qi * tq, tq), tq)
        qs = []
        for hh in range(hp):
            q = q_ref[qrows, head(hh)]
            zero = jnp.zeros_like(q)
            qs.append(jnp.concatenate([jnp.where(lane < dh, q, zero), jnp.where(lane >= dh, q, zero)],
                                      axis=0))
        return qs

    def score_dots(j, qs):
        rows = pl.ds(pl.multiple_of(j * tq, tq), tq)
        for hh in range(hp):
            s_ref[hh] = lax.dot_general(k_ref[rows, head(hh)], qs[hh], (((1,), (1,)), ((), ())),
                                        preferred_element_type=F32)

    score_dots(0, query_maps(0))

    def q_body(qi, carry):
        qrows = pl.ds(pl.multiple_of(qi * tq, tq), tq)
        qs = query_maps(qi)
        qs_next = query_maps(jnp.minimum(qi + 1, nq - 1))

        m_ref[...] = jnp.full(m_ref.shape, -jnp.inf, F32)
        acc_ref[...] = jnp.zeros(acc_ref.shape, F32)

        def step(j, tile, prefetch):
            probs, alphas = [], []
            for hh in range(hp):
                s = s_ref[hh]
                if tile is not None:
                    bias = bias_ref[hh, tile]
                    s = s + jnp.concatenate([bias, bias], axis=1)
                m_old = m_ref[hh]
                m_new = jnp.maximum(m_old, jnp.max(s, axis=0, keepdims=True))
                alpha = jnp.exp2(m_old - m_new)
                m_ref[hh] = m_new
                probs.append(jnp.exp2(s - m_new).astype(BF16))
                alphas.append(alpha)
            if prefetch == "block":
                score_dots(j + 1, qs)
            else:
                score_dots(0, qs_next)
            for hh in range(hp):
                acc_ref[hh] = alphas[hh] * acc_ref[hh] + jnp.dot(vt_ref[hh, j], probs[hh],
                                                                 preferred_element_type=F32)

        def far_body(j, c):
            step(j, None, "block")
            return c

        lax.fori_loop(0, jnp.maximum(qi - 1, 0), far_body, 0)

        @pl.when(qi >= 1)
        def _():
            step(qi - 1, 1, "block")

        step(qi, 0, "tile")

        for hh in range(hp):
            acc = acc_ref[hh]
            on = acc[:dv] / acc[dv:dv + 1]
            ot = on[:, :tq] - lam * on[:, tq:]
            ot = ot * lax.rsqrt(jnp.mean(ot * ot, axis=0, keepdims=True) + SUBLN_EPS)
            o_ref[qrows, head(hh)] = (ot.T * g_ref[...] * (1.0 - lambda_init)).astype(o_ref.dtype)
        return carry

    lax.fori_loop(0, nq, q_body, 0)


def _t5_bucket(rel):
    max_exact = NUM_BUCKETS // 2
    n = jnp.maximum(rel, 0)
    is_small = n < max_exact
    nf = jnp.maximum(n, max_exact).astype(F32)
    large = max_exact + (jnp.log(nf / max_exact) / math.log(MAX_DISTANCE / max_exact)
                         * (NUM_BUCKETS - max_exact)).astype(jnp.int32)
    large = jnp.minimum(large, NUM_BUCKETS - 1)
    return jnp.where(is_small, n, large)


def _bias_tiles(rel_table):
    heads = rel_table.shape[1]
    span = 2 * TQ + 1
    bucket = _t5_bucket(jnp.arange(span))
    onehot = (bucket[:, None] == jnp.arange(NUM_BUCKETS)[None, :]).astype(F32)
    table = (rel_table.astype(F32) - rel_table[NUM_BUCKETS - 1].astype(F32)[None, :]) * LOG2E
    by_rel = jnp.dot(onehot, table, precision=lax.Precision.HIGHEST).T
    toep = jnp.tile(by_rel, (1, TQ))[:, :TQ * (span - 1)].reshape(heads, TQ, span - 1)
    kj = jnp.arange(TQ)[:, None]
    qi = jnp.arange(TQ)[None, :]
    diag = jnp.where((qi >= kj)[None], toep[:, :, :TQ], -jnp.inf)
    return jnp.stack([diag, toep[:, :, TQ:]], axis=1)


def _diff_attention(qkv, rel_table, lam_qk, subln_g, lambda_init, batch, seq, heads):
    t = qkv.shape[0]
    d = qkv.shape[1] // 3
    dv = d // heads
    hp = ATTN_HEADS
    assert TQ + 1 >= MAX_DISTANCE and seq % TQ == 0 and dv == LANES and heads % hp == 0
    nq = seq // TQ
    groups = heads // hp
    bias = _bias_tiles(rel_table)
    return pl.pallas_call(
        functools.partial(_diff_attn_kernel, lambda_init=lambda_init),
        grid=(batch, groups),
        in_specs=[
            pl.BlockSpec((seq, hp * dv), lambda b, g: (b, g)),
            pl.BlockSpec((seq, hp * dv), lambda b, g: (b, groups + g)),
            pl.BlockSpec((seq, hp * dv), lambda b, g: (b, 2 * groups + g)),
            pl.BlockSpec((hp, 2, TQ, TQ), lambda b, g: (g, 0, 0, 0)),
            pl.BlockSpec(lam_qk.shape, lambda b, g: (0, 0)),
            pl.BlockSpec((1, dv), lambda b, g: (0, 0)),
        ],
        out_specs=pl.BlockSpec((seq, hp * dv), lambda b, g: (b, g)),
        out_shape=jax.ShapeDtypeStruct((t, d), BF16),
        scratch_shapes=[
            pltpu.VMEM((hp, nq, dv + BF16_SUBLANES, TQ), BF16),
            pltpu.VMEM((hp, 1, 2 * TQ), F32),
            pltpu.VMEM((hp, dv + BF16_SUBLANES, 2 * TQ), F32),
            pltpu.VMEM((hp, TQ, 2 * TQ), F32),
        ],
        compiler_params=_cparams(("parallel", "parallel")),
    )(qkv, qkv, qkv, bias, lam_qk.astype(F32), subln_g.reshape(1, dv).astype(F32))


def _retention_kernel(q_ref, k_ref, v_ref, dec_ref, tab_ref, o_ref, state_ref):
    seq = q_ref.shape[0]
    hp, c, _ = dec_ref.shape
    dk = q_ref.shape[1] // hp
    dv = v_ref.shape[1] // hp
    heads = range(hp)
    state_ref[...] = jnp.zeros(state_ref.shape, F32)

    def body(n, carry):
        rows = pl.ds(pl.multiple_of(n * c, c), c)
        q = [q_ref[rows, hh * dk:(hh + 1) * dk] for hh in heads]
        k = [k_ref[rows, hh * dk:(hh + 1) * dk] for hh in heads]
        v = [v_ref[rows, hh * dv:(hh + 1) * dv] for hh in heads]
        tab = [tab_ref[hh] for hh in heads]
        st = [state_ref[hh] for hh in heads]
        s = [lax.dot_general(q[hh], k[hh], (((1,), (1,)), ((), ())), preferred_element_type=F32)
             for hh in heads]
        cross = [jnp.dot(q[hh], st[hh].astype(BF16), preferred_element_type=F32) for hh in heads]
        kt = [(k[hh].astype(F32) * tab[hh][:, 1:2]).T.astype(BF16) for hh in heads]
        inner = [jnp.dot((s[hh] * dec_ref[hh]).astype(BF16), v[hh], preferred_element_type=F32)
                 for hh in heads]
        for hh in heads:
            state_ref[hh] = st[hh] * tab[hh][0:1, 2:3] + jnp.dot(kt[hh], v[hh], preferred_element_type=F32)
        for hh in heads:
            o = inner[hh] + cross[hh] * tab[hh][:, 0:1]
            o = o * lax.rsqrt(jnp.mean(o * o, axis=-1, keepdims=True) + RMS_EPS)
            o_ref[rows, hh * dv:(hh + 1) * dv] = o.astype(o_ref.dtype)
        return carry

    lax.fori_loop(0, seq // c, body, 0)


def _retention_tables(heads):
    c = RET_C
    log_gamma = jnp.log(1.0 - 2.0 ** (-5.0 - jnp.arange(heads, dtype=F32)))
    pos = jnp.arange(c, dtype=F32)
    dist = pos[:, None] - pos[None, :]
    inner_decay = jnp.where(dist[None] >= 0,
                            jnp.exp(jnp.maximum(dist, 0.0)[None] * log_gamma[:, None, None]), 0.0)
    q_decay = jnp.exp((pos + 1.0)[None] * log_gamma[:, None])
    k_decay = jnp.exp((c - 1.0 - pos)[None] * log_gamma[:, None])
    chunk_decay = jnp.broadcast_to(jnp.exp(c * log_gamma)[:, None], (heads, c))
    tab = jnp.stack([q_decay, k_decay, chunk_decay], axis=-1)
    tab = jnp.pad(tab, ((0, 0), (0, 0), (0, LANES - 3)))
    return inner_decay, tab


def _retention(qkvg, batch, seq, qk_dim, v_dim):
    t = qkvg.shape[0]
    dk = qk_dim // RET_HEADS
    dvh = v_dim // RET_HEADS
    inner_decay, tab = _retention_tables(RET_HEADS)
    hp = RET_HP
    assert RET_HEADS % hp == 0
    groups = RET_HEADS // hp
    kb = qk_dim // (hp * dk)
    vb = 2 * qk_dim // (hp * dvh)
    return pl.pallas_call(
        _retention_kernel,
        grid=(batch, groups),
        in_specs=[
            pl.BlockSpec((seq, hp * dk), lambda b, g: (b, g)),
            pl.BlockSpec((seq, hp * dk), lambda b, g: (b, kb + g)),
            pl.BlockSpec((seq, hp * dvh), lambda b, g: (b, vb + g)),
            pl.BlockSpec((hp, RET_C, RET_C), lambda b, g: (g, 0, 0)),
            pl.BlockSpec((hp, RET_C, LANES), lambda b, g: (g, 0, 0)),
        ],
        out_specs=pl.BlockSpec((seq, hp * dvh), lambda b, g: (b, g)),
        out_shape=jax.ShapeDtypeStruct((t, v_dim), BF16),
        scratch_shapes=[pltpu.VMEM((hp, dk, dvh), F32)],
        compiler_params=_cparams(("parallel", "parallel")),
    )(qkvg, qkvg, qkvg, inner_decay, tab)


def _out_router_kernel(*refs, gated, n_groups, n_experts):
    if gated:
        a_ref, gate_ref, w_ref, x_ref, nf_ref, wr_ref, br_ref = refs[:7]
        x1_ref, hn_ref, route_ref, route_t_ref, cnt_ref, carry_ref = refs[7:]
    else:
        a_ref, w_ref, x_ref, nf_ref, wr_ref, br_ref = refs[:6]
        x1_ref, hn_ref, route_ref, route_t_ref, cnt_ref, carry_ref = refs[6:]
    epg = n_experts // n_groups
    tm = x_ref.shape[0] // ROUTER_SPLIT

    @pl.when(pl.program_id(0) == 0)
    def _():
        carry_ref[...] = jnp.zeros(carry_ref.shape, F32)

    x1_parts = []
    for part in range(ROUTER_SPLIT):
        rows = slice(part * tm, (part + 1) * tm)
        a = a_ref[rows, :]
        if gated:
            gate = gate_ref[rows, :]
            a = gate * jax.nn.sigmoid(gate) * a
        x1_parts.append(x_ref[rows, :] + jnp.dot(a, w_ref[...], preferred_element_type=F32))
    for part in range(ROUTER_SPLIT):
        rows = slice(part * tm, (part + 1) * tm)
        _route_rows(x1_parts[part], rows, nf_ref, wr_ref, br_ref, x1_ref, hn_ref, route_ref, route_t_ref,
                    cnt_ref, carry_ref, n_groups=n_groups, n_experts=n_experts, epg=epg)


def _route_rows(x1, rows, nf_ref, wr_ref, br_ref, x1_ref, hn_ref, route_ref, route_t_ref, cnt_ref,
                carry_ref, *, n_groups, n_experts, epg):
    tm = x1.shape[0]
    x1_ref[rows, :] = x1
    hn = x1 * lax.rsqrt(jnp.mean(x1 * x1, axis=-1, keepdims=True) + RMS_EPS) * nf_ref[...]
    hn_ref[rows, :] = _pack_bf16_pairs(hn)

    hn_hi = hn.astype(BF16)
    hn_lo = (hn - hn_hi.astype(F32)).astype(BF16)
    logits = (jnp.dot(hn_hi, wr_ref[0], preferred_element_type=F32)
              + jnp.dot(hn_lo, wr_ref[0], preferred_element_type=F32)
              + jnp.dot(hn_hi, wr_ref[1], preferred_element_type=F32)) + br_ref[...]

    n_rows = -(-(n_groups + n_experts) // SUBLANES) * SUBLANES
    lt = logits.T[:n_rows, :]
    row = lax.broadcasted_iota(jnp.int32, lt.shape, 0)
    big = jnp.int32(LANES)
    neg = jnp.float32(-jnp.inf)

    gl = jnp.where(row < n_groups, lt, neg)
    gmax = jnp.max(gl, axis=0, keepdims=True)
    g_p = 1.0 / jnp.sum(jnp.exp(gl - gmax), axis=0, keepdims=True)
    g_idx = jnp.min(jnp.where(gl == gmax, row, big), axis=0, keepdims=True)

    lo = n_groups + g_idx * epg
    el = jnp.where((row >= lo) & (row < lo + epg), lt, neg)
    v1 = jnp.max(el, axis=0, keepdims=True)
    i1 = jnp.min(jnp.where(el == v1, row, big), axis=0, keepdims=True)
    el2 = jnp.where(row == i1, neg, el)
    v2 = jnp.max(el2, axis=0, keepdims=True)
    i2 = jnp.min(jnp.where(el2 == v2, row, big), axis=0, keepdims=True)
    tt = jnp.exp(v2 - v1)
    w1 = g_p / (1.0 + tt)
    w2 = g_p * tt / (1.0 + tt)
    e1 = i1 - n_groups
    e2 = i2 - n_groups

    erow = lax.broadcasted_iota(jnp.int32, (n_experts, tm), 0)
    oh1 = erow == e1
    oh2 = erow == e2
    onehot = jnp.where(oh1 | oh2, 1.0, 0.0)
    before = lax.broadcasted_iota(jnp.int32, (tm, tm), 0) < lax.broadcasted_iota(jnp.int32, (tm, tm), 1)
    tri = jnp.where(before, 1.0, 0.0).astype(BF16)
    carry = carry_ref[...]
    base = jnp.dot(onehot.astype(BF16), tri, preferred_element_type=F32) + carry[:, 0:1]
    rank1 = jnp.sum(jnp.where(oh1, base, 0.0), axis=0, keepdims=True)
    rank2 = jnp.sum(jnp.where(oh2, base, 0.0), axis=0, keepdims=True)
    carry = carry + jnp.sum(onehot, axis=1, keepdims=True)
    carry_ref[...] = carry
    cnt_ref[...] = carry

    frow = lax.broadcasted_iota(jnp.int32, (LANES, tm), 0)
    out = jnp.zeros((LANES, tm), F32)
    for idx, val in enumerate((e1.astype(F32), e2.astype(F32), w1, w2, rank1, rank2)):
        out = jnp.where(frow == idx, val, out)
    route_t_ref[:, rows] = out[:route_t_ref.shape[0], :]
    route_ref[rows, :] = out.T


def _out_router(a, gate, w, x, nf, w_group, b_group, w_expert, b_expert):
    t, d = x.shape
    k = w.shape[0]
    n_groups = w_group.shape[1]
    n_experts = w_expert.shape[1]
    assert n_groups + n_experts <= LANES
    pad = LANES - n_groups - n_experts
    wr = jnp.pad(jnp.concatenate([w_group, w_expert], axis=1).astype(F32), ((0, 0), (0, pad)))
    wr_hi = wr.astype(BF16)
    wr = jnp.stack([wr_hi, (wr - wr_hi.astype(F32)).astype(BF16)])
    br = jnp.pad(jnp.concatenate([b_group, b_expert]).astype(F32), (0, pad)).reshape(1, LANES)
    gated = gate is not None
    row = lambda i: (i, 0)
    fixed = lambda i: (0, 0)
    in_specs = [pl.BlockSpec((TM, k), row)]
    args = [a]
    if gated:
        gate_arr, gate_col = gate
        in_specs.append(pl.BlockSpec((TM, k), lambda i: (i, gate_col)))
        args.append(gate_arr)
    in_specs += [
        pl.BlockSpec((k, d), fixed),
        pl.BlockSpec((TM, d), row),
        pl.BlockSpec((1, d), fixed),
        pl.BlockSpec((2, d, LANES), lambda i: (0, 0, 0)),
        pl.BlockSpec((1, LANES), fixed),
    ]
    args += [w, x, nf.reshape(1, d), wr, br]
    return pl.pallas_call(
        functools.partial(_out_router_kernel, gated=gated, n_groups=n_groups, n_experts=n_experts),
        grid=(t // TM,),
        in_specs=in_specs,
        out_specs=[
            pl.BlockSpec((TM, d), row),
            pl.BlockSpec((TM, d // 2), row),
            pl.BlockSpec((TM, LANES), row),
            pl.BlockSpec((SUBLANES, TM), lambda i: (0, i)),
            pl.BlockSpec((n_experts, LANES), fixed),
        ],
        out_shape=[
            jax.ShapeDtypeStruct((t, d), F32),
            jax.ShapeDtypeStruct((t, d // 2), jnp.uint32),
            jax.ShapeDtypeStruct((t, LANES), F32),
            jax.ShapeDtypeStruct((SUBLANES, t), F32),
            jax.ShapeDtypeStruct((n_experts, LANES), F32),
        ],
        scratch_shapes=[pltpu.VMEM((n_experts, LANES), F32)],
        compiler_params=_cparams(("arbitrary",)),
    )(*args)


def _sc_mesh():
    return plsc.VectorSubcoreMesh(core_axis_name="c", subcore_axis_name="s",
                                  num_cores=SC_CORES, num_subcores=SC_SUBCORES)


def _sc_worker():
    return lax.axis_index("s") * SC_CORES + lax.axis_index("c")


def _sc_scatter_rows(src, d1, d2, n_rows):
    t, d = src.shape
    workers = SC_CORES * SC_SUBCORES
    per_w = t // workers
    n_ch = per_w // SC_CHUNK
    assert t % (workers * SC_CHUNK) == 0
    idx = jnp.stack([d1, d2]).reshape(2, workers, n_ch, SC_CHUNK)

    @functools.partial(
        pl.kernel, mesh=_sc_mesh(),
        out_type=jax.ShapeDtypeStruct((n_rows, d), src.dtype),
        scratch_types=[pltpu.VMEM((2, n_ch, SC_CHUNK), jnp.int32),
                       pltpu.VMEM((SC_CHUNK, d), src.dtype),
                       pltpu.VMEM((SC_CHUNK, d), src.dtype),
                       pltpu.SemaphoreType.DMA, pltpu.SemaphoreType.DMA],
    )
    def scatter(src_hbm, idx_hbm, out_hbm, idx_v, rows_a, rows_b, sem_a, sem_b):
        wid = _sc_worker()
        base = wid * per_w
        for s in range(2):
            pltpu.sync_copy(idx_hbm.at[s, wid], idx_v.at[s])

        def move(c, rows_v, sem):
            pltpu.sync_copy(src_hbm.at[pl.ds(base + c * SC_CHUNK, SC_CHUNK)], rows_v)
            return [pltpu.async_copy(rows_v, out_hbm.at[idx_v.at[s, c]], sem) for s in range(2)]

        @pl.loop(0, n_ch, step=2)
        def _(c):
            first = move(c, rows_a, sem_a)
            second = move(c + 1, rows_b, sem_b)
            for cp in first + second:
                cp.wait()

    return scatter(src, idx)


def _sc_gather_rows(table, idx):
    b = idx.shape[0]
    d = table.shape[1]
    workers = SC_CORES * SC_SUBCORES
    per_w = b // workers
    n_ch = per_w // SC_CHUNK
    assert b % (workers * SC_CHUNK * 2) == 0

    @functools.partial(
        pl.kernel, mesh=_sc_mesh(),
        out_type=jax.ShapeDtypeStruct((b, d), table.dtype),
        scratch_types=[pltpu.VMEM((per_w,), jnp.int32),
                       pltpu.VMEM((SC_CHUNK, d), table.dtype),
                       pltpu.VMEM((SC_CHUNK, d), table.dtype),
                       pltpu.SemaphoreType.DMA, pltpu.SemaphoreType.DMA],
    )
    def gather(table_hbm, idx_hbm, out_hbm, idx_v, rows_a, rows_b, sem_a, sem_b):
        base = _sc_worker() * per_w
        pltpu.sync_copy(idx_hbm.at[pl.ds(base, per_w)], idx_v)

        def fetch(c, rows_v, sem):
            return pltpu.async_copy(table_hbm.at[idx_v.at[pl.ds(c * SC_CHUNK, SC_CHUNK)]], rows_v, sem)

        def store(c, rows_v):
            pltpu.sync_copy(rows_v, out_hbm.at[pl.ds(base + c * SC_CHUNK, SC_CHUNK)])

        @pl.loop(0, n_ch, step=2)
        def _(c):
            first = fetch(c, rows_a, sem_a)
            second = fetch(c + 1, rows_b, sem_b)
            first.wait()
            store(c, rows_a)
            second.wait()
            store(c + 1, rows_b)

    return gather(table, idx)


def _expert_kernel(fb_ref, nblk_ref, nv_ref, nu_ref, xs_ref, wg_ref, wu_ref, wd_ref, ys_ref,
                   wgb_ref, wub_ref, wdb_ref, xbuf, ybuf, xsem, ysem):
    e = pl.program_id(0)
    rb = xbuf.shape[1]
    nb = ys_ref.shape[0] // rb
    n_used = nu_ref[0]

    def x_copy(g, slot):
        rows = pl.ds(pl.multiple_of(g * rb, rb), rb)
        return pltpu.make_async_copy(xs_ref.at[rows], xbuf.at[slot], xsem.at[slot])

    def y_copy(g, slot):
        rows = pl.ds(pl.multiple_of(g * rb, rb), rb)
        return pltpu.make_async_copy(ybuf.at[slot], ys_ref.at[rows], ysem.at[slot])

    @pl.when(e == 0)
    def _():
        x_copy(0, 0).start()

    @pl.when(nblk_ref[e] > 0)
    def _():
        wgb_ref[...] = wg_ref[0, 0].astype(BF16)
        wub_ref[...] = wu_ref[0, 0].astype(BF16)
        wdb_ref[...] = wd_ref[0, 0].astype(BF16)

    def block(b, carry):
        g = fb_ref[e] + b
        slot = lax.rem(g, 2)
        x_copy(g, slot).wait()

        @pl.when(g + 1 < n_used)
        def _():
            x_copy(g + 1, 1 - slot).start()

        @pl.when(g >= 2)
        def _():
            y_copy(g - 2, slot).wait()

        nv = nv_ref[g]

        def ffn(rows):
            valid = lax.broadcasted_iota(jnp.int32, (rows, 1), 0) < nv
            x_hi, x_lo = _unpack_bf16_pairs(jnp.where(valid, xbuf[slot, :rows, :], jnp.uint32(0)))
            x_hi, x_lo = x_hi.astype(BF16), x_lo.astype(BF16)
            h = x_hi.shape[1]

            def in_dot(w_ref):
                return (jnp.dot(x_hi, w_ref[:h, :], preferred_element_type=F32)
                        + jnp.dot(x_lo, w_ref[h:, :], preferred_element_type=F32))

            gate = in_dot(wgb_ref)
            up = in_dot(wub_ref)
            hmid = (gate * jax.nn.sigmoid(gate) * up).astype(BF16)
            ybuf[slot, :rows, :] = _pack_bf16_pairs(jnp.dot(hmid, wdb_ref[...], preferred_element_type=F32))
            if rows < rb:
                ybuf[slot, rows:, :] = jnp.zeros((rb - rows, ybuf.shape[2]), ybuf.dtype)

        @pl.when(nv > rb // 2)
        def _():
            ffn(rb)

        @pl.when(nv <= rb // 2)
        def _():
            ffn(rb // 2)

        y_copy(g, slot).start()
        return carry

    lax.fori_loop(0, nblk_ref[e], block, 0)

    @pl.when(e == pl.num_programs(0) - 1)
    def _():
        @pl.when(n_used >= 2)
        def _():
            y_copy(n_used - 2, lax.rem(n_used, 2)).wait()

        y_copy(n_used - 1, lax.rem(n_used - 1, 2)).wait()
        ybuf[0] = jnp.zeros(ybuf.shape[1:], ybuf.dtype)

        def tail_start(g, carry):
            y_copy(g, 0).start()
            return carry

        def tail_wait(g, carry):
            y_copy(g, 0).wait()
            return carry

        lax.fori_loop(n_used, nb, tail_start, 0)
        lax.fori_loop(n_used, nb, tail_wait, 0)


def _experts(xs, first_block, n_blocks, n_valid, n_used, w_gate, w_up, w_down, layer):
    n_rows, dp = xs.shape
    d = 2 * dp
    n_experts, ff = w_gate.shape[1], w_gate.shape[3]
    wsel = lambda e, *_: (layer, e, 0, 0)
    grid_spec = pltpu.PrefetchScalarGridSpec(
        num_scalar_prefetch=4,
        grid=(n_experts,),
        in_specs=[
            pl.BlockSpec(memory_space=pl.ANY),
            pl.BlockSpec((1, 1, d, ff), wsel),
            pl.BlockSpec((1, 1, d, ff), wsel),
            pl.BlockSpec((1, 1, ff, d), wsel),
        ],
        out_specs=pl.BlockSpec(memory_space=pl.ANY),
        scratch_shapes=[pltpu.VMEM((d, ff), BF16), pltpu.VMEM((d, ff), BF16), pltpu.VMEM((ff, d), BF16),
                        pltpu.VMEM((2, ROW_BLOCK, dp), xs.dtype), pltpu.VMEM((2, ROW_BLOCK, dp), xs.dtype),
                        pltpu.SemaphoreType.DMA((2,)), pltpu.SemaphoreType.DMA((2,))],
    )
    return pl.pallas_call(
        _expert_kernel,
        grid_spec=grid_spec,
        out_shape=jax.ShapeDtypeStruct((n_rows, dp), xs.dtype),
        compiler_params=_cparams(("arbitrary",)),
    )(first_block, n_blocks, n_valid, n_used, xs, w_gate, w_up, w_down)


def _final_combine_kernel(x_ref, route_ref, y1_ref, y2_ref, nf_ref, o_ref):
    route = route_ref[...]
    y1 = jnp.concatenate(_unpack_bf16_pairs(y1_ref[...]), axis=1)
    y2 = jnp.concatenate(_unpack_bf16_pairs(y2_ref[...]), axis=1)
    y = x_ref[...] + (route[:, 2:3] * y1 + route[:, 3:4] * y2)
    o_ref[...] = y * lax.rsqrt(jnp.mean(y * y, axis=-1, keepdims=True) + RMS_EPS) * nf_ref[...]


def _final_combine(x1, route, yg, nf):
    t, d = x1.shape
    slot2 = t // TM
    row = lambda i: (i, 0)
    return pl.pallas_call(
        _final_combine_kernel,
        grid=(t // TM,),
        in_specs=[
            pl.BlockSpec((TM, d), row),
            pl.BlockSpec((TM, LANES), row),
            pl.BlockSpec((TM, d // 2), row),
            pl.BlockSpec((TM, d // 2), lambda i: (slot2 + i, 0)),
            pl.BlockSpec((1, d), lambda i: (0, 0)),
        ],
        out_specs=pl.BlockSpec((TM, d), row),
        out_shape=jax.ShapeDtypeStruct((t, d), F32),
        compiler_params=_cparams(("parallel",)),
    )(x1, route, yg, yg, nf.reshape(1, d).astype(F32))


def _moe(hn, route_t, cnt, w_gate, w_up, w_down, layer):
    t, d = hn.shape
    n_experts = w_gate.shape[1]
    ids = jnp.arange(n_experts, dtype=jnp.int32)
    e1, e2, rank1, rank2 = (route_t[f].astype(jnp.int32) for f in (0, 1, 4, 5))
    counts = cnt[:, 0].astype(jnp.int32)
    padded = ((counts + ROW_BLOCK - 1) // ROW_BLOCK) * ROW_BLOCK
    pad_end = jnp.cumsum(padded)
    pad_start = pad_end - padded

    def dest(e, rank):
        return jnp.sum(jnp.where(e[:, None] == ids[None, :], pad_start[None, :], 0), axis=1) + rank

    d1 = dest(e1, rank1)
    d2 = dest(e2, rank2)
    nb = (t * TOP_K_INNER) // ROW_BLOCK + n_experts
    n_used = (pad_end[-1:] // ROW_BLOCK).astype(jnp.int32)
    first_row = jnp.arange(nb, dtype=jnp.int32) * ROW_BLOCK
    block_e = jnp.sum((pad_end[None, :] <= first_row[:, None]).astype(jnp.int32), axis=1)
    block_e = jnp.minimum(block_e, n_experts - 1)
    row_end = jnp.sum(jnp.where(block_e[:, None] == ids[None, :], (pad_start + counts)[None, :], 0), axis=1)
    n_valid = jnp.clip(row_end - first_row, 0, ROW_BLOCK).astype(jnp.int32)
    xs = _sc_scatter_rows(hn, d1, d2, nb * ROW_BLOCK)
    ys = _experts(xs, (pad_start // ROW_BLOCK).astype(jnp.int32), (padded // ROW_BLOCK).astype(jnp.int32),
                  n_valid, n_used, w_gate, w_up, w_down, layer)
    return _sc_gather_rows(ys, jnp.concatenate([d1, d2]))


def _diff_lambda_init(layer_idx):
    return 0.8 - 0.6 * math.exp(-0.3 * layer_idx)


def _rope_tables(seq, dk):
    angle = 1.0 / (10000.0 ** jnp.linspace(0.0, 1.0, dk // 2, dtype=F32))
    angle = jnp.repeat(angle, 2)
    phase = jnp.arange(seq, dtype=F32)[:, None] * angle[None, :]
    sin, cos = jnp.sin(phase), jnp.cos(phase)
    even = (jnp.arange(dk) % 2 == 0)[None, :]
    return cos, jnp.where(even, -sin, 0.0), jnp.where(even, 0.0, sin)


def kernel(x, rel_bias_table, norm_mix, norm_ffn, norm_final, diff_w_in, diff_lambda_qk, diff_subln,
           diff_w_out, ret_w_in, ret_w_out, moe_w_group, moe_b_group, moe_w_expert, moe_b_expert,
           moe_w_gate, moe_w_up, moe_w_down):
    batch, seq, d = x.shape
    depth = norm_mix.shape[0]
    t = batch * seq
    heads = rel_bias_table.shape[1]
    dh = diff_lambda_qk.shape[-1]
    qk_dim = (ret_w_in.shape[2] - 2 * ret_w_out.shape[1]) // 2
    v_dim = ret_w_out.shape[1]
    dk = qk_dim // RET_HEADS
    xt = x.reshape(t, d)
    moe = None
    for i in range(depth):
        j = i // N_MIXERS
        if i % N_MIXERS == 0:
            col_scale = jnp.concatenate([jnp.full((d,), dh ** -0.5 * LOG2E, F32), jnp.ones((2 * d,), F32)])
            qkv = _norm_matmul(xt, norm_mix[i], diff_w_in, j, col_scale, moe=moe)
            if moe is not None:
                qkv, xt = qkv
            mix = _diff_attention(qkv, rel_bias_table, diff_lambda_qk[j], diff_subln[j],
                                  _diff_lambda_init(i), batch, seq, heads)
            gate = None
            w_out = diff_w_out[j].astype(BF16)
        else:
            n_in = ret_w_in.shape[2]
            col_scale = jnp.concatenate([jnp.ones((qk_dim,), F32), jnp.full((qk_dim,), dk ** -0.5, F32),
                                         jnp.ones((n_in - 2 * qk_dim,), F32)])
            qkvg = _norm_matmul(xt, norm_mix[i], ret_w_in, j, col_scale, rope=_rope_tables(seq, dk),
                                n_rope=2 * qk_dim // TN, seq=seq, moe=moe)
            if moe is not None:
                qkvg, xt = qkvg
            mix = _retention(qkvg, batch, seq, qk_dim, v_dim)
            gate = (qkvg, (2 * qk_dim + v_dim) // v_dim)
            w_out = ret_w_out[j].astype(BF16)
        x1, hn, route, route_t, cnt = _out_router(mix, gate, w_out, xt, norm_ffn[i], moe_w_group[i],
                                                  moe_b_group[i], moe_w_expert[i], moe_b_expert[i])
        xt = x1
        moe = (route, _moe(hn, route_t, cnt, moe_w_gate, moe_w_up, moe_w_down, i))
    return _final_combine(xt, moe[0], moe[1], norm_final).reshape(batch, seq, d)
```

```python
import functools
import math

import jax
import jax.numpy as jnp
from jax import lax
from jax.experimental import pallas as pl
from jax.experimental.pallas import tpu as pltpu
from jax.experimental.pallas import tpu_sc as plsc

F32 = jnp.float32
BF16 = jnp.bfloat16

N_MIXERS = 2
NUM_BUCKETS = 32
MAX_DISTANCE = 128
RET_HEADS = 4
TOP_K_INNER = 2
RMS_EPS = 1e-6
SUBLN_EPS = 1e-5

LANES = 128
SUBLANES = 8
BF16_SUBLANES = 16
LOG2E = 1.4426950408889634
VMEM_LIMIT = 56 * 1024 * 1024
SC_CORES = 2
SC_SUBCORES = 16

TM = 512
TN = 1024
TQ = 256
ATTN_HEADS = 4
RET_C = 256
RET_HP = 2
ROUTER_SPLIT = 2
ROW_BLOCK = 512
SC_CHUNK = 64


def _cparams(sem):
    return pltpu.CompilerParams(dimension_semantics=sem, vmem_limit_bytes=VMEM_LIMIT)


def _pack_bf16_pairs(x):
    h = x.shape[1] // 2
    hi = lax.bitcast_convert_type(x[:, :h].astype(BF16).astype(F32), jnp.uint32)
    lo = lax.bitcast_convert_type(x[:, h:].astype(BF16).astype(F32), jnp.uint32)
    return hi | (lo >> 16)


def _unpack_bf16_pairs(w):
    hi = lax.bitcast_convert_type(w & jnp.uint32(0xFFFF0000), F32)
    lo = lax.bitcast_convert_type(w << 16, F32)
    return hi, lo


def _norm_matmul_kernel(x_ref, *rest, n_rope, moe, layer):
    rest = list(rest)
    w_ref, stage_ref, sem = rest[-3:]
    rest = rest[:-3]
    if moe:
        route_ref, y1_ref, y2_ref = rest[:3]
        x_out_ref = rest.pop()
        rest = rest[3:]
    g_ref, scale_ref, w_hbm = rest[:3]
    if n_rope:
        cos_ref, sa_ref, sb_ref, o_ref = rest[3:]
    else:
        (o_ref,) = rest[3:]
    n_chunks = o_ref.shape[1] // TN

    @pl.when(pl.program_id(0) == 0)
    def _():
        def fetch(j):
            return pltpu.make_async_copy(w_hbm.at[layer, :, pl.ds(j * TN, TN)], stage_ref.at[j % 2],
                                         sem.at[j % 2])

        fetch(0).start()
        for j in range(n_chunks):
            if j + 1 < n_chunks:
                fetch(j + 1).start()
            fetch(j).wait()
            cols = slice(j * TN, (j + 1) * TN)
            w_ref[:, cols] = (stage_ref[j % 2] * scale_ref[:, cols]).astype(BF16)

    x = x_ref[...]
    if moe:
        route = route_ref[...]
        y1 = jnp.concatenate(_unpack_bf16_pairs(y1_ref[...]), axis=1)
        y2 = jnp.concatenate(_unpack_bf16_pairs(y2_ref[...]), axis=1)
        x = x + (route[:, 2:3] * y1 + route[:, 3:4] * y2)
        x_out_ref[...] = x
    xn = (x * lax.rsqrt(jnp.mean(x * x, axis=-1, keepdims=True) + RMS_EPS) * g_ref[...]).astype(BF16)
    for j in range(n_chunks):
        cols = slice(j * TN, (j + 1) * TN)
        acc = jnp.dot(xn, w_ref[:, cols], preferred_element_type=F32)
        if j < n_rope:
            reps = TN // cos_ref.shape[1]
            c = jnp.tile(cos_ref[...], (1, reps))
            sa = jnp.tile(sa_ref[...], (1, reps))
            sb = jnp.tile(sb_ref[...], (1, reps))
            nxt = pltpu.roll(acc, TN - 1, axis=1)
            prv = pltpu.roll(acc, 1, axis=1)
            acc = acc * c + nxt * sa + prv * sb
        o_ref[:, cols] = acc.astype(o_ref.dtype)


def _norm_matmul(x, g, w, layer, col_scale, rope=None, n_rope=0, seq=None, moe=None):
    t, d = x.shape
    n = w.shape[2]
    row = lambda i: (i, 0)
    in_specs = [pl.BlockSpec((TM, d), row)]
    args = [x]
    if moe is not None:
        route, yg = moe
        slot2 = t // TM
        in_specs += [pl.BlockSpec((TM, LANES), row), pl.BlockSpec((TM, d // 2), row),
                     pl.BlockSpec((TM, d // 2), lambda i: (slot2 + i, 0))]
        args += [route, yg, yg]
    in_specs += [
        pl.BlockSpec((1, d), lambda i: (0, 0)),
        pl.BlockSpec((1, n), lambda i: (0, 0)),
        pl.BlockSpec(memory_space=pl.ANY),
    ]
    args += [g.reshape(1, d), col_scale.reshape(1, n).astype(F32), w]
    if rope is not None:
        nb = seq // TM
        dk = rope[0].shape[1]
        in_specs += [pl.BlockSpec((TM, dk), lambda i: (i % nb, 0))] * 3
        args += list(rope)
    out_specs = [pl.BlockSpec((TM, n), row)]
    out_shape = [jax.ShapeDtypeStruct((t, n), BF16)]
    if moe is not None:
        out_specs.append(pl.BlockSpec((TM, d), row))
        out_shape.append(jax.ShapeDtypeStruct((t, d), F32))
    out = pl.pallas_call(
        functools.partial(_norm_matmul_kernel, n_rope=n_rope, moe=moe is not None, layer=layer),
        grid=(t // TM,),
        in_specs=in_specs,
        out_specs=out_specs,
        out_shape=out_shape,
        scratch_shapes=[pltpu.VMEM((d, n), BF16), pltpu.VMEM((2, d, TN), F32), pltpu.SemaphoreType.DMA((2,))],
        compiler_params=_cparams(("arbitrary",)),
    )(*args)
    return out if moe is not None else out[0]


def _diff_attn_kernel(q_ref, k_ref, v_ref, bias_ref, lam_ref, g_ref, o_ref,
                      vt_ref, m_ref, acc_ref, s_ref, *, lambda_init):
    seq = q_ref.shape[0]
    hp, _, _, tq = bias_ref.shape
    dv = q_ref.shape[1] // hp
    dh = dv // 2
    nq = seq // tq

    def head(hh):
        return slice(hh * dv, (hh + 1) * dv)

    extra = vt_ref.shape[2] - dv
    ones_row = jnp.where(lax.broadcasted_iota(jnp.int32, (extra, tq), 0) == 0, 1.0, 0.0).astype(BF16)
    for hh in range(hp):
        for j in range(nq):
            vt_ref[hh, j, :dv, :] = v_ref[j * tq:(j + 1) * tq, head(hh)].astype(F32).T.astype(BF16)
            vt_ref[hh, j, dv:, :] = ones_row

    lq = lam_ref[...]
    lam = (jnp.exp(jnp.sum(lq[0:1] * lq[1:2], axis=-1, keepdims=True))
           - jnp.exp(jnp.sum(lq[2:3] * lq[3:4], axis=-1, keepdims=True)) + lambda_init)
    lane = lax.broadcasted_iota(jnp.int32, (tq, dv), 1)

    def query_maps(qi):
        qrows = pl.ds(pl.multiple_of(qi * tq, tq), tq)
        qs = []
        for hh in range(hp):
            q = q_ref[qrows, head(hh)]
            zero = jnp.zeros_like(q)
            qs.append(jnp.concatenate([jnp.where(lane < dh, q, zero), jnp.where(lane >= dh, q, zero)],
                                      axis=0))
        return qs

    def score_dots(j, qs):
        rows = pl.ds(pl.multiple_of(j * tq, tq), tq)
        for hh in range(hp):
            s_ref[hh] = lax.dot_general(k_ref[rows, head(hh)], qs[hh], (((1,), (1,)), ((), ())),
                                        preferred_element_type=F32)

    score_dots(0, query_maps(0))

    def q_body(qi, carry):
        qrows = pl.ds(pl.multiple_of(qi * tq, tq), tq)
        qs = query_maps(qi)
        qs_next = query_maps(jnp.minimum(qi + 1, nq - 1))

        m_ref[...] = jnp.full(m_ref.shape, -jnp.inf, F32)
        acc_ref[...] = jnp.zeros(acc_ref.shape, F32)

        def step(j, tile, prefetch):
            probs, alphas = [], []
            for hh in range(hp):
                s = s_ref[hh]
                if tile is not None:
                    bias = bias_ref[hh, tile]
                    s = s + jnp.concatenate([bias, bias], axis=1)
                m_old = m_ref[hh]
                m_new = jnp.maximum(m_old, jnp.max(s, axis=0, keepdims=True))
                alpha = jnp.exp2(m_old - m_new)
                m_ref[hh] = m_new
                probs.append(jnp.exp2(s - m_new).astype(BF16))
                alphas.append(alpha)
            if prefetch == "block":
                score_dots(j + 1, qs)
            else:
                score_dots(0, qs_next)
            for hh in range(hp):
                acc_ref[hh] = alphas[hh] * acc_ref[hh] + jnp.dot(vt_ref[hh, j], probs[hh],
                                                                 preferred_element_type=F32)

        def far_body(j, c):
            step(j, None, "block")
            return c

        lax.fori_loop(0, jnp.maximum(qi - 1, 0), far_body, 0)

        @pl.when(qi >= 1)
        def _():
            step(qi - 1, 1, "block")

        step(qi, 0, "tile")

        for hh in range(hp):
            acc = acc_ref[hh]
            on = acc[:dv] / acc[dv:dv + 1]
            ot = on[:, :tq] - lam * on[:, tq:]
            ot = ot * lax.rsqrt(jnp.mean(ot * ot, axis=0, keepdims=True) + SUBLN_EPS)
            o_ref[qrows, head(hh)] = (ot.T * g_ref[...] * (1.0 - lambda_init)).astype(o_ref.dtype)
        return carry

    lax.fori_loop(0, nq, q_body, 0)


def _t5_bucket(rel):
    max_exact = NUM_BUCKETS // 2
    n = jnp.maximum(rel, 0)
    is_small = n < max_exact
    nf = jnp.maximum(n, max_exact).astype(F32)
    large = max_exact + (jnp.log(nf / max_exact) / math.log(MAX_DISTANCE / max_exact)
                         * (NUM_BUCKETS - max_exact)).astype(jnp.int32)
    large = jnp.minimum(large, NUM_BUCKETS - 1)
    return jnp.where(is_small, n, large)


def _bias_tiles(rel_table):
    heads = rel_table.shape[1]
    span = 2 * TQ + 1
    bucket = _t5_bucket(jnp.arange(span))
    onehot = (bucket[:, None] == jnp.arange(NUM_BUCKETS)[None, :]).astype(F32)
    table = (rel_table.astype(F32) - rel_table[NUM_BUCKETS - 1].astype(F32)[None, :]) * LOG2E
    by_rel = jnp.dot(onehot, table, precision=lax.Precision.HIGHEST).T
    toep = jnp.tile(by_rel, (1, TQ))[:, :TQ * (span - 1)].reshape(heads, TQ, span - 1)
    kj = jnp.arange(TQ)[:, None]
    qi = jnp.arange(TQ)[None, :]
    diag = jnp.where((qi >= kj)[None], toep[:, :, :TQ], -jnp.inf)
    return jnp.stack([diag, toep[:, :, TQ:]], axis=1)


def _diff_attention(qkv, rel_table, lam_qk, subln_g, lambda_init, batch, seq, heads):
    t = qkv.shape[0]
    d = qkv.shape[1] // 3
    dv = d // heads
    hp = ATTN_HEADS
    assert TQ + 1 >= MAX_DISTANCE and seq % TQ == 0 and dv == LANES and heads % hp == 0
    nq = seq // TQ
    groups = heads // hp
    bias = _bias_tiles(rel_table)
    return pl.pallas_call(
        functools.partial(_diff_attn_kernel, lambda_init=lambda_init),
        grid=(batch, groups),
        in_specs=[
            pl.BlockSpec((seq, hp * dv), lambda b, g: (b, g)),
            pl.BlockSpec((seq, hp * dv), lambda b, g: (b, groups + g)),
            pl.BlockSpec((seq, hp * dv), lambda b, g: (b, 2 * groups + g)),
            pl.BlockSpec((hp, 2, TQ, TQ), lambda b, g: (g, 0, 0, 0)),
            pl.BlockSpec(lam_qk.shape, lambda b, g: (0, 0)),
            pl.BlockSpec((1, dv), lambda b, g: (0, 0)),
        ],
        out_specs=pl.BlockSpec((seq, hp * dv), lambda b, g: (b, g)),
        out_shape=jax.ShapeDtypeStruct((t, d), BF16),
        scratch_shapes=[
            pltpu.VMEM((hp, nq, dv + BF16_SUBLANES, TQ), BF16),
            pltpu.VMEM((hp, 1, 2 * TQ), F32),
            pltpu.VMEM((hp, dv + BF16_SUBLANES, 2 * TQ), F32),
            pltpu.VMEM((hp, TQ, 2 * TQ), F32),
        ],
        compiler_params=_cparams(("parallel", "parallel")),
    )(qkv, qkv, qkv, bias, lam_qk.astype(F32), subln_g.reshape(1, dv).astype(F32))


def _retention_kernel(q_ref, k_ref, v_ref, dec_ref, tab_ref, o_ref, state_ref):
    seq = q_ref.shape[0]
    hp, c, _ = dec_ref.shape
    dk = q_ref.shape[1] // hp
    dv = v_ref.shape[1] // hp
    heads = range(hp)
    state_ref[...] = jnp.zeros(state_ref.shape, F32)

    def body(n, carry):
        rows = pl.ds(pl.multiple_of(n * c, c), c)
        q = [q_ref[rows, hh * dk:(hh + 1) * dk] for hh in heads]
        k = [k_ref[rows, hh * dk:(hh + 1) * dk] for hh in heads]
        v = [v_ref[rows, hh * dv:(hh + 1) * dv] for hh in heads]
        tab = [tab_ref[hh] for hh in heads]
        st = [state_ref[hh] for hh in heads]
        s = [lax.dot_general(q[hh], k[hh], (((1,), (1,)), ((), ())), preferred_element_type=F32)
             for hh in heads]
        cross = [jnp.dot(q[hh], st[hh].astype(BF16), preferred_element_type=F32) for hh in heads]
        kt = [(k[hh].astype(F32) * tab[hh][:, 1:2]).T.astype(BF16) for hh in heads]
        inner = [jnp.dot((s[hh] * dec_ref[hh]).astype(BF16), v[hh], preferred_element_type=F32)
                 for hh in heads]
        for hh in heads:
            state_ref[hh] = st[hh] * tab[hh][0:1, 2:3] + jnp.dot(kt[hh], v[hh], preferred_element_type=F32)
        for hh in heads:
            o = inner[hh] + cross[hh] * tab[hh][:, 0:1]
            o = o * lax.rsqrt(jnp.mean(o * o, axis=-1, keepdims=True) + RMS_EPS)
            o_ref[rows, hh * dv:(hh + 1) * dv] = o.astype(o_ref.dtype)
        return carry

    lax.fori_loop(0, seq // c, body, 0)


def _retention_tables(heads):
    c = RET_C
    log_gamma = jnp.log(1.0 - 2.0 ** (-5.0 - jnp.arange(heads, dtype=F32)))
    pos = jnp.arange(c, dtype=F32)
    dist = pos[:, None] - pos[None, :]
    inner_decay = jnp.where(dist[None] >= 0,
                            jnp.exp(jnp.maximum(dist, 0.0)[None] * log_gamma[:, None, None]), 0.0)
    q_decay = jnp.exp((pos + 1.0)[None] * log_gamma[:, None])
    k_decay = jnp.exp((c - 1.0 - pos)[None] * log_gamma[:, None])
    chunk_decay = jnp.broadcast_to(jnp.exp(c * log_gamma)[:, None], (heads, c))
    tab = jnp.stack([q_decay, k_decay, chunk_decay], axis=-1)
    tab = jnp.pad(tab, ((0, 0), (0, 0), (0, LANES - 3)))
    return inner_decay, tab


def _retention(qkvg, batch, seq, qk_dim, v_dim):
    t = qkvg.shape[0]
    dk = qk_dim // RET_HEADS
    dvh = v_dim // RET_HEADS
    inner_decay, tab = _retention_tables(RET_HEADS)
    hp = RET_HP
    assert RET_HEADS % hp == 0
    groups = RET_HEADS // hp
    kb = qk_dim // (hp * dk)
    vb = 2 * qk_dim // (hp * dvh)
    return pl.pallas_call(
        _retention_kernel,
        grid=(batch, groups),
        in_specs=[
            pl.BlockSpec((seq, hp * dk), lambda b, g: (b, g)),
            pl.BlockSpec((seq, hp * dk), lambda b, g: (b, kb + g)),
            pl.BlockSpec((seq, hp * dvh), lambda b, g: (b, vb + g)),
            pl.BlockSpec((hp, RET_C, RET_C), lambda b, g: (g, 0, 0)),
            pl.BlockSpec((hp, RET_C, LANES), lambda b, g: (g, 0, 0)),
        ],
        out_specs=pl.BlockSpec((seq, hp * dvh), lambda b, g: (b, g)),
        out_shape=jax.ShapeDtypeStruct((t, v_dim), BF16),
        scratch_shapes=[pltpu.VMEM((hp, dk, dvh), F32)],
        compiler_params=_cparams(("parallel", "parallel")),
    )(qkvg, qkvg, qkvg, inner_decay, tab)


def _out_router_kernel(*refs, gated, n_groups, n_experts):
    if gated:
        a_ref, gate_ref, w_ref, x_ref, nf_ref, wr_ref, br_ref = refs[:7]
        x1_ref, hn_ref, route_ref, route_t_ref, cnt_ref, carry_ref = refs[7:]
    else:
        a_ref, w_ref, x_ref, nf_ref, wr_ref, br_ref = refs[:6]
        x1_ref, hn_ref, route_ref, route_t_ref, cnt_ref, carry_ref = refs[6:]
    epg = n_experts // n_groups
    tm = x_ref.shape[0] // ROUTER_SPLIT

    @pl.when(pl.program_id(0) == 0)
    def _():
        carry_ref[...] = jnp.zeros(carry_ref.shape, F32)

    x1_parts = []
    for part in range(ROUTER_SPLIT):
        rows = slice(part * tm, (part + 1) * tm)
        a = a_ref[rows, :]
        if gated:
            gate = gate_ref[rows, :]
            a = gate * jax.nn.sigmoid(gate) * a
        x1_parts.append(x_ref[rows, :] + jnp.dot(a, w_ref[...], preferred_element_type=F32))
    for part in range(ROUTER_SPLIT):
        rows = slice(part * tm, (part + 1) * tm)
        _route_rows(x1_parts[part], rows, nf_ref, wr_ref, br_ref, x1_ref, hn_ref, route_ref, route_t_ref,
                    cnt_ref, carry_ref, n_groups=n_groups, n_experts=n_experts, epg=epg)


def _route_rows(x1, rows, nf_ref, wr_ref, br_ref, x1_ref, hn_ref, route_ref, route_t_ref, cnt_ref,
                carry_ref, *, n_groups, n_experts, epg):
    tm = x1.shape[0]
    x1_ref[rows, :] = x1
    hn = x1 * lax.rsqrt(jnp.mean(x1 * x1, axis=-1, keepdims=True) + RMS_EPS) * nf_ref[...]
    hn_ref[rows, :] = _pack_bf16_pairs(hn)

    hn_hi = hn.astype(BF16)
    hn_lo = (hn - hn_hi.astype(F32)).astype(BF16)
    logits = (jnp.dot(hn_hi, wr_ref[0], preferred_element_type=F32)
              + jnp.dot(hn_lo, wr_ref[0], preferred_element_type=F32)
              + jnp.dot(hn_hi, wr_ref[1], preferred_element_type=F32)) + br_ref[...]

    n_rows = -(-(n_groups + n_experts) // SUBLANES) * SUBLANES
    lt = logits.T[:n_rows, :]
    row = lax.broadcasted_iota(jnp.int32, lt.shape, 0)
    big = jnp.int32(LANES)
    neg = jnp.float32(-jnp.inf)

    gl = jnp.where(row < n_groups, lt, neg)
    gmax = jnp.max(gl, axis=0, keepdims=True)
    g_p = 1.0 / jnp.sum(jnp.exp(gl - gmax), axis=0, keepdims=True)
    g_idx = jnp.min(jnp.where(gl == gmax, row, big), axis=0, keepdims=True)

    lo = n_groups + g_idx * epg
    el = jnp.where((row >= lo) & (row < lo + epg), lt, neg)
    v1 = jnp.max(el, axis=0, keepdims=True)
    i1 = jnp.min(jnp.where(el == v1, row, big), axis=0, keepdims=True)
    el2 = jnp.where(row == i1, neg, el)
    v2 = jnp.max(el2, axis=0, keepdims=True)
    i2 = jnp.min(jnp.where(el2 == v2, row, big), axis=0, keepdims=True)
    tt = jnp.exp(v2 - v1)
    w1 = g_p / (1.0 + tt)
    w2 = g_p * tt / (1.0 + tt)
    e1 = i1 - n_groups
    e2 = i2 - n_groups

    erow = lax.broadcasted_iota(jnp.int32, (n_experts, tm), 0)
    oh1 = erow == e1
    oh2 = erow == e2
    onehot = jnp.where(oh1 | oh2, 1.0, 0.0)
    before = lax.broadcasted_iota(jnp.int32, (tm, tm), 0) < lax.broadcasted_iota(jnp.int32, (tm, tm), 1)
    tri = jnp.where(before, 1.0, 0.0).astype(BF16)
    carry = carry_ref[...]
    base = jnp.dot(onehot.astype(BF16), tri, preferred_element_type=F32) + carry[:, 0:1]
    rank1 = jnp.sum(jnp.where(oh1, base, 0.0), axis=0, keepdims=True)
    rank2 = jnp.sum(jnp.where(oh2, base, 0.0), axis=0, keepdims=True)
    carry = carry + jnp.sum(onehot, axis=1, keepdims=True)
    carry_ref[...] = carry
    cnt_ref[...] = carry

    frow = lax.broadcasted_iota(jnp.int32, (LANES, tm), 0)
    out = jnp.zeros((LANES, tm), F32)
    for idx, val in enumerate((e1.astype(F32), e2.astype(F32), w1, w2, rank1, rank2)):
        out = jnp.where(frow == idx, val, out)
    route_t_ref[:, rows] = out[:route_t_ref.shape[0], :]
    route_ref[rows, :] = out.T


def _out_router(a, gate, w, x, nf, w_group, b_group, w_expert, b_expert):
    t, d = x.shape
    k = w.shape[0]
    n_groups = w_group.shape[1]
    n_experts = w_expert.shape[1]
    assert n_groups + n_experts <= LANES
    pad = LANES - n_groups - n_experts
    wr = jnp.pad(jnp.concatenate([w_group, w_expert], axis=1).astype(F32), ((0, 0), (0, pad)))
    wr_hi = wr.astype(BF16)
    wr = jnp.stack([wr_hi, (wr - wr_hi.astype(F32)).astype(BF16)])
    br = jnp.pad(jnp.concatenate([b_group, b_expert]).astype(F32), (0, pad)).reshape(1, LANES)
    gated = gate is not None
    row = lambda i: (i, 0)
    fixed = lambda i: (0, 0)
    in_specs = [pl.BlockSpec((TM, k), row)]
    args = [a]
    if gated:
        gate_arr, gate_col = gate
        in_specs.append(pl.BlockSpec((TM, k), lambda i: (i, gate_col)))
        args.append(gate_arr)
    in_specs += [
        pl.BlockSpec((k, d), fixed),
        pl.BlockSpec((TM, d), row),
        pl.BlockSpec((1, d), fixed),
        pl.BlockSpec((2, d, LANES), lambda i: (0, 0, 0)),
        pl.BlockSpec((1, LANES), fixed),
    ]
    args += [w, x, nf.reshape(1, d), wr, br]
    return pl.pallas_call(
        functools.partial(_out_router_kernel, gated=gated, n_groups=n_groups, n_experts=n_experts),
        grid=(t // TM,),
        in_specs=in_specs,
        out_specs=[
            pl.BlockSpec((TM, d), row),
            pl.BlockSpec((TM, d // 2), row),
            pl.BlockSpec((TM, LANES), row),
            pl.BlockSpec((SUBLANES, TM), lambda i: (0, i)),
            pl.BlockSpec((n_experts, LANES), fixed),
        ],
        out_shape=[
            jax.ShapeDtypeStruct((t, d), F32),
            jax.ShapeDtypeStruct((t, d // 2), jnp.uint32),
            jax.ShapeDtypeStruct((t, LANES), F32),
            jax.ShapeDtypeStruct((SUBLANES, t), F32),
            jax.ShapeDtypeStruct((n_experts, LANES), F32),
        ],
        scratch_shapes=[pltpu.VMEM((n_experts, LANES), F32)],
        compiler_params=_cparams(("arbitrary",)),
    )(*args)


def _sc_mesh():
    return plsc.VectorSubcoreMesh(core_axis_name="c", subcore_axis_name="s",
                                  num_cores=SC_CORES, num_subcores=SC_SUBCORES)


def _sc_worker():
    return lax.axis_index("s") * SC_CORES + lax.axis_index("c")


def _sc_scatter_rows(src, d1, d2, n_rows):
    t, d = src.shape
    workers = SC_CORES * SC_SUBCORES
    per_w = t // workers
    n_ch = per_w // SC_CHUNK
    assert t % (workers * SC_CHUNK) == 0
    idx = jnp.stack([d1, d2]).reshape(2, workers, n_ch, SC_CHUNK)

    @functools.partial(
        pl.kernel, mesh=_sc_mesh(),
        out_type=jax.ShapeDtypeStruct((n_rows, d), src.dtype),
        scratch_types=[pltpu.VMEM((2, n_ch, SC_CHUNK), jnp.int32),
                       pltpu.VMEM((SC_CHUNK, d), src.dtype),
                       pltpu.VMEM((SC_CHUNK, d), src.dtype),
                       pltpu.SemaphoreType.DMA, pltpu.SemaphoreType.DMA],
    )
    def scatter(src_hbm, idx_hbm, out_hbm, idx_v, rows_a, rows_b, sem_a, sem_b):
        wid = _sc_worker()
        base = wid * per_w
        for s in range(2):
            pltpu.sync_copy(idx_hbm.at[s, wid], idx_v.at[s])

        def move(c, rows_v, sem):
            pltpu.sync_copy(src_hbm.at[pl.ds(base + c * SC_CHUNK, SC_CHUNK)], rows_v)
            return [pltpu.async_copy(rows_v, out_hbm.at[idx_v.at[s, c]], sem) for s in range(2)]

        @pl.loop(0, n_ch, step=2)
        def _(c):
            first = move(c, rows_a, sem_a)
            second = move(c + 1, rows_b, sem_b)
            for cp in first + second:
                cp.wait()

    return scatter(src, idx)


def _sc_gather_rows(table, idx):
    b = idx.shape[0]
    d = table.shape[1]
    workers = SC_CORES * SC_SUBCORES
    per_w = b // workers
    n_ch = per_w // SC_CHUNK
    assert b % (workers * SC_CHUNK * 2) == 0

    @functools.partial(
        pl.kernel, mesh=_sc_mesh(),
        out_type=jax.ShapeDtypeStruct((b, d), table.dtype),
        scratch_types=[pltpu.VMEM((per_w,), jnp.int32),
                       pltpu.VMEM((SC_CHUNK, d), table.dtype),
                       pltpu.VMEM((SC_CHUNK, d), table.dtype),
                       pltpu.SemaphoreType.DMA, pltpu.SemaphoreType.DMA],
    )
    def gather(table_hbm, idx_hbm, out_hbm, idx_v, rows_a, rows_b, sem_a, sem_b):
        base = _sc_worker() * per_w
        pltpu.sync_copy(idx_hbm.at[pl.ds(base, per_w)], idx_v)

        def fetch(c, rows_v, sem):
            return pltpu.async_copy(table_hbm.at[idx_v.at[pl.ds(c * SC_CHUNK, SC_CHUNK)]], rows_v, sem)

        def store(c, rows_v):
            pltpu.sync_copy(rows_v, out_hbm.at[pl.ds(base + c * SC_CHUNK, SC_CHUNK)])

        @pl.loop(0, n_ch, step=2)
        def _(c):
            first = fetch(c, rows_a, sem_a)
            second = fetch(c + 1, rows_b, sem_b)
            first.wait()
            store(c, rows_a)
            second.wait()
            store(c + 1, rows_b)

    return gather(table, idx)


def _expert_kernel(fb_ref, nblk_ref, nv_ref, nu_ref, xs_ref, wg_ref, wu_ref, wd_ref, ys_ref,
                   wgb_ref, wub_ref, wdb_ref, xbuf, ybuf, xsem, ysem):
    e = pl.program_id(0)
    rb = xbuf.shape[1]
    nb = ys_ref.shape[0] // rb
    n_used = nu_ref[0]

    def x_copy(g, slot):
        rows = pl.ds(pl.multiple_of(g * rb, rb), rb)
        return pltpu.make_async_copy(xs_ref.at[rows], xbuf.at[slot], xsem.at[slot])

    def y_copy(g, slot):
        rows = pl.ds(pl.multiple_of(g * rb, rb), rb)
        return pltpu.make_async_copy(ybuf.at[slot], ys_ref.at[rows], ysem.at[slot])

    @pl.when(e == 0)
    def _():
        x_copy(0, 0).start()

    @pl.when(nblk_ref[e] > 0)
    def _():
        wgb_ref[...] = wg_ref[0, 0].astype(BF16)
        wub_ref[...] = wu_ref[0, 0].astype(BF16)
        wdb_ref[...] = wd_ref[0, 0].astype(BF16)

    def block(b, carry):
        g = fb_ref[e] + b
        slot = lax.rem(g, 2)
        x_copy(g, slot).wait()

        @pl.when(g + 1 < n_used)
        def _():
            x_copy(g + 1, 1 - slot).start(priority=1)

        @pl.when(g >= 2)
        def _():
            y_copy(g - 2, slot).wait()

        nv = nv_ref[g]

        def ffn(rows):
            valid = lax.broadcasted_iota(jnp.int32, (rows, 1), 0) < nv
            x_hi, x_lo = _unpack_bf16_pairs(jnp.where(valid, xbuf[slot, :rows, :], jnp.uint32(0)))
            x_hi, x_lo = x_hi.astype(BF16), x_lo.astype(BF16)
            h = x_hi.shape[1]

            def in_dot(w_ref):
                return (jnp.dot(x_hi, w_ref[:h, :], preferred_element_type=F32)
                        + jnp.dot(x_lo, w_ref[h:, :], preferred_element_type=F32))

            gate = in_dot(wgb_ref)
            up = in_dot(wub_ref)
            hmid = (gate * jax.nn.sigmoid(gate) * up).astype(BF16)
            ybuf[slot, :rows, :] = _pack_bf16_pairs(jnp.dot(hmid, wdb_ref[...], preferred_element_type=F32))
            if rows < rb:
                ybuf[slot, rows:, :] = jnp.zeros((rb - rows, ybuf.shape[2]), ybuf.dtype)

        @pl.when(nv > rb // 2)
        def _():
            ffn(rb)

        @pl.when(nv <= rb // 2)
        def _():
            ffn(rb // 2)

        y_copy(g, slot).start()
        return carry

    lax.fori_loop(0, nblk_ref[e], block, 0)

    @pl.when(e == pl.num_programs(0) - 1)
    def _():
        @pl.when(n_used >= 2)
        def _():
            y_copy(n_used - 2, lax.rem(n_used, 2)).wait()

        y_copy(n_used - 1, lax.rem(n_used - 1, 2)).wait()
        ybuf[0] = jnp.zeros(ybuf.shape[1:], ybuf.dtype)

        def tail_start(g, carry):
            y_copy(g, 0).start()
            return carry

        def tail_wait(g, carry):
            y_copy(g, 0).wait()
            return carry

        lax.fori_loop(n_used, nb, tail_start, 0)
        lax.fori_loop(n_used, nb, tail_wait, 0)


def _experts(xs, first_block, n_blocks, n_valid, n_used, w_gate, w_up, w_down, layer):
    n_rows, dp = xs.shape
    d = 2 * dp
    n_experts, ff = w_gate.shape[1], w_gate.shape[3]
    wsel = lambda e, *_: (layer, e, 0, 0)
    grid_spec = pltpu.PrefetchScalarGridSpec(
        num_scalar_prefetch=4,
        grid=(n_experts,),
        in_specs=[
            pl.BlockSpec(memory_space=pl.ANY),
            pl.BlockSpec((1, 1, d, ff), wsel),
            pl.BlockSpec((1, 1, d, ff), wsel),
            pl.BlockSpec((1, 1, ff, d), wsel),
        ],
        out_specs=pl.BlockSpec(memory_space=pl.ANY),
        scratch_shapes=[pltpu.VMEM((d, ff), BF16), pltpu.VMEM((d, ff), BF16), pltpu.VMEM((ff, d), BF16),
                        pltpu.VMEM((2, ROW_BLOCK, dp), xs.dtype), pltpu.VMEM((2, ROW_BLOCK, dp), xs.dtype),
                        pltpu.SemaphoreType.DMA((2,)), pltpu.SemaphoreType.DMA((2,))],
    )
    return pl.pallas_call(
        _expert_kernel,
        grid_spec=grid_spec,
        out_shape=jax.ShapeDtypeStruct((n_rows, dp), xs.dtype),
        compiler_params=_cparams(("arbitrary",)),
    )(first_block, n_blocks, n_valid, n_used, xs, w_gate, w_up, w_down)


def _final_combine_kernel(x_ref, route_ref, y1_ref, y2_ref, nf_ref, o_ref):
    route = route_ref[...]
    y1 = jnp.concatenate(_unpack_bf16_pairs(y1_ref[...]), axis=1)
    y2 = jnp.concatenate(_unpack_bf16_pairs(y2_ref[...]), axis=1)
    y = x_ref[...] + (route[:, 2:3] * y1 + route[:, 3:4] * y2)
    o_ref[...] = y * lax.rsqrt(jnp.mean(y * y, axis=-1, keepdims=True) + RMS_EPS) * nf_ref[...]


def _final_combine(x1, route, yg, nf):
    t, d = x1.shape
    slot2 = t // TM
    row = lambda i: (i, 0)
    return pl.pallas_call(
        _final_combine_kernel,
        grid=(t // TM,),
        in_specs=[
            pl.BlockSpec((TM, d), row),
            pl.BlockSpec((TM, LANES), row),
            pl.BlockSpec((TM, d // 2), row),
            pl.BlockSpec((TM, d // 2), lambda i: (slot2 + i, 0)),
            pl.BlockSpec((1, d), lambda i: (0, 0)),
        ],
        out_specs=pl.BlockSpec((TM, d), row),
        out_shape=jax.ShapeDtypeStruct((t, d), F32),
        compiler_params=_cparams(("parallel",)),
    )(x1, route, yg, yg, nf.reshape(1, d).astype(F32))


def _moe(hn, route_t, cnt, w_gate, w_up, w_down, layer):
    t, d = hn.shape
    n_experts = w_gate.shape[1]
    ids = jnp.arange(n_experts, dtype=jnp.int32)
    e1, e2, rank1, rank2 = (route_t[f].astype(jnp.int32) for f in (0, 1, 4, 5))
    counts = cnt[:, 0].astype(jnp.int32)
    padded = ((counts + ROW_BLOCK - 1) // ROW_BLOCK) * ROW_BLOCK
    pad_end = jnp.cumsum(padded)
    pad_start = pad_end - padded

    def dest(e, rank):
        return jnp.sum(jnp.where(e[:, None] == ids[None, :], pad_start[None, :], 0), axis=1) + rank

    d1 = dest(e1, rank1)
    d2 = dest(e2, rank2)
    nb = (t * TOP_K_INNER) // ROW_BLOCK + n_experts
    n_used = (pad_end[-1:] // ROW_BLOCK).astype(jnp.int32)
    first_row = jnp.arange(nb, dtype=jnp.int32) * ROW_BLOCK
    block_e = jnp.sum((pad_end[None, :] <= first_row[:, None]).astype(jnp.int32), axis=1)
    block_e = jnp.minimum(block_e, n_experts - 1)
    row_end = jnp.sum(jnp.where(block_e[:, None] == ids[None, :], (pad_start + counts)[None, :], 0), axis=1)
    n_valid = jnp.clip(row_end - first_row, 0, ROW_BLOCK).astype(jnp.int32)
    xs = _sc_scatter_rows(hn, d1, d2, nb * ROW_BLOCK)
    ys = _experts(xs, (pad_start // ROW_BLOCK).astype(jnp.int32), (padded // ROW_BLOCK).astype(jnp.int32),
                  n_valid, n_used, w_gate, w_up, w_down, layer)
    return _sc_gather_rows(ys, jnp.concatenate([d1, d2]))


def _diff_lambda_init(layer_idx):
    return 0.8 - 0.6 * math.exp(-0.3 * layer_idx)


def _rope_tables(seq, dk):
    angle = 1.0 / (10000.0 ** jnp.linspace(0.0, 1.0, dk // 2, dtype=F32))
    angle = jnp.repeat(angle, 2)
    phase = jnp.arange(seq, dtype=F32)[:, None] * angle[None, :]
    sin, cos = jnp.sin(phase), jnp.cos(phase)
    even = (jnp.arange(dk) % 2 == 0)[None, :]
    return cos, jnp.where(even, -sin, 0.0), jnp.where(even, 0.0, sin)


def kernel(x, rel_bias_table, norm_mix, norm_ffn, norm_final, diff_w_in, diff_lambda_qk, diff_subln,
           diff_w_out, ret_w_in, ret_w_out, moe_w_group, moe_b_group, moe_w_expert, moe_b_expert,
           moe_w_gate, moe_w_up, moe_w_down):
    batch, seq, d = x.shape
    depth = norm_mix.shape[0]
    t = batch * seq
    heads = rel_bias_table.shape[1]
    dh = diff_lambda_qk.shape[-1]
    qk_dim = (ret_w_in.shape[2] - 2 * ret_w_out.shape[1]) // 2
    v_dim = ret_w_out.shape[1]
    dk = qk_dim // RET_HEADS
    xt = x.reshape(t, d)
    moe = None
    for i in range(depth):
        j = i // N_MIXERS
        if i % N_MIXERS == 0:
            col_scale = jnp.concatenate([jnp.full((d,), dh ** -0.5 * LOG2E, F32), jnp.ones((2 * d,), F32)])
            qkv = _norm_matmul(xt, norm_mix[i], diff_w_in, j, col_scale, moe=moe)
            if moe is not None:
                qkv, xt = qkv
            mix = _diff_attention(qkv, rel_bias_table, diff_lambda_qk[j], diff_subln[j],
                                  _diff_lambda_init(i), batch, seq, heads)
            gate = None
            w_out = diff_w_out[j].astype(BF16)
        else:
            n_in = ret_w_in.shape[2]
            col_scale = jnp.concatenate([jnp.ones((qk_dim,), F32), jnp.full((qk_dim,), dk ** -0.5, F32),
                                         jnp.ones((n_in - 2 * qk_dim,), F32)])
            qkvg = _norm_matmul(xt, norm_mix[i], ret_w_in, j, col_scale, rope=_rope_tables(seq, dk),
                                n_rope=2 * qk_dim // TN, seq=seq, moe=moe)
            if moe is not None:
                qkvg, xt = qkvg
            mix = _retention(qkvg, batch, seq, qk_dim, v_dim)
            gate = (qkvg, (2 * qk_dim + v_dim) // v_dim)
            w_out = ret_w_out[j].astype(BF16)
        x1, hn, route, route_t, cnt = _out_router(mix, gate, w_out, xt, norm_ffn[i], moe_w_group[i],
                                                  moe_b_group[i], moe_w_expert[i], moe_b_expert[i])
        xt = x1
        moe = (route, _moe(hn, route_t, cnt, moe_w_gate, moe_w_up, moe_w_down, i))
    return _final_combine(xt, moe[0], moe[1], norm_final).reshape(batch, seq, d)
```

```python
import functools
import math

import jax
import jax.numpy as jnp
from jax import lax
from jax.experimental import pallas as pl
from jax.experimental.pallas import tpu as pltpu
from jax.experimental.pallas import tpu_sc as plsc

F32 = jnp.float32
BF16 = jnp.bfloat16

N_MIXERS = 2
NUM_BUCKETS = 32
MAX_DISTANCE = 128
RET_HEADS = 4
TOP_K_INNER = 2
RMS_EPS = 1e-6
SUBLN_EPS = 1e-5

LANES = 128
SUBLANES = 8
BF16_SUBLANES = 16
LOG2E = 1.4426950408889634
VMEM_LIMIT = 56 * 1024 * 1024
SC_CORES = 2
SC_SUBCORES = 16

TM = 512
TN = 1024
TQ = 256
ATTN_HEADS = 4
RET_C = 256
RET_HP = 2
EXPERTS_PER_STEP = 2
ROUTER_SPLIT = 2
ROW_BLOCK = 512
SC_CHUNK = 64


def _cparams(sem):
    return pltpu.CompilerParams(dimension_semantics=sem, vmem_limit_bytes=VMEM_LIMIT)


def _pack_bf16_pairs(x):
    h = x.shape[1] // 2
    hi = lax.bitcast_convert_type(x[:, :h].astype(BF16).astype(F32), jnp.uint32)
    lo = lax.bitcast_convert_type(x[:, h:].astype(BF16).astype(F32), jnp.uint32)
    return hi | (lo >> 16)


def _unpack_bf16_pairs(w):
    hi = lax.bitcast_convert_type(w & jnp.uint32(0xFFFF0000), F32)
    lo = lax.bitcast_convert_type(w << 16, F32)
    return hi, lo


def _norm_matmul_kernel(x_ref, *rest, n_rope, moe, layer):
    rest = list(rest)
    w_ref, stage_ref, sem = rest[-3:]
    rest = rest[:-3]
    if moe:
        route_ref, y1_ref, y2_ref = rest[:3]
        x_out_ref = rest.pop()
        rest = rest[3:]
    g_ref, scale_ref, w_hbm = rest[:3]
    if n_rope:
        cos_ref, sa_ref, sb_ref, o_ref = rest[3:]
    else:
        (o_ref,) = rest[3:]
    n_chunks = o_ref.shape[1] // TN

    @pl.when(pl.program_id(0) == 0)
    def _():
        def fetch(j):
            return pltpu.make_async_copy(w_hbm.at[layer, :, pl.ds(j * TN, TN)], stage_ref.at[j % 2],
                                         sem.at[j % 2])

        fetch(0).start()
        for j in range(n_chunks):
            if j + 1 < n_chunks:
                fetch(j + 1).start()
            fetch(j).wait()
            cols = slice(j * TN, (j + 1) * TN)
            w_ref[:, cols] = (stage_ref[j % 2] * scale_ref[:, cols]).astype(BF16)

    x = x_ref[...]
    if moe:
        route = route_ref[...]
        y1 = jnp.concatenate(_unpack_bf16_pairs(y1_ref[...]), axis=1)
        y2 = jnp.concatenate(_unpack_bf16_pairs(y2_ref[...]), axis=1)
        x = x + (route[:, 2:3] * y1 + route[:, 3:4] * y2)
        x_out_ref[...] = x
    xn = (x * lax.rsqrt(jnp.mean(x * x, axis=-1, keepdims=True) + RMS_EPS) * g_ref[...]).astype(BF16)
    for j in range(n_chunks):
        cols = slice(j * TN, (j + 1) * TN)
        acc = jnp.dot(xn, w_ref[:, cols], preferred_element_type=F32)
        if j < n_rope:
            reps = TN // cos_ref.shape[1]
            c = jnp.tile(cos_ref[...], (1, reps))
            sa = jnp.tile(sa_ref[...], (1, reps))
            sb = jnp.tile(sb_ref[...], (1, reps))
            nxt = pltpu.roll(acc, TN - 1, axis=1)
            prv = pltpu.roll(acc, 1, axis=1)
            acc = acc * c + nxt * sa + prv * sb
        o_ref[:, cols] = acc.astype(o_ref.dtype)


def _norm_matmul(x, g, w, layer, col_scale, rope=None, n_rope=0, seq=None, moe=None):
    t, d = x.shape
    n = w.shape[2]
    row = lambda i: (i, 0)
    in_specs = [pl.BlockSpec((TM, d), row)]
    args = [x]
    if moe is not None:
        route, yg = moe
        slot2 = t // TM
        in_specs += [pl.BlockSpec((TM, LANES), row), pl.BlockSpec((TM, d // 2), row),
                     pl.BlockSpec((TM, d // 2), lambda i: (slot2 + i, 0))]
        args += [route, yg, yg]
    in_specs += [
        pl.BlockSpec((1, d), lambda i: (0, 0)),
        pl.BlockSpec((1, n), lambda i: (0, 0)),
        pl.BlockSpec(memory_space=pl.ANY),
    ]
    args += [g.reshape(1, d), col_scale.reshape(1, n).astype(F32), w]
    if rope is not None:
        nb = seq // TM
        dk = rope[0].shape[1]
        in_specs += [pl.BlockSpec((TM, dk), lambda i: (i % nb, 0))] * 3
        args += list(rope)
    out_specs = [pl.BlockSpec((TM, n), row)]
    out_shape = [jax.ShapeDtypeStruct((t, n), BF16)]
    if moe is not None:
        out_specs.append(pl.BlockSpec((TM, d), row))
        out_shape.append(jax.ShapeDtypeStruct((t, d), F32))
    out = pl.pallas_call(
        functools.partial(_norm_matmul_kernel, n_rope=n_rope, moe=moe is not None, layer=layer),
        grid=(t // TM,),
        in_specs=in_specs,
        out_specs=out_specs,
        out_shape=out_shape,
        scratch_shapes=[pltpu.VMEM((d, n), BF16), pltpu.VMEM((2, d, TN), F32), pltpu.SemaphoreType.DMA((2,))],
        compiler_params=_cparams(("arbitrary",)),
    )(*args)
    return out if moe is not None else out[0]


def _diff_attn_kernel(q_ref, k_ref, v_ref, bias_ref, lam_ref, g_ref, o_ref,
                      vt_ref, m_ref, acc_ref, s_ref, *, lambda_init):
    seq = q_ref.shape[0]
    hp, _, _, tq = bias_ref.shape
    dv = q_ref.shape[1] // hp
    dh = dv // 2
    nq = seq // tq

    def head(hh):
        return slice(hh * dv, (hh + 1) * dv)

    extra = vt_ref.shape[2] - dv
    ones_row = jnp.where(lax.broadcasted_iota(jnp.int32, (extra, tq), 0) == 0, 1.0, 0.0).astype(BF16)
    for hh in range(hp):
        for j in range(nq):
            vt_ref[hh, j, :dv, :] = v_ref[j * tq:(j + 1) * tq, head(hh)].astype(F32).T.astype(BF16)
            vt_ref[hh, j, dv:, :] = ones_row

    lq = lam_ref[...]
    lam = (jnp.exp(jnp.sum(lq[0:1] * lq[1:2], axis=-1, keepdims=True))
           - jnp.exp(jnp.sum(lq[2:3] * lq[3:4], axis=-1, keepdims=True)) + lambda_init)
    lane = lax.broadcasted_iota(jnp.int32, (tq, dv), 1)

    def query_maps(qi):
        qrows = pl.ds(pl.multiple_of(qi * tq, tq), tq)
        qs = []
        for hh in range(hp):
            q = q_ref[qrows, head(hh)]
            zero = jnp.zeros_like(q)
            qs.append(jnp.concatenate([jnp.where(lane < dh, q, zero), jnp.where(lane >= dh, q, zero)],
                                      axis=0))
        return qs

    def score_dots(j, qs):
        rows = pl.ds(pl.multiple_of(j * tq, tq), tq)
        for hh in range(hp):
            s_ref[hh] = lax.dot_general(k_ref[rows, head(hh)], qs[hh], (((1,), (1,)), ((), ())),
                                        preferred_element_type=F32)

    score_dots(0, query_maps(0))

    def q_body(qi, carry):
        qrows = pl.ds(pl.multiple_of(qi * tq, tq), tq)
        qs = query_maps(qi)
        qs_next = query_maps(jnp.minimum(qi + 1, nq - 1))

        m_ref[...] = jnp.full(m_ref.shape, -jnp.inf, F32)
        acc_ref[...] = jnp.zeros(acc_ref.shape, F32)

        def step(j, tile, prefetch):
            probs, alphas = [], []
            for hh in range(hp):
                s = s_ref[hh]
                if tile is not None:
                    bias = bias_ref[hh, tile]
                    s = s + jnp.concatenate([bias, bias], axis=1)
                m_old = m_ref[hh]
                m_new = jnp.maximum(m_old, jnp.max(s, axis=0, keepdims=True))
                alpha = jnp.exp2(m_old - m_new)
                m_ref[hh] = m_new
                probs.append(jnp.exp2(s - m_new).astype(BF16))
                alphas.append(alpha)
            if prefetch == "block":
                score_dots(j + 1, qs)
            else:
                score_dots(0, qs_next)
            for hh in range(hp):
                acc_ref[hh] = alphas[hh] * acc_ref[hh] + jnp.dot(vt_ref[hh, j], probs[hh],
                                                                 preferred_element_type=F32)

        def far_body(j, c):
            step(j, None, "block")
            return c

        lax.fori_loop(0, jnp.maximum(qi - 1, 0), far_body, 0)

        @pl.when(qi >= 1)
        def _():
            step(qi - 1, 1, "block")

        step(qi, 0, "tile")

        for hh in range(hp):
            acc = acc_ref[hh]
            on = acc[:dv] / acc[dv:dv + 1]
            ot = on[:, :tq] - lam * on[:, tq:]
            ot = ot * lax.rsqrt(jnp.mean(ot * ot, axis=0, keepdims=True) + SUBLN_EPS)
            o_ref[qrows, head(hh)] = (ot.T * g_ref[...] * (1.0 - lambda_init)).astype(o_ref.dtype)
        return carry

    lax.fori_loop(0, nq, q_body, 0)


def _t5_bucket(rel):
    max_exact = NUM_BUCKETS // 2
    n = jnp.maximum(rel, 0)
    is_small = n < max_exact
    nf = jnp.maximum(n, max_exact).astype(F32)
    large = max_exact + (jnp.log(nf / max_exact) / math.log(MAX_DISTANCE / max_exact)
                         * (NUM_BUCKETS - max_exact)).astype(jnp.int32)
    large = jnp.minimum(large, NUM_BUCKETS - 1)
    return jnp.where(is_small, n, large)


def _bias_tiles(rel_table):
    heads = rel_table.shape[1]
    span = 2 * TQ + 1
    bucket = _t5_bucket(jnp.arange(span))
    onehot = (bucket[:, None] == jnp.arange(NUM_BUCKETS)[None, :]).astype(F32)
    table = (rel_table.astype(F32) - rel_table[NUM_BUCKETS - 1].astype(F32)[None, :]) * LOG2E
    by_rel = jnp.dot(onehot, table, precision=lax.Precision.HIGHEST).T
    toep = jnp.tile(by_rel, (1, TQ))[:, :TQ * (span - 1)].reshape(heads, TQ, span - 1)
    kj = jnp.arange(TQ)[:, None]
    qi = jnp.arange(TQ)[None, :]
    diag = jnp.where((qi >= kj)[None], toep[:, :, :TQ], -jnp.inf)
    return jnp.stack([diag, toep[:, :, TQ:]], axis=1)


def _diff_attention(qkv, rel_table, lam_qk, subln_g, lambda_init, batch, seq, heads):
    t = qkv.shape[0]
    d = qkv.shape[1] // 3
    dv = d // heads
    hp = ATTN_HEADS
    assert TQ + 1 >= MAX_DISTANCE and seq % TQ == 0 and dv == LANES and heads % hp == 0
    nq = seq // TQ
    groups = heads // hp
    bias = _bias_tiles(rel_table)
    return pl.pallas_call(
        functools.partial(_diff_attn_kernel, lambda_init=lambda_init),
        grid=(batch, groups),
        in_specs=[
            pl.BlockSpec((seq, hp * dv), lambda b, g: (b, g)),
            pl.BlockSpec((seq, hp * dv), lambda b, g: (b, groups + g)),
            pl.BlockSpec((seq, hp * dv), lambda b, g: (b, 2 * groups + g)),
            pl.BlockSpec((hp, 2, TQ, TQ), lambda b, g: (g, 0, 0, 0)),
            pl.BlockSpec(lam_qk.shape, lambda b, g: (0, 0)),
            pl.BlockSpec((1, dv), lambda b, g: (0, 0)),
        ],
        out_specs=pl.BlockSpec((seq, hp * dv), lambda b, g: (b, g)),
        out_shape=jax.ShapeDtypeStruct((t, d), BF16),
        scratch_shapes=[
            pltpu.VMEM((hp, nq, dv + BF16_SUBLANES, TQ), BF16),
            pltpu.VMEM((hp, 1, 2 * TQ), F32),
            pltpu.VMEM((hp, dv + BF16_SUBLANES, 2 * TQ), F32),
            pltpu.VMEM((hp, TQ, 2 * TQ), F32),
        ],
        compiler_params=_cparams(("parallel", "parallel")),
    )(qkv, qkv, qkv, bias, lam_qk.astype(F32), subln_g.reshape(1, dv).astype(F32))


def _retention_kernel(q_ref, k_ref, v_ref, dec_ref, tab_ref, o_ref, state_ref):
    seq = q_ref.shape[0]
    hp, c, _ = dec_ref.shape
    dk = q_ref.shape[1] // hp
    dv = v_ref.shape[1] // hp
    heads = range(hp)
    state_ref[...] = jnp.zeros(state_ref.shape, F32)

    def body(n, carry):
        rows = pl.ds(pl.multiple_of(n * c, c), c)
        q = [q_ref[rows, hh * dk:(hh + 1) * dk] for hh in heads]
        k = [k_ref[rows, hh * dk:(hh + 1) * dk] for hh in heads]
        v = [v_ref[rows, hh * dv:(hh + 1) * dv] for hh in heads]
        tab = [tab_ref[hh] for hh in heads]
        st = [state_ref[hh] for hh in heads]
        s = [lax.dot_general(q[hh], k[hh], (((1,), (1,)), ((), ())), preferred_element_type=F32)
             for hh in heads]
        cross = [jnp.dot(q[hh], st[hh].astype(BF16), preferred_element_type=F32) for hh in heads]
        kt = [(k[hh].astype(F32) * tab[hh][:, 1:2]).T.astype(BF16) for hh in heads]
        inner = [jnp.dot((s[hh] * dec_ref[hh]).astype(BF16), v[hh], preferred_element_type=F32)
                 for hh in heads]
        for hh in heads:
            state_ref[hh] = st[hh] * tab[hh][0:1, 2:3] + jnp.dot(kt[hh], v[hh], preferred_element_type=F32)
        for hh in heads:
            o = inner[hh] + cross[hh] * tab[hh][:, 0:1]
            o = o * lax.rsqrt(jnp.mean(o * o, axis=-1, keepdims=True) + RMS_EPS)
            o_ref[rows, hh * dv:(hh + 1) * dv] = o.astype(o_ref.dtype)
        return carry

    lax.fori_loop(0, seq // c, body, 0)


def _retention_tables(heads):
    c = RET_C
    log_gamma = jnp.log(1.0 - 2.0 ** (-5.0 - jnp.arange(heads, dtype=F32)))
    pos = jnp.arange(c, dtype=F32)
    dist = pos[:, None] - pos[None, :]
    inner_decay = jnp.where(dist[None] >= 0,
                            jnp.exp(jnp.maximum(dist, 0.0)[None] * log_gamma[:, None, None]), 0.0)
    q_decay = jnp.exp((pos + 1.0)[None] * log_gamma[:, None])
    k_decay = jnp.exp((c - 1.0 - pos)[None] * log_gamma[:, None])
    chunk_decay = jnp.broadcast_to(jnp.exp(c * log_gamma)[:, None], (heads, c))
    tab = jnp.stack([q_decay, k_decay, chunk_decay], axis=-1)
    tab = jnp.pad(tab, ((0, 0), (0, 0), (0, LANES - 3)))
    return inner_decay, tab


def _retention(qkvg, batch, seq, qk_dim, v_dim):
    t = qkvg.shape[0]
    dk = qk_dim // RET_HEADS
    dvh = v_dim // RET_HEADS
    inner_decay, tab = _retention_tables(RET_HEADS)
    hp = RET_HP
    assert RET_HEADS % hp == 0
    groups = RET_HEADS // hp
    kb = qk_dim // (hp * dk)
    vb = 2 * qk_dim // (hp * dvh)
    return pl.pallas_call(
        _retention_kernel,
        grid=(batch, groups),
        in_specs=[
            pl.BlockSpec((seq, hp * dk), lambda b, g: (b, g)),
            pl.BlockSpec((seq, hp * dk), lambda b, g: (b, kb + g)),
            pl.BlockSpec((seq, hp * dvh), lambda b, g: (b, vb + g)),
            pl.BlockSpec((hp, RET_C, RET_C), lambda b, g: (g, 0, 0)),
            pl.BlockSpec((hp, RET_C, LANES), lambda b, g: (g, 0, 0)),
        ],
        out_specs=pl.BlockSpec((seq, hp * dvh), lambda b, g: (b, g)),
        out_shape=jax.ShapeDtypeStruct((t, v_dim), BF16),
        scratch_shapes=[pltpu.VMEM((hp, dk, dvh), F32)],
        compiler_params=_cparams(("parallel", "parallel")),
    )(qkvg, qkvg, qkvg, inner_decay, tab)


def _out_router_kernel(*refs, gated, n_groups, n_experts):
    if gated:
        a_ref, gate_ref, w_ref, x_ref, nf_ref, wr_ref, br_ref = refs[:7]
        x1_ref, hn_ref, route_ref, route_t_ref, cnt_ref, carry_ref = refs[7:]
    else:
        a_ref, w_ref, x_ref, nf_ref, wr_ref, br_ref = refs[:6]
        x1_ref, hn_ref, route_ref, route_t_ref, cnt_ref, carry_ref = refs[6:]
    epg = n_experts // n_groups
    tm = x_ref.shape[0] // ROUTER_SPLIT

    @pl.when(pl.program_id(0) == 0)
    def _():
        carry_ref[...] = jnp.zeros(carry_ref.shape, F32)

    x1_parts = []
    for part in range(ROUTER_SPLIT):
        rows = slice(part * tm, (part + 1) * tm)
        a = a_ref[rows, :]
        if gated:
            gate = gate_ref[rows, :]
            a = gate * jax.nn.sigmoid(gate) * a
        x1_parts.append(x_ref[rows, :] + jnp.dot(a, w_ref[...], preferred_element_type=F32))
    for part in range(ROUTER_SPLIT):
        rows = slice(part * tm, (part + 1) * tm)
        _route_rows(x1_parts[part], rows, nf_ref, wr_ref, br_ref, x1_ref, hn_ref, route_ref, route_t_ref,
                    cnt_ref, carry_ref, n_groups=n_groups, n_experts=n_experts, epg=epg)


def _route_rows(x1, rows, nf_ref, wr_ref, br_ref, x1_ref, hn_ref, route_ref, route_t_ref, cnt_ref,
                carry_ref, *, n_groups, n_experts, epg):
    tm = x1.shape[0]
    x1_ref[rows, :] = x1
    hn = x1 * lax.rsqrt(jnp.mean(x1 * x1, axis=-1, keepdims=True) + RMS_EPS) * nf_ref[...]
    hn_ref[rows, :] = _pack_bf16_pairs(hn)

    hn_hi = hn.astype(BF16)
    hn_lo = (hn - hn_hi.astype(F32)).astype(BF16)
    logits = (jnp.dot(hn_hi, wr_ref[0], preferred_element_type=F32)
              + jnp.dot(hn_lo, wr_ref[0], preferred_element_type=F32)
              + jnp.dot(hn_hi, wr_ref[1], preferred_element_type=F32)) + br_ref[...]

    n_rows = -(-(n_groups + n_experts) // SUBLANES) * SUBLANES
    lt = logits.T[:n_rows, :]
    row = lax.broadcasted_iota(jnp.int32, lt.shape, 0)
    big = jnp.int32(LANES)
    neg = jnp.float32(-jnp.inf)

    gl = jnp.where(row < n_groups, lt, neg)
    gmax = jnp.max(gl, axis=0, keepdims=True)
    g_p = 1.0 / jnp.sum(jnp.exp(gl - gmax), axis=0, keepdims=True)
    g_idx = jnp.min(jnp.where(gl == gmax, row, big), axis=0, keepdims=True)

    lo = n_groups + g_idx * epg
    el = jnp.where((row >= lo) & (row < lo + epg), lt, neg)
    v1 = jnp.max(el, axis=0, keepdims=True)
    i1 = jnp.min(jnp.where(el == v1, row, big), axis=0, keepdims=True)
    el2 = jnp.where(row == i1, neg, el)
    v2 = jnp.max(el2, axis=0, keepdims=True)
    i2 = jnp.min(jnp.where(el2 == v2, row, big), axis=0, keepdims=True)
    tt = jnp.exp(v2 - v1)
    w1 = g_p / (1.0 + tt)
    w2 = g_p * tt / (1.0 + tt)
    e1 = i1 - n_groups
    e2 = i2 - n_groups

    erow = lax.broadcasted_iota(jnp.int32, (n_experts, tm), 0)
    oh1 = erow == e1
    oh2 = erow == e2
    onehot = jnp.where(oh1 | oh2, 1.0, 0.0)
    before = lax.broadcasted_iota(jnp.int32, (tm, tm), 0) < lax.broadcasted_iota(jnp.int32, (tm, tm), 1)
    tri = jnp.where(before, 1.0, 0.0).astype(BF16)
    carry = carry_ref[...]
    base = jnp.dot(onehot.astype(BF16), tri, preferred_element_type=F32) + carry[:, 0:1]
    rank1 = jnp.sum(jnp.where(oh1, base, 0.0), axis=0, keepdims=True)
    rank2 = jnp.sum(jnp.where(oh2, base, 0.0), axis=0, keepdims=True)
    carry = carry + jnp.sum(onehot, axis=1, keepdims=True)
    carry_ref[...] = carry
    cnt_ref[...] = carry

    frow = lax.broadcasted_iota(jnp.int32, (LANES, tm), 0)
    out = jnp.zeros((LANES, tm), F32)
    for idx, val in enumerate((e1.astype(F32), e2.astype(F32), w1, w2, rank1, rank2)):
        out = jnp.where(frow == idx, val, out)
    route_t_ref[:, rows] = out[:route_t_ref.shape[0], :]
    route_ref[rows, :] = out.T


def _out_router(a, gate, w, x, nf, w_group, b_group, w_expert, b_expert):
    t, d = x.shape
    k = w.shape[0]
    n_groups = w_group.shape[1]
    n_experts = w_expert.shape[1]
    assert n_groups + n_experts <= LANES
    pad = LANES - n_groups - n_experts
    wr = jnp.pad(jnp.concatenate([w_group, w_expert], axis=1).astype(F32), ((0, 0), (0, pad)))
    wr_hi = wr.astype(BF16)
    wr = jnp.stack([wr_hi, (wr - wr_hi.astype(F32)).astype(BF16)])
    br = jnp.pad(jnp.concatenate([b_group, b_expert]).astype(F32), (0, pad)).reshape(1, LANES)
    gated = gate is not None
    row = lambda i: (i, 0)
    fixed = lambda i: (0, 0)
    in_specs = [pl.BlockSpec((TM, k), row)]
    args = [a]
    if gated:
        gate_arr, gate_col = gate
        in_specs.append(pl.BlockSpec((TM, k), lambda i: (i, gate_col)))
        args.append(gate_arr)
    in_specs += [
        pl.BlockSpec((k, d), fixed),
        pl.BlockSpec((TM, d), row),
        pl.BlockSpec((1, d), fixed),
        pl.BlockSpec((2, d, LANES), lambda i: (0, 0, 0)),
        pl.BlockSpec((1, LANES), fixed),
    ]
    args += [w, x, nf.reshape(1, d), wr, br]
    return pl.pallas_call(
        functools.partial(_out_router_kernel, gated=gated, n_groups=n_groups, n_experts=n_experts),
        grid=(t // TM,),
        in_specs=in_specs,
        out_specs=[
            pl.BlockSpec((TM, d), row),
            pl.BlockSpec((TM, d // 2), row),
            pl.BlockSpec((TM, LANES), row),
            pl.BlockSpec((SUBLANES, TM), lambda i: (0, i)),
            pl.BlockSpec((n_experts, LANES), fixed),
        ],
        out_shape=[
            jax.ShapeDtypeStruct((t, d), F32),
            jax.ShapeDtypeStruct((t, d // 2), jnp.uint32),
            jax.ShapeDtypeStruct((t, LANES), F32),
            jax.ShapeDtypeStruct((SUBLANES, t), F32),
            jax.ShapeDtypeStruct((n_experts, LANES), F32),
        ],
        scratch_shapes=[pltpu.VMEM((n_experts, LANES), F32)],
        compiler_params=_cparams(("arbitrary",)),
    )(*args)


def _sc_mesh():
    return plsc.VectorSubcoreMesh(core_axis_name="c", subcore_axis_name="s",
                                  num_cores=SC_CORES, num_subcores=SC_SUBCORES)


def _sc_worker():
    return lax.axis_index("s") * SC_CORES + lax.axis_index("c")


def _sc_scatter_rows(src, d1, d2, n_rows):
    t, d = src.shape
    workers = SC_CORES * SC_SUBCORES
    per_w = t // workers
    n_ch = per_w // SC_CHUNK
    assert t % (workers * SC_CHUNK) == 0
    idx = jnp.stack([d1, d2]).reshape(2, workers, n_ch, SC_CHUNK)

    @functools.partial(
        pl.kernel, mesh=_sc_mesh(),
        out_type=jax.ShapeDtypeStruct((n_rows, d), src.dtype),
        scratch_types=[pltpu.VMEM((2, n_ch, SC_CHUNK), jnp.int32),
                       pltpu.VMEM((SC_CHUNK, d), src.dtype),
                       pltpu.VMEM((SC_CHUNK, d), src.dtype),
                       pltpu.SemaphoreType.DMA, pltpu.SemaphoreType.DMA],
    )
    def scatter(src_hbm, idx_hbm, out_hbm, idx_v, rows_a, rows_b, sem_a, sem_b):
        wid = _sc_worker()
        base = wid * per_w
        for s in range(2):
            pltpu.sync_copy(idx_hbm.at[s, wid], idx_v.at[s])

        def move(c, rows_v, sem):
            pltpu.sync_copy(src_hbm.at[pl.ds(base + c * SC_CHUNK, SC_CHUNK)], rows_v)
            return [pltpu.async_copy(rows_v, out_hbm.at[idx_v.at[s, c]], sem) for s in range(2)]

        @pl.loop(0, n_ch, step=2)
        def _(c):
            first = move(c, rows_a, sem_a)
            second = move(c + 1, rows_b, sem_b)
            for cp in first + second:
                cp.wait()

    return scatter(src, idx)


def _sc_gather_rows(table, idx):
    b = idx.shape[0]
    d = table.shape[1]
    workers = SC_CORES * SC_SUBCORES
    per_w = b // workers
    n_ch = per_w // SC_CHUNK
    assert b % (workers * SC_CHUNK * 2) == 0

    @functools.partial(
        pl.kernel, mesh=_sc_mesh(),
        out_type=jax.ShapeDtypeStruct((b, d), table.dtype),
        scratch_types=[pltpu.VMEM((per_w,), jnp.int32),
                       pltpu.VMEM((SC_CHUNK, d), table.dtype),
                       pltpu.VMEM((SC_CHUNK, d), table.dtype),
                       pltpu.SemaphoreType.DMA, pltpu.SemaphoreType.DMA],
    )
    def gather(table_hbm, idx_hbm, out_hbm, idx_v, rows_a, rows_b, sem_a, sem_b):
        base = _sc_worker() * per_w
        pltpu.sync_copy(idx_hbm.at[pl.ds(base, per_w)], idx_v)

        def fetch(c, rows_v, sem):
            return pltpu.async_copy(table_hbm.at[idx_v.at[pl.ds(c * SC_CHUNK, SC_CHUNK)]], rows_v, sem)

        def store(c, rows_v):
            pltpu.sync_copy(rows_v, out_hbm.at[pl.ds(base + c * SC_CHUNK, SC_CHUNK)])

        @pl.loop(0, n_ch, step=2)
        def _(c):
            first = fetch(c, rows_a, sem_a)
            second = fetch(c + 1, rows_b, sem_b)
            first.wait()
            store(c, rows_a)
            second.wait()
            store(c + 1, rows_b)

    return gather(table, idx)


def _expert_kernel(fb_ref, nblk_ref, nv_ref, nu_ref, xs_ref, wg_ref, wu_ref, wd_ref, ys_ref,
                   wgb_ref, wub_ref, wdb_ref, xbuf, ybuf, xsem, ysem):
    step = pl.program_id(0)
    rb = xbuf.shape[1]
    nb = ys_ref.shape[0] // rb
    n_used = nu_ref[0]

    def x_copy(g, slot):
        rows = pl.ds(pl.multiple_of(g * rb, rb), rb)
        return pltpu.make_async_copy(xs_ref.at[rows], xbuf.at[slot], xsem.at[slot])

    def y_copy(g, slot):
        rows = pl.ds(pl.multiple_of(g * rb, rb), rb)
        return pltpu.make_async_copy(ybuf.at[slot], ys_ref.at[rows], ysem.at[slot])

    @pl.when(step == 0)
    def _():
        x_copy(0, 0).start()

    def block(b, first):
        g = first + b
        slot = lax.rem(g, 2)
        x_copy(g, slot).wait()

        @pl.when(g + 1 < n_used)
        def _():
            x_copy(g + 1, 1 - slot).start()

        @pl.when(g >= 2)
        def _():
            y_copy(g - 2, slot).wait()

        nv = nv_ref[g]

        def ffn(rows):
            valid = lax.broadcasted_iota(jnp.int32, (rows, 1), 0) < nv
            x_hi, x_lo = _unpack_bf16_pairs(jnp.where(valid, xbuf[slot, :rows, :], jnp.uint32(0)))
            x_hi, x_lo = x_hi.astype(BF16), x_lo.astype(BF16)
            h = x_hi.shape[1]

            def in_dot(w_ref):
                return (jnp.dot(x_hi, w_ref[:h, :], preferred_element_type=F32)
                        + jnp.dot(x_lo, w_ref[h:, :], preferred_element_type=F32))

            gate = in_dot(wgb_ref)
            up = in_dot(wub_ref)
            hmid = (gate * jax.nn.sigmoid(gate) * up).astype(BF16)
            ybuf[slot, :rows, :] = _pack_bf16_pairs(jnp.dot(hmid, wdb_ref[...], preferred_element_type=F32))
            if rows < rb:
                ybuf[slot, rows:, :] = jnp.zeros((rb - rows, ybuf.shape[2]), ybuf.dtype)

        @pl.when(nv > rb // 2)
        def _():
            ffn(rb)

        @pl.when(nv <= rb // 2)
        def _():
            ffn(rb // 2)

        y_copy(g, slot).start()
        return first

    for sub in range(EXPERTS_PER_STEP):
        e = step * EXPERTS_PER_STEP + sub

        @pl.when(nblk_ref[e] > 0)
        def _(sub=sub):
            wgb_ref[...] = wg_ref[0, sub].astype(BF16)
            wub_ref[...] = wu_ref[0, sub].astype(BF16)
            wdb_ref[...] = wd_ref[0, sub].astype(BF16)

        lax.fori_loop(0, nblk_ref[e], block, fb_ref[e])

    @pl.when(step == pl.num_programs(0) - 1)
    def _():
        @pl.when(n_used >= 2)
        def _():
            y_copy(n_used - 2, lax.rem(n_used, 2)).wait()

        y_copy(n_used - 1, lax.rem(n_used - 1, 2)).wait()
        ybuf[0] = jnp.zeros(ybuf.shape[1:], ybuf.dtype)

        def tail_start(g, carry):
            y_copy(g, 0).start()
            return carry

        def tail_wait(g, carry):
            y_copy(g, 0).wait()
            return carry

        lax.fori_loop(n_used, nb, tail_start, 0)
        lax.fori_loop(n_used, nb, tail_wait, 0)


def _experts(xs, first_block, n_blocks, n_valid, n_used, w_gate, w_up, w_down, layer):
    n_rows, dp = xs.shape
    d = 2 * dp
    n_experts, ff = w_gate.shape[1], w_gate.shape[3]
    eps = EXPERTS_PER_STEP
    assert n_experts % eps == 0
    wsel = lambda s, *_: (layer, s, 0, 0)
    grid_spec = pltpu.PrefetchScalarGridSpec(
        num_scalar_prefetch=4,
        grid=(n_experts // eps,),
        in_specs=[
            pl.BlockSpec(memory_space=pl.ANY),
            pl.BlockSpec((1, eps, d, ff), wsel),
            pl.BlockSpec((1, eps, d, ff), wsel),
            pl.BlockSpec((1, eps, ff, d), wsel),
        ],
        out_specs=pl.BlockSpec(memory_space=pl.ANY),
        scratch_shapes=[pltpu.VMEM((d, ff), BF16), pltpu.VMEM((d, ff), BF16), pltpu.VMEM((ff, d), BF16),
                        pltpu.VMEM((2, ROW_BLOCK, dp), xs.dtype), pltpu.VMEM((2, ROW_BLOCK, dp), xs.dtype),
                        pltpu.SemaphoreType.DMA((2,)), pltpu.SemaphoreType.DMA((2,))],
    )
    return pl.pallas_call(
        _expert_kernel,
        grid_spec=grid_spec,
        out_shape=jax.ShapeDtypeStruct((n_rows, dp), xs.dtype),
        compiler_params=_cparams(("arbitrary",)),
    )(first_block, n_blocks, n_valid, n_used, xs, w_gate, w_up, w_down)


def _final_combine_kernel(x_ref, route_ref, y1_ref, y2_ref, nf_ref, o_ref):
    route = route_ref[...]
    y1 = jnp.concatenate(_unpack_bf16_pairs(y1_ref[...]), axis=1)
    y2 = jnp.concatenate(_unpack_bf16_pairs(y2_ref[...]), axis=1)
    y = x_ref[...] + (route[:, 2:3] * y1 + route[:, 3:4] * y2)
    o_ref[...] = y * lax.rsqrt(jnp.mean(y * y, axis=-1, keepdims=True) + RMS_EPS) * nf_ref[...]


def _final_combine(x1, route, yg, nf):
    t, d = x1.shape
    slot2 = t // TM
    row = lambda i: (i, 0)
    return pl.pallas_call(
        _final_combine_kernel,
        grid=(t // TM,),
        in_specs=[
            pl.BlockSpec((TM, d), row),
            pl.BlockSpec((TM, LANES), row),
            pl.BlockSpec((TM, d // 2), row),
            pl.BlockSpec((TM, d // 2), lambda i: (slot2 + i, 0)),
            pl.BlockSpec((1, d), lambda i: (0, 0)),
        ],
        out_specs=pl.BlockSpec((TM, d), row),
        out_shape=jax.ShapeDtypeStruct((t, d), F32),
        compiler_params=_cparams(("parallel",)),
    )(x1, route, yg, yg, nf.reshape(1, d).astype(F32))


def _moe(hn, route_t, cnt, w_gate, w_up, w_down, layer):
    t, d = hn.shape
    n_experts = w_gate.shape[1]
    ids = jnp.arange(n_experts, dtype=jnp.int32)
    e1, e2, rank1, rank2 = (route_t[f].astype(jnp.int32) for f in (0, 1, 4, 5))
    counts = cnt[:, 0].astype(jnp.int32)
    padded = ((counts + ROW_BLOCK - 1) // ROW_BLOCK) * ROW_BLOCK
    pad_end = jnp.cumsum(padded)
    pad_start = pad_end - padded

    def dest(e, rank):
        return jnp.sum(jnp.where(e[:, None] == ids[None, :], pad_start[None, :], 0), axis=1) + rank

    d1 = dest(e1, rank1)
    d2 = dest(e2, rank2)
    nb = (t * TOP_K_INNER) // ROW_BLOCK + n_experts
    n_used = (pad_end[-1:] // ROW_BLOCK).astype(jnp.int32)
    first_row = jnp.arange(nb, dtype=jnp.int32) * ROW_BLOCK
    block_e = jnp.sum((pad_end[None, :] <= first_row[:, None]).astype(jnp.int32), axis=1)
    block_e = jnp.minimum(block_e, n_experts - 1)
    row_end = jnp.sum(jnp.where(block_e[:, None] == ids[None, :], (pad_start + counts)[None, :], 0), axis=1)
    n_valid = jnp.clip(row_end - first_row, 0, ROW_BLOCK).astype(jnp.int32)
    xs = _sc_scatter_rows(hn, d1, d2, nb * ROW_BLOCK)
    ys = _experts(xs, (pad_start // ROW_BLOCK).astype(jnp.int32), (padded // ROW_BLOCK).astype(jnp.int32),
                  n_valid, n_used, w_gate, w_up, w_down, layer)
    return _sc_gather_rows(ys, jnp.concatenate([d1, d2]))


def _diff_lambda_init(layer_idx):
    return 0.8 - 0.6 * math.exp(-0.3 * layer_idx)


def _rope_tables(seq, dk):
    angle = 1.0 / (10000.0 ** jnp.linspace(0.0, 1.0, dk // 2, dtype=F32))
    angle = jnp.repeat(angle, 2)
    phase = jnp.arange(seq, dtype=F32)[:, None] * angle[None, :]
    sin, cos = jnp.sin(phase), jnp.cos(phase)
    even = (jnp.arange(dk) % 2 == 0)[None, :]
    return cos, jnp.where(even, -sin, 0.0), jnp.where(even, 0.0, sin)


def kernel(x, rel_bias_table, norm_mix, norm_ffn, norm_final, diff_w_in, diff_lambda_qk, diff_subln,
           diff_w_out, ret_w_in, ret_w_out, moe_w_group, moe_b_group, moe_w_expert, moe_b_expert,
           moe_w_gate, moe_w_up, moe_w_down):
    batch, seq, d = x.shape
    depth = norm_mix.shape[0]
    t = batch * seq
    heads = rel_bias_table.shape[1]
    dh = diff_lambda_qk.shape[-1]
    qk_dim = (ret_w_in.shape[2] - 2 * ret_w_out.shape[1]) // 2
    v_dim = ret_w_out.shape[1]
    dk = qk_dim // RET_HEADS
    xt = x.reshape(t, d)
    moe = None
    for i in range(depth):
        j = i // N_MIXERS
        if i % N_MIXERS == 0:
            col_scale = jnp.concatenate([jnp.full((d,), dh ** -0.5 * LOG2E, F32), jnp.ones((2 * d,), F32)])
            qkv = _norm_matmul(xt, norm_mix[i], diff_w_in, j, col_scale, moe=moe)
            if moe is not None:
                qkv, xt = qkv
            mix = _diff_attention(qkv, rel_bias_table, diff_lambda_qk[j], diff_subln[j],
                                  _diff_lambda_init(i), batch, seq, heads)
            gate = None
            w_out = diff_w_out[j].astype(BF16)
        else:
            n_in = ret_w_in.shape[2]
            col_scale = jnp.concatenate([jnp.ones((qk_dim,), F32), jnp.full((qk_dim,), dk ** -0.5, F32),
                                         jnp.ones((n_in - 2 * qk_dim,), F32)])
            qkvg = _norm_matmul(xt, norm_mix[i], ret_w_in, j, col_scale, rope=_rope_tables(seq, dk),
                                n_rope=2 * qk_dim // TN, seq=seq, moe=moe)
            if moe is not None:
                qkvg, xt = qkvg
            mix = _retention(qkvg, batch, seq, qk_dim, v_dim)
            gate = (qkvg, (2 * qk_dim + v_dim) // v_dim)
            w_out = ret_w_out[j].astype(BF16)
        x1, hn, route, route_t, cnt = _out_router(mix, gate, w_out, xt, norm_ffn[i], moe_w_group[i],
                                                  moe_b_group[i], moe_w_expert[i], moe_b_expert[i])
        xt = x1
        moe = (route, _moe(hn, route_t, cnt, moe_w_gate, moe_w_up, moe_w_down, i))
    return _final_combine(xt, moe[0], moe[1], norm_final).reshape(batch, seq, d)
```

```python
import functools
import math

import jax
import jax.numpy as jnp
from jax import lax
from jax.experimental import pallas as pl
from jax.experimental.pallas import tpu as pltpu
from jax.experimental.pallas import tpu_sc as plsc

F32 = jnp.float32
BF16 = jnp.bfloat16

N_MIXERS = 2
NUM_BUCKETS = 32
MAX_DISTANCE = 128
RET_HEADS = 4
TOP_K_INNER = 2
RMS_EPS = 1e-6
SUBLN_EPS = 1e-5

LANES = 128
SUBLANES = 8
BF16_SUBLANES = 16
LOG2E = 1.4426950408889634
VMEM_LIMIT = 56 * 1024 * 1024
SC_CORES = 2
SC_SUBCORES = 16

TM = 512
TN = 1024
TQ = 256
ATTN_HEADS = 4
RET_C = 256
RET_HP = 2
ROUTER_SPLIT = 2
ROW_BLOCK = 512
SC_CHUNK = 64


def _cparams(sem):
    return pltpu.CompilerParams(dimension_semantics=sem, vmem_limit_bytes=VMEM_LIMIT)


def _pack_bf16_pairs(x):
    h = x.shape[1] // 2
    hi = lax.bitcast_convert_type(x[:, :h].astype(BF16).astype(F32), jnp.uint32)
    lo = lax.bitcast_convert_type(x[:, h:].astype(BF16).astype(F32), jnp.uint32)
    return hi | (lo >> 16)


def _unpack_bf16_pairs(w):
    hi = lax.bitcast_convert_type(w & jnp.uint32(0xFFFF0000), F32)
    lo = lax.bitcast_convert_type(w << 16, F32)
    return hi, lo


def _norm_matmul_kernel(x_ref, *rest, n_rope, moe, layer):
    rest = list(rest)
    w_ref, stage_ref, sem = rest[-3:]
    rest = rest[:-3]
    if moe:
        route_ref, y1_ref, y2_ref = rest[:3]
        x_out_ref = rest.pop()
        rest = rest[3:]
    g_ref, scale_ref, w_hbm = rest[:3]
    if n_rope:
        cos_ref, sa_ref, sb_ref, o_ref = rest[3:]
    else:
        (o_ref,) = rest[3:]
    n_chunks = o_ref.shape[1] // TN

    @pl.when(pl.program_id(0) == 0)
    def _():
        def fetch(j):
            return pltpu.make_async_copy(w_hbm.at[layer, :, pl.ds(j * TN, TN)], stage_ref.at[j % 2],
                                         sem.at[j % 2])

        fetch(0).start()
        for j in range(n_chunks):
            if j + 1 < n_chunks:
                fetch(j + 1).start()
            fetch(j).wait()
            cols = slice(j * TN, (j + 1) * TN)
            w_ref[:, cols] = (stage_ref[j % 2] * scale_ref[:, cols]).astype(BF16)

    x = x_ref[...]
    if moe:
        route = route_ref[...]
        y1 = jnp.concatenate(_unpack_bf16_pairs(y1_ref[...]), axis=1)
        y2 = jnp.concatenate(_unpack_bf16_pairs(y2_ref[...]), axis=1)
        x = x + (route[:, 2:3] * y1 + route[:, 3:4] * y2)
        x_out_ref[...] = x
    xn = (x * lax.rsqrt(jnp.mean(x * x, axis=-1, keepdims=True) + RMS_EPS) * g_ref[...]).astype(BF16)
    for j in range(n_chunks):
        cols = slice(j * TN, (j + 1) * TN)
        acc = jnp.dot(xn, w_ref[:, cols], preferred_element_type=F32)
        if j < n_rope:
            reps = TN // cos_ref.shape[1]
            c = jnp.tile(cos_ref[...], (1, reps))
            sa = jnp.tile(sa_ref[...], (1, reps))
            sb = jnp.tile(sb_ref[...], (1, reps))
            nxt = pltpu.roll(acc, TN - 1, axis=1)
            prv = pltpu.roll(acc, 1, axis=1)
            acc = acc * c + nxt * sa + prv * sb
        o_ref[:, cols] = acc.astype(o_ref.dtype)


def _norm_matmul(x, g, w, layer, col_scale, rope=None, n_rope=0, seq=None, moe=None):
    t, d = x.shape
    n = w.shape[2]
    row = lambda i: (i, 0)
    in_specs = [pl.BlockSpec((TM, d), row)]
    args = [x]
    if moe is not None:
        route, yg = moe
        slot2 = t // TM
        in_specs += [pl.BlockSpec((TM, LANES), row), pl.BlockSpec((TM, d // 2), row),
                     pl.BlockSpec((TM, d // 2), lambda i: (slot2 + i, 0))]
        args += [route, yg, yg]
    in_specs += [
        pl.BlockSpec((1, d), lambda i: (0, 0)),
        pl.BlockSpec((1, n), lambda i: (0, 0)),
        pl.BlockSpec(memory_space=pl.ANY),
    ]
    args += [g.reshape(1, d), col_scale.reshape(1, n).astype(F32), w]
    if rope is not None:
        nb = seq // TM
        dk = rope[0].shape[1]
        in_specs += [pl.BlockSpec((TM, dk), lambda i: (i % nb, 0))] * 3
        args += list(rope)
    out_specs = [pl.BlockSpec((TM, n), row)]
    out_shape = [jax.ShapeDtypeStruct((t, n), BF16)]
    if moe is not None:
        out_specs.append(pl.BlockSpec((TM, d), row))
        out_shape.append(jax.ShapeDtypeStruct((t, d), F32))
    out = pl.pallas_call(
        functools.partial(_norm_matmul_kernel, n_rope=n_rope, moe=moe is not None, layer=layer),
        grid=(t // TM,),
        in_specs=in_specs,
        out_specs=out_specs,
        out_shape=out_shape,
        scratch_shapes=[pltpu.VMEM((d, n), BF16), pltpu.VMEM((2, d, TN), F32), pltpu.SemaphoreType.DMA((2,))],
        compiler_params=_cparams(("arbitrary",)),
    )(*args)
    return out if moe is not None else out[0]


def _diff_attn_kernel(q_ref, k_ref, v_ref, bias_ref, lam_ref, g_ref, o_ref,
                      vt_ref, m_ref, acc_ref, s_ref, *, lambda_init):
    seq = q_ref.shape[0]
    hp, _, _, tq = bias_ref.shape
    dv = q_ref.shape[1] // hp
    dh = dv // 2
    nq = seq // tq

    def head(hh):
        return slice(hh * dv, (hh + 1) * dv)

    extra = vt_ref.shape[2] - dv
    ones_row = jnp.where(lax.broadcasted_iota(jnp.int32, (extra, tq), 0) == 0, 1.0, 0.0).astype(BF16)
    for hh in range(hp):
        for j in range(nq):
            vt_ref[hh, j, :dv, :] = v_ref[j * tq:(j + 1) * tq, head(hh)].astype(F32).T.astype(BF16)
            vt_ref[hh, j, dv:, :] = ones_row

    lq = lam_ref[...]
    lam = (jnp.exp(jnp.sum(lq[0:1] * lq[1:2], axis=-1, keepdims=True))
           - jnp.exp(jnp.sum(lq[2:3] * lq[3:4], axis=-1, keepdims=True)) + lambda_init)
    lane = lax.broadcasted_iota(jnp.int32, (tq, dv), 1)

    def query_maps(qi):
        qrows = pl.ds(pl.multiple_of(qi * tq, tq), tq)
        qs = []
        for hh in range(hp):
            q = q_ref[qrows, head(hh)]
            zero = jnp.zeros_like(q)
            qs.append(jnp.concatenate([jnp.where(lane < dh, q, zero), jnp.where(lane >= dh, q, zero)],
                                      axis=0))
        return qs

    def score_dots(j, qs):
        rows = pl.ds(pl.multiple_of(j * tq, tq), tq)
        for hh in range(hp):
            s_ref[hh] = lax.dot_general(k_ref[rows, head(hh)], qs[hh], (((1,), (1,)), ((), ())),
                                        preferred_element_type=F32)

    score_dots(0, query_maps(0))

    def q_body(qi, carry):
        qrows = pl.ds(pl.multiple_of(qi * tq, tq), tq)
        qs = query_maps(qi)
        qs_next = query_maps(jnp.minimum(qi + 1, nq - 1))

        m_ref[...] = jnp.full(m_ref.shape, -jnp.inf, F32)
        acc_ref[...] = jnp.zeros(acc_ref.shape, F32)

        def step(j, tile, prefetch):
            probs, alphas = [], []
            for hh in range(hp):
                s = s_ref[hh]
                if tile is not None:
                    bias = bias_ref[hh, tile]
                    s = s + jnp.concatenate([bias, bias], axis=1)
                m_old = m_ref[hh]
                m_new = jnp.maximum(m_old, jnp.max(s, axis=0, keepdims=True))
                alpha = jnp.exp2(m_old - m_new)
                m_ref[hh] = m_new
                probs.append(jnp.exp2(s - m_new).astype(BF16))
                alphas.append(alpha)
            if prefetch == "block":
                score_dots(j + 1, qs)
            else:
                score_dots(0, qs_next)
            for hh in range(hp):
                acc_ref[hh] = alphas[hh] * acc_ref[hh] + jnp.dot(vt_ref[hh, j], probs[hh],
                                                                 preferred_element_type=F32)

        def far_body(j, c):
            step(j, None, "block")
            return c

        lax.fori_loop(0, jnp.maximum(qi - 1, 0), far_body, 0)

        @pl.when(qi >= 1)
        def _():
            step(qi - 1, 1, "block")

        step(qi, 0, "tile")

        for hh in range(hp):
            acc = acc_ref[hh]
            on = acc[:dv] / acc[dv:dv + 1]
            ot = on[:, :tq] - lam * on[:, tq:]
            ot = ot * lax.rsqrt(jnp.mean(ot * ot, axis=0, keepdims=True) + SUBLN_EPS)
            o_ref[qrows, head(hh)] = (ot.T * g_ref[...] * (1.0 - lambda_init)).astype(o_ref.dtype)
        return carry

    lax.fori_loop(0, nq, q_body, 0)


def _t5_bucket(rel):
    max_exact = NUM_BUCKETS // 2
    n = jnp.maximum(rel, 0)
    is_small = n < max_exact
    nf = jnp.maximum(n, max_exact).astype(F32)
    large = max_exact + (jnp.log(nf / max_exact) / math.log(MAX_DISTANCE / max_exact)
                         * (NUM_BUCKETS - max_exact)).astype(jnp.int32)
    large = jnp.minimum(large, NUM_BUCKETS - 1)
    return jnp.where(is_small, n, large)


def _bias_tiles(rel_table):
    heads = rel_table.shape[1]
    span = 2 * TQ + 1
    bucket = _t5_bucket(jnp.arange(span))
    onehot = (bucket[:, None] == jnp.arange(NUM_BUCKETS)[None, :]).astype(F32)
    table = (rel_table.astype(F32) - rel_table[NUM_BUCKETS - 1].astype(F32)[None, :]) * LOG2E
    by_rel = jnp.dot(onehot, table, precision=lax.Precision.HIGHEST).T
    toep = jnp.tile(by_rel, (1, TQ))[:, :TQ * (span - 1)].reshape(heads, TQ, span - 1)
    kj = jnp.arange(TQ)[:, None]
    qi = jnp.arange(TQ)[None, :]
    diag = jnp.where((qi >= kj)[None], toep[:, :, :TQ], -jnp.inf)
    return jnp.stack([diag, toep[:, :, TQ:]], axis=1)


def _diff_attention(qkv, rel_table, lam_qk, subln_g, lambda_init, batch, seq, heads):
    t = qkv.shape[0]
    d = qkv.shape[1] // 3
    dv = d // heads
    hp = ATTN_HEADS
    assert TQ + 1 >= MAX_DISTANCE and seq % TQ == 0 and dv == LANES and heads % hp == 0
    nq = seq // TQ
    groups = heads // hp
    bias = _bias_tiles(rel_table)
    return pl.pallas_call(
        functools.partial(_diff_attn_kernel, lambda_init=lambda_init),
        grid=(batch, groups),
        in_specs=[
            pl.BlockSpec((seq, hp * dv), lambda b, g: (b, g)),
            pl.BlockSpec((seq, hp * dv), lambda b, g: (b, groups + g)),
            pl.BlockSpec((seq, hp * dv), lambda b, g: (b, 2 * groups + g)),
            pl.BlockSpec((hp, 2, TQ, TQ), lambda b, g: (g, 0, 0, 0)),
            pl.BlockSpec(lam_qk.shape, lambda b, g: (0, 0)),
            pl.BlockSpec((1, dv), lambda b, g: (0, 0)),
        ],
        out_specs=pl.BlockSpec((seq, hp * dv), lambda b, g: (b, g)),
        out_shape=jax.ShapeDtypeStruct((t, d), BF16),
        scratch_shapes=[
            pltpu.VMEM((hp, nq, dv + BF16_SUBLANES, TQ), BF16),
            pltpu.VMEM((hp, 1, 2 * TQ), F32),
            pltpu.VMEM((hp, dv + BF16_SUBLANES, 2 * TQ), F32),
            pltpu.VMEM((hp, TQ, 2 * TQ), F32),
        ],
        compiler_params=_cparams(("parallel", "parallel")),
    )(qkv, qkv, qkv, bias, lam_qk.astype(F32), subln_g.reshape(1, dv).astype(F32))


def _retention_kernel(q_ref, k_ref, v_ref, dec_ref, tab_ref, o_ref, state_ref):
    seq = q_ref.shape[0]
    hp, c, _ = dec_ref.shape
    dk = q_ref.shape[1] // hp
    dv = v_ref.shape[1] // hp
    heads = range(hp)
    state_ref[...] = jnp.zeros(state_ref.shape, F32)

    def body(n, carry):
        rows = pl.ds(pl.multiple_of(n * c, c), c)
        q = [q_ref[rows, hh * dk:(hh + 1) * dk] for hh in heads]
        k = [k_ref[rows, hh * dk:(hh + 1) * dk] for hh in heads]
        v = [v_ref[rows, hh * dv:(hh + 1) * dv] for hh in heads]
        tab = [tab_ref[hh] for hh in heads]
        st = [state_ref[hh] for hh in heads]
        s = [lax.dot_general(q[hh], k[hh], (((1,), (1,)), ((), ())), preferred_element_type=F32)
             for hh in heads]
        cross = [jnp.dot(q[hh], st[hh].astype(BF16), preferred_element_type=F32) for hh in heads]
        kt = [(k[hh].astype(F32) * tab[hh][:, 1:2]).T.astype(BF16) for hh in heads]
        inner = [jnp.dot((s[hh] * dec_ref[hh]).astype(BF16), v[hh], preferred_element_type=F32)
                 for hh in heads]
        for hh in heads:
            state_ref[hh] = st[hh] * tab[hh][0:1, 2:3] + jnp.dot(kt[hh], v[hh], preferred_element_type=F32)
        for hh in heads:
            o = inner[hh] + cross[hh] * tab[hh][:, 0:1]
            o = o * lax.rsqrt(jnp.mean(o * o, axis=-1, keepdims=True) + RMS_EPS)
            o_ref[rows, hh * dv:(hh + 1) * dv] = o.astype(o_ref.dtype)
        return carry

    lax.fori_loop(0, seq // c, body, 0)


def _retention_tables(heads):
    c = RET_C
    log_gamma = jnp.log(1.0 - 2.0 ** (-5.0 - jnp.arange(heads, dtype=F32)))
    pos = jnp.arange(c, dtype=F32)
    dist = pos[:, None] - pos[None, :]
    inner_decay = jnp.where(dist[None] >= 0,
                            jnp.exp(jnp.maximum(dist, 0.0)[None] * log_gamma[:, None, None]), 0.0)
    q_decay = jnp.exp((pos + 1.0)[None] * log_gamma[:, None])
    k_decay = jnp.exp((c - 1.0 - pos)[None] * log_gamma[:, None])
    chunk_decay = jnp.broadcast_to(jnp.exp(c * log_gamma)[:, None], (heads, c))
    tab = jnp.stack([q_decay, k_decay, chunk_decay], axis=-1)
    tab = jnp.pad(tab, ((0, 0), (0, 0), (0, LANES - 3)))
    return inner_decay, tab


def _retention(qkvg, batch, seq, qk_dim, v_dim):
    t = qkvg.shape[0]
    dk = qk_dim // RET_HEADS
    dvh = v_dim // RET_HEADS
    inner_decay, tab = _retention_tables(RET_HEADS)
    hp = RET_HP
    assert RET_HEADS % hp == 0
    groups = RET_HEADS // hp
    kb = qk_dim // (hp * dk)
    vb = 2 * qk_dim // (hp * dvh)
    return pl.pallas_call(
        _retention_kernel,
        grid=(batch, groups),
        in_specs=[
            pl.BlockSpec((seq, hp * dk), lambda b, g: (b, g)),
            pl.BlockSpec((seq, hp * dk), lambda b, g: (b, kb + g)),
            pl.BlockSpec((seq, hp * dvh), lambda b, g: (b, vb + g)),
            pl.BlockSpec((hp, RET_C, RET_C), lambda b, g: (g, 0, 0)),
            pl.BlockSpec((hp, RET_C, LANES), lambda b, g: (g, 0, 0)),
        ],
        out_specs=pl.BlockSpec((seq, hp * dvh), lambda b, g: (b, g)),
        out_shape=jax.ShapeDtypeStruct((t, v_dim), BF16),
        scratch_shapes=[pltpu.VMEM((hp, dk, dvh), F32)],
        compiler_params=_cparams(("parallel", "parallel")),
    )(qkvg, qkvg, qkvg, inner_decay, tab)


def _out_router_kernel(*refs, gated, n_groups, n_experts):
    if gated:
        a_ref, gate_ref, w_ref, x_ref, nf_ref, wr_ref, br_ref = refs[:7]
        x1_ref, hn_ref, route_ref, route_t_ref, cnt_ref, carry_ref = refs[7:]
    else:
        a_ref, w_ref, x_ref, nf_ref, wr_ref, br_ref = refs[:6]
        x1_ref, hn_ref, route_ref, route_t_ref, cnt_ref, carry_ref = refs[6:]
    epg = n_experts // n_groups
    tm = x_ref.shape[0] // ROUTER_SPLIT

    @pl.when(pl.program_id(0) == 0)
    def _():
        carry_ref[...] = jnp.zeros(carry_ref.shape, F32)

    x1_parts = []
    for part in range(ROUTER_SPLIT):
        rows = slice(part * tm, (part + 1) * tm)
        a = a_ref[rows, :]
        if gated:
            gate = gate_ref[rows, :]
            a = gate * jax.nn.sigmoid(gate) * a
        x1_parts.append(x_ref[rows, :] + jnp.dot(a, w_ref[...], preferred_element_type=F32))
    for part in range(ROUTER_SPLIT):
        rows = slice(part * tm, (part + 1) * tm)
        _route_rows(x1_parts[part], rows, nf_ref, wr_ref, br_ref, x1_ref, hn_ref, route_ref, route_t_ref,
                    cnt_ref, carry_ref, n_groups=n_groups, n_experts=n_experts, epg=epg)


def _route_rows(x1, rows, nf_ref, wr_ref, br_ref, x1_ref, hn_ref, route_ref, route_t_ref, cnt_ref,
                carry_ref, *, n_groups, n_experts, epg):
    tm = x1.shape[0]
    x1_ref[rows, :] = x1
    hn = x1 * lax.rsqrt(jnp.mean(x1 * x1, axis=-1, keepdims=True) + RMS_EPS) * nf_ref[...]
    hn_ref[rows, :] = _pack_bf16_pairs(hn)

    hn_hi = hn.astype(BF16)
    hn_lo = (hn - hn_hi.astype(F32)).astype(BF16)
    both = jnp.dot(hn_hi, wr_ref[...], preferred_element_type=F32)
    logits = (both[:, :LANES] + both[:, LANES:]
              + jnp.dot(hn_lo, wr_ref[:, :LANES], preferred_element_type=F32)) + br_ref[...]

    n_rows = -(-(n_groups + n_experts) // SUBLANES) * SUBLANES
    lt = logits.T[:n_rows, :]
    row = lax.broadcasted_iota(jnp.int32, lt.shape, 0)
    big = jnp.int32(LANES)
    neg = jnp.float32(-jnp.inf)

    gl = jnp.where(row < n_groups, lt, neg)
    gmax = jnp.max(gl, axis=0, keepdims=True)
    g_p = 1.0 / jnp.sum(jnp.exp(gl - gmax), axis=0, keepdims=True)
    g_idx = jnp.min(jnp.where(gl == gmax, row, big), axis=0, keepdims=True)

    lo = n_groups + g_idx * epg
    el = jnp.where((row >= lo) & (row < lo + epg), lt, neg)
    v1 = jnp.max(el, axis=0, keepdims=True)
    i1 = jnp.min(jnp.where(el == v1, row, big), axis=0, keepdims=True)
    el2 = jnp.where(row == i1, neg, el)
    v2 = jnp.max(el2, axis=0, keepdims=True)
    i2 = jnp.min(jnp.where(el2 == v2, row, big), axis=0, keepdims=True)
    tt = jnp.exp(v2 - v1)
    w1 = g_p / (1.0 + tt)
    w2 = g_p * tt / (1.0 + tt)
    e1 = i1 - n_groups
    e2 = i2 - n_groups

    erow = lax.broadcasted_iota(jnp.int32, (n_experts, tm), 0)
    oh1 = erow == e1
    oh2 = erow == e2
    onehot = jnp.where(oh1 | oh2, 1.0, 0.0)
    before = lax.broadcasted_iota(jnp.int32, (tm, tm), 0) < lax.broadcasted_iota(jnp.int32, (tm, tm), 1)
    tri = jnp.where(before, 1.0, 0.0).astype(BF16)
    carry = carry_ref[...]
    base = jnp.dot(onehot.astype(BF16), tri, preferred_element_type=F32) + carry[:, 0:1]
    rank1 = jnp.sum(jnp.where(oh1, base, 0.0), axis=0, keepdims=True)
    rank2 = jnp.sum(jnp.where(oh2, base, 0.0), axis=0, keepdims=True)
    carry = carry + jnp.sum(onehot, axis=1, keepdims=True)
    carry_ref[...] = carry
    cnt_ref[...] = carry

    frow = lax.broadcasted_iota(jnp.int32, (LANES, tm), 0)
    out = jnp.zeros((LANES, tm), F32)
    for idx, val in enumerate((e1.astype(F32), e2.astype(F32), w1, w2, rank1, rank2)):
        out = jnp.where(frow == idx, val, out)
    route_t_ref[:, rows] = out[:route_t_ref.shape[0], :]
    route_ref[rows, :] = out.T


def _out_router(a, gate, w, x, nf, w_group, b_group, w_expert, b_expert):
    t, d = x.shape
    k = w.shape[0]
    n_groups = w_group.shape[1]
    n_experts = w_expert.shape[1]
    assert n_groups + n_experts <= LANES
    pad = LANES - n_groups - n_experts
    wr = jnp.pad(jnp.concatenate([w_group, w_expert], axis=1).astype(F32), ((0, 0), (0, pad)))
    wr_hi = wr.astype(BF16)
    wr = jnp.concatenate([wr_hi, (wr - wr_hi.astype(F32)).astype(BF16)], axis=1)
    br = jnp.pad(jnp.concatenate([b_group, b_expert]).astype(F32), (0, pad)).reshape(1, LANES)
    gated = gate is not None
    row = lambda i: (i, 0)
    fixed = lambda i: (0, 0)
    in_specs = [pl.BlockSpec((TM, k), row)]
    args = [a]
    if gated:
        gate_arr, gate_col = gate
        in_specs.append(pl.BlockSpec((TM, k), lambda i: (i, gate_col)))
        args.append(gate_arr)
    in_specs += [
        pl.BlockSpec((k, d), fixed),
        pl.BlockSpec((TM, d), row),
        pl.BlockSpec((1, d), fixed),
        pl.BlockSpec((d, 2 * LANES), fixed),
        pl.BlockSpec((1, LANES), fixed),
    ]
    args += [w, x, nf.reshape(1, d), wr, br]
    return pl.pallas_call(
        functools.partial(_out_router_kernel, gated=gated, n_groups=n_groups, n_experts=n_experts),
        grid=(t // TM,),
        in_specs=in_specs,
        out_specs=[
            pl.BlockSpec((TM, d), row),
            pl.BlockSpec((TM, d // 2), row),
            pl.BlockSpec((TM, LANES), row),
            pl.BlockSpec((SUBLANES, TM), lambda i: (0, i)),
            pl.BlockSpec((n_experts, LANES), fixed),
        ],
        out_shape=[
            jax.ShapeDtypeStruct((t, d), F32),
            jax.ShapeDtypeStruct((t, d // 2), jnp.uint32),
            jax.ShapeDtypeStruct((t, LANES), F32),
            jax.ShapeDtypeStruct((SUBLANES, t), F32),
            jax.ShapeDtypeStruct((n_experts, LANES), F32),
        ],
        scratch_shapes=[pltpu.VMEM((n_experts, LANES), F32)],
        compiler_params=_cparams(("arbitrary",)),
    )(*args)


def _sc_mesh():
    return plsc.VectorSubcoreMesh(core_axis_name="c", subcore_axis_name="s",
                                  num_cores=SC_CORES, num_subcores=SC_SUBCORES)


def _sc_worker():
    return lax.axis_index("s") * SC_CORES + lax.axis_index("c")


def _sc_scatter_rows(src, d1, d2, n_rows):
    t, d = src.shape
    workers = SC_CORES * SC_SUBCORES
    per_w = t // workers
    n_ch = per_w // SC_CHUNK
    assert t % (workers * SC_CHUNK) == 0
    idx = jnp.stack([d1, d2]).reshape(2, workers, n_ch, SC_CHUNK)

    @functools.partial(
        pl.kernel, mesh=_sc_mesh(),
        out_type=jax.ShapeDtypeStruct((n_rows, d), src.dtype),
        scratch_types=[pltpu.VMEM((2, n_ch, SC_CHUNK), jnp.int32),
                       pltpu.VMEM((SC_CHUNK, d), src.dtype),
                       pltpu.VMEM((SC_CHUNK, d), src.dtype),
                       pltpu.SemaphoreType.DMA, pltpu.SemaphoreType.DMA],
    )
    def scatter(src_hbm, idx_hbm, out_hbm, idx_v, rows_a, rows_b, sem_a, sem_b):
        wid = _sc_worker()
        base = wid * per_w
        for s in range(2):
            pltpu.sync_copy(idx_hbm.at[s, wid], idx_v.at[s])

        def move(c, rows_v, sem):
            pltpu.sync_copy(src_hbm.at[pl.ds(base + c * SC_CHUNK, SC_CHUNK)], rows_v)
            return [pltpu.async_copy(rows_v, out_hbm.at[idx_v.at[s, c]], sem) for s in range(2)]

        @pl.loop(0, n_ch, step=2)
        def _(c):
            first = move(c, rows_a, sem_a)
            second = move(c + 1, rows_b, sem_b)
            for cp in first + second:
                cp.wait()

    return scatter(src, idx)


def _sc_gather_rows(table, idx):
    b = idx.shape[0]
    d = table.shape[1]
    workers = SC_CORES * SC_SUBCORES
    per_w = b // workers
    n_ch = per_w // SC_CHUNK
    assert b % (workers * SC_CHUNK * 2) == 0

    @functools.partial(
        pl.kernel, mesh=_sc_mesh(),
        out_type=jax.ShapeDtypeStruct((b, d), table.dtype),
        scratch_types=[pltpu.VMEM((per_w,), jnp.int32),
                       pltpu.VMEM((SC_CHUNK, d), table.dtype),
                       pltpu.VMEM((SC_CHUNK, d), table.dtype),
                       pltpu.SemaphoreType.DMA, pltpu.SemaphoreType.DMA],
    )
    def gather(table_hbm, idx_hbm, out_hbm, idx_v, rows_a, rows_b, sem_a, sem_b):
        base = _sc_worker() * per_w
        pltpu.sync_copy(idx_hbm.at[pl.ds(base, per_w)], idx_v)

        def fetch(c, rows_v, sem):
            return pltpu.async_copy(table_hbm.at[idx_v.at[pl.ds(c * SC_CHUNK, SC_CHUNK)]], rows_v, sem)

        def store(c, rows_v):
            pltpu.sync_copy(rows_v, out_hbm.at[pl.ds(base + c * SC_CHUNK, SC_CHUNK)])

        @pl.loop(0, n_ch, step=2)
        def _(c):
            first = fetch(c, rows_a, sem_a)
            second = fetch(c + 1, rows_b, sem_b)
            first.wait()
            store(c, rows_a)
            second.wait()
            store(c + 1, rows_b)

    return gather(table, idx)


def _expert_kernel(fb_ref, nblk_ref, nv_ref, nu_ref, xs_ref, wg_ref, wu_ref, wd_ref, ys_ref,
                   wgb_ref, wub_ref, wdb_ref, xbuf, ybuf, xsem, ysem):
    e = pl.program_id(0)
    rb = xbuf.shape[1]
    nb = ys_ref.shape[0] // rb
    n_used = nu_ref[0]

    def x_copy(g, slot):
        rows = pl.ds(pl.multiple_of(g * rb, rb), rb)
        return pltpu.make_async_copy(xs_ref.at[rows], xbuf.at[slot], xsem.at[slot])

    def y_copy(g, slot):
        rows = pl.ds(pl.multiple_of(g * rb, rb), rb)
        return pltpu.make_async_copy(ybuf.at[slot], ys_ref.at[rows], ysem.at[slot])

    @pl.when(e == 0)
    def _():
        x_copy(0, 0).start()

    @pl.when(nblk_ref[e] > 0)
    def _():
        wgb_ref[...] = wg_ref[0, 0].astype(BF16)
        wub_ref[...] = wu_ref[0, 0].astype(BF16)
        wdb_ref[...] = wd_ref[0, 0].astype(BF16)

    def block(b, carry):
        g = fb_ref[e] + b
        slot = lax.rem(g, 2)
        x_copy(g, slot).wait()

        @pl.when(g + 1 < n_used)
        def _():
            x_copy(g + 1, 1 - slot).start()

        @pl.when(g >= 2)
        def _():
            y_copy(g - 2, slot).wait()

        nv = nv_ref[g]

        def ffn(rows):
            valid = lax.broadcasted_iota(jnp.int32, (rows, 1), 0) < nv
            x_hi, x_lo = _unpack_bf16_pairs(jnp.where(valid, xbuf[slot, :rows, :], jnp.uint32(0)))
            x_hi, x_lo = x_hi.astype(BF16), x_lo.astype(BF16)
            h = x_hi.shape[1]

            def in_dot(w_ref):
                return (jnp.dot(x_hi, w_ref[:h, :], preferred_element_type=F32)
                        + jnp.dot(x_lo, w_ref[h:, :], preferred_element_type=F32))

            gate = in_dot(wgb_ref)
            up = in_dot(wub_ref)
            hmid = (gate * jax.nn.sigmoid(gate) * up).astype(BF16)
            ybuf[slot, :rows, :] = _pack_bf16_pairs(jnp.dot(hmid, wdb_ref[...], preferred_element_type=F32))
            if rows < rb:
                ybuf[slot, rows:, :] = jnp.zeros((rb - rows, ybuf.shape[2]), ybuf.dtype)

        @pl.when(nv > rb // 2)
        def _():
            ffn(rb)

        @pl.when(nv <= rb // 2)
        def _():
            ffn(rb // 2)

        y_copy(g, slot).start()
        return carry

    lax.fori_loop(0, nblk_ref[e], block, 0)

    @pl.when(e == pl.num_programs(0) - 1)
    def _():
        @pl.when(n_used >= 2)
        def _():
            y_copy(n_used - 2, lax.rem(n_used, 2)).wait()

        y_copy(n_used - 1, lax.rem(n_used - 1, 2)).wait()
        ybuf[0] = jnp.zeros(ybuf.shape[1:], ybuf.dtype)

        def tail_start(g, carry):
            y_copy(g, 0).start()
            return carry

        def tail_wait(g, carry):
            y_copy(g, 0).wait()
            return carry

        lax.fori_loop(n_used, nb, tail_start, 0)
        lax.fori_loop(n_used, nb, tail_wait, 0)


def _experts(xs, first_block, n_blocks, n_valid, n_used, w_gate, w_up, w_down, layer):
    n_rows, dp = xs.shape
    d = 2 * dp
    n_experts, ff = w_gate.shape[1], w_gate.shape[3]
    wsel = lambda e, *_: (layer, e, 0, 0)
    grid_spec = pltpu.PrefetchScalarGridSpec(
        num_scalar_prefetch=4,
        grid=(n_experts,),
        in_specs=[
            pl.BlockSpec(memory_space=pl.ANY),
            pl.BlockSpec((1, 1, d, ff), wsel),
            pl.BlockSpec((1, 1, d, ff), wsel),
            pl.BlockSpec((1, 1, ff, d), wsel),
        ],
        out_specs=pl.BlockSpec(memory_space=pl.ANY),
        scratch_shapes=[pltpu.VMEM((d, ff), BF16), pltpu.VMEM((d, ff), BF16), pltpu.VMEM((ff, d), BF16),
                        pltpu.VMEM((2, ROW_BLOCK, dp), xs.dtype), pltpu.VMEM((2, ROW_BLOCK, dp), xs.dtype),
                        pltpu.SemaphoreType.DMA((2,)), pltpu.SemaphoreType.DMA((2,))],
    )
    return pl.pallas_call(
        _expert_kernel,
        grid_spec=grid_spec,
        out_shape=jax.ShapeDtypeStruct((n_rows, dp), xs.dtype),
        compiler_params=_cparams(("arbitrary",)),
    )(first_block, n_blocks, n_valid, n_used, xs, w_gate, w_up, w_down)


def _final_combine_kernel(x_ref, route_ref, y1_ref, y2_ref, nf_ref, o_ref):
    route = route_ref[...]
    y1 = jnp.concatenate(_unpack_bf16_pairs(y1_ref[...]), axis=1)
    y2 = jnp.concatenate(_unpack_bf16_pairs(y2_ref[...]), axis=1)
    y = x_ref[...] + (route[:, 2:3] * y1 + route[:, 3:4] * y2)
    o_ref[...] = y * lax.rsqrt(jnp.mean(y * y, axis=-1, keepdims=True) + RMS_EPS) * nf_ref[...]


def _final_combine(x1, route, yg, nf):
    t, d = x1.shape
    slot2 = t // TM
    row = lambda i: (i, 0)
    return pl.pallas_call(
        _final_combine_kernel,
        grid=(t // TM,),
        in_specs=[
            pl.BlockSpec((TM, d), row),
            pl.BlockSpec((TM, LANES), row),
            pl.BlockSpec((TM, d // 2), row),
            pl.BlockSpec((TM, d // 2), lambda i: (slot2 + i, 0)),
            pl.BlockSpec((1, d), lambda i: (0, 0)),
        ],
        out_specs=pl.BlockSpec((TM, d), row),
        out_shape=jax.ShapeDtypeStruct((t, d), F32),
        compiler_params=_cparams(("parallel",)),
    )(x1, route, yg, yg, nf.reshape(1, d).astype(F32))


def _moe(hn, route_t, cnt, w_gate, w_up, w_down, layer):
    t, d = hn.shape
    n_experts = w_gate.shape[1]
    ids = jnp.arange(n_experts, dtype=jnp.int32)
    e1, e2, rank1, rank2 = (route_t[f].astype(jnp.int32) for f in (0, 1, 4, 5))
    counts = cnt[:, 0].astype(jnp.int32)
    padded = ((counts + ROW_BLOCK - 1) // ROW_BLOCK) * ROW_BLOCK
    pad_end = jnp.cumsum(padded)
    pad_start = pad_end - padded

    def dest(e, rank):
        return jnp.sum(jnp.where(e[:, None] == ids[None, :], pad_start[None, :], 0), axis=1) + rank

    d1 = dest(e1, rank1)
    d2 = dest(e2, rank2)
    nb = (t * TOP_K_INNER) // ROW_BLOCK + n_experts
    n_used = (pad_end[-1:] // ROW_BLOCK).astype(jnp.int32)
    first_row = jnp.arange(nb, dtype=jnp.int32) * ROW_BLOCK
    block_e = jnp.sum((pad_end[None, :] <= first_row[:, None]).astype(jnp.int32), axis=1)
    block_e = jnp.minimum(block_e, n_experts - 1)
    row_end = jnp.sum(jnp.where(block_e[:, None] == ids[None, :], (pad_start + counts)[None, :], 0), axis=1)
    n_valid = jnp.clip(row_end - first_row, 0, ROW_BLOCK).astype(jnp.int32)
    xs = _sc_scatter_rows(hn, d1, d2, nb * ROW_BLOCK)
    ys = _experts(xs, (pad_start // ROW_BLOCK).astype(jnp.int32), (padded // ROW_BLOCK).astype(jnp.int32),
                  n_valid, n_used, w_gate, w_up, w_down, layer)
    return _sc_gather_rows(ys, jnp.concatenate([d1, d2]))


def _diff_lambda_init(layer_idx):
    return 0.8 - 0.6 * math.exp(-0.3 * layer_idx)


def _rope_tables(seq, dk):
    angle = 1.0 / (10000.0 ** jnp.linspace(0.0, 1.0, dk // 2, dtype=F32))
    angle = jnp.repeat(angle, 2)
    phase = jnp.arange(seq, dtype=F32)[:, None] * angle[None, :]
    sin, cos = jnp.sin(phase), jnp.cos(phase)
    even = (jnp.arange(dk) % 2 == 0)[None, :]
    return cos, jnp.where(even, -sin, 0.0), jnp.where(even, 0.0, sin)


def kernel(x, rel_bias_table, norm_mix, norm_ffn, norm_final, diff_w_in, diff_lambda_qk, diff_subln,
           diff_w_out, ret_w_in, ret_w_out, moe_w_group, moe_b_group, moe_w_expert, moe_b_expert,
           moe_w_gate, moe_w_up, moe_w_down):
    batch, seq, d = x.shape
    depth = norm_mix.shape[0]
    t = batch * seq
    heads = rel_bias_table.shape[1]
    dh = diff_lambda_qk.shape[-1]
    qk_dim = (ret_w_in.shape[2] - 2 * ret_w_out.shape[1]) // 2
    v_dim = ret_w_out.shape[1]
    dk = qk_dim // RET_HEADS
    xt = x.reshape(t, d)
    moe = None
    for i in range(depth):
        j = i // N_MIXERS
        if i % N_MIXERS == 0:
            col_scale = jnp.concatenate([jnp.full((d,), dh ** -0.5 * LOG2E, F32), jnp.ones((2 * d,), F32)])
            qkv = _norm_matmul(xt, norm_mix[i], diff_w_in, j, col_scale, moe=moe)
            if moe is not None:
                qkv, xt = qkv
            mix = _diff_attention(qkv, rel_bias_table, diff_lambda_qk[j], diff_subln[j],
                                  _diff_lambda_init(i), batch, seq, heads)
            gate = None
            w_out = diff_w_out[j].astype(BF16)
        else:
            n_in = ret_w_in.shape[2]
            col_scale = jnp.concatenate([jnp.ones((qk_dim,), F32), jnp.full((qk_dim,), dk ** -0.5, F32),
                                         jnp.ones((n_in - 2 * qk_dim,), F32)])
            qkvg = _norm_matmul(xt, norm_mix[i], ret_w_in, j, col_scale, rope=_rope_tables(seq, dk),
                                n_rope=2 * qk_dim // TN, seq=seq, moe=moe)
            if moe is not None:
                qkvg, xt = qkvg
            mix = _retention(qkvg, batch, seq, qk_dim, v_dim)
            gate = (qkvg, (2 * qk_dim + v_dim) // v_dim)
            w_out = ret_w_out[j].astype(BF16)
        x1, hn, route, route_t, cnt = _out_router(mix, gate, w_out, xt, norm_ffn[i], moe_w_group[i],
                                                  moe_b_group[i], moe_w_expert[i], moe_b_expert[i])
        xt = x1
        moe = (route, _moe(hn, route_t, cnt, moe_w_gate, moe_w_up, moe_w_down, i))
    return _final_combine(xt, moe[0], moe[1], norm_final).reshape(batch, seq, d)
```

```python
import functools
import math

import jax
import jax.numpy as jnp
from jax import lax
from jax.experimental import pallas as pl
from jax.experimental.pallas import tpu as pltpu
from jax.experimental.pallas import tpu_sc as plsc

F32 = jnp.float32
BF16 = jnp.bfloat16

N_MIXERS = 2
NUM_BUCKETS = 32
MAX_DISTANCE = 128
RET_HEADS = 4
TOP_K_INNER = 2
RMS_EPS = 1e-6
SUBLN_EPS = 1e-5

LANES = 128
SUBLANES = 8
BF16_SUBLANES = 16
LOG2E = 1.4426950408889634
VMEM_LIMIT = 56 * 1024 * 1024
SC_CORES = 2
SC_SUBCORES = 16

TM = 512
TN = 1024
TQ = 256
ATTN_HEADS = 4
RET_C = 256
RET_HP = 2
ROUTER_SPLIT = 2
ROW_BLOCK = 512
SC_CHUNK = 64


def _cparams(sem):
    return pltpu.CompilerParams(dimension_semantics=sem, vmem_limit_bytes=VMEM_LIMIT)


def _pack_bf16_pairs(x):
    h = x.shape[1] // 2
    hi = lax.bitcast_convert_type(x[:, :h].astype(BF16).astype(F32), jnp.uint32)
    lo = lax.bitcast_convert_type(x[:, h:].astype(BF16).astype(F32), jnp.uint32)
    return hi | (lo >> 16)


def _unpack_bf16_pairs(w):
    hi = lax.bitcast_convert_type(w & jnp.uint32(0xFFFF0000), F32)
    lo = lax.bitcast_convert_type(w << 16, F32)
    return hi, lo


def _norm_matmul_kernel(x_ref, *rest, n_rope, moe, layer):
    rest = list(rest)
    w_ref, stage_ref, sem = rest[-3:]
    rest = rest[:-3]
    if moe:
        route_ref, y1_ref, y2_ref = rest[:3]
        x_out_ref = rest.pop()
        rest = rest[3:]
    g_ref, scale_ref, w_hbm = rest[:3]
    if n_rope:
        cos_ref, sa_ref, sb_ref, o_ref = rest[3:]
    else:
        (o_ref,) = rest[3:]
    n_chunks = o_ref.shape[1] // TN

    @pl.when(pl.program_id(0) == 0)
    def _():
        def fetch(j):
            return pltpu.make_async_copy(w_hbm.at[layer, :, pl.ds(j * TN, TN)], stage_ref.at[j % 2],
                                         sem.at[j % 2])

        fetch(0).start()
        for j in range(n_chunks):
            if j + 1 < n_chunks:
                fetch(j + 1).start()
            fetch(j).wait()
            cols = slice(j * TN, (j + 1) * TN)
            w_ref[:, cols] = (stage_ref[j % 2] * scale_ref[:, cols]).astype(BF16)

    x = x_ref[...]
    if moe:
        route = route_ref[...]
        y1 = jnp.concatenate(_unpack_bf16_pairs(y1_ref[...]), axis=1)
        y2 = jnp.concatenate(_unpack_bf16_pairs(y2_ref[...]), axis=1)
        x = x + (route[:, 2:3] * y1 + route[:, 3:4] * y2)
        x_out_ref[...] = x
    xn = (x * lax.rsqrt(jnp.mean(x * x, axis=-1, keepdims=True) + RMS_EPS) * g_ref[...]).astype(BF16)
    for j in range(n_chunks):
        cols = slice(j * TN, (j + 1) * TN)
        acc = jnp.dot(xn, w_ref[:, cols], preferred_element_type=F32)
        if j < n_rope:
            reps = TN // cos_ref.shape[1]
            c = jnp.tile(cos_ref[...], (1, reps))
            sa = jnp.tile(sa_ref[...], (1, reps))
            sb = jnp.tile(sb_ref[...], (1, reps))
            nxt = pltpu.roll(acc, TN - 1, axis=1)
            prv = pltpu.roll(acc, 1, axis=1)
            acc = acc * c + nxt * sa + prv * sb
        o_ref[:, cols] = acc.astype(o_ref.dtype)


def _norm_matmul(x, g, w, layer, col_scale, rope=None, n_rope=0, seq=None, moe=None):
    t, d = x.shape
    n = w.shape[2]
    row = lambda i: (i, 0)
    in_specs = [pl.BlockSpec((TM, d), row)]
    args = [x]
    if moe is not None:
        route, yg = moe
        slot2 = t // TM
        in_specs += [pl.BlockSpec((TM, LANES), row), pl.BlockSpec((TM, d // 2), row),
                     pl.BlockSpec((TM, d // 2), lambda i: (slot2 + i, 0))]
        args += [route, yg, yg]
    in_specs += [
        pl.BlockSpec((1, d), lambda i: (0, 0)),
        pl.BlockSpec((1, n), lambda i: (0, 0)),
        pl.BlockSpec(memory_space=pl.ANY),
    ]
    args += [g.reshape(1, d), col_scale.reshape(1, n).astype(F32), w]
    if rope is not None:
        nb = seq // TM
        dk = rope[0].shape[1]
        in_specs += [pl.BlockSpec((TM, dk), lambda i: (i % nb, 0))] * 3
        args += list(rope)
    out_specs = [pl.BlockSpec((TM, n), row)]
    out_shape = [jax.ShapeDtypeStruct((t, n), BF16)]
    if moe is not None:
        out_specs.append(pl.BlockSpec((TM, d), row))
        out_shape.append(jax.ShapeDtypeStruct((t, d), F32))
    out = pl.pallas_call(
        functools.partial(_norm_matmul_kernel, n_rope=n_rope, moe=moe is not None, layer=layer),
        grid=(t // TM,),
        in_specs=in_specs,
        out_specs=out_specs,
        out_shape=out_shape,
        scratch_shapes=[pltpu.VMEM((d, n), BF16), pltpu.VMEM((2, d, TN), F32), pltpu.SemaphoreType.DMA((2,))],
        compiler_params=_cparams(("arbitrary",)),
    )(*args)
    return out if moe is not None else out[0]


def _diff_attn_kernel(q_ref, k_ref, v_ref, bias_ref, lam_ref, g_ref, o_ref,
                      vt_ref, m_ref, acc_ref, s_ref, *, lambda_init):
    seq = q_ref.shape[0]
    hp, _, _, tq = bias_ref.shape
    dv = q_ref.shape[1] // hp
    dh = dv // 2
    nq = seq // tq

    def head(hh):
        return slice(hh * dv, (hh + 1) * dv)

    extra = vt_ref.shape[2] - dv
    ones_row = jnp.where(lax.broadcasted_iota(jnp.int32, (extra, tq), 0) == 0, 1.0, 0.0).astype(BF16)
    for hh in range(hp):
        for j in range(nq):
            vt_ref[hh, j, :dv, :] = v_ref[j * tq:(j + 1) * tq, head(hh)].astype(F32).T.astype(BF16)
            vt_ref[hh, j, dv:, :] = ones_row

    lq = lam_ref[...]
    lam = (jnp.exp(jnp.sum(lq[0:1] * lq[1:2], axis=-1, keepdims=True))
           - jnp.exp(jnp.sum(lq[2:3] * lq[3:4], axis=-1, keepdims=True)) + lambda_init)
    lane = lax.broadcasted_iota(jnp.int32, (tq, dv), 1)

    def query_maps(qi):
        qrows = pl.ds(pl.multiple_of(qi * tq, tq), tq)
        qs = []
        for hh in range(hp):
            q = q_ref[qrows, head(hh)]
            zero = jnp.zeros_like(q)
            qs.append(jnp.concatenate([jnp.where(lane < dh, q, zero), jnp.where(lane >= dh, q, zero)],
                                      axis=0))
        return qs

    def score_dots(j, qs):
        rows = pl.ds(pl.multiple_of(j * tq, tq), tq)
        for hh in range(hp):
            s_ref[hh] = lax.dot_general(k_ref[rows, head(hh)], qs[hh], (((1,), (1,)), ((), ())),
                                        preferred_element_type=F32)

    score_dots(0, query_maps(0))

    def q_body(qi, carry):
        qrows = pl.ds(pl.multiple_of(qi * tq, tq), tq)
        qs = query_maps(qi)
        qs_next = query_maps(jnp.minimum(qi + 1, nq - 1))

        m_ref[...] = jnp.full(m_ref.shape, -jnp.inf, F32)
        acc_ref[...] = jnp.zeros(acc_ref.shape, F32)

        def step(j, tile, prefetch):
            probs, alphas = [], []
            for hh in range(hp):
                s = s_ref[hh]
                if tile is not None:
                    bias = bias_ref[hh, tile]
                    s = s + jnp.concatenate([bias, bias], axis=1)
                m_old = m_ref[hh]
                m_new = jnp.maximum(m_old, jnp.max(s, axis=0, keepdims=True))
                alpha = jnp.exp2(m_old - m_new)
                m_ref[hh] = m_new
                probs.append(jnp.exp2(s - m_new).astype(BF16))
                alphas.append(alpha)
            if prefetch == "block":
                score_dots(j + 1, qs)
            else:
                score_dots(0, qs_next)
            for hh in range(hp):
                acc_ref[hh] = alphas[hh] * acc_ref[hh] + jnp.dot(vt_ref[hh, j], probs[hh],
                                                                 preferred_element_type=F32)

        def far_body(j, c):
            step(j, None, "block")
            return c

        lax.fori_loop(0, jnp.maximum(qi - 1, 0), far_body, 0)

        @pl.when(qi >= 1)
        def _():
            step(qi - 1, 1, "block")

        step(qi, 0, "tile")

        for hh in range(hp):
            acc = acc_ref[hh]
            on = acc[:dv] / acc[dv:dv + 1]
            ot = on[:, :tq] - lam * on[:, tq:]
            ot = ot * lax.rsqrt(jnp.mean(ot * ot, axis=0, keepdims=True) + SUBLN_EPS)
            o_ref[qrows, head(hh)] = (ot.T * g_ref[...] * (1.0 - lambda_init)).astype(o_ref.dtype)
        return carry

    lax.fori_loop(0, nq, q_body, 0)


def _t5_bucket(rel):
    max_exact = NUM_BUCKETS // 2
    n = jnp.maximum(rel, 0)
    is_small = n < max_exact
    nf = jnp.maximum(n, max_exact).astype(F32)
    large = max_exact + (jnp.log(nf / max_exact) / math.log(MAX_DISTANCE / max_exact)
                         * (NUM_BUCKETS - max_exact)).astype(jnp.int32)
    large = jnp.minimum(large, NUM_BUCKETS - 1)
    return jnp.where(is_small, n, large)


def _bias_tiles(rel_table):
    heads = rel_table.shape[1]
    span = 2 * TQ + 1
    bucket = _t5_bucket(jnp.arange(span))
    onehot = (bucket[:, None] == jnp.arange(NUM_BUCKETS)[None, :]).astype(F32)
    table = (rel_table.astype(F32) - rel_table[NUM_BUCKETS - 1].astype(F32)[None, :]) * LOG2E
    by_rel = jnp.dot(onehot, table, precision=lax.Precision.HIGHEST).T
    toep = jnp.tile(by_rel, (1, TQ))[:, :TQ * (span - 1)].reshape(heads, TQ, span - 1)
    kj = jnp.arange(TQ)[:, None]
    qi = jnp.arange(TQ)[None, :]
    diag = jnp.where((qi >= kj)[None], toep[:, :, :TQ], -jnp.inf)
    return jnp.stack([diag, toep[:, :, TQ:]], axis=1)


def _diff_attention(qkv, rel_table, lam_qk, subln_g, lambda_init, batch, seq, heads):
    t = qkv.shape[0]
    d = qkv.shape[1] // 3
    dv = d // heads
    hp = ATTN_HEADS
    assert TQ + 1 >= MAX_DISTANCE and seq % TQ == 0 and dv == LANES and heads % hp == 0
    nq = seq // TQ
    groups = heads // hp
    bias = _bias_tiles(rel_table)
    return pl.pallas_call(
        functools.partial(_diff_attn_kernel, lambda_init=lambda_init),
        grid=(batch, groups),
        in_specs=[
            pl.BlockSpec((seq, hp * dv), lambda b, g: (b, g)),
            pl.BlockSpec((seq, hp * dv), lambda b, g: (b, groups + g)),
            pl.BlockSpec((seq, hp * dv), lambda b, g: (b, 2 * groups + g)),
            pl.BlockSpec((hp, 2, TQ, TQ), lambda b, g: (g, 0, 0, 0)),
            pl.BlockSpec(lam_qk.shape, lambda b, g: (0, 0)),
            pl.BlockSpec((1, dv), lambda b, g: (0, 0)),
        ],
        out_specs=pl.BlockSpec((seq, hp * dv), lambda b, g: (b, g)),
        out_shape=jax.ShapeDtypeStruct((t, d), BF16),
        scratch_shapes=[
            pltpu.VMEM((hp, nq, dv + BF16_SUBLANES, TQ), BF16),
            pltpu.VMEM((hp, 1, 2 * TQ), F32),
            pltpu.VMEM((hp, dv + BF16_SUBLANES, 2 * TQ), F32),
            pltpu.VMEM((hp, TQ, 2 * TQ), F32),
        ],
        compiler_params=_cparams(("parallel", "parallel")),
    )(qkv, qkv, qkv, bias, lam_qk.astype(F32), subln_g.reshape(1, dv).astype(F32))


def _retention_kernel(q_ref, k_ref, v_ref, dec_ref, tab_ref, o_ref, state_ref):
    seq = q_ref.shape[0]
    hp, c, _ = dec_ref.shape
    dk = q_ref.shape[1] // hp
    dv = v_ref.shape[1] // hp
    heads = range(hp)
    state_ref[...] = jnp.zeros(state_ref.shape, F32)

    def body(n, carry):
        rows = pl.ds(pl.multiple_of(n * c, c), c)
        q = [q_ref[rows, hh * dk:(hh + 1) * dk] for hh in heads]
        k = [k_ref[rows, hh * dk:(hh + 1) * dk] for hh in heads]
        v = [v_ref[rows, hh * dv:(hh + 1) * dv] for hh in heads]
        tab = [tab_ref[hh] for hh in heads]
        st = [state_ref[hh] for hh in heads]
        s = [lax.dot_general(q[hh], k[hh], (((1,), (1,)), ((), ())), preferred_element_type=F32)
             for hh in heads]
        cross = [jnp.dot(q[hh], st[hh].astype(BF16), preferred_element_type=F32) for hh in heads]
        kt = [(k[hh].astype(F32) * tab[hh][:, 1:2]).T.astype(BF16) for hh in heads]
        inner = [jnp.dot((s[hh] * dec_ref[hh]).astype(BF16), v[hh], preferred_element_type=F32)
                 for hh in heads]
        for hh in heads:
            state_ref[hh] = st[hh] * tab[hh][0:1, 2:3] + jnp.dot(kt[hh], v[hh], preferred_element_type=F32)
        for hh in heads:
            o = inner[hh] + cross[hh] * tab[hh][:, 0:1]
            o = o * lax.rsqrt(jnp.mean(o * o, axis=-1, keepdims=True) + RMS_EPS)
            o_ref[rows, hh * dv:(hh + 1) * dv] = o.astype(o_ref.dtype)
        return carry

    lax.fori_loop(0, seq // c, body, 0)


def _retention_tables(heads):
    c = RET_C
    log_gamma = jnp.log(1.0 - 2.0 ** (-5.0 - jnp.arange(heads, dtype=F32)))
    pos = jnp.arange(c, dtype=F32)
    dist = pos[:, None] - pos[None, :]
    inner_decay = jnp.where(dist[None] >= 0,
                            jnp.exp(jnp.maximum(dist, 0.0)[None] * log_gamma[:, None, None]), 0.0)
    q_decay = jnp.exp((pos + 1.0)[None] * log_gamma[:, None])
    k_decay = jnp.exp((c - 1.0 - pos)[None] * log_gamma[:, None])
    chunk_decay = jnp.broadcast_to(jnp.exp(c * log_gamma)[:, None], (heads, c))
    tab = jnp.stack([q_decay, k_decay, chunk_decay], axis=-1)
    tab = jnp.pad(tab, ((0, 0), (0, 0), (0, LANES - 3)))
    return inner_decay, tab


def _retention(qkvg, batch, seq, qk_dim, v_dim):
    t = qkvg.shape[0]
    dk = qk_dim // RET_HEADS
    dvh = v_dim // RET_HEADS
    inner_decay, tab = _retention_tables(RET_HEADS)
    hp = RET_HP
    assert RET_HEADS % hp == 0
    groups = RET_HEADS // hp
    kb = qk_dim // (hp * dk)
    vb = 2 * qk_dim // (hp * dvh)
    return pl.pallas_call(
        _retention_kernel,
        grid=(batch, groups),
        in_specs=[
            pl.BlockSpec((seq, hp * dk), lambda b, g: (b, g)),
            pl.BlockSpec((seq, hp * dk), lambda b, g: (b, kb + g)),
            pl.BlockSpec((seq, hp * dvh), lambda b, g: (b, vb + g)),
            pl.BlockSpec((hp, RET_C, RET_C), lambda b, g: (g, 0, 0)),
            pl.BlockSpec((hp, RET_C, LANES), lambda b, g: (g, 0, 0)),
        ],
        out_specs=pl.BlockSpec((seq, hp * dvh), lambda b, g: (b, g)),
        out_shape=jax.ShapeDtypeStruct((t, v_dim), BF16),
        scratch_shapes=[pltpu.VMEM((hp, dk, dvh), F32)],
        compiler_params=_cparams(("parallel", "parallel")),
    )(qkvg, qkvg, qkvg, inner_decay, tab)


def _out_router_kernel(*refs, gated, n_groups, n_experts):
    if gated:
        a_ref, gate_ref, w_ref, x_ref, nf_ref, wr_ref, br_ref = refs[:7]
        x1_ref, hn_ref, route_ref, route_t_ref, cnt_ref, carry_ref = refs[7:]
    else:
        a_ref, w_ref, x_ref, nf_ref, wr_ref, br_ref = refs[:6]
        x1_ref, hn_ref, route_ref, route_t_ref, cnt_ref, carry_ref = refs[6:]
    epg = n_experts // n_groups
    tm = x_ref.shape[0] // ROUTER_SPLIT

    @pl.when(pl.program_id(0) == 0)
    def _():
        carry_ref[...] = jnp.zeros(carry_ref.shape, F32)

    x1_parts = []
    for part in range(ROUTER_SPLIT):
        rows = slice(part * tm, (part + 1) * tm)
        a = a_ref[rows, :]
        if gated:
            gate = gate_ref[rows, :]
            a = gate * jax.nn.sigmoid(gate) * a
        x1_parts.append(x_ref[rows, :] + jnp.dot(a, w_ref[...], preferred_element_type=F32))
    for part in range(ROUTER_SPLIT):
        rows = slice(part * tm, (part + 1) * tm)
        _route_rows(x1_parts[part], rows, nf_ref, wr_ref, br_ref, x1_ref, hn_ref, route_ref, route_t_ref,
                    cnt_ref, carry_ref, n_groups=n_groups, n_experts=n_experts, epg=epg)


def _route_rows(x1, rows, nf_ref, wr_ref, br_ref, x1_ref, hn_ref, route_ref, route_t_ref, cnt_ref,
                carry_ref, *, n_groups, n_experts, epg):
    tm = x1.shape[0]
    x1_ref[rows, :] = x1
    hn = x1 * lax.rsqrt(jnp.mean(x1 * x1, axis=-1, keepdims=True) + RMS_EPS) * nf_ref[...]
    hn_ref[rows, :] = _pack_bf16_pairs(hn)

    hn_hi = hn.astype(BF16)
    hn_lo = (hn - hn_hi.astype(F32)).astype(BF16)
    both = jnp.dot(hn_hi, wr_ref[...], preferred_element_type=F32)
    logits = (both[:, :LANES] + both[:, LANES:]
              + jnp.dot(hn_lo, wr_ref[:, :LANES], preferred_element_type=F32)) + br_ref[...]

    n_rows = -(-(n_groups + n_experts) // SUBLANES) * SUBLANES
    lt = logits.T[:n_rows, :]
    row = lax.broadcasted_iota(jnp.int32, lt.shape, 0)
    big = jnp.int32(LANES)
    neg = jnp.float32(-jnp.inf)

    gl = jnp.where(row < n_groups, lt, neg)
    gmax = jnp.max(gl, axis=0, keepdims=True)
    g_p = 1.0 / jnp.sum(jnp.exp(gl - gmax), axis=0, keepdims=True)
    g_idx = jnp.min(jnp.where(gl == gmax, row, big), axis=0, keepdims=True)

    lo = n_groups + g_idx * epg
    el = jnp.where((row >= lo) & (row < lo + epg), lt, neg)
    v1 = jnp.max(el, axis=0, keepdims=True)
    i1 = jnp.min(jnp.where(el == v1, row, big), axis=0, keepdims=True)
    el2 = jnp.where(row == i1, neg, el)
    v2 = jnp.max(el2, axis=0, keepdims=True)
    i2 = jnp.min(jnp.where(el2 == v2, row, big), axis=0, keepdims=True)
    tt = jnp.exp(v2 - v1)
    w1 = g_p / (1.0 + tt)
    w2 = g_p * tt / (1.0 + tt)
    e1 = i1 - n_groups
    e2 = i2 - n_groups

    erow = lax.broadcasted_iota(jnp.int32, (n_experts, tm), 0)
    oh1 = erow == e1
    oh2 = erow == e2
    onehot = jnp.where(oh1 | oh2, 1.0, 0.0)
    before = lax.broadcasted_iota(jnp.int32, (tm, tm), 0) < lax.broadcasted_iota(jnp.int32, (tm, tm), 1)
    tri = jnp.where(before, 1.0, 0.0).astype(BF16)
    carry = carry_ref[...]
    base = jnp.dot(onehot.astype(BF16), tri, preferred_element_type=F32) + carry[:, 0:1]
    rank1 = jnp.sum(jnp.where(oh1, base, 0.0), axis=0, keepdims=True)
    rank2 = jnp.sum(jnp.where(oh2, base, 0.0), axis=0, keepdims=True)
    carry = carry + jnp.sum(onehot, axis=1, keepdims=True)
    carry_ref[...] = carry
    cnt_ref[...] = carry

    frow = lax.broadcasted_iota(jnp.int32, (LANES, tm), 0)
    out = jnp.zeros((LANES, tm), F32)
    for idx, val in enumerate((e1.astype(F32), e2.astype(F32), w1, w2, rank1, rank2)):
        out = jnp.where(frow == idx, val, out)
    route_t_ref[:, rows] = out[:route_t_ref.shape[0], :]
    route_ref[rows, :] = out.T


def _out_router(a, gate, w, x, nf, w_group, b_group, w_expert, b_expert):
    t, d = x.shape
    k = w.shape[0]
    n_groups = w_group.shape[1]
    n_experts = w_expert.shape[1]
    assert n_groups + n_experts <= LANES
    pad = LANES - n_groups - n_experts
    wr = jnp.pad(jnp.concatenate([w_group, w_expert], axis=1).astype(F32), ((0, 0), (0, pad)))
    wr_hi = wr.astype(BF16)
    wr = jnp.concatenate([wr_hi, (wr - wr_hi.astype(F32)).astype(BF16)], axis=1)
    br = jnp.pad(jnp.concatenate([b_group, b_expert]).astype(F32), (0, pad)).reshape(1, LANES)
    gated = gate is not None
    row = lambda i: (i, 0)
    fixed = lambda i: (0, 0)
    in_specs = [pl.BlockSpec((TM, k), row)]
    args = [a]
    if gated:
        gate_arr, gate_col = gate
        in_specs.append(pl.BlockSpec((TM, k), lambda i: (i, gate_col)))
        args.append(gate_arr)
    in_specs += [
        pl.BlockSpec((k, d), fixed),
        pl.BlockSpec((TM, d), row),
        pl.BlockSpec((1, d), fixed),
        pl.BlockSpec((d, 2 * LANES), fixed),
        pl.BlockSpec((1, LANES), fixed),
    ]
    args += [w, x, nf.reshape(1, d), wr, br]
    return pl.pallas_call(
        functools.partial(_out_router_kernel, gated=gated, n_groups=n_groups, n_experts=n_experts),
        grid=(t // TM,),
        in_specs=in_specs,
        out_specs=[
            pl.BlockSpec((TM, d), row),
            pl.BlockSpec((TM, d // 2), row),
            pl.BlockSpec((TM, LANES), row),
            pl.BlockSpec((SUBLANES, TM), lambda i: (0, i)),
            pl.BlockSpec((n_experts, LANES), fixed),
        ],
        out_shape=[
            jax.ShapeDtypeStruct((t, d), F32),
            jax.ShapeDtypeStruct((t, d // 2), jnp.uint32),
            jax.ShapeDtypeStruct((t, LANES), F32),
            jax.ShapeDtypeStruct((SUBLANES, t), F32),
            jax.ShapeDtypeStruct((n_experts, LANES), F32),
        ],
        scratch_shapes=[pltpu.VMEM((n_experts, LANES), F32)],
        compiler_params=_cparams(("arbitrary",)),
    )(*args)


def _sc_mesh():
    return plsc.VectorSubcoreMesh(core_axis_name="c", subcore_axis_name="s",
                                  num_cores=SC_CORES, num_subcores=SC_SUBCORES)


def _sc_worker():
    return lax.axis_index("s") * SC_CORES + lax.axis_index("c")


def _sc_scatter_rows(src, d1, d2, n_rows):
    t, d = src.shape
    workers = SC_CORES * SC_SUBCORES
    per_w = t // workers
    n_ch = per_w // SC_CHUNK
    assert t % (workers * SC_CHUNK) == 0
    idx = jnp.stack([d1, d2]).reshape(2, workers, n_ch, SC_CHUNK)

    @functools.partial(
        pl.kernel, mesh=_sc_mesh(),
        out_type=jax.ShapeDtypeStruct((n_rows, d), src.dtype),
        scratch_types=[pltpu.VMEM((2, n_ch, SC_CHUNK), jnp.int32),
                       pltpu.VMEM((SC_CHUNK, d), src.dtype),
                       pltpu.VMEM((SC_CHUNK, d), src.dtype),
                       pltpu.SemaphoreType.DMA, pltpu.SemaphoreType.DMA],
    )
    def scatter(src_hbm, idx_hbm, out_hbm, idx_v, rows_a, rows_b, sem_a, sem_b):
        wid = _sc_worker()
        base = wid * per_w
        for s in range(2):
            pltpu.sync_copy(idx_hbm.at[s, wid], idx_v.at[s])

        def move(c, rows_v, sem):
            pltpu.sync_copy(src_hbm.at[pl.ds(base + c * SC_CHUNK, SC_CHUNK)], rows_v)
            return [pltpu.async_copy(rows_v, out_hbm.at[idx_v.at[s, c]], sem) for s in range(2)]

        @pl.loop(0, n_ch, step=2)
        def _(c):
            first = move(c, rows_a, sem_a)
            second = move(c + 1, rows_b, sem_b)
            for cp in first + second:
                cp.wait()

    return scatter(src, idx)


def _sc_gather_rows(table, idx):
    b = idx.shape[0]
    d = table.shape[1]
    workers = SC_CORES * SC_SUBCORES
    per_w = b // workers
    n_ch = per_w // SC_CHUNK
    assert b % (workers * SC_CHUNK * 2) == 0

    @functools.partial(
        pl.kernel, mesh=_sc_mesh(),
        out_type=jax.ShapeDtypeStruct((b, d), table.dtype),
        scratch_types=[pltpu.VMEM((per_w,), jnp.int32),
                       pltpu.VMEM((SC_CHUNK, d), table.dtype),
                       pltpu.VMEM((SC_CHUNK, d), table.dtype),
                       pltpu.SemaphoreType.DMA, pltpu.SemaphoreType.DMA],
    )
    def gather(table_hbm, idx_hbm, out_hbm, idx_v, rows_a, rows_b, sem_a, sem_b):
        base = _sc_worker() * per_w
        pltpu.sync_copy(idx_hbm.at[pl.ds(base, per_w)], idx_v)

        def fetch(c, rows_v, sem):
            return pltpu.async_copy(table_hbm.at[idx_v.at[pl.ds(c * SC_CHUNK, SC_CHUNK)]], rows_v, sem)

        def store(c, rows_v):
            pltpu.sync_copy(rows_v, out_hbm.at[pl.ds(base + c * SC_CHUNK, SC_CHUNK)])

        @pl.loop(0, n_ch, step=2)
        def _(c):
            first = fetch(c, rows_a, sem_a)
            second = fetch(c + 1, rows_b, sem_b)
            first.wait()
            store(c, rows_a)
            second.wait()
            store(c + 1, rows_b)

    return gather(table, idx)


def _expert_kernel(fb_ref, nblk_ref, nv_ref, nu_ref, xs_ref, wg_ref, wu_ref, wd_ref, ys_ref,
                   wgb_ref, wub_ref, wdb_ref, xbuf, ybuf, xsem, ysem):
    e = pl.program_id(0)
    rb = xbuf.shape[1]
    nb = ys_ref.shape[0] // rb
    n_used = nu_ref[0]

    def x_copy(g, slot):
        rows = pl.ds(pl.multiple_of(g * rb, rb), rb)
        return pltpu.make_async_copy(xs_ref.at[rows], xbuf.at[slot], xsem.at[slot])

    def y_copy(g, slot):
        rows = pl.ds(pl.multiple_of(g * rb, rb), rb)
        return pltpu.make_async_copy(ybuf.at[slot], ys_ref.at[rows], ysem.at[slot])

    @pl.when(e == 0)
    def _():
        x_copy(0, 0).start()

    @pl.when(nblk_ref[e] > 0)
    def _():
        wgb_ref[...] = wg_ref[0, 0].astype(BF16)
        wub_ref[...] = wu_ref[0, 0].astype(BF16)
        wdb_ref[...] = wd_ref[0, 0].astype(BF16)

    def block(b, carry):
        g = fb_ref[e] + b
        slot = lax.rem(g, 2)
        x_copy(g, slot).wait()

        @pl.when(g + 1 < n_used)
        def _():
            x_copy(g + 1, 1 - slot).start()

        @pl.when(g >= 2)
        def _():
            y_copy(g - 2, slot).wait()

        nv = nv_ref[g]

        def ffn(rows):
            valid = lax.broadcasted_iota(jnp.int32, (rows, 1), 0) < nv
            x_hi, x_lo = _unpack_bf16_pairs(jnp.where(valid, xbuf[slot, :rows, :], jnp.uint32(0)))
            x = jnp.concatenate([x_hi.astype(BF16), x_lo.astype(BF16)], axis=1)
            gate = jnp.dot(x, wgb_ref[...], preferred_element_type=F32)
            up = jnp.dot(x, wub_ref[...], preferred_element_type=F32)
            hmid = (gate * jax.nn.sigmoid(gate) * up).astype(BF16)
            ybuf[slot, :rows, :] = _pack_bf16_pairs(jnp.dot(hmid, wdb_ref[...], preferred_element_type=F32))
            if rows < rb:
                ybuf[slot, rows:, :] = jnp.zeros((rb - rows, ybuf.shape[2]), ybuf.dtype)

        @pl.when(nv > rb // 2)
        def _():
            ffn(rb)

        @pl.when(nv <= rb // 2)
        def _():
            ffn(rb // 2)

        y_copy(g, slot).start()
        return carry

    lax.fori_loop(0, nblk_ref[e], block, 0)

    @pl.when(e == pl.num_programs(0) - 1)
    def _():
        @pl.when(n_used >= 2)
        def _():
            y_copy(n_used - 2, lax.rem(n_used, 2)).wait()

        y_copy(n_used - 1, lax.rem(n_used - 1, 2)).wait()
        ybuf[0] = jnp.zeros(ybuf.shape[1:], ybuf.dtype)

        def tail_start(g, carry):
            y_copy(g, 0).start()
            return carry

        def tail_wait(g, carry):
            y_copy(g, 0).wait()
            return carry

        lax.fori_loop(n_used, nb, tail_start, 0)
        lax.fori_loop(n_used, nb, tail_wait, 0)


def _experts(xs, first_block, n_blocks, n_valid, n_used, w_gate, w_up, w_down, layer):
    n_rows, dp = xs.shape
    d = 2 * dp
    n_experts, ff = w_gate.shape[1], w_gate.shape[3]
    wsel = lambda e, *_: (layer, e, 0, 0)
    grid_spec = pltpu.PrefetchScalarGridSpec(
        num_scalar_prefetch=4,
        grid=(n_experts,),
        in_specs=[
            pl.BlockSpec(memory_space=pl.ANY),
            pl.BlockSpec((1, 1, d, ff), wsel),
            pl.BlockSpec((1, 1, d, ff), wsel),
            pl.BlockSpec((1, 1, ff, d), wsel),
        ],
        out_specs=pl.BlockSpec(memory_space=pl.ANY),
        scratch_shapes=[pltpu.VMEM((d, ff), BF16), pltpu.VMEM((d, ff), BF16), pltpu.VMEM((ff, d), BF16),
                        pltpu.VMEM((2, ROW_BLOCK, dp), xs.dtype), pltpu.VMEM((2, ROW_BLOCK, dp), xs.dtype),
                        pltpu.SemaphoreType.DMA((2,)), pltpu.SemaphoreType.DMA((2,))],
    )
    return pl.pallas_call(
        _expert_kernel,
        grid_spec=grid_spec,
        out_shape=jax.ShapeDtypeStruct((n_rows, dp), xs.dtype),
        compiler_params=_cparams(("arbitrary",)),
    )(first_block, n_blocks, n_valid, n_used, xs, w_gate, w_up, w_down)


def _final_combine_kernel(x_ref, route_ref, y1_ref, y2_ref, nf_ref, o_ref):
    route = route_ref[...]
    y1 = jnp.concatenate(_unpack_bf16_pairs(y1_ref[...]), axis=1)
    y2 = jnp.concatenate(_unpack_bf16_pairs(y2_ref[...]), axis=1)
    y = x_ref[...] + (route[:, 2:3] * y1 + route[:, 3:4] * y2)
    o_ref[...] = y * lax.rsqrt(jnp.mean(y * y, axis=-1, keepdims=True) + RMS_EPS) * nf_ref[...]


def _final_combine(x1, route, yg, nf):
    t, d = x1.shape
    slot2 = t // TM
    row = lambda i: (i, 0)
    return pl.pallas_call(
        _final_combine_kernel,
        grid=(t // TM,),
        in_specs=[
            pl.BlockSpec((TM, d), row),
            pl.BlockSpec((TM, LANES), row),
            pl.BlockSpec((TM, d // 2), row),
            pl.BlockSpec((TM, d // 2), lambda i: (slot2 + i, 0)),
            pl.BlockSpec((1, d), lambda i: (0, 0)),
        ],
        out_specs=pl.BlockSpec((TM, d), row),
        out_shape=jax.ShapeDtypeStruct((t, d), F32),
        compiler_params=_cparams(("parallel",)),
    )(x1, route, yg, yg, nf.reshape(1, d).astype(F32))


def _moe(hn, route_t, cnt, w_gate, w_up, w_down, layer):
    t, d = hn.shape
    n_experts = w_gate.shape[1]
    ids = jnp.arange(n_experts, dtype=jnp.int32)
    e1, e2, rank1, rank2 = (route_t[f].astype(jnp.int32) for f in (0, 1, 4, 5))
    counts = cnt[:, 0].astype(jnp.int32)
    padded = ((counts + ROW_BLOCK - 1) // ROW_BLOCK) * ROW_BLOCK
    pad_end = jnp.cumsum(padded)
    pad_start = pad_end - padded

    def dest(e, rank):
        return jnp.sum(jnp.where(e[:, None] == ids[None, :], pad_start[None, :], 0), axis=1) + rank

    d1 = dest(e1, rank1)
    d2 = dest(e2, rank2)
    nb = (t * TOP_K_INNER) // ROW_BLOCK + n_experts
    n_used = (pad_end[-1:] // ROW_BLOCK).astype(jnp.int32)
    first_row = jnp.arange(nb, dtype=jnp.int32) * ROW_BLOCK
    block_e = jnp.sum((pad_end[None, :] <= first_row[:, None]).astype(jnp.int32), axis=1)
    block_e = jnp.minimum(block_e, n_experts - 1)
    row_end = jnp.sum(jnp.where(block_e[:, None] == ids[None, :], (pad_start + counts)[None, :], 0), axis=1)
    n_valid = jnp.clip(row_end - first_row, 0, ROW_BLOCK).astype(jnp.int32)
    xs = _sc_scatter_rows(hn, d1, d2, nb * ROW_BLOCK)
    ys = _experts(xs, (pad_start // ROW_BLOCK).astype(jnp.int32), (padded // ROW_BLOCK).astype(jnp.int32),
                  n_valid, n_used, w_gate, w_up, w_down, layer)
    return _sc_gather_rows(ys, jnp.concatenate([d1, d2]))


def _diff_lambda_init(layer_idx):
    return 0.8 - 0.6 * math.exp(-0.3 * layer_idx)


def _rope_tables(seq, dk):
    angle = 1.0 / (10000.0 ** jnp.linspace(0.0, 1.0, dk // 2, dtype=F32))
    angle = jnp.repeat(angle, 2)
    phase = jnp.arange(seq, dtype=F32)[:, None] * angle[None, :]
    sin, cos = jnp.sin(phase), jnp.cos(phase)
    even = (jnp.arange(dk) % 2 == 0)[None, :]
    return cos, jnp.where(even, -sin, 0.0), jnp.where(even, 0.0, sin)


def kernel(x, rel_bias_table, norm_mix, norm_ffn, norm_final, diff_w_in, diff_lambda_qk, diff_subln,
           diff_w_out, ret_w_in, ret_w_out, moe_w_group, moe_b_group, moe_w_expert, moe_b_expert,
           moe_w_gate, moe_w_up, moe_w_down):
    batch, seq, d = x.shape
    depth = norm_mix.shape[0]
    t = batch * seq
    heads = rel_bias_table.shape[1]
    dh = diff_lambda_qk.shape[-1]
    qk_dim = (ret_w_in.shape[2] - 2 * ret_w_out.shape[1]) // 2
    v_dim = ret_w_out.shape[1]
    dk = qk_dim // RET_HEADS
    xt = x.reshape(t, d)
    moe = None
    for i in range(depth):
        j = i // N_MIXERS
        if i % N_MIXERS == 0:
            col_scale = jnp.concatenate([jnp.full((d,), dh ** -0.5 * LOG2E, F32), jnp.ones((2 * d,), F32)])
            qkv = _norm_matmul(xt, norm_mix[i], diff_w_in, j, col_scale, moe=moe)
            if moe is not None:
                qkv, xt = qkv
            mix = _diff_attention(qkv, rel_bias_table, diff_lambda_qk[j], diff_subln[j],
                                  _diff_lambda_init(i), batch, seq, heads)
            gate = None
            w_out = diff_w_out[j].astype(BF16)
        else:
            n_in = ret_w_in.shape[2]
            col_scale = jnp.concatenate([jnp.ones((qk_dim,), F32), jnp.full((qk_dim,), dk ** -0.5, F32),
                                         jnp.ones((n_in - 2 * qk_dim,), F32)])
            qkvg = _norm_matmul(xt, norm_mix[i], ret_w_in, j, col_scale, rope=_rope_tables(seq, dk),
                                n_rope=2 * qk_dim // TN, seq=seq, moe=moe)
            if moe is not None:
                qkvg, xt = qkvg
            mix = _retention(qkvg, batch, seq, qk_dim, v_dim)
            gate = (qkvg, (2 * qk_dim + v_dim) // v_dim)
            w_out = ret_w_out[j].astype(BF16)
        x1, hn, route, route_t, cnt = _out_router(mix, gate, w_out, xt, norm_ffn[i], moe_w_group[i],
                                                  moe_b_group[i], moe_w_expert[i], moe_b_expert[i])
        xt = x1
        moe = (route, _moe(hn, route_t, cnt, moe_w_gate, moe_w_up, moe_w_down, i))
    return _final_combine(xt, moe[0], moe[1], norm_final).reshape(batch, seq, d)
```

```python
import functools
import math

import jax
import jax.numpy as jnp
from jax import lax
from jax.experimental import pallas as pl
from jax.experimental.pallas import tpu as pltpu
from jax.experimental.pallas import tpu_sc as plsc

F32 = jnp.float32
BF16 = jnp.bfloat16

N_MIXERS = 2
NUM_BUCKETS = 32
MAX_DISTANCE = 128
RET_HEADS = 4
TOP_K_INNER = 2
RMS_EPS = 1e-6
SUBLN_EPS = 1e-5

LANES = 128
SUBLANES = 8
BF16_SUBLANES = 16
LOG2E = 1.4426950408889634
VMEM_LIMIT = 56 * 1024 * 1024
SC_CORES = 2
SC_SUBCORES = 16

TM = 512
TN = 1024
TQ = 256
ATTN_HEADS = 4
RET_C = 256
RET_HP = 2
ROUTER_SPLIT = 2
ROW_BLOCK = 512
SC_CHUNK = 64


def _cparams(sem):
    return pltpu.CompilerParams(dimension_semantics=sem, vmem_limit_bytes=VMEM_LIMIT)


def _pack_bf16_pairs(x):
    h = x.shape[1] // 2
    hi = lax.bitcast_convert_type(x[:, :h].astype(BF16).astype(F32), jnp.uint32)
    lo = lax.bitcast_convert_type(x[:, h:].astype(BF16).astype(F32), jnp.uint32)
    return hi | (lo >> 16)


def _unpack_bf16_pairs(w):
    hi = lax.bitcast_convert_type(w & jnp.uint32(0xFFFF0000), F32)
    lo = lax.bitcast_convert_type(w << 16, F32)
    return hi, lo


def _norm_matmul_kernel(x_ref, *rest, n_rope, moe, layer):
    rest = list(rest)
    w_ref, stage_ref, sem = rest[-3:]
    rest = rest[:-3]
    if moe:
        route_ref, y1_ref, y2_ref = rest[:3]
        x_out_ref = rest.pop()
        rest = rest[3:]
    g_ref, scale_ref, w_hbm = rest[:3]
    if n_rope:
        cos_ref, sa_ref, sb_ref, o_ref = rest[3:]
    else:
        (o_ref,) = rest[3:]
    n_chunks = o_ref.shape[1] // TN

    @pl.when(pl.program_id(0) == 0)
    def _():
        def fetch(j):
            return pltpu.make_async_copy(w_hbm.at[layer, :, pl.ds(j * TN, TN)], stage_ref.at[j % 2],
                                         sem.at[j % 2])

        fetch(0).start()
        for j in range(n_chunks):
            if j + 1 < n_chunks:
                fetch(j + 1).start()
            fetch(j).wait()
            cols = slice(j * TN, (j + 1) * TN)
            w_ref[:, cols] = (stage_ref[j % 2] * scale_ref[:, cols]).astype(BF16)

    x = x_ref[...]
    if moe:
        route = route_ref[...]
        y1 = jnp.concatenate(_unpack_bf16_pairs(y1_ref[...]), axis=1)
        y2 = jnp.concatenate(_unpack_bf16_pairs(y2_ref[...]), axis=1)
        x = x + (route[:, 2:3] * y1 + route[:, 3:4] * y2)
        x_out_ref[...] = x
    xn = (x * lax.rsqrt(jnp.mean(x * x, axis=-1, keepdims=True) + RMS_EPS) * g_ref[...]).astype(BF16)
    for j in range(n_chunks):
        cols = slice(j * TN, (j + 1) * TN)
        acc = jnp.dot(xn, w_ref[:, cols], preferred_element_type=F32)
        if j < n_rope:
            reps = TN // cos_ref.shape[1]
            c = jnp.tile(cos_ref[...], (1, reps))
            sa = jnp.tile(sa_ref[...], (1, reps))
            sb = jnp.tile(sb_ref[...], (1, reps))
            nxt = pltpu.roll(acc, TN - 1, axis=1)
            prv = pltpu.roll(acc, 1, axis=1)
            acc = acc * c + nxt * sa + prv * sb
        o_ref[:, cols] = acc.astype(o_ref.dtype)


def _norm_matmul(x, g, w, layer, col_scale, rope=None, n_rope=0, seq=None, moe=None):
    t, d = x.shape
    n = w.shape[2]
    row = lambda i: (i, 0)
    in_specs = [pl.BlockSpec((TM, d), row)]
    args = [x]
    if moe is not None:
        route, yg = moe
        slot2 = t // TM
        in_specs += [pl.BlockSpec((TM, LANES), row), pl.BlockSpec((TM, d // 2), row),
                     pl.BlockSpec((TM, d // 2), lambda i: (slot2 + i, 0))]
        args += [route, yg, yg]
    in_specs += [
        pl.BlockSpec((1, d), lambda i: (0, 0)),
        pl.BlockSpec((1, n), lambda i: (0, 0)),
        pl.BlockSpec(memory_space=pl.ANY),
    ]
    args += [g.reshape(1, d), col_scale.reshape(1, n).astype(F32), w]
    if rope is not None:
        nb = seq // TM
        dk = rope[0].shape[1]
        in_specs += [pl.BlockSpec((TM, dk), lambda i: (i % nb, 0))] * 3
        args += list(rope)
    out_specs = [pl.BlockSpec((TM, n), row)]
    out_shape = [jax.ShapeDtypeStruct((t, n), BF16)]
    if moe is not None:
        out_specs.append(pl.BlockSpec((TM, d), row))
        out_shape.append(jax.ShapeDtypeStruct((t, d), F32))
    out = pl.pallas_call(
        functools.partial(_norm_matmul_kernel, n_rope=n_rope, moe=moe is not None, layer=layer),
        grid=(t // TM,),
        in_specs=in_specs,
        out_specs=out_specs,
        out_shape=out_shape,
        scratch_shapes=[pltpu.VMEM((d, n), BF16), pltpu.VMEM((2, d, TN), F32), pltpu.SemaphoreType.DMA((2,))],
        compiler_params=_cparams(("arbitrary",)),
    )(*args)
    return out if moe is not None else out[0]


def _diff_attn_kernel(q_ref, k_ref, v_ref, bias_ref, lam_ref, g_ref, o_ref,
                      vt_ref, m_ref, acc_ref, s_ref, *, lambda_init):
    seq = q_ref.shape[0]
    hp, _, _, tq = bias_ref.shape
    dv = q_ref.shape[1] // hp
    dh = dv // 2
    nq = seq // tq

    def head(hh):
        return slice(hh * dv, (hh + 1) * dv)

    extra = vt_ref.shape[2] - dv
    ones_row = jnp.where(lax.broadcasted_iota(jnp.int32, (extra, tq), 0) == 0, 1.0, 0.0).astype(BF16)
    for hh in range(hp):
        for j in range(nq):
            vt_ref[hh, j, :dv, :] = v_ref[j * tq:(j + 1) * tq, head(hh)].astype(F32).T.astype(BF16)
            vt_ref[hh, j, dv:, :] = ones_row

    lq = lam_ref[...]
    lam = (jnp.exp(jnp.sum(lq[0:1] * lq[1:2], axis=-1, keepdims=True))
           - jnp.exp(jnp.sum(lq[2:3] * lq[3:4], axis=-1, keepdims=True)) + lambda_init)
    lane = lax.broadcasted_iota(jnp.int32, (tq, dv), 1)

    def query_maps(qi):
        qrows = pl.ds(pl.multiple_of(qi * tq, tq), tq)
        qs = []
        for hh in range(hp):
            q = q_ref[qrows, head(hh)]
            zero = jnp.zeros_like(q)
            qs.append(jnp.concatenate([jnp.where(lane < dh, q, zero), jnp.where(lane >= dh, q, zero)],
                                      axis=0))
        return qs

    def score_dots(j, qs):
        rows = pl.ds(pl.multiple_of(j * tq, tq), tq)
        for hh in range(hp):
            s_ref[hh] = lax.dot_general(k_ref[rows, head(hh)], qs[hh], (((1,), (1,)), ((), ())),
                                        preferred_element_type=F32)

    score_dots(0, query_maps(0))

    def q_body(qi, carry):
        qrows = pl.ds(pl.multiple_of(qi * tq, tq), tq)
        qs = query_maps(qi)
        qs_next = query_maps(jnp.minimum(qi + 1, nq - 1))

        m_ref[...] = jnp.full(m_ref.shape, -jnp.inf, F32)
        acc_ref[...] = jnp.zeros(acc_ref.shape, F32)

        def step(j, tile, prefetch):
            probs, alphas = [], []
            for hh in range(hp):
                s = s_ref[hh]
                if tile is not None:
                    bias = bias_ref[hh, tile]
                    s = s + jnp.concatenate([bias, bias], axis=1)
                m_old = m_ref[hh]
                m_new = jnp.maximum(m_old, jnp.max(s, axis=0, keepdims=True))
                alpha = jnp.exp2(m_old - m_new)
                m_ref[hh] = m_new
                probs.append(jnp.exp2(s - m_new).astype(BF16))
                alphas.append(alpha)
            if prefetch == "block":
                score_dots(j + 1, qs)
            else:
                score_dots(0, qs_next)
            for hh in range(hp):
                acc_ref[hh] = alphas[hh] * acc_ref[hh] + jnp.dot(vt_ref[hh, j], probs[hh],
                                                                 preferred_element_type=F32)

        def far_body(j, c):
            step(j, None, "block")
            return c

        lax.fori_loop(0, jnp.maximum(qi - 1, 0), far_body, 0)

        @pl.when(qi >= 1)
        def _():
            step(qi - 1, 1, "block")

        step(qi, 0, "tile")

        for hh in range(hp):
            acc = acc_ref[hh]
            on = acc[:dv] / acc[dv:dv + 1]
            ot = on[:, :tq] - lam * on[:, tq:]
            ot = ot * lax.rsqrt(jnp.mean(ot * ot, axis=0, keepdims=True) + SUBLN_EPS)
            o_ref[qrows, head(hh)] = (ot.T * g_ref[...] * (1.0 - lambda_init)).astype(o_ref.dtype)
        return carry

    lax.fori_loop(0, nq, q_body, 0)


def _t5_bucket(rel):
    max_exact = NUM_BUCKETS // 2
    n = jnp.maximum(rel, 0)
    is_small = n < max_exact
    nf = jnp.maximum(n, max_exact).astype(F32)
    large = max_exact + (jnp.log(nf / max_exact) / math.log(MAX_DISTANCE / max_exact)
                         * (NUM_BUCKETS - max_exact)).astype(jnp.int32)
    large = jnp.minimum(large, NUM_BUCKETS - 1)
    return jnp.where(is_small, n, large)


def _bias_tiles(rel_table):
    heads = rel_table.shape[1]
    span = 2 * TQ + 1
    bucket = _t5_bucket(jnp.arange(span))
    onehot = (bucket[:, None] == jnp.arange(NUM_BUCKETS)[None, :]).astype(F32)
    table = (rel_table.astype(F32) - rel_table[NUM_BUCKETS - 1].astype(F32)[None, :]) * LOG2E
    by_rel = jnp.dot(onehot, table, precision=lax.Precision.HIGHEST).T
    toep = jnp.tile(by_rel, (1, TQ))[:, :TQ * (span - 1)].reshape(heads, TQ, span - 1)
    kj = jnp.arange(TQ)[:, None]
    qi = jnp.arange(TQ)[None, :]
    diag = jnp.where((qi >= kj)[None], toep[:, :, :TQ], -jnp.inf)
    return jnp.stack([diag, toep[:, :, TQ:]], axis=1)


def _diff_attention(qkv, rel_table, lam_qk, subln_g, lambda_init, batch, seq, heads):
    t = qkv.shape[0]
    d = qkv.shape[1] // 3
    dv = d // heads
    hp = ATTN_HEADS
    assert TQ + 1 >= MAX_DISTANCE and seq % TQ == 0 and dv == LANES and heads % hp == 0
    nq = seq // TQ
    groups = heads // hp
    bias = _bias_tiles(rel_table)
    return pl.pallas_call(
        functools.partial(_diff_attn_kernel, lambda_init=lambda_init),
        grid=(batch, groups),
        in_specs=[
            pl.BlockSpec((seq, hp * dv), lambda b, g: (b, g)),
            pl.BlockSpec((seq, hp * dv), lambda b, g: (b, groups + g)),
            pl.BlockSpec((seq, hp * dv), lambda b, g: (b, 2 * groups + g)),
            pl.BlockSpec((hp, 2, TQ, TQ), lambda b, g: (g, 0, 0, 0)),
            pl.BlockSpec(lam_qk.shape, lambda b, g: (0, 0)),
            pl.BlockSpec((1, dv), lambda b, g: (0, 0)),
        ],
        out_specs=pl.BlockSpec((seq, hp * dv), lambda b, g: (b, g)),
        out_shape=jax.ShapeDtypeStruct((t, d), BF16),
        scratch_shapes=[
            pltpu.VMEM((hp, nq, dv + BF16_SUBLANES, TQ), BF16),
            pltpu.VMEM((hp, 1, 2 * TQ), F32),
            pltpu.VMEM((hp, dv + BF16_SUBLANES, 2 * TQ), F32),
            pltpu.VMEM((hp, TQ, 2 * TQ), F32),
        ],
        compiler_params=_cparams(("parallel", "parallel")),
    )(qkv, qkv, qkv, bias, lam_qk.astype(F32), subln_g.reshape(1, dv).astype(F32))


def _retention_kernel(q_ref, k_ref, v_ref, dec_ref, tab_ref, o_ref, state_ref):
    seq = q_ref.shape[0]
    hp, c, _ = dec_ref.shape
    dk = q_ref.shape[1] // hp
    dv = v_ref.shape[1] // hp
    heads = range(hp)
    state_ref[...] = jnp.zeros(state_ref.shape, F32)

    def body(n, carry):
        rows = pl.ds(pl.multiple_of(n * c, c), c)
        q = [q_ref[rows, hh * dk:(hh + 1) * dk] for hh in heads]
        k = [k_ref[rows, hh * dk:(hh + 1) * dk] for hh in heads]
        v = [v_ref[rows, hh * dv:(hh + 1) * dv] for hh in heads]
        tab = [tab_ref[hh] for hh in heads]
        st = [state_ref[hh] for hh in heads]
        s = [lax.dot_general(q[hh], k[hh], (((1,), (1,)), ((), ())), preferred_element_type=F32)
             for hh in heads]
        cross = [jnp.dot(q[hh], st[hh].astype(BF16), preferred_element_type=F32) for hh in heads]
        kt = [(k[hh].astype(F32) * tab[hh][:, 1:2]).T.astype(BF16) for hh in heads]
        inner = [jnp.dot((s[hh] * dec_ref[hh]).astype(BF16), v[hh], preferred_element_type=F32)
                 for hh in heads]
        for hh in heads:
            state_ref[hh] = st[hh] * tab[hh][0:1, 2:3] + jnp.dot(kt[hh], v[hh], preferred_element_type=F32)
        for hh in heads:
            o = inner[hh] + cross[hh] * tab[hh][:, 0:1]
            o = o * lax.rsqrt(jnp.mean(o * o, axis=-1, keepdims=True) + RMS_EPS)
            o_ref[rows, hh * dv:(hh + 1) * dv] = o.astype(o_ref.dtype)
        return carry

    lax.fori_loop(0, seq // c, body, 0)


def _retention_tables(heads):
    c = RET_C
    log_gamma = jnp.log(1.0 - 2.0 ** (-5.0 - jnp.arange(heads, dtype=F32)))
    pos = jnp.arange(c, dtype=F32)
    dist = pos[:, None] - pos[None, :]
    inner_decay = jnp.where(dist[None] >= 0,
                            jnp.exp(jnp.maximum(dist, 0.0)[None] * log_gamma[:, None, None]), 0.0)
    q_decay = jnp.exp((pos + 1.0)[None] * log_gamma[:, None])
    k_decay = jnp.exp((c - 1.0 - pos)[None] * log_gamma[:, None])
    chunk_decay = jnp.broadcast_to(jnp.exp(c * log_gamma)[:, None], (heads, c))
    tab = jnp.stack([q_decay, k_decay, chunk_decay], axis=-1)
    tab = jnp.pad(tab, ((0, 0), (0, 0), (0, LANES - 3)))
    return inner_decay, tab


def _retention(qkvg, batch, seq, qk_dim, v_dim):
    t = qkvg.shape[0]
    dk = qk_dim // RET_HEADS
    dvh = v_dim // RET_HEADS
    inner_decay, tab = _retention_tables(RET_HEADS)
    hp = RET_HP
    assert RET_HEADS % hp == 0
    groups = RET_HEADS // hp
    kb = qk_dim // (hp * dk)
    vb = 2 * qk_dim // (hp * dvh)
    return pl.pallas_call(
        _retention_kernel,
        grid=(batch, groups),
        in_specs=[
            pl.BlockSpec((seq, hp * dk), lambda b, g: (b, g)),
            pl.BlockSpec((seq, hp * dk), lambda b, g: (b, kb + g)),
            pl.BlockSpec((seq, hp * dvh), lambda b, g: (b, vb + g)),
            pl.BlockSpec((hp, RET_C, RET_C), lambda b, g: (g, 0, 0)),
            pl.BlockSpec((hp, RET_C, LANES), lambda b, g: (g, 0, 0)),
        ],
        out_specs=pl.BlockSpec((seq, hp * dvh), lambda b, g: (b, g)),
        out_shape=jax.ShapeDtypeStruct((t, v_dim), BF16),
        scratch_shapes=[pltpu.VMEM((hp, dk, dvh), F32)],
        compiler_params=_cparams(("parallel", "parallel")),
    )(qkvg, qkvg, qkvg, inner_decay, tab)


def _out_router_kernel(*refs, gated, n_groups, n_experts):
    if gated:
        a_ref, gate_ref, w_ref, x_ref, nf_ref, wr_ref, br_ref = refs[:7]
        x1_ref, hn_ref, route_ref, route_t_ref, cnt_ref, carry_ref = refs[7:]
    else:
        a_ref, w_ref, x_ref, nf_ref, wr_ref, br_ref = refs[:6]
        x1_ref, hn_ref, route_ref, route_t_ref, cnt_ref, carry_ref = refs[6:]
    epg = n_experts // n_groups
    tm = x_ref.shape[0] // ROUTER_SPLIT

    @pl.when(pl.program_id(0) == 0)
    def _():
        carry_ref[...] = jnp.zeros(carry_ref.shape, F32)

    x1_parts = []
    for part in range(ROUTER_SPLIT):
        rows = slice(part * tm, (part + 1) * tm)
        a = a_ref[rows, :]
        if gated:
            gate = gate_ref[rows, :]
            a = gate * jax.nn.sigmoid(gate) * a
        x1_parts.append(x_ref[rows, :] + jnp.dot(a, w_ref[...], preferred_element_type=F32))
    for part in range(ROUTER_SPLIT):
        rows = slice(part * tm, (part + 1) * tm)
        _route_rows(x1_parts[part], rows, nf_ref, wr_ref, br_ref, x1_ref, hn_ref, route_ref, route_t_ref,
                    cnt_ref, carry_ref, n_groups=n_groups, n_experts=n_experts, epg=epg)


def _route_rows(x1, rows, nf_ref, wr_ref, br_ref, x1_ref, hn_ref, route_ref, route_t_ref, cnt_ref,
                carry_ref, *, n_groups, n_experts, epg):
    tm = x1.shape[0]
    x1_ref[rows, :] = x1
    hn = x1 * lax.rsqrt(jnp.mean(x1 * x1, axis=-1, keepdims=True) + RMS_EPS) * nf_ref[...]
    hn_ref[rows, :] = _pack_bf16_pairs(hn)

    hn_hi = hn.astype(BF16)
    hn_lo = (hn - hn_hi.astype(F32)).astype(BF16)
    both = jnp.dot(hn_hi, wr_ref[...], preferred_element_type=F32)
    logits = (both[:, :LANES] + both[:, LANES:]
              + jnp.dot(hn_lo, wr_ref[:, :LANES], preferred_element_type=F32)) + br_ref[...]

    n_rows = -(-(n_groups + n_experts) // SUBLANES) * SUBLANES
    lt = logits.T[:n_rows, :]
    row = lax.broadcasted_iota(jnp.int32, lt.shape, 0)
    big = jnp.int32(LANES)
    neg = jnp.float32(-jnp.inf)

    gl = jnp.where(row < n_groups, lt, neg)
    gmax = jnp.max(gl, axis=0, keepdims=True)
    g_p = 1.0 / jnp.sum(jnp.exp(gl - gmax), axis=0, keepdims=True)
    g_idx = jnp.min(jnp.where(gl == gmax, row, big), axis=0, keepdims=True)

    lo = n_groups + g_idx * epg
    el = jnp.where((row >= lo) & (row < lo + epg), lt, neg)
    v1 = jnp.max(el, axis=0, keepdims=True)
    i1 = jnp.min(jnp.where(el == v1, row, big), axis=0, keepdims=True)
    el2 = jnp.where(row == i1, neg, el)
    v2 = jnp.max(el2, axis=0, keepdims=True)
    i2 = jnp.min(jnp.where(el2 == v2, row, big), axis=0, keepdims=True)
    tt = jnp.exp(v2 - v1)
    w1 = g_p / (1.0 + tt)
    w2 = g_p * tt / (1.0 + tt)
    e1 = i1 - n_groups
    e2 = i2 - n_groups

    erow = lax.broadcasted_iota(jnp.int32, (n_experts, tm), 0)
    oh1 = erow == e1
    oh2 = erow == e2
    onehot = jnp.where(oh1 | oh2, 1.0, 0.0)
    before = lax.broadcasted_iota(jnp.int32, (tm, tm), 0) < lax.broadcasted_iota(jnp.int32, (tm, tm), 1)
    tri = jnp.where(before, 1.0, 0.0).astype(BF16)
    carry = carry_ref[...]
    base = jnp.dot(onehot.astype(BF16), tri, preferred_element_type=F32) + carry[:, 0:1]
    rank1 = jnp.sum(jnp.where(oh1, base, 0.0), axis=0, keepdims=True)
    rank2 = jnp.sum(jnp.where(oh2, base, 0.0), axis=0, keepdims=True)
    carry = carry + jnp.sum(onehot, axis=1, keepdims=True)
    carry_ref[...] = carry
    cnt_ref[...] = carry

    frow = lax.broadcasted_iota(jnp.int32, (LANES, tm), 0)
    out = jnp.zeros((LANES, tm), F32)
    for idx, val in enumerate((e1.astype(F32), e2.astype(F32), w1, w2, rank1, rank2)):
        out = jnp.where(frow == idx, val, out)
    route_t_ref[:, rows] = out[:route_t_ref.shape[0], :]
    route_ref[rows, :] = out.T


def _out_router(a, gate, w, x, nf, w_group, b_group, w_expert, b_expert):
    t, d = x.shape
    k = w.shape[0]
    n_groups = w_group.shape[1]
    n_experts = w_expert.shape[1]
    assert n_groups + n_experts <= LANES
    pad = LANES - n_groups - n_experts
    wr = jnp.pad(jnp.concatenate([w_group, w_expert], axis=1).astype(F32), ((0, 0), (0, pad)))
    wr_hi = wr.astype(BF16)
    wr = jnp.concatenate([wr_hi, (wr - wr_hi.astype(F32)).astype(BF16)], axis=1)
    br = jnp.pad(jnp.concatenate([b_group, b_expert]).astype(F32), (0, pad)).reshape(1, LANES)
    gated = gate is not None
    row = lambda i: (i, 0)
    fixed = lambda i: (0, 0)
    in_specs = [pl.BlockSpec((TM, k), row)]
    args = [a]
    if gated:
        gate_arr, gate_col = gate
        in_specs.append(pl.BlockSpec((TM, k), lambda i: (i, gate_col)))
        args.append(gate_arr)
    in_specs += [
        pl.BlockSpec((k, d), fixed),
        pl.BlockSpec((TM, d), row),
        pl.BlockSpec((1, d), fixed),
        pl.BlockSpec((d, 2 * LANES), fixed),
        pl.BlockSpec((1, LANES), fixed),
    ]
    args += [w, x, nf.reshape(1, d), wr, br]
    return pl.pallas_call(
        functools.partial(_out_router_kernel, gated=gated, n_groups=n_groups, n_experts=n_experts),
        grid=(t // TM,),
        in_specs=in_specs,
        out_specs=[
            pl.BlockSpec((TM, d), row),
            pl.BlockSpec((TM, d // 2), row),
            pl.BlockSpec((TM, LANES), row),
            pl.BlockSpec((SUBLANES, TM), lambda i: (0, i)),
            pl.BlockSpec((n_experts, LANES), fixed),
        ],
        out_shape=[
            jax.ShapeDtypeStruct((t, d), F32),
            jax.ShapeDtypeStruct((t, d // 2), jnp.uint32),
            jax.ShapeDtypeStruct((t, LANES), F32),
            jax.ShapeDtypeStruct((SUBLANES, t), F32),
            jax.ShapeDtypeStruct((n_experts, LANES), F32),
        ],
        scratch_shapes=[pltpu.VMEM((n_experts, LANES), F32)],
        compiler_params=_cparams(("arbitrary",)),
    )(*args)


def _sc_mesh():
    return plsc.VectorSubcoreMesh(core_axis_name="c", subcore_axis_name="s",
                                  num_cores=SC_CORES, num_subcores=SC_SUBCORES)


def _sc_worker():
    return lax.axis_index("s") * SC_CORES + lax.axis_index("c")


def _sc_scatter_rows(src, d1, d2, n_rows):
    t, d = src.shape
    workers = SC_CORES * SC_SUBCORES
    per_w = t // workers
    n_ch = per_w // SC_CHUNK
    assert t % (workers * SC_CHUNK) == 0
    idx = jnp.stack([d1, d2]).reshape(2, workers, n_ch, SC_CHUNK)

    @functools.partial(
        pl.kernel, mesh=_sc_mesh(),
        out_type=jax.ShapeDtypeStruct((n_rows, d), src.dtype),
        scratch_types=[pltpu.VMEM((2, n_ch, SC_CHUNK), jnp.int32),
                       pltpu.VMEM((SC_CHUNK, d), src.dtype),
                       pltpu.VMEM((SC_CHUNK, d), src.dtype),
                       pltpu.SemaphoreType.DMA, pltpu.SemaphoreType.DMA],
    )
    def scatter(src_hbm, idx_hbm, out_hbm, idx_v, rows_a, rows_b, sem_a, sem_b):
        wid = _sc_worker()
        base = wid * per_w
        for s in range(2):
            pltpu.sync_copy(idx_hbm.at[s, wid], idx_v.at[s])

        def move(c, rows_v, sem):
            pltpu.sync_copy(src_hbm.at[pl.ds(base + c * SC_CHUNK, SC_CHUNK)], rows_v)
            return [pltpu.async_copy(rows_v, out_hbm.at[idx_v.at[s, c]], sem) for s in range(2)]

        @pl.loop(0, n_ch, step=2)
        def _(c):
            first = move(c, rows_a, sem_a)
            second = move(c + 1, rows_b, sem_b)
            for cp in first + second:
                cp.wait()

    return scatter(src, idx)


def _sc_gather_rows(table, idx):
    b = idx.shape[0]
    d = table.shape[1]
    workers = SC_CORES * SC_SUBCORES
    per_w = b // workers
    n_ch = per_w // SC_CHUNK
    assert b % (workers * SC_CHUNK * 2) == 0

    @functools.partial(
        pl.kernel, mesh=_sc_mesh(),
        out_type=jax.ShapeDtypeStruct((b, d), table.dtype),
        scratch_types=[pltpu.VMEM((per_w,), jnp.int32),
                       pltpu.VMEM((SC_CHUNK, d), table.dtype),
                       pltpu.VMEM((SC_CHUNK, d), table.dtype),
                       pltpu.SemaphoreType.DMA, pltpu.SemaphoreType.DMA],
    )
    def gather(table_hbm, idx_hbm, out_hbm, idx_v, rows_a, rows_b, sem_a, sem_b):
        base = _sc_worker() * per_w
        pltpu.sync_copy(idx_hbm.at[pl.ds(base, per_w)], idx_v)

        def fetch(c, rows_v, sem):
            return pltpu.async_copy(table_hbm.at[idx_v.at[pl.ds(c * SC_CHUNK, SC_CHUNK)]], rows_v, sem)

        def store(c, rows_v):
            pltpu.sync_copy(rows_v, out_hbm.at[pl.ds(base + c * SC_CHUNK, SC_CHUNK)])

        @pl.loop(0, n_ch, step=2)
        def _(c):
            first = fetch(c, rows_a, sem_a)
            second = fetch(c + 1, rows_b, sem_b)
            first.wait()
            store(c, rows_a)
            second.wait()
            store(c + 1, rows_b)

    return gather(table, idx)


def _expert_kernel(fb_ref, nblk_ref, nv_ref, nu_ref, xs_ref, wg_hbm, wu_hbm, wd_hbm, ys_ref,
                   wgb_ref, wub_ref, wdb_ref, wg_stage, wu_stage, wd_stage, xbuf, ybuf, wsem, xsem, ysem,
                   *, layer):
    e = pl.program_id(0)
    n_experts = pl.num_programs(0)
    rb = xbuf.shape[1]
    nb = ys_ref.shape[0] // rb
    n_used = nu_ref[0]
    stages = ((wg_hbm, wg_stage), (wu_hbm, wu_stage), (wd_hbm, wd_stage))

    def w_copies(ex, slot):
        return [pltpu.make_async_copy(hbm.at[layer, ex], stage.at[slot], wsem.at[slot, i])
                for i, (hbm, stage) in enumerate(stages)]

    def x_copy(g, slot):
        rows = pl.ds(pl.multiple_of(g * rb, rb), rb)
        return pltpu.make_async_copy(xs_ref.at[rows], xbuf.at[slot], xsem.at[slot])

    def y_copy(g, slot):
        rows = pl.ds(pl.multiple_of(g * rb, rb), rb)
        return pltpu.make_async_copy(ybuf.at[slot], ys_ref.at[rows], ysem.at[slot])

    wslot = lax.rem(e, 2)

    @pl.when(e == 0)
    def _():
        x_copy(0, 0).start()
        for cp in w_copies(0, 0):
            cp.start()

    @pl.when(e + 1 < n_experts)
    def _():
        for cp in w_copies(e + 1, 1 - wslot):
            cp.start()

    for cp in w_copies(e, wslot):
        cp.wait()

    @pl.when(nblk_ref[e] > 0)
    def _():
        wgb_ref[...] = wg_stage[wslot].astype(BF16)
        wub_ref[...] = wu_stage[wslot].astype(BF16)
        wdb_ref[...] = wd_stage[wslot].astype(BF16)

    def block(b, carry):
        g = fb_ref[e] + b
        slot = lax.rem(g, 2)
        x_copy(g, slot).wait()

        @pl.when(g + 1 < n_used)
        def _():
            x_copy(g + 1, 1 - slot).start()

        @pl.when(g >= 2)
        def _():
            y_copy(g - 2, slot).wait()

        nv = nv_ref[g]

        def ffn(rows):
            valid = lax.broadcasted_iota(jnp.int32, (rows, 1), 0) < nv
            x_hi, x_lo = _unpack_bf16_pairs(jnp.where(valid, xbuf[slot, :rows, :], jnp.uint32(0)))
            x = jnp.concatenate([x_hi.astype(BF16), x_lo.astype(BF16)], axis=1)
            gate = jnp.dot(x, wgb_ref[...], preferred_element_type=F32)
            up = jnp.dot(x, wub_ref[...], preferred_element_type=F32)
            hmid = (gate * jax.nn.sigmoid(gate) * up).astype(BF16)
            ybuf[slot, :rows, :] = _pack_bf16_pairs(jnp.dot(hmid, wdb_ref[...], preferred_element_type=F32))
            if rows < rb:
                ybuf[slot, rows:, :] = jnp.zeros((rb - rows, ybuf.shape[2]), ybuf.dtype)

        @pl.when(nv > rb // 2)
        def _():
            ffn(rb)

        @pl.when(nv <= rb // 2)
        def _():
            ffn(rb // 2)

        y_copy(g, slot).start()
        return carry

    lax.fori_loop(0, nblk_ref[e], block, 0)

    @pl.when(e == pl.num_programs(0) - 1)
    def _():
        @pl.when(n_used >= 2)
        def _():
            y_copy(n_used - 2, lax.rem(n_used, 2)).wait()

        y_copy(n_used - 1, lax.rem(n_used - 1, 2)).wait()
        ybuf[0] = jnp.zeros(ybuf.shape[1:], ybuf.dtype)

        def tail_start(g, carry):
            y_copy(g, 0).start()
            return carry

        def tail_wait(g, carry):
            y_copy(g, 0).wait()
            return carry

        lax.fori_loop(n_used, nb, tail_start, 0)
        lax.fori_loop(n_used, nb, tail_wait, 0)


def _experts(xs, first_block, n_blocks, n_valid, n_used, w_gate, w_up, w_down, layer):
    n_rows, dp = xs.shape
    d = 2 * dp
    n_experts, ff = w_gate.shape[1], w_gate.shape[3]
    grid_spec = pltpu.PrefetchScalarGridSpec(
        num_scalar_prefetch=4,
        grid=(n_experts,),
        in_specs=[pl.BlockSpec(memory_space=pl.ANY)] * 4,
        out_specs=pl.BlockSpec(memory_space=pl.ANY),
        scratch_shapes=[pltpu.VMEM((d, ff), BF16), pltpu.VMEM((d, ff), BF16), pltpu.VMEM((ff, d), BF16),
                        pltpu.VMEM((2, d, ff), F32), pltpu.VMEM((2, d, ff), F32), pltpu.VMEM((2, ff, d), F32),
                        pltpu.VMEM((2, ROW_BLOCK, dp), xs.dtype), pltpu.VMEM((2, ROW_BLOCK, dp), xs.dtype),
                        pltpu.SemaphoreType.DMA((2, 3)),
                        pltpu.SemaphoreType.DMA((2,)), pltpu.SemaphoreType.DMA((2,))],
    )
    return pl.pallas_call(
        functools.partial(_expert_kernel, layer=layer),
        grid_spec=grid_spec,
        out_shape=jax.ShapeDtypeStruct((n_rows, dp), xs.dtype),
        compiler_params=_cparams(("arbitrary",)),
    )(first_block, n_blocks, n_valid, n_used, xs, w_gate, w_up, w_down)


def _final_combine_kernel(x_ref, route_ref, y1_ref, y2_ref, nf_ref, o_ref):
    route = route_ref[...]
    y1 = jnp.concatenate(_unpack_bf16_pairs(y1_ref[...]), axis=1)
    y2 = jnp.concatenate(_unpack_bf16_pairs(y2_ref[...]), axis=1)
    y = x_ref[...] + (route[:, 2:3] * y1 + route[:, 3:4] * y2)
    o_ref[...] = y * lax.rsqrt(jnp.mean(y * y, axis=-1, keepdims=True) + RMS_EPS) * nf_ref[...]


def _final_combine(x1, route, yg, nf):
    t, d = x1.shape
    slot2 = t // TM
    row = lambda i: (i, 0)
    return pl.pallas_call(
        _final_combine_kernel,
        grid=(t // TM,),
        in_specs=[
            pl.BlockSpec((TM, d), row),
            pl.BlockSpec((TM, LANES), row),
            pl.BlockSpec((TM, d // 2), row),
            pl.BlockSpec((TM, d // 2), lambda i: (slot2 + i, 0)),
            pl.BlockSpec((1, d), lambda i: (0, 0)),
        ],
        out_specs=pl.BlockSpec((TM, d), row),
        out_shape=jax.ShapeDtypeStruct((t, d), F32),
        compiler_params=_cparams(("parallel",)),
    )(x1, route, yg, yg, nf.reshape(1, d).astype(F32))


def _moe(hn, route_t, cnt, w_gate, w_up, w_down, layer):
    t, d = hn.shape
    n_experts = w_gate.shape[1]
    ids = jnp.arange(n_experts, dtype=jnp.int32)
    e1, e2, rank1, rank2 = (route_t[f].astype(jnp.int32) for f in (0, 1, 4, 5))
    counts = cnt[:, 0].astype(jnp.int32)
    padded = ((counts + ROW_BLOCK - 1) // ROW_BLOCK) * ROW_BLOCK
    pad_end = jnp.cumsum(padded)
    pad_start = pad_end - padded

    def dest(e, rank):
        return jnp.sum(jnp.where(e[:, None] == ids[None, :], pad_start[None, :], 0), axis=1) + rank

    d1 = dest(e1, rank1)
    d2 = dest(e2, rank2)
    nb = (t * TOP_K_INNER) // ROW_BLOCK + n_experts
    n_used = (pad_end[-1:] // ROW_BLOCK).astype(jnp.int32)
    first_row = jnp.arange(nb, dtype=jnp.int32) * ROW_BLOCK
    block_e = jnp.sum((pad_end[None, :] <= first_row[:, None]).astype(jnp.int32), axis=1)
    block_e = jnp.minimum(block_e, n_experts - 1)
    row_end = jnp.sum(jnp.where(block_e[:, None] == ids[None, :], (pad_start + counts)[None, :], 0), axis=1)
    n_valid = jnp.clip(row_end - first_row, 0, ROW_BLOCK).astype(jnp.int32)
    xs = _sc_scatter_rows(hn, d1, d2, nb * ROW_BLOCK)
    ys = _experts(xs, (pad_start // ROW_BLOCK).astype(jnp.int32), (padded // ROW_BLOCK).astype(jnp.int32),
                  n_valid, n_used, w_gate, w_up, w_down, layer)
    return _sc_gather_rows(ys, jnp.concatenate([d1, d2]))


def _diff_lambda_init(layer_idx):
    return 0.8 - 0.6 * math.exp(-0.3 * layer_idx)


def _rope_tables(seq, dk):
    angle = 1.0 / (10000.0 ** jnp.linspace(0.0, 1.0, dk // 2, dtype=F32))
    angle = jnp.repeat(angle, 2)
    phase = jnp.arange(seq, dtype=F32)[:, None] * angle[None, :]
    sin, cos = jnp.sin(phase), jnp.cos(phase)
    even = (jnp.arange(dk) % 2 == 0)[None, :]
    return cos, jnp.where(even, -sin, 0.0), jnp.where(even, 0.0, sin)


def kernel(x, rel_bias_table, norm_mix, norm_ffn, norm_final, diff_w_in, diff_lambda_qk, diff_subln,
           diff_w_out, ret_w_in, ret_w_out, moe_w_group, moe_b_group, moe_w_expert, moe_b_expert,
           moe_w_gate, moe_w_up, moe_w_down):
    batch, seq, d = x.shape
    depth = norm_mix.shape[0]
    t = batch * seq
    heads = rel_bias_table.shape[1]
    dh = diff_lambda_qk.shape[-1]
    qk_dim = (ret_w_in.shape[2] - 2 * ret_w_out.shape[1]) // 2
    v_dim = ret_w_out.shape[1]
    dk = qk_dim // RET_HEADS
    xt = x.reshape(t, d)
    moe = None
    for i in range(depth):
        j = i // N_MIXERS
        if i % N_MIXERS == 0:
            col_scale = jnp.concatenate([jnp.full((d,), dh ** -0.5 * LOG2E, F32), jnp.ones((2 * d,), F32)])
            qkv = _norm_matmul(xt, norm_mix[i], diff_w_in, j, col_scale, moe=moe)
            if moe is not None:
                qkv, xt = qkv
            mix = _diff_attention(qkv, rel_bias_table, diff_lambda_qk[j], diff_subln[j],
                                  _diff_lambda_init(i), batch, seq, heads)
            gate = None
            w_out = diff_w_out[j].astype(BF16)
        else:
            n_in = ret_w_in.shape[2]
            col_scale = jnp.concatenate([jnp.ones((qk_dim,), F32), jnp.full((qk_dim,), dk ** -0.5, F32),
                                         jnp.ones((n_in - 2 * qk_dim,), F32)])
            qkvg = _norm_matmul(xt, norm_mix[i], ret_w_in, j, col_scale, rope=_rope_tables(seq, dk),
                                n_rope=2 * qk_dim // TN, seq=seq, moe=moe)
            if moe is not None:
                qkvg, xt = qkvg
            mix = _retention(qkvg, batch, seq, qk_dim, v_dim)
            gate = (qkvg, (2 * qk_dim + v_dim) // v_dim)
            w_out = ret_w_out[j].astype(BF16)
        x1, hn, route, route_t, cnt = _out_router(mix, gate, w_out, xt, norm_ffn[i], moe_w_group[i],
                                                  moe_b_group[i], moe_w_expert[i], moe_b_expert[i])
        xt = x1
        moe = (route, _moe(hn, route_t, cnt, moe_w_gate, moe_w_up, moe_w_down, i))
    return _final_combine(xt, moe[0], moe[1], norm_final).reshape(batch, seq, d)
```

```python
import functools
import math

import jax
import jax.numpy as jnp
from jax import lax
from jax.experimental import pallas as pl
from jax.experimental.pallas import tpu as pltpu
from jax.experimental.pallas import tpu_sc as plsc

F32 = jnp.float32
BF16 = jnp.bfloat16

N_MIXERS = 2
NUM_BUCKETS = 32
MAX_DISTANCE = 128
RET_HEADS = 4
TOP_K_INNER = 2
RMS_EPS = 1e-6
SUBLN_EPS = 1e-5

LANES = 128
SUBLANES = 8
BF16_SUBLANES = 16
LOG2E = 1.4426950408889634
VMEM_LIMIT = 56 * 1024 * 1024
SC_CORES = 2
SC_SUBCORES = 16

TM = 512
TN = 1024
TQ = 256
ATTN_HEADS = 4
RET_C = 256
RET_HP = 2
ROUTER_SPLIT = 2
ROW_BLOCK = 512
SC_CHUNK = 64


def _cparams(sem):
    return pltpu.CompilerParams(dimension_semantics=sem, vmem_limit_bytes=VMEM_LIMIT)


def _pack_bf16_pairs(x):
    h = x.shape[1] // 2
    hi = lax.bitcast_convert_type(x[:, :h].astype(BF16).astype(F32), jnp.uint32)
    lo = lax.bitcast_convert_type(x[:, h:].astype(BF16).astype(F32), jnp.uint32)
    return hi | (lo >> 16)


def _unpack_bf16_pairs(w):
    hi = lax.bitcast_convert_type(w & jnp.uint32(0xFFFF0000), F32)
    lo = lax.bitcast_convert_type(w << 16, F32)
    return hi, lo


def _norm_matmul_kernel(x_ref, *rest, n_rope, moe, layer):
    rest = list(rest)
    w_ref, stage_ref, sem = rest[-3:]
    rest = rest[:-3]
    if moe:
        route_ref, y1_ref, y2_ref = rest[:3]
        x_out_ref = rest.pop()
        rest = rest[3:]
    g_ref, scale_ref, w_hbm = rest[:3]
    if n_rope:
        cos_ref, sa_ref, sb_ref, o_ref = rest[3:]
    else:
        (o_ref,) = rest[3:]
    n_chunks = o_ref.shape[1] // TN

    @pl.when(pl.program_id(0) == 0)
    def _():
        def fetch(j):
            return pltpu.make_async_copy(w_hbm.at[layer, :, pl.ds(j * TN, TN)], stage_ref.at[j % 2],
                                         sem.at[j % 2])

        fetch(0).start()
        for j in range(n_chunks):
            if j + 1 < n_chunks:
                fetch(j + 1).start()
            fetch(j).wait()
            cols = slice(j * TN, (j + 1) * TN)
            w_ref[:, cols] = (stage_ref[j % 2] * scale_ref[:, cols]).astype(BF16)

    x = x_ref[...]
    if moe:
        route = route_ref[...]
        y1 = jnp.concatenate(_unpack_bf16_pairs(y1_ref[...]), axis=1)
        y2 = jnp.concatenate(_unpack_bf16_pairs(y2_ref[...]), axis=1)
        x = x + (route[:, 2:3] * y1 + route[:, 3:4] * y2)
        x_out_ref[...] = x
    xn = (x * lax.rsqrt(jnp.mean(x * x, axis=-1, keepdims=True) + RMS_EPS) * g_ref[...]).astype(BF16)
    for j in range(n_chunks):
        cols = slice(j * TN, (j + 1) * TN)
        acc = jnp.dot(xn, w_ref[:, cols], preferred_element_type=F32)
        if j < n_rope:
            reps = TN // cos_ref.shape[1]
            c = jnp.tile(cos_ref[...], (1, reps))
            sa = jnp.tile(sa_ref[...], (1, reps))
            sb = jnp.tile(sb_ref[...], (1, reps))
            nxt = pltpu.roll(acc, TN - 1, axis=1)
            prv = pltpu.roll(acc, 1, axis=1)
            acc = acc * c + nxt * sa + prv * sb
        o_ref[:, cols] = acc.astype(o_ref.dtype)


def _norm_matmul(x, g, w, layer, col_scale, rope=None, n_rope=0, seq=None, moe=None):
    t, d = x.shape
    n = w.shape[2]
    row = lambda i: (i, 0)
    in_specs = [pl.BlockSpec((TM, d), row)]
    args = [x]
    if moe is not None:
        route, yg = moe
        slot2 = t // TM
        in_specs += [pl.BlockSpec((TM, LANES), row), pl.BlockSpec((TM, d // 2), row),
                     pl.BlockSpec((TM, d // 2), lambda i: (slot2 + i, 0))]
        args += [route, yg, yg]
    in_specs += [
        pl.BlockSpec((1, d), lambda i: (0, 0)),
        pl.BlockSpec((1, n), lambda i: (0, 0)),
        pl.BlockSpec(memory_space=pl.ANY),
    ]
    args += [g.reshape(1, d), col_scale.reshape(1, n).astype(F32), w]
    if rope is not None:
        nb = seq // TM
        dk = rope[0].shape[1]
        in_specs += [pl.BlockSpec((TM, dk), lambda i: (i % nb, 0))] * 3
        args += list(rope)
    out_specs = [pl.BlockSpec((TM, n), row)]
    out_shape = [jax.ShapeDtypeStruct((t, n), BF16)]
    if moe is not None:
        out_specs.append(pl.BlockSpec((TM, d), row))
        out_shape.append(jax.ShapeDtypeStruct((t, d), F32))
    out = pl.pallas_call(
        functools.partial(_norm_matmul_kernel, n_rope=n_rope, moe=moe is not None, layer=layer),
        grid=(t // TM,),
        in_specs=in_specs,
        out_specs=out_specs,
        out_shape=out_shape,
        scratch_shapes=[pltpu.VMEM((d, n), BF16), pltpu.VMEM((2, d, TN), F32), pltpu.SemaphoreType.DMA((2,))],
        compiler_params=_cparams(("arbitrary",)),
    )(*args)
    return out if moe is not None else out[0]


def _diff_attn_kernel(q_ref, k_ref, v_ref, bias_ref, lam_ref, g_ref, o_ref,
                      vt_ref, m_ref, acc_ref, s_ref, *, lambda_init):
    seq = q_ref.shape[0]
    hp, _, _, tq = bias_ref.shape
    dv = q_ref.shape[1] // hp
    dh = dv // 2
    nq = seq // tq

    def head(hh):
        return slice(hh * dv, (hh + 1) * dv)

    extra = vt_ref.shape[2] - dv
    ones_row = jnp.where(lax.broadcasted_iota(jnp.int32, (extra, tq), 0) == 0, 1.0, 0.0).astype(BF16)
    for hh in range(hp):
        for j in range(nq):
            vt_ref[hh, j, :dv, :] = v_ref[j * tq:(j + 1) * tq, head(hh)].astype(F32).T.astype(BF16)
            vt_ref[hh, j, dv:, :] = ones_row

    lq = lam_ref[...]
    lam = (jnp.exp(jnp.sum(lq[0:1] * lq[1:2], axis=-1, keepdims=True))
           - jnp.exp(jnp.sum(lq[2:3] * lq[3:4], axis=-1, keepdims=True)) + lambda_init)
    lane = lax.broadcasted_iota(jnp.int32, (tq, dv), 1)

    def query_maps(qi):
        qrows = pl.ds(pl.multiple_of(qi * tq, tq), tq)
        qs = []
        for hh in range(hp):
            q = q_ref[qrows, head(hh)]
            zero = jnp.zeros_like(q)
            qs.append(jnp.concatenate([jnp.where(lane < dh, q, zero), jnp.where(lane >= dh, q, zero)],
                                      axis=0))
        return qs

    def score_dots(j, qs):
        rows = pl.ds(pl.multiple_of(j * tq, tq), tq)
        for hh in range(hp):
            s_ref[hh] = lax.dot_general(k_ref[rows, head(hh)], qs[hh], (((1,), (1,)), ((), ())),
                                        preferred_element_type=F32)

    score_dots(0, query_maps(0))

    def q_body(qi, carry):
        qrows = pl.ds(pl.multiple_of(qi * tq, tq), tq)
        qs = query_maps(qi)
        qs_next = query_maps(jnp.minimum(qi + 1, nq - 1))

        m_ref[...] = jnp.full(m_ref.shape, -jnp.inf, F32)
        acc_ref[...] = jnp.zeros(acc_ref.shape, F32)

        def step(j, tile, prefetch):
            probs, alphas = [], []
            for hh in range(hp):
                s = s_ref[hh]
                if tile is not None:
                    bias = bias_ref[hh, tile]
                    s = s + jnp.concatenate([bias, bias], axis=1)
                m_old = m_ref[hh]
                m_new = jnp.maximum(m_old, jnp.max(s, axis=0, keepdims=True))
                alpha = jnp.exp2(m_old - m_new)
                m_ref[hh] = m_new
                probs.append(jnp.exp2(s - m_new).astype(BF16))
                alphas.append(alpha)
            if prefetch == "block":
                score_dots(j + 1, qs)
            else:
                score_dots(0, qs_next)
            for hh in range(hp):
                acc_ref[hh] = alphas[hh] * acc_ref[hh] + jnp.dot(vt_ref[hh, j], probs[hh],
                                                                 preferred_element_type=F32)

        def far_body(j, c):
            step(j, None, "block")
            return c

        lax.fori_loop(0, jnp.maximum(qi - 1, 0), far_body, 0)

        @pl.when(qi >= 1)
        def _():
            step(qi - 1, 1, "block")

        step(qi, 0, "tile")

        for hh in range(hp):
            acc = acc_ref[hh]
            on = acc[:dv] / acc[dv:dv + 1]
            ot = on[:, :tq] - lam * on[:, tq:]
            ot = ot * lax.rsqrt(jnp.mean(ot * ot, axis=0, keepdims=True) + SUBLN_EPS)
            o_ref[qrows, head(hh)] = (ot.T * g_ref[...] * (1.0 - lambda_init)).astype(o_ref.dtype)
        return carry

    lax.fori_loop(0, nq, q_body, 0)


def _t5_bucket(rel):
    max_exact = NUM_BUCKETS // 2
    n = jnp.maximum(rel, 0)
    is_small = n < max_exact
    nf = jnp.maximum(n, max_exact).astype(F32)
    large = max_exact + (jnp.log(nf / max_exact) / math.log(MAX_DISTANCE / max_exact)
                         * (NUM_BUCKETS - max_exact)).astype(jnp.int32)
    large = jnp.minimum(large, NUM_BUCKETS - 1)
    return jnp.where(is_small, n, large)


def _bias_tiles(rel_table):
    heads = rel_table.shape[1]
    span = 2 * TQ + 1
    bucket = _t5_bucket(jnp.arange(span))
    onehot = (bucket[:, None] == jnp.arange(NUM_BUCKETS)[None, :]).astype(F32)
    table = (rel_table.astype(F32) - rel_table[NUM_BUCKETS - 1].astype(F32)[None, :]) * LOG2E
    by_rel = jnp.dot(onehot, table, precision=lax.Precision.HIGHEST).T
    toep = jnp.tile(by_rel, (1, TQ))[:, :TQ * (span - 1)].reshape(heads, TQ, span - 1)
    kj = jnp.arange(TQ)[:, None]
    qi = jnp.arange(TQ)[None, :]
    diag = jnp.where((qi >= kj)[None], toep[:, :, :TQ], -jnp.inf)
    return jnp.stack([diag, toep[:, :, TQ:]], axis=1)


def _diff_attention(qkv, rel_table, lam_qk, subln_g, lambda_init, batch, seq, heads):
    t = qkv.shape[0]
    d = qkv.shape[1] // 3
    dv = d // heads
    hp = ATTN_HEADS
    assert TQ + 1 >= MAX_DISTANCE and seq % TQ == 0 and dv == LANES and heads % hp == 0
    nq = seq // TQ
    groups = heads // hp
    bias = _bias_tiles(rel_table)
    return pl.pallas_call(
        functools.partial(_diff_attn_kernel, lambda_init=lambda_init),
        grid=(batch, groups),
        in_specs=[
            pl.BlockSpec((seq, hp * dv), lambda b, g: (b, g)),
            pl.BlockSpec((seq, hp * dv), lambda b, g: (b, groups + g)),
            pl.BlockSpec((seq, hp * dv), lambda b, g: (b, 2 * groups + g)),
            pl.BlockSpec((hp, 2, TQ, TQ), lambda b, g: (g, 0, 0, 0)),
            pl.BlockSpec(lam_qk.shape, lambda b, g: (0, 0)),
            pl.BlockSpec((1, dv), lambda b, g: (0, 0)),
        ],
        out_specs=pl.BlockSpec((seq, hp * dv), lambda b, g: (b, g)),
        out_shape=jax.ShapeDtypeStruct((t, d), BF16),
        scratch_shapes=[
            pltpu.VMEM((hp, nq, dv + BF16_SUBLANES, TQ), BF16),
            pltpu.VMEM((hp, 1, 2 * TQ), F32),
            pltpu.VMEM((hp, dv + BF16_SUBLANES, 2 * TQ), F32),
            pltpu.VMEM((hp, TQ, 2 * TQ), F32),
        ],
        compiler_params=_cparams(("parallel", "parallel")),
    )(qkv, qkv, qkv, bias, lam_qk.astype(F32), subln_g.reshape(1, dv).astype(F32))


def _retention_kernel(q_ref, k_ref, v_ref, dec_ref, tab_ref, o_ref, state_ref):
    seq = q_ref.shape[0]
    hp, c, _ = dec_ref.shape
    dk = q_ref.shape[1] // hp
    dv = v_ref.shape[1] // hp
    heads = range(hp)
    state_ref[...] = jnp.zeros(state_ref.shape, F32)

    def body(n, carry):
        rows = pl.ds(pl.multiple_of(n * c, c), c)
        q = [q_ref[rows, hh * dk:(hh + 1) * dk] for hh in heads]
        k = [k_ref[rows, hh * dk:(hh + 1) * dk] for hh in heads]
        v = [v_ref[rows, hh * dv:(hh + 1) * dv] for hh in heads]
        tab = [tab_ref[hh] for hh in heads]
        st = [state_ref[hh] for hh in heads]
        s = [lax.dot_general(q[hh], k[hh], (((1,), (1,)), ((), ())), preferred_element_type=F32)
             for hh in heads]
        cross = [jnp.dot(q[hh], st[hh].astype(BF16), preferred_element_type=F32) for hh in heads]
        kt = [(k[hh].astype(F32) * tab[hh][:, 1:2]).T.astype(BF16) for hh in heads]
        inner = [jnp.dot((s[hh] * dec_ref[hh]).astype(BF16), v[hh], preferred_element_type=F32)
                 for hh in heads]
        for hh in heads:
            state_ref[hh] = st[hh] * tab[hh][0:1, 2:3] + jnp.dot(kt[hh], v[hh], preferred_element_type=F32)
        for hh in heads:
            o = inner[hh] + cross[hh] * tab[hh][:, 0:1]
            o = o * lax.rsqrt(jnp.mean(o * o, axis=-1, keepdims=True) + RMS_EPS)
            o_ref[rows, hh * dv:(hh + 1) * dv] = o.astype(o_ref.dtype)
        return carry

    lax.fori_loop(0, seq // c, body, 0)


def _retention_tables(heads):
    c = RET_C
    log_gamma = jnp.log(1.0 - 2.0 ** (-5.0 - jnp.arange(heads, dtype=F32)))
    pos = jnp.arange(c, dtype=F32)
    dist = pos[:, None] - pos[None, :]
    inner_decay = jnp.where(dist[None] >= 0,
                            jnp.exp(jnp.maximum(dist, 0.0)[None] * log_gamma[:, None, None]), 0.0)
    q_decay = jnp.exp((pos + 1.0)[None] * log_gamma[:, None])
    k_decay = jnp.exp((c - 1.0 - pos)[None] * log_gamma[:, None])
    chunk_decay = jnp.broadcast_to(jnp.exp(c * log_gamma)[:, None], (heads, c))
    tab = jnp.stack([q_decay, k_decay, chunk_decay], axis=-1)
    tab = jnp.pad(tab, ((0, 0), (0, 0), (0, LANES - 3)))
    return inner_decay, tab


def _retention(qkvg, batch, seq, qk_dim, v_dim):
    t = qkvg.shape[0]
    dk = qk_dim // RET_HEADS
    dvh = v_dim // RET_HEADS
    inner_decay, tab = _retention_tables(RET_HEADS)
    hp = RET_HP
    assert RET_HEADS % hp == 0
    groups = RET_HEADS // hp
    kb = qk_dim // (hp * dk)
    vb = 2 * qk_dim // (hp * dvh)
    return pl.pallas_call(
        _retention_kernel,
        grid=(batch, groups),
        in_specs=[
            pl.BlockSpec((seq, hp * dk), lambda b, g: (b, g)),
            pl.BlockSpec((seq, hp * dk), lambda b, g: (b, kb + g)),
            pl.BlockSpec((seq, hp * dvh), lambda b, g: (b, vb + g)),
            pl.BlockSpec((hp, RET_C, RET_C), lambda b, g: (g, 0, 0)),
            pl.BlockSpec((hp, RET_C, LANES), lambda b, g: (g, 0, 0)),
        ],
        out_specs=pl.BlockSpec((seq, hp * dvh), lambda b, g: (b, g)),
        out_shape=jax.ShapeDtypeStruct((t, v_dim), BF16),
        scratch_shapes=[pltpu.VMEM((hp, dk, dvh), F32)],
        compiler_params=_cparams(("parallel", "parallel")),
    )(qkvg, qkvg, qkvg, inner_decay, tab)


def _out_router_kernel(*refs, gated, n_groups, n_experts):
    if gated:
        a_ref, gate_ref, w_ref, x_ref, nf_ref, wr_ref, br_ref = refs[:7]
        x1_ref, hn_ref, route_ref, route_t_ref, cnt_ref, carry_ref = refs[7:]
    else:
        a_ref, w_ref, x_ref, nf_ref, wr_ref, br_ref = refs[:6]
        x1_ref, hn_ref, route_ref, route_t_ref, cnt_ref, carry_ref = refs[6:]
    epg = n_experts // n_groups
    tm = x_ref.shape[0] // ROUTER_SPLIT

    @pl.when(pl.program_id(0) == 0)
    def _():
        carry_ref[...] = jnp.zeros(carry_ref.shape, F32)

    x1_parts = []
    for part in range(ROUTER_SPLIT):
        rows = slice(part * tm, (part + 1) * tm)
        a = a_ref[rows, :]
        if gated:
            gate = gate_ref[rows, :]
            a = gate * jax.nn.sigmoid(gate) * a
        x1_parts.append(x_ref[rows, :] + jnp.dot(a, w_ref[...], preferred_element_type=F32))
    for part in range(ROUTER_SPLIT):
        rows = slice(part * tm, (part + 1) * tm)
        _route_rows(x1_parts[part], rows, nf_ref, wr_ref, br_ref, x1_ref, hn_ref, route_ref, route_t_ref,
                    cnt_ref, carry_ref, n_groups=n_groups, n_experts=n_experts, epg=epg)


def _route_rows(x1, rows, nf_ref, wr_ref, br_ref, x1_ref, hn_ref, route_ref, route_t_ref, cnt_ref,
                carry_ref, *, n_groups, n_experts, epg):
    tm = x1.shape[0]
    x1_ref[rows, :] = x1
    hn = x1 * lax.rsqrt(jnp.mean(x1 * x1, axis=-1, keepdims=True) + RMS_EPS) * nf_ref[...]
    hn_ref[rows, :] = _pack_bf16_pairs(hn)

    hn_hi = hn.astype(BF16)
    hn_lo = (hn - hn_hi.astype(F32)).astype(BF16)
    both = jnp.dot(hn_hi, wr_ref[...], preferred_element_type=F32)
    logits = (both[:, :LANES] + both[:, LANES:]
              + jnp.dot(hn_lo, wr_ref[:, :LANES], preferred_element_type=F32)) + br_ref[...]

    n_rows = -(-(n_groups + n_experts) // SUBLANES) * SUBLANES
    lt = logits.T[:n_rows, :]
    row = lax.broadcasted_iota(jnp.int32, lt.shape, 0)
    big = jnp.int32(LANES)
    neg = jnp.float32(-jnp.inf)

    gl = jnp.where(row < n_groups, lt, neg)
    gmax = jnp.max(gl, axis=0, keepdims=True)
    g_p = 1.0 / jnp.sum(jnp.exp(gl - gmax), axis=0, keepdims=True)
    g_idx = jnp.min(jnp.where(gl == gmax, row, big), axis=0, keepdims=True)

    lo = n_groups + g_idx * epg
    el = jnp.where((row >= lo) & (row < lo + epg), lt, neg)
    v1 = jnp.max(el, axis=0, keepdims=True)
    i1 = jnp.min(jnp.where(el == v1, row, big), axis=0, keepdims=True)
    el2 = jnp.where(row == i1, neg, el)
    v2 = jnp.max(el2, axis=0, keepdims=True)
    i2 = jnp.min(jnp.where(el2 == v2, row, big), axis=0, keepdims=True)
    tt = jnp.exp(v2 - v1)
    w1 = g_p / (1.0 + tt)
    w2 = g_p * tt / (1.0 + tt)
    e1 = i1 - n_groups
    e2 = i2 - n_groups

    erow = lax.broadcasted_iota(jnp.int32, (n_experts, tm), 0)
    oh1 = erow == e1
    oh2 = erow == e2
    onehot = jnp.where(oh1 | oh2, 1.0, 0.0)
    before = lax.broadcasted_iota(jnp.int32, (tm, tm), 0) < lax.broadcasted_iota(jnp.int32, (tm, tm), 1)
    tri = jnp.where(before, 1.0, 0.0).astype(BF16)
    carry = carry_ref[...]
    base = jnp.dot(onehot.astype(BF16), tri, preferred_element_type=F32) + carry[:, 0:1]
    rank1 = jnp.sum(jnp.where(oh1, base, 0.0), axis=0, keepdims=True)
    rank2 = jnp.sum(jnp.where(oh2, base, 0.0), axis=0, keepdims=True)
    carry = carry + jnp.sum(onehot, axis=1, keepdims=True)
    carry_ref[...] = carry
    cnt_ref[...] = carry

    frow = lax.broadcasted_iota(jnp.int32, (LANES, tm), 0)
    out = jnp.zeros((LANES, tm), F32)
    for idx, val in enumerate((e1.astype(F32), e2.astype(F32), w1, w2, rank1, rank2)):
        out = jnp.where(frow == idx, val, out)
    route_t_ref[:, rows] = out[:route_t_ref.shape[0], :]
    route_ref[rows, :] = out.T


def _out_router(a, gate, w, x, nf, w_group, b_group, w_expert, b_expert):
    t, d = x.shape
    k = w.shape[0]
    n_groups = w_group.shape[1]
    n_experts = w_expert.shape[1]
    assert n_groups + n_experts <= LANES
    pad = LANES - n_groups - n_experts
    wr = jnp.pad(jnp.concatenate([w_group, w_expert], axis=1).astype(F32), ((0, 0), (0, pad)))
    wr_hi = wr.astype(BF16)
    wr = jnp.concatenate([wr_hi, (wr - wr_hi.astype(F32)).astype(BF16)], axis=1)
    br = jnp.pad(jnp.concatenate([b_group, b_expert]).astype(F32), (0, pad)).reshape(1, LANES)
    gated = gate is not None
    row = lambda i: (i, 0)
    fixed = lambda i: (0, 0)
    in_specs = [pl.BlockSpec((TM, k), row)]
    args = [a]
    if gated:
        gate_arr, gate_col = gate
        in_specs.append(pl.BlockSpec((TM, k), lambda i: (i, gate_col)))
        args.append(gate_arr)
    in_specs += [
        pl.BlockSpec((k, d), fixed),
        pl.BlockSpec((TM, d), row),
        pl.BlockSpec((1, d), fixed),
        pl.BlockSpec((d, 2 * LANES), fixed),
        pl.BlockSpec((1, LANES), fixed),
    ]
    args += [w, x, nf.reshape(1, d), wr, br]
    return pl.pallas_call(
        functools.partial(_out_router_kernel, gated=gated, n_groups=n_groups, n_experts=n_experts),
        grid=(t // TM,),
        in_specs=in_specs,
        out_specs=[
            pl.BlockSpec((TM, d), row),
            pl.BlockSpec((TM, d // 2), row),
            pl.BlockSpec((TM, LANES), row),
            pl.BlockSpec((SUBLANES, TM), lambda i: (0, i)),
            pl.BlockSpec((n_experts, LANES), fixed),
        ],
        out_shape=[
            jax.ShapeDtypeStruct((t, d), F32),
            jax.ShapeDtypeStruct((t, d // 2), jnp.uint32),
            jax.ShapeDtypeStruct((t, LANES), F32),
            jax.ShapeDtypeStruct((SUBLANES, t), F32),
            jax.ShapeDtypeStruct((n_experts, LANES), F32),
        ],
        scratch_shapes=[pltpu.VMEM((n_experts, LANES), F32)],
        compiler_params=_cparams(("arbitrary",)),
    )(*args)


def _sc_mesh():
    return plsc.VectorSubcoreMesh(core_axis_name="c", subcore_axis_name="s",
                                  num_cores=SC_CORES, num_subcores=SC_SUBCORES)


def _sc_worker():
    return lax.axis_index("s") * SC_CORES + lax.axis_index("c")


def _sc_scatter_rows(src, d1, d2, n_rows):
    t, d = src.shape
    workers = SC_CORES * SC_SUBCORES
    per_w = t // workers
    n_ch = per_w // SC_CHUNK
    assert t % (workers * SC_CHUNK) == 0
    idx = jnp.stack([d1, d2]).reshape(2, workers, n_ch, SC_CHUNK)

    @functools.partial(
        pl.kernel, mesh=_sc_mesh(),
        out_type=jax.ShapeDtypeStruct((n_rows, d), src.dtype),
        scratch_types=[pltpu.VMEM((2, n_ch, SC_CHUNK), jnp.int32),
                       pltpu.VMEM((SC_CHUNK, d), src.dtype),
                       pltpu.VMEM((SC_CHUNK, d), src.dtype),
                       pltpu.SemaphoreType.DMA, pltpu.SemaphoreType.DMA],
    )
    def scatter(src_hbm, idx_hbm, out_hbm, idx_v, rows_a, rows_b, sem_a, sem_b):
        wid = _sc_worker()
        base = wid * per_w
        for s in range(2):
            pltpu.sync_copy(idx_hbm.at[s, wid], idx_v.at[s])

        def move(c, rows_v, sem):
            pltpu.sync_copy(src_hbm.at[pl.ds(base + c * SC_CHUNK, SC_CHUNK)], rows_v)
            return [pltpu.async_copy(rows_v, out_hbm.at[idx_v.at[s, c]], sem) for s in range(2)]

        @pl.loop(0, n_ch, step=2)
        def _(c):
            first = move(c, rows_a, sem_a)
            second = move(c + 1, rows_b, sem_b)
            for cp in first + second:
                cp.wait()

    return scatter(src, idx)


def _sc_gather_rows(table, idx):
    b = idx.shape[0]
    d = table.shape[1]
    workers = SC_CORES * SC_SUBCORES
    per_w = b // workers
    n_ch = per_w // SC_CHUNK
    assert b % (workers * SC_CHUNK * 2) == 0

    @functools.partial(
        pl.kernel, mesh=_sc_mesh(),
        out_type=jax.ShapeDtypeStruct((b, d), table.dtype),
        scratch_types=[pltpu.VMEM((per_w,), jnp.int32),
                       pltpu.VMEM((SC_CHUNK, d), table.dtype),
                       pltpu.VMEM((SC_CHUNK, d), table.dtype),
                       pltpu.SemaphoreType.DMA, pltpu.SemaphoreType.DMA],
    )
    def gather(table_hbm, idx_hbm, out_hbm, idx_v, rows_a, rows_b, sem_a, sem_b):
        base = _sc_worker() * per_w
        pltpu.sync_copy(idx_hbm.at[pl.ds(base, per_w)], idx_v)

        def fetch(c, rows_v, sem):
            return pltpu.async_copy(table_hbm.at[idx_v.at[pl.ds(c * SC_CHUNK, SC_CHUNK)]], rows_v, sem)

        def store(c, rows_v):
            pltpu.sync_copy(rows_v, out_hbm.at[pl.ds(base + c * SC_CHUNK, SC_CHUNK)])

        @pl.loop(0, n_ch, step=2)
        def _(c):
            first = fetch(c, rows_a, sem_a)
            second = fetch(c + 1, rows_b, sem_b)
            first.wait()
            store(c, rows_a)
            second.wait()
            store(c + 1, rows_b)

    return gather(table, idx)


def _expert_kernel(fb_ref, nblk_ref, nv_ref, nu_ref, xs_ref, wg_ref, wu_ref, wd_ref, ys_ref,
                   wgb_ref, wub_ref, wdb_ref, xbuf, ybuf, xsem, ysem):
    e = pl.program_id(0)
    rb = xbuf.shape[1]
    nb = ys_ref.shape[0] // rb
    n_used = nu_ref[0]

    def x_copy(g, slot):
        rows = pl.ds(pl.multiple_of(g * rb, rb), rb)
        return pltpu.make_async_copy(xs_ref.at[rows], xbuf.at[slot], xsem.at[slot])

    def y_copy(g, slot):
        rows = pl.ds(pl.multiple_of(g * rb, rb), rb)
        return pltpu.make_async_copy(ybuf.at[slot], ys_ref.at[rows], ysem.at[slot])

    @pl.when(e == 0)
    def _():
        x_copy(0, 0).start()

    @pl.when(nblk_ref[e] > 0)
    def _():
        wgb_ref[...] = wg_ref[0, 0].astype(BF16)
        wub_ref[...] = wu_ref[0, 0].astype(BF16)
        wdb_ref[...] = wd_ref[0, 0].astype(BF16)

    def block(b, carry):
        g = fb_ref[e] + b
        slot = lax.rem(g, 2)
        x_copy(g, slot).wait()

        @pl.when(g + 1 < n_used)
        def _():
            x_copy(g + 1, 1 - slot).start()

        @pl.when(g >= 2)
        def _():
            y_copy(g - 2, slot).wait()

        nv = nv_ref[g]

        def ffn(rows):
            valid = lax.broadcasted_iota(jnp.int32, (rows, 1), 0) < nv
            x_hi, x_lo = _unpack_bf16_pairs(jnp.where(valid, xbuf[slot, :rows, :], jnp.uint32(0)))
            x = jnp.concatenate([x_hi.astype(BF16), x_lo.astype(BF16)], axis=1)
            gate = jnp.dot(x, wgb_ref[...], preferred_element_type=F32)
            up = jnp.dot(x, wub_ref[...], preferred_element_type=F32)
            hmid = (gate * jax.nn.sigmoid(gate) * up).astype(BF16)
            ybuf[slot, :rows, :] = _pack_bf16_pairs(jnp.dot(hmid, wdb_ref[...], preferred_element_type=F32))
            if rows < rb:
                ybuf[slot, rows:, :] = jnp.zeros((rb - rows, ybuf.shape[2]), ybuf.dtype)

        @pl.when(nv > rb // 2)
        def _():
            ffn(rb)

        @pl.when(nv <= rb // 2)
        def _():
            ffn(rb // 2)

        y_copy(g, slot).start()
        return carry

    lax.fori_loop(0, nblk_ref[e], block, 0)

    @pl.when(e == pl.num_programs(0) - 1)
    def _():
        @pl.when(n_used >= 2)
        def _():
            y_copy(n_used - 2, lax.rem(n_used, 2)).wait()

        y_copy(n_used - 1, lax.rem(n_used - 1, 2)).wait()
        ybuf[0] = jnp.zeros(ybuf.shape[1:], ybuf.dtype)

        def tail_start(g, carry):
            y_copy(g, 0).start()
            return carry

        def tail_wait(g, carry):
            y_copy(g, 0).wait()
            return carry

        lax.fori_loop(n_used, nb, tail_start, 0)
        lax.fori_loop(n_used, nb, tail_wait, 0)


def _experts(xs, first_block, n_blocks, n_valid, n_used, w_gate, w_up, w_down, layer):
    n_rows, dp = xs.shape
    d = 2 * dp
    n_experts, ff = w_gate.shape[1], w_gate.shape[3]
    wsel = lambda e, *_: (layer, e, 0, 0)
    grid_spec = pltpu.PrefetchScalarGridSpec(
        num_scalar_prefetch=4,
        grid=(n_experts,),
        in_specs=[
            pl.BlockSpec(memory_space=pl.ANY),
            pl.BlockSpec((1, 1, d, ff), wsel),
            pl.BlockSpec((1, 1, d, ff), wsel),
            pl.BlockSpec((1, 1, ff, d), wsel),
        ],
        out_specs=pl.BlockSpec(memory_space=pl.ANY),
        scratch_shapes=[pltpu.VMEM((d, ff), BF16), pltpu.VMEM((d, ff), BF16), pltpu.VMEM((ff, d), BF16),
                        pltpu.VMEM((2, ROW_BLOCK, dp), xs.dtype), pltpu.VMEM((2, ROW_BLOCK, dp), xs.dtype),
                        pltpu.SemaphoreType.DMA((2,)), pltpu.SemaphoreType.DMA((2,))],
    )
    return pl.pallas_call(
        _expert_kernel,
        grid_spec=grid_spec,
        out_shape=jax.ShapeDtypeStruct((n_rows, dp), xs.dtype),
        compiler_params=_cparams(("arbitrary",)),
    )(first_block, n_blocks, n_valid, n_used, xs, w_gate, w_up, w_down)


def _final_combine_kernel(x_ref, route_ref, y1_ref, y2_ref, nf_ref, o_ref):
    route = route_ref[...]
    y1 = jnp.concatenate(_unpack_bf16_pairs(y1_ref[...]), axis=1)
    y2 = jnp.concatenate(_unpack_bf16_pairs(y2_ref[...]), axis=1)
    y = x_ref[...] + (route[:, 2:3] * y1 + route[:, 3:4] * y2)
    o_ref[...] = y * lax.rsqrt(jnp.mean(y * y, axis=-1, keepdims=True) + RMS_EPS) * nf_ref[...]


def _final_combine(x1, route, yg, nf):
    t, d = x1.shape
    slot2 = t // TM
    row = lambda i: (i, 0)
    return pl.pallas_call(
        _final_combine_kernel,
        grid=(t // TM,),
        in_specs=[
            pl.BlockSpec((TM, d), row),
            pl.BlockSpec((TM, LANES), row),
            pl.BlockSpec((TM, d // 2), row),
            pl.BlockSpec((TM, d // 2), lambda i: (slot2 + i, 0)),
            pl.BlockSpec((1, d), lambda i: (0, 0)),
        ],
        out_specs=pl.BlockSpec((TM, d), row),
        out_shape=jax.ShapeDtypeStruct((t, d), F32),
        compiler_params=_cparams(("parallel",)),
    )(x1, route, yg, yg, nf.reshape(1, d).astype(F32))


def _moe(hn, route_t, cnt, w_gate, w_up, w_down, layer):
    t, d = hn.shape
    n_experts = w_gate.shape[1]
    ids = jnp.arange(n_experts, dtype=jnp.int32)
    e1, e2, rank1, rank2 = (route_t[f].astype(jnp.int32) for f in (0, 1, 4, 5))
    counts = cnt[:, 0].astype(jnp.int32)
    padded = ((counts + ROW_BLOCK - 1) // ROW_BLOCK) * ROW_BLOCK
    pad_end = jnp.cumsum(padded)
    pad_start = pad_end - padded

    def dest(e, rank):
        return jnp.sum(jnp.where(e[:, None] == ids[None, :], pad_start[None, :], 0), axis=1) + rank

    d1 = dest(e1, rank1)
    d2 = dest(e2, rank2)
    nb = (t * TOP_K_INNER) // ROW_BLOCK + n_experts
    n_used = (pad_end[-1:] // ROW_BLOCK).astype(jnp.int32)
    first_row = jnp.arange(nb, dtype=jnp.int32) * ROW_BLOCK
    block_e = jnp.sum((pad_end[None, :] <= first_row[:, None]).astype(jnp.int32), axis=1)
    block_e = jnp.minimum(block_e, n_experts - 1)
    row_end = jnp.sum(jnp.where(block_e[:, None] == ids[None, :], (pad_start + counts)[None, :], 0), axis=1)
    n_valid = jnp.clip(row_end - first_row, 0, ROW_BLOCK).astype(jnp.int32)
    xs = _sc_scatter_rows(hn, d1, d2, nb * ROW_BLOCK)
    ys = _experts(xs, (pad_start // ROW_BLOCK).astype(jnp.int32), (padded // ROW_BLOCK).astype(jnp.int32),
                  n_valid, n_used, w_gate, w_up, w_down, layer)
    return _sc_gather_rows(ys, jnp.concatenate([d1, d2]))


def _diff_lambda_init(layer_idx):
    return 0.8 - 0.6 * math.exp(-0.3 * layer_idx)


def _rope_tables(seq, dk):
    angle = 1.0 / (10000.0 ** jnp.linspace(0.0, 1.0, dk // 2, dtype=F32))
    angle = jnp.repeat(angle, 2)
    phase = jnp.arange(seq, dtype=F32)[:, None] * angle[None, :]
    sin, cos = jnp.sin(phase), jnp.cos(phase)
    even = (jnp.arange(dk) % 2 == 0)[None, :]
    return cos, jnp.where(even, -sin, 0.0), jnp.where(even, 0.0, sin)


def kernel(x, rel_bias_table, norm_mix, norm_ffn, norm_final, diff_w_in, diff_lambda_qk, diff_subln,
           diff_w_out, ret_w_in, ret_w_out, moe_w_group, moe_b_group, moe_w_expert, moe_b_expert,
           moe_w_gate, moe_w_up, moe_w_down):
    batch, seq, d = x.shape
    depth = norm_mix.shape[0]
    t = batch * seq
    heads = rel_bias_table.shape[1]
    dh = diff_lambda_qk.shape[-1]
    qk_dim = (ret_w_in.shape[2] - 2 * ret_w_out.shape[1]) // 2
    v_dim = ret_w_out.shape[1]
    dk = qk_dim // RET_HEADS
    xt = x.reshape(t, d)
    moe = None
    for i in range(depth):
        j = i // N_MIXERS
        if i % N_MIXERS == 0:
            col_scale = jnp.concatenate([jnp.full((d,), dh ** -0.5 * LOG2E, F32), jnp.ones((2 * d,), F32)])
            qkv = _norm_matmul(xt, norm_mix[i], diff_w_in, j, col_scale, moe=moe)
            if moe is not None:
                qkv, xt = qkv
            mix = _diff_attention(qkv, rel_bias_table, diff_lambda_qk[j], diff_subln[j],
                                  _diff_lambda_init(i), batch, seq, heads)
            gate = None
            w_out = diff_w_out[j].astype(BF16)
        else:
            n_in = ret_w_in.shape[2]
            col_scale = jnp.concatenate([jnp.ones((qk_dim,), F32), jnp.full((qk_dim,), dk ** -0.5, F32),
                                         jnp.ones((n_in - 2 * qk_dim,), F32)])
            qkvg = _norm_matmul(xt, norm_mix[i], ret_w_in, j, col_scale, rope=_rope_tables(seq, dk),
                                n_rope=2 * qk_dim // TN, seq=seq, moe=moe)
            if moe is not None:
                qkvg, xt = qkvg
            mix = _retention(qkvg, batch, seq, qk_dim, v_dim)
            gate = (qkvg, (2 * qk_dim + v_dim) // v_dim)
            w_out = ret_w_out[j].astype(BF16)
        x1, hn, route, route_t, cnt = _out_router(mix, gate, w_out, xt, norm_ffn[i], moe_w_group[i],
                                                  moe_b_group[i], moe_w_expert[i], moe_b_expert[i])
        xt = x1
        moe = (route, _moe(hn, route_t, cnt, moe_w_gate, moe_w_up, moe_w_down, i))
    return _final_combine(xt, moe[0], moe[1], norm_final).reshape(batch, seq, d)
```

```python
import functools
import math

import jax
import jax.numpy as jnp
from jax import lax
from jax.experimental import pallas as pl
from jax.experimental.pallas import tpu as pltpu
from jax.experimental.pallas import tpu_sc as plsc

F32 = jnp.float32
BF16 = jnp.bfloat16

N_MIXERS = 2
NUM_BUCKETS = 32
MAX_DISTANCE = 128
RET_HEADS = 4
TOP_K_INNER = 2
RMS_EPS = 1e-6
SUBLN_EPS = 1e-5

LANES = 128
SUBLANES = 8
BF16_SUBLANES = 16
LOG2E = 1.4426950408889634
VMEM_LIMIT = 56 * 1024 * 1024
SC_CORES = 2
SC_SUBCORES = 16

TM = 512
TN = 1024
TQ = 256
ATTN_HEADS = 4
RET_C = 256
RET_HP = 2
FF_SPLIT = 2
ROUTER_SPLIT = 2
ROW_BLOCK = 512
SC_CHUNK = 64


def _cparams(sem):
    return pltpu.CompilerParams(dimension_semantics=sem, vmem_limit_bytes=VMEM_LIMIT)


def _pack_bf16_pairs(x):
    h = x.shape[1] // 2
    hi = lax.bitcast_convert_type(x[:, :h].astype(BF16).astype(F32), jnp.uint32)
    lo = lax.bitcast_convert_type(x[:, h:].astype(BF16).astype(F32), jnp.uint32)
    return hi | (lo >> 16)


def _unpack_bf16_pairs(w):
    hi = lax.bitcast_convert_type(w & jnp.uint32(0xFFFF0000), F32)
    lo = lax.bitcast_convert_type(w << 16, F32)
    return hi, lo


def _norm_matmul_kernel(x_ref, *rest, n_rope, moe, layer):
    rest = list(rest)
    w_ref, stage_ref, sem = rest[-3:]
    rest = rest[:-3]
    if moe:
        route_ref, y1_ref, y2_ref = rest[:3]
        x_out_ref = rest.pop()
        rest = rest[3:]
    g_ref, scale_ref, w_hbm = rest[:3]
    if n_rope:
        cos_ref, sa_ref, sb_ref, o_ref = rest[3:]
    else:
        (o_ref,) = rest[3:]
    n_chunks = o_ref.shape[1] // TN

    @pl.when(pl.program_id(0) == 0)
    def _():
        def fetch(j):
            return pltpu.make_async_copy(w_hbm.at[layer, :, pl.ds(j * TN, TN)], stage_ref.at[j % 2],
                                         sem.at[j % 2])

        fetch(0).start()
        for j in range(n_chunks):
            if j + 1 < n_chunks:
                fetch(j + 1).start()
            fetch(j).wait()
            cols = slice(j * TN, (j + 1) * TN)
            w_ref[:, cols] = (stage_ref[j % 2] * scale_ref[:, cols]).astype(BF16)

    x = x_ref[...]
    if moe:
        route = route_ref[...]
        y1 = jnp.concatenate(_unpack_bf16_pairs(y1_ref[...]), axis=1)
        y2 = jnp.concatenate(_unpack_bf16_pairs(y2_ref[...]), axis=1)
        x = x + (route[:, 2:3] * y1 + route[:, 3:4] * y2)
        x_out_ref[...] = x
    xn = (x * lax.rsqrt(jnp.mean(x * x, axis=-1, keepdims=True) + RMS_EPS) * g_ref[...]).astype(BF16)
    for j in range(n_chunks):
        cols = slice(j * TN, (j + 1) * TN)
        acc = jnp.dot(xn, w_ref[:, cols], preferred_element_type=F32)
        if j < n_rope:
            reps = TN // cos_ref.shape[1]
            c = jnp.tile(cos_ref[...], (1, reps))
            sa = jnp.tile(sa_ref[...], (1, reps))
            sb = jnp.tile(sb_ref[...], (1, reps))
            nxt = pltpu.roll(acc, TN - 1, axis=1)
            prv = pltpu.roll(acc, 1, axis=1)
            acc = acc * c + nxt * sa + prv * sb
        o_ref[:, cols] = acc.astype(o_ref.dtype)


def _norm_matmul(x, g, w, layer, col_scale, rope=None, n_rope=0, seq=None, moe=None):
    t, d = x.shape
    n = w.shape[2]
    row = lambda i: (i, 0)
    in_specs = [pl.BlockSpec((TM, d), row)]
    args = [x]
    if moe is not None:
        route, yg = moe
        slot2 = t // TM
        in_specs += [pl.BlockSpec((TM, LANES), row), pl.BlockSpec((TM, d // 2), row),
                     pl.BlockSpec((TM, d // 2), lambda i: (slot2 + i, 0))]
        args += [route, yg, yg]
    in_specs += [
        pl.BlockSpec((1, d), lambda i: (0, 0)),
        pl.BlockSpec((1, n), lambda i: (0, 0)),
        pl.BlockSpec(memory_space=pl.ANY),
    ]
    args += [g.reshape(1, d), col_scale.reshape(1, n).astype(F32), w]
    if rope is not None:
        nb = seq // TM
        dk = rope[0].shape[1]
        in_specs += [pl.BlockSpec((TM, dk), lambda i: (i % nb, 0))] * 3
        args += list(rope)
    out_specs = [pl.BlockSpec((TM, n), row)]
    out_shape = [jax.ShapeDtypeStruct((t, n), BF16)]
    if moe is not None:
        out_specs.append(pl.BlockSpec((TM, d), row))
        out_shape.append(jax.ShapeDtypeStruct((t, d), F32))
    out = pl.pallas_call(
        functools.partial(_norm_matmul_kernel, n_rope=n_rope, moe=moe is not None, layer=layer),
        grid=(t // TM,),
        in_specs=in_specs,
        out_specs=out_specs,
        out_shape=out_shape,
        scratch_shapes=[pltpu.VMEM((d, n), BF16), pltpu.VMEM((2, d, TN), F32), pltpu.SemaphoreType.DMA((2,))],
        compiler_params=_cparams(("arbitrary",)),
    )(*args)
    return out if moe is not None else out[0]


def _diff_attn_kernel(q_ref, k_ref, v_ref, bias_ref, lam_ref, g_ref, o_ref,
                      vt_ref, m_ref, acc_ref, s_ref, *, lambda_init):
    seq = q_ref.shape[0]
    hp, _, _, tq = bias_ref.shape
    dv = q_ref.shape[1] // hp
    dh = dv // 2
    nq = seq // tq

    def head(hh):
        return slice(hh * dv, (hh + 1) * dv)

    extra = vt_ref.shape[2] - dv
    ones_row = jnp.where(lax.broadcasted_iota(jnp.int32, (extra, tq), 0) == 0, 1.0, 0.0).astype(BF16)
    for hh in range(hp):
        for j in range(nq):
            vt_ref[hh, j, :dv, :] = v_ref[j * tq:(j + 1) * tq, head(hh)].astype(F32).T.astype(BF16)
            vt_ref[hh, j, dv:, :] = ones_row

    lq = lam_ref[...]
    lam = (jnp.exp(jnp.sum(lq[0:1] * lq[1:2], axis=-1, keepdims=True))
           - jnp.exp(jnp.sum(lq[2:3] * lq[3:4], axis=-1, keepdims=True)) + lambda_init)
    lane = lax.broadcasted_iota(jnp.int32, (tq, dv), 1)

    def query_maps(qi):
        qrows = pl.ds(pl.multiple_of(qi * tq, tq), tq)
        qs = []
        for hh in range(hp):
            q = q_ref[qrows, head(hh)]
            zero = jnp.zeros_like(q)
            qs.append(jnp.concatenate([jnp.where(lane < dh, q, zero), jnp.where(lane >= dh, q, zero)],
                                      axis=0))
        return qs

    def score_dots(j, qs):
        rows = pl.ds(pl.multiple_of(j * tq, tq), tq)
        for hh in range(hp):
            s_ref[hh] = lax.dot_general(k_ref[rows, head(hh)], qs[hh], (((1,), (1,)), ((), ())),
                                        preferred_element_type=F32)

    score_dots(0, query_maps(0))

    def q_body(qi, carry):
        qrows = pl.ds(pl.multiple_of(qi * tq, tq), tq)
        qs = query_maps(qi)
        qs_next = query_maps(jnp.minimum(qi + 1, nq - 1))

        m_ref[...] = jnp.full(m_ref.shape, -jnp.inf, F32)
        acc_ref[...] = jnp.zeros(acc_ref.shape, F32)

        def step(j, tile, prefetch):
            probs, alphas = [], []
            for hh in range(hp):
                s = s_ref[hh]
                if tile is not None:
                    bias = bias_ref[hh, tile]
                    s = s + jnp.concatenate([bias, bias], axis=1)
                m_old = m_ref[hh]
                m_new = jnp.maximum(m_old, jnp.max(s, axis=0, keepdims=True))
                alpha = jnp.exp2(m_old - m_new)
                m_ref[hh] = m_new
                p = jnp.exp2(s - m_new).astype(BF16)
                rows = pl.ds(pl.multiple_of((j + 1 if prefetch == "block" else 0) * tq, tq), tq)
                s_ref[hh] = lax.dot_general(k_ref[rows, head(hh)], (qs if prefetch == "block" else qs_next)[hh],
                                            (((1,), (1,)), ((), ())), preferred_element_type=F32)
                acc_ref[hh] = alpha * acc_ref[hh] + jnp.dot(vt_ref[hh, j], p, preferred_element_type=F32)

        def far_body(j, c):
            step(j, None, "block")
            return c

        lax.fori_loop(0, jnp.maximum(qi - 1, 0), far_body, 0)

        @pl.when(qi >= 1)
        def _():
            step(qi - 1, 1, "block")

        step(qi, 0, "tile")

        for hh in range(hp):
            acc = acc_ref[hh]
            on = acc[:dv] / acc[dv:dv + 1]
            ot = on[:, :tq] - lam * on[:, tq:]
            ot = ot * lax.rsqrt(jnp.mean(ot * ot, axis=0, keepdims=True) + SUBLN_EPS)
            o_ref[qrows, head(hh)] = (ot.T * g_ref[...] * (1.0 - lambda_init)).astype(o_ref.dtype)
        return carry

    lax.fori_loop(0, nq, q_body, 0)


def _t5_bucket(rel):
    max_exact = NUM_BUCKETS // 2
    n = jnp.maximum(rel, 0)
    is_small = n < max_exact
    nf = jnp.maximum(n, max_exact).astype(F32)
    large = max_exact + (jnp.log(nf / max_exact) / math.log(MAX_DISTANCE / max_exact)
                         * (NUM_BUCKETS - max_exact)).astype(jnp.int32)
    large = jnp.minimum(large, NUM_BUCKETS - 1)
    return jnp.where(is_small, n, large)


def _bias_tiles(rel_table):
    heads = rel_table.shape[1]
    span = 2 * TQ + 1
    bucket = _t5_bucket(jnp.arange(span))
    onehot = (bucket[:, None] == jnp.arange(NUM_BUCKETS)[None, :]).astype(F32)
    table = (rel_table.astype(F32) - rel_table[NUM_BUCKETS - 1].astype(F32)[None, :]) * LOG2E
    by_rel = jnp.dot(onehot, table, precision=lax.Precision.HIGHEST).T
    toep = jnp.tile(by_rel, (1, TQ))[:, :TQ * (span - 1)].reshape(heads, TQ, span - 1)
    kj = jnp.arange(TQ)[:, None]
    qi = jnp.arange(TQ)[None, :]
    diag = jnp.where((qi >= kj)[None], toep[:, :, :TQ], -jnp.inf)
    return jnp.stack([diag, toep[:, :, TQ:]], axis=1)


def _diff_attention(qkv, rel_table, lam_qk, subln_g, lambda_init, batch, seq, heads):
    t = qkv.shape[0]
    d = qkv.shape[1] // 3
    dv = d // heads
    hp = ATTN_HEADS
    assert TQ + 1 >= MAX_DISTANCE and seq % TQ == 0 and dv == LANES and heads % hp == 0
    nq = seq // TQ
    groups = heads // hp
    bias = _bias_tiles(rel_table)
    return pl.pallas_call(
        functools.partial(_diff_attn_kernel, lambda_init=lambda_init),
        grid=(batch, groups),
        in_specs=[
            pl.BlockSpec((seq, hp * dv), lambda b, g: (b, g)),
            pl.BlockSpec((seq, hp * dv), lambda b, g: (b, groups + g)),
            pl.BlockSpec((seq, hp * dv), lambda b, g: (b, 2 * groups + g)),
            pl.BlockSpec((hp, 2, TQ, TQ), lambda b, g: (g, 0, 0, 0)),
            pl.BlockSpec(lam_qk.shape, lambda b, g: (0, 0)),
            pl.BlockSpec((1, dv), lambda b, g: (0, 0)),
        ],
        out_specs=pl.BlockSpec((seq, hp * dv), lambda b, g: (b, g)),
        out_shape=jax.ShapeDtypeStruct((t, d), BF16),
        scratch_shapes=[
            pltpu.VMEM((hp, nq, dv + BF16_SUBLANES, TQ), BF16),
            pltpu.VMEM((hp, 1, 2 * TQ), F32),
            pltpu.VMEM((hp, dv + BF16_SUBLANES, 2 * TQ), F32),
            pltpu.VMEM((hp, TQ, 2 * TQ), F32),
        ],
        compiler_params=_cparams(("parallel", "parallel")),
    )(qkv, qkv, qkv, bias, lam_qk.astype(F32), subln_g.reshape(1, dv).astype(F32))


def _retention_kernel(q_ref, k_ref, v_ref, dec_ref, tab_ref, o_ref, state_ref):
    seq = q_ref.shape[0]
    hp, c, _ = dec_ref.shape
    dk = q_ref.shape[1] // hp
    dv = v_ref.shape[1] // hp
    heads = range(hp)
    state_ref[...] = jnp.zeros(state_ref.shape, F32)

    def body(n, carry):
        rows = pl.ds(pl.multiple_of(n * c, c), c)
        q = [q_ref[rows, hh * dk:(hh + 1) * dk] for hh in heads]
        k = [k_ref[rows, hh * dk:(hh + 1) * dk] for hh in heads]
        v = [v_ref[rows, hh * dv:(hh + 1) * dv] for hh in heads]
        tab = [tab_ref[hh] for hh in heads]
        st = [state_ref[hh] for hh in heads]
        s = [lax.dot_general(q[hh], k[hh], (((1,), (1,)), ((), ())), preferred_element_type=F32)
             for hh in heads]
        cross = [jnp.dot(q[hh], st[hh].astype(BF16), preferred_element_type=F32) for hh in heads]
        kt = [(k[hh].astype(F32) * tab[hh][:, 1:2]).T.astype(BF16) for hh in heads]
        inner = [jnp.dot((s[hh] * dec_ref[hh]).astype(BF16), v[hh], preferred_element_type=F32)
                 for hh in heads]
        for hh in heads:
            state_ref[hh] = st[hh] * tab[hh][0:1, 2:3] + jnp.dot(kt[hh], v[hh], preferred_element_type=F32)
        for hh in heads:
            o = inner[hh] + cross[hh] * tab[hh][:, 0:1]
            o = o * lax.rsqrt(jnp.mean(o * o, axis=-1, keepdims=True) + RMS_EPS)
            o_ref[rows, hh * dv:(hh + 1) * dv] = o.astype(o_ref.dtype)
        return carry

    lax.fori_loop(0, seq // c, body, 0)


def _retention_tables(heads):
    c = RET_C
    log_gamma = jnp.log(1.0 - 2.0 ** (-5.0 - jnp.arange(heads, dtype=F32)))
    pos = jnp.arange(c, dtype=F32)
    dist = pos[:, None] - pos[None, :]
    inner_decay = jnp.where(dist[None] >= 0,
                            jnp.exp(jnp.maximum(dist, 0.0)[None] * log_gamma[:, None, None]), 0.0)
    q_decay = jnp.exp((pos + 1.0)[None] * log_gamma[:, None])
    k_decay = jnp.exp((c - 1.0 - pos)[None] * log_gamma[:, None])
    chunk_decay = jnp.broadcast_to(jnp.exp(c * log_gamma)[:, None], (heads, c))
    tab = jnp.stack([q_decay, k_decay, chunk_decay], axis=-1)
    tab = jnp.pad(tab, ((0, 0), (0, 0), (0, LANES - 3)))
    return inner_decay, tab


def _retention(qkvg, batch, seq, qk_dim, v_dim):
    t = qkvg.shape[0]
    dk = qk_dim // RET_HEADS
    dvh = v_dim // RET_HEADS
    inner_decay, tab = _retention_tables(RET_HEADS)
    hp = RET_HP
    assert RET_HEADS % hp == 0
    groups = RET_HEADS // hp
    kb = qk_dim // (hp * dk)
    vb = 2 * qk_dim // (hp * dvh)
    return pl.pallas_call(
        _retention_kernel,
        grid=(batch, groups),
        in_specs=[
            pl.BlockSpec((seq, hp * dk), lambda b, g: (b, g)),
            pl.BlockSpec((seq, hp * dk), lambda b, g: (b, kb + g)),
            pl.BlockSpec((seq, hp * dvh), lambda b, g: (b, vb + g)),
            pl.BlockSpec((hp, RET_C, RET_C), lambda b, g: (g, 0, 0)),
            pl.BlockSpec((hp, RET_C, LANES), lambda b, g: (g, 0, 0)),
        ],
        out_specs=pl.BlockSpec((seq, hp * dvh), lambda b, g: (b, g)),
        out_shape=jax.ShapeDtypeStruct((t, v_dim), BF16),
        scratch_shapes=[pltpu.VMEM((hp, dk, dvh), F32)],
        compiler_params=_cparams(("parallel", "parallel")),
    )(qkvg, qkvg, qkvg, inner_decay, tab)


def _out_router_kernel(*refs, gated, n_groups, n_experts):
    if gated:
        a_ref, gate_ref, w_ref, x_ref, nf_ref, wr_ref, br_ref = refs[:7]
        x1_ref, hn_ref, route_ref, route_t_ref, cnt_ref, carry_ref = refs[7:]
    else:
        a_ref, w_ref, x_ref, nf_ref, wr_ref, br_ref = refs[:6]
        x1_ref, hn_ref, route_ref, route_t_ref, cnt_ref, carry_ref = refs[6:]
    epg = n_experts // n_groups
    tm = x_ref.shape[0] // ROUTER_SPLIT

    @pl.when(pl.program_id(0) == 0)
    def _():
        carry_ref[...] = jnp.zeros(carry_ref.shape, F32)

    x1_parts = []
    for part in range(ROUTER_SPLIT):
        rows = slice(part * tm, (part + 1) * tm)
        a = a_ref[rows, :]
        if gated:
            gate = gate_ref[rows, :]
            a = gate * jax.nn.sigmoid(gate) * a
        x1_parts.append(x_ref[rows, :] + jnp.dot(a, w_ref[...], preferred_element_type=F32))
    for part in range(ROUTER_SPLIT):
        rows = slice(part * tm, (part + 1) * tm)
        _route_rows(x1_parts[part], rows, nf_ref, wr_ref, br_ref, x1_ref, hn_ref, route_ref, route_t_ref,
                    cnt_ref, carry_ref, n_groups=n_groups, n_experts=n_experts, epg=epg)


def _route_rows(x1, rows, nf_ref, wr_ref, br_ref, x1_ref, hn_ref, route_ref, route_t_ref, cnt_ref,
                carry_ref, *, n_groups, n_experts, epg):
    tm = x1.shape[0]
    x1_ref[rows, :] = x1
    hn = x1 * lax.rsqrt(jnp.mean(x1 * x1, axis=-1, keepdims=True) + RMS_EPS) * nf_ref[...]
    hn_ref[rows, :] = _pack_bf16_pairs(hn)

    hn_hi = hn.astype(BF16)
    hn_lo = (hn - hn_hi.astype(F32)).astype(BF16)
    both = jnp.dot(hn_hi, wr_ref[...], preferred_element_type=F32)
    logits = (both[:, :LANES] + both[:, LANES:]
              + jnp.dot(hn_lo, wr_ref[:, :LANES], preferred_element_type=F32)) + br_ref[...]

    n_rows = -(-(n_groups + n_experts) // SUBLANES) * SUBLANES
    lt = logits.T[:n_rows, :]
    row = lax.broadcasted_iota(jnp.int32, lt.shape, 0)
    big = jnp.int32(LANES)
    neg = jnp.float32(-jnp.inf)

    gl = jnp.where(row < n_groups, lt, neg)
    gmax = jnp.max(gl, axis=0, keepdims=True)
    g_p = 1.0 / jnp.sum(jnp.exp(gl - gmax), axis=0, keepdims=True)
    g_idx = jnp.min(jnp.where(gl == gmax, row, big), axis=0, keepdims=True)

    lo = n_groups + g_idx * epg
    el = jnp.where((row >= lo) & (row < lo + epg), lt, neg)
    v1 = jnp.max(el, axis=0, keepdims=True)
    i1 = jnp.min(jnp.where(el == v1, row, big), axis=0, keepdims=True)
    el2 = jnp.where(row == i1, neg, el)
    v2 = jnp.max(el2, axis=0, keepdims=True)
    i2 = jnp.min(jnp.where(el2 == v2, row, big), axis=0, keepdims=True)
    tt = jnp.exp(v2 - v1)
    w1 = g_p / (1.0 + tt)
    w2 = g_p * tt / (1.0 + tt)
    e1 = i1 - n_groups
    e2 = i2 - n_groups

    erow = lax.broadcasted_iota(jnp.int32, (n_experts, tm), 0)
    oh1 = erow == e1
    oh2 = erow == e2
    onehot = jnp.where(oh1 | oh2, 1.0, 0.0)
    before = lax.broadcasted_iota(jnp.int32, (tm, tm), 0) < lax.broadcasted_iota(jnp.int32, (tm, tm), 1)
    tri = jnp.where(before, 1.0, 0.0).astype(BF16)
    carry = carry_ref[...]
    base = jnp.dot(onehot.astype(BF16), tri, preferred_element_type=F32) + carry[:, 0:1]
    rank1 = jnp.sum(jnp.where(oh1, base, 0.0), axis=0, keepdims=True)
    rank2 = jnp.sum(jnp.where(oh2, base, 0.0), axis=0, keepdims=True)
    carry = carry + jnp.sum(onehot, axis=1, keepdims=True)
    carry_ref[...] = carry
    cnt_ref[...] = carry

    frow = lax.broadcasted_iota(jnp.int32, (LANES, tm), 0)
    out = jnp.zeros((LANES, tm), F32)
    for idx, val in enumerate((e1.astype(F32), e2.astype(F32), w1, w2, rank1, rank2)):
        out = jnp.where(frow == idx, val, out)
    route_t_ref[:, rows] = out[:route_t_ref.shape[0], :]
    route_ref[rows, :] = out.T


def _out_router(a, gate, w, x, nf, w_group, b_group, w_expert, b_expert):
    t, d = x.shape
    k = w.shape[0]
    n_groups = w_group.shape[1]
    n_experts = w_expert.shape[1]
    assert n_groups + n_experts <= LANES
    pad = LANES - n_groups - n_experts
    wr = jnp.pad(jnp.concatenate([w_group, w_expert], axis=1).astype(F32), ((0, 0), (0, pad)))
    wr_hi = wr.astype(BF16)
    wr = jnp.concatenate([wr_hi, (wr - wr_hi.astype(F32)).astype(BF16)], axis=1)
    br = jnp.pad(jnp.concatenate([b_group, b_expert]).astype(F32), (0, pad)).reshape(1, LANES)
    gated = gate is not None
    row = lambda i: (i, 0)
    fixed = lambda i: (0, 0)
    in_specs = [pl.BlockSpec((TM, k), row)]
    args = [a]
    if gated:
        gate_arr, gate_col = gate
        in_specs.append(pl.BlockSpec((TM, k), lambda i: (i, gate_col)))
        args.append(gate_arr)
    in_specs += [
        pl.BlockSpec((k, d), fixed),
        pl.BlockSpec((TM, d), row),
        pl.BlockSpec((1, d), fixed),
        pl.BlockSpec((d, 2 * LANES), fixed),
        pl.BlockSpec((1, LANES), fixed),
    ]
    args += [w, x, nf.reshape(1, d), wr, br]
    return pl.pallas_call(
        functools.partial(_out_router_kernel, gated=gated, n_groups=n_groups, n_experts=n_experts),
        grid=(t // TM,),
        in_specs=in_specs,
        out_specs=[
            pl.BlockSpec((TM, d), row),
            pl.BlockSpec((TM, d // 2), row),
            pl.BlockSpec((TM, LANES), row),
            pl.BlockSpec((SUBLANES, TM), lambda i: (0, i)),
            pl.BlockSpec((n_experts, LANES), fixed),
        ],
        out_shape=[
            jax.ShapeDtypeStruct((t, d), F32),
            jax.ShapeDtypeStruct((t, d // 2), jnp.uint32),
            jax.ShapeDtypeStruct((t, LANES), F32),
            jax.ShapeDtypeStruct((SUBLANES, t), F32),
            jax.ShapeDtypeStruct((n_experts, LANES), F32),
        ],
        scratch_shapes=[pltpu.VMEM((n_experts, LANES), F32)],
        compiler_params=_cparams(("arbitrary",)),
    )(*args)


def _sc_mesh():
    return plsc.VectorSubcoreMesh(core_axis_name="c", subcore_axis_name="s",
                                  num_cores=SC_CORES, num_subcores=SC_SUBCORES)


def _sc_worker():
    return lax.axis_index("s") * SC_CORES + lax.axis_index("c")


def _sc_scatter_rows(src, d1, d2, n_rows):
    t, d = src.shape
    workers = SC_CORES * SC_SUBCORES
    per_w = t // workers
    n_ch = per_w // SC_CHUNK
    assert t % (workers * SC_CHUNK) == 0
    idx = jnp.stack([d1, d2]).reshape(2, workers, n_ch, SC_CHUNK)

    @functools.partial(
        pl.kernel, mesh=_sc_mesh(),
        out_type=jax.ShapeDtypeStruct((n_rows, d), src.dtype),
        scratch_types=[pltpu.VMEM((2, n_ch, SC_CHUNK), jnp.int32),
                       pltpu.VMEM((SC_CHUNK, d), src.dtype),
                       pltpu.VMEM((SC_CHUNK, d), src.dtype),
                       pltpu.SemaphoreType.DMA, pltpu.SemaphoreType.DMA],
    )
    def scatter(src_hbm, idx_hbm, out_hbm, idx_v, rows_a, rows_b, sem_a, sem_b):
        wid = _sc_worker()
        base = wid * per_w
        for s in range(2):
            pltpu.sync_copy(idx_hbm.at[s, wid], idx_v.at[s])

        def move(c, rows_v, sem):
            pltpu.sync_copy(src_hbm.at[pl.ds(base + c * SC_CHUNK, SC_CHUNK)], rows_v)
            return [pltpu.async_copy(rows_v, out_hbm.at[idx_v.at[s, c]], sem) for s in range(2)]

        @pl.loop(0, n_ch, step=2)
        def _(c):
            first = move(c, rows_a, sem_a)
            second = move(c + 1, rows_b, sem_b)
            for cp in first + second:
                cp.wait()

    return scatter(src, idx)


def _sc_gather_rows(table, idx):
    b = idx.shape[0]
    d = table.shape[1]
    workers = SC_CORES * SC_SUBCORES
    per_w = b // workers
    n_ch = per_w // SC_CHUNK
    assert b % (workers * SC_CHUNK * 2) == 0

    @functools.partial(
        pl.kernel, mesh=_sc_mesh(),
        out_type=jax.ShapeDtypeStruct((b, d), table.dtype),
        scratch_types=[pltpu.VMEM((per_w,), jnp.int32),
                       pltpu.VMEM((SC_CHUNK, d), table.dtype),
                       pltpu.VMEM((SC_CHUNK, d), table.dtype),
                       pltpu.SemaphoreType.DMA, pltpu.SemaphoreType.DMA],
    )
    def gather(table_hbm, idx_hbm, out_hbm, idx_v, rows_a, rows_b, sem_a, sem_b):
        base = _sc_worker() * per_w
        pltpu.sync_copy(idx_hbm.at[pl.ds(base, per_w)], idx_v)

        def fetch(c, rows_v, sem):
            return pltpu.async_copy(table_hbm.at[idx_v.at[pl.ds(c * SC_CHUNK, SC_CHUNK)]], rows_v, sem)

        def store(c, rows_v):
            pltpu.sync_copy(rows_v, out_hbm.at[pl.ds(base + c * SC_CHUNK, SC_CHUNK)])

        @pl.loop(0, n_ch, step=2)
        def _(c):
            first = fetch(c, rows_a, sem_a)
            second = fetch(c + 1, rows_b, sem_b)
            first.wait()
            store(c, rows_a)
            second.wait()
            store(c + 1, rows_b)

    return gather(table, idx)


def _expert_kernel(fb_ref, nblk_ref, nv_ref, nu_ref, xs_ref, wg_ref, wu_ref, wd_ref, ys_ref,
                   wgb_ref, wub_ref, wdb_ref, xbuf, ybuf, xsem, ysem):
    e = pl.program_id(0)
    rb = xbuf.shape[1]
    nb = ys_ref.shape[0] // rb
    n_used = nu_ref[0]

    def x_copy(g, slot):
        rows = pl.ds(pl.multiple_of(g * rb, rb), rb)
        return pltpu.make_async_copy(xs_ref.at[rows], xbuf.at[slot], xsem.at[slot])

    def y_copy(g, slot):
        rows = pl.ds(pl.multiple_of(g * rb, rb), rb)
        return pltpu.make_async_copy(ybuf.at[slot], ys_ref.at[rows], ysem.at[slot])

    @pl.when(e == 0)
    def _():
        x_copy(0, 0).start()

    @pl.when(nblk_ref[e] > 0)
    def _():
        wgb_ref[...] = wg_ref[0, 0].astype(BF16)
        wub_ref[...] = wu_ref[0, 0].astype(BF16)
        wdb_ref[...] = wd_ref[0, 0].astype(BF16)

    def block(b, carry):
        g = fb_ref[e] + b
        slot = lax.rem(g, 2)
        x_copy(g, slot).wait()

        @pl.when(g + 1 < n_used)
        def _():
            x_copy(g + 1, 1 - slot).start()

        @pl.when(g >= 2)
        def _():
            y_copy(g - 2, slot).wait()

        nv = nv_ref[g]

        def ffn(rows):
            valid = lax.broadcasted_iota(jnp.int32, (rows, 1), 0) < nv
            x_hi, x_lo = _unpack_bf16_pairs(jnp.where(valid, xbuf[slot, :rows, :], jnp.uint32(0)))
            x = jnp.concatenate([x_hi.astype(BF16), x_lo.astype(BF16)], axis=1)
            slab = wgb_ref.shape[1] // FF_SPLIT
            hmid = []
            for c in range(FF_SPLIT):
                cols = slice(c * slab, (c + 1) * slab)
                gate = jnp.dot(x, wgb_ref[:, cols], preferred_element_type=F32)
                up = jnp.dot(x, wub_ref[:, cols], preferred_element_type=F32)
                hmid.append((gate * jax.nn.sigmoid(gate) * up).astype(BF16))
            hmid = jnp.concatenate(hmid, axis=1)
            ybuf[slot, :rows, :] = _pack_bf16_pairs(jnp.dot(hmid, wdb_ref[...], preferred_element_type=F32))
            if rows < rb:
                ybuf[slot, rows:, :] = jnp.zeros((rb - rows, ybuf.shape[2]), ybuf.dtype)

        @pl.when(nv > rb // 2)
        def _():
            ffn(rb)

        @pl.when(nv <= rb // 2)
        def _():
            ffn(rb // 2)

        y_copy(g, slot).start()
        return carry

    lax.fori_loop(0, nblk_ref[e], block, 0)

    @pl.when(e == pl.num_programs(0) - 1)
    def _():
        @pl.when(n_used >= 2)
        def _():
            y_copy(n_used - 2, lax.rem(n_used, 2)).wait()

        y_copy(n_used - 1, lax.rem(n_used - 1, 2)).wait()
        ybuf[0] = jnp.zeros(ybuf.shape[1:], ybuf.dtype)

        def tail_start(g, carry):
            y_copy(g, 0).start()
            return carry

        def tail_wait(g, carry):
            y_copy(g, 0).wait()
            return carry

        lax.fori_loop(n_used, nb, tail_start, 0)
        lax.fori_loop(n_used, nb, tail_wait, 0)


def _experts(xs, first_block, n_blocks, n_valid, n_used, w_gate, w_up, w_down, layer):
    n_rows, dp = xs.shape
    d = 2 * dp
    n_experts, ff = w_gate.shape[1], w_gate.shape[3]
    wsel = lambda e, *_: (layer, e, 0, 0)
    grid_spec = pltpu.PrefetchScalarGridSpec(
        num_scalar_prefetch=4,
        grid=(n_experts,),
        in_specs=[
            pl.BlockSpec(memory_space=pl.ANY),
            pl.BlockSpec((1, 1, d, ff), wsel),
            pl.BlockSpec((1, 1, d, ff), wsel),
            pl.BlockSpec((1, 1, ff, d), wsel),
        ],
        out_specs=pl.BlockSpec(memory_space=pl.ANY),
        scratch_shapes=[pltpu.VMEM((d, ff), BF16), pltpu.VMEM((d, ff), BF16), pltpu.VMEM((ff, d), BF16),
                        pltpu.VMEM((2, ROW_BLOCK, dp), xs.dtype), pltpu.VMEM((2, ROW_BLOCK, dp), xs.dtype),
                        pltpu.SemaphoreType.DMA((2,)), pltpu.SemaphoreType.DMA((2,))],
    )
    return pl.pallas_call(
        _expert_kernel,
        grid_spec=grid_spec,
        out_shape=jax.ShapeDtypeStruct((n_rows, dp), xs.dtype),
        compiler_params=_cparams(("arbitrary",)),
    )(first_block, n_blocks, n_valid, n_used, xs, w_gate, w_up, w_down)


def _final_combine_kernel(x_ref, route_ref, y1_ref, y2_ref, nf_ref, o_ref):
    route = route_ref[...]
    y1 = jnp.concatenate(_unpack_bf16_pairs(y1_ref[...]), axis=1)
    y2 = jnp.concatenate(_unpack_bf16_pairs(y2_ref[...]), axis=1)
    y = x_ref[...] + (route[:, 2:3] * y1 + route[:, 3:4] * y2)
    o_ref[...] = y * lax.rsqrt(jnp.mean(y * y, axis=-1, keepdims=True) + RMS_EPS) * nf_ref[...]


def _final_combine(x1, route, yg, nf):
    t, d = x1.shape
    slot2 = t // TM
    row = lambda i: (i, 0)
    return pl.pallas_call(
        _final_combine_kernel,
        grid=(t // TM,),
        in_specs=[
            pl.BlockSpec((TM, d), row),
            pl.BlockSpec((TM, LANES), row),
            pl.BlockSpec((TM, d // 2), row),
            pl.BlockSpec((TM, d // 2), lambda i: (slot2 + i, 0)),
            pl.BlockSpec((1, d), lambda i: (0, 0)),
        ],
        out_specs=pl.BlockSpec((TM, d), row),
        out_shape=jax.ShapeDtypeStruct((t, d), F32),
        compiler_params=_cparams(("parallel",)),
    )(x1, route, yg, yg, nf.reshape(1, d).astype(F32))


def _moe(hn, route_t, cnt, w_gate, w_up, w_down, layer):
    t, d = hn.shape
    n_experts = w_gate.shape[1]
    ids = jnp.arange(n_experts, dtype=jnp.int32)
    e1, e2, rank1, rank2 = (route_t[f].astype(jnp.int32) for f in (0, 1, 4, 5))
    counts = cnt[:, 0].astype(jnp.int32)
    padded = ((counts + ROW_BLOCK - 1) // ROW_BLOCK) * ROW_BLOCK
    pad_end = jnp.cumsum(padded)
    pad_start = pad_end - padded

    def dest(e, rank):
        return jnp.sum(jnp.where(e[:, None] == ids[None, :], pad_start[None, :], 0), axis=1) + rank

    d1 = dest(e1, rank1)
    d2 = dest(e2, rank2)
    nb = (t * TOP_K_INNER) // ROW_BLOCK + n_experts
    n_used = (pad_end[-1:] // ROW_BLOCK).astype(jnp.int32)
    first_row = jnp.arange(nb, dtype=jnp.int32) * ROW_BLOCK
    block_e = jnp.sum((pad_end[None, :] <= first_row[:, None]).astype(jnp.int32), axis=1)
    block_e = jnp.minimum(block_e, n_experts - 1)
    row_end = jnp.sum(jnp.where(block_e[:, None] == ids[None, :], (pad_start + counts)[None, :], 0), axis=1)
    n_valid = jnp.clip(row_end - first_row, 0, ROW_BLOCK).astype(jnp.int32)
    xs = _sc_scatter_rows(hn, d1, d2, nb * ROW_BLOCK)
    ys = _experts(xs, (pad_start // ROW_BLOCK).astype(jnp.int32), (padded // ROW_BLOCK).astype(jnp.int32),
                  n_valid, n_used, w_gate, w_up, w_down, layer)
    return _sc_gather_rows(ys, jnp.concatenate([d1, d2]))


def _diff_lambda_init(layer_idx):
    return 0.8 - 0.6 * math.exp(-0.3 * layer_idx)


def _rope_tables(seq, dk):
    angle = 1.0 / (10000.0 ** jnp.linspace(0.0, 1.0, dk // 2, dtype=F32))
    angle = jnp.repeat(angle, 2)
    phase = jnp.arange(seq, dtype=F32)[:, None] * angle[None, :]
    sin, cos = jnp.sin(phase), jnp.cos(phase)
    even = (jnp.arange(dk) % 2 == 0)[None, :]
    return cos, jnp.where(even, -sin, 0.0), jnp.where(even, 0.0, sin)


def kernel(x, rel_bias_table, norm_mix, norm_ffn, norm_final, diff_w_in, diff_lambda_qk, diff_subln,
           diff_w_out, ret_w_in, ret_w_out, moe_w_group, moe_b_group, moe_w_expert, moe_b_expert,
           moe_w_gate, moe_w_up, moe_w_down):
    batch, seq, d = x.shape
    depth = norm_mix.shape[0]
    t = batch * seq
    heads = rel_bias_table.shape[1]
    dh = diff_lambda_qk.shape[-1]
    qk_dim = (ret_w_in.shape[2] - 2 * ret_w_out.shape[1]) // 2
    v_dim = ret_w_out.shape[1]
    dk = qk_dim // RET_HEADS
    xt = x.reshape(t, d)
    moe = None
    for i in range(depth):
        j = i // N_MIXERS
        if i % N_MIXERS == 0:
            col_scale = jnp.concatenate([jnp.full((d,), dh ** -0.5 * LOG2E, F32), jnp.ones((2 * d,), F32)])
            qkv = _norm_matmul(xt, norm_mix[i], diff_w_in, j, col_scale, moe=moe)
            if moe is not None:
                qkv, xt = qkv
            mix = _diff_attention(qkv, rel_bias_table, diff_lambda_qk[j], diff_subln[j],
                                  _diff_lambda_init(i), batch, seq, heads)
            gate = None
            w_out = diff_w_out[j].astype(BF16)
        else:
            n_in = ret_w_in.shape[2]
            col_scale = jnp.concatenate([jnp.ones((qk_dim,), F32), jnp.full((qk_dim,), dk ** -0.5, F32),
                                         jnp.ones((n_in - 2 * qk_dim,), F32)])
            qkvg = _norm_matmul(xt, norm_mix[i], ret_w_in, j, col_scale, rope=_rope_tables(seq, dk),
                                n_rope=2 * qk_dim // TN, seq=seq, moe=moe)
            if moe is not None:
                qkvg, xt = qkvg
            mix = _retention(qkvg, batch, seq, qk_dim, v_dim)
            gate = (qkvg, (2 * qk_dim + v_dim) // v_dim)
            w_out = ret_w_out[j].astype(BF16)
        x1, hn, route, route_t, cnt = _out_router(mix, gate, w_out, xt, norm_ffn[i], moe_w_group[i],
                                                  moe_b_group[i], moe_w_expert[i], moe_b_expert[i])
        xt = x1
        moe = (route, _moe(hn, route_t, cnt, moe_w_gate, moe_w_up, moe_w_down, i))
    return _final_combine(xt, moe[0], moe[1], norm_final).reshape(batch, seq, d)
```

```python
import functools
import math

import jax
import jax.numpy as jnp
from jax import lax
from jax.experimental import pallas as pl
from jax.experimental.pallas import tpu as pltpu
from jax.experimental.pallas import tpu_sc as plsc

F32 = jnp.float32
BF16 = jnp.bfloat16

N_MIXERS = 2
NUM_BUCKETS = 32
MAX_DISTANCE = 128
RET_HEADS = 4
TOP_K_INNER = 2
RMS_EPS = 1e-6
SUBLN_EPS = 1e-5

LANES = 128
SUBLANES = 8
BF16_SUBLANES = 16
LOG2E = 1.4426950408889634
VMEM_LIMIT = 56 * 1024 * 1024
SC_CORES = 2
SC_SUBCORES = 16

TM = 512
TN = 1024
TQ = 256
ATTN_HEADS = 4
RET_C = 256
RET_HP = 2
FF_SPLIT = 2
ROUTER_SPLIT = 2
ROW_BLOCK = 512
SC_CHUNK = 64


def _cparams(sem):
    return pltpu.CompilerParams(dimension_semantics=sem, vmem_limit_bytes=VMEM_LIMIT)


def _pack_bf16_pairs(x):
    h = x.shape[1] // 2
    hi = lax.bitcast_convert_type(x[:, :h].astype(BF16).astype(F32), jnp.uint32)
    lo = lax.bitcast_convert_type(x[:, h:].astype(BF16).astype(F32), jnp.uint32)
    return hi | (lo >> 16)


def _unpack_bf16_pairs(w):
    hi = lax.bitcast_convert_type(w & jnp.uint32(0xFFFF0000), F32)
    lo = lax.bitcast_convert_type(w << 16, F32)
    return hi, lo


def _norm_matmul_kernel(x_ref, *rest, n_rope, moe, layer):
    rest = list(rest)
    w_ref, stage_ref, sem = rest[-3:]
    rest = rest[:-3]
    if moe:
        route_ref, y1_ref, y2_ref = rest[:3]
        x_out_ref = rest.pop()
        rest = rest[3:]
    g_ref, scale_ref, w_hbm = rest[:3]
    if n_rope:
        cos_ref, sa_ref, sb_ref, o_ref = rest[3:]
    else:
        (o_ref,) = rest[3:]
    n_chunks = o_ref.shape[1] // TN

    @pl.when(pl.program_id(0) == 0)
    def _():
        def fetch(j):
            return pltpu.make_async_copy(w_hbm.at[layer, :, pl.ds(j * TN, TN)], stage_ref.at[j % 2],
                                         sem.at[j % 2])

        fetch(0).start()
        for j in range(n_chunks):
            if j + 1 < n_chunks:
                fetch(j + 1).start()
            fetch(j).wait()
            cols = slice(j * TN, (j + 1) * TN)
            w_ref[:, cols] = (stage_ref[j % 2] * scale_ref[:, cols]).astype(BF16)

    x = x_ref[...]
    if moe:
        route = route_ref[...]
        y1 = jnp.concatenate(_unpack_bf16_pairs(y1_ref[...]), axis=1)
        y2 = jnp.concatenate(_unpack_bf16_pairs(y2_ref[...]), axis=1)
        x = x + (route[:, 2:3] * y1 + route[:, 3:4] * y2)
        x_out_ref[...] = x
    xn = (x * lax.rsqrt(jnp.mean(x * x, axis=-1, keepdims=True) + RMS_EPS) * g_ref[...]).astype(BF16)
    for j in range(n_chunks):
        cols = slice(j * TN, (j + 1) * TN)
        acc = jnp.dot(xn, w_ref[:, cols], preferred_element_type=F32)
        if j < n_rope:
            reps = TN // cos_ref.shape[1]
            c = jnp.tile(cos_ref[...], (1, reps))
            sa = jnp.tile(sa_ref[...], (1, reps))
            sb = jnp.tile(sb_ref[...], (1, reps))
            nxt = pltpu.roll(acc, TN - 1, axis=1)
            prv = pltpu.roll(acc, 1, axis=1)
            acc = acc * c + nxt * sa + prv * sb
        o_ref[:, cols] = acc.astype(o_ref.dtype)


def _norm_matmul(x, g, w, layer, col_scale, rope=None, n_rope=0, seq=None, moe=None):
    t, d = x.shape
    n = w.shape[2]
    row = lambda i: (i, 0)
    in_specs = [pl.BlockSpec((TM, d), row)]
    args = [x]
    if moe is not None:
        route, yg = moe
        slot2 = t // TM
        in_specs += [pl.BlockSpec((TM, LANES), row), pl.BlockSpec((TM, d // 2), row),
                     pl.BlockSpec((TM, d // 2), lambda i: (slot2 + i, 0))]
        args += [route, yg, yg]
    in_specs += [
        pl.BlockSpec((1, d), lambda i: (0, 0)),
        pl.BlockSpec((1, n), lambda i: (0, 0)),
        pl.BlockSpec(memory_space=pl.ANY),
    ]
    args += [g.reshape(1, d), col_scale.reshape(1, n).astype(F32), w]
    if rope is not None:
        nb = seq // TM
        dk = rope[0].shape[1]
        in_specs += [pl.BlockSpec((TM, dk), lambda i: (i % nb, 0))] * 3
        args += list(rope)
    out_specs = [pl.BlockSpec((TM, n), row)]
    out_shape = [jax.ShapeDtypeStruct((t, n), BF16)]
    if moe is not None:
        out_specs.append(pl.BlockSpec((TM, d), row))
        out_shape.append(jax.ShapeDtypeStruct((t, d), F32))
    out = pl.pallas_call(
        functools.partial(_norm_matmul_kernel, n_rope=n_rope, moe=moe is not None, layer=layer),
        grid=(t // TM,),
        in_specs=in_specs,
        out_specs=out_specs,
        out_shape=out_shape,
        scratch_shapes=[pltpu.VMEM((d, n), BF16), pltpu.VMEM((2, d, TN), F32), pltpu.SemaphoreType.DMA((2,))],
        compiler_params=_cparams(("arbitrary",)),
    )(*args)
    return out if moe is not None else out[0]


def _diff_attn_kernel(q_ref, k_ref, v_ref, bias_ref, lam_ref, g_ref, o_ref,
                      vt_ref, m_ref, acc_ref, s_ref, *, lambda_init):
    seq = q_ref.shape[0]
    hp, _, _, tq = bias_ref.shape
    dv = q_ref.shape[1] // hp
    dh = dv // 2
    nq = seq // tq

    def head(hh):
        return slice(hh * dv, (hh + 1) * dv)

    extra = vt_ref.shape[2] - dv
    ones_row = jnp.where(lax.broadcasted_iota(jnp.int32, (extra, tq), 0) == 0, 1.0, 0.0).astype(BF16)
    for hh in range(hp):
        for j in range(nq):
            vt_ref[hh, j, :dv, :] = v_ref[j * tq:(j + 1) * tq, head(hh)].astype(F32).T.astype(BF16)
            vt_ref[hh, j, dv:, :] = ones_row

    lq = lam_ref[...]
    lam = (jnp.exp(jnp.sum(lq[0:1] * lq[1:2], axis=-1, keepdims=True))
           - jnp.exp(jnp.sum(lq[2:3] * lq[3:4], axis=-1, keepdims=True)) + lambda_init)
    lane = lax.broadcasted_iota(jnp.int32, (tq, dv), 1)

    def query_maps(qi):
        qrows = pl.ds(pl.multiple_of(qi * tq, tq), tq)
        qs = []
        for hh in range(hp):
            q = q_ref[qrows, head(hh)]
            zero = jnp.zeros_like(q)
            qs.append(jnp.concatenate([jnp.where(lane < dh, q, zero), jnp.where(lane >= dh, q, zero)],
                                      axis=0))
        return qs

    def score_dots(j, qs):
        rows = pl.ds(pl.multiple_of(j * tq, tq), tq)
        for hh in range(hp):
            s_ref[hh] = lax.dot_general(k_ref[rows, head(hh)], qs[hh], (((1,), (1,)), ((), ())),
                                        preferred_element_type=F32)

    score_dots(0, query_maps(0))

    def q_body(qi, carry):
        qrows = pl.ds(pl.multiple_of(qi * tq, tq), tq)
        qs = query_maps(qi)
        qs_next = query_maps(jnp.minimum(qi + 1, nq - 1))

        m_ref[...] = jnp.full(m_ref.shape, -jnp.inf, F32)
        acc_ref[...] = jnp.zeros(acc_ref.shape, F32)

        def step(j, tile, prefetch):
            nxt, qn = (j + 1, qs) if prefetch == "block" else (0, qs_next)
            rows = pl.ds(pl.multiple_of(nxt * tq, tq), tq)
            for hh in range(hp):
                s = s_ref[hh]
                if tile is not None:
                    bias = bias_ref[hh, tile]
                    s = s + jnp.concatenate([bias, bias], axis=1)
                m_old = m_ref[hh]
                m_new = jnp.maximum(m_old, jnp.max(s, axis=0, keepdims=True))
                alpha = jnp.exp2(m_old - m_new)
                m_ref[hh] = m_new
                p = jnp.exp2(s - m_new).astype(BF16)
                s_ref[hh] = lax.dot_general(k_ref[rows, head(hh)], qn[hh], (((1,), (1,)), ((), ())),
                                            preferred_element_type=F32)
                acc_ref[hh] = alpha * acc_ref[hh] + jnp.dot(vt_ref[hh, j], p, preferred_element_type=F32)

        def far_body(j, c):
            step(j, None, "block")
            return c

        lax.fori_loop(0, jnp.maximum(qi - 1, 0), far_body, 0)

        @pl.when(qi >= 1)
        def _():
            step(qi - 1, 1, "block")

        step(qi, 0, "tile")

        for hh in range(hp):
            acc = acc_ref[hh]
            on = acc[:dv] / acc[dv:dv + 1]
            ot = on[:, :tq] - lam * on[:, tq:]
            ot = ot * lax.rsqrt(jnp.mean(ot * ot, axis=0, keepdims=True) + SUBLN_EPS)
            o_ref[qrows, head(hh)] = (ot.T * g_ref[...] * (1.0 - lambda_init)).astype(o_ref.dtype)
        return carry

    lax.fori_loop(0, nq, q_body, 0)


def _t5_bucket(rel):
    max_exact = NUM_BUCKETS // 2
    n = jnp.maximum(rel, 0)
    is_small = n < max_exact
    nf = jnp.maximum(n, max_exact).astype(F32)
    large = max_exact + (jnp.log(nf / max_exact) / math.log(MAX_DISTANCE / max_exact)
                         * (NUM_BUCKETS - max_exact)).astype(jnp.int32)
    large = jnp.minimum(large, NUM_BUCKETS - 1)
    return jnp.where(is_small, n, large)


def _bias_tiles(rel_table):
    heads = rel_table.shape[1]
    span = 2 * TQ + 1
    bucket = _t5_bucket(jnp.arange(span))
    onehot = (bucket[:, None] == jnp.arange(NUM_BUCKETS)[None, :]).astype(F32)
    table = (rel_table.astype(F32) - rel_table[NUM_BUCKETS - 1].astype(F32)[None, :]) * LOG2E
    by_rel = jnp.dot(onehot, table, precision=lax.Precision.HIGHEST).T
    toep = jnp.tile(by_rel, (1, TQ))[:, :TQ * (span - 1)].reshape(heads, TQ, span - 1)
    kj = jnp.arange(TQ)[:, None]
    qi = jnp.arange(TQ)[None, :]
    diag = jnp.where((qi >= kj)[None], toep[:, :, :TQ], -jnp.inf)
    return jnp.stack([diag, toep[:, :, TQ:]], axis=1)


def _diff_attention(qkv, rel_table, lam_qk, subln_g, lambda_init, batch, seq, heads):
    t = qkv.shape[0]
    d = qkv.shape[1] // 3
    dv = d // heads
    hp = ATTN_HEADS
    assert TQ + 1 >= MAX_DISTANCE and seq % TQ == 0 and dv == LANES and heads % hp == 0
    nq = seq // TQ
    groups = heads // hp
    bias = _bias_tiles(rel_table)
    return pl.pallas_call(
        functools.partial(_diff_attn_kernel, lambda_init=lambda_init),
        grid=(batch, groups),
        in_specs=[
            pl.BlockSpec((seq, hp * dv), lambda b, g: (b, g)),
            pl.BlockSpec((seq, hp * dv), lambda b, g: (b, groups + g)),
            pl.BlockSpec((seq, hp * dv), lambda b, g: (b, 2 * groups + g)),
            pl.BlockSpec((hp, 2, TQ, TQ), lambda b, g: (g, 0, 0, 0)),
            pl.BlockSpec(lam_qk.shape, lambda b, g: (0, 0)),
            pl.BlockSpec((1, dv), lambda b, g: (0, 0)),
        ],
        out_specs=pl.BlockSpec((seq, hp * dv), lambda b, g: (b, g)),
        out_shape=jax.ShapeDtypeStruct((t, d), BF16),
        scratch_shapes=[
            pltpu.VMEM((hp, nq, dv + BF16_SUBLANES, TQ), BF16),
            pltpu.VMEM((hp, 1, 2 * TQ), F32),
            pltpu.VMEM((hp, dv + BF16_SUBLANES, 2 * TQ), F32),
            pltpu.VMEM((hp, TQ, 2 * TQ), F32),
        ],
        compiler_params=_cparams(("parallel", "parallel")),
    )(qkv, qkv, qkv, bias, lam_qk.astype(F32), subln_g.reshape(1, dv).astype(F32))


def _retention_kernel(q_ref, k_ref, v_ref, dec_ref, tab_ref, o_ref, state_ref):
    seq = q_ref.shape[0]
    hp, c, _ = dec_ref.shape
    dk = q_ref.shape[1] // hp
    dv = v_ref.shape[1] // hp
    heads = range(hp)
    state_ref[...] = jnp.zeros(state_ref.shape, F32)

    def body(n, carry):
        rows = pl.ds(pl.multiple_of(n * c, c), c)
        q = [q_ref[rows, hh * dk:(hh + 1) * dk] for hh in heads]
        k = [k_ref[rows, hh * dk:(hh + 1) * dk] for hh in heads]
        v = [v_ref[rows, hh * dv:(hh + 1) * dv] for hh in heads]
        tab = [tab_ref[hh] for hh in heads]
        st = [state_ref[hh] for hh in heads]
        s = [lax.dot_general(q[hh], k[hh], (((1,), (1,)), ((), ())), preferred_element_type=F32)
             for hh in heads]
        cross = [jnp.dot(q[hh], st[hh].astype(BF16), preferred_element_type=F32) for hh in heads]
        kt = [(k[hh].astype(F32) * tab[hh][:, 1:2]).T.astype(BF16) for hh in heads]
        inner = [jnp.dot((s[hh] * dec_ref[hh]).astype(BF16), v[hh], preferred_element_type=F32)
                 for hh in heads]
        for hh in heads:
            state_ref[hh] = st[hh] * tab[hh][0:1, 2:3] + jnp.dot(kt[hh], v[hh], preferred_element_type=F32)
        for hh in heads:
            o = inner[hh] + cross[hh] * tab[hh][:, 0:1]
            o = o * lax.rsqrt(jnp.mean(o * o, axis=-1, keepdims=True) + RMS_EPS)
            o_ref[rows, hh * dv:(hh + 1) * dv] = o.astype(o_ref.dtype)
        return carry

    lax.fori_loop(0, seq // c, body, 0)


def _retention_tables(heads):
    c = RET_C
    log_gamma = jnp.log(1.0 - 2.0 ** (-5.0 - jnp.arange(heads, dtype=F32)))
    pos = jnp.arange(c, dtype=F32)
    dist = pos[:, None] - pos[None, :]
    inner_decay = jnp.where(dist[None] >= 0,
                            jnp.exp(jnp.maximum(dist, 0.0)[None] * log_gamma[:, None, None]), 0.0)
    q_decay = jnp.exp((pos + 1.0)[None] * log_gamma[:, None])
    k_decay = jnp.exp((c - 1.0 - pos)[None] * log_gamma[:, None])
    chunk_decay = jnp.broadcast_to(jnp.exp(c * log_gamma)[:, None], (heads, c))
    tab = jnp.stack([q_decay, k_decay, chunk_decay], axis=-1)
    tab = jnp.pad(tab, ((0, 0), (0, 0), (0, LANES - 3)))
    return inner_decay, tab


def _retention(qkvg, batch, seq, qk_dim, v_dim):
    t = qkvg.shape[0]
    dk = qk_dim // RET_HEADS
    dvh = v_dim // RET_HEADS
    inner_decay, tab = _retention_tables(RET_HEADS)
    hp = RET_HP
    assert RET_HEADS % hp == 0
    groups = RET_HEADS // hp
    kb = qk_dim // (hp * dk)
    vb = 2 * qk_dim // (hp * dvh)
    return pl.pallas_call(
        _retention_kernel,
        grid=(batch, groups),
        in_specs=[
            pl.BlockSpec((seq, hp * dk), lambda b, g: (b, g)),
            pl.BlockSpec((seq, hp * dk), lambda b, g: (b, kb + g)),
            pl.BlockSpec((seq, hp * dvh), lambda b, g: (b, vb + g)),
            pl.BlockSpec((hp, RET_C, RET_C), lambda b, g: (g, 0, 0)),
            pl.BlockSpec((hp, RET_C, LANES), lambda b, g: (g, 0, 0)),
        ],
        out_specs=pl.BlockSpec((seq, hp * dvh), lambda b, g: (b, g)),
        out_shape=jax.ShapeDtypeStruct((t, v_dim), BF16),
        scratch_shapes=[pltpu.VMEM((hp, dk, dvh), F32)],
        compiler_params=_cparams(("parallel", "parallel")),
    )(qkvg, qkvg, qkvg, inner_decay, tab)


def _out_router_kernel(*refs, gated, n_groups, n_experts):
    if gated:
        a_ref, gate_ref, w_ref, x_ref, nf_ref, wr_ref, br_ref = refs[:7]
        x1_ref, hn_ref, route_ref, route_t_ref, cnt_ref, carry_ref = refs[7:]
    else:
        a_ref, w_ref, x_ref, nf_ref, wr_ref, br_ref = refs[:6]
        x1_ref, hn_ref, route_ref, route_t_ref, cnt_ref, carry_ref = refs[6:]
    epg = n_experts // n_groups
    tm = x_ref.shape[0] // ROUTER_SPLIT

    @pl.when(pl.program_id(0) == 0)
    def _():
        carry_ref[...] = jnp.zeros(carry_ref.shape, F32)

    x1_parts = []
    for part in range(ROUTER_SPLIT):
        rows = slice(part * tm, (part + 1) * tm)
        a = a_ref[rows, :]
        if gated:
            gate = gate_ref[rows, :]
            a = gate * jax.nn.sigmoid(gate) * a
        x1_parts.append(x_ref[rows, :] + jnp.dot(a, w_ref[...], preferred_element_type=F32))
    for part in range(ROUTER_SPLIT):
        rows = slice(part * tm, (part + 1) * tm)
        _route_rows(x1_parts[part], rows, nf_ref, wr_ref, br_ref, x1_ref, hn_ref, route_ref, route_t_ref,
                    cnt_ref, carry_ref, n_groups=n_groups, n_experts=n_experts, epg=epg)


def _route_rows(x1, rows, nf_ref, wr_ref, br_ref, x1_ref, hn_ref, route_ref, route_t_ref, cnt_ref,
                carry_ref, *, n_groups, n_experts, epg):
    tm = x1.shape[0]
    x1_ref[rows, :] = x1
    hn = x1 * lax.rsqrt(jnp.mean(x1 * x1, axis=-1, keepdims=True) + RMS_EPS) * nf_ref[...]
    hn_ref[rows, :] = _pack_bf16_pairs(hn)

    hn_hi = hn.astype(BF16)
    hn_lo = (hn - hn_hi.astype(F32)).astype(BF16)
    both = jnp.dot(hn_hi, wr_ref[...], preferred_element_type=F32)
    logits = (both[:, :LANES] + both[:, LANES:]
              + jnp.dot(hn_lo, wr_ref[:, :LANES], preferred_element_type=F32)) + br_ref[...]

    n_rows = -(-(n_groups + n_experts) // SUBLANES) * SUBLANES
    lt = logits.T[:n_rows, :]
    row = lax.broadcasted_iota(jnp.int32, lt.shape, 0)
    big = jnp.int32(LANES)
    neg = jnp.float32(-jnp.inf)

    gl = jnp.where(row < n_groups, lt, neg)
    gmax = jnp.max(gl, axis=0, keepdims=True)
    g_p = 1.0 / jnp.sum(jnp.exp(gl - gmax), axis=0, keepdims=True)
    g_idx = jnp.min(jnp.where(gl == gmax, row, big), axis=0, keepdims=True)

    lo = n_groups + g_idx * epg
    el = jnp.where((row >= lo) & (row < lo + epg), lt, neg)
    v1 = jnp.max(el, axis=0, keepdims=True)
    i1 = jnp.min(jnp.where(el == v1, row, big), axis=0, keepdims=True)
    el2 = jnp.where(row == i1, neg, el)
    v2 = jnp.max(el2, axis=0, keepdims=True)
    i2 = jnp.min(jnp.where(el2 == v2, row, big), axis=0, keepdims=True)
    tt = jnp.exp(v2 - v1)
    w1 = g_p / (1.0 + tt)
    w2 = g_p * tt / (1.0 + tt)
    e1 = i1 - n_groups
    e2 = i2 - n_groups

    erow = lax.broadcasted_iota(jnp.int32, (n_experts, tm), 0)
    oh1 = erow == e1
    oh2 = erow == e2
    onehot = jnp.where(oh1 | oh2, 1.0, 0.0)
    before = lax.broadcasted_iota(jnp.int32, (tm, tm), 0) < lax.broadcasted_iota(jnp.int32, (tm, tm), 1)
    tri = jnp.where(before, 1.0, 0.0).astype(BF16)
    carry = carry_ref[...]
    base = jnp.dot(onehot.astype(BF16), tri, preferred_element_type=F32) + carry[:, 0:1]
    rank1 = jnp.sum(jnp.where(oh1, base, 0.0), axis=0, keepdims=True)
    rank2 = jnp.sum(jnp.where(oh2, base, 0.0), axis=0, keepdims=True)
    carry = carry + jnp.sum(onehot, axis=1, keepdims=True)
    carry_ref[...] = carry
    cnt_ref[...] = carry

    frow = lax.broadcasted_iota(jnp.int32, (LANES, tm), 0)
    out = jnp.zeros((LANES, tm), F32)
    for idx, val in enumerate((e1.astype(F32), e2.astype(F32), w1, w2, rank1, rank2)):
        out = jnp.where(frow == idx, val, out)
    route_t_ref[:, rows] = out[:route_t_ref.shape[0], :]
    route_ref[rows, :] = out.T


def _out_router(a, gate, w, x, nf, w_group, b_group, w_expert, b_expert):
    t, d = x.shape
    k = w.shape[0]
    n_groups = w_group.shape[1]
    n_experts = w_expert.shape[1]
    assert n_groups + n_experts <= LANES
    pad = LANES - n_groups - n_experts
    wr = jnp.pad(jnp.concatenate([w_group, w_expert], axis=1).astype(F32), ((0, 0), (0, pad)))
    wr_hi = wr.astype(BF16)
    wr = jnp.concatenate([wr_hi, (wr - wr_hi.astype(F32)).astype(BF16)], axis=1)
    br = jnp.pad(jnp.concatenate([b_group, b_expert]).astype(F32), (0, pad)).reshape(1, LANES)
    gated = gate is not None
    row = lambda i: (i, 0)
    fixed = lambda i: (0, 0)
    in_specs = [pl.BlockSpec((TM, k), row)]
    args = [a]
    if gated:
        gate_arr, gate_col = gate
        in_specs.append(pl.BlockSpec((TM, k), lambda i: (i, gate_col)))
        args.append(gate_arr)
    in_specs += [
        pl.BlockSpec((k, d), fixed),
        pl.BlockSpec((TM, d), row),
        pl.BlockSpec((1, d), fixed),
        pl.BlockSpec((d, 2 * LANES), fixed),
        pl.BlockSpec((1, LANES), fixed),
    ]
    args += [w, x, nf.reshape(1, d), wr, br]
    return pl.pallas_call(
        functools.partial(_out_router_kernel, gated=gated, n_groups=n_groups, n_experts=n_experts),
        grid=(t // TM,),
        in_specs=in_specs,
        out_specs=[
            pl.BlockSpec((TM, d), row),
            pl.BlockSpec((TM, d // 2), row),
            pl.BlockSpec((TM, LANES), row),
            pl.BlockSpec((SUBLANES, TM), lambda i: (0, i)),
            pl.BlockSpec((n_experts, LANES), fixed),
        ],
        out_shape=[
            jax.ShapeDtypeStruct((t, d), F32),
            jax.ShapeDtypeStruct((t, d // 2), jnp.uint32),
            jax.ShapeDtypeStruct((t, LANES), F32),
            jax.ShapeDtypeStruct((SUBLANES, t), F32),
            jax.ShapeDtypeStruct((n_experts, LANES), F32),
        ],
        scratch_shapes=[pltpu.VMEM((n_experts, LANES), F32)],
        compiler_params=_cparams(("arbitrary",)),
    )(*args)


def _sc_mesh():
    return plsc.VectorSubcoreMesh(core_axis_name="c", subcore_axis_name="s",
                                  num_cores=SC_CORES, num_subcores=SC_SUBCORES)


def _sc_worker():
    return lax.axis_index("s") * SC_CORES + lax.axis_index("c")


def _sc_scatter_rows(src, d1, d2, n_rows):
    t, d = src.shape
    workers = SC_CORES * SC_SUBCORES
    per_w = t // workers
    n_ch = per_w // SC_CHUNK
    assert t % (workers * SC_CHUNK) == 0
    idx = jnp.stack([d1, d2]).reshape(2, workers, n_ch, SC_CHUNK)

    @functools.partial(
        pl.kernel, mesh=_sc_mesh(),
        out_type=jax.ShapeDtypeStruct((n_rows, d), src.dtype),
        scratch_types=[pltpu.VMEM((2, n_ch, SC_CHUNK), jnp.int32),
                       pltpu.VMEM((SC_CHUNK, d), src.dtype),
                       pltpu.VMEM((SC_CHUNK, d), src.dtype),
                       pltpu.SemaphoreType.DMA, pltpu.SemaphoreType.DMA],
    )
    def scatter(src_hbm, idx_hbm, out_hbm, idx_v, rows_a, rows_b, sem_a, sem_b):
        wid = _sc_worker()
        base = wid * per_w
        for s in range(2):
            pltpu.sync_copy(idx_hbm.at[s, wid], idx_v.at[s])

        def move(c, rows_v, sem):
            pltpu.sync_copy(src_hbm.at[pl.ds(base + c * SC_CHUNK, SC_CHUNK)], rows_v)
            return [pltpu.async_copy(rows_v, out_hbm.at[idx_v.at[s, c]], sem) for s in range(2)]

        @pl.loop(0, n_ch, step=2)
        def _(c):
            first = move(c, rows_a, sem_a)
            second = move(c + 1, rows_b, sem_b)
            for cp in first + second:
                cp.wait()

    return scatter(src, idx)


def _sc_gather_rows(table, idx):
    b = idx.shape[0]
    d = table.shape[1]
    workers = SC_CORES * SC_SUBCORES
    per_w = b // workers
    n_ch = per_w // SC_CHUNK
    assert b % (workers * SC_CHUNK * 2) == 0

    @functools.partial(
        pl.kernel, mesh=_sc_mesh(),
        out_type=jax.ShapeDtypeStruct((b, d), table.dtype),
        scratch_types=[pltpu.VMEM((per_w,), jnp.int32),
                       pltpu.VMEM((SC_CHUNK, d), table.dtype),
                       pltpu.VMEM((SC_CHUNK, d), table.dtype),
                       pltpu.SemaphoreType.DMA, pltpu.SemaphoreType.DMA],
    )
    def gather(table_hbm, idx_hbm, out_hbm, idx_v, rows_a, rows_b, sem_a, sem_b):
        base = _sc_worker() * per_w
        pltpu.sync_copy(idx_hbm.at[pl.ds(base, per_w)], idx_v)

        def fetch(c, rows_v, sem):
            return pltpu.async_copy(table_hbm.at[idx_v.at[pl.ds(c * SC_CHUNK, SC_CHUNK)]], rows_v, sem)

        def store(c, rows_v):
            pltpu.sync_copy(rows_v, out_hbm.at[pl.ds(base + c * SC_CHUNK, SC_CHUNK)])

        @pl.loop(0, n_ch, step=2)
        def _(c):
            first = fetch(c, rows_a, sem_a)
            second = fetch(c + 1, rows_b, sem_b)
            first.wait()
            store(c, rows_a)
            second.wait()
            store(c + 1, rows_b)

    return gather(table, idx)


def _expert_kernel(fb_ref, nblk_ref, nv_ref, nu_ref, xs_ref, wg_ref, wu_ref, wd_ref, ys_ref,
                   wgb_ref, wub_ref, wdb_ref, xbuf, ybuf, xsem, ysem):
    e = pl.program_id(0)
    rb = xbuf.shape[1]
    nb = ys_ref.shape[0] // rb
    n_used = nu_ref[0]

    def x_copy(g, slot):
        rows = pl.ds(pl.multiple_of(g * rb, rb), rb)
        return pltpu.make_async_copy(xs_ref.at[rows], xbuf.at[slot], xsem.at[slot])

    def y_copy(g, slot):
        rows = pl.ds(pl.multiple_of(g * rb, rb), rb)
        return pltpu.make_async_copy(ybuf.at[slot], ys_ref.at[rows], ysem.at[slot])

    @pl.when(e == 0)
    def _():
        x_copy(0, 0).start()

    @pl.when(nblk_ref[e] > 0)
    def _():
        wgb_ref[...] = wg_ref[0, 0].astype(BF16)
        wub_ref[...] = wu_ref[0, 0].astype(BF16)
        wdb_ref[...] = wd_ref[0, 0].astype(BF16)

    def block(b, carry):
        g = fb_ref[e] + b
        slot = lax.rem(g, 2)
        x_copy(g, slot).wait()

        @pl.when(g + 1 < n_used)
        def _():
            x_copy(g + 1, 1 - slot).start()

        @pl.when(g >= 2)
        def _():
            y_copy(g - 2, slot).wait()

        nv = nv_ref[g]

        def ffn(rows):
            valid = lax.broadcasted_iota(jnp.int32, (rows, 1), 0) < nv
            x_hi, x_lo = _unpack_bf16_pairs(jnp.where(valid, xbuf[slot, :rows, :], jnp.uint32(0)))
            x = jnp.concatenate([x_hi.astype(BF16), x_lo.astype(BF16)], axis=1)
            slab = wgb_ref.shape[1] // FF_SPLIT
            hmid = []
            for c in range(FF_SPLIT):
                cols = slice(c * slab, (c + 1) * slab)
                gate = jnp.dot(x, wgb_ref[:, cols], preferred_element_type=F32)
                up = jnp.dot(x, wub_ref[:, cols], preferred_element_type=F32)
                hmid.append((gate * jax.nn.sigmoid(gate) * up).astype(BF16))
            hmid = jnp.concatenate(hmid, axis=1)
            ybuf[slot, :rows, :] = _pack_bf16_pairs(jnp.dot(hmid, wdb_ref[...], preferred_element_type=F32))
            if rows < rb:
                ybuf[slot, rows:, :] = jnp.zeros((rb - rows, ybuf.shape[2]), ybuf.dtype)

        @pl.when(nv > rb // 2)
        def _():
            ffn(rb)

        @pl.when(nv <= rb // 2)
        def _():
            ffn(rb // 2)

        y_copy(g, slot).start()
        return carry

    lax.fori_loop(0, nblk_ref[e], block, 0)

    @pl.when(e == pl.num_programs(0) - 1)
    def _():
        @pl.when(n_used >= 2)
        def _():
            y_copy(n_used - 2, lax.rem(n_used, 2)).wait()

        y_copy(n_used - 1, lax.rem(n_used - 1, 2)).wait()
        ybuf[0] = jnp.zeros(ybuf.shape[1:], ybuf.dtype)

        def tail_start(g, carry):
            y_copy(g, 0).start()
            return carry

        def tail_wait(g, carry):
            y_copy(g, 0).wait()
            return carry

        lax.fori_loop(n_used, nb, tail_start, 0)
        lax.fori_loop(n_used, nb, tail_wait, 0)


def _experts(xs, first_block, n_blocks, n_valid, n_used, w_gate, w_up, w_down, layer):
    n_rows, dp = xs.shape
    d = 2 * dp
    n_experts, ff = w_gate.shape[1], w_gate.shape[3]
    wsel = lambda e, *_: (layer, e, 0, 0)
    grid_spec = pltpu.PrefetchScalarGridSpec(
        num_scalar_prefetch=4,
        grid=(n_experts,),
        in_specs=[
            pl.BlockSpec(memory_space=pl.ANY),
            pl.BlockSpec((1, 1, d, ff), wsel),
            pl.BlockSpec((1, 1, d, ff), wsel),
            pl.BlockSpec((1, 1, ff, d), wsel),
        ],
        out_specs=pl.BlockSpec(memory_space=pl.ANY),
        scratch_shapes=[pltpu.VMEM((d, ff), BF16), pltpu.VMEM((d, ff), BF16), pltpu.VMEM((ff, d), BF16),
                        pltpu.VMEM((2, ROW_BLOCK, dp), xs.dtype), pltpu.VMEM((2, ROW_BLOCK, dp), xs.dtype),
                        pltpu.SemaphoreType.DMA((2,)), pltpu.SemaphoreType.DMA((2,))],
    )
    return pl.pallas_call(
        _expert_kernel,
        grid_spec=grid_spec,
        out_shape=jax.ShapeDtypeStruct((n_rows, dp), xs.dtype),
        compiler_params=_cparams(("arbitrary",)),
    )(first_block, n_blocks, n_valid, n_used, xs, w_gate, w_up, w_down)


def _final_combine_kernel(x_ref, route_ref, y1_ref, y2_ref, nf_ref, o_ref):
    route = route_ref[...]
    y1 = jnp.concatenate(_unpack_bf16_pairs(y1_ref[...]), axis=1)
    y2 = jnp.concatenate(_unpack_bf16_pairs(y2_ref[...]), axis=1)
    y = x_ref[...] + (route[:, 2:3] * y1 + route[:, 3:4] * y2)
    o_ref[...] = y * lax.rsqrt(jnp.mean(y * y, axis=-1, keepdims=True) + RMS_EPS) * nf_ref[...]


def _final_combine(x1, route, yg, nf):
    t, d = x1.shape
    slot2 = t // TM
    row = lambda i: (i, 0)
    return pl.pallas_call(
        _final_combine_kernel,
        grid=(t // TM,),
        in_specs=[
            pl.BlockSpec((TM, d), row),
            pl.BlockSpec((TM, LANES), row),
            pl.BlockSpec((TM, d // 2), row),
            pl.BlockSpec((TM, d // 2), lambda i: (slot2 + i, 0)),
            pl.BlockSpec((1, d), lambda i: (0, 0)),
        ],
        out_specs=pl.BlockSpec((TM, d), row),
        out_shape=jax.ShapeDtypeStruct((t, d), F32),
        compiler_params=_cparams(("parallel",)),
    )(x1, route, yg, yg, nf.reshape(1, d).astype(F32))


def _moe(hn, route_t, cnt, w_gate, w_up, w_down, layer):
    t, d = hn.shape
    n_experts = w_gate.shape[1]
    ids = jnp.arange(n_experts, dtype=jnp.int32)
    e1, e2, rank1, rank2 = (route_t[f].astype(jnp.int32) for f in (0, 1, 4, 5))
    counts = cnt[:, 0].astype(jnp.int32)
    padded = ((counts + ROW_BLOCK - 1) // ROW_BLOCK) * ROW_BLOCK
    pad_end = jnp.cumsum(padded)
    pad_start = pad_end - padded

    def dest(e, rank):
        return jnp.sum(jnp.where(e[:, None] == ids[None, :], pad_start[None, :], 0), axis=1) + rank

    d1 = dest(e1, rank1)
    d2 = dest(e2, rank2)
    nb = (t * TOP_K_INNER) // ROW_BLOCK + n_experts
    n_used = (pad_end[-1:] // ROW_BLOCK).astype(jnp.int32)
    first_row = jnp.arange(nb, dtype=jnp.int32) * ROW_BLOCK
    block_e = jnp.sum((pad_end[None, :] <= first_row[:, None]).astype(jnp.int32), axis=1)
    block_e = jnp.minimum(block_e, n_experts - 1)
    row_end = jnp.sum(jnp.where(block_e[:, None] == ids[None, :], (pad_start + counts)[None, :], 0), axis=1)
    n_valid = jnp.clip(row_end - first_row, 0, ROW_BLOCK).astype(jnp.int32)
    xs = _sc_scatter_rows(hn, d1, d2, nb * ROW_BLOCK)
    ys = _experts(xs, (pad_start // ROW_BLOCK).astype(jnp.int32), (padded // ROW_BLOCK).astype(jnp.int32),
                  n_valid, n_used, w_gate, w_up, w_down, layer)
    return _sc_gather_rows(ys, jnp.concatenate([d1, d2]))


def _diff_lambda_init(layer_idx):
    return 0.8 - 0.6 * math.exp(-0.3 * layer_idx)


def _rope_tables(seq, dk):
    angle = 1.0 / (10000.0 ** jnp.linspace(0.0, 1.0, dk // 2, dtype=F32))
    angle = jnp.repeat(angle, 2)
    phase = jnp.arange(seq, dtype=F32)[:, None] * angle[None, :]
    sin, cos = jnp.sin(phase), jnp.cos(phase)
    even = (jnp.arange(dk) % 2 == 0)[None, :]
    return cos, jnp.where(even, -sin, 0.0), jnp.where(even, 0.0, sin)


def kernel(x, rel_bias_table, norm_mix, norm_ffn, norm_final, diff_w_in, diff_lambda_qk, diff_subln,
           diff_w_out, ret_w_in, ret_w_out, moe_w_group, moe_b_group, moe_w_expert, moe_b_expert,
           moe_w_gate, moe_w_up, moe_w_down):
    batch, seq, d = x.shape
    depth = norm_mix.shape[0]
    t = batch * seq
    heads = rel_bias_table.shape[1]
    dh = diff_lambda_qk.shape[-1]
    qk_dim = (ret_w_in.shape[2] - 2 * ret_w_out.shape[1]) // 2
    v_dim = ret_w_out.shape[1]
    dk = qk_dim // RET_HEADS
    xt = x.reshape(t, d)
    moe = None
    for i in range(depth):
        j = i // N_MIXERS
        if i % N_MIXERS == 0:
            col_scale = jnp.concatenate([jnp.full((d,), dh ** -0.5 * LOG2E, F32), jnp.ones((2 * d,), F32)])
            qkv = _norm_matmul(xt, norm_mix[i], diff_w_in, j, col_scale, moe=moe)
            if moe is not None:
                qkv, xt = qkv
            mix = _diff_attention(qkv, rel_bias_table, diff_lambda_qk[j], diff_subln[j],
                                  _diff_lambda_init(i), batch, seq, heads)
            gate = None
            w_out = diff_w_out[j].astype(BF16)
        else:
            n_in = ret_w_in.shape[2]
            col_scale = jnp.concatenate([jnp.ones((qk_dim,), F32), jnp.full((qk_dim,), dk ** -0.5, F32),
                                         jnp.ones((n_in - 2 * qk_dim,), F32)])
            qkvg = _norm_matmul(xt, norm_mix[i], ret_w_in, j, col_scale, rope=_rope_tables(seq, dk),
                                n_rope=2 * qk_dim // TN, seq=seq, moe=moe)
            if moe is not None:
                qkvg, xt = qkvg
            mix = _retention(qkvg, batch, seq, qk_dim, v_dim)
            gate = (qkvg, (2 * qk_dim + v_dim) // v_dim)
            w_out = ret_w_out[j].astype(BF16)
        x1, hn, route, route_t, cnt = _out_router(mix, gate, w_out, xt, norm_ffn[i], moe_w_group[i],
                                                  moe_b_group[i], moe_w_expert[i], moe_b_expert[i])
        xt = x1
        moe = (route, _moe(hn, route_t, cnt, moe_w_gate, moe_w_up, moe_w_down, i))
    return _final_combine(xt, moe[0], moe[1], norm_final).reshape(batch, seq, d)
```

```python
import functools
import math

import jax
import jax.numpy as jnp
from jax import lax
from jax.experimental import pallas as pl
from jax.experimental.pallas import tpu as pltpu
from jax.experimental.pallas import tpu_sc as plsc

F32 = jnp.float32
BF16 = jnp.bfloat16

N_MIXERS = 2
NUM_BUCKETS = 32
MAX_DISTANCE = 128
RET_HEADS = 4
TOP_K_INNER = 2
RMS_EPS = 1e-6
SUBLN_EPS = 1e-5

LANES = 128
SUBLANES = 8
BF16_SUBLANES = 16
LOG2E = 1.4426950408889634
VMEM_LIMIT = 56 * 1024 * 1024
SC_CORES = 2
SC_SUBCORES = 16

TM = 512
TN = 1024
TQ = 256
ATTN_HEADS = 8
RET_C = 256
RET_HP = 2
FF_SPLIT = 2
ROUTER_SPLIT = 2
ROW_BLOCK = 512
SC_CHUNK = 64


def _cparams(sem):
    return pltpu.CompilerParams(dimension_semantics=sem, vmem_limit_bytes=VMEM_LIMIT)


def _pack_bf16_pairs(x):
    h = x.shape[1] // 2
    hi = lax.bitcast_convert_type(x[:, :h].astype(BF16).astype(F32), jnp.uint32)
    lo = lax.bitcast_convert_type(x[:, h:].astype(BF16).astype(F32), jnp.uint32)
    return hi | (lo >> 16)


def _unpack_bf16_pairs(w):
    hi = lax.bitcast_convert_type(w & jnp.uint32(0xFFFF0000), F32)
    lo = lax.bitcast_convert_type(w << 16, F32)
    return hi, lo


def _norm_matmul_kernel(x_ref, *rest, n_rope, moe, layer):
    rest = list(rest)
    w_ref, stage_ref, sem = rest[-3:]
    rest = rest[:-3]
    if moe:
        route_ref, y1_ref, y2_ref = rest[:3]
        x_out_ref = rest.pop()
        rest = rest[3:]
    g_ref, scale_ref, w_hbm = rest[:3]
    if n_rope:
        cos_ref, sa_ref, sb_ref, o_ref = rest[3:]
    else:
        (o_ref,) = rest[3:]
    n_chunks = o_ref.shape[1] // TN

    @pl.when(pl.program_id(0) == 0)
    def _():
        def fetch(j):
            return pltpu.make_async_copy(w_hbm.at[layer, :, pl.ds(j * TN, TN)], stage_ref.at[j % 2],
                                         sem.at[j % 2])

        fetch(0).start()
        for j in range(n_chunks):
            if j + 1 < n_chunks:
                fetch(j + 1).start()
            fetch(j).wait()
            cols = slice(j * TN, (j + 1) * TN)
            w_ref[:, cols] = (stage_ref[j % 2] * scale_ref[:, cols]).astype(BF16)

    x = x_ref[...]
    if moe:
        route = route_ref[...]
        y1 = jnp.concatenate(_unpack_bf16_pairs(y1_ref[...]), axis=1)
        y2 = jnp.concatenate(_unpack_bf16_pairs(y2_ref[...]), axis=1)
        x = x + (route[:, 2:3] * y1 + route[:, 3:4] * y2)
        x_out_ref[...] = x
    xn = (x * lax.rsqrt(jnp.mean(x * x, axis=-1, keepdims=True) + RMS_EPS) * g_ref[...]).astype(BF16)
    for j in range(n_chunks):
        cols = slice(j * TN, (j + 1) * TN)
        acc = jnp.dot(xn, w_ref[:, cols], preferred_element_type=F32)
        if j < n_rope:
            reps = TN // cos_ref.shape[1]
            c = jnp.tile(cos_ref[...], (1, reps))
            sa = jnp.tile(sa_ref[...], (1, reps))
            sb = jnp.tile(sb_ref[...], (1, reps))
            nxt = pltpu.roll(acc, TN - 1, axis=1)
            prv = pltpu.roll(acc, 1, axis=1)
            acc = acc * c + nxt * sa + prv * sb
        o_ref[:, cols] = acc.astype(o_ref.dtype)


def _norm_matmul(x, g, w, layer, col_scale, rope=None, n_rope=0, seq=None, moe=None):
    t, d = x.shape
    n = w.shape[2]
    row = lambda i: (i, 0)
    in_specs = [pl.BlockSpec((TM, d), row)]
    args = [x]
    if moe is not None:
        route, yg = moe
        slot2 = t // TM
        in_specs += [pl.BlockSpec((TM, LANES), row), pl.BlockSpec((TM, d // 2), row),
                     pl.BlockSpec((TM, d // 2), lambda i: (slot2 + i, 0))]
        args += [route, yg, yg]
    in_specs += [
        pl.BlockSpec((1, d), lambda i: (0, 0)),
        pl.BlockSpec((1, n), lambda i: (0, 0)),
        pl.BlockSpec(memory_space=pl.ANY),
    ]
    args += [g.reshape(1, d), col_scale.reshape(1, n).astype(F32), w]
    if rope is not None:
        nb = seq // TM
        dk = rope[0].shape[1]
        in_specs += [pl.BlockSpec((TM, dk), lambda i: (i % nb, 0))] * 3
        args += list(rope)
    out_specs = [pl.BlockSpec((TM, n), row)]
    out_shape = [jax.ShapeDtypeStruct((t, n), BF16)]
    if moe is not None:
        out_specs.append(pl.BlockSpec((TM, d), row))
        out_shape.append(jax.ShapeDtypeStruct((t, d), F32))
    out = pl.pallas_call(
        functools.partial(_norm_matmul_kernel, n_rope=n_rope, moe=moe is not None, layer=layer),
        grid=(t // TM,),
        in_specs=in_specs,
        out_specs=out_specs,
        out_shape=out_shape,
        scratch_shapes=[pltpu.VMEM((d, n), BF16), pltpu.VMEM((2, d, TN), F32), pltpu.SemaphoreType.DMA((2,))],
        compiler_params=_cparams(("arbitrary",)),
    )(*args)
    return out if moe is not None else out[0]


def _diff_attn_kernel(q_ref, k_ref, v_ref, bias_ref, lam_ref, g_ref, o_ref,
                      vt_ref, m_ref, acc_ref, s_ref, *, lambda_init):
    seq = q_ref.shape[0]
    hp, _, _, tq = bias_ref.shape
    dv = q_ref.shape[1] // hp
    dh = dv // 2
    nq = seq // tq

    def head(hh):
        return slice(hh * dv, (hh + 1) * dv)

    extra = vt_ref.shape[2] - dv
    ones_row = jnp.where(lax.broadcasted_iota(jnp.int32, (extra, tq), 0) == 0, 1.0, 0.0).astype(BF16)
    for hh in range(hp):
        for j in range(nq):
            vt_ref[hh, j, :dv, :] = v_ref[j * tq:(j + 1) * tq, head(hh)].astype(F32).T.astype(BF16)
            vt_ref[hh, j, dv:, :] = ones_row

    lq = lam_ref[...]
    lam = (jnp.exp(jnp.sum(lq[0:1] * lq[1:2], axis=-1, keepdims=True))
           - jnp.exp(jnp.sum(lq[2:3] * lq[3:4], axis=-1, keepdims=True)) + lambda_init)
    lane = lax.broadcasted_iota(jnp.int32, (tq, dv), 1)

    def query_maps(qi):
        qrows = pl.ds(pl.multiple_of(qi * tq, tq), tq)
        qs = []
        for hh in range(hp):
            q = q_ref[qrows, head(hh)]
            zero = jnp.zeros_like(q)
            qs.append(jnp.concatenate([jnp.where(lane < dh, q, zero), jnp.where(lane >= dh, q, zero)],
                                      axis=0))
        return qs

    def score_dots(j, qs):
        rows = pl.ds(pl.multiple_of(j * tq, tq), tq)
        for hh in range(hp):
            s_ref[hh] = lax.dot_general(k_ref[rows, head(hh)], qs[hh], (((1,), (1,)), ((), ())),
                                        preferred_element_type=F32)

    score_dots(0, query_maps(0))

    def q_body(qi, carry):
        qrows = pl.ds(pl.multiple_of(qi * tq, tq), tq)
        qs = query_maps(qi)
        qs_next = query_maps(jnp.minimum(qi + 1, nq - 1))

        m_ref[...] = jnp.full(m_ref.shape, -jnp.inf, F32)
        acc_ref[...] = jnp.zeros(acc_ref.shape, F32)

        def step(j, tile, prefetch):
            nxt, qn = (j + 1, qs) if prefetch == "block" else (0, qs_next)
            rows = pl.ds(pl.multiple_of(nxt * tq, tq), tq)
            for hh in range(hp):
                s = s_ref[hh]
                if tile is not None:
                    bias = bias_ref[hh, tile]
                    s = s + jnp.concatenate([bias, bias], axis=1)
                m_old = m_ref[hh]
                m_new = jnp.maximum(m_old, jnp.max(s, axis=0, keepdims=True))
                alpha = jnp.exp2(m_old - m_new)
                m_ref[hh] = m_new
                p = jnp.exp2(s - m_new).astype(BF16)
                s_ref[hh] = lax.dot_general(k_ref[rows, head(hh)], qn[hh], (((1,), (1,)), ((), ())),
                                            preferred_element_type=F32)
                acc_ref[hh] = alpha * acc_ref[hh] + jnp.dot(vt_ref[hh, j], p, preferred_element_type=F32)

        def far_body(j, c):
            step(j, None, "block")
            return c

        lax.fori_loop(0, jnp.maximum(qi - 1, 0), far_body, 0)

        @pl.when(qi >= 1)
        def _():
            step(qi - 1, 1, "block")

        step(qi, 0, "tile")

        for hh in range(hp):
            acc = acc_ref[hh]
            on = acc[:dv] / acc[dv:dv + 1]
            ot = on[:, :tq] - lam * on[:, tq:]
            ot = ot * lax.rsqrt(jnp.mean(ot * ot, axis=0, keepdims=True) + SUBLN_EPS)
            o_ref[qrows, head(hh)] = (ot.T * g_ref[...] * (1.0 - lambda_init)).astype(o_ref.dtype)
        return carry

    lax.fori_loop(0, nq, q_body, 0)


def _t5_bucket(rel):
    max_exact = NUM_BUCKETS // 2
    n = jnp.maximum(rel, 0)
    is_small = n < max_exact
    nf = jnp.maximum(n, max_exact).astype(F32)
    large = max_exact + (jnp.log(nf / max_exact) / math.log(MAX_DISTANCE / max_exact)
                         * (NUM_BUCKETS - max_exact)).astype(jnp.int32)
    large = jnp.minimum(large, NUM_BUCKETS - 1)
    return jnp.where(is_small, n, large)


def _bias_tiles(rel_table):
    heads = rel_table.shape[1]
    span = 2 * TQ + 1
    bucket = _t5_bucket(jnp.arange(span))
    onehot = (bucket[:, None] == jnp.arange(NUM_BUCKETS)[None, :]).astype(F32)
    table = (rel_table.astype(F32) - rel_table[NUM_BUCKETS - 1].astype(F32)[None, :]) * LOG2E
    by_rel = jnp.dot(onehot, table, precision=lax.Precision.HIGHEST).T
    toep = jnp.tile(by_rel, (1, TQ))[:, :TQ * (span - 1)].reshape(heads, TQ, span - 1)
    kj = jnp.arange(TQ)[:, None]
    qi = jnp.arange(TQ)[None, :]
    diag = jnp.where((qi >= kj)[None], toep[:, :, :TQ], -jnp.inf)
    return jnp.stack([diag, toep[:, :, TQ:]], axis=1)


def _diff_attention(qkv, rel_table, lam_qk, subln_g, lambda_init, batch, seq, heads):
    t = qkv.shape[0]
    d = qkv.shape[1] // 3
    dv = d // heads
    hp = ATTN_HEADS
    assert TQ + 1 >= MAX_DISTANCE and seq % TQ == 0 and dv == LANES and heads % hp == 0
    nq = seq // TQ
    groups = heads // hp
    bias = _bias_tiles(rel_table)
    return pl.pallas_call(
        functools.partial(_diff_attn_kernel, lambda_init=lambda_init),
        grid=(batch, groups),
        in_specs=[
            pl.BlockSpec((seq, hp * dv), lambda b, g: (b, g)),
            pl.BlockSpec((seq, hp * dv), lambda b, g: (b, groups + g)),
            pl.BlockSpec((seq, hp * dv), lambda b, g: (b, 2 * groups + g)),
            pl.BlockSpec((hp, 2, TQ, TQ), lambda b, g: (g, 0, 0, 0)),
            pl.BlockSpec(lam_qk.shape, lambda b, g: (0, 0)),
            pl.BlockSpec((1, dv), lambda b, g: (0, 0)),
        ],
        out_specs=pl.BlockSpec((seq, hp * dv), lambda b, g: (b, g)),
        out_shape=jax.ShapeDtypeStruct((t, d), BF16),
        scratch_shapes=[
            pltpu.VMEM((hp, nq, dv + BF16_SUBLANES, TQ), BF16),
            pltpu.VMEM((hp, 1, 2 * TQ), F32),
            pltpu.VMEM((hp, dv + BF16_SUBLANES, 2 * TQ), F32),
            pltpu.VMEM((hp, TQ, 2 * TQ), F32),
        ],
        compiler_params=_cparams(("parallel", "parallel")),
    )(qkv, qkv, qkv, bias, lam_qk.astype(F32), subln_g.reshape(1, dv).astype(F32))


def _retention_kernel(q_ref, k_ref, v_ref, dec_ref, tab_ref, o_ref, state_ref):
    seq = q_ref.shape[0]
    hp, c, _ = dec_ref.shape
    dk = q_ref.shape[1] // hp
    dv = v_ref.shape[1] // hp
    heads = range(hp)
    state_ref[...] = jnp.zeros(state_ref.shape, F32)

    def body(n, carry):
        rows = pl.ds(pl.multiple_of(n * c, c), c)
        q = [q_ref[rows, hh * dk:(hh + 1) * dk] for hh in heads]
        k = [k_ref[rows, hh * dk:(hh + 1) * dk] for hh in heads]
        v = [v_ref[rows, hh * dv:(hh + 1) * dv] for hh in heads]
        tab = [tab_ref[hh] for hh in heads]
        st = [state_ref[hh] for hh in heads]
        s = [lax.dot_general(q[hh], k[hh], (((1,), (1,)), ((), ())), preferred_element_type=F32)
             for hh in heads]
        cross = [jnp.dot(q[hh], st[hh].astype(BF16), preferred_element_type=F32) for hh in heads]
        kt = [(k[hh].astype(F32) * tab[hh][:, 1:2]).T.astype(BF16) for hh in heads]
        inner = [jnp.dot((s[hh] * dec_ref[hh]).astype(BF16), v[hh], preferred_element_type=F32)
                 for hh in heads]
        for hh in heads:
            state_ref[hh] = st[hh] * tab[hh][0:1, 2:3] + jnp.dot(kt[hh], v[hh], preferred_element_type=F32)
        for hh in heads:
            o = inner[hh] + cross[hh] * tab[hh][:, 0:1]
            o = o * lax.rsqrt(jnp.mean(o * o, axis=-1, keepdims=True) + RMS_EPS)
            o_ref[rows, hh * dv:(hh + 1) * dv] = o.astype(o_ref.dtype)
        return carry

    lax.fori_loop(0, seq // c, body, 0)


def _retention_tables(heads):
    c = RET_C
    log_gamma = jnp.log(1.0 - 2.0 ** (-5.0 - jnp.arange(heads, dtype=F32)))
    pos = jnp.arange(c, dtype=F32)
    dist = pos[:, None] - pos[None, :]
    inner_decay = jnp.where(dist[None] >= 0,
                            jnp.exp(jnp.maximum(dist, 0.0)[None] * log_gamma[:, None, None]), 0.0)
    q_decay = jnp.exp((pos + 1.0)[None] * log_gamma[:, None])
    k_decay = jnp.exp((c - 1.0 - pos)[None] * log_gamma[:, None])
    chunk_decay = jnp.broadcast_to(jnp.exp(c * log_gamma)[:, None], (heads, c))
    tab = jnp.stack([q_decay, k_decay, chunk_decay], axis=-1)
    tab = jnp.pad(tab, ((0, 0), (0, 0), (0, LANES - 3)))
    return inner_decay, tab


def _retention(qkvg, batch, seq, qk_dim, v_dim):
    t = qkvg.shape[0]
    dk = qk_dim // RET_HEADS
    dvh = v_dim // RET_HEADS
    inner_decay, tab = _retention_tables(RET_HEADS)
    hp = RET_HP
    assert RET_HEADS % hp == 0
    groups = RET_HEADS // hp
    kb = qk_dim // (hp * dk)
    vb = 2 * qk_dim // (hp * dvh)
    return pl.pallas_call(
        _retention_kernel,
        grid=(batch, groups),
        in_specs=[
            pl.BlockSpec((seq, hp * dk), lambda b, g: (b, g)),
            pl.BlockSpec((seq, hp * dk), lambda b, g: (b, kb + g)),
            pl.BlockSpec((seq, hp * dvh), lambda b, g: (b, vb + g)),
            pl.BlockSpec((hp, RET_C, RET_C), lambda b, g: (g, 0, 0)),
            pl.BlockSpec((hp, RET_C, LANES), lambda b, g: (g, 0, 0)),
        ],
        out_specs=pl.BlockSpec((seq, hp * dvh), lambda b, g: (b, g)),
        out_shape=jax.ShapeDtypeStruct((t, v_dim), BF16),
        scratch_shapes=[pltpu.VMEM((hp, dk, dvh), F32)],
        compiler_params=_cparams(("parallel", "parallel")),
    )(qkvg, qkvg, qkvg, inner_decay, tab)


def _out_router_kernel(*refs, gated, n_groups, n_experts):
    if gated:
        a_ref, gate_ref, w_ref, x_ref, nf_ref, wr_ref, br_ref = refs[:7]
        x1_ref, hn_ref, route_ref, route_t_ref, cnt_ref, carry_ref = refs[7:]
    else:
        a_ref, w_ref, x_ref, nf_ref, wr_ref, br_ref = refs[:6]
        x1_ref, hn_ref, route_ref, route_t_ref, cnt_ref, carry_ref = refs[6:]
    epg = n_experts // n_groups
    tm = x_ref.shape[0] // ROUTER_SPLIT

    @pl.when(pl.program_id(0) == 0)
    def _():
        carry_ref[...] = jnp.zeros(carry_ref.shape, F32)

    x1_parts = []
    for part in range(ROUTER_SPLIT):
        rows = slice(part * tm, (part + 1) * tm)
        a = a_ref[rows, :]
        if gated:
            gate = gate_ref[rows, :]
            a = gate * jax.nn.sigmoid(gate) * a
        x1_parts.append(x_ref[rows, :] + jnp.dot(a, w_ref[...], preferred_element_type=F32))
    for part in range(ROUTER_SPLIT):
        rows = slice(part * tm, (part + 1) * tm)
        _route_rows(x1_parts[part], rows, nf_ref, wr_ref, br_ref, x1_ref, hn_ref, route_ref, route_t_ref,
                    cnt_ref, carry_ref, n_groups=n_groups, n_experts=n_experts, epg=epg)


def _route_rows(x1, rows, nf_ref, wr_ref, br_ref, x1_ref, hn_ref, route_ref, route_t_ref, cnt_ref,
                carry_ref, *, n_groups, n_experts, epg):
    tm = x1.shape[0]
    x1_ref[rows, :] = x1
    hn = x1 * lax.rsqrt(jnp.mean(x1 * x1, axis=-1, keepdims=True) + RMS_EPS) * nf_ref[...]
    hn_ref[rows, :] = _pack_bf16_pairs(hn)

    hn_hi = hn.astype(BF16)
    hn_lo = (hn - hn_hi.astype(F32)).astype(BF16)
    both = jnp.dot(hn_hi, wr_ref[...], preferred_element_type=F32)
    logits = (both[:, :LANES] + both[:, LANES:]
              + jnp.dot(hn_lo, wr_ref[:, :LANES], preferred_element_type=F32)) + br_ref[...]

    n_rows = -(-(n_groups + n_experts) // SUBLANES) * SUBLANES
    lt = logits.T[:n_rows, :]
    row = lax.broadcasted_iota(jnp.int32, lt.shape, 0)
    big = jnp.int32(LANES)
    neg = jnp.float32(-jnp.inf)

    gl = jnp.where(row < n_groups, lt, neg)
    gmax = jnp.max(gl, axis=0, keepdims=True)
    g_p = 1.0 / jnp.sum(jnp.exp(gl - gmax), axis=0, keepdims=True)
    g_idx = jnp.min(jnp.where(gl == gmax, row, big), axis=0, keepdims=True)

    lo = n_groups + g_idx * epg
    el = jnp.where((row >= lo) & (row < lo + epg), lt, neg)
    v1 = jnp.max(el, axis=0, keepdims=True)
    i1 = jnp.min(jnp.where(el == v1, row, big), axis=0, keepdims=True)
    el2 = jnp.where(row == i1, neg, el)
    v2 = jnp.max(el2, axis=0, keepdims=True)
    i2 = jnp.min(jnp.where(el2 == v2, row, big), axis=0, keepdims=True)
    tt = jnp.exp(v2 - v1)
    w1 = g_p / (1.0 + tt)
    w2 = g_p * tt / (1.0 + tt)
    e1 = i1 - n_groups
    e2 = i2 - n_groups

    erow = lax.broadcasted_iota(jnp.int32, (n_experts, tm), 0)
    oh1 = erow == e1
    oh2 = erow == e2
    onehot = jnp.where(oh1 | oh2, 1.0, 0.0)
    before = lax.broadcasted_iota(jnp.int32, (tm, tm), 0) < lax.broadcasted_iota(jnp.int32, (tm, tm), 1)
    tri = jnp.where(before, 1.0, 0.0).astype(BF16)
    carry = carry_ref[...]
    base = jnp.dot(onehot.astype(BF16), tri, preferred_element_type=F32) + carry[:, 0:1]
    rank1 = jnp.sum(jnp.where(oh1, base, 0.0), axis=0, keepdims=True)
    rank2 = jnp.sum(jnp.where(oh2, base, 0.0), axis=0, keepdims=True)
    carry = carry + jnp.sum(onehot, axis=1, keepdims=True)
    carry_ref[...] = carry
    cnt_ref[...] = carry

    frow = lax.broadcasted_iota(jnp.int32, (LANES, tm), 0)
    out = jnp.zeros((LANES, tm), F32)
    for idx, val in enumerate((e1.astype(F32), e2.astype(F32), w1, w2, rank1, rank2)):
        out = jnp.where(frow == idx, val, out)
    route_t_ref[:, rows] = out[:route_t_ref.shape[0], :]
    route_ref[rows, :] = out.T


def _out_router(a, gate, w, x, nf, w_group, b_group, w_expert, b_expert):
    t, d = x.shape
    k = w.shape[0]
    n_groups = w_group.shape[1]
    n_experts = w_expert.shape[1]
    assert n_groups + n_experts <= LANES
    pad = LANES - n_groups - n_experts
    wr = jnp.pad(jnp.concatenate([w_group, w_expert], axis=1).astype(F32), ((0, 0), (0, pad)))
    wr_hi = wr.astype(BF16)
    wr = jnp.concatenate([wr_hi, (wr - wr_hi.astype(F32)).astype(BF16)], axis=1)
    br = jnp.pad(jnp.concatenate([b_group, b_expert]).astype(F32), (0, pad)).reshape(1, LANES)
    gated = gate is not None
    row = lambda i: (i, 0)
    fixed = lambda i: (0, 0)
    in_specs = [pl.BlockSpec((TM, k), row)]
    args = [a]
    if gated:
        gate_arr, gate_col = gate
        in_specs.append(pl.BlockSpec((TM, k), lambda i: (i, gate_col)))
        args.append(gate_arr)
    in_specs += [
        pl.BlockSpec((k, d), fixed),
        pl.BlockSpec((TM, d), row),
        pl.BlockSpec((1, d), fixed),
        pl.BlockSpec((d, 2 * LANES), fixed),
        pl.BlockSpec((1, LANES), fixed),
    ]
    args += [w, x, nf.reshape(1, d), wr, br]
    return pl.pallas_call(
        functools.partial(_out_router_kernel, gated=gated, n_groups=n_groups, n_experts=n_experts),
        grid=(t // TM,),
        in_specs=in_specs,
        out_specs=[
            pl.BlockSpec((TM, d), row),
            pl.BlockSpec((TM, d // 2), row),
            pl.BlockSpec((TM, LANES), row),
            pl.BlockSpec((SUBLANES, TM), lambda i: (0, i)),
            pl.BlockSpec((n_experts, LANES), fixed),
        ],
        out_shape=[
            jax.ShapeDtypeStruct((t, d), F32),
            jax.ShapeDtypeStruct((t, d // 2), jnp.uint32),
            jax.ShapeDtypeStruct((t, LANES), F32),
            jax.ShapeDtypeStruct((SUBLANES, t), F32),
            jax.ShapeDtypeStruct((n_experts, LANES), F32),
        ],
        scratch_shapes=[pltpu.VMEM((n_experts, LANES), F32)],
        compiler_params=_cparams(("arbitrary",)),
    )(*args)


def _sc_mesh():
    return plsc.VectorSubcoreMesh(core_axis_name="c", subcore_axis_name="s",
                                  num_cores=SC_CORES, num_subcores=SC_SUBCORES)


def _sc_worker():
    return lax.axis_index("s") * SC_CORES + lax.axis_index("c")


def _sc_scatter_rows(src, d1, d2, n_rows):
    t, d = src.shape
    workers = SC_CORES * SC_SUBCORES
    per_w = t // workers
    n_ch = per_w // SC_CHUNK
    assert t % (workers * SC_CHUNK) == 0
    idx = jnp.stack([d1, d2]).reshape(2, workers, n_ch, SC_CHUNK)

    @functools.partial(
        pl.kernel, mesh=_sc_mesh(),
        out_type=jax.ShapeDtypeStruct((n_rows, d), src.dtype),
        scratch_types=[pltpu.VMEM((2, n_ch, SC_CHUNK), jnp.int32),
                       pltpu.VMEM((SC_CHUNK, d), src.dtype),
                       pltpu.VMEM((SC_CHUNK, d), src.dtype),
                       pltpu.SemaphoreType.DMA, pltpu.SemaphoreType.DMA],
    )
    def scatter(src_hbm, idx_hbm, out_hbm, idx_v, rows_a, rows_b, sem_a, sem_b):
        wid = _sc_worker()
        base = wid * per_w
        for s in range(2):
            pltpu.sync_copy(idx_hbm.at[s, wid], idx_v.at[s])

        def move(c, rows_v, sem):
            pltpu.sync_copy(src_hbm.at[pl.ds(base + c * SC_CHUNK, SC_CHUNK)], rows_v)
            return [pltpu.async_copy(rows_v, out_hbm.at[idx_v.at[s, c]], sem) for s in range(2)]

        @pl.loop(0, n_ch, step=2)
        def _(c):
            first = move(c, rows_a, sem_a)
            second = move(c + 1, rows_b, sem_b)
            for cp in first + second:
                cp.wait()

    return scatter(src, idx)


def _sc_gather_rows(table, idx):
    b = idx.shape[0]
    d = table.shape[1]
    workers = SC_CORES * SC_SUBCORES
    per_w = b // workers
    n_ch = per_w // SC_CHUNK
    assert b % (workers * SC_CHUNK * 2) == 0

    @functools.partial(
        pl.kernel, mesh=_sc_mesh(),
        out_type=jax.ShapeDtypeStruct((b, d), table.dtype),
        scratch_types=[pltpu.VMEM((per_w,), jnp.int32),
                       pltpu.VMEM((SC_CHUNK, d), table.dtype),
                       pltpu.VMEM((SC_CHUNK, d), table.dtype),
                       pltpu.SemaphoreType.DMA, pltpu.SemaphoreType.DMA],
    )
    def gather(table_hbm, idx_hbm, out_hbm, idx_v, rows_a, rows_b, sem_a, sem_b):
        base = _sc_worker() * per_w
        pltpu.sync_copy(idx_hbm.at[pl.ds(base, per_w)], idx_v)

        def fetch(c, rows_v, sem):
            return pltpu.async_copy(table_hbm.at[idx_v.at[pl.ds(c * SC_CHUNK, SC_CHUNK)]], rows_v, sem)

        def store(c, rows_v):
            pltpu.sync_copy(rows_v, out_hbm.at[pl.ds(base + c * SC_CHUNK, SC_CHUNK)])

        @pl.loop(0, n_ch, step=2)
        def _(c):
            first = fetch(c, rows_a, sem_a)
            second = fetch(c + 1, rows_b, sem_b)
            first.wait()
            store(c, rows_a)
            second.wait()
            store(c + 1, rows_b)

    return gather(table, idx)


def _expert_kernel(fb_ref, nblk_ref, nv_ref, nu_ref, xs_ref, wg_ref, wu_ref, wd_ref, ys_ref,
                   wgb_ref, wub_ref, wdb_ref, xbuf, ybuf, xsem, ysem):
    e = pl.program_id(0)
    rb = xbuf.shape[1]
    nb = ys_ref.shape[0] // rb
    n_used = nu_ref[0]

    def x_copy(g, slot):
        rows = pl.ds(pl.multiple_of(g * rb, rb), rb)
        return pltpu.make_async_copy(xs_ref.at[rows], xbuf.at[slot], xsem.at[slot])

    def y_copy(g, slot):
        rows = pl.ds(pl.multiple_of(g * rb, rb), rb)
        return pltpu.make_async_copy(ybuf.at[slot], ys_ref.at[rows], ysem.at[slot])

    @pl.when(e == 0)
    def _():
        x_copy(0, 0).start()

    @pl.when(nblk_ref[e] > 0)
    def _():
        wgb_ref[...] = wg_ref[0, 0].astype(BF16)
        wub_ref[...] = wu_ref[0, 0].astype(BF16)
        wdb_ref[...] = wd_ref[0, 0].astype(BF16)

    def block(b, carry):
        g = fb_ref[e] + b
        slot = lax.rem(g, 2)
        x_copy(g, slot).wait()

        @pl.when(g + 1 < n_used)
        def _():
            x_copy(g + 1, 1 - slot).start()

        @pl.when(g >= 2)
        def _():
            y_copy(g - 2, slot).wait()

        nv = nv_ref[g]

        def ffn(rows):
            valid = lax.broadcasted_iota(jnp.int32, (rows, 1), 0) < nv
            x_hi, x_lo = _unpack_bf16_pairs(jnp.where(valid, xbuf[slot, :rows, :], jnp.uint32(0)))
            x = jnp.concatenate([x_hi.astype(BF16), x_lo.astype(BF16)], axis=1)
            slab = wgb_ref.shape[1] // FF_SPLIT
            hmid = []
            for c in range(FF_SPLIT):
                cols = slice(c * slab, (c + 1) * slab)
                gate = jnp.dot(x, wgb_ref[:, cols], preferred_element_type=F32)
                up = jnp.dot(x, wub_ref[:, cols], preferred_element_type=F32)
                hmid.append((gate * jax.nn.sigmoid(gate) * up).astype(BF16))
            hmid = jnp.concatenate(hmid, axis=1)
            ybuf[slot, :rows, :] = _pack_bf16_pairs(jnp.dot(hmid, wdb_ref[...], preferred_element_type=F32))
            if rows < rb:
                ybuf[slot, rows:, :] = jnp.zeros((rb - rows, ybuf.shape[2]), ybuf.dtype)

        @pl.when(nv > rb // 2)
        def _():
            ffn(rb)

        @pl.when(nv <= rb // 2)
        def _():
            ffn(rb // 2)

        y_copy(g, slot).start()
        return carry

    lax.fori_loop(0, nblk_ref[e], block, 0)

    @pl.when(e == pl.num_programs(0) - 1)
    def _():
        @pl.when(n_used >= 2)
        def _():
            y_copy(n_used - 2, lax.rem(n_used, 2)).wait()

        y_copy(n_used - 1, lax.rem(n_used - 1, 2)).wait()
        ybuf[0] = jnp.zeros(ybuf.shape[1:], ybuf.dtype)

        def tail_start(g, carry):
            y_copy(g, 0).start()
            return carry

        def tail_wait(g, carry):
            y_copy(g, 0).wait()
            return carry

        lax.fori_loop(n_used, nb, tail_start, 0)
        lax.fori_loop(n_used, nb, tail_wait, 0)


def _experts(xs, first_block, n_blocks, n_valid, n_used, w_gate, w_up, w_down, layer):
    n_rows, dp = xs.shape
    d = 2 * dp
    n_experts, ff = w_gate.shape[1], w_gate.shape[3]
    wsel = lambda e, *_: (layer, e, 0, 0)
    grid_spec = pltpu.PrefetchScalarGridSpec(
        num_scalar_prefetch=4,
        grid=(n_experts,),
        in_specs=[
            pl.BlockSpec(memory_space=pl.ANY),
            pl.BlockSpec((1, 1, d, ff), wsel),
            pl.BlockSpec((1, 1, d, ff), wsel),
            pl.BlockSpec((1, 1, ff, d), wsel),
        ],
        out_specs=pl.BlockSpec(memory_space=pl.ANY),
        scratch_shapes=[pltpu.VMEM((d, ff), BF16), pltpu.VMEM((d, ff), BF16), pltpu.VMEM((ff, d), BF16),
                        pltpu.VMEM((2, ROW_BLOCK, dp), xs.dtype), pltpu.VMEM((2, ROW_BLOCK, dp), xs.dtype),
                        pltpu.SemaphoreType.DMA((2,)), pltpu.SemaphoreType.DMA((2,))],
    )
    return pl.pallas_call(
        _expert_kernel,
        grid_spec=grid_spec,
        out_shape=jax.ShapeDtypeStruct((n_rows, dp), xs.dtype),
        compiler_params=_cparams(("arbitrary",)),
    )(first_block, n_blocks, n_valid, n_used, xs, w_gate, w_up, w_down)


def _final_combine_kernel(x_ref, route_ref, y1_ref, y2_ref, nf_ref, o_ref):
    route = route_ref[...]
    y1 = jnp.concatenate(_unpack_bf16_pairs(y1_ref[...]), axis=1)
    y2 = jnp.concatenate(_unpack_bf16_pairs(y2_ref[...]), axis=1)
    y = x_ref[...] + (route[:, 2:3] * y1 + route[:, 3:4] * y2)
    o_ref[...] = y * lax.rsqrt(jnp.mean(y * y, axis=-1, keepdims=True) + RMS_EPS) * nf_ref[...]


def _final_combine(x1, route, yg, nf):
    t, d = x1.shape
    slot2 = t // TM
    row = lambda i: (i, 0)
    return pl.pallas_call(
        _final_combine_kernel,
        grid=(t // TM,),
        in_specs=[
            pl.BlockSpec((TM, d), row),
            pl.BlockSpec((TM, LANES), row),
            pl.BlockSpec((TM, d // 2), row),
            pl.BlockSpec((TM, d // 2), lambda i: (slot2 + i, 0)),
            pl.BlockSpec((1, d), lambda i: (0, 0)),
        ],
        out_specs=pl.BlockSpec((TM, d), row),
        out_shape=jax.ShapeDtypeStruct((t, d), F32),
        compiler_params=_cparams(("parallel",)),
    )(x1, route, yg, yg, nf.reshape(1, d).astype(F32))


def _moe(hn, route_t, cnt, w_gate, w_up, w_down, layer):
    t, d = hn.shape
    n_experts = w_gate.shape[1]
    ids = jnp.arange(n_experts, dtype=jnp.int32)
    e1, e2, rank1, rank2 = (route_t[f].astype(jnp.int32) for f in (0, 1, 4, 5))
    counts = cnt[:, 0].astype(jnp.int32)
    padded = ((counts + ROW_BLOCK - 1) // ROW_BLOCK) * ROW_BLOCK
    pad_end = jnp.cumsum(padded)
    pad_start = pad_end - padded

    def dest(e, rank):
        return jnp.sum(jnp.where(e[:, None] == ids[None, :], pad_start[None, :], 0), axis=1) + rank

    d1 = dest(e1, rank1)
    d2 = dest(e2, rank2)
    nb = (t * TOP_K_INNER) // ROW_BLOCK + n_experts
    n_used = (pad_end[-1:] // ROW_BLOCK).astype(jnp.int32)
    first_row = jnp.arange(nb, dtype=jnp.int32) * ROW_BLOCK
    block_e = jnp.sum((pad_end[None, :] <= first_row[:, None]).astype(jnp.int32), axis=1)
    block_e = jnp.minimum(block_e, n_experts - 1)
    row_end = jnp.sum(jnp.where(block_e[:, None] == ids[None, :], (pad_start + counts)[None, :], 0), axis=1)
    n_valid = jnp.clip(row_end - first_row, 0, ROW_BLOCK).astype(jnp.int32)
    xs = _sc_scatter_rows(hn, d1, d2, nb * ROW_BLOCK)
    ys = _experts(xs, (pad_start // ROW_BLOCK).astype(jnp.int32), (padded // ROW_BLOCK).astype(jnp.int32),
                  n_valid, n_used, w_gate, w_up, w_down, layer)
    return _sc_gather_rows(ys, jnp.concatenate([d1, d2]))


def _diff_lambda_init(layer_idx):
    return 0.8 - 0.6 * math.exp(-0.3 * layer_idx)


def _rope_tables(seq, dk):
    angle = 1.0 / (10000.0 ** jnp.linspace(0.0, 1.0, dk // 2, dtype=F32))
    angle = jnp.repeat(angle, 2)
    phase = jnp.arange(seq, dtype=F32)[:, None] * angle[None, :]
    sin, cos = jnp.sin(phase), jnp.cos(phase)
    even = (jnp.arange(dk) % 2 == 0)[None, :]
    return cos, jnp.where(even, -sin, 0.0), jnp.where(even, 0.0, sin)


def kernel(x, rel_bias_table, norm_mix, norm_ffn, norm_final, diff_w_in, diff_lambda_qk, diff_subln,
           diff_w_out, ret_w_in, ret_w_out, moe_w_group, moe_b_group, moe_w_expert, moe_b_expert,
           moe_w_gate, moe_w_up, moe_w_down):
    batch, seq, d = x.shape
    depth = norm_mix.shape[0]
    t = batch * seq
    heads = rel_bias_table.shape[1]
    dh = diff_lambda_qk.shape[-1]
    qk_dim = (ret_w_in.shape[2] - 2 * ret_w_out.shape[1]) // 2
    v_dim = ret_w_out.shape[1]
    dk = qk_dim // RET_HEADS
    xt = x.reshape(t, d)
    moe = None
    for i in range(depth):
        j = i // N_MIXERS
        if i % N_MIXERS == 0:
            col_scale = jnp.concatenate([jnp.full((d,), dh ** -0.5 * LOG2E, F32), jnp.ones((2 * d,), F32)])
            qkv = _norm_matmul(xt, norm_mix[i], diff_w_in, j, col_scale, moe=moe)
            if moe is not None:
                qkv, xt = qkv
            mix = _diff_attention(qkv, rel_bias_table, diff_lambda_qk[j], diff_subln[j],
                                  _diff_lambda_init(i), batch, seq, heads)
            gate = None
            w_out = diff_w_out[j].astype(BF16)
        else:
            n_in = ret_w_in.shape[2]
            col_scale = jnp.concatenate([jnp.ones((qk_dim,), F32), jnp.full((qk_dim,), dk ** -0.5, F32),
                                         jnp.ones((n_in - 2 * qk_dim,), F32)])
            qkvg = _norm_matmul(xt, norm_mix[i], ret_w_in, j, col_scale, rope=_rope_tables(seq, dk),
                                n_rope=2 * qk_dim // TN, seq=seq, moe=moe)
            if moe is not None:
                qkvg, xt = qkvg
            mix = _retention(qkvg, batch, seq, qk_dim, v_dim)
            gate = (qkvg, (2 * qk_dim + v_dim) // v_dim)
            w_out = ret_w_out[j].astype(BF16)
        x1, hn, route, route_t, cnt = _out_router(mix, gate, w_out, xt, norm_ffn[i], moe_w_group[i],
                                                  moe_b_group[i], moe_w_expert[i], moe_b_expert[i])
        xt = x1
        moe = (route, _moe(hn, route_t, cnt, moe_w_gate, moe_w_up, moe_w_down, i))
    return _final_combine(xt, moe[0], moe[1], norm_final).reshape(batch, seq, d)
```

```python
import functools
import math

import jax
import jax.numpy as jnp
from jax import lax
from jax.experimental import pallas as pl
from jax.experimental.pallas import tpu as pltpu
from jax.experimental.pallas import tpu_sc as plsc

F32 = jnp.float32
BF16 = jnp.bfloat16

N_MIXERS = 2
NUM_BUCKETS = 32
MAX_DISTANCE = 128
RET_HEADS = 4
TOP_K_INNER = 2
RMS_EPS = 1e-6
SUBLN_EPS = 1e-5

LANES = 128
SUBLANES = 8
BF16_SUBLANES = 16
LOG2E = 1.4426950408889634
VMEM_LIMIT = 56 * 1024 * 1024
SC_CORES = 2
SC_SUBCORES = 16

TM = 512
TN = 1024
TQ = 256
ATTN_HEADS = 8
RET_C = 256
RET_HP = 4
FF_SPLIT = 2
ROUTER_SPLIT = 2
ROW_BLOCK = 512
SC_CHUNK = 64


def _cparams(sem):
    return pltpu.CompilerParams(dimension_semantics=sem, vmem_limit_bytes=VMEM_LIMIT)


def _pack_bf16_pairs(x):
    h = x.shape[1] // 2
    hi = lax.bitcast_convert_type(x[:, :h].astype(BF16).astype(F32), jnp.uint32)
    lo = lax.bitcast_convert_type(x[:, h:].astype(BF16).astype(F32), jnp.uint32)
    return hi | (lo >> 16)


def _unpack_bf16_pairs(w):
    hi = lax.bitcast_convert_type(w & jnp.uint32(0xFFFF0000), F32)
    lo = lax.bitcast_convert_type(w << 16, F32)
    return hi, lo


def _norm_matmul_kernel(x_ref, *rest, n_rope, moe, layer):
    rest = list(rest)
    w_ref, stage_ref, sem = rest[-3:]
    rest = rest[:-3]
    if moe:
        route_ref, y1_ref, y2_ref = rest[:3]
        x_out_ref = rest.pop()
        rest = rest[3:]
    g_ref, scale_ref, w_hbm = rest[:3]
    if n_rope:
        cos_ref, sa_ref, sb_ref, o_ref = rest[3:]
    else:
        (o_ref,) = rest[3:]
    n_chunks = o_ref.shape[1] // TN

    @pl.when(pl.program_id(0) == 0)
    def _():
        def fetch(j):
            return pltpu.make_async_copy(w_hbm.at[layer, :, pl.ds(j * TN, TN)], stage_ref.at[j % 2],
                                         sem.at[j % 2])

        fetch(0).start()
        for j in range(n_chunks):
            if j + 1 < n_chunks:
                fetch(j + 1).start()
            fetch(j).wait()
            cols = slice(j * TN, (j + 1) * TN)
            w_ref[:, cols] = (stage_ref[j % 2] * scale_ref[:, cols]).astype(BF16)

    x = x_ref[...]
    if moe:
        route = route_ref[...]
        y1 = jnp.concatenate(_unpack_bf16_pairs(y1_ref[...]), axis=1)
        y2 = jnp.concatenate(_unpack_bf16_pairs(y2_ref[...]), axis=1)
        x = x + (route[:, 2:3] * y1 + route[:, 3:4] * y2)
        x_out_ref[...] = x
    xn = (x * lax.rsqrt(jnp.mean(x * x, axis=-1, keepdims=True) + RMS_EPS) * g_ref[...]).astype(BF16)
    for j in range(n_chunks):
        cols = slice(j * TN, (j + 1) * TN)
        acc = jnp.dot(xn, w_ref[:, cols], preferred_element_type=F32)
        if j < n_rope:
            reps = TN // cos_ref.shape[1]
            c = jnp.tile(cos_ref[...], (1, reps))
            sa = jnp.tile(sa_ref[...], (1, reps))
            sb = jnp.tile(sb_ref[...], (1, reps))
            nxt = pltpu.roll(acc, TN - 1, axis=1)
            prv = pltpu.roll(acc, 1, axis=1)
            acc = acc * c + nxt * sa + prv * sb
        o_ref[:, cols] = acc.astype(o_ref.dtype)


def _norm_matmul(x, g, w, layer, col_scale, rope=None, n_rope=0, seq=None, moe=None):
    t, d = x.shape
    n = w.shape[2]
    row = lambda i: (i, 0)
    in_specs = [pl.BlockSpec((TM, d), row)]
    args = [x]
    if moe is not None:
        route, yg = moe
        slot2 = t // TM
        in_specs += [pl.BlockSpec((TM, LANES), row), pl.BlockSpec((TM, d // 2), row),
                     pl.BlockSpec((TM, d // 2), lambda i: (slot2 + i, 0))]
        args += [route, yg, yg]
    in_specs += [
        pl.BlockSpec((1, d), lambda i: (0, 0)),
        pl.BlockSpec((1, n), lambda i: (0, 0)),
        pl.BlockSpec(memory_space=pl.ANY),
    ]
    args += [g.reshape(1, d), col_scale.reshape(1, n).astype(F32), w]
    if rope is not None:
        nb = seq // TM
        dk = rope[0].shape[1]
        in_specs += [pl.BlockSpec((TM, dk), lambda i: (i % nb, 0))] * 3
        args += list(rope)
    out_specs = [pl.BlockSpec((TM, n), row)]
    out_shape = [jax.ShapeDtypeStruct((t, n), BF16)]
    if moe is not None:
        out_specs.append(pl.BlockSpec((TM, d), row))
        out_shape.append(jax.ShapeDtypeStruct((t, d), F32))
    out = pl.pallas_call(
        functools.partial(_norm_matmul_kernel, n_rope=n_rope, moe=moe is not None, layer=layer),
        grid=(t // TM,),
        in_specs=in_specs,
        out_specs=out_specs,
        out_shape=out_shape,
        scratch_shapes=[pltpu.VMEM((d, n), BF16), pltpu.VMEM((2, d, TN), F32), pltpu.SemaphoreType.DMA((2,))],
        compiler_params=_cparams(("arbitrary",)),
    )(*args)
    return out if moe is not None else out[0]


def _diff_attn_kernel(q_ref, k_ref, v_ref, bias_ref, lam_ref, g_ref, o_ref,
                      vt_ref, m_ref, acc_ref, s_ref, *, lambda_init):
    seq = q_ref.shape[0]
    hp, _, _, tq = bias_ref.shape
    dv = q_ref.shape[1] // hp
    dh = dv // 2
    nq = seq // tq

    def head(hh):
        return slice(hh * dv, (hh + 1) * dv)

    extra = vt_ref.shape[2] - dv
    ones_row = jnp.where(lax.broadcasted_iota(jnp.int32, (extra, tq), 0) == 0, 1.0, 0.0).astype(BF16)
    for hh in range(hp):
        for j in range(nq):
            vt_ref[hh, j, :dv, :] = v_ref[j * tq:(j + 1) * tq, head(hh)].astype(F32).T.astype(BF16)
            vt_ref[hh, j, dv:, :] = ones_row

    lq = lam_ref[...]
    lam = (jnp.exp(jnp.sum(lq[0:1] * lq[1:2], axis=-1, keepdims=True))
           - jnp.exp(jnp.sum(lq[2:3] * lq[3:4], axis=-1, keepdims=True)) + lambda_init)
    lane = lax.broadcasted_iota(jnp.int32, (tq, dv), 1)

    def query_maps(qi):
        qrows = pl.ds(pl.multiple_of(qi * tq, tq), tq)
        qs = []
        for hh in range(hp):
            q = q_ref[qrows, head(hh)]
            zero = jnp.zeros_like(q)
            qs.append(jnp.concatenate([jnp.where(lane < dh, q, zero), jnp.where(lane >= dh, q, zero)],
                                      axis=0))
        return qs

    def score_dots(j, qs):
        rows = pl.ds(pl.multiple_of(j * tq, tq), tq)
        for hh in range(hp):
            s_ref[hh] = lax.dot_general(k_ref[rows, head(hh)], qs[hh], (((1,), (1,)), ((), ())),
                                        preferred_element_type=F32)

    score_dots(0, query_maps(0))

    def q_body(qi, carry):
        qrows = pl.ds(pl.multiple_of(qi * tq, tq), tq)
        qs = query_maps(qi)
        qs_next = query_maps(jnp.minimum(qi + 1, nq - 1))

        m_ref[...] = jnp.full(m_ref.shape, -jnp.inf, F32)
        acc_ref[...] = jnp.zeros(acc_ref.shape, F32)

        def step(j, tile, prefetch):
            nxt, qn = (j + 1, qs) if prefetch == "block" else (0, qs_next)
            rows = pl.ds(pl.multiple_of(nxt * tq, tq), tq)
            for hh in range(hp):
                s = s_ref[hh]
                if tile is not None:
                    bias = bias_ref[hh, tile]
                    s = s + jnp.concatenate([bias, bias], axis=1)
                m_old = m_ref[hh]
                m_new = jnp.maximum(m_old, jnp.max(s, axis=0, keepdims=True))
                alpha = jnp.exp2(m_old - m_new)
                m_ref[hh] = m_new
                p = jnp.exp2(s - m_new).astype(BF16)
                s_ref[hh] = lax.dot_general(k_ref[rows, head(hh)], qn[hh], (((1,), (1,)), ((), ())),
                                            preferred_element_type=F32)
                acc_ref[hh] = alpha * acc_ref[hh] + jnp.dot(vt_ref[hh, j], p, preferred_element_type=F32)

        def far_body(j, c):
            step(j, None, "block")
            return c

        lax.fori_loop(0, jnp.maximum(qi - 1, 0), far_body, 0)

        @pl.when(qi >= 1)
        def _():
            step(qi - 1, 1, "block")

        step(qi, 0, "tile")

        for hh in range(hp):
            acc = acc_ref[hh]
            on = acc[:dv] / acc[dv:dv + 1]
            ot = on[:, :tq] - lam * on[:, tq:]
            ot = ot * lax.rsqrt(jnp.mean(ot * ot, axis=0, keepdims=True) + SUBLN_EPS)
            o_ref[qrows, head(hh)] = (ot.T * g_ref[...] * (1.0 - lambda_init)).astype(o_ref.dtype)
        return carry

    lax.fori_loop(0, nq, q_body, 0)


def _t5_bucket(rel):
    max_exact = NUM_BUCKETS // 2
    n = jnp.maximum(rel, 0)
    is_small = n < max_exact
    nf = jnp.maximum(n, max_exact).astype(F32)
    large = max_exact + (jnp.log(nf / max_exact) / math.log(MAX_DISTANCE / max_exact)
                         * (NUM_BUCKETS - max_exact)).astype(jnp.int32)
    large = jnp.minimum(large, NUM_BUCKETS - 1)
    return jnp.where(is_small, n, large)


def _bias_tiles(rel_table):
    heads = rel_table.shape[1]
    span = 2 * TQ + 1
    bucket = _t5_bucket(jnp.arange(span))
    onehot = (bucket[:, None] == jnp.arange(NUM_BUCKETS)[None, :]).astype(F32)
    table = (rel_table.astype(F32) - rel_table[NUM_BUCKETS - 1].astype(F32)[None, :]) * LOG2E
    by_rel = jnp.dot(onehot, table, precision=lax.Precision.HIGHEST).T
    toep = jnp.tile(by_rel, (1, TQ))[:, :TQ * (span - 1)].reshape(heads, TQ, span - 1)
    kj = jnp.arange(TQ)[:, None]
    qi = jnp.arange(TQ)[None, :]
    diag = jnp.where((qi >= kj)[None], toep[:, :, :TQ], -jnp.inf)
    return jnp.stack([diag, toep[:, :, TQ:]], axis=1)


def _diff_attention(qkv, rel_table, lam_qk, subln_g, lambda_init, batch, seq, heads):
    t = qkv.shape[0]
    d = qkv.shape[1] // 3
    dv = d // heads
    hp = ATTN_HEADS
    assert TQ + 1 >= MAX_DISTANCE and seq % TQ == 0 and dv == LANES and heads % hp == 0
    nq = seq // TQ
    groups = heads // hp
    bias = _bias_tiles(rel_table)
    return pl.pallas_call(
        functools.partial(_diff_attn_kernel, lambda_init=lambda_init),
        grid=(batch, groups),
        in_specs=[
            pl.BlockSpec((seq, hp * dv), lambda b, g: (b, g)),
            pl.BlockSpec((seq, hp * dv), lambda b, g: (b, groups + g)),
            pl.BlockSpec((seq, hp * dv), lambda b, g: (b, 2 * groups + g)),
            pl.BlockSpec((hp, 2, TQ, TQ), lambda b, g: (g, 0, 0, 0)),
            pl.BlockSpec(lam_qk.shape, lambda b, g: (0, 0)),
            pl.BlockSpec((1, dv), lambda b, g: (0, 0)),
        ],
        out_specs=pl.BlockSpec((seq, hp * dv), lambda b, g: (b, g)),
        out_shape=jax.ShapeDtypeStruct((t, d), BF16),
        scratch_shapes=[
            pltpu.VMEM((hp, nq, dv + BF16_SUBLANES, TQ), BF16),
            pltpu.VMEM((hp, 1, 2 * TQ), F32),
            pltpu.VMEM((hp, dv + BF16_SUBLANES, 2 * TQ), F32),
            pltpu.VMEM((hp, TQ, 2 * TQ), F32),
        ],
        compiler_params=_cparams(("parallel", "parallel")),
    )(qkv, qkv, qkv, bias, lam_qk.astype(F32), subln_g.reshape(1, dv).astype(F32))


def _retention_kernel(q_ref, k_ref, v_ref, dec_ref, tab_ref, o_ref, state_ref):
    seq = q_ref.shape[0]
    hp, c, _ = dec_ref.shape
    dk = q_ref.shape[1] // hp
    dv = v_ref.shape[1] // hp
    heads = range(hp)
    state_ref[...] = jnp.zeros(state_ref.shape, F32)

    def body(n, carry):
        rows = pl.ds(pl.multiple_of(n * c, c), c)
        q = [q_ref[rows, hh * dk:(hh + 1) * dk] for hh in heads]
        k = [k_ref[rows, hh * dk:(hh + 1) * dk] for hh in heads]
        v = [v_ref[rows, hh * dv:(hh + 1) * dv] for hh in heads]
        tab = [tab_ref[hh] for hh in heads]
        st = [state_ref[hh] for hh in heads]
        s = [lax.dot_general(q[hh], k[hh], (((1,), (1,)), ((), ())), preferred_element_type=F32)
             for hh in heads]
        cross = [jnp.dot(q[hh], st[hh].astype(BF16), preferred_element_type=F32) for hh in heads]
        kt = [(k[hh].astype(F32) * tab[hh][:, 1:2]).T.astype(BF16) for hh in heads]
        inner = [jnp.dot((s[hh] * dec_ref[hh]).astype(BF16), v[hh], preferred_element_type=F32)
                 for hh in heads]
        for hh in heads:
            state_ref[hh] = st[hh] * tab[hh][0:1, 2:3] + jnp.dot(kt[hh], v[hh], preferred_element_type=F32)
        for hh in heads:
            o = inner[hh] + cross[hh] * tab[hh][:, 0:1]
            o = o * lax.rsqrt(jnp.mean(o * o, axis=-1, keepdims=True) + RMS_EPS)
            o_ref[rows, hh * dv:(hh + 1) * dv] = o.astype(o_ref.dtype)
        return carry

    lax.fori_loop(0, seq // c, body, 0)


def _retention_tables(heads):
    c = RET_C
    log_gamma = jnp.log(1.0 - 2.0 ** (-5.0 - jnp.arange(heads, dtype=F32)))
    pos = jnp.arange(c, dtype=F32)
    dist = pos[:, None] - pos[None, :]
    inner_decay = jnp.where(dist[None] >= 0,
                            jnp.exp(jnp.maximum(dist, 0.0)[None] * log_gamma[:, None, None]), 0.0)
    q_decay = jnp.exp((pos + 1.0)[None] * log_gamma[:, None])
    k_decay = jnp.exp((c - 1.0 - pos)[None] * log_gamma[:, None])
    chunk_decay = jnp.broadcast_to(jnp.exp(c * log_gamma)[:, None], (heads, c))
    tab = jnp.stack([q_decay, k_decay, chunk_decay], axis=-1)
    tab = jnp.pad(tab, ((0, 0), (0, 0), (0, LANES - 3)))
    return inner_decay, tab


def _retention(qkvg, batch, seq, qk_dim, v_dim):
    t = qkvg.shape[0]
    dk = qk_dim // RET_HEADS
    dvh = v_dim // RET_HEADS
    inner_decay, tab = _retention_tables(RET_HEADS)
    hp = RET_HP
    assert RET_HEADS % hp == 0
    groups = RET_HEADS // hp
    kb = qk_dim // (hp * dk)
    vb = 2 * qk_dim // (hp * dvh)
    return pl.pallas_call(
        _retention_kernel,
        grid=(batch, groups),
        in_specs=[
            pl.BlockSpec((seq, hp * dk), lambda b, g: (b, g)),
            pl.BlockSpec((seq, hp * dk), lambda b, g: (b, kb + g)),
            pl.BlockSpec((seq, hp * dvh), lambda b, g: (b, vb + g)),
            pl.BlockSpec((hp, RET_C, RET_C), lambda b, g: (g, 0, 0)),
            pl.BlockSpec((hp, RET_C, LANES), lambda b, g: (g, 0, 0)),
        ],
        out_specs=pl.BlockSpec((seq, hp * dvh), lambda b, g: (b, g)),
        out_shape=jax.ShapeDtypeStruct((t, v_dim), BF16),
        scratch_shapes=[pltpu.VMEM((hp, dk, dvh), F32)],
        compiler_params=_cparams(("parallel", "parallel")),
    )(qkvg, qkvg, qkvg, inner_decay, tab)


def _out_router_kernel(*refs, gated, n_groups, n_experts):
    if gated:
        a_ref, gate_ref, w_ref, x_ref, nf_ref, wr_ref, br_ref = refs[:7]
        x1_ref, hn_ref, route_ref, route_t_ref, cnt_ref, carry_ref = refs[7:]
    else:
        a_ref, w_ref, x_ref, nf_ref, wr_ref, br_ref = refs[:6]
        x1_ref, hn_ref, route_ref, route_t_ref, cnt_ref, carry_ref = refs[6:]
    epg = n_experts // n_groups
    tm = x_ref.shape[0] // ROUTER_SPLIT

    @pl.when(pl.program_id(0) == 0)
    def _():
        carry_ref[...] = jnp.zeros(carry_ref.shape, F32)

    x1_parts = []
    for part in range(ROUTER_SPLIT):
        rows = slice(part * tm, (part + 1) * tm)
        a = a_ref[rows, :]
        if gated:
            gate = gate_ref[rows, :]
            a = gate * jax.nn.sigmoid(gate) * a
        x1_parts.append(x_ref[rows, :] + jnp.dot(a, w_ref[...], preferred_element_type=F32))
    for part in range(ROUTER_SPLIT):
        rows = slice(part * tm, (part + 1) * tm)
        _route_rows(x1_parts[part], rows, nf_ref, wr_ref, br_ref, x1_ref, hn_ref, route_ref, route_t_ref,
                    cnt_ref, carry_ref, n_groups=n_groups, n_experts=n_experts, epg=epg)


def _route_rows(x1, rows, nf_ref, wr_ref, br_ref, x1_ref, hn_ref, route_ref, route_t_ref, cnt_ref,
                carry_ref, *, n_groups, n_experts, epg):
    tm = x1.shape[0]
    x1_ref[rows, :] = x1
    hn = x1 * lax.rsqrt(jnp.mean(x1 * x1, axis=-1, keepdims=True) + RMS_EPS) * nf_ref[...]
    hn_ref[rows, :] = _pack_bf16_pairs(hn)

    hn_hi = hn.astype(BF16)
    hn_lo = (hn - hn_hi.astype(F32)).astype(BF16)
    both = jnp.dot(hn_hi, wr_ref[...], preferred_element_type=F32)
    logits = (both[:, :LANES] + both[:, LANES:]
              + jnp.dot(hn_lo, wr_ref[:, :LANES], preferred_element_type=F32)) + br_ref[...]

    n_rows = -(-(n_groups + n_experts) // SUBLANES) * SUBLANES
    lt = logits.T[:n_rows, :]
    row = lax.broadcasted_iota(jnp.int32, lt.shape, 0)
    big = jnp.int32(LANES)
    neg = jnp.float32(-jnp.inf)

    gl = jnp.where(row < n_groups, lt, neg)
    gmax = jnp.max(gl, axis=0, keepdims=True)
    g_p = 1.0 / jnp.sum(jnp.exp(gl - gmax), axis=0, keepdims=True)
    g_idx = jnp.min(jnp.where(gl == gmax, row, big), axis=0, keepdims=True)

    lo = n_groups + g_idx * epg
    el = jnp.where((row >= lo) & (row < lo + epg), lt, neg)
    v1 = jnp.max(el, axis=0, keepdims=True)
    i1 = jnp.min(jnp.where(el == v1, row, big), axis=0, keepdims=True)
    el2 = jnp.where(row == i1, neg, el)
    v2 = jnp.max(el2, axis=0, keepdims=True)
    i2 = jnp.min(jnp.where(el2 == v2, row, big), axis=0, keepdims=True)
    tt = jnp.exp(v2 - v1)
    w1 = g_p / (1.0 + tt)
    w2 = g_p * tt / (1.0 + tt)
    e1 = i1 - n_groups
    e2 = i2 - n_groups

    erow = lax.broadcasted_iota(jnp.int32, (n_experts, tm), 0)
    oh1 = erow == e1
    oh2 = erow == e2
    onehot = jnp.where(oh1 | oh2, 1.0, 0.0)
    before = lax.broadcasted_iota(jnp.int32, (tm, tm), 0) < lax.broadcasted_iota(jnp.int32, (tm, tm), 1)
    tri = jnp.where(before, 1.0, 0.0).astype(BF16)
    carry = carry_ref[...]
    base = jnp.dot(onehot.astype(BF16), tri, preferred_element_type=F32) + carry[:, 0:1]
    rank1 = jnp.sum(jnp.where(oh1, base, 0.0), axis=0, keepdims=True)
    rank2 = jnp.sum(jnp.where(oh2, base, 0.0), axis=0, keepdims=True)
    carry = carry + jnp.sum(onehot, axis=1, keepdims=True)
    carry_ref[...] = carry
    cnt_ref[...] = carry

    frow = lax.broadcasted_iota(jnp.int32, (LANES, tm), 0)
    out = jnp.zeros((LANES, tm), F32)
    for idx, val in enumerate((e1.astype(F32), e2.astype(F32), w1, w2, rank1, rank2)):
        out = jnp.where(frow == idx, val, out)
    route_t_ref[:, rows] = out[:route_t_ref.shape[0], :]
    route_ref[rows, :] = out.T


def _out_router(a, gate, w, x, nf, w_group, b_group, w_expert, b_expert):
    t, d = x.shape
    k = w.shape[0]
    n_groups = w_group.shape[1]
    n_experts = w_expert.shape[1]
    assert n_groups + n_experts <= LANES
    pad = LANES - n_groups - n_experts
    wr = jnp.pad(jnp.concatenate([w_group, w_expert], axis=1).astype(F32), ((0, 0), (0, pad)))
    wr_hi = wr.astype(BF16)
    wr = jnp.concatenate([wr_hi, (wr - wr_hi.astype(F32)).astype(BF16)], axis=1)
    br = jnp.pad(jnp.concatenate([b_group, b_expert]).astype(F32), (0, pad)).reshape(1, LANES)
    gated = gate is not None
    row = lambda i: (i, 0)
    fixed = lambda i: (0, 0)
    in_specs = [pl.BlockSpec((TM, k), row)]
    args = [a]
    if gated:
        gate_arr, gate_col = gate
        in_specs.append(pl.BlockSpec((TM, k), lambda i: (i, gate_col)))
        args.append(gate_arr)
    in_specs += [
        pl.BlockSpec((k, d), fixed),
        pl.BlockSpec((TM, d), row),
        pl.BlockSpec((1, d), fixed),
        pl.BlockSpec((d, 2 * LANES), fixed),
        pl.BlockSpec((1, LANES), fixed),
    ]
    args += [w, x, nf.reshape(1, d), wr, br]
    return pl.pallas_call(
        functools.partial(_out_router_kernel, gated=gated, n_groups=n_groups, n_experts=n_experts),
        grid=(t // TM,),
        in_specs=in_specs,
        out_specs=[
            pl.BlockSpec((TM, d), row),
            pl.BlockSpec((TM, d // 2), row),
            pl.BlockSpec((TM, LANES), row),
            pl.BlockSpec((SUBLANES, TM), lambda i: (0, i)),
            pl.BlockSpec((n_experts, LANES), fixed),
        ],
        out_shape=[
            jax.ShapeDtypeStruct((t, d), F32),
            jax.ShapeDtypeStruct((t, d // 2), jnp.uint32),
            jax.ShapeDtypeStruct((t, LANES), F32),
            jax.ShapeDtypeStruct((SUBLANES, t), F32),
            jax.ShapeDtypeStruct((n_experts, LANES), F32),
        ],
        scratch_shapes=[pltpu.VMEM((n_experts, LANES), F32)],
        compiler_params=_cparams(("arbitrary",)),
    )(*args)


def _sc_mesh():
    return plsc.VectorSubcoreMesh(core_axis_name="c", subcore_axis_name="s",
                                  num_cores=SC_CORES, num_subcores=SC_SUBCORES)


def _sc_worker():
    return lax.axis_index("s") * SC_CORES + lax.axis_index("c")


def _sc_scatter_rows(src, d1, d2, n_rows):
    t, d = src.shape
    workers = SC_CORES * SC_SUBCORES
    per_w = t // workers
    n_ch = per_w // SC_CHUNK
    assert t % (workers * SC_CHUNK) == 0
    idx = jnp.stack([d1, d2]).reshape(2, workers, n_ch, SC_CHUNK)

    @functools.partial(
        pl.kernel, mesh=_sc_mesh(),
        out_type=jax.ShapeDtypeStruct((n_rows, d), src.dtype),
        scratch_types=[pltpu.VMEM((2, n_ch, SC_CHUNK), jnp.int32),
                       pltpu.VMEM((SC_CHUNK, d), src.dtype),
                       pltpu.VMEM((SC_CHUNK, d), src.dtype),
                       pltpu.SemaphoreType.DMA, pltpu.SemaphoreType.DMA],
    )
    def scatter(src_hbm, idx_hbm, out_hbm, idx_v, rows_a, rows_b, sem_a, sem_b):
        wid = _sc_worker()
        base = wid * per_w
        for s in range(2):
            pltpu.sync_copy(idx_hbm.at[s, wid], idx_v.at[s])

        def move(c, rows_v, sem):
            pltpu.sync_copy(src_hbm.at[pl.ds(base + c * SC_CHUNK, SC_CHUNK)], rows_v)
            return [pltpu.async_copy(rows_v, out_hbm.at[idx_v.at[s, c]], sem) for s in range(2)]

        @pl.loop(0, n_ch, step=2)
        def _(c):
            first = move(c, rows_a, sem_a)
            second = move(c + 1, rows_b, sem_b)
            for cp in first + second:
                cp.wait()

    return scatter(src, idx)


def _sc_gather_rows(table, idx):
    b = idx.shape[0]
    d = table.shape[1]
    workers = SC_CORES * SC_SUBCORES
    per_w = b // workers
    n_ch = per_w // SC_CHUNK
    assert b % (workers * SC_CHUNK * 2) == 0

    @functools.partial(
        pl.kernel, mesh=_sc_mesh(),
        out_type=jax.ShapeDtypeStruct((b, d), table.dtype),
        scratch_types=[pltpu.VMEM((per_w,), jnp.int32),
                       pltpu.VMEM((SC_CHUNK, d), table.dtype),
                       pltpu.VMEM((SC_CHUNK, d), table.dtype),
                       pltpu.SemaphoreType.DMA, pltpu.SemaphoreType.DMA],
    )
    def gather(table_hbm, idx_hbm, out_hbm, idx_v, rows_a, rows_b, sem_a, sem_b):
        base = _sc_worker() * per_w
        pltpu.sync_copy(idx_hbm.at[pl.ds(base, per_w)], idx_v)

        def fetch(c, rows_v, sem):
            return pltpu.async_copy(table_hbm.at[idx_v.at[pl.ds(c * SC_CHUNK, SC_CHUNK)]], rows_v, sem)

        def store(c, rows_v):
            pltpu.sync_copy(rows_v, out_hbm.at[pl.ds(base + c * SC_CHUNK, SC_CHUNK)])

        @pl.loop(0, n_ch, step=2)
        def _(c):
            first = fetch(c, rows_a, sem_a)
            second = fetch(c + 1, rows_b, sem_b)
            first.wait()
            store(c, rows_a)
            second.wait()
            store(c + 1, rows_b)

    return gather(table, idx)


def _expert_kernel(fb_ref, nblk_ref, nv_ref, nu_ref, xs_ref, wg_ref, wu_ref, wd_ref, ys_ref,
                   wgb_ref, wub_ref, wdb_ref, xbuf, ybuf, xsem, ysem):
    e = pl.program_id(0)
    rb = xbuf.shape[1]
    nb = ys_ref.shape[0] // rb
    n_used = nu_ref[0]

    def x_copy(g, slot):
        rows = pl.ds(pl.multiple_of(g * rb, rb), rb)
        return pltpu.make_async_copy(xs_ref.at[rows], xbuf.at[slot], xsem.at[slot])

    def y_copy(g, slot):
        rows = pl.ds(pl.multiple_of(g * rb, rb), rb)
        return pltpu.make_async_copy(ybuf.at[slot], ys_ref.at[rows], ysem.at[slot])

    @pl.when(e == 0)
    def _():
        x_copy(0, 0).start()

    @pl.when(nblk_ref[e] > 0)
    def _():
        wgb_ref[...] = wg_ref[0, 0].astype(BF16)
        wub_ref[...] = wu_ref[0, 0].astype(BF16)
        wdb_ref[...] = wd_ref[0, 0].astype(BF16)

    def block(b, carry):
        g = fb_ref[e] + b
        slot = lax.rem(g, 2)
        x_copy(g, slot).wait()

        @pl.when(g + 1 < n_used)
        def _():
            x_copy(g + 1, 1 - slot).start()

        @pl.when(g >= 2)
        def _():
            y_copy(g - 2, slot).wait()

        nv = nv_ref[g]

        def ffn(rows):
            valid = lax.broadcasted_iota(jnp.int32, (rows, 1), 0) < nv
            x_hi, x_lo = _unpack_bf16_pairs(jnp.where(valid, xbuf[slot, :rows, :], jnp.uint32(0)))
            x = jnp.concatenate([x_hi.astype(BF16), x_lo.astype(BF16)], axis=1)
            slab = wgb_ref.shape[1] // FF_SPLIT
            hmid = []
            for c in range(FF_SPLIT):
                cols = slice(c * slab, (c + 1) * slab)
                gate = jnp.dot(x, wgb_ref[:, cols], preferred_element_type=F32)
                up = jnp.dot(x, wub_ref[:, cols], preferred_element_type=F32)
                hmid.append((gate * jax.nn.sigmoid(gate) * up).astype(BF16))
            hmid = jnp.concatenate(hmid, axis=1)
            ybuf[slot, :rows, :] = _pack_bf16_pairs(jnp.dot(hmid, wdb_ref[...], preferred_element_type=F32))
            if rows < rb:
                ybuf[slot, rows:, :] = jnp.zeros((rb - rows, ybuf.shape[2]), ybuf.dtype)

        @pl.when(nv > rb // 2)
        def _():
            ffn(rb)

        @pl.when(nv <= rb // 2)
        def _():
            ffn(rb // 2)

        y_copy(g, slot).start()
        return carry

    lax.fori_loop(0, nblk_ref[e], block, 0)

    @pl.when(e == pl.num_programs(0) - 1)
    def _():
        @pl.when(n_used >= 2)
        def _():
            y_copy(n_used - 2, lax.rem(n_used, 2)).wait()

        y_copy(n_used - 1, lax.rem(n_used - 1, 2)).wait()
        ybuf[0] = jnp.zeros(ybuf.shape[1:], ybuf.dtype)

        def tail_start(g, carry):
            y_copy(g, 0).start()
            return carry

        def tail_wait(g, carry):
            y_copy(g, 0).wait()
            return carry

        lax.fori_loop(n_used, nb, tail_start, 0)
        lax.fori_loop(n_used, nb, tail_wait, 0)


def _experts(xs, first_block, n_blocks, n_valid, n_used, w_gate, w_up, w_down, layer):
    n_rows, dp = xs.shape
    d = 2 * dp
    n_experts, ff = w_gate.shape[1], w_gate.shape[3]
    wsel = lambda e, *_: (layer, e, 0, 0)
    grid_spec = pltpu.PrefetchScalarGridSpec(
        num_scalar_prefetch=4,
        grid=(n_experts,),
        in_specs=[
            pl.BlockSpec(memory_space=pl.ANY),
            pl.BlockSpec((1, 1, d, ff), wsel),
            pl.BlockSpec((1, 1, d, ff), wsel),
            pl.BlockSpec((1, 1, ff, d), wsel),
        ],
        out_specs=pl.BlockSpec(memory_space=pl.ANY),
        scratch_shapes=[pltpu.VMEM((d, ff), BF16), pltpu.VMEM((d, ff), BF16), pltpu.VMEM((ff, d), BF16),
                        pltpu.VMEM((2, ROW_BLOCK, dp), xs.dtype), pltpu.VMEM((2, ROW_BLOCK, dp), xs.dtype),
                        pltpu.SemaphoreType.DMA((2,)), pltpu.SemaphoreType.DMA((2,))],
    )
    return pl.pallas_call(
        _expert_kernel,
        grid_spec=grid_spec,
        out_shape=jax.ShapeDtypeStruct((n_rows, dp), xs.dtype),
        compiler_params=_cparams(("arbitrary",)),
    )(first_block, n_blocks, n_valid, n_used, xs, w_gate, w_up, w_down)


def _final_combine_kernel(x_ref, route_ref, y1_ref, y2_ref, nf_ref, o_ref):
    route = route_ref[...]
    y1 = jnp.concatenate(_unpack_bf16_pairs(y1_ref[...]), axis=1)
    y2 = jnp.concatenate(_unpack_bf16_pairs(y2_ref[...]), axis=1)
    y = x_ref[...] + (route[:, 2:3] * y1 + route[:, 3:4] * y2)
    o_ref[...] = y * lax.rsqrt(jnp.mean(y * y, axis=-1, keepdims=True) + RMS_EPS) * nf_ref[...]


def _final_combine(x1, route, yg, nf):
    t, d = x1.shape
    slot2 = t // TM
    row = lambda i: (i, 0)
    return pl.pallas_call(
        _final_combine_kernel,
        grid=(t // TM,),
        in_specs=[
            pl.BlockSpec((TM, d), row),
            pl.BlockSpec((TM, LANES), row),
            pl.BlockSpec((TM, d // 2), row),
            pl.BlockSpec((TM, d // 2), lambda i: (slot2 + i, 0)),
            pl.BlockSpec((1, d), lambda i: (0, 0)),
        ],
        out_specs=pl.BlockSpec((TM, d), row),
        out_shape=jax.ShapeDtypeStruct((t, d), F32),
        compiler_params=_cparams(("parallel",)),
    )(x1, route, yg, yg, nf.reshape(1, d).astype(F32))


def _moe(hn, route_t, cnt, w_gate, w_up, w_down, layer):
    t, d = hn.shape
    n_experts = w_gate.shape[1]
    ids = jnp.arange(n_experts, dtype=jnp.int32)
    e1, e2, rank1, rank2 = (route_t[f].astype(jnp.int32) for f in (0, 1, 4, 5))
    counts = cnt[:, 0].astype(jnp.int32)
    padded = ((counts + ROW_BLOCK - 1) // ROW_BLOCK) * ROW_BLOCK
    pad_end = jnp.cumsum(padded)
    pad_start = pad_end - padded

    def dest(e, rank):
        return jnp.sum(jnp.where(e[:, None] == ids[None, :], pad_start[None, :], 0), axis=1) + rank

    d1 = dest(e1, rank1)
    d2 = dest(e2, rank2)
    nb = (t * TOP_K_INNER) // ROW_BLOCK + n_experts
    n_used = (pad_end[-1:] // ROW_BLOCK).astype(jnp.int32)
    first_row = jnp.arange(nb, dtype=jnp.int32) * ROW_BLOCK
    block_e = jnp.sum((pad_end[None, :] <= first_row[:, None]).astype(jnp.int32), axis=1)
    block_e = jnp.minimum(block_e, n_experts - 1)
    row_end = jnp.sum(jnp.where(block_e[:, None] == ids[None, :], (pad_start + counts)[None, :], 0), axis=1)
    n_valid = jnp.clip(row_end - first_row, 0, ROW_BLOCK).astype(jnp.int32)
    xs = _sc_scatter_rows(hn, d1, d2, nb * ROW_BLOCK)
    ys = _experts(xs, (pad_start // ROW_BLOCK).astype(jnp.int32), (padded // ROW_BLOCK).astype(jnp.int32),
                  n_valid, n_used, w_gate, w_up, w_down, layer)
    return _sc_gather_rows(ys, jnp.concatenate([d1, d2]))


def _diff_lambda_init(layer_idx):
    return 0.8 - 0.6 * math.exp(-0.3 * layer_idx)


def _rope_tables(seq, dk):
    angle = 1.0 / (10000.0 ** jnp.linspace(0.0, 1.0, dk // 2, dtype=F32))
    angle = jnp.repeat(angle, 2)
    phase = jnp.arange(seq, dtype=F32)[:, None] * angle[None, :]
    sin, cos = jnp.sin(phase), jnp.cos(phase)
    even = (jnp.arange(dk) % 2 == 0)[None, :]
    return cos, jnp.where(even, -sin, 0.0), jnp.where(even, 0.0, sin)


def kernel(x, rel_bias_table, norm_mix, norm_ffn, norm_final, diff_w_in, diff_lambda_qk, diff_subln,
           diff_w_out, ret_w_in, ret_w_out, moe_w_group, moe_b_group, moe_w_expert, moe_b_expert,
           moe_w_gate, moe_w_up, moe_w_down):
    batch, seq, d = x.shape
    depth = norm_mix.shape[0]
    t = batch * seq
    heads = rel_bias_table.shape[1]
    dh = diff_lambda_qk.shape[-1]
    qk_dim = (ret_w_in.shape[2] - 2 * ret_w_out.shape[1]) // 2
    v_dim = ret_w_out.shape[1]
    dk = qk_dim // RET_HEADS
    xt = x.reshape(t, d)
    moe = None
    for i in range(depth):
        j = i // N_MIXERS
        if i % N_MIXERS == 0:
            col_scale = jnp.concatenate([jnp.full((d,), dh ** -0.5 * LOG2E, F32), jnp.ones((2 * d,), F32)])
            qkv = _norm_matmul(xt, norm_mix[i], diff_w_in, j, col_scale, moe=moe)
            if moe is not None:
                qkv, xt = qkv
            mix = _diff_attention(qkv, rel_bias_table, diff_lambda_qk[j], diff_subln[j],
                                  _diff_lambda_init(i), batch, seq, heads)
            gate = None
            w_out = diff_w_out[j].astype(BF16)
        else:
            n_in = ret_w_in.shape[2]
            col_scale = jnp.concatenate([jnp.ones((qk_dim,), F32), jnp.full((qk_dim,), dk ** -0.5, F32),
                                         jnp.ones((n_in - 2 * qk_dim,), F32)])
            qkvg = _norm_matmul(xt, norm_mix[i], ret_w_in, j, col_scale, rope=_rope_tables(seq, dk),
                                n_rope=2 * qk_dim // TN, seq=seq, moe=moe)
            if moe is not None:
                qkvg, xt = qkvg
            mix = _retention(qkvg, batch, seq, qk_dim, v_dim)
            gate = (qkvg, (2 * qk_dim + v_dim) // v_dim)
            w_out = ret_w_out[j].astype(BF16)
        x1, hn, route, route_t, cnt = _out_router(mix, gate, w_out, xt, norm_ffn[i], moe_w_group[i],
                                                  moe_b_group[i], moe_w_expert[i], moe_b_expert[i])
        xt = x1
        moe = (route, _moe(hn, route_t, cnt, moe_w_gate, moe_w_up, moe_w_down, i))
    return _final_combine(xt, moe[0], moe[1], norm_final).reshape(batch, seq, d)
```
